```python
import math
import jax, jax.numpy as jnp
from jax import lax
import numpy as np

D_MODEL = 2048
BATCH = 8
SEQ = 2048
DEPTH = 2

GRID_W = 64
CTX_LEN = 256

NA_HEADS = 8
NA_HEAD_DIM = 128
NA_WIN_ROWS = 8
NA_WIN_COLS = 16

RET_HEADS = 8
RET_KEY_DIM = 128
RET_VAL_DIM = 256
RET_CHUNK = 128

ROPE_BASE = 10000.0
ROPE_FREQS_PER_AXIS = RET_KEY_DIM // 4
NORM_EPS = 1e-6
MASK_VALUE = -1e30

W_NA = NA_HEADS * NA_HEAD_DIM
W_RET_QK = RET_HEADS * RET_KEY_DIM
W_RET_V = RET_HEADS * RET_VAL_DIM

NA_Q, NA_K, NA_V, NA_Z, RET_Q, RET_K, RET_V, RET_Z, G_NA, G_RET = range(10)
SPLIT_SIZES = (W_NA, W_NA, W_NA, W_NA, W_RET_QK, W_RET_QK, W_RET_V, W_RET_V, D_MODEL, D_MODEL)
SPLIT_OFFSETS = tuple(int(o) for o in np.cumsum((0,) + SPLIT_SIZES))
N_SPLITS = len(SPLIT_SIZES)
IN_COLS = SPLIT_OFFSETS[-1]

kernel_name = 'hybrid_na_retention_dit'


def rmsnorm(x, g):
    xf = x.astype(jnp.float32)
    y = xf * lax.rsqrt(jnp.mean(xf * xf, axis=-1, keepdims=True) + NORM_EPS)
    return (y * g.astype(jnp.float32)).astype(x.dtype)


def to_heads(t, n_heads):
    b, n, w = t.shape
    return t.reshape(b, n, n_heads, w // n_heads).transpose(0, 2, 1, 3)


def from_heads(t):
    b, h, n, d = t.shape
    return t.transpose(0, 2, 1, 3).reshape(b, n, h * d)


def in_block(t, i):
    return t[..., SPLIT_OFFSETS[i]:SPLIT_OFFSETS[i + 1]]


def axial_rope(n_tokens, dtype):
    t = jnp.arange(n_tokens)
    row = (t // GRID_W).astype(jnp.float32)
    col = (t % GRID_W).astype(jnp.float32)
    inv_freq = ROPE_BASE ** (-jnp.arange(ROPE_FREQS_PER_AXIS, dtype=jnp.float32) / ROPE_FREQS_PER_AXIS)
    ang = jnp.concatenate([row[:, None] * inv_freq, col[:, None] * inv_freq], axis=-1)
    return jnp.cos(ang).astype(dtype), jnp.sin(ang).astype(dtype)


def apply_rope(x, cos, sin):
    half = x.shape[-1] // 2
    x1, x2 = x[..., :half], x[..., half:]
    return jnp.concatenate([x1 * cos - x2 * sin, x2 * cos + x1 * sin], axis=-1)


def neighbourhood_attention(q, k, v, k_ctx, v_ctx, rpb):
    b, h, s, d = q.shape
    rows = s // GRID_W
    kh = min(NA_WIN_ROWS, rows)
    kw = NA_WIN_COLS
    r = jnp.arange(rows)
    cidx = jnp.arange(GRID_W)
    r0 = jnp.clip(r - kh // 2, 0, rows - kh)
    row_idx = r0[:, None] + jnp.arange(kh)[None, :]
    c0 = jnp.clip(cidx - kw // 2, 0, GRID_W - kw)
    col_in = (cidx[None, :] >= c0[:, None]) & (cidx[None, :] < c0[:, None] + kw)
    qg = q.reshape(b, h, rows, GRID_W, d)
    kg = jnp.take(k.reshape(b, h, rows, GRID_W, d), row_idx, axis=2)
    vg = jnp.take(v.reshape(b, h, rows, GRID_W, d), row_idx, axis=2)
    scale = d ** -0.5
    s_loc = jnp.einsum('bhrcd,bhrkwd->bhrckw', qg, kg).astype(jnp.float32) * scale
    dr = row_idx - r[:, None] + (NA_WIN_ROWS - 1)
    dc = jnp.clip(cidx[None, :] - cidx[:, None] + (NA_WIN_COLS - 1), 0, 2 * NA_WIN_COLS - 2)
    bias = rpb[:, dr[:, None, :, None], dc[None, :, None, :]].astype(jnp.float32)
    s_loc = jnp.where(col_in[None, None, None, :, None, :], s_loc + bias[None], MASK_VALUE)
    s_ctx = jnp.einsum('bhrcd,bhld->bhrcl', qg, k_ctx).astype(jnp.float32) * scale
    n_loc = kh * GRID_W
    scores = jnp.concatenate([s_loc.reshape(b, h, rows, GRID_W, n_loc), s_ctx], axis=-1)
    p = jax.nn.softmax(scores, axis=-1).astype(v.dtype)
    p_loc = p[..., :n_loc].reshape(b, h, rows, GRID_W, kh, GRID_W)
    p_ctx = p[..., n_loc:]
    out = (jnp.einsum('bhrckw,bhrkwd->bhrcd', p_loc, vg)
           + jnp.einsum('bhrcl,bhld->bhrcd', p_ctx, v_ctx))
    return out.reshape(b, h, s, d)


def context_attention(q, k, v):
    s = jnp.einsum('bhqd,bhkd->bhqk', q, k).astype(jnp.float32) * (q.shape[-1] ** -0.5)
    p = jax.nn.softmax(s, axis=-1).astype(v.dtype)
    return jnp.einsum('bhqk,bhkd->bhqd', p, v)


def retention_chunkwise(q, k, v, log_gamma, state0):
    b, h, n, dk = q.shape
    dv = v.shape[-1]
    cs = RET_CHUNK
    nc = n // cs
    pos = jnp.arange(cs, dtype=jnp.float32)
    diff = pos[:, None] - pos[None, :]
    intra = jnp.where(diff >= 0, jnp.exp(jnp.maximum(diff, 0.0) * log_gamma[:, None, None]), 0.0)
    q_dec = jnp.exp((pos + 1.0)[None, :] * log_gamma[:, None])
    k_dec = jnp.exp((cs - 1.0 - pos)[None, :] * log_gamma[:, None])
    chunk_dec = jnp.exp(cs * log_gamma)
    qc = q.reshape(b, h, nc, cs, dk)
    kc = k.reshape(b, h, nc, cs, dk)
    vc = v.reshape(b, h, nc, cs, dv)
    scores = jnp.einsum('bhnid,bhnjd->bhnij', qc, kc) * intra[None, :, None]
    inner = jnp.einsum('bhnij,bhnje->bhnie', scores, vc)

    def step(state, xs):
        q_i, k_i, v_i = xs
        cross = jnp.einsum('bhid,bhde->bhie', q_i * q_dec[None, :, :, None], state)
        state = (state * chunk_dec[None, :, None, None]
                 + jnp.einsum('bhjd,bhje->bhde', k_i * k_dec[None, :, :, None], v_i))
        return state, cross

    xs = (jnp.moveaxis(qc, 2, 0), jnp.moveaxis(kc, 2, 0), jnp.moveaxis(vc, 2, 0))
    state_final, cross = lax.scan(step, state0, xs)
    out = inner + jnp.moveaxis(cross, 0, 2)
    return out.reshape(b, h, n, dv), state_final


def bidir_retention(q, k, v, log_gamma, state_fwd, state_bwd):
    q, k, v = q.astype(jnp.float32), k.astype(jnp.float32), v.astype(jnp.float32)
    o_f, s_f = retention_chunkwise(q, k, v, log_gamma[0], state_fwd)
    o_b, s_b = retention_chunkwise(jnp.flip(q, 2), jnp.flip(k, 2), jnp.flip(v, 2), log_gamma[1], state_bwd)
    return o_f + jnp.flip(o_b, 2), s_f, s_b


def context_final_states(k, v, log_gamma):
    k, v = k.astype(jnp.float32), v.astype(jnp.float32)
    n = k.shape[2]
    pos = jnp.arange(n, dtype=jnp.float32)
    w_f = jnp.exp((n - 1.0 - pos)[None, :] * log_gamma[0][:, None])
    w_b = jnp.exp(pos[None, :] * log_gamma[1][:, None])
    s_f = jnp.einsum('bhld,bhle->bhde', k * w_f[None, :, :, None], v)
    s_b = jnp.einsum('bhld,bhle->bhde', k * w_b[None, :, :, None], v)
    return s_f, s_b


def ret_head_norm(o, dtype):
    o = o * lax.rsqrt(jnp.mean(o * o, axis=-1, keepdims=True) + NORM_EPS)
    return o.astype(dtype)


def merge_branches(o_na, o_ret, blocks, w_proj_na, w_proj_ret, w_out):
    dtype = blocks[NA_Z].dtype
    y_na = (from_heads(o_na) * jax.nn.silu(blocks[NA_Z])) @ w_proj_na
    y_ret = (from_heads(ret_head_norm(o_ret, dtype)) * jax.nn.silu(blocks[RET_Z])) @ w_proj_ret
    merged = jax.nn.sigmoid(blocks[G_NA]) * y_na + jax.nn.sigmoid(blocks[G_RET]) * y_ret
    return merged @ w_out


def hybrid_layer(x_lat, x_ctx, mod_lat, mod_ctx, norm_g, w_in, rpb, decay_logit,
                 w_proj_na, w_proj_ret, w_out, update_ctx):
    shift, scale, gate = jnp.split(mod_lat, 3, axis=-1)
    c_shift, c_scale, c_gate = jnp.split(mod_ctx, 3, axis=-1)
    h_lat = rmsnorm(x_lat, norm_g) * (1.0 + scale[:, None]) + shift[:, None]
    h_ctx = rmsnorm(x_ctx, norm_g) * (1.0 + c_scale) + c_shift
    log_gamma = jax.nn.log_sigmoid(decay_logit.astype(jnp.float32))

    u = h_lat @ w_in
    lat = {i: in_block(u, i) for i in range(N_SPLITS)}
    if update_ctx:
        uc = h_ctx @ w_in
        cb = {i: in_block(uc, i) for i in range(N_SPLITS)}
    else:
        cb = {i: h_ctx @ in_block(w_in, i) for i in (NA_K, NA_V, RET_K, RET_V)}

    k_na_ctx = to_heads(cb[NA_K], NA_HEADS)
    v_na_ctx = to_heads(cb[NA_V], NA_HEADS)
    o_na = neighbourhood_attention(to_heads(lat[NA_Q], NA_HEADS), to_heads(lat[NA_K], NA_HEADS),
                                   to_heads(lat[NA_V], NA_HEADS), k_na_ctx, v_na_ctx, rpb)

    k_scale = RET_KEY_DIM ** -0.5
    cos, sin = axial_rope(x_lat.shape[1], x_lat.dtype)
    q_ret = apply_rope(to_heads(lat[RET_Q], RET_HEADS), cos, sin)
    k_ret = apply_rope(to_heads(lat[RET_K], RET_HEADS), cos, sin) * k_scale
    v_ret = to_heads(lat[RET_V], RET_HEADS)
    k_ret_ctx = to_heads(cb[RET_K], RET_HEADS) * k_scale
    v_ret_ctx = to_heads(cb[RET_V], RET_HEADS)
    if update_ctx:
        b = x_ctx.shape[0]
        zeros = jnp.zeros((b, RET_HEADS, RET_KEY_DIM, RET_VAL_DIM), jnp.float32)
        o_ret_ctx, s_f, s_b = bidir_retention(to_heads(cb[RET_Q], RET_HEADS), k_ret_ctx, v_ret_ctx,
                                              log_gamma, zeros, zeros)
    else:
        s_f, s_b = context_final_states(k_ret_ctx, v_ret_ctx, log_gamma)
    o_ret, _, _ = bidir_retention(q_ret, k_ret, v_ret, log_gamma, s_f, s_b)

    out_lat = merge_branches(o_na, o_ret, lat, w_proj_na, w_proj_ret, w_out)
    x_lat = x_lat + gate[:, None] * out_lat
    if update_ctx:
        o_na_ctx = context_attention(to_heads(cb[NA_Q], NA_HEADS), k_na_ctx, v_na_ctx)
        out_ctx = merge_branches(o_na_ctx, o_ret_ctx, cb, w_proj_na, w_proj_ret, w_out)
        x_ctx = x_ctx + c_gate * out_ctx
    return x_lat, x_ctx


def _fwd_setup_inputs(seed: int = 0) -> dict:
    key = jax.random.key(seed)
    ks = jax.random.split(key, 14)
    f32 = jnp.float32
    base_logit = jnp.log(2.0 ** (5.0 + jnp.arange(RET_HEADS, dtype=f32)) - 1.0)
    return {
        'x': jax.random.normal(ks[0], (BATCH, SEQ, D_MODEL), f32),
        'c': jax.random.normal(ks[1], (BATCH, D_MODEL), f32),
        'ctx': jax.random.normal(ks[2], (BATCH, CTX_LEN, D_MODEL), f32),
        'c_ctx': jax.random.normal(ks[3], (D_MODEL,), f32),
        'ada_w': jax.random.normal(ks[4], (DEPTH, D_MODEL, 3 * D_MODEL), f32) * D_MODEL ** -0.5,
        'ada_b': jax.random.normal(ks[5], (DEPTH, 3 * D_MODEL), f32) * 0.01,
        'norm_g': 1.0 + 0.1 * jax.random.normal(ks[6], (DEPTH, D_MODEL), f32),
        'w_in': jax.random.normal(ks[7], (DEPTH, D_MODEL, IN_COLS), f32) * D_MODEL ** -0.5,
        'na_rpb': 0.1 * jax.random.normal(ks[8], (DEPTH, NA_HEADS, 2 * NA_WIN_ROWS - 1, 2 * NA_WIN_COLS - 1), f32),
        'ret_decay_logit': base_logit[None, None, :] + 0.1 * jax.random.normal(ks[9], (DEPTH, 2, RET_HEADS), f32),
        'w_proj_na': jax.random.normal(ks[10], (DEPTH, W_NA, D_MODEL), f32) * W_NA ** -0.5,
        'w_proj_ret': jax.random.normal(ks[11], (DEPTH, W_RET_V, D_MODEL), f32) * W_RET_V ** -0.5,
        'w_out': jax.random.normal(ks[12], (DEPTH, D_MODEL, D_MODEL), f32) * D_MODEL ** -0.5,
        'final_g': 1.0 + 0.1 * jax.random.normal(ks[13], (D_MODEL,), f32),
    }


def _fwd_reference(x, c, ctx, c_ctx, ada_w, ada_b, norm_g, w_in, na_rpb, ret_decay_logit,
              w_proj_na, w_proj_ret, w_out, final_g):
    c_silu = jax.nn.silu(c)
    cc_silu = jax.nn.silu(c_ctx)
    x_lat, x_ctx = x, ctx
    for l in range(DEPTH):
        mod_lat = c_silu @ ada_w[l] + ada_b[l]
        mod_ctx = cc_silu @ ada_w[l] + ada_b[l]
        x_lat, x_ctx = hybrid_layer(x_lat, x_ctx, mod_lat, mod_ctx, norm_g[l], w_in[l], na_rpb[l],
                                    ret_decay_logit[l], w_proj_na[l], w_proj_ret[l], w_out[l],
                                    update_ctx=(l < DEPTH - 1))
    return rmsnorm(x_lat, final_g)


import jax as _jax
import jax.numpy as _jnp

TWIN_FORMAT = 'train_step'
FWD_PARAMS = ['x', 'c', 'ctx', 'c_ctx', 'ada_w', 'ada_b', 'norm_g', 'w_in', 'na_rpb', 'ret_decay_logit', 'w_proj_na', 'w_proj_ret', 'w_out', 'final_g']
TWIN_WEIGHTS = ['c_ctx', 'ada_w', 'ada_b', 'norm_g', 'w_in', 'na_rpb', 'ret_decay_logit', 'w_proj_na', 'w_proj_ret', 'w_out', 'final_g']
TWIN_DIFF_INPUT = 'x'
TWIN_INPUTS = ['x', 'c', 'ctx', 'c_ctx', 'ada_w', 'ada_b', 'norm_g', 'w_in', 'na_rpb', 'ret_decay_logit', 'w_proj_na', 'w_proj_ret', 'w_out', 'final_g', 'loss_target', 'm_c_ctx', 'm_ada_w', 'm_ada_b', 'm_norm_g', 'm_w_in', 'm_na_rpb', 'm_ret_decay_logit', 'm_w_proj_na', 'm_w_proj_ret', 'm_w_out', 'm_final_g', 'v_c_ctx', 'v_ada_w', 'v_ada_b', 'v_norm_g', 'v_w_in', 'v_na_rpb', 'v_ret_decay_logit', 'v_w_proj_na', 'v_w_proj_ret', 'v_w_out', 'v_final_g']
TWIN_OUTPUTS = ['loss', 'grad_x', 'grad_c_ctx', 'grad_ada_w', 'grad_ada_b', 'grad_norm_g', 'grad_w_in', 'grad_na_rpb', 'grad_ret_decay_logit', 'grad_w_proj_na', 'grad_w_proj_ret', 'grad_w_out', 'grad_final_g', 'delta_c_ctx', 'delta_ada_w', 'delta_ada_b', 'delta_norm_g', 'delta_w_in', 'delta_na_rpb', 'delta_ret_decay_logit', 'delta_w_proj_na', 'delta_w_proj_ret', 'delta_w_out', 'delta_final_g', 'new_m_c_ctx', 'new_m_ada_w', 'new_m_ada_b', 'new_m_norm_g', 'new_m_w_in', 'new_m_na_rpb', 'new_m_ret_decay_logit', 'new_m_w_proj_na', 'new_m_w_proj_ret', 'new_m_w_out', 'new_m_final_g', 'new_v_c_ctx', 'new_v_ada_w', 'new_v_ada_b', 'new_v_norm_g', 'new_v_w_in', 'new_v_na_rpb', 'new_v_ret_decay_logit', 'new_v_w_proj_na', 'new_v_w_proj_ret', 'new_v_w_out', 'new_v_final_g']
TWIN_LEAF_KINDS = {'loss': 'loss', 'grad_x': 'grad_x', 'grad_c_ctx': 'grad_w', 'grad_ada_w': 'grad_w', 'grad_ada_b': 'grad_w', 'grad_norm_g': 'grad_w', 'grad_w_in': 'grad_w', 'grad_na_rpb': 'grad_w', 'grad_ret_decay_logit': 'grad_w', 'grad_w_proj_na': 'grad_w', 'grad_w_proj_ret': 'grad_w', 'grad_w_out': 'grad_w', 'grad_final_g': 'grad_w', 'delta_c_ctx': 'delta_w', 'delta_ada_w': 'delta_w', 'delta_ada_b': 'delta_w', 'delta_norm_g': 'delta_w', 'delta_w_in': 'delta_w', 'delta_na_rpb': 'delta_w', 'delta_ret_decay_logit': 'delta_w', 'delta_w_proj_na': 'delta_w', 'delta_w_proj_ret': 'delta_w', 'delta_w_out': 'delta_w', 'delta_final_g': 'delta_w', 'new_m_c_ctx': 'new_m', 'new_m_ada_w': 'new_m', 'new_m_ada_b': 'new_m', 'new_m_norm_g': 'new_m', 'new_m_w_in': 'new_m', 'new_m_na_rpb': 'new_m', 'new_m_ret_decay_logit': 'new_m', 'new_m_w_proj_na': 'new_m', 'new_m_w_proj_ret': 'new_m', 'new_m_w_out': 'new_m', 'new_m_final_g': 'new_m', 'new_v_c_ctx': 'new_v', 'new_v_ada_w': 'new_v', 'new_v_ada_b': 'new_v', 'new_v_norm_g': 'new_v', 'new_v_w_in': 'new_v', 'new_v_na_rpb': 'new_v', 'new_v_ret_decay_logit': 'new_v', 'new_v_w_proj_na': 'new_v', 'new_v_w_proj_ret': 'new_v', 'new_v_w_out': 'new_v', 'new_v_final_g': 'new_v'}


def _forward(args):
    return _fwd_reference(*[args[k] for k in FWD_PARAMS])


def _output_shape():
    out = _jax.eval_shape(lambda: _forward(_fwd_setup_inputs(0)))
    return out.shape, out.dtype

N_MICROBATCH = 1
ADAM_LR = 0.001
ADAM_B1 = 0.9
ADAM_B2 = 0.999
ADAM_EPS = 1e-08
ADAM_WD = 0.01
ADAM_STEP = 10
PER_EXAMPLE_BATCH_AXIS = {'x': 0, 'c': 0, 'ctx': 0, 'loss_target': 0}
SHARED_INPUTS = []
_WEIGHT_DTYPES = {'c_ctx': _jnp.float32, 'ada_w': _jnp.float32, 'ada_b': _jnp.float32, 'norm_g': _jnp.float32, 'w_in': _jnp.float32, 'na_rpb': _jnp.float32, 'ret_decay_logit': _jnp.float32, 'w_proj_na': _jnp.float32, 'w_proj_ret': _jnp.float32, 'w_out': _jnp.float32, 'final_g': _jnp.float32}
MOMENT_SCALE = {'c_ctx': 3.205995e-02, 'ada_w': 4.174787e-02, 'ada_b': 7.590042e-02, 'norm_g': 3.636762e-02, 'w_in': 2.082820e-02, 'na_rpb': 1.529985e-03, 'ret_decay_logit': 6.606826e-02, 'w_proj_na': 1.059721e-02, 'w_proj_ret': 1.871533e-02, 'w_out': 2.169768e-02, 'final_g': 8.088962e+00}


def _to_microbatches(a, axis):
    t = _jnp.moveaxis(a, axis, 0)
    t = t.reshape((N_MICROBATCH, t.shape[0] // N_MICROBATCH) + t.shape[1:])
    return _jnp.moveaxis(t, 1, axis + 1)


def setup_inputs(seed: int = 0) -> dict:
    inp = _fwd_setup_inputs(seed)
    key = _jax.random.fold_in(_jax.random.key(seed), 7919)
    shape, _ = _output_shape()
    out = dict(inp)
    out["loss_target"] = _jax.random.normal(_jax.random.fold_in(key, 0), shape, _jnp.float32)
    for i, name in enumerate(TWIN_WEIGHTS):
        w = inp[name].astype(_jnp.float32)
        if MOMENT_SCALE is None:
            s = _jnp.sqrt(_jnp.mean(_jnp.square(w)) + 1e-30)
        else:
            s = MOMENT_SCALE[name]
        km, kv = _jax.random.split(_jax.random.fold_in(key, i + 1))
        out[name] = w
        out["m_" + name] = s * _jax.random.normal(km, w.shape, _jnp.float32)
        out["v_" + name] = (s * s) * _jax.random.uniform(kv, w.shape, _jnp.float32, 0.5, 1.5)
    if N_MICROBATCH > 1:
        for name, axis in PER_EXAMPLE_BATCH_AXIS.items():
            out[name] = _to_microbatches(out[name], axis)
    return {'x': out['x'], 'c': out['c'], 'ctx': out['ctx'], 'c_ctx': out['c_ctx'], 'ada_w': out['ada_w'], 'ada_b': out['ada_b'], 'norm_g': out['norm_g'], 'w_in': out['w_in'], 'na_rpb': out['na_rpb'], 'ret_decay_logit': out['ret_decay_logit'], 'w_proj_na': out['w_proj_na'], 'w_proj_ret': out['w_proj_ret'], 'w_out': out['w_out'], 'final_g': out['final_g'], 'loss_target': out['loss_target'], 'm_c_ctx': out['m_c_ctx'], 'm_ada_w': out['m_ada_w'], 'm_ada_b': out['m_ada_b'], 'm_norm_g': out['m_norm_g'], 'm_w_in': out['m_w_in'], 'm_na_rpb': out['m_na_rpb'], 'm_ret_decay_logit': out['m_ret_decay_logit'], 'm_w_proj_na': out['m_w_proj_na'], 'm_w_proj_ret': out['m_w_proj_ret'], 'm_w_out': out['m_w_out'], 'm_final_g': out['m_final_g'], 'v_c_ctx': out['v_c_ctx'], 'v_ada_w': out['v_ada_w'], 'v_ada_b': out['v_ada_b'], 'v_norm_g': out['v_norm_g'], 'v_w_in': out['v_w_in'], 'v_na_rpb': out['v_na_rpb'], 'v_ret_decay_logit': out['v_ret_decay_logit'], 'v_w_proj_na': out['v_w_proj_na'], 'v_w_proj_ret': out['v_w_proj_ret'], 'v_w_out': out['v_w_out'], 'v_final_g': out['v_final_g']}


def _loss(weights, diff, rest, loss_target):
    with _jax.named_scope("forward"):
        args = {**rest, TWIN_DIFF_INPUT: diff, **{k: w.astype(_WEIGHT_DTYPES[k]) for k, w in weights.items()}}
        y = _forward(args)
    with _jax.named_scope("loss_head"):
        err = _jnp.square(y.astype(_jnp.float32) - loss_target)
        return 0.5 * _jnp.sum(_jnp.mean(err, axis=-1)) if err.ndim else 0.5 * err


def _adamw(w, g, m, v):
    m = ADAM_B1 * m + (1.0 - ADAM_B1) * g
    v = ADAM_B2 * v + (1.0 - ADAM_B2) * _jnp.square(g)
    m_hat = m / (1.0 - ADAM_B1 ** ADAM_STEP)
    v_hat = v / (1.0 - ADAM_B2 ** ADAM_STEP)
    delta = -ADAM_LR * (m_hat / (_jnp.sqrt(v_hat) + ADAM_EPS) + ADAM_WD * w)
    return delta, m, v


def reference(x, c, ctx, c_ctx, ada_w, ada_b, norm_g, w_in, na_rpb, ret_decay_logit, w_proj_na, w_proj_ret, w_out, final_g, loss_target, m_c_ctx, m_ada_w, m_ada_b, m_norm_g, m_w_in, m_na_rpb, m_ret_decay_logit, m_w_proj_na, m_w_proj_ret, m_w_out, m_final_g, v_c_ctx, v_ada_w, v_ada_b, v_norm_g, v_w_in, v_na_rpb, v_ret_decay_logit, v_w_proj_na, v_w_proj_ret, v_w_out, v_final_g):
    given = dict(x=x, c=c, ctx=ctx, c_ctx=c_ctx, ada_w=ada_w, ada_b=ada_b, norm_g=norm_g, w_in=w_in, na_rpb=na_rpb, ret_decay_logit=ret_decay_logit, w_proj_na=w_proj_na, w_proj_ret=w_proj_ret, w_out=w_out, final_g=final_g, loss_target=loss_target, m_c_ctx=m_c_ctx, m_ada_w=m_ada_w, m_ada_b=m_ada_b, m_norm_g=m_norm_g, m_w_in=m_w_in, m_na_rpb=m_na_rpb, m_ret_decay_logit=m_ret_decay_logit, m_w_proj_na=m_w_proj_na, m_w_proj_ret=m_w_proj_ret, m_w_out=m_w_out, m_final_g=m_final_g, v_c_ctx=v_c_ctx, v_ada_w=v_ada_w, v_ada_b=v_ada_b, v_norm_g=v_norm_g, v_w_in=v_w_in, v_na_rpb=v_na_rpb, v_ret_decay_logit=v_ret_decay_logit, v_w_proj_na=v_w_proj_na, v_w_proj_ret=v_w_proj_ret, v_w_out=v_w_out, v_final_g=v_final_g)
    weights = {n: given[n] for n in TWIN_WEIGHTS}
    shared = {n: given[n] for n in SHARED_INPUTS}
    per_example = {n: given[n] for n in ['x', 'c', 'ctx']}
    grad_fn = _jax.value_and_grad(_loss, argnums=(0, 1))

    def one_microbatch(ex, loss_target):
        ex = dict(ex)
        diff = ex.pop(TWIN_DIFF_INPUT)
        return grad_fn(weights, diff, {**shared, **ex}, loss_target)

    if N_MICROBATCH == 1:
        loss, (grad_w, grad_x) = one_microbatch(per_example, given["loss_target"])
    else:
        def body(carry, xs):
            loss_sum, grad_sum = carry
            l_k, (gw_k, gx_k) = one_microbatch(xs[0], xs[1])
            with _jax.named_scope("update"):
                return (loss_sum + l_k, _jax.tree.map(_jnp.add, grad_sum, gw_k)), gx_k

        init = (_jnp.zeros((), _jnp.float32), _jax.tree.map(_jnp.zeros_like, weights))
        (loss, grad_w), grad_x = _jax.lax.scan(body, init, (per_example, given["loss_target"]))
    with _jax.named_scope("update"):
        delta_w, new_m, new_v = {}, {}, {}
        for n in TWIN_WEIGHTS:
            delta_w[n], new_m[n], new_v[n] = _adamw(weights[n], grad_w[n], given["m_" + n], given["v_" + n])
    return (loss, grad_x, *[grad_w[n] for n in TWIN_WEIGHTS], *[delta_w[n] for n in TWIN_WEIGHTS],
            *[new_m[n] for n in TWIN_WEIGHTS], *[new_v[n] for n in TWIN_WEIGHTS])
```

```python
import functools

import numpy as np
import jax
import jax.numpy as jnp
from jax import lax
from jax.experimental import pallas as pl
from jax.experimental.pallas import tpu as pltpu

F32 = jnp.float32
BF16 = jnp.bfloat16

N_DEV = 8
GRID_W = 64
NA_HEAD_DIM = 128
NA_WIN_ROWS = 8
NA_WIN_COLS = 16
RET_KEY_DIM = 128
RET_VAL_DIM = 256
RET_CHUNK = 128
ROPE_BASE = 10000.0
NORM_EPS = 1e-6
MASK_VALUE = -1e30

ADAM_LR = 0.001
ADAM_B1 = 0.9
ADAM_B2 = 0.999
ADAM_EPS = 1e-08
ADAM_WD = 0.01
ADAM_STEP = 10

VMEM_LIMIT = 48 * 1024 * 1024
MESH = pl.DeviceIdType.MESH
ANY = pl.BlockSpec(memory_space=pl.ANY)
VMEM_SPEC = pl.BlockSpec(memory_space=pltpu.VMEM)


def _params(sem=None):
    return pltpu.CompilerParams(dimension_semantics=sem, vmem_limit_bytes=VMEM_LIMIT)


def _pick(n, prefs):
    for p in prefs:
        if n % p == 0:
            return p
    return n


def _dot(a, b):
    return lax.dot_general(a, b, (((1,), (0,)), ((), ())), preferred_element_type=F32)


def _dot_nt(a, b):
    return lax.dot_general(a, b, (((1,), (1,)), ((), ())), preferred_element_type=F32)


def _dot_tn(a, b):
    return lax.dot_general(a, b, (((0,), (0,)), ((), ())), preferred_element_type=F32)


def _silu(x):
    return x * jax.nn.sigmoid(x)


def _matmul(a, b, *, trans_a=False, trans_b=False, out_dtype=F32, name="matmul"):
    if trans_a:
        kdim, m = a.shape
    else:
        m, kdim = a.shape
    if trans_b:
        n, kb = b.shape
    else:
        kb, n = b.shape
    assert kdim == kb, (a.shape, b.shape, trans_a, trans_b)
    tm = _pick(m, (1152, 1024, 768, 512, 256, 128))
    tn = _pick(n, (512, 256, 128))
    tk = _pick(kdim, (2304, 2048, 1024, 512, 256, 128))
    nk = kdim // tk
    dn = (((0 if trans_a else 1,), (1 if trans_b else 0,)), ((), ()))

    def body(a_ref, b_ref, o_ref, acc_ref):
        part = lax.dot_general(a_ref[...], b_ref[...], dn, preferred_element_type=F32)
        if nk == 1:
            o_ref[...] = part.astype(o_ref.dtype)
        else:
            k = pl.program_id(2)

            @pl.when(k == 0)
            def _():
                acc_ref[...] = part

            @pl.when(k > 0)
            def _():
                acc_ref[...] += part

            @pl.when(k == nk - 1)
            def _():
                o_ref[...] = acc_ref[...].astype(o_ref.dtype)

    a_spec = pl.BlockSpec((tk, tm), lambda i, j, k: (k, i)) if trans_a else pl.BlockSpec((tm, tk), lambda i, j, k: (i, k))
    b_spec = pl.BlockSpec((tn, tk), lambda i, j, k: (j, k)) if trans_b else pl.BlockSpec((tk, tn), lambda i, j, k: (k, j))
    return pl.pallas_call(
        body,
        name=name,
        grid=(m // tm, n // tn, nk),
        in_specs=[a_spec, b_spec],
        out_specs=pl.BlockSpec((tm, tn), lambda i, j, k: (i, j)),
        out_shape=jax.ShapeDtypeStruct((m, n), out_dtype),
        scratch_shapes=[pltpu.VMEM((tm, tn) if nk > 1 else (8, 128), F32)],
        compiler_params=_params(("parallel", "parallel", "arbitrary")),
    )(a, b)


def _make_mm_w(name, out_dtype):
    @jax.custom_vjp
    def mm_w(a, w):
        return _matmul(a, w, out_dtype=out_dtype, name=name + "_fwd")

    def fwd(a, w):
        return _matmul(a, w, out_dtype=out_dtype, name=name + "_fwd"), (a, w)

    def bwd(res, g):
        a, w = res
        g = g.astype(BF16)
        da = _matmul(g, w, trans_b=True, out_dtype=a.dtype, name=name + "_da")
        dw = _matmul(a, g, trans_a=True, out_dtype=w.dtype, name=name + "_dw")
        return da, dw

    mm_w.defvjp(fwd, bwd)
    return mm_w


def _make_rowwise(f, name, out_dtypes, out_cols, n_lat, tm, diff_rows):
    def tile_fn(*args):
        return tuple(o.astype(dt) for o, dt in zip(f(*args), out_dtypes))

    def fwd_call(rows, vecs):
        t = rows[0].shape[0]
        nr, nv = len(rows), len(vecs)
        nl = n_lat // tm

        def body(*refs):
            grp = (pl.program_id(0) >= nl).astype(jnp.int32)
            args = [r[...] for r in refs[:nr]] + [v[grp] for v in refs[nr:nr + nv]]
            for o_ref, o in zip(refs[nr + nv:], tile_fn(*args)):
                o_ref[...] = o

        return pl.pallas_call(
            body,
            name=name + "_fwd",
            grid=(t // tm,),
            in_specs=[pl.BlockSpec((tm, r.shape[1]), lambda i: (i, 0)) for r in rows]
            + [pl.BlockSpec(v.shape, lambda i: (0, 0, 0)) for v in vecs],
            out_specs=[pl.BlockSpec((tm, c), lambda i: (i, 0)) for c in out_cols],
            out_shape=[jax.ShapeDtypeStruct((t, c), dt) for c, dt in zip(out_cols, out_dtypes)],
            compiler_params=_params(("parallel",)),
        )(*rows, *vecs)

    def bwd_call(rows, vecs, gs):
        t = rows[0].shape[0]
        nr, nv, ng = len(rows), len(vecs), len(gs)
        nl = n_lat // tm
        nd = len(diff_rows)

        def body(*refs):
            i = pl.program_id(0)
            grp = (i >= nl).astype(jnp.int32)
            args = [r[...] for r in refs[:nr]] + [v[grp] for v in refs[nr:nr + nv]]
            g_refs = refs[nr + nv:nr + nv + ng]
            drow_refs = refs[nr + nv + ng:nr + nv + ng + nd]
            dvec_refs = refs[nr + nv + ng + nd:]
            _, vjp = jax.vjp(tile_fn, *args)
            grads = vjp(tuple(g[...] for g in g_refs))
            for d_ref, k in zip(drow_refs, diff_rows):
                d_ref[...] = grads[k].astype(d_ref.dtype)

            @pl.when(i == 0)
            def _():
                for d_ref in dvec_refs:
                    d_ref[...] = jnp.zeros_like(d_ref)

            for j, d_ref in enumerate(dvec_refs):
                d_ref[grp] += grads[nr + j]

        outs = pl.pallas_call(
            body,
            name=name + "_bwd",
            grid=(t // tm,),
            in_specs=[pl.BlockSpec((tm, r.shape[1]), lambda i: (i, 0)) for r in rows]
            + [pl.BlockSpec(v.shape, lambda i: (0, 0, 0)) for v in vecs]
            + [pl.BlockSpec((tm, g.shape[1]), lambda i: (i, 0)) for g in gs],
            out_specs=[pl.BlockSpec((tm, rows[k].shape[1]), lambda i: (i, 0)) for k in diff_rows]
            + [pl.BlockSpec(v.shape, lambda i: (0, 0, 0)) for v in vecs],
            out_shape=[jax.ShapeDtypeStruct(rows[k].shape, rows[k].dtype) for k in diff_rows]
            + [jax.ShapeDtypeStruct(v.shape, F32) for v in vecs],
            compiler_params=_params(("arbitrary",)),
        )(*rows, *vecs, *gs)
        return outs[:nd], outs[nd:]

    @jax.custom_vjp
    def op(rows, vecs):
        return tuple(fwd_call(rows, vecs))

    def op_fwd(rows, vecs):
        return tuple(fwd_call(rows, vecs)), (rows, vecs)

    def op_bwd(res, gs):
        rows, vecs = res
        drows, dvecs = bwd_call(rows, vecs, tuple(gs))
        full = [jnp.zeros_like(r) for r in rows]
        for d, k in zip(drows, diff_rows):
            full[k] = d
        return tuple(full), tuple(dvecs)

    op.defvjp(op_fwd, op_bwd)
    return op


def _f_norm_mod(x, g, scale, shift):
    r = lax.rsqrt(jnp.mean(x * x, axis=-1, keepdims=True) + NORM_EPS)
    return ((x * r * g) * (1.0 + scale) + shift,)


def _f_gate_na(o, z):
    return (o.astype(F32) * _silu(z.astype(F32)),)


def _f_merge(g_na, g_ret, y_na, y_ret):
    return (jax.nn.sigmoid(g_na.astype(F32)) * y_na + jax.nn.sigmoid(g_ret.astype(F32)) * y_ret,)


def _f_residual(x, out, gate):
    return (x + gate * out,)


def _f_loss(x, target, g):
    r = lax.rsqrt(jnp.mean(x * x, axis=-1, keepdims=True) + NORM_EPS)
    y = x * r * g
    e = 0.5 * jnp.mean(jnp.square(y - target), axis=-1, keepdims=True)
    return (jnp.broadcast_to(e * (1.0 / 128.0), (x.shape[0], 128)),)


def _gate_ret_fwd_call(of, ob, z, tm):
    t, w = of.shape
    nh = w // RET_VAL_DIM

    def body(of_ref, ob_ref, z_ref, a_ref):
        for hh in range(nh):
            sl = slice(hh * RET_VAL_DIM, (hh + 1) * RET_VAL_DIM)
            o = of_ref[:, sl] + ob_ref[:, sl]
            r = lax.rsqrt(jnp.mean(o * o, axis=-1, keepdims=True) + NORM_EPS)
            a_ref[:, sl] = ((o * r) * _silu(z_ref[:, sl].astype(F32))).astype(a_ref.dtype)

    spec = pl.BlockSpec((tm, w), lambda i: (i, 0))
    return pl.pallas_call(
        body, name="gate_ret_fwd", grid=(t // tm,), in_specs=[spec, spec, spec], out_specs=spec,
        out_shape=jax.ShapeDtypeStruct((t, w), BF16), compiler_params=_params(("parallel",)),
    )(of, ob, z)


def _gate_ret_bwd_call(of, ob, z, da, tm):
    t, w = of.shape
    nh = w // RET_VAL_DIM

    def body(of_ref, ob_ref, z_ref, da_ref, do_ref, dz_ref):
        for hh in range(nh):
            sl = slice(hh * RET_VAL_DIM, (hh + 1) * RET_VAL_DIM)
            o = of_ref[:, sl] + ob_ref[:, sl]
            r = lax.rsqrt(jnp.mean(o * o, axis=-1, keepdims=True) + NORM_EPS)
            n = o * r
            zf = z_ref[:, sl].astype(F32)
            sg = jax.nn.sigmoid(zf)
            g = da_ref[:, sl].astype(F32)
            dn = g * (zf * sg)
            dz_ref[:, sl] = (g * n * (sg * (1.0 + zf * (1.0 - sg)))).astype(dz_ref.dtype)
            do_ref[:, sl] = r * (dn - n * jnp.mean(dn * n, axis=-1, keepdims=True))

    spec = pl.BlockSpec((tm, w), lambda i: (i, 0))
    return pl.pallas_call(
        body, name="gate_ret_bwd", grid=(t // tm,), in_specs=[spec, spec, spec, spec], out_specs=[spec, spec],
        out_shape=[jax.ShapeDtypeStruct((t, w), F32), jax.ShapeDtypeStruct((t, w), z.dtype)],
        compiler_params=_params(("parallel",)),
    )(of, ob, z, da)


def _make_gate_ret(tm):
    @jax.custom_vjp
    def gate_ret(of, ob, z):
        return _gate_ret_fwd_call(of, ob, z, tm)

    def fwd(of, ob, z):
        return _gate_ret_fwd_call(of, ob, z, tm), (of, ob, z)

    def bwd(res, da):
        of, ob, z = res
        do, dz = _gate_ret_bwd_call(of, ob, z, da, tm)
        return do, do, dz

    gate_ret.defvjp(fwd, bwd)
    return gate_ret


def _na_geometry(t, n_lat):
    rows = n_lat // GRID_W
    kh = min(NA_WIN_ROWS, rows)
    return rows, kh, kh * GRID_W, t - n_lat, t // GRID_W


def _na_row0(r, rows, kh):
    return jnp.clip(r - kh // 2, 0, rows - kh)


def _na_bias_idx(r, rows, kh):
    return jnp.clip(_na_row0(r, rows, kh) - r + (NA_WIN_ROWS - 1), 0, NA_WIN_ROWS - 1)


def _na_fwd_call(q, k, v, bt, n_lat):
    t, w = q.shape
    nh = w // NA_HEAD_DIM
    rows, kh, n_loc, n_ctx, nq = _na_geometry(t, n_lat)
    scale = NA_HEAD_DIM ** -0.5

    def body(q_ref, k_ref, v_ref, bt_ref, o_ref):
        r = pl.program_id(1)
        qb = q_ref[...]
        kc = k_ref[pl.ds(n_lat, n_ctx), :]
        vc = v_ref[pl.ds(n_lat, n_ctx), :]
        s_ctx = _dot_nt(qb, kc) * scale

        @pl.when(r < rows)
        def _():
            start = pl.multiple_of(_na_row0(r, rows, kh) * GRID_W, GRID_W)
            kw = k_ref[pl.ds(start, n_loc), :]
            vw = v_ref[pl.ds(start, n_loc), :]
            s_loc = _dot_nt(qb, kw) * scale + bt_ref[0, 0]
            m = jnp.maximum(jnp.max(s_loc, axis=-1, keepdims=True), jnp.max(s_ctx, axis=-1, keepdims=True))
            p_loc = jnp.exp(s_loc - m)
            p_ctx = jnp.exp(s_ctx - m)
            l = jnp.sum(p_loc, axis=-1, keepdims=True) + jnp.sum(p_ctx, axis=-1, keepdims=True)
            o = _dot(p_loc.astype(BF16), vw) + _dot(p_ctx.astype(BF16), vc)
            o_ref[...] = (o / l).astype(o_ref.dtype)

        @pl.when(r >= rows)
        def _():
            m = jnp.max(s_ctx, axis=-1, keepdims=True)
            p = jnp.exp(s_ctx - m)
            l = jnp.sum(p, axis=-1, keepdims=True)
            o_ref[...] = (_dot(p.astype(BF16), vc) / l).astype(o_ref.dtype)

    return pl.pallas_call(
        body,
        name="na_attn_fwd",
        grid=(nh, nq),
        in_specs=[
            pl.BlockSpec((GRID_W, NA_HEAD_DIM), lambda h, r: (r, h)),
            pl.BlockSpec((t, NA_HEAD_DIM), lambda h, r: (0, h)),
            pl.BlockSpec((t, NA_HEAD_DIM), lambda h, r: (0, h)),
            pl.BlockSpec((1, 1, GRID_W, n_loc), lambda h, r: (h, _na_bias_idx(r, rows, kh), 0, 0)),
        ],
        out_specs=pl.BlockSpec((GRID_W, NA_HEAD_DIM), lambda h, r: (r, h)),
        out_shape=jax.ShapeDtypeStruct((t, w), BF16),
        compiler_params=_params(("parallel", "arbitrary")),
    )(q, k, v, bt)


def _na_bwd_call(q, k, v, bt, do, n_lat):
    t, w = q.shape
    nh = w // NA_HEAD_DIM
    rows, kh, n_loc, n_ctx, nq = _na_geometry(t, n_lat)
    scale = NA_HEAD_DIM ** -0.5

    def body(q_ref, k_ref, v_ref, bt_ref, do_ref, dq_ref, dk_ref, dv_ref, dbt_ref):
        r = pl.program_id(1)

        @pl.when(r == 0)
        def _():
            dk_ref[...] = jnp.zeros_like(dk_ref)
            dv_ref[...] = jnp.zeros_like(dv_ref)

        qb = q_ref[...]
        dob = do_ref[...]
        kc = k_ref[pl.ds(n_lat, n_ctx), :]
        vc = v_ref[pl.ds(n_lat, n_ctx), :]
        s_ctx = _dot_nt(qb, kc) * scale
        dp_ctx = _dot_nt(dob, vc)

        @pl.when(r < rows)
        def _():
            start = pl.multiple_of(_na_row0(r, rows, kh) * GRID_W, GRID_W)
            kw = k_ref[pl.ds(start, n_loc), :]
            vw = v_ref[pl.ds(start, n_loc), :]
            s_loc = _dot_nt(qb, kw) * scale + bt_ref[0, 0]
            m = jnp.maximum(jnp.max(s_loc, axis=-1, keepdims=True), jnp.max(s_ctx, axis=-1, keepdims=True))
            p_loc = jnp.exp(s_loc - m)
            p_ctx = jnp.exp(s_ctx - m)
            inv = 1.0 / (jnp.sum(p_loc, axis=-1, keepdims=True) + jnp.sum(p_ctx, axis=-1, keepdims=True))
            p_loc = p_loc * inv
            p_ctx = p_ctx * inv
            dp_loc = _dot_nt(dob, vw)
            delta = jnp.sum(p_loc * dp_loc, axis=-1, keepdims=True) + jnp.sum(p_ctx * dp_ctx, axis=-1, keepdims=True)
            ds_loc = p_loc * (dp_loc - delta)
            ds_ctx = p_ctx * (dp_ctx - delta)
            first = jnp.logical_or(r == 0, _na_bias_idx(r, rows, kh) != _na_bias_idx(r - 1, rows, kh))

            @pl.when(first)
            def _():
                dbt_ref[0, 0] = ds_loc

            @pl.when(jnp.logical_not(first))
            def _():
                dbt_ref[0, 0] += ds_loc

            dsl = (ds_loc * scale).astype(BF16)
            dsc = (ds_ctx * scale).astype(BF16)
            dq_ref[...] = (_dot(dsl, kw) + _dot(dsc, kc)).astype(dq_ref.dtype)
            dk_ref[pl.ds(start, n_loc), :] += _dot_tn(dsl, qb)
            dv_ref[pl.ds(start, n_loc), :] += _dot_tn(p_loc.astype(BF16), dob)
            dk_ref[pl.ds(n_lat, n_ctx), :] += _dot_tn(dsc, qb)
            dv_ref[pl.ds(n_lat, n_ctx), :] += _dot_tn(p_ctx.astype(BF16), dob)

        @pl.when(r >= rows)
        def _():
            m = jnp.max(s_ctx, axis=-1, keepdims=True)
            p = jnp.exp(s_ctx - m)
            p = p * (1.0 / jnp.sum(p, axis=-1, keepdims=True))
            delta = jnp.sum(p * dp_ctx, axis=-1, keepdims=True)
            dsc = (p * (dp_ctx - delta) * scale).astype(BF16)
            dq_ref[...] = _dot(dsc, kc).astype(dq_ref.dtype)
            dk_ref[pl.ds(n_lat, n_ctx), :] += _dot_tn(dsc, qb)
            dv_ref[pl.ds(n_lat, n_ctx), :] += _dot_tn(p.astype(BF16), dob)

    qspec = pl.BlockSpec((GRID_W, NA_HEAD_DIM), lambda h, r: (r, h))
    kspec = pl.BlockSpec((t, NA_HEAD_DIM), lambda h, r: (0, h))
    bspec = pl.BlockSpec((1, 1, GRID_W, n_loc), lambda h, r: (h, _na_bias_idx(r, rows, kh), 0, 0))
    return pl.pallas_call(
        body,
        name="na_attn_bwd",
        grid=(nh, nq),
        in_specs=[qspec, kspec, kspec, bspec, qspec],
        out_specs=[qspec, kspec, kspec, bspec],
        out_shape=[
            jax.ShapeDtypeStruct((t, w), BF16),
            jax.ShapeDtypeStruct((t, w), F32),
            jax.ShapeDtypeStruct((t, w), F32),
            jax.ShapeDtypeStruct(bt.shape, F32),
        ],
        compiler_params=_params(("parallel", "arbitrary")),
    )(q, k, v, bt, do)


def _make_na_attn(n_lat):
    @jax.custom_vjp
    def na_attn(q, k, v, bt):
        return _na_fwd_call(q, k, v, bt, n_lat)

    def fwd(q, k, v, bt):
        return _na_fwd_call(q, k, v, bt, n_lat), (q, k, v, bt)

    def bwd(res, do):
        q, k, v, bt = res
        dq, dk, dv, dbt = _na_bwd_call(q, k, v, bt, do, n_lat)
        return dq, dk.astype(k.dtype), dv.astype(v.dtype), dbt

    na_attn.defvjp(fwd, bwd)
    return na_attn


def _na_bias_table(rpb, rows):
    kh = min(NA_WIN_ROWS, rows)
    nj = NA_WIN_ROWS
    e1 = np.zeros((nj, kh, 2 * NA_WIN_ROWS - 1), np.float32)
    for j in range(nj):
        for kk in range(kh):
            if j + kk < 2 * NA_WIN_ROWS - 1:
                e1[j, kk, j + kk] = 1.0
    cidx = np.arange(GRID_W)
    dc = np.clip(cidx[None, :] - cidx[:, None] + (NA_WIN_COLS - 1), 0, 2 * NA_WIN_COLS - 2)
    e2 = np.zeros((GRID_W, GRID_W, 2 * NA_WIN_COLS - 1), np.float32)
    e2[np.arange(GRID_W)[:, None], np.arange(GRID_W)[None, :], dc] = 1.0
    c0 = np.clip(cidx - NA_WIN_COLS // 2, 0, GRID_W - NA_WIN_COLS)
    col_in = (cidx[None, :] >= c0[:, None]) & (cidx[None, :] < c0[:, None] + NA_WIN_COLS)
    t1 = jnp.einsum("hab,jka->hjkb", rpb, jnp.asarray(e1), precision=lax.Precision.HIGHEST)
    b = jnp.einsum("hjkb,cwb->hjckw", t1, jnp.asarray(e2), precision=lax.Precision.HIGHEST)
    b = jnp.where(jnp.asarray(col_in)[None, None, :, None, :], b, MASK_VALUE)
    return b.reshape(rpb.shape[0], nj, GRID_W, kh * GRID_W)


def _ret_decays(lam_s, reverse):
    c = RET_CHUNK
    ii = lax.broadcasted_iota(jnp.int32, (c, c), 0)
    jj = lax.broadcasted_iota(jnp.int32, (c, c), 1)
    d = (jj - ii) if reverse else (ii - jj)
    dpos = jnp.maximum(d.astype(F32), 0.0)
    mask = jnp.where(d >= 0, jnp.exp(dpos * lam_s), 0.0)
    pi = lax.broadcasted_iota(jnp.int32, (c, 1), 0).astype(F32)
    qpos = (c - pi) if reverse else (pi + 1.0)
    kpos = pi if reverse else (c - 1.0 - pi)
    qd = jnp.exp(qpos * lam_s)
    kd = jnp.exp(kpos * lam_s)
    g = jnp.exp(jnp.full((1, RET_VAL_DIM), c * lam_s, F32))
    return mask, dpos, qd, kd, qpos, kpos, g


def _ret_chunk_of(t, nt, nl, reverse):
    return (nt - 1 - t) if reverse else (t + nl) % nt


def _ret_fwd_call(qr, kr, v, lam, n_lat, reverse):
    t = qr.shape[0]
    nh = qr.shape[1] // RET_KEY_DIM
    c = RET_CHUNK
    nt, nl = t // c, n_lat // c

    def body(lam_ref, q_ref, k_ref, v_ref, o_ref, s_ref, state):
        h, step = pl.program_id(0), pl.program_id(1)

        @pl.when(step == 0)
        def _():
            state[...] = jnp.zeros_like(state)

        mask, _, qd, kd, _, _, g = _ret_decays(lam_ref[h], reverse)
        q, k, vv = q_ref[...], k_ref[...], v_ref[...]
        p = _dot_nt(q, k) * mask
        s = state[...]
        qs = (q.astype(F32) * qd).astype(BF16)
        o_ref[...] = _dot(p.astype(BF16), vv) + _dot(qs, s.astype(BF16))
        s_ref[0, 0] = s
        ks = (k.astype(F32) * kd).astype(BF16)
        state[...] = s * g + _dot_tn(ks, vv)

    def cmap(h, step, lam_ref):
        return (_ret_chunk_of(step, nt, nl, reverse), h)

    return pl.pallas_call(
        body,
        name="retention_rev_fwd" if reverse else "retention_fwd",
        grid_spec=pltpu.PrefetchScalarGridSpec(
            num_scalar_prefetch=1,
            grid=(nh, nt),
            in_specs=[
                pl.BlockSpec((c, RET_KEY_DIM), cmap),
                pl.BlockSpec((c, RET_KEY_DIM), cmap),
                pl.BlockSpec((c, RET_VAL_DIM), cmap),
            ],
            out_specs=[
                pl.BlockSpec((c, RET_VAL_DIM), cmap),
                pl.BlockSpec((1, 1, RET_KEY_DIM, RET_VAL_DIM), lambda h, step, lam_ref: (h, step, 0, 0)),
            ],
            scratch_shapes=[pltpu.VMEM((RET_KEY_DIM, RET_VAL_DIM), F32)],
        ),
        out_shape=[
            jax.ShapeDtypeStruct((t, nh * RET_VAL_DIM), F32),
            jax.ShapeDtypeStruct((nh, nt, RET_KEY_DIM, RET_VAL_DIM), F32),
        ],
        compiler_params=_params(("parallel", "arbitrary")),
    )(lam, qr, kr, v)


def _ret_bwd_call(qr, kr, v, lam, states, do, n_lat, reverse):
    t = qr.shape[0]
    nh = qr.shape[1] // RET_KEY_DIM
    c = RET_CHUNK
    nt, nl = t // c, n_lat // c

    def body(lam_ref, q_ref, k_ref, v_ref, s_ref, do_ref, dq_ref, dk_ref, dv_ref, dl_ref, dstate):
        h, rstep = pl.program_id(0), pl.program_id(1)

        @pl.when(rstep == 0)
        def _():
            dstate[...] = jnp.zeros_like(dstate)
            dl_ref[...] = jnp.zeros_like(dl_ref)

        mask, dpos, qd, kd, qpos, kpos, g = _ret_decays(lam_ref[h], reverse)
        q, k, vv = q_ref[...], k_ref[...], v_ref[...]
        qf, kf = q.astype(F32), k.astype(F32)
        s = s_ref[0, 0]
        ds = dstate[...]
        dob = do_ref[...].astype(BF16)
        sb, dsb = s.astype(BF16), ds.astype(BF16)
        a = _dot_nt(q, k)
        p = a * mask
        dp = _dot_nt(dob, vv)
        da = dp * mask
        dab = da.astype(BF16)
        dqc = _dot_nt(dob, sb)
        dkc = _dot_nt(vv, dsb)
        qs = (qf * qd).astype(BF16)
        ks = (kf * kd).astype(BF16)
        dq_ref[...] = (_dot(dab, k) + dqc * qd).astype(dq_ref.dtype)
        dk_ref[...] = (_dot_tn(dab, q) + dkc * kd).astype(dk_ref.dtype)
        dv_ref[...] = (_dot_tn(p.astype(BF16), dob) + _dot(ks, dsb)).astype(dv_ref.dtype)
        terms = (
            jnp.sum(jnp.sum(da * a * dpos, axis=1, keepdims=True), axis=0, keepdims=True)
            + jnp.sum(jnp.sum(dqc * qf * (qd * qpos), axis=1, keepdims=True), axis=0, keepdims=True)
            + jnp.sum(jnp.sum(dkc * kf * (kd * kpos), axis=1, keepdims=True), axis=0, keepdims=True)
            + jnp.sum(jnp.sum(ds * s * (g * c), axis=1, keepdims=True), axis=0, keepdims=True)
        )
        dl_ref[0] += jnp.broadcast_to(terms, (8, 128))
        dstate[...] = ds * g + _dot_tn(qs, dob)

    def cmap(h, rstep, lam_ref):
        return (_ret_chunk_of(nt - 1 - rstep, nt, nl, reverse), h)

    return pl.pallas_call(
        body,
        name="retention_rev_bwd" if reverse else "retention_bwd",
        grid_spec=pltpu.PrefetchScalarGridSpec(
            num_scalar_prefetch=1,
            grid=(nh, nt),
            in_specs=[
                pl.BlockSpec((c, RET_KEY_DIM), cmap),
                pl.BlockSpec((c, RET_KEY_DIM), cmap),
                pl.BlockSpec((c, RET_VAL_DIM), cmap),
                pl.BlockSpec((1, 1, RET_KEY_DIM, RET_VAL_DIM), lambda h, rstep, lam_ref: (h, nt - 1 - rstep, 0, 0)),
                pl.BlockSpec((c, RET_VAL_DIM), cmap),
            ],
            out_specs=[
                pl.BlockSpec((c, RET_KEY_DIM), cmap),
                pl.BlockSpec((c, RET_KEY_DIM), cmap),
                pl.BlockSpec((c, RET_VAL_DIM), cmap),
                pl.BlockSpec((1, 8, 128), lambda h, rstep, lam_ref: (h, 0, 0)),
            ],
            scratch_shapes=[pltpu.VMEM((RET_KEY_DIM, RET_VAL_DIM), F32)],
        ),
        out_shape=[
            jax.ShapeDtypeStruct(qr.shape, qr.dtype),
            jax.ShapeDtypeStruct(kr.shape, kr.dtype),
            jax.ShapeDtypeStruct(v.shape, v.dtype),
            jax.ShapeDtypeStruct((nh, 8, 128), F32),
        ],
        compiler_params=_params(("parallel", "arbitrary")),
    )(lam, qr, kr, v, states, do)


def _make_retention(n_lat, reverse):
    @jax.custom_vjp
    def ret(qr, kr, v, lam):
        return _ret_fwd_call(qr, kr, v, lam, n_lat, reverse)[0]

    def fwd(qr, kr, v, lam):
        o, states = _ret_fwd_call(qr, kr, v, lam, n_lat, reverse)
        return o, (qr, kr, v, lam, states)

    def bwd(res, do):
        qr, kr, v, lam, states = res
        dq, dk, dv, dl = _ret_bwd_call(qr, kr, v, lam, states, do, n_lat, reverse)
        return dq, dk, dv, dl[:, 0, 0]

    ret.defvjp(fwd, bwd)
    return ret


def _rope_tables(t, n_lat):
    nf = RET_KEY_DIM // 4
    tok = np.arange(n_lat)
    inv_freq = (ROPE_BASE ** (-np.arange(nf, dtype=np.float32) / nf)).astype(np.float32)
    row = (tok // GRID_W).astype(np.float32)
    col = (tok % GRID_W).astype(np.float32)
    ang = np.concatenate([row[:, None] * inv_freq, col[:, None] * inv_freq], axis=-1).astype(np.float32)
    cos = np.ones((t, 2 * nf), np.float32)
    sin = np.zeros((t, 2 * nf), np.float32)
    cos[:n_lat] = np.cos(ang)
    sin[:n_lat] = np.sin(ang)
    return jnp.asarray(cos), jnp.asarray(sin)


def _rope(xb, cos, sin, mult):
    t, w = xb.shape
    nh = w // RET_KEY_DIM
    half = RET_KEY_DIM // 2
    x = xb.astype(F32).reshape(t, nh, 2, half)
    x1, x2 = x[:, :, 0], x[:, :, 1]
    c, s = cos[:, None, :], sin[:, None, :]
    out = jnp.stack([x1 * c - x2 * s, x2 * c + x1 * s], axis=2) * mult
    return out.reshape(t, w).astype(BF16)


def _my_position():
    return lax.axis_index("x"), lax.axis_index("y"), lax.axis_index("c")


def _flip(pos, k):
    x, y, c = pos
    return (1 - x if k & 4 else x, 1 - y if k & 2 else y, 1 - c if k & 1 else c)


def _linear(pos):
    return 4 * pos[0] + 2 * pos[1] + pos[2]


def _slab(ref, axis, idx, size):
    start = pl.multiple_of(idx * size, size)
    return ref.at[pl.ds(start, size), :] if axis == 0 else ref.at[:, pl.ds(start, size)]


def _allgather_weights(shards, axes):
    n = len(shards)

    def body(*refs):
        x_refs, o_refs = refs[:n], refs[n:2 * n]
        send_sems, recv_sems, local_sems = refs[2 * n:]
        me = _my_position()
        sibling = _flip(me, 1)
        chips = (4, 2, 6)

        def rows(a, pos):
            return _slab(o_refs[a], axes[a], _linear(pos), shards[a].shape[axes[a]])

        def copy(a, k, block, to, src=None):
            return pltpu.make_async_remote_copy(
                src_ref=rows(a, block) if src is None else src, dst_ref=rows(a, block),
                send_sem=send_sems.at[a, k], recv_sem=recv_sems.at[a, k], device_id=to, device_id_type=MESH)

        mine = [pltpu.make_async_copy(x_refs[a], rows(a, me), local_sems.at[a]) for a in range(n)]
        for cp in mine:
            cp.start()
        first = []
        for a in range(n):
            first.append(copy(a, 0, me, sibling, src=x_refs[a]))
            for j, kc in enumerate(chips):
                first.append(copy(a, 1 + j, me, _flip(me, kc), src=x_refs[a]))
        for cp in first:
            cp.start()
        passed = []
        for j, kc in enumerate(chips):
            for a in range(n):
                copy(a, 1 + j, _flip(me, kc), me).wait_recv()
                cp = copy(a, 4 + j, _flip(me, kc), sibling)
                cp.start()
                passed.append(cp)
        for a in range(n):
            copy(a, 0, sibling, me).wait_recv()
        for j, kc in enumerate(chips):
            for a in range(n):
                copy(a, 4 + j, _flip(sibling, kc), me).wait_recv()
        for cp in first + passed:
            cp.wait_send()
        for cp in mine:
            cp.wait()

    def full_shape(s, ax):
        return (s.shape[0] * N_DEV, s.shape[1]) if ax == 0 else (s.shape[0], s.shape[1] * N_DEV)

    return pl.pallas_call(
        body,
        name="allgather_weights",
        in_specs=[ANY] * n,
        out_specs=[ANY] * n,
        out_shape=[jax.ShapeDtypeStruct(full_shape(s, ax), s.dtype) for s, ax in zip(shards, axes)],
        scratch_shapes=[pltpu.SemaphoreType.DMA((n, 7)), pltpu.SemaphoreType.DMA((n, 7)), pltpu.SemaphoreType.DMA((n,))],
        compiler_params=pltpu.CompilerParams(has_side_effects=True),
    )(*shards)


def _reduce_scatter_grads(grads, axes, sizes):
    n = len(grads)
    nl = len(grads[0])
    flat = [g for gl in grads for g in gl]

    def slab_shape(a):
        s = grads[a][0].shape
        return (sizes[a], s[1]) if axes[a] == 0 else (s[0], sizes[a])

    def body(*refs):
        g_refs, r_refs = refs[:n * nl], refs[n * nl:n * nl + n]
        send_sems, recv_sems, local_sems = refs[n * nl + n:]
        me = _my_position()

        def src(a, l, pos):
            return _slab(g_refs[a * nl + l], axes[a], _linear(pos), sizes[a])

        def copy(a, l, k):
            peer = _flip(me, k)
            return pltpu.make_async_remote_copy(
                src_ref=src(a, l, peer), dst_ref=r_refs[a].at[k, l],
                send_sem=send_sems.at[a * nl + l, k - 1], recv_sem=recv_sems.at[a * nl + l, k - 1],
                device_id=peer, device_id_type=MESH)

        mine = [pltpu.make_async_copy(src(a, l, me), r_refs[a].at[0, l], local_sems.at[a * nl + l])
                for a in range(n) for l in range(nl)]
        for cp in mine:
            cp.start()
        sends = [copy(a, l, k) for k in (1, 2, 4, 6, 3, 5, 7) for a in range(n) for l in range(nl)]
        for cp in sends:
            cp.start()
        for cp in sends:
            cp.wait_recv()
        for cp in sends:
            cp.wait_send()
        for cp in mine:
            cp.wait()

    return pl.pallas_call(
        body,
        name="reduce_scatter_grads",
        in_specs=[ANY] * (n * nl),
        out_specs=[ANY] * n,
        out_shape=[jax.ShapeDtypeStruct((N_DEV, nl) + slab_shape(a), grads[a][0].dtype) for a in range(n)],
        scratch_shapes=[pltpu.SemaphoreType.DMA((n * nl, 7)), pltpu.SemaphoreType.DMA((n * nl, 7)),
                        pltpu.SemaphoreType.DMA((n * nl,))],
        compiler_params=pltpu.CompilerParams(has_side_effects=True),
    )(*flat)


def _small_allgather(v, name):
    r, c = v.shape

    def body(v_ref, all_ref, sum_ref, send_sems, recv_sems):
        me = _my_position()
        all_ref[_linear(me)] = v_ref[...]
        copies = []
        for k in range(1, N_DEV):
            peer = _flip(me, k)
            copies.append(pltpu.make_async_remote_copy(
                src_ref=v_ref, dst_ref=all_ref.at[_linear(me)], send_sem=send_sems.at[k - 1], recv_sem=recv_sems.at[k - 1],
                device_id=peer, device_id_type=MESH))
        for cp in copies:
            cp.start()
        for k in range(1, N_DEV):
            peer = _flip(me, k)
            pltpu.make_async_remote_copy(
                src_ref=v_ref, dst_ref=all_ref.at[_linear(peer)], send_sem=send_sems.at[k - 1], recv_sem=recv_sems.at[k - 1],
                device_id=peer, device_id_type=MESH).wait_recv()
        for cp in copies:
            cp.wait_send()
        acc = all_ref[0]
        for d in range(1, N_DEV):
            acc = acc + all_ref[d]
        sum_ref[...] = acc

    return pl.pallas_call(
        body,
        name=name,
        in_specs=[VMEM_SPEC],
        out_specs=[VMEM_SPEC, VMEM_SPEC],
        out_shape=[jax.ShapeDtypeStruct((N_DEV, r, c), F32), jax.ShapeDtypeStruct((r, c), F32)],
        scratch_shapes=[pltpu.SemaphoreType.DMA((N_DEV - 1,)), pltpu.SemaphoreType.DMA((N_DEV - 1,))],
        compiler_params=pltpu.CompilerParams(has_side_effects=True, vmem_limit_bytes=VMEM_LIMIT),
    )(v)


def _ada_fwd_call(cin, ada_w, ada_b_cols):
    nl, d, ncol = ada_w.shape
    nrow = cin.shape[0]

    def body(c_ref, w_ref, b_ref, o_ref):
        cs = _silu(c_ref[...]).astype(BF16)
        for l in range(nl):
            o_ref[l] = _dot(cs, w_ref[l].astype(BF16)) + b_ref[l]

    return pl.pallas_call(
        body, name="ada_fwd", in_specs=[VMEM_SPEC] * 3, out_specs=VMEM_SPEC,
        out_shape=jax.ShapeDtypeStruct((nl, nrow, ncol), F32), compiler_params=_params(),
    )(cin, ada_w, ada_b_cols)


def _ada_bwd_call(cin, ada_w, dmod):
    nl, d, ncol = ada_w.shape
    nrow = cin.shape[0]

    def body(c_ref, w_ref, dm_ref, gw_ref, dcs_ref):
        cs = _silu(c_ref[...]).astype(BF16)
        acc = jnp.zeros((nrow, d), F32)
        for l in range(nl):
            dm = dm_ref[l].astype(BF16)
            gw_ref[l] = _dot_tn(cs, dm)
            acc = acc + _dot_nt(dm, w_ref[l].astype(BF16))
        dcs_ref[...] = acc

    return pl.pallas_call(
        body, name="ada_bwd", in_specs=[VMEM_SPEC] * 3, out_specs=[VMEM_SPEC, VMEM_SPEC],
        out_shape=[jax.ShapeDtypeStruct((nl, d, ncol), F32), jax.ShapeDtypeStruct((nrow, d), F32)],
        compiler_params=_params(),
    )(cin, ada_w, dmod)


def _adamw_math(w, g, m, v):
    m = ADAM_B1 * m + (1.0 - ADAM_B1) * g
    v = ADAM_B2 * v + (1.0 - ADAM_B2) * jnp.square(g)
    m_hat = m / (1.0 - ADAM_B1 ** ADAM_STEP)
    v_hat = v / (1.0 - ADAM_B2 ** ADAM_STEP)
    delta = -ADAM_LR * (m_hat / (jnp.sqrt(v_hat) + ADAM_EPS) + ADAM_WD * w)
    return delta, m, v


def _adamw_sharded(w, m, v, slabs, name):
    nl, r, c = w.shape
    tm = _pick(r, (128, 64, 32, 16))

    def body(w_ref, m_ref, v_ref, s_ref, g_ref, d_ref, nm_ref, nv_ref):
        g = s_ref[0, 0].astype(F32)
        for k in range(1, N_DEV):
            g = g + s_ref[k, 0].astype(F32)
        delta, nm, nv = _adamw_math(w_ref[0], g, m_ref[0], v_ref[0])
        g_ref[0], d_ref[0], nm_ref[0], nv_ref[0] = g, delta, nm, nv

    spec = pl.BlockSpec((1, tm, c), lambda l, i: (l, i, 0))
    out = jax.ShapeDtypeStruct(w.shape, F32)
    return pl.pallas_call(
        body, name=name, grid=(nl, r // tm),
        in_specs=[spec, spec, spec, pl.BlockSpec((N_DEV, 1, tm, c), lambda l, i: (0, l, i, 0))],
        out_specs=[spec] * 4, out_shape=[out] * 4, compiler_params=_params(("parallel", "parallel")),
    )(w, m, v, slabs)


def _adamw_dense(w, g, m, v, name):
    r, c = w.shape
    tm = _pick(r, (256, 128, 64, 32, 16, 8))

    def body(w_ref, g_ref, m_ref, v_ref, d_ref, nm_ref, nv_ref):
        d_ref[...], nm_ref[...], nv_ref[...] = _adamw_math(w_ref[...], g_ref[...], m_ref[...], v_ref[...])

    spec = pl.BlockSpec((tm, c), lambda i: (i, 0))
    out = jax.ShapeDtypeStruct(w.shape, F32)
    return pl.pallas_call(
        body, name=name, grid=(r // tm,), in_specs=[spec] * 4, out_specs=[spec] * 3, out_shape=[out] * 3,
        compiler_params=_params(("parallel",)),
    )(w, g, m, v)


def _pack(parts, width=128):
    flat = jnp.concatenate([p.reshape(-1).astype(F32) for p in parts])
    n = flat.shape[0]
    total = -(-n // (8 * width)) * (8 * width)
    return jnp.pad(flat, (0, total - n)).reshape(total // width, width)


def _unpack(buf, shapes):
    flat = buf.reshape(-1)
    out, off = [], 0
    for s in shapes:
        n = int(np.prod(s))
        out.append(flat[off:off + n].reshape(s))
        off += n
    return out


def kernel(x, c, ctx, c_ctx, ada_w, ada_b, norm_g, w_in, na_rpb, ret_decay_logit, w_proj_na, w_proj_ret, w_out, final_g, loss_target, m_c_ctx, m_ada_w, m_ada_b, m_norm_g, m_w_in, m_na_rpb, m_ret_decay_logit, m_w_proj_na, m_w_proj_ret, m_w_out, m_final_g, v_c_ctx, v_ada_w, v_ada_b, v_norm_g, v_w_in, v_na_rpb, v_ret_decay_logit, v_w_proj_na, v_w_proj_ret, v_w_out, v_final_g):
    depth = w_in.shape[0]
    n_lat, d = x.shape[1], x.shape[2]
    n_ctx = ctx.shape[1]
    t = n_lat + n_ctx
    w_na = w_proj_na.shape[1]
    w_retv = w_proj_ret.shape[1] * N_DEV
    in_cols = w_in.shape[2] * N_DEV
    w_qk = (in_cols - 4 * w_na - 2 * w_retv - 2 * d) // 2
    sizes = (w_na, w_na, w_na, w_na, w_qk, w_qk, w_retv, w_retv, d, d)
    off = tuple(int(o) for o in np.cumsum((0,) + sizes))
    NA_Q, NA_K, NA_V, NA_Z, RET_Q, RET_K, RET_V, RET_Z, G_NA, G_RET = range(10)
    rows = n_lat // GRID_W
    me = _my_position()
    my_idx = _linear(me)
    tm_row = _pick(n_ctx, (256, 128))

    shards, axes = [], []
    for l in range(depth):
        shards += [w_in[l].astype(BF16), w_proj_na[l].astype(BF16), w_proj_ret[l].astype(BF16), w_out[l].astype(BF16)]
        axes += [1, 1, 0, 0]
    full = _allgather_weights(shards, axes)
    w_full = [full[4 * l:4 * l + 4] for l in range(depth)]

    ncol = ada_w.shape[2]
    c_all, _ = _small_allgather(jnp.pad(c, ((0, 7), (0, 0))), "allgather_c")
    cin = jnp.concatenate([c_all[:, 0, :], c_ctx[None, :], jnp.zeros((7, d), F32)], axis=0)
    ada_b_cols = lax.dynamic_slice_in_dim(ada_b, my_idx * ncol, ncol, axis=1)[:, None, :]
    mod_cols = _ada_fwd_call(cin, ada_w, ada_b_cols)
    mod_all, _ = _small_allgather(mod_cols.reshape(depth * 16, ncol), "allgather_mod")
    mod_all = mod_all.reshape(N_DEV, depth, 16, ncol).transpose(1, 2, 0, 3).reshape(depth, 16, N_DEV * ncol)
    mod_lat = lax.dynamic_index_in_dim(mod_all, my_idx, axis=1, keepdims=False)
    mod_ctx = mod_all[:, 8, :]

    cos, sin = _rope_tables(t, n_lat)
    k_scale = RET_KEY_DIM ** -0.5
    norm_mod = _make_rowwise(_f_norm_mod, "norm_mod", (BF16,), (d,), n_lat, tm_row, (0,))
    gate_na = _make_rowwise(_f_gate_na, "gate_na", (BF16,), (w_na,), n_lat, tm_row, (0, 1))
    merge = _make_rowwise(_f_merge, "merge", (BF16,), (d,), n_lat, tm_row, (0, 1, 2, 3))
    residual = _make_rowwise(_f_residual, "residual", (F32,), (d,), n_lat, tm_row, (0, 1))
    loss_rows = _make_rowwise(_f_loss, "loss_head", (F32,), (128,), n_lat, tm_row, (0,))
    gate_ret = _make_gate_ret(tm_row)
    na_attn = _make_na_attn(n_lat)
    ret_f = _make_retention(n_lat, False)
    ret_b = _make_retention(n_lat, True)
    mm_in = _make_mm_w("in_proj", BF16)
    mm_pna = _make_mm_w("proj_na", F32)
    mm_pret = _make_mm_w("proj_ret", F32)
    mm_out = _make_mm_w("out_proj", F32)

    def pair(a, b):
        return jnp.stack([a, b])[:, None, :]

    def forward(xs, ctxs, mod_lat, mod_ctx, norm_g, na_rpb, ret_decay_logit, final_g, weights):
        xa = jnp.concatenate([xs, ctxs], axis=0)
        for l in range(depth):
            wl_in, wl_pna, wl_pret, wl_out = weights[l]
            shift, scale, gate = jnp.split(mod_lat[l], 3)
            c_shift, c_scale, c_gate = jnp.split(mod_ctx[l], 3)
            (h,) = norm_mod((xa,), (pair(norm_g[l], norm_g[l]), pair(scale, c_scale), pair(shift, c_shift)))
            u = mm_in(h, wl_in)
            blk = [u[:, off[i]:off[i + 1]] for i in range(10)]
            bt = _na_bias_table(na_rpb[l], rows)
            o_na = na_attn(blk[NA_Q], blk[NA_K], blk[NA_V], bt)
            lam = jax.nn.log_sigmoid(ret_decay_logit[l].astype(F32))
            qr = _rope(blk[RET_Q], cos, sin, 1.0)
            kr = _rope(blk[RET_K], cos, sin, k_scale)
            o_f = ret_f(qr, kr, blk[RET_V], lam[0])
            o_b = ret_b(qr, kr, blk[RET_V], lam[1])
            (a_na,) = gate_na((o_na, blk[NA_Z]), ())
            a_ret = gate_ret(o_f, o_b, blk[RET_Z])
            y_na = mm_pna(a_na, wl_pna)
            y_ret = mm_pret(a_ret, wl_pret)
            (merged,) = merge((blk[G_NA], blk[G_RET], y_na, y_ret), ())
            out = mm_out(merged, wl_out)
            (xa,) = residual((xa, out), (pair(gate, c_gate),))
        (lr,) = loss_rows((xa[:n_lat], loss_target[0]), (pair(final_g, final_g),))
        return jnp.sum(lr)

    loss_local, vjp = jax.vjp(forward, x[0], ctx[0], mod_lat, mod_ctx, norm_g, na_rpb, ret_decay_logit, final_g, w_full)
    gx, _, d_mod_lat, d_mod_ctx, d_norm_g, d_rpb, d_decay, d_final_g, d_w = vjp(jnp.ones((), F32))
    loss = lax.psum(loss_local, ("x", "y", "c"))

    small_shapes = [d_mod_lat.shape, d_mod_ctx.shape, d_norm_g.shape, d_final_g.shape, d_rpb.shape, d_decay.shape]
    packed = _pack([d_mod_lat, d_mod_ctx, d_norm_g, d_final_g, d_rpb, d_decay])
    g_all, g_sum = _small_allgather(packed, "allgather_small_grads")
    dml_sum, dmc_sum, grad_norm_g, grad_final_g, grad_na_rpb, grad_decay = _unpack(g_sum, small_shapes)
    grad_ada_b = dml_sum + dmc_sum
    dml_all = g_all.reshape(N_DEV, -1)[:, :depth * 3 * d].reshape(N_DEV, depth, 3 * d)

    def my_cols(a):
        return lax.dynamic_slice_in_dim(a, my_idx * ncol, ncol, axis=a.ndim - 1)

    dmod = jnp.concatenate(
        [my_cols(dml_all).transpose(1, 0, 2), my_cols(dmc_sum)[:, None, :], jnp.zeros((depth, 7, ncol), F32)], axis=1)
    grad_ada_w, dcs_part = _ada_bwd_call(cin, ada_w, dmod)
    _, dcs = _small_allgather(dcs_part, "allgather_dcsilu")
    sg = jax.nn.sigmoid(c_ctx)
    grad_c_ctx = dcs[8] * (sg * (1.0 + c_ctx * (1.0 - sg)))

    grads = [[d_w[l][a] for l in range(depth)] for a in range(4)]
    slabs = _reduce_scatter_grads(
        grads, (1, 1, 0, 0), (w_in.shape[2], w_proj_na.shape[2], w_proj_ret.shape[1], w_out.shape[1]))
    g_w_in, d_w_in, nm_w_in, nv_w_in = _adamw_sharded(w_in, m_w_in, v_w_in, slabs[0], "adamw_w_in")
    g_pna, d_pna, nm_pna, nv_pna = _adamw_sharded(w_proj_na, m_w_proj_na, v_w_proj_na, slabs[1], "adamw_w_proj_na")
    g_pret, d_pret, nm_pret, nv_pret = _adamw_sharded(w_proj_ret, m_w_proj_ret, v_w_proj_ret, slabs[2], "adamw_w_proj_ret")
    g_out, d_out, nm_out, nv_out = _adamw_sharded(w_out, m_w_out, v_w_out, slabs[3], "adamw_w_out")

    def flat2(a):
        return a.reshape(a.shape[0] * a.shape[1], a.shape[2])

    d_ada, nm_ada, nv_ada = [a.reshape(ada_w.shape) for a in _adamw_dense(
        flat2(ada_w), flat2(grad_ada_w), flat2(m_ada_w), flat2(v_ada_w), "adamw_ada_w")]

    small_w = [c_ctx, ada_b, norm_g, na_rpb, ret_decay_logit, final_g]
    small_g = [grad_c_ctx, grad_ada_b, grad_norm_g, grad_na_rpb, grad_decay, grad_final_g]
    small_m = [m_c_ctx, m_ada_b, m_norm_g, m_na_rpb, m_ret_decay_logit, m_final_g]
    small_v = [v_c_ctx, v_ada_b, v_norm_g, v_na_rpb, v_ret_decay_logit, v_final_g]
    shp = [a.shape for a in small_w]
    ds_, nms_, nvs_ = _adamw_dense(_pack(small_w), _pack(small_g), _pack(small_m), _pack(small_v), "adamw_small")
    ds_, nms_, nvs_ = _unpack(ds_, shp), _unpack(nms_, shp), _unpack(nvs_, shp)

    def order(cc, aw, ab, ng, wi, rp, dl, pn, pr, wo, fg):
        return [cc, aw, ab, ng, wi, rp, dl, pn, pr, wo, fg]

    grads_out = order(grad_c_ctx, grad_ada_w, grad_ada_b, grad_norm_g, g_w_in, grad_na_rpb, grad_decay, g_pna, g_pret, g_out, grad_final_g)
    delta_out = order(ds_[0], d_ada, ds_[1], ds_[2], d_w_in, ds_[3], ds_[4], d_pna, d_pret, d_out, ds_[5])
    m_out = order(nms_[0], nm_ada, nms_[1], nms_[2], nm_w_in, nms_[3], nms_[4], nm_pna, nm_pret, nm_out, nms_[5])
    v_out = order(nvs_[0], nv_ada, nvs_[1], nvs_[2], nv_w_in, nvs_[3], nvs_[4], nv_pna, nv_pret, nv_out, nvs_[5])
    return (loss, gx[None], *grads_out, *delta_out, *m_out, *v_out)
```

```python
import functools

import numpy as np
import jax
import jax.numpy as jnp
from jax import lax
from jax.experimental import pallas as pl
from jax.experimental.pallas import tpu as pltpu

F32 = jnp.float32
BF16 = jnp.bfloat16

N_DEV = 8
GRID_W = 64
NA_HEAD_DIM = 128
NA_WIN_ROWS = 8
NA_WIN_COLS = 16
RET_KEY_DIM = 128
RET_VAL_DIM = 256
RET_CHUNK = 128
ROPE_BASE = 10000.0
NORM_EPS = 1e-6
MASK_VALUE = -1e30

ADAM_LR = 0.001
ADAM_B1 = 0.9
ADAM_B2 = 0.999
ADAM_EPS = 1e-08
ADAM_WD = 0.01
ADAM_STEP = 10

VMEM_LIMIT = 48 * 1024 * 1024
MESH = pl.DeviceIdType.MESH
ANY = pl.BlockSpec(memory_space=pl.ANY)
VMEM_SPEC = pl.BlockSpec(memory_space=pltpu.VMEM)


def _params(sem=None):
    return pltpu.CompilerParams(dimension_semantics=sem, vmem_limit_bytes=VMEM_LIMIT)


def _pick(n, prefs):
    for p in prefs:
        if n % p == 0:
            return p
    return n


def _dot(a, b):
    return lax.dot_general(a, b, (((1,), (0,)), ((), ())), preferred_element_type=F32)


def _dot_nt(a, b):
    return lax.dot_general(a, b, (((1,), (1,)), ((), ())), preferred_element_type=F32)


def _dot_tn(a, b):
    return lax.dot_general(a, b, (((0,), (0,)), ((), ())), preferred_element_type=F32)


def _silu(x):
    return x * jax.nn.sigmoid(x)


def _matmul(a, b, *, trans_a=False, trans_b=False, out_dtype=F32, name="matmul", after=None):
    if trans_a:
        kdim, m = a.shape
    else:
        m, kdim = a.shape
    if trans_b:
        n, kb = b.shape
    else:
        kb, n = b.shape
    assert kdim == kb, (a.shape, b.shape, trans_a, trans_b)
    tm = _pick(m, (1152, 1024, 768, 512, 256, 128))
    tn = _pick(n, (512, 256, 128))
    tk = _pick(kdim, (2304, 2048, 1024, 512, 256, 128))
    nk = kdim // tk
    dn = (((0 if trans_a else 1,), (1 if trans_b else 0,)), ((), ()))

    def body(a_ref, b_ref, *rest):
        o_ref, acc_ref = rest[-2:]
        part = lax.dot_general(a_ref[...], b_ref[...], dn, preferred_element_type=F32)
        if nk == 1:
            o_ref[...] = part.astype(o_ref.dtype)
        else:
            k = pl.program_id(2)

            @pl.when(k == 0)
            def _():
                acc_ref[...] = part

            @pl.when(k > 0)
            def _():
                acc_ref[...] += part

            @pl.when(k == nk - 1)
            def _():
                o_ref[...] = acc_ref[...].astype(o_ref.dtype)

    a_spec = pl.BlockSpec((tk, tm), lambda i, j, k: (k, i)) if trans_a else pl.BlockSpec((tm, tk), lambda i, j, k: (i, k))
    b_spec = pl.BlockSpec((tn, tk), lambda i, j, k: (j, k)) if trans_b else pl.BlockSpec((tk, tn), lambda i, j, k: (k, j))
    return pl.pallas_call(
        body,
        name=name,
        grid=(m // tm, n // tn, nk),
        in_specs=[a_spec, b_spec] + ([] if after is None else [ANY]),
        out_specs=pl.BlockSpec((tm, tn), lambda i, j, k: (i, j)),
        out_shape=jax.ShapeDtypeStruct((m, n), out_dtype),
        scratch_shapes=[pltpu.VMEM((tm, tn) if nk > 1 else (8, 128), F32)],
        compiler_params=_params(("parallel", "parallel", "arbitrary")),
    )(*((a, b) if after is None else (a, b, after)))


def _make_rowwise(f, name, out_dtypes, out_cols, n_lat, tm, diff_rows):
    def tile_fn(*args):
        return tuple(o.astype(dt) for o, dt in zip(f(*args), out_dtypes))

    def fwd_call(rows, vecs):
        t = rows[0].shape[0]
        nr, nv = len(rows), len(vecs)
        nl = n_lat // tm

        def body(*refs):
            grp = (pl.program_id(0) >= nl).astype(jnp.int32)
            args = [r[...] for r in refs[:nr]] + [v[grp] for v in refs[nr:nr + nv]]
            for o_ref, o in zip(refs[nr + nv:], tile_fn(*args)):
                o_ref[...] = o

        return pl.pallas_call(
            body,
            name=name + "_fwd",
            grid=(t // tm,),
            in_specs=[pl.BlockSpec((tm, r.shape[1]), lambda i: (i, 0)) for r in rows]
            + [pl.BlockSpec(v.shape, lambda i: (0, 0, 0)) for v in vecs],
            out_specs=[pl.BlockSpec((tm, c), lambda i: (i, 0)) for c in out_cols],
            out_shape=[jax.ShapeDtypeStruct((t, c), dt) for c, dt in zip(out_cols, out_dtypes)],
            compiler_params=_params(("parallel",)),
        )(*rows, *vecs)

    def bwd_call(rows, vecs, gs):
        t = rows[0].shape[0]
        nr, nv, ng = len(rows), len(vecs), len(gs)
        nl = n_lat // tm
        nd = len(diff_rows)

        def body(*refs):
            i = pl.program_id(0)
            grp = (i >= nl).astype(jnp.int32)
            args = [r[...] for r in refs[:nr]] + [v[grp] for v in refs[nr:nr + nv]]
            g_refs = refs[nr + nv:nr + nv + ng]
            drow_refs = refs[nr + nv + ng:nr + nv + ng + nd]
            dvec_refs = refs[nr + nv + ng + nd:]
            _, vjp = jax.vjp(tile_fn, *args)
            grads = vjp(tuple(g[...] for g in g_refs))
            for d_ref, k in zip(drow_refs, diff_rows):
                d_ref[...] = grads[k].astype(d_ref.dtype)

            @pl.when(i == 0)
            def _():
                for d_ref in dvec_refs:
                    d_ref[...] = jnp.zeros_like(d_ref)

            for j, d_ref in enumerate(dvec_refs):
                d_ref[grp] += grads[nr + j]

        outs = pl.pallas_call(
            body,
            name=name + "_bwd",
            grid=(t // tm,),
            in_specs=[pl.BlockSpec((tm, r.shape[1]), lambda i: (i, 0)) for r in rows]
            + [pl.BlockSpec(v.shape, lambda i: (0, 0, 0)) for v in vecs]
            + [pl.BlockSpec((tm, g.shape[1]), lambda i: (i, 0)) for g in gs],
            out_specs=[pl.BlockSpec((tm, rows[k].shape[1]), lambda i: (i, 0)) for k in diff_rows]
            + [pl.BlockSpec(v.shape, lambda i: (0, 0, 0)) for v in vecs],
            out_shape=[jax.ShapeDtypeStruct(rows[k].shape, rows[k].dtype) for k in diff_rows]
            + [jax.ShapeDtypeStruct(v.shape, F32) for v in vecs],
            compiler_params=_params(("arbitrary",)),
        )(*rows, *vecs, *gs)
        return outs[:nd], outs[nd:]

    return fwd_call, bwd_call


def _f_norm_mod(x, g, scale, shift):
    r = lax.rsqrt(jnp.mean(x * x, axis=-1, keepdims=True) + NORM_EPS)
    return ((x * r * g) * (1.0 + scale) + shift,)


def _f_gate_na(o, z):
    return (o.astype(F32) * _silu(z.astype(F32)),)


def _f_merge(g_na, g_ret, y_na, y_ret):
    return (jax.nn.sigmoid(g_na.astype(F32)) * y_na + jax.nn.sigmoid(g_ret.astype(F32)) * y_ret,)


def _f_residual(x, out, gate):
    return (x + gate * out,)


def _f_loss(x, target, g):
    r = lax.rsqrt(jnp.mean(x * x, axis=-1, keepdims=True) + NORM_EPS)
    y = x * r * g
    e = 0.5 * jnp.mean(jnp.square(y - target), axis=-1, keepdims=True)
    return (jnp.broadcast_to(e * (1.0 / 128.0), (x.shape[0], 128)),)


def _gate_ret_fwd_call(of, ob, z, tm):
    t, w = of.shape
    nh = w // RET_VAL_DIM

    def body(of_ref, ob_ref, z_ref, a_ref):
        for hh in range(nh):
            sl = slice(hh * RET_VAL_DIM, (hh + 1) * RET_VAL_DIM)
            o = of_ref[:, sl] + ob_ref[:, sl]
            r = lax.rsqrt(jnp.mean(o * o, axis=-1, keepdims=True) + NORM_EPS)
            a_ref[:, sl] = ((o * r) * _silu(z_ref[:, sl].astype(F32))).astype(a_ref.dtype)

    spec = pl.BlockSpec((tm, w), lambda i: (i, 0))
    return pl.pallas_call(
        body, name="gate_ret_fwd", grid=(t // tm,), in_specs=[spec, spec, spec], out_specs=spec,
        out_shape=jax.ShapeDtypeStruct((t, w), BF16), compiler_params=_params(("parallel",)),
    )(of, ob, z)


def _gate_ret_bwd_call(of, ob, z, da, tm):
    t, w = of.shape
    nh = w // RET_VAL_DIM

    def body(of_ref, ob_ref, z_ref, da_ref, do_ref, dz_ref):
        for hh in range(nh):
            sl = slice(hh * RET_VAL_DIM, (hh + 1) * RET_VAL_DIM)
            o = of_ref[:, sl] + ob_ref[:, sl]
            r = lax.rsqrt(jnp.mean(o * o, axis=-1, keepdims=True) + NORM_EPS)
            n = o * r
            zf = z_ref[:, sl].astype(F32)
            sg = jax.nn.sigmoid(zf)
            g = da_ref[:, sl].astype(F32)
            dn = g * (zf * sg)
            dz_ref[:, sl] = (g * n * (sg * (1.0 + zf * (1.0 - sg)))).astype(dz_ref.dtype)
            do_ref[:, sl] = r * (dn - n * jnp.mean(dn * n, axis=-1, keepdims=True))

    spec = pl.BlockSpec((tm, w), lambda i: (i, 0))
    return pl.pallas_call(
        body, name="gate_ret_bwd", grid=(t // tm,), in_specs=[spec, spec, spec, spec], out_specs=[spec, spec],
        out_shape=[jax.ShapeDtypeStruct((t, w), F32), jax.ShapeDtypeStruct((t, w), z.dtype)],
        compiler_params=_params(("parallel",)),
    )(of, ob, z, da)


def _na_geometry(t, n_lat):
    rows = n_lat // GRID_W
    kh = min(NA_WIN_ROWS, rows)
    return rows, kh, kh * GRID_W, t - n_lat, t // GRID_W


def _na_row0(r, rows, kh):
    return jnp.clip(r - kh // 2, 0, rows - kh)


def _na_bias_idx(r, rows, kh):
    return jnp.clip(_na_row0(r, rows, kh) - r + (NA_WIN_ROWS - 1), 0, NA_WIN_ROWS - 1)


def _na_fwd_call(q, k, v, bt, n_lat):
    t, w = q.shape
    nh = w // NA_HEAD_DIM
    rows, kh, n_loc, n_ctx, nq = _na_geometry(t, n_lat)
    scale = NA_HEAD_DIM ** -0.5

    def body(q_ref, k_ref, v_ref, bt_ref, o_ref):
        r = pl.program_id(1)
        qb = q_ref[...]
        kc = k_ref[pl.ds(n_lat, n_ctx), :]
        vc = v_ref[pl.ds(n_lat, n_ctx), :]
        s_ctx = _dot_nt(qb, kc) * scale

        @pl.when(r < rows)
        def _():
            start = pl.multiple_of(_na_row0(r, rows, kh) * GRID_W, GRID_W)
            kw = k_ref[pl.ds(start, n_loc), :]
            vw = v_ref[pl.ds(start, n_loc), :]
            s_loc = _dot_nt(qb, kw) * scale + bt_ref[0, 0]
            m = jnp.maximum(jnp.max(s_loc, axis=-1, keepdims=True), jnp.max(s_ctx, axis=-1, keepdims=True))
            p_loc = jnp.exp(s_loc - m)
            p_ctx = jnp.exp(s_ctx - m)
            l = jnp.sum(p_loc, axis=-1, keepdims=True) + jnp.sum(p_ctx, axis=-1, keepdims=True)
            o = _dot(p_loc.astype(BF16), vw) + _dot(p_ctx.astype(BF16), vc)
            o_ref[...] = (o / l).astype(o_ref.dtype)

        @pl.when(r >= rows)
        def _():
            m = jnp.max(s_ctx, axis=-1, keepdims=True)
            p = jnp.exp(s_ctx - m)
            l = jnp.sum(p, axis=-1, keepdims=True)
            o_ref[...] = (_dot(p.astype(BF16), vc) / l).astype(o_ref.dtype)

    return pl.pallas_call(
        body,
        name="na_attn_fwd",
        grid=(nh, nq),
        in_specs=[
            pl.BlockSpec((GRID_W, NA_HEAD_DIM), lambda h, r: (r, h)),
            pl.BlockSpec((t, NA_HEAD_DIM), lambda h, r: (0, h)),
            pl.BlockSpec((t, NA_HEAD_DIM), lambda h, r: (0, h)),
            pl.BlockSpec((1, 1, GRID_W, n_loc), lambda h, r: (h, _na_bias_idx(r, rows, kh), 0, 0)),
        ],
        out_specs=pl.BlockSpec((GRID_W, NA_HEAD_DIM), lambda h, r: (r, h)),
        out_shape=jax.ShapeDtypeStruct((t, w), BF16),
        compiler_params=_params(("parallel", "arbitrary")),
    )(q, k, v, bt)


def _na_bwd_call(q, k, v, bt, do, n_lat):
    t, w = q.shape
    nh = w // NA_HEAD_DIM
    rows, kh, n_loc, n_ctx, nq = _na_geometry(t, n_lat)
    scale = NA_HEAD_DIM ** -0.5

    def body(q_ref, k_ref, v_ref, bt_ref, do_ref, dq_ref, dk_ref, dv_ref, dbt_ref):
        r = pl.program_id(1)

        @pl.when(r == 0)
        def _():
            dk_ref[...] = jnp.zeros_like(dk_ref)
            dv_ref[...] = jnp.zeros_like(dv_ref)

        qb = q_ref[...]
        dob = do_ref[...]
        kc = k_ref[pl.ds(n_lat, n_ctx), :]
        vc = v_ref[pl.ds(n_lat, n_ctx), :]
        s_ctx = _dot_nt(qb, kc) * scale
        dp_ctx = _dot_nt(dob, vc)

        @pl.when(r < rows)
        def _():
            start = pl.multiple_of(_na_row0(r, rows, kh) * GRID_W, GRID_W)
            kw = k_ref[pl.ds(start, n_loc), :]
            vw = v_ref[pl.ds(start, n_loc), :]
            s_loc = _dot_nt(qb, kw) * scale + bt_ref[0, 0]
            m = jnp.maximum(jnp.max(s_loc, axis=-1, keepdims=True), jnp.max(s_ctx, axis=-1, keepdims=True))
            p_loc = jnp.exp(s_loc - m)
            p_ctx = jnp.exp(s_ctx - m)
            inv = 1.0 / (jnp.sum(p_loc, axis=-1, keepdims=True) + jnp.sum(p_ctx, axis=-1, keepdims=True))
            p_loc = p_loc * inv
            p_ctx = p_ctx * inv
            dp_loc = _dot_nt(dob, vw)
            delta = jnp.sum(p_loc * dp_loc, axis=-1, keepdims=True) + jnp.sum(p_ctx * dp_ctx, axis=-1, keepdims=True)
            ds_loc = p_loc * (dp_loc - delta)
            ds_ctx = p_ctx * (dp_ctx - delta)
            first = jnp.logical_or(r == 0, _na_bias_idx(r, rows, kh) != _na_bias_idx(r - 1, rows, kh))

            @pl.when(first)
            def _():
                dbt_ref[0, 0] = ds_loc

            @pl.when(jnp.logical_not(first))
            def _():
                dbt_ref[0, 0] += ds_loc

            dsl = (ds_loc * scale).astype(BF16)
            dsc = (ds_ctx * scale).astype(BF16)
            dq_ref[...] = (_dot(dsl, kw) + _dot(dsc, kc)).astype(dq_ref.dtype)
            dk_ref[pl.ds(start, n_loc), :] += _dot_tn(dsl, qb)
            dv_ref[pl.ds(start, n_loc), :] += _dot_tn(p_loc.astype(BF16), dob)
            dk_ref[pl.ds(n_lat, n_ctx), :] += _dot_tn(dsc, qb)
            dv_ref[pl.ds(n_lat, n_ctx), :] += _dot_tn(p_ctx.astype(BF16), dob)

        @pl.when(r >= rows)
        def _():
            m = jnp.max(s_ctx, axis=-1, keepdims=True)
            p = jnp.exp(s_ctx - m)
            p = p * (1.0 / jnp.sum(p, axis=-1, keepdims=True))
            delta = jnp.sum(p * dp_ctx, axis=-1, keepdims=True)
            dsc = (p * (dp_ctx - delta) * scale).astype(BF16)
            dq_ref[...] = _dot(dsc, kc).astype(dq_ref.dtype)
            dk_ref[pl.ds(n_lat, n_ctx), :] += _dot_tn(dsc, qb)
            dv_ref[pl.ds(n_lat, n_ctx), :] += _dot_tn(p.astype(BF16), dob)

    qspec = pl.BlockSpec((GRID_W, NA_HEAD_DIM), lambda h, r: (r, h))
    kspec = pl.BlockSpec((t, NA_HEAD_DIM), lambda h, r: (0, h))
    bspec = pl.BlockSpec((1, 1, GRID_W, n_loc), lambda h, r: (h, _na_bias_idx(r, rows, kh), 0, 0))
    return pl.pallas_call(
        body,
        name="na_attn_bwd",
        grid=(nh, nq),
        in_specs=[qspec, kspec, kspec, bspec, qspec],
        out_specs=[qspec, kspec, kspec, bspec],
        out_shape=[
            jax.ShapeDtypeStruct((t, w), BF16),
            jax.ShapeDtypeStruct((t, w), F32),
            jax.ShapeDtypeStruct((t, w), F32),
            jax.ShapeDtypeStruct(bt.shape, F32),
        ],
        compiler_params=_params(("parallel", "arbitrary")),
    )(q, k, v, bt, do)


def _na_bias_table(rpb, rows):
    kh = min(NA_WIN_ROWS, rows)
    nj = NA_WIN_ROWS
    e1 = np.zeros((nj, kh, 2 * NA_WIN_ROWS - 1), np.float32)
    for j in range(nj):
        for kk in range(kh):
            if j + kk < 2 * NA_WIN_ROWS - 1:
                e1[j, kk, j + kk] = 1.0
    cidx = np.arange(GRID_W)
    dc = np.clip(cidx[None, :] - cidx[:, None] + (NA_WIN_COLS - 1), 0, 2 * NA_WIN_COLS - 2)
    e2 = np.zeros((GRID_W, GRID_W, 2 * NA_WIN_COLS - 1), np.float32)
    e2[np.arange(GRID_W)[:, None], np.arange(GRID_W)[None, :], dc] = 1.0
    c0 = np.clip(cidx - NA_WIN_COLS // 2, 0, GRID_W - NA_WIN_COLS)
    col_in = (cidx[None, :] >= c0[:, None]) & (cidx[None, :] < c0[:, None] + NA_WIN_COLS)
    t1 = jnp.einsum("hab,jka->hjkb", rpb, jnp.asarray(e1), precision=lax.Precision.HIGHEST)
    b = jnp.einsum("hjkb,cwb->hjckw", t1, jnp.asarray(e2), precision=lax.Precision.HIGHEST)
    b = jnp.where(jnp.asarray(col_in)[None, None, :, None, :], b, MASK_VALUE)
    return b.reshape(rpb.shape[0], nj, GRID_W, kh * GRID_W)


def _ret_decays(lam_s, reverse):
    c = RET_CHUNK
    ii = lax.broadcasted_iota(jnp.int32, (c, c), 0)
    jj = lax.broadcasted_iota(jnp.int32, (c, c), 1)
    d = (jj - ii) if reverse else (ii - jj)
    dpos = jnp.maximum(d.astype(F32), 0.0)
    mask = jnp.where(d >= 0, jnp.exp(dpos * lam_s), 0.0)
    pi = lax.broadcasted_iota(jnp.int32, (c, 1), 0).astype(F32)
    qpos = (c - pi) if reverse else (pi + 1.0)
    kpos = pi if reverse else (c - 1.0 - pi)
    qd = jnp.exp(qpos * lam_s)
    kd = jnp.exp(kpos * lam_s)
    g = jnp.exp(jnp.full((1, RET_VAL_DIM), c * lam_s, F32))
    return mask, dpos, qd, kd, qpos, kpos, g


def _ret_chunk_of(t, nt, nl, reverse):
    return (nt - 1 - t) if reverse else (t + nl) % nt


def _ret_fwd_call(qr, kr, v, lam, n_lat, reverse):
    t = qr.shape[0]
    nh = qr.shape[1] // RET_KEY_DIM
    c = RET_CHUNK
    nt, nl = t // c, n_lat // c

    def body(lam_ref, q_ref, k_ref, v_ref, o_ref, s_ref, state):
        h, step = pl.program_id(0), pl.program_id(1)

        @pl.when(step == 0)
        def _():
            state[...] = jnp.zeros_like(state)

        mask, _, qd, kd, _, _, g = _ret_decays(lam_ref[h], reverse)
        q, k, vv = q_ref[...], k_ref[...], v_ref[...]
        p = _dot_nt(q, k) * mask
        s = state[...]
        qs = (q.astype(F32) * qd).astype(BF16)
        o_ref[...] = _dot(p.astype(BF16), vv) + _dot(qs, s.astype(BF16))
        s_ref[0, 0] = s
        ks = (k.astype(F32) * kd).astype(BF16)
        state[...] = s * g + _dot_tn(ks, vv)

    def cmap(h, step, lam_ref):
        return (_ret_chunk_of(step, nt, nl, reverse), h)

    return pl.pallas_call(
        body,
        name="retention_rev_fwd" if reverse else "retention_fwd",
        grid_spec=pltpu.PrefetchScalarGridSpec(
            num_scalar_prefetch=1,
            grid=(nh, nt),
            in_specs=[
                pl.BlockSpec((c, RET_KEY_DIM), cmap),
                pl.BlockSpec((c, RET_KEY_DIM), cmap),
                pl.BlockSpec((c, RET_VAL_DIM), cmap),
            ],
            out_specs=[
                pl.BlockSpec((c, RET_VAL_DIM), cmap),
                pl.BlockSpec((1, 1, RET_KEY_DIM, RET_VAL_DIM), lambda h, step, lam_ref: (h, step, 0, 0)),
            ],
            scratch_shapes=[pltpu.VMEM((RET_KEY_DIM, RET_VAL_DIM), F32)],
        ),
        out_shape=[
            jax.ShapeDtypeStruct((t, nh * RET_VAL_DIM), F32),
            jax.ShapeDtypeStruct((nh, nt, RET_KEY_DIM, RET_VAL_DIM), F32),
        ],
        compiler_params=_params(("parallel", "arbitrary")),
    )(lam, qr, kr, v)


def _ret_bwd_call(qr, kr, v, lam, states, do, n_lat, reverse):
    t = qr.shape[0]
    nh = qr.shape[1] // RET_KEY_DIM
    c = RET_CHUNK
    nt, nl = t // c, n_lat // c

    def body(lam_ref, q_ref, k_ref, v_ref, s_ref, do_ref, dq_ref, dk_ref, dv_ref, dl_ref, dstate):
        h, rstep = pl.program_id(0), pl.program_id(1)

        @pl.when(rstep == 0)
        def _():
            dstate[...] = jnp.zeros_like(dstate)
            dl_ref[...] = jnp.zeros_like(dl_ref)

        mask, dpos, qd, kd, qpos, kpos, g = _ret_decays(lam_ref[h], reverse)
        q, k, vv = q_ref[...], k_ref[...], v_ref[...]
        qf, kf = q.astype(F32), k.astype(F32)
        s = s_ref[0, 0]
        ds = dstate[...]
        dob = do_ref[...].astype(BF16)
        sb, dsb = s.astype(BF16), ds.astype(BF16)
        a = _dot_nt(q, k)
        p = a * mask
        dp = _dot_nt(dob, vv)
        da = dp * mask
        dab = da.astype(BF16)
        dqc = _dot_nt(dob, sb)
        dkc = _dot_nt(vv, dsb)
        qs = (qf * qd).astype(BF16)
        ks = (kf * kd).astype(BF16)
        dq_ref[...] = (_dot(dab, k) + dqc * qd).astype(dq_ref.dtype)
        dk_ref[...] = (_dot_tn(dab, q) + dkc * kd).astype(dk_ref.dtype)
        dv_ref[...] = (_dot_tn(p.astype(BF16), dob) + _dot(ks, dsb)).astype(dv_ref.dtype)
        terms = (
            jnp.sum(jnp.sum(da * a * dpos, axis=1, keepdims=True), axis=0, keepdims=True)
            + jnp.sum(jnp.sum(dqc * qf * (qd * qpos), axis=1, keepdims=True), axis=0, keepdims=True)
            + jnp.sum(jnp.sum(dkc * kf * (kd * kpos), axis=1, keepdims=True), axis=0, keepdims=True)
            + jnp.sum(jnp.sum(ds * s * (g * c), axis=1, keepdims=True), axis=0, keepdims=True)
        )
        dl_ref[0] += jnp.broadcast_to(terms, (8, 128))
        dstate[...] = ds * g + _dot_tn(qs, dob)

    def cmap(h, rstep, lam_ref):
        return (_ret_chunk_of(nt - 1 - rstep, nt, nl, reverse), h)

    return pl.pallas_call(
        body,
        name="retention_rev_bwd" if reverse else "retention_bwd",
        grid_spec=pltpu.PrefetchScalarGridSpec(
            num_scalar_prefetch=1,
            grid=(nh, nt),
            in_specs=[
                pl.BlockSpec((c, RET_KEY_DIM), cmap),
                pl.BlockSpec((c, RET_KEY_DIM), cmap),
                pl.BlockSpec((c, RET_VAL_DIM), cmap),
                pl.BlockSpec((1, 1, RET_KEY_DIM, RET_VAL_DIM), lambda h, rstep, lam_ref: (h, nt - 1 - rstep, 0, 0)),
                pl.BlockSpec((c, RET_VAL_DIM), cmap),
            ],
            out_specs=[
                pl.BlockSpec((c, RET_KEY_DIM), cmap),
                pl.BlockSpec((c, RET_KEY_DIM), cmap),
                pl.BlockSpec((c, RET_VAL_DIM), cmap),
                pl.BlockSpec((1, 8, 128), lambda h, rstep, lam_ref: (h, 0, 0)),
            ],
            scratch_shapes=[pltpu.VMEM((RET_KEY_DIM, RET_VAL_DIM), F32)],
        ),
        out_shape=[
            jax.ShapeDtypeStruct(qr.shape, qr.dtype),
            jax.ShapeDtypeStruct(kr.shape, kr.dtype),
            jax.ShapeDtypeStruct(v.shape, v.dtype),
            jax.ShapeDtypeStruct((nh, 8, 128), F32),
        ],
        compiler_params=_params(("parallel", "arbitrary")),
    )(lam, qr, kr, v, states, do)


def _rope_tables(t, n_lat):
    nf = RET_KEY_DIM // 4
    tok = np.arange(n_lat)
    inv_freq = (ROPE_BASE ** (-np.arange(nf, dtype=np.float32) / nf)).astype(np.float32)
    row = (tok // GRID_W).astype(np.float32)
    col = (tok % GRID_W).astype(np.float32)
    ang = np.concatenate([row[:, None] * inv_freq, col[:, None] * inv_freq], axis=-1).astype(np.float32)
    cos = np.ones((t, 2 * nf), np.float32)
    sin = np.zeros((t, 2 * nf), np.float32)
    cos[:n_lat] = np.cos(ang)
    sin[:n_lat] = np.sin(ang)
    return jnp.asarray(cos), jnp.asarray(sin)


def _rope(xb, cos, sin, mult):
    t, w = xb.shape
    nh = w // RET_KEY_DIM
    half = RET_KEY_DIM // 2
    x = xb.astype(F32).reshape(t, nh, 2, half)
    x1, x2 = x[:, :, 0], x[:, :, 1]
    c, s = cos[:, None, :], sin[:, None, :]
    out = jnp.stack([x1 * c - x2 * s, x2 * c + x1 * s], axis=2) * mult
    return out.reshape(t, w).astype(BF16)


def _my_position():
    return lax.axis_index("x"), lax.axis_index("y"), lax.axis_index("c")


def _flip(pos, k):
    x, y, c = pos
    return (1 - x if k & 4 else x, 1 - y if k & 2 else y, 1 - c if k & 1 else c)


def _linear(pos):
    return 4 * pos[0] + 2 * pos[1] + pos[2]


def _slab(ref, axis, idx, size):
    start = pl.multiple_of(idx * size, size)
    return ref.at[pl.ds(start, size), :] if axis == 0 else ref.at[:, pl.ds(start, size)]


HBM_SPEC = pl.BlockSpec(memory_space=pltpu.HBM)
SEM_SPEC = pl.BlockSpec(memory_space=pltpu.SEMAPHORE)
DATAFLOW = pltpu.SideEffectType.DATAFLOW_SIDE_EFFECTING
PEER_BITS = (1, 2, 4, 6, 3, 5, 7)


def _in_hbm(a):
    return pltpu.with_memory_space_constraint(a, pltpu.HBM)


def _gather_views(me, k, a, src_refs, land_refs, axes):
    size = src_refs[a].shape[axes[a]]
    peer = _flip(me, k)
    return src_refs[a], _slab(land_refs[a], axes[a], _linear(me), size), _slab(land_refs[a], axes[a], _linear(peer), size)


def _scatter_views(me, k, a, src_refs, land_refs, axes):
    size = land_refs[a].shape[1 + axes[a]]
    peer = _flip(me, k)
    return _slab(src_refs[a], axes[a], _linear(peer), size), land_refs[a].at[k], land_refs[a].at[k]


def _place_local(name, srcs, lands, axes, gather):
    ns = len(srcs)

    def body(*refs):
        src_refs, land_refs, sems = refs[:ns], refs[ns:2 * ns], refs[-1]
        me = _my_position()
        copies = []
        for a in range(ns):
            if gather:
                s, d = src_refs[a], _slab(land_refs[a], axes[a], _linear(me), src_refs[a].shape[axes[a]])
            else:
                s, d = _slab(src_refs[a], axes[a], _linear(me), land_refs[a].shape[1 + axes[a]]), land_refs[a].at[0]
            copies.append(pltpu.make_async_copy(s, d, sems.at[a]))
        for cp in copies:
            cp.start()
        for cp in copies:
            cp.wait()

    return pl.pallas_call(
        body, name=name, in_specs=[ANY] * (2 * ns), out_specs=[ANY] * ns,
        out_shape=[jax.ShapeDtypeStruct(l.shape, l.dtype) for l in lands],
        input_output_aliases={ns + a: a for a in range(ns)},
        scratch_shapes=[pltpu.SemaphoreType.DMA((ns,))],
        compiler_params=pltpu.CompilerParams(has_side_effects=True),
    )(*srcs, *lands)


def _push_start(name, srcs, lands, axes, views, deps):
    ns = len(srcs)

    def body(*refs):
        src_refs, land_refs = refs[:ns], refs[ns:2 * ns]
        send_sems, recv_sems = refs[2 * ns + len(deps):2 * ns + len(deps) + 2]
        token = refs[-1]
        me = _my_position()
        for k in PEER_BITS:
            for a in range(ns):
                s, d, _ = views(me, k, a, src_refs, land_refs, axes)
                pltpu.make_async_remote_copy(
                    src_ref=s, dst_ref=d, send_sem=send_sems.at[7 * a + k - 1], recv_sem=recv_sems.at[7 * a + k - 1],
                    device_id=_flip(me, k), device_id_type=MESH).start()
        token[...] = jnp.zeros_like(token)

    thru = [pltpu.HBM(a.shape, a.dtype) for a in list(srcs) + list(lands)]
    outs = pl.pallas_call(
        body, name=name,
        in_specs=[HBM_SPEC] * (2 * ns) + [ANY] * len(deps),
        out_specs=[SEM_SPEC, SEM_SPEC] + [HBM_SPEC] * (2 * ns) + [VMEM_SPEC],
        out_shape=[pltpu.SemaphoreType.DMA((7 * ns,)), pltpu.SemaphoreType.DMA((7 * ns,))] + thru
        + [jax.ShapeDtypeStruct((8, 128), F32)],
        input_output_aliases={i: 2 + i for i in range(2 * ns)},
        compiler_params=pltpu.CompilerParams(has_side_effects=DATAFLOW),
    )(*[_in_hbm(a) for a in srcs], *[_in_hbm(a) for a in lands], *deps)
    return (outs[0], outs[1]), outs[2:2 + ns], outs[2 + ns:2 + 2 * ns], outs[-1]


def _push_wait(name, sems, srcs, lands, axes, views, after):
    ns = len(srcs)

    def body(*refs):
        src_refs, land_refs = refs[:ns], refs[ns:2 * ns]
        send_sems, recv_sems = refs[2 * ns:2 * ns + 2]
        me = _my_position()
        for k in PEER_BITS:
            for a in range(ns):
                s, d, got = views(me, k, a, src_refs, land_refs, axes)
                cp = pltpu.make_async_remote_copy(
                    src_ref=s, dst_ref=got, send_sem=send_sems.at[7 * a + k - 1], recv_sem=recv_sems.at[7 * a + k - 1],
                    device_id=_flip(me, k), device_id_type=MESH)
                cp.wait_send()
                cp.wait_recv()

    thru = [pltpu.HBM(a.shape, a.dtype) for a in list(srcs) + list(lands)]
    outs = pl.pallas_call(
        body, name=name,
        in_specs=[HBM_SPEC] * (2 * ns) + [SEM_SPEC, SEM_SPEC, ANY],
        out_specs=[HBM_SPEC] * (2 * ns),
        out_shape=thru,
        input_output_aliases={i: i for i in range(2 * ns)},
        compiler_params=pltpu.CompilerParams(has_side_effects=DATAFLOW),
    )(*srcs, *lands, sems[0], sems[1], after)
    return outs[ns:]


def _small_allgather(v, name):
    r, c = v.shape

    def body(v_ref, all_ref, sum_ref, send_sems, recv_sems):
        me = _my_position()
        all_ref[_linear(me)] = v_ref[...]
        copies = []
        for k in range(1, N_DEV):
            peer = _flip(me, k)
            copies.append(pltpu.make_async_remote_copy(
                src_ref=v_ref, dst_ref=all_ref.at[_linear(me)], send_sem=send_sems.at[k - 1], recv_sem=recv_sems.at[k - 1],
                device_id=peer, device_id_type=MESH))
        for cp in copies:
            cp.start()
        for k in range(1, N_DEV):
            peer = _flip(me, k)
            pltpu.make_async_remote_copy(
                src_ref=v_ref, dst_ref=all_ref.at[_linear(peer)], send_sem=send_sems.at[k - 1], recv_sem=recv_sems.at[k - 1],
                device_id=peer, device_id_type=MESH).wait_recv()
        for cp in copies:
            cp.wait_send()
        acc = all_ref[0]
        for d in range(1, N_DEV):
            acc = acc + all_ref[d]
        sum_ref[...] = acc

    return pl.pallas_call(
        body,
        name=name,
        in_specs=[VMEM_SPEC],
        out_specs=[VMEM_SPEC, VMEM_SPEC],
        out_shape=[jax.ShapeDtypeStruct((N_DEV, r, c), F32), jax.ShapeDtypeStruct((r, c), F32)],
        scratch_shapes=[pltpu.SemaphoreType.DMA((N_DEV - 1,)), pltpu.SemaphoreType.DMA((N_DEV - 1,))],
        compiler_params=pltpu.CompilerParams(has_side_effects=True, vmem_limit_bytes=VMEM_LIMIT),
    )(v)


def _ada_fwd_call(cin, ada_w, ada_b_cols):
    nl, d, ncol = ada_w.shape
    nrow = cin.shape[0]

    def body(c_ref, w_ref, b_ref, o_ref):
        cs = _silu(c_ref[...]).astype(BF16)
        for l in range(nl):
            o_ref[l] = _dot(cs, w_ref[l].astype(BF16)) + b_ref[l]

    return pl.pallas_call(
        body, name="ada_fwd", in_specs=[VMEM_SPEC] * 3, out_specs=VMEM_SPEC,
        out_shape=jax.ShapeDtypeStruct((nl, nrow, ncol), F32), compiler_params=_params(),
    )(cin, ada_w, ada_b_cols)


def _ada_bwd_call(cin, ada_w, dmod):
    nl, d, ncol = ada_w.shape
    nrow = cin.shape[0]

    def body(c_ref, w_ref, dm_ref, gw_ref, dcs_ref):
        cs = _silu(c_ref[...]).astype(BF16)
        acc = jnp.zeros((nrow, d), F32)
        for l in range(nl):
            dm = dm_ref[l].astype(BF16)
            gw_ref[l] = _dot_tn(cs, dm)
            acc = acc + _dot_nt(dm, w_ref[l].astype(BF16))
        dcs_ref[...] = acc

    return pl.pallas_call(
        body, name="ada_bwd", in_specs=[VMEM_SPEC] * 3, out_specs=[VMEM_SPEC, VMEM_SPEC],
        out_shape=[jax.ShapeDtypeStruct((nl, d, ncol), F32), jax.ShapeDtypeStruct((nrow, d), F32)],
        compiler_params=_params(),
    )(cin, ada_w, dmod)


def _adamw_math(w, g, m, v):
    m = ADAM_B1 * m + (1.0 - ADAM_B1) * g
    v = ADAM_B2 * v + (1.0 - ADAM_B2) * jnp.square(g)
    m_hat = m / (1.0 - ADAM_B1 ** ADAM_STEP)
    v_hat = v / (1.0 - ADAM_B2 ** ADAM_STEP)
    delta = -ADAM_LR * (m_hat / (jnp.sqrt(v_hat) + ADAM_EPS) + ADAM_WD * w)
    return delta, m, v


def _adamw_sharded(w, m, v, slabs, layer, prev, name):
    nl, r, c = w.shape
    tm = _pick(r, (128, 64, 32, 16))
    nprev = 0 if prev is None else len(prev)

    def body(w_ref, m_ref, v_ref, s_ref, *rest):
        g_ref, d_ref, nm_ref, nv_ref = rest[nprev:]
        g = s_ref[0].astype(F32)
        for k in range(1, N_DEV):
            g = g + s_ref[k].astype(F32)
        delta, nm, nv = _adamw_math(w_ref[0], g, m_ref[0], v_ref[0])
        g_ref[0], d_ref[0], nm_ref[0], nv_ref[0] = g, delta, nm, nv

    spec = pl.BlockSpec((1, tm, c), lambda i: (layer, i, 0))
    out = jax.ShapeDtypeStruct(w.shape, F32)
    return pl.pallas_call(
        body, name=name, grid=(r // tm,),
        in_specs=[spec, spec, spec, pl.BlockSpec((N_DEV, tm, c), lambda i: (0, i, 0))] + [ANY] * nprev,
        out_specs=[spec] * 4, out_shape=[out] * 4,
        input_output_aliases={4 + j: j for j in range(nprev)},
        compiler_params=_params(("parallel",)),
    )(w, m, v, slabs, *(() if prev is None else prev))


def _adamw_dense(w, g, m, v, name):
    r, c = w.shape
    tm = _pick(r, (256, 128, 64, 32, 16, 8))

    def body(w_ref, g_ref, m_ref, v_ref, d_ref, nm_ref, nv_ref):
        d_ref[...], nm_ref[...], nv_ref[...] = _adamw_math(w_ref[...], g_ref[...], m_ref[...], v_ref[...])

    spec = pl.BlockSpec((tm, c), lambda i: (i, 0))
    out = jax.ShapeDtypeStruct(w.shape, F32)
    return pl.pallas_call(
        body, name=name, grid=(r // tm,), in_specs=[spec] * 4, out_specs=[spec] * 3, out_shape=[out] * 3,
        compiler_params=_params(("parallel",)),
    )(w, g, m, v)


def _pack(parts, width=128):
    flat = jnp.concatenate([p.reshape(-1).astype(F32) for p in parts])
    n = flat.shape[0]
    total = -(-n // (8 * width)) * (8 * width)
    return jnp.pad(flat, (0, total - n)).reshape(total // width, width)


def _unpack(buf, shapes):
    flat = buf.reshape(-1)
    out, off = [], 0
    for s in shapes:
        n = int(np.prod(s))
        out.append(flat[off:off + n].reshape(s))
        off += n
    return out


def kernel(x, c, ctx, c_ctx, ada_w, ada_b, norm_g, w_in, na_rpb, ret_decay_logit, w_proj_na, w_proj_ret, w_out, final_g, loss_target, m_c_ctx, m_ada_w, m_ada_b, m_norm_g, m_w_in, m_na_rpb, m_ret_decay_logit, m_w_proj_na, m_w_proj_ret, m_w_out, m_final_g, v_c_ctx, v_ada_w, v_ada_b, v_norm_g, v_w_in, v_na_rpb, v_ret_decay_logit, v_w_proj_na, v_w_proj_ret, v_w_out, v_final_g):
    depth = w_in.shape[0]
    n_lat, d = x.shape[1], x.shape[2]
    n_ctx = ctx.shape[1]
    t = n_lat + n_ctx
    w_na = w_proj_na.shape[1]
    w_retv = w_proj_ret.shape[1] * N_DEV
    in_cols = w_in.shape[2] * N_DEV
    w_qk = (in_cols - 4 * w_na - 2 * w_retv - 2 * d) // 2
    sizes = (w_na, w_na, w_na, w_na, w_qk, w_qk, w_retv, w_retv, d, d)
    off = tuple(int(o) for o in np.cumsum((0,) + sizes))
    NA_Q, NA_K, NA_V, NA_Z, RET_Q, RET_K, RET_V, RET_Z, G_NA, G_RET = range(10)
    rows = n_lat // GRID_W
    me = _my_position()
    my_idx = _linear(me)
    tm_row = _pick(n_ctx, (256, 128))

    w_axes = (1, 1, 0, 0)
    w_names = ("w_in", "w_proj_na", "w_proj_ret", "w_out")
    shard = [[w[l].astype(BF16) for w in (w_in, w_proj_na, w_proj_ret, w_out)] for l in range(depth)]
    groups = [[(0, 0)], [(0, 1), (0, 2), (0, 3)]] + [[(l, a) for a in range(4)] for l in range(1, depth)]
    gathers, token = {}, None
    for gi, keys in enumerate(groups):
        srcs = [shard[l][a] for l, a in keys]
        axes = tuple(w_axes[a] for _, a in keys)
        lands = [lax.empty(tuple(n * (N_DEV if i == ax else 1) for i, n in enumerate(s.shape)), BF16)
                 for s, ax in zip(srcs, axes)]
        lands = _place_local("gather_place", srcs, lands, axes, True)
        sems, srcs, lands, token = _push_start(
            f"gather_start_{gi}", srcs, lands, axes, _gather_views, () if token is None else (token,))
        flight = dict(name=f"gather_wait_{gi}", sems=sems, srcs=srcs, lands=lands, axes=axes, ready=None)
        for pos, key in enumerate(keys):
            gathers[key] = (flight, pos)

    def landed(l, a, act):
        flight, pos = gathers[(l, a)]
        if flight["ready"] is None:
            flight["ready"] = _push_wait(flight["name"], flight["sems"], flight["srcs"], flight["lands"],
                                         flight["axes"], _gather_views, act)
        return flight["ready"][pos]

    pending, scatters = {}, []

    def send_dw(l, a, dw):
        pending[(l, a)] = dw
        if a == 0:
            keys = [(0, 0)] if l == 0 else [(l, b) for b in range(4)]
        elif l == 0 and a == 1:
            keys = [(0, 1), (0, 2), (0, 3)]
        else:
            return None
        srcs = [pending[k] for k in keys]
        axes = tuple(w_axes[b] for _, b in keys)
        lands = [lax.empty((N_DEV,) + tuple(n // (N_DEV if i == ax else 1) for i, n in enumerate(s.shape)), BF16)
                 for s, ax in zip(srcs, axes)]
        lands = _place_local("scatter_place", srcs, lands, axes, False)
        sems, srcs, lands, tok = _push_start(f"scatter_start_{len(scatters)}", srcs, lands, axes, _scatter_views, ())
        scatters.append(dict(name=f"scatter_wait_{len(scatters)}", sems=sems, srcs=srcs, lands=lands, axes=axes, keys=keys))
        return tok

    ncol = ada_w.shape[2]
    c_all, _ = _small_allgather(jnp.pad(c, ((0, 7), (0, 0))) + token[:, :1], "allgather_c")
    cin = jnp.concatenate([c_all[:, 0, :], c_ctx[None, :], jnp.zeros((7, d), F32)], axis=0)
    ada_b_cols = lax.dynamic_slice_in_dim(ada_b, my_idx * ncol, ncol, axis=1)[:, None, :]
    mod_cols = _ada_fwd_call(cin, ada_w, ada_b_cols)
    mod_all, _ = _small_allgather(mod_cols.reshape(depth * 16, ncol), "allgather_mod")
    mod_all = mod_all.reshape(N_DEV, depth, 16, ncol).transpose(1, 2, 0, 3).reshape(depth, 16, N_DEV * ncol)
    mod_lat = lax.dynamic_index_in_dim(mod_all, my_idx, axis=1, keepdims=False)
    mod_ctx = mod_all[:, 8, :]

    cos, sin = _rope_tables(t, n_lat)
    k_scale = RET_KEY_DIM ** -0.5
    norm_mod_fwd, norm_mod_bwd = _make_rowwise(_f_norm_mod, "norm_mod", (BF16,), (d,), n_lat, tm_row, (0,))
    gate_na_fwd, gate_na_bwd = _make_rowwise(_f_gate_na, "gate_na", (BF16,), (w_na,), n_lat, tm_row, (0, 1))
    merge_fwd, merge_bwd = _make_rowwise(_f_merge, "merge", (BF16,), (d,), n_lat, tm_row, (0, 1, 2, 3))
    residual_fwd, residual_bwd = _make_rowwise(_f_residual, "residual", (F32,), (d,), n_lat, tm_row, (0, 1))
    loss_fwd, loss_bwd = _make_rowwise(_f_loss, "loss_head", (F32,), (128,), n_lat, tm_row, (0,))

    def pair(a, b):
        return jnp.stack([a, b])[:, None, :]

    def mod_vectors(mod_lat_l, mod_ctx_l, norm_g_l):
        shift, scale, gate = jnp.split(mod_lat_l, 3)
        c_shift, c_scale, c_gate = jnp.split(mod_ctx_l, 3)
        return pair(norm_g_l, norm_g_l), pair(scale, c_scale), pair(shift, c_shift), pair(gate, c_gate)

    def split_u(u):
        blk = [u[:, off[i]:off[i + 1]] for i in range(10)]
        blk[RET_Q] = _rope(blk[RET_Q], cos, sin, 1.0)
        blk[RET_K] = _rope(blk[RET_K], cos, sin, k_scale)
        return tuple(blk)

    def log_decay(logit):
        return jax.nn.log_sigmoid(logit.astype(F32))

    xa = jnp.concatenate([x[0], ctx[0]], axis=0)
    saved = []
    for l in range(depth):
        vecs, vecs_vjp = jax.vjp(mod_vectors, mod_lat[l], mod_ctx[l], norm_g[l])
        (h,) = norm_mod_fwd((xa,), vecs[:3])
        wl_in = landed(l, 0, h)
        u = _matmul(h, wl_in, out_dtype=BF16, name="in_proj_fwd")
        blk, split_vjp = jax.vjp(split_u, u)
        bt, bt_vjp = jax.vjp(lambda r: _na_bias_table(r, rows), na_rpb[l])
        lam, lam_vjp = jax.vjp(log_decay, ret_decay_logit[l])
        o_na = _na_fwd_call(blk[NA_Q], blk[NA_K], blk[NA_V], bt, n_lat)
        o_f, st_f = _ret_fwd_call(blk[RET_Q], blk[RET_K], blk[RET_V], lam[0], n_lat, False)
        o_b, st_b = _ret_fwd_call(blk[RET_Q], blk[RET_K], blk[RET_V], lam[1], n_lat, True)
        (a_na,) = gate_na_fwd((o_na, blk[NA_Z]), ())
        a_ret = _gate_ret_fwd_call(o_f, o_b, blk[RET_Z], tm_row)
        wl_pna, wl_pret, wl_out = landed(l, 1, a_na), landed(l, 2, a_na), landed(l, 3, a_na)
        y_na = _matmul(a_na, wl_pna, out_dtype=F32, name="proj_na_fwd")
        y_ret = _matmul(a_ret, wl_pret, out_dtype=F32, name="proj_ret_fwd")
        (merged,) = merge_fwd((blk[G_NA], blk[G_RET], y_na, y_ret), ())
        out = _matmul(merged, wl_out, out_dtype=F32, name="out_proj_fwd")
        (xa_next,) = residual_fwd((xa, out), vecs[3:])
        saved.append(dict(xa=xa, vecs=vecs, vecs_vjp=vecs_vjp, h=h, w=(wl_in, wl_pna, wl_pret, wl_out), blk=blk,
                          split_vjp=split_vjp, bt=bt, bt_vjp=bt_vjp, lam=lam, lam_vjp=lam_vjp, o_na=o_na, o_f=o_f,
                          o_b=o_b, st_f=st_f, st_b=st_b, a_na=a_na, a_ret=a_ret, y_na=y_na, y_ret=y_ret,
                          merged=merged, out=out))
        xa = xa_next

    fg_pair, fg_vjp = jax.vjp(lambda g: pair(g, g), final_g)
    x_last = xa[:n_lat]
    (loss_rows,) = loss_fwd((x_last, loss_target[0]), (fg_pair,))
    loss = lax.psum(jnp.sum(loss_rows), ("x", "y", "c"))
    (dx_last,), (d_fg_pair,) = loss_bwd((x_last, loss_target[0]), (fg_pair,), (jnp.ones_like(loss_rows),))
    (d_final_g,) = fg_vjp(d_fg_pair)
    dxa = jnp.pad(dx_last, ((0, n_ctx), (0, 0)))

    d_mod_lat, d_mod_ctx, d_norm_g, d_rpb, d_decay = ([None] * depth for _ in range(5))
    for l in reversed(range(depth)):
        s = saved[l]
        blk = s["blk"]
        wl_in, wl_pna, wl_pret, wl_out = s["w"]
        (dxa_res, d_out), (d_gate,) = residual_bwd((s["xa"], s["out"]), s["vecs"][3:], (dxa,))
        d_out = d_out.astype(BF16)
        send_dw(l, 3, _matmul(s["merged"], d_out, trans_a=True, out_dtype=BF16, name="out_proj_dw"))
        d_merged = _matmul(d_out, wl_out, trans_b=True, out_dtype=BF16, name="out_proj_da")
        (dg_na, dg_ret, dy_na, dy_ret), _ = merge_bwd((blk[G_NA], blk[G_RET], s["y_na"], s["y_ret"]), (), (d_merged,))
        dy_na, dy_ret = dy_na.astype(BF16), dy_ret.astype(BF16)
        send_dw(l, 2, _matmul(s["a_ret"], dy_ret, trans_a=True, out_dtype=BF16, name="proj_ret_dw"))
        da_ret = _matmul(dy_ret, wl_pret, trans_b=True, out_dtype=BF16, name="proj_ret_da")
        tok = send_dw(l, 1, _matmul(s["a_na"], dy_na, trans_a=True, out_dtype=BF16, name="proj_na_dw"))
        da_na = _matmul(dy_na, wl_pna, trans_b=True, out_dtype=BF16, name="proj_na_da", after=tok)
        do_ret, dz_ret = _gate_ret_bwd_call(s["o_f"], s["o_b"], blk[RET_Z], da_ret, tm_row)
        (do_na, dz_na), _ = gate_na_bwd((s["o_na"], blk[NA_Z]), (), (da_na,))
        dq_f, dk_f, dv_f, dl_f = _ret_bwd_call(blk[RET_Q], blk[RET_K], blk[RET_V], s["lam"][0], s["st_f"], do_ret, n_lat, False)
        dq_b, dk_b, dv_b, dl_b = _ret_bwd_call(blk[RET_Q], blk[RET_K], blk[RET_V], s["lam"][1], s["st_b"], do_ret, n_lat, True)
        dq, dk, dv, dbt = _na_bwd_call(blk[NA_Q], blk[NA_K], blk[NA_V], s["bt"], do_na, n_lat)
        d_blk = [None] * 10
        d_blk[NA_Q], d_blk[NA_K], d_blk[NA_V], d_blk[NA_Z] = dq, dk.astype(BF16), dv.astype(BF16), dz_na
        d_blk[RET_Q], d_blk[RET_K], d_blk[RET_V], d_blk[RET_Z] = dq_f + dq_b, dk_f + dk_b, dv_f + dv_b, dz_ret
        d_blk[G_NA], d_blk[G_RET] = dg_na, dg_ret
        (du,) = s["split_vjp"](tuple(d_blk))
        (d_rpb[l],) = s["bt_vjp"](dbt)
        (d_decay[l],) = s["lam_vjp"](jnp.stack([dl_f[:, 0, 0], dl_b[:, 0, 0]]))
        tok = send_dw(l, 0, _matmul(s["h"], du, trans_a=True, out_dtype=BF16, name="in_proj_dw"))
        dh = _matmul(du, wl_in, trans_b=True, out_dtype=BF16, name="in_proj_da", after=tok)
        (dxa_norm,), d_vecs = norm_mod_bwd((s["xa"],), s["vecs"][:3], (dh,))
        dxa = dxa_res + dxa_norm
        d_mod_lat[l], d_mod_ctx[l], d_norm_g[l] = s["vecs_vjp"](tuple(d_vecs) + (d_gate,))
    gx = dxa[:n_lat]
    d_mod_lat, d_mod_ctx, d_norm_g, d_rpb, d_decay = (jnp.stack(a) for a in (d_mod_lat, d_mod_ctx, d_norm_g, d_rpb, d_decay))

    small_shapes = [d_mod_lat.shape, d_mod_ctx.shape, d_norm_g.shape, d_final_g.shape, d_rpb.shape, d_decay.shape]
    packed = _pack([d_mod_lat, d_mod_ctx, d_norm_g, d_final_g, d_rpb, d_decay])
    g_all, g_sum = _small_allgather(packed, "allgather_small_grads")
    dml_sum, dmc_sum, grad_norm_g, grad_final_g, grad_na_rpb, grad_decay = _unpack(g_sum, small_shapes)
    grad_ada_b = dml_sum + dmc_sum
    dml_all = g_all.reshape(N_DEV, -1)[:, :depth * 3 * d].reshape(N_DEV, depth, 3 * d)

    def my_cols(a):
        return lax.dynamic_slice_in_dim(a, my_idx * ncol, ncol, axis=a.ndim - 1)

    dmod = jnp.concatenate(
        [my_cols(dml_all).transpose(1, 0, 2), my_cols(dmc_sum)[:, None, :], jnp.zeros((depth, 7, ncol), F32)], axis=1)
    grad_ada_w, dcs_part = _ada_bwd_call(cin, ada_w, dmod)
    _, dcs = _small_allgather(dcs_part, "allgather_dcsilu")
    sg = jax.nn.sigmoid(c_ctx)
    grad_c_ctx = dcs[8] * (sg * (1.0 + c_ctx * (1.0 - sg)))

    def flat2(a):
        return a.reshape(a.shape[0] * a.shape[1], a.shape[2])

    small_w = [c_ctx, ada_b, norm_g, na_rpb, ret_decay_logit, final_g]
    small_g = [grad_c_ctx, grad_ada_b, grad_norm_g, grad_na_rpb, grad_decay, grad_final_g]
    small_m = [m_c_ctx, m_ada_b, m_norm_g, m_na_rpb, m_ret_decay_logit, m_final_g]
    small_v = [v_c_ctx, v_ada_b, v_norm_g, v_na_rpb, v_ret_decay_logit, v_final_g]
    shp = [a.shape for a in small_w]
    ds_, nms_, nvs_ = _adamw_dense(_pack(small_w), _pack(small_g), _pack(small_m), _pack(small_v), "adamw_small")
    ds_, nms_, nvs_ = _unpack(ds_, shp), _unpack(nms_, shp), _unpack(nvs_, shp)

    d_ada, nm_ada, nv_ada = [a.reshape(ada_w.shape) for a in _adamw_dense(
        flat2(ada_w), flat2(grad_ada_w), flat2(m_ada_w), flat2(v_ada_w), "adamw_ada_w")]

    w_all = (w_in, w_proj_na, w_proj_ret, w_out)
    m_all = (m_w_in, m_w_proj_na, m_w_proj_ret, m_w_out)
    v_all = (v_w_in, v_w_proj_na, v_w_proj_ret, v_w_out)
    upd = [None] * 4
    after = d_ada
    for flight in scatters:
        slabs = _push_wait(flight["name"], flight["sems"], flight["srcs"], flight["lands"], flight["axes"],
                           _scatter_views, after)
        for (l, a), s in zip(flight["keys"], slabs):
            upd[a] = _adamw_sharded(w_all[a], m_all[a], v_all[a], s, l, upd[a], "adamw_" + w_names[a])
            after = upd[a][1]
    (g_w_in, d_w_in, nm_w_in, nv_w_in), (g_pna, d_pna, nm_pna, nv_pna) = upd[0], upd[1]
    (g_pret, d_pret, nm_pret, nv_pret), (g_out, d_out, nm_out, nv_out) = upd[2], upd[3]

    def order(cc, aw, ab, ng, wi, rp, dl, pn, pr, wo, fg):
        return [cc, aw, ab, ng, wi, rp, dl, pn, pr, wo, fg]

    grads_out = order(grad_c_ctx, grad_ada_w, grad_ada_b, grad_norm_g, g_w_in, grad_na_rpb, grad_decay, g_pna, g_pret, g_out, grad_final_g)
    delta_out = order(ds_[0], d_ada, ds_[1], ds_[2], d_w_in, ds_[3], ds_[4], d_pna, d_pret, d_out, ds_[5])
    m_out = order(nms_[0], nm_ada, nms_[1], nms_[2], nm_w_in, nms_[3], nms_[4], nm_pna, nm_pret, nm_out, nms_[5])
    v_out = order(nvs_[0], nv_ada, nvs_[1], nvs_[2], nv_w_in, nvs_[3], nvs_[4], nv_pna, nv_pret, nv_out, nvs_[5])
    return (loss, gx[None], *grads_out, *delta_out, *m_out, *v_out)
```

```python
import functools

import numpy as np
import jax
import jax.numpy as jnp
from jax import lax
from jax.experimental import pallas as pl
from jax.experimental.pallas import tpu as pltpu

F32 = jnp.float32
BF16 = jnp.bfloat16

N_DEV = 8
GRID_W = 64
NA_HEAD_DIM = 128
NA_WIN_ROWS = 8
NA_WIN_COLS = 16
RET_KEY_DIM = 128
RET_VAL_DIM = 256
RET_CHUNK = 128
ROPE_BASE = 10000.0
NORM_EPS = 1e-6
MASK_VALUE = -1e30

ADAM_LR = 0.001
ADAM_B1 = 0.9
ADAM_B2 = 0.999
ADAM_EPS = 1e-08
ADAM_WD = 0.01
ADAM_STEP = 10

VMEM_LIMIT = 48 * 1024 * 1024
MESH = pl.DeviceIdType.MESH
ANY = pl.BlockSpec(memory_space=pl.ANY)
VMEM_SPEC = pl.BlockSpec(memory_space=pltpu.VMEM)


def _params(sem=None):
    return pltpu.CompilerParams(dimension_semantics=sem, vmem_limit_bytes=VMEM_LIMIT)


def _pick(n, prefs):
    for p in prefs:
        if n % p == 0:
            return p
    return n


def _dot(a, b):
    return lax.dot_general(a, b, (((1,), (0,)), ((), ())), preferred_element_type=F32)


def _dot_nt(a, b):
    return lax.dot_general(a, b, (((1,), (1,)), ((), ())), preferred_element_type=F32)


def _dot_tn(a, b):
    return lax.dot_general(a, b, (((0,), (0,)), ((), ())), preferred_element_type=F32)


def _silu(x):
    return x * jax.nn.sigmoid(x)


def _matmul(a, b, *, trans_a=False, trans_b=False, out_dtype=F32, name="matmul", after=None):
    if trans_a:
        kdim, m = a.shape
    else:
        m, kdim = a.shape
    if trans_b:
        n, kb = b.shape
    else:
        kb, n = b.shape
    assert kdim == kb, (a.shape, b.shape, trans_a, trans_b)
    tm = _pick(m, (1152, 1024, 768, 512, 256, 128))
    tn = _pick(n, (512, 256, 128))
    tk = _pick(kdim, (2304, 2048, 1024, 512, 256, 128))
    nk = kdim // tk
    dn = (((0 if trans_a else 1,), (1 if trans_b else 0,)), ((), ()))

    def body(a_ref, b_ref, *rest):
        o_ref, acc_ref = rest[-2:]
        part = lax.dot_general(a_ref[...], b_ref[...], dn, preferred_element_type=F32)
        if nk == 1:
            o_ref[...] = part.astype(o_ref.dtype)
        else:
            k = pl.program_id(2)

            @pl.when(k == 0)
            def _():
                acc_ref[...] = part

            @pl.when(k > 0)
            def _():
                acc_ref[...] += part

            @pl.when(k == nk - 1)
            def _():
                o_ref[...] = acc_ref[...].astype(o_ref.dtype)

    a_spec = pl.BlockSpec((tk, tm), lambda i, j, k: (k, i)) if trans_a else pl.BlockSpec((tm, tk), lambda i, j, k: (i, k))
    b_spec = pl.BlockSpec((tn, tk), lambda i, j, k: (j, k)) if trans_b else pl.BlockSpec((tk, tn), lambda i, j, k: (k, j))
    return pl.pallas_call(
        body,
        name=name,
        grid=(m // tm, n // tn, nk),
        in_specs=[a_spec, b_spec] + ([] if after is None else [ANY]),
        out_specs=pl.BlockSpec((tm, tn), lambda i, j, k: (i, j)),
        out_shape=jax.ShapeDtypeStruct((m, n), out_dtype),
        scratch_shapes=[pltpu.VMEM((tm, tn) if nk > 1 else (8, 128), F32)],
        compiler_params=_params(("parallel", "parallel", "arbitrary")),
    )(*((a, b) if after is None else (a, b, after)))


def _make_rowwise(f, name, out_dtypes, out_cols, n_lat, tm, diff_rows):
    def tile_fn(*args):
        return tuple(o.astype(dt) for o, dt in zip(f(*args), out_dtypes))

    def fwd_call(rows, vecs):
        t = rows[0].shape[0]
        nr, nv = len(rows), len(vecs)
        nl = n_lat // tm

        def body(*refs):
            grp = (pl.program_id(0) >= nl).astype(jnp.int32)
            args = [r[...] for r in refs[:nr]] + [v[grp] for v in refs[nr:nr + nv]]
            for o_ref, o in zip(refs[nr + nv:], tile_fn(*args)):
                o_ref[...] = o

        return pl.pallas_call(
            body,
            name=name + "_fwd",
            grid=(t // tm,),
            in_specs=[pl.BlockSpec((tm, r.shape[1]), lambda i: (i, 0)) for r in rows]
            + [pl.BlockSpec(v.shape, lambda i: (0, 0, 0)) for v in vecs],
            out_specs=[pl.BlockSpec((tm, c), lambda i: (i, 0)) for c in out_cols],
            out_shape=[jax.ShapeDtypeStruct((t, c), dt) for c, dt in zip(out_cols, out_dtypes)],
            compiler_params=_params(("parallel",)),
        )(*rows, *vecs)

    def bwd_call(rows, vecs, gs):
        t = rows[0].shape[0]
        nr, nv, ng = len(rows), len(vecs), len(gs)
        nl = n_lat // tm
        nd = len(diff_rows)

        def body(*refs):
            i = pl.program_id(0)
            grp = (i >= nl).astype(jnp.int32)
            args = [r[...] for r in refs[:nr]] + [v[grp] for v in refs[nr:nr + nv]]
            g_refs = refs[nr + nv:nr + nv + ng]
            drow_refs = refs[nr + nv + ng:nr + nv + ng + nd]
            dvec_refs = refs[nr + nv + ng + nd:]
            _, vjp = jax.vjp(tile_fn, *args)
            grads = vjp(tuple(g[...] for g in g_refs))
            for d_ref, k in zip(drow_refs, diff_rows):
                d_ref[...] = grads[k].astype(d_ref.dtype)

            @pl.when(i == 0)
            def _():
                for d_ref in dvec_refs:
                    d_ref[...] = jnp.zeros_like(d_ref)

            for j, d_ref in enumerate(dvec_refs):
                d_ref[grp] += grads[nr + j]

        outs = pl.pallas_call(
            body,
            name=name + "_bwd",
            grid=(t // tm,),
            in_specs=[pl.BlockSpec((tm, r.shape[1]), lambda i: (i, 0)) for r in rows]
            + [pl.BlockSpec(v.shape, lambda i: (0, 0, 0)) for v in vecs]
            + [pl.BlockSpec((tm, g.shape[1]), lambda i: (i, 0)) for g in gs],
            out_specs=[pl.BlockSpec((tm, rows[k].shape[1]), lambda i: (i, 0)) for k in diff_rows]
            + [pl.BlockSpec(v.shape, lambda i: (0, 0, 0)) for v in vecs],
            out_shape=[jax.ShapeDtypeStruct(rows[k].shape, rows[k].dtype) for k in diff_rows]
            + [jax.ShapeDtypeStruct(v.shape, F32) for v in vecs],
            compiler_params=_params(("arbitrary",)),
        )(*rows, *vecs, *gs)
        return outs[:nd], outs[nd:]

    return fwd_call, bwd_call


def _f_norm_mod(x, g, scale, shift):
    r = lax.rsqrt(jnp.mean(x * x, axis=-1, keepdims=True) + NORM_EPS)
    return ((x * r * g) * (1.0 + scale) + shift,)


def _f_gate_na(o, z):
    return (o.astype(F32) * _silu(z.astype(F32)),)


def _f_merge(g_na, g_ret, y_na, y_ret):
    return (jax.nn.sigmoid(g_na.astype(F32)) * y_na + jax.nn.sigmoid(g_ret.astype(F32)) * y_ret,)


def _f_residual(x, out, gate):
    return (x + gate * out,)


def _f_loss(x, target, g):
    r = lax.rsqrt(jnp.mean(x * x, axis=-1, keepdims=True) + NORM_EPS)
    y = x * r * g
    e = 0.5 * jnp.mean(jnp.square(y - target), axis=-1, keepdims=True)
    return (jnp.broadcast_to(e * (1.0 / 128.0), (x.shape[0], 128)),)


def _gate_ret_fwd_call(of, ob, z, tm):
    t, w = of.shape
    nh = w // RET_VAL_DIM

    def body(of_ref, ob_ref, z_ref, a_ref):
        for hh in range(nh):
            sl = slice(hh * RET_VAL_DIM, (hh + 1) * RET_VAL_DIM)
            o = of_ref[:, sl] + ob_ref[:, sl]
            r = lax.rsqrt(jnp.mean(o * o, axis=-1, keepdims=True) + NORM_EPS)
            a_ref[:, sl] = ((o * r) * _silu(z_ref[:, sl].astype(F32))).astype(a_ref.dtype)

    spec = pl.BlockSpec((tm, w), lambda i: (i, 0))
    return pl.pallas_call(
        body, name="gate_ret_fwd", grid=(t // tm,), in_specs=[spec, spec, spec], out_specs=spec,
        out_shape=jax.ShapeDtypeStruct((t, w), BF16), compiler_params=_params(("parallel",)),
    )(of, ob, z)


def _gate_ret_bwd_call(of, ob, z, da, tm):
    t, w = of.shape
    nh = w // RET_VAL_DIM

    def body(of_ref, ob_ref, z_ref, da_ref, do_ref, dz_ref):
        for hh in range(nh):
            sl = slice(hh * RET_VAL_DIM, (hh + 1) * RET_VAL_DIM)
            o = of_ref[:, sl] + ob_ref[:, sl]
            r = lax.rsqrt(jnp.mean(o * o, axis=-1, keepdims=True) + NORM_EPS)
            n = o * r
            zf = z_ref[:, sl].astype(F32)
            sg = jax.nn.sigmoid(zf)
            g = da_ref[:, sl].astype(F32)
            dn = g * (zf * sg)
            dz_ref[:, sl] = (g * n * (sg * (1.0 + zf * (1.0 - sg)))).astype(dz_ref.dtype)
            do_ref[:, sl] = r * (dn - n * jnp.mean(dn * n, axis=-1, keepdims=True))

    spec = pl.BlockSpec((tm, w), lambda i: (i, 0))
    return pl.pallas_call(
        body, name="gate_ret_bwd", grid=(t // tm,), in_specs=[spec, spec, spec, spec], out_specs=[spec, spec],
        out_shape=[jax.ShapeDtypeStruct((t, w), F32), jax.ShapeDtypeStruct((t, w), z.dtype)],
        compiler_params=_params(("parallel",)),
    )(of, ob, z, da)


def _na_geometry(t, n_lat):
    rows = n_lat // GRID_W
    kh = min(NA_WIN_ROWS, rows)
    return rows, kh, kh * GRID_W, t - n_lat, t // GRID_W


def _na_row0(r, rows, kh):
    return jnp.clip(r - kh // 2, 0, rows - kh)


def _na_bias_idx(r, rows, kh):
    return jnp.clip(_na_row0(r, rows, kh) - r + (NA_WIN_ROWS - 1), 0, NA_WIN_ROWS - 1)


def _na_fwd_call(q, k, v, bt, n_lat):
    t, w = q.shape
    nh = w // NA_HEAD_DIM
    rows, kh, n_loc, n_ctx, nq = _na_geometry(t, n_lat)
    scale = NA_HEAD_DIM ** -0.5

    def body(q_ref, k_ref, v_ref, bt_ref, o_ref):
        r = pl.program_id(1)
        qb = q_ref[...]
        kc = k_ref[pl.ds(n_lat, n_ctx), :]
        vc = v_ref[pl.ds(n_lat, n_ctx), :]
        s_ctx = _dot_nt(qb, kc) * scale

        @pl.when(r < rows)
        def _():
            start = pl.multiple_of(_na_row0(r, rows, kh) * GRID_W, GRID_W)
            kw = k_ref[pl.ds(start, n_loc), :]
            vw = v_ref[pl.ds(start, n_loc), :]
            s_loc = _dot_nt(qb, kw) * scale + bt_ref[0, 0]
            m = jnp.maximum(jnp.max(s_loc, axis=-1, keepdims=True), jnp.max(s_ctx, axis=-1, keepdims=True))
            p_loc = jnp.exp(s_loc - m)
            p_ctx = jnp.exp(s_ctx - m)
            l = jnp.sum(p_loc, axis=-1, keepdims=True) + jnp.sum(p_ctx, axis=-1, keepdims=True)
            o = _dot(p_loc.astype(BF16), vw) + _dot(p_ctx.astype(BF16), vc)
            o_ref[...] = (o / l).astype(o_ref.dtype)

        @pl.when(r >= rows)
        def _():
            m = jnp.max(s_ctx, axis=-1, keepdims=True)
            p = jnp.exp(s_ctx - m)
            l = jnp.sum(p, axis=-1, keepdims=True)
            o_ref[...] = (_dot(p.astype(BF16), vc) / l).astype(o_ref.dtype)

    return pl.pallas_call(
        body,
        name="na_attn_fwd",
        grid=(nh, nq),
        in_specs=[
            pl.BlockSpec((GRID_W, NA_HEAD_DIM), lambda h, r: (r, h)),
            pl.BlockSpec((t, NA_HEAD_DIM), lambda h, r: (0, h)),
            pl.BlockSpec((t, NA_HEAD_DIM), lambda h, r: (0, h)),
            pl.BlockSpec((1, 1, GRID_W, n_loc), lambda h, r: (h, _na_bias_idx(r, rows, kh), 0, 0)),
        ],
        out_specs=pl.BlockSpec((GRID_W, NA_HEAD_DIM), lambda h, r: (r, h)),
        out_shape=jax.ShapeDtypeStruct((t, w), BF16),
        compiler_params=_params(("parallel", "arbitrary")),
    )(q, k, v, bt)


def _na_bwd_call(q, k, v, bt, do, n_lat):
    t, w = q.shape
    nh = w // NA_HEAD_DIM
    rows, kh, n_loc, n_ctx, nq = _na_geometry(t, n_lat)
    scale = NA_HEAD_DIM ** -0.5

    def body(q_ref, k_ref, v_ref, bt_ref, do_ref, dq_ref, dk_ref, dv_ref, dbt_ref):
        r = pl.program_id(1)

        @pl.when(r == 0)
        def _():
            dk_ref[...] = jnp.zeros_like(dk_ref)
            dv_ref[...] = jnp.zeros_like(dv_ref)

        qb = q_ref[...]
        dob = do_ref[...]
        kc = k_ref[pl.ds(n_lat, n_ctx), :]
        vc = v_ref[pl.ds(n_lat, n_ctx), :]
        s_ctx = _dot_nt(qb, kc) * scale
        dp_ctx = _dot_nt(dob, vc)

        @pl.when(r < rows)
        def _():
            start = pl.multiple_of(_na_row0(r, rows, kh) * GRID_W, GRID_W)
            kw = k_ref[pl.ds(start, n_loc), :]
            vw = v_ref[pl.ds(start, n_loc), :]
            s_loc = _dot_nt(qb, kw) * scale + bt_ref[0, 0]
            m = jnp.maximum(jnp.max(s_loc, axis=-1, keepdims=True), jnp.max(s_ctx, axis=-1, keepdims=True))
            p_loc = jnp.exp(s_loc - m)
            p_ctx = jnp.exp(s_ctx - m)
            inv = 1.0 / (jnp.sum(p_loc, axis=-1, keepdims=True) + jnp.sum(p_ctx, axis=-1, keepdims=True))
            p_loc = p_loc * inv
            p_ctx = p_ctx * inv
            dp_loc = _dot_nt(dob, vw)
            delta = jnp.sum(p_loc * dp_loc, axis=-1, keepdims=True) + jnp.sum(p_ctx * dp_ctx, axis=-1, keepdims=True)
            ds_loc = p_loc * (dp_loc - delta)
            ds_ctx = p_ctx * (dp_ctx - delta)
            first = jnp.logical_or(r == 0, _na_bias_idx(r, rows, kh) != _na_bias_idx(r - 1, rows, kh))

            @pl.when(first)
            def _():
                dbt_ref[0, 0] = ds_loc

            @pl.when(jnp.logical_not(first))
            def _():
                dbt_ref[0, 0] += ds_loc

            dsl = (ds_loc * scale).astype(BF16)
            dsc = (ds_ctx * scale).astype(BF16)
            dq_ref[...] = (_dot(dsl, kw) + _dot(dsc, kc)).astype(dq_ref.dtype)
            dk_ref[pl.ds(start, n_loc), :] += _dot_tn(dsl, qb)
            dv_ref[pl.ds(start, n_loc), :] += _dot_tn(p_loc.astype(BF16), dob)
            dk_ref[pl.ds(n_lat, n_ctx), :] += _dot_tn(dsc, qb)
            dv_ref[pl.ds(n_lat, n_ctx), :] += _dot_tn(p_ctx.astype(BF16), dob)

        @pl.when(r >= rows)
        def _():
            m = jnp.max(s_ctx, axis=-1, keepdims=True)
            p = jnp.exp(s_ctx - m)
            p = p * (1.0 / jnp.sum(p, axis=-1, keepdims=True))
            delta = jnp.sum(p * dp_ctx, axis=-1, keepdims=True)
            dsc = (p * (dp_ctx - delta) * scale).astype(BF16)
            dq_ref[...] = _dot(dsc, kc).astype(dq_ref.dtype)
            dk_ref[pl.ds(n_lat, n_ctx), :] += _dot_tn(dsc, qb)
            dv_ref[pl.ds(n_lat, n_ctx), :] += _dot_tn(p.astype(BF16), dob)

    qspec = pl.BlockSpec((GRID_W, NA_HEAD_DIM), lambda h, r: (r, h))
    kspec = pl.BlockSpec((t, NA_HEAD_DIM), lambda h, r: (0, h))
    bspec = pl.BlockSpec((1, 1, GRID_W, n_loc), lambda h, r: (h, _na_bias_idx(r, rows, kh), 0, 0))
    return pl.pallas_call(
        body,
        name="na_attn_bwd",
        grid=(nh, nq),
        in_specs=[qspec, kspec, kspec, bspec, qspec],
        out_specs=[qspec, kspec, kspec, bspec],
        out_shape=[
            jax.ShapeDtypeStruct((t, w), BF16),
            jax.ShapeDtypeStruct((t, w), F32),
            jax.ShapeDtypeStruct((t, w), F32),
            jax.ShapeDtypeStruct(bt.shape, F32),
        ],
        compiler_params=_params(("parallel", "arbitrary")),
    )(q, k, v, bt, do)


def _na_bias_table(rpb, rows):
    kh = min(NA_WIN_ROWS, rows)
    nj = NA_WIN_ROWS
    e1 = np.zeros((nj, kh, 2 * NA_WIN_ROWS - 1), np.float32)
    for j in range(nj):
        for kk in range(kh):
            if j + kk < 2 * NA_WIN_ROWS - 1:
                e1[j, kk, j + kk] = 1.0
    cidx = np.arange(GRID_W)
    dc = np.clip(cidx[None, :] - cidx[:, None] + (NA_WIN_COLS - 1), 0, 2 * NA_WIN_COLS - 2)
    e2 = np.zeros((GRID_W, GRID_W, 2 * NA_WIN_COLS - 1), np.float32)
    e2[np.arange(GRID_W)[:, None], np.arange(GRID_W)[None, :], dc] = 1.0
    c0 = np.clip(cidx - NA_WIN_COLS // 2, 0, GRID_W - NA_WIN_COLS)
    col_in = (cidx[None, :] >= c0[:, None]) & (cidx[None, :] < c0[:, None] + NA_WIN_COLS)
    t1 = jnp.einsum("hab,jka->hjkb", rpb, jnp.asarray(e1), precision=lax.Precision.HIGHEST)
    b = jnp.einsum("hjkb,cwb->hjckw", t1, jnp.asarray(e2), precision=lax.Precision.HIGHEST)
    b = jnp.where(jnp.asarray(col_in)[None, None, :, None, :], b, MASK_VALUE)
    return b.reshape(rpb.shape[0], nj, GRID_W, kh * GRID_W)


def _ret_decays(lam_s, reverse):
    c = RET_CHUNK
    ii = lax.broadcasted_iota(jnp.int32, (c, c), 0)
    jj = lax.broadcasted_iota(jnp.int32, (c, c), 1)
    d = (jj - ii) if reverse else (ii - jj)
    dpos = jnp.maximum(d.astype(F32), 0.0)
    mask = jnp.where(d >= 0, jnp.exp(dpos * lam_s), 0.0)
    pi = lax.broadcasted_iota(jnp.int32, (c, 1), 0).astype(F32)
    qpos = (c - pi) if reverse else (pi + 1.0)
    kpos = pi if reverse else (c - 1.0 - pi)
    qd = jnp.exp(qpos * lam_s)
    kd = jnp.exp(kpos * lam_s)
    g = jnp.exp(jnp.full((1, RET_VAL_DIM), c * lam_s, F32))
    return mask, dpos, qd, kd, qpos, kpos, g


def _ret_chunk_of(t, nt, nl, reverse):
    return (nt - 1 - t) if reverse else (t + nl) % nt


def _ret_fwd_call(qr, kr, v, lam, n_lat, reverse):
    t = qr.shape[0]
    nh = qr.shape[1] // RET_KEY_DIM
    c = RET_CHUNK
    nt, nl = t // c, n_lat // c

    def body(lam_ref, q_ref, k_ref, v_ref, o_ref, s_ref, state):
        h, step = pl.program_id(0), pl.program_id(1)

        @pl.when(step == 0)
        def _():
            state[...] = jnp.zeros_like(state)

        mask, _, qd, kd, _, _, g = _ret_decays(lam_ref[h], reverse)
        q, k, vv = q_ref[...], k_ref[...], v_ref[...]
        p = _dot_nt(q, k) * mask
        s = state[...]
        qs = (q.astype(F32) * qd).astype(BF16)
        o_ref[...] = _dot(p.astype(BF16), vv) + _dot(qs, s.astype(BF16))
        s_ref[0, 0] = s
        ks = (k.astype(F32) * kd).astype(BF16)
        state[...] = s * g + _dot_tn(ks, vv)

    def cmap(h, step, lam_ref):
        return (_ret_chunk_of(step, nt, nl, reverse), h)

    return pl.pallas_call(
        body,
        name="retention_rev_fwd" if reverse else "retention_fwd",
        grid_spec=pltpu.PrefetchScalarGridSpec(
            num_scalar_prefetch=1,
            grid=(nh, nt),
            in_specs=[
                pl.BlockSpec((c, RET_KEY_DIM), cmap),
                pl.BlockSpec((c, RET_KEY_DIM), cmap),
                pl.BlockSpec((c, RET_VAL_DIM), cmap),
            ],
            out_specs=[
                pl.BlockSpec((c, RET_VAL_DIM), cmap),
                pl.BlockSpec((1, 1, RET_KEY_DIM, RET_VAL_DIM), lambda h, step, lam_ref: (h, step, 0, 0)),
            ],
            scratch_shapes=[pltpu.VMEM((RET_KEY_DIM, RET_VAL_DIM), F32)],
        ),
        out_shape=[
            jax.ShapeDtypeStruct((t, nh * RET_VAL_DIM), F32),
            jax.ShapeDtypeStruct((nh, nt, RET_KEY_DIM, RET_VAL_DIM), F32),
        ],
        compiler_params=_params(("parallel", "arbitrary")),
    )(lam, qr, kr, v)


def _ret_bwd_call(qr, kr, v, lam, states, do, n_lat, reverse):
    t = qr.shape[0]
    nh = qr.shape[1] // RET_KEY_DIM
    c = RET_CHUNK
    nt, nl = t // c, n_lat // c

    def body(lam_ref, q_ref, k_ref, v_ref, s_ref, do_ref, dq_ref, dk_ref, dv_ref, dl_ref, dstate):
        h, rstep = pl.program_id(0), pl.program_id(1)

        @pl.when(rstep == 0)
        def _():
            dstate[...] = jnp.zeros_like(dstate)
            dl_ref[...] = jnp.zeros_like(dl_ref)

        mask, dpos, qd, kd, qpos, kpos, g = _ret_decays(lam_ref[h], reverse)
        q, k, vv = q_ref[...], k_ref[...], v_ref[...]
        qf, kf = q.astype(F32), k.astype(F32)
        s = s_ref[0, 0]
        ds = dstate[...]
        dob = do_ref[...].astype(BF16)
        sb, dsb = s.astype(BF16), ds.astype(BF16)
        a = _dot_nt(q, k)
        p = a * mask
        dp = _dot_nt(dob, vv)
        da = dp * mask
        dab = da.astype(BF16)
        dqc = _dot_nt(dob, sb)
        dkc = _dot_nt(vv, dsb)
        qs = (qf * qd).astype(BF16)
        ks = (kf * kd).astype(BF16)
        dq_ref[...] = (_dot(dab, k) + dqc * qd).astype(dq_ref.dtype)
        dk_ref[...] = (_dot_tn(dab, q) + dkc * kd).astype(dk_ref.dtype)
        dv_ref[...] = (_dot_tn(p.astype(BF16), dob) + _dot(ks, dsb)).astype(dv_ref.dtype)
        terms = (
            jnp.sum(jnp.sum(da * a * dpos, axis=1, keepdims=True), axis=0, keepdims=True)
            + jnp.sum(jnp.sum(dqc * qf * (qd * qpos), axis=1, keepdims=True), axis=0, keepdims=True)
            + jnp.sum(jnp.sum(dkc * kf * (kd * kpos), axis=1, keepdims=True), axis=0, keepdims=True)
            + jnp.sum(jnp.sum(ds * s * (g * c), axis=1, keepdims=True), axis=0, keepdims=True)
        )
        dl_ref[0] += jnp.broadcast_to(terms, (8, 128))
        dstate[...] = ds * g + _dot_tn(qs, dob)

    def cmap(h, rstep, lam_ref):
        return (_ret_chunk_of(nt - 1 - rstep, nt, nl, reverse), h)

    return pl.pallas_call(
        body,
        name="retention_rev_bwd" if reverse else "retention_bwd",
        grid_spec=pltpu.PrefetchScalarGridSpec(
            num_scalar_prefetch=1,
            grid=(nh, nt),
            in_specs=[
                pl.BlockSpec((c, RET_KEY_DIM), cmap),
                pl.BlockSpec((c, RET_KEY_DIM), cmap),
                pl.BlockSpec((c, RET_VAL_DIM), cmap),
                pl.BlockSpec((1, 1, RET_KEY_DIM, RET_VAL_DIM), lambda h, rstep, lam_ref: (h, nt - 1 - rstep, 0, 0)),
                pl.BlockSpec((c, RET_VAL_DIM), cmap),
            ],
            out_specs=[
                pl.BlockSpec((c, RET_KEY_DIM), cmap),
                pl.BlockSpec((c, RET_KEY_DIM), cmap),
                pl.BlockSpec((c, RET_VAL_DIM), cmap),
                pl.BlockSpec((1, 8, 128), lambda h, rstep, lam_ref: (h, 0, 0)),
            ],
            scratch_shapes=[pltpu.VMEM((RET_KEY_DIM, RET_VAL_DIM), F32)],
        ),
        out_shape=[
            jax.ShapeDtypeStruct(qr.shape, qr.dtype),
            jax.ShapeDtypeStruct(kr.shape, kr.dtype),
            jax.ShapeDtypeStruct(v.shape, v.dtype),
            jax.ShapeDtypeStruct((nh, 8, 128), F32),
        ],
        compiler_params=_params(("parallel", "arbitrary")),
    )(lam, qr, kr, v, states, do)


def _rope_tables(t, n_lat):
    nf = RET_KEY_DIM // 4
    tok = np.arange(n_lat)
    inv_freq = (ROPE_BASE ** (-np.arange(nf, dtype=np.float32) / nf)).astype(np.float32)
    row = (tok // GRID_W).astype(np.float32)
    col = (tok % GRID_W).astype(np.float32)
    ang = np.concatenate([row[:, None] * inv_freq, col[:, None] * inv_freq], axis=-1).astype(np.float32)
    cos = np.ones((t, 2 * nf), np.float32)
    sin = np.zeros((t, 2 * nf), np.float32)
    cos[:n_lat] = np.cos(ang)
    sin[:n_lat] = np.sin(ang)
    return jnp.asarray(cos), jnp.asarray(sin)


def _rope(xb, cos, sin, mult):
    t, w = xb.shape
    nh = w // RET_KEY_DIM
    half = RET_KEY_DIM // 2
    x = xb.astype(F32).reshape(t, nh, 2, half)
    x1, x2 = x[:, :, 0], x[:, :, 1]
    c, s = cos[:, None, :], sin[:, None, :]
    out = jnp.stack([x1 * c - x2 * s, x2 * c + x1 * s], axis=2) * mult
    return out.reshape(t, w).astype(BF16)


def _my_position():
    return lax.axis_index("x"), lax.axis_index("y"), lax.axis_index("c")


def _flip(pos, k):
    x, y, c = pos
    return (1 - x if k & 4 else x, 1 - y if k & 2 else y, 1 - c if k & 1 else c)


def _linear(pos):
    return 4 * pos[0] + 2 * pos[1] + pos[2]


def _slab(ref, axis, idx, size):
    start = pl.multiple_of(idx * size, size)
    return ref.at[pl.ds(start, size), :] if axis == 0 else ref.at[:, pl.ds(start, size)]


HBM_SPEC = pl.BlockSpec(memory_space=pltpu.HBM)
SEM_SPEC = pl.BlockSpec(memory_space=pltpu.SEMAPHORE)
DATAFLOW = pltpu.SideEffectType.DATAFLOW_SIDE_EFFECTING
PEER_BITS = (1, 2, 4, 6, 3, 5, 7)


def _in_hbm(a):
    return pltpu.with_memory_space_constraint(a, pltpu.HBM)


def _gather_views(me, k, a, src_refs, land_refs, axes):
    size = src_refs[a].shape[axes[a]]
    peer = _flip(me, k)
    return src_refs[a], _slab(land_refs[a], axes[a], _linear(me), size), _slab(land_refs[a], axes[a], _linear(peer), size)


def _scatter_views(me, k, a, src_refs, land_refs, axes):
    size = land_refs[a].shape[1 + axes[a]]
    peer = _flip(me, k)
    return _slab(src_refs[a], axes[a], _linear(peer), size), land_refs[a].at[k - 1], land_refs[a].at[k - 1]


def _slab_block(rows, cols, tm, axis):
    if axis == 0:
        return pl.BlockSpec((tm, cols), lambda i, idx: (idx[0] * (rows // tm) + i, 0))
    return pl.BlockSpec((tm, cols), lambda i, idx: (i, idx[0]))


def _place_shard(shard, land, axis, my_idx):
    r, c = shard.shape
    tm = _pick(r, (512, 256, 128, 64, 32, 16))

    def body(idx_ref, s_ref, land_ref, o_ref):
        o_ref[...] = s_ref[...]

    return pl.pallas_call(
        body, name="gather_place",
        grid_spec=pltpu.PrefetchScalarGridSpec(
            num_scalar_prefetch=1, grid=(r // tm,),
            in_specs=[pl.BlockSpec((tm, c), lambda i, idx: (i, 0)), ANY],
            out_specs=_slab_block(r, c, tm, axis)),
        out_shape=jax.ShapeDtypeStruct(land.shape, land.dtype),
        input_output_aliases={2: 0},
        compiler_params=_params(("parallel",)),
    )(my_idx, shard, land)


def _push_start(name, srcs, lands, axes, views, deps):
    ns = len(srcs)

    def body(*refs):
        src_refs, land_refs = refs[:ns], refs[ns:2 * ns]
        send_sems, recv_sems = refs[2 * ns + len(deps):2 * ns + len(deps) + 2]
        token = refs[-1]
        me = _my_position()
        for k in PEER_BITS:
            for a in range(ns):
                s, d, _ = views(me, k, a, src_refs, land_refs, axes)
                pltpu.make_async_remote_copy(
                    src_ref=s, dst_ref=d, send_sem=send_sems.at[7 * a + k - 1], recv_sem=recv_sems.at[7 * a + k - 1],
                    device_id=_flip(me, k), device_id_type=MESH).start()
        token[...] = jnp.zeros_like(token)

    thru = [pltpu.HBM(a.shape, a.dtype) for a in list(srcs) + list(lands)]
    outs = pl.pallas_call(
        body, name=name,
        in_specs=[HBM_SPEC] * (2 * ns) + [ANY] * len(deps),
        out_specs=[SEM_SPEC, SEM_SPEC] + [HBM_SPEC] * (2 * ns) + [VMEM_SPEC],
        out_shape=[pltpu.SemaphoreType.DMA((7 * ns,)), pltpu.SemaphoreType.DMA((7 * ns,))] + thru
        + [jax.ShapeDtypeStruct((8, 128), F32)],
        input_output_aliases={i: 2 + i for i in range(2 * ns)},
        compiler_params=pltpu.CompilerParams(has_side_effects=DATAFLOW),
    )(*[_in_hbm(a) for a in srcs], *[_in_hbm(a) for a in lands], *deps)
    return (outs[0], outs[1]), outs[2:2 + ns], outs[2 + ns:2 + 2 * ns], outs[-1]


def _push_wait(name, sems, srcs, lands, axes, views, after):
    ns = len(srcs)

    def body(*refs):
        src_refs, land_refs = refs[:ns], refs[ns:2 * ns]
        send_sems, recv_sems = refs[2 * ns:2 * ns + 2]
        me = _my_position()
        for k in PEER_BITS:
            for a in range(ns):
                s, d, got = views(me, k, a, src_refs, land_refs, axes)
                cp = pltpu.make_async_remote_copy(
                    src_ref=s, dst_ref=got, send_sem=send_sems.at[7 * a + k - 1], recv_sem=recv_sems.at[7 * a + k - 1],
                    device_id=_flip(me, k), device_id_type=MESH)
                cp.wait_send()
                cp.wait_recv()

    thru = [pltpu.HBM(a.shape, a.dtype) for a in list(srcs) + list(lands)]
    outs = pl.pallas_call(
        body, name=name,
        in_specs=[HBM_SPEC] * (2 * ns) + [SEM_SPEC, SEM_SPEC, ANY],
        out_specs=[HBM_SPEC] * (2 * ns),
        out_shape=thru,
        input_output_aliases={i: i for i in range(2 * ns)},
        compiler_params=pltpu.CompilerParams(has_side_effects=DATAFLOW),
    )(*srcs, *lands, sems[0], sems[1], after)
    return outs[:ns], outs[ns:]


def _small_allgather(v, name):
    r, c = v.shape

    def body(v_ref, all_ref, sum_ref, send_sems, recv_sems):
        me = _my_position()
        all_ref[_linear(me)] = v_ref[...]
        copies = []
        for k in range(1, N_DEV):
            peer = _flip(me, k)
            copies.append(pltpu.make_async_remote_copy(
                src_ref=v_ref, dst_ref=all_ref.at[_linear(me)], send_sem=send_sems.at[k - 1], recv_sem=recv_sems.at[k - 1],
                device_id=peer, device_id_type=MESH))
        for cp in copies:
            cp.start()
        for k in range(1, N_DEV):
            peer = _flip(me, k)
            pltpu.make_async_remote_copy(
                src_ref=v_ref, dst_ref=all_ref.at[_linear(peer)], send_sem=send_sems.at[k - 1], recv_sem=recv_sems.at[k - 1],
                device_id=peer, device_id_type=MESH).wait_recv()
        for cp in copies:
            cp.wait_send()
        acc = all_ref[0]
        for d in range(1, N_DEV):
            acc = acc + all_ref[d]
        sum_ref[...] = acc

    return pl.pallas_call(
        body,
        name=name,
        in_specs=[VMEM_SPEC],
        out_specs=[VMEM_SPEC, VMEM_SPEC],
        out_shape=[jax.ShapeDtypeStruct((N_DEV, r, c), F32), jax.ShapeDtypeStruct((r, c), F32)],
        scratch_shapes=[pltpu.SemaphoreType.DMA((N_DEV - 1,)), pltpu.SemaphoreType.DMA((N_DEV - 1,))],
        compiler_params=pltpu.CompilerParams(has_side_effects=True, vmem_limit_bytes=VMEM_LIMIT),
    )(v)


def _ada_fwd_call(cin, ada_w, ada_b_cols):
    nl, d, ncol = ada_w.shape
    nrow = cin.shape[0]

    def body(c_ref, w_ref, b_ref, o_ref):
        cs = _silu(c_ref[...]).astype(BF16)
        for l in range(nl):
            o_ref[l] = _dot(cs, w_ref[l].astype(BF16)) + b_ref[l]

    return pl.pallas_call(
        body, name="ada_fwd", in_specs=[VMEM_SPEC] * 3, out_specs=VMEM_SPEC,
        out_shape=jax.ShapeDtypeStruct((nl, nrow, ncol), F32), compiler_params=_params(),
    )(cin, ada_w, ada_b_cols)


def _ada_bwd_call(cin, ada_w, dmod):
    nl, d, ncol = ada_w.shape
    nrow = cin.shape[0]

    def body(c_ref, w_ref, dm_ref, gw_ref, dcs_ref):
        cs = _silu(c_ref[...]).astype(BF16)
        acc = jnp.zeros((nrow, d), F32)
        for l in range(nl):
            dm = dm_ref[l].astype(BF16)
            gw_ref[l] = _dot_tn(cs, dm)
            acc = acc + _dot_nt(dm, w_ref[l].astype(BF16))
        dcs_ref[...] = acc

    return pl.pallas_call(
        body, name="ada_bwd", in_specs=[VMEM_SPEC] * 3, out_specs=[VMEM_SPEC, VMEM_SPEC],
        out_shape=[jax.ShapeDtypeStruct((nl, d, ncol), F32), jax.ShapeDtypeStruct((nrow, d), F32)],
        compiler_params=_params(),
    )(cin, ada_w, dmod)


def _adamw_math(w, g, m, v):
    m = ADAM_B1 * m + (1.0 - ADAM_B1) * g
    v = ADAM_B2 * v + (1.0 - ADAM_B2) * jnp.square(g)
    m_hat = m / (1.0 - ADAM_B1 ** ADAM_STEP)
    v_hat = v / (1.0 - ADAM_B2 ** ADAM_STEP)
    delta = -ADAM_LR * (m_hat / (jnp.sqrt(v_hat) + ADAM_EPS) + ADAM_WD * w)
    return delta, m, v


def _adamw_sharded(w, m, v, mine, slabs, axis, my_idx, layer, prev, name):
    nl, r, c = w.shape
    tm = _pick(r, (128, 64, 32, 16))
    nprev = 0 if prev is None else len(prev)

    def body(idx_ref, w_ref, m_ref, v_ref, mine_ref, s_ref, *rest):
        g_ref, d_ref, nm_ref, nv_ref = rest[nprev:]
        g = mine_ref[...].astype(F32)
        for k in range(N_DEV - 1):
            g = g + s_ref[k].astype(F32)
        delta, nm, nv = _adamw_math(w_ref[0], g, m_ref[0], v_ref[0])
        g_ref[0], d_ref[0], nm_ref[0], nv_ref[0] = g, delta, nm, nv

    spec = pl.BlockSpec((1, tm, c), lambda i, idx: (layer, i, 0))
    out = jax.ShapeDtypeStruct(w.shape, F32)
    return pl.pallas_call(
        body, name=name,
        grid_spec=pltpu.PrefetchScalarGridSpec(
            num_scalar_prefetch=1, grid=(r // tm,),
            in_specs=[spec, spec, spec, _slab_block(r, c, tm, axis),
                      pl.BlockSpec((N_DEV - 1, tm, c), lambda i, idx: (0, i, 0))] + [ANY] * nprev,
            out_specs=[spec] * 4),
        out_shape=[out] * 4,
        input_output_aliases={6 + j: j for j in range(nprev)},
        compiler_params=_params(("parallel",)),
    )(my_idx, w, m, v, mine, slabs, *(() if prev is None else prev))


def _adamw_dense(w, g, m, v, name):
    r, c = w.shape
    tm = _pick(r, (256, 128, 64, 32, 16, 8))

    def body(w_ref, g_ref, m_ref, v_ref, d_ref, nm_ref, nv_ref):
        d_ref[...], nm_ref[...], nv_ref[...] = _adamw_math(w_ref[...], g_ref[...], m_ref[...], v_ref[...])

    spec = pl.BlockSpec((tm, c), lambda i: (i, 0))
    out = jax.ShapeDtypeStruct(w.shape, F32)
    return pl.pallas_call(
        body, name=name, grid=(r // tm,), in_specs=[spec] * 4, out_specs=[spec] * 3, out_shape=[out] * 3,
        compiler_params=_params(("parallel",)),
    )(w, g, m, v)


def _pack(parts, width=128):
    flat = jnp.concatenate([p.reshape(-1).astype(F32) for p in parts])
    n = flat.shape[0]
    total = -(-n // (8 * width)) * (8 * width)
    return jnp.pad(flat, (0, total - n)).reshape(total // width, width)


def _unpack(buf, shapes):
    flat = buf.reshape(-1)
    out, off = [], 0
    for s in shapes:
        n = int(np.prod(s))
        out.append(flat[off:off + n].reshape(s))
        off += n
    return out


def kernel(x, c, ctx, c_ctx, ada_w, ada_b, norm_g, w_in, na_rpb, ret_decay_logit, w_proj_na, w_proj_ret, w_out, final_g, loss_target, m_c_ctx, m_ada_w, m_ada_b, m_norm_g, m_w_in, m_na_rpb, m_ret_decay_logit, m_w_proj_na, m_w_proj_ret, m_w_out, m_final_g, v_c_ctx, v_ada_w, v_ada_b, v_norm_g, v_w_in, v_na_rpb, v_ret_decay_logit, v_w_proj_na, v_w_proj_ret, v_w_out, v_final_g):
    depth = w_in.shape[0]
    n_lat, d = x.shape[1], x.shape[2]
    n_ctx = ctx.shape[1]
    t = n_lat + n_ctx
    w_na = w_proj_na.shape[1]
    w_retv = w_proj_ret.shape[1] * N_DEV
    in_cols = w_in.shape[2] * N_DEV
    w_qk = (in_cols - 4 * w_na - 2 * w_retv - 2 * d) // 2
    sizes = (w_na, w_na, w_na, w_na, w_qk, w_qk, w_retv, w_retv, d, d)
    off = tuple(int(o) for o in np.cumsum((0,) + sizes))
    NA_Q, NA_K, NA_V, NA_Z, RET_Q, RET_K, RET_V, RET_Z, G_NA, G_RET = range(10)
    rows = n_lat // GRID_W
    me = _my_position()
    my_idx = _linear(me)
    tm_row = _pick(n_ctx, (256, 128))

    idx_arr = jnp.reshape(my_idx, (1,)).astype(jnp.int32)

    ncol = ada_w.shape[2]
    c_all, _ = _small_allgather(jnp.pad(c, ((0, 7), (0, 0))), "allgather_c")
    cin = jnp.concatenate([c_all[:, 0, :], c_ctx[None, :], jnp.zeros((7, d), F32)], axis=0)
    ada_b_cols = lax.dynamic_slice_in_dim(ada_b, my_idx * ncol, ncol, axis=1)[:, None, :]
    mod_cols = _ada_fwd_call(cin, ada_w, ada_b_cols)
    mod_gathered, _ = _small_allgather(mod_cols.reshape(depth * 16, ncol), "allgather_mod")
    mod_all = mod_gathered.reshape(N_DEV, depth, 16, ncol).transpose(1, 2, 0, 3).reshape(depth, 16, N_DEV * ncol)
    mod_lat = lax.dynamic_index_in_dim(mod_all, my_idx, axis=1, keepdims=False)
    mod_ctx = mod_all[:, 8, :]

    w_axes = (1, 1, 0, 0)
    w_names = ("w_in", "w_proj_na", "w_proj_ret", "w_out")
    shard = [[w[l].astype(BF16) for w in (w_in, w_proj_na, w_proj_ret, w_out)] for l in range(depth)]
    groups = [[(0, 0)], [(0, 1), (0, 2), (0, 3)]] + [[(l, a) for a in range(4)] for l in range(1, depth)]
    gathers, token = {}, mod_gathered
    for gi, keys in enumerate(groups):
        srcs = [shard[l][a] for l, a in keys]
        axes = tuple(w_axes[a] for _, a in keys)
        lands = [_place_shard(s, lax.empty(tuple(n * (N_DEV if i == ax else 1) for i, n in enumerate(s.shape)), BF16),
                              ax, idx_arr) for s, ax in zip(srcs, axes)]
        sems, srcs, lands, token = _push_start(f"gather_start_{gi}", srcs, lands, axes, _gather_views, (token,))
        flight = dict(name=f"gather_wait_{gi}", sems=sems, srcs=srcs, lands=lands, axes=axes, ready=None)
        for pos, key in enumerate(keys):
            gathers[key] = (flight, pos)

    def landed(l, a, act):
        flight, pos = gathers[(l, a)]
        if flight["ready"] is None:
            flight["ready"] = _push_wait(flight["name"], flight["sems"], flight["srcs"], flight["lands"],
                                         flight["axes"], _gather_views, act)[1]
        return flight["ready"][pos]

    pending, scatters = {}, []

    def send_dw(l, a, dw):
        pending[(l, a)] = dw
        if a == 0:
            keys = [(0, 0)] if l == 0 else [(l, b) for b in range(4)]
        elif l == 0 and a == 1:
            keys = [(0, 1), (0, 2), (0, 3)]
        else:
            return None
        srcs = [pending[k] for k in keys]
        axes = tuple(w_axes[b] for _, b in keys)
        lands = [lax.empty((N_DEV - 1,) + tuple(n // (N_DEV if i == ax else 1) for i, n in enumerate(s.shape)), BF16)
                 for s, ax in zip(srcs, axes)]
        sems, srcs, lands, tok = _push_start(f"scatter_start_{len(scatters)}", srcs, lands, axes, _scatter_views, ())
        scatters.append(dict(name=f"scatter_wait_{len(scatters)}", sems=sems, srcs=srcs, lands=lands, axes=axes, keys=keys))
        return tok

    cos, sin = _rope_tables(t, n_lat)
    k_scale = RET_KEY_DIM ** -0.5
    norm_mod_fwd, norm_mod_bwd = _make_rowwise(_f_norm_mod, "norm_mod", (BF16,), (d,), n_lat, tm_row, (0,))
    gate_na_fwd, gate_na_bwd = _make_rowwise(_f_gate_na, "gate_na", (BF16,), (w_na,), n_lat, tm_row, (0, 1))
    merge_fwd, merge_bwd = _make_rowwise(_f_merge, "merge", (BF16,), (d,), n_lat, tm_row, (0, 1, 2, 3))
    residual_fwd, residual_bwd = _make_rowwise(_f_residual, "residual", (F32,), (d,), n_lat, tm_row, (0, 1))
    loss_fwd, loss_bwd = _make_rowwise(_f_loss, "loss_head", (F32,), (128,), n_lat, tm_row, (0,))

    def pair(a, b):
        return jnp.stack([a, b])[:, None, :]

    def mod_vectors(mod_lat_l, mod_ctx_l, norm_g_l):
        shift, scale, gate = jnp.split(mod_lat_l, 3)
        c_shift, c_scale, c_gate = jnp.split(mod_ctx_l, 3)
        return pair(norm_g_l, norm_g_l), pair(scale, c_scale), pair(shift, c_shift), pair(gate, c_gate)

    def split_u(u):
        blk = [u[:, off[i]:off[i + 1]] for i in range(10)]
        blk[RET_Q] = _rope(blk[RET_Q], cos, sin, 1.0)
        blk[RET_K] = _rope(blk[RET_K], cos, sin, k_scale)
        return tuple(blk)

    def log_decay(logit):
        return jax.nn.log_sigmoid(logit.astype(F32))

    xa = jnp.concatenate([x[0], ctx[0]], axis=0)
    saved = []
    for l in range(depth):
        vecs, vecs_vjp = jax.vjp(mod_vectors, mod_lat[l], mod_ctx[l], norm_g[l])
        (h,) = norm_mod_fwd((xa,), vecs[:3])
        wl_in = landed(l, 0, h)
        u = _matmul(h, wl_in, out_dtype=BF16, name="in_proj_fwd")
        blk, split_vjp = jax.vjp(split_u, u)
        bt, bt_vjp = jax.vjp(lambda r: _na_bias_table(r, rows), na_rpb[l])
        lam, lam_vjp = jax.vjp(log_decay, ret_decay_logit[l])
        o_na = _na_fwd_call(blk[NA_Q], blk[NA_K], blk[NA_V], bt, n_lat)
        o_f, st_f = _ret_fwd_call(blk[RET_Q], blk[RET_K], blk[RET_V], lam[0], n_lat, False)
        o_b, st_b = _ret_fwd_call(blk[RET_Q], blk[RET_K], blk[RET_V], lam[1], n_lat, True)
        (a_na,) = gate_na_fwd((o_na, blk[NA_Z]), ())
        a_ret = _gate_ret_fwd_call(o_f, o_b, blk[RET_Z], tm_row)
        wl_pna, wl_pret, wl_out = landed(l, 1, a_na), landed(l, 2, a_na), landed(l, 3, a_na)
        y_na = _matmul(a_na, wl_pna, out_dtype=F32, name="proj_na_fwd")
        y_ret = _matmul(a_ret, wl_pret, out_dtype=F32, name="proj_ret_fwd")
        (merged,) = merge_fwd((blk[G_NA], blk[G_RET], y_na, y_ret), ())
        out = _matmul(merged, wl_out, out_dtype=F32, name="out_proj_fwd")
        (xa_next,) = residual_fwd((xa, out), vecs[3:])
        saved.append(dict(xa=xa, vecs=vecs, vecs_vjp=vecs_vjp, h=h, w=(wl_in, wl_pna, wl_pret, wl_out), blk=blk,
                          split_vjp=split_vjp, bt=bt, bt_vjp=bt_vjp, lam=lam, lam_vjp=lam_vjp, o_na=o_na, o_f=o_f,
                          o_b=o_b, st_f=st_f, st_b=st_b, a_na=a_na, a_ret=a_ret, y_na=y_na, y_ret=y_ret,
                          merged=merged, out=out))
        xa = xa_next

    fg_pair, fg_vjp = jax.vjp(lambda g: pair(g, g), final_g)
    x_last = xa[:n_lat]
    (loss_rows,) = loss_fwd((x_last, loss_target[0]), (fg_pair,))
    loss = lax.psum(jnp.sum(loss_rows), ("x", "y", "c"))
    (dx_last,), (d_fg_pair,) = loss_bwd((x_last, loss_target[0]), (fg_pair,), (jnp.ones_like(loss_rows),))
    (d_final_g,) = fg_vjp(d_fg_pair)
    dxa = jnp.pad(dx_last, ((0, n_ctx), (0, 0)))

    d_mod_lat, d_mod_ctx, d_norm_g, d_rpb, d_decay = ([None] * depth for _ in range(5))
    for l in reversed(range(depth)):
        s = saved[l]
        blk = s["blk"]
        wl_in, wl_pna, wl_pret, wl_out = s["w"]
        (dxa_res, d_out), (d_gate,) = residual_bwd((s["xa"], s["out"]), s["vecs"][3:], (dxa,))
        d_out = d_out.astype(BF16)
        send_dw(l, 3, _matmul(s["merged"], d_out, trans_a=True, out_dtype=BF16, name="out_proj_dw"))
        d_merged = _matmul(d_out, wl_out, trans_b=True, out_dtype=BF16, name="out_proj_da")
        (dg_na, dg_ret, dy_na, dy_ret), _ = merge_bwd((blk[G_NA], blk[G_RET], s["y_na"], s["y_ret"]), (), (d_merged,))
        dy_na, dy_ret = dy_na.astype(BF16), dy_ret.astype(BF16)
        send_dw(l, 2, _matmul(s["a_ret"], dy_ret, trans_a=True, out_dtype=BF16, name="proj_ret_dw"))
        da_ret = _matmul(dy_ret, wl_pret, trans_b=True, out_dtype=BF16, name="proj_ret_da")
        tok = send_dw(l, 1, _matmul(s["a_na"], dy_na, trans_a=True, out_dtype=BF16, name="proj_na_dw"))
        da_na = _matmul(dy_na, wl_pna, trans_b=True, out_dtype=BF16, name="proj_na_da", after=tok)
        do_ret, dz_ret = _gate_ret_bwd_call(s["o_f"], s["o_b"], blk[RET_Z], da_ret, tm_row)
        (do_na, dz_na), _ = gate_na_bwd((s["o_na"], blk[NA_Z]), (), (da_na,))
        dq_f, dk_f, dv_f, dl_f = _ret_bwd_call(blk[RET_Q], blk[RET_K], blk[RET_V], s["lam"][0], s["st_f"], do_ret, n_lat, False)
        dq_b, dk_b, dv_b, dl_b = _ret_bwd_call(blk[RET_Q], blk[RET_K], blk[RET_V], s["lam"][1], s["st_b"], do_ret, n_lat, True)
        dq, dk, dv, dbt = _na_bwd_call(blk[NA_Q], blk[NA_K], blk[NA_V], s["bt"], do_na, n_lat)
        d_blk = [None] * 10
        d_blk[NA_Q], d_blk[NA_K], d_blk[NA_V], d_blk[NA_Z] = dq, dk.astype(BF16), dv.astype(BF16), dz_na
        d_blk[RET_Q], d_blk[RET_K], d_blk[RET_V], d_blk[RET_Z] = dq_f + dq_b, dk_f + dk_b, dv_f + dv_b, dz_ret
        d_blk[G_NA], d_blk[G_RET] = dg_na, dg_ret
        (du,) = s["split_vjp"](tuple(d_blk))
        (d_rpb[l],) = s["bt_vjp"](dbt)
        (d_decay[l],) = s["lam_vjp"](jnp.stack([dl_f[:, 0, 0], dl_b[:, 0, 0]]))
        tok = send_dw(l, 0, _matmul(s["h"], du, trans_a=True, out_dtype=BF16, name="in_proj_dw"))
        dh = _matmul(du, wl_in, trans_b=True, out_dtype=BF16, name="in_proj_da", after=tok)
        (dxa_norm,), d_vecs = norm_mod_bwd((s["xa"],), s["vecs"][:3], (dh,))
        dxa = dxa_res + dxa_norm
        d_mod_lat[l], d_mod_ctx[l], d_norm_g[l] = s["vecs_vjp"](tuple(d_vecs) + (d_gate,))
    gx = dxa[:n_lat]
    d_mod_lat, d_mod_ctx, d_norm_g, d_rpb, d_decay = (jnp.stack(a) for a in (d_mod_lat, d_mod_ctx, d_norm_g, d_rpb, d_decay))

    small_shapes = [d_mod_lat.shape, d_mod_ctx.shape, d_norm_g.shape, d_final_g.shape, d_rpb.shape, d_decay.shape]
    packed = _pack([d_mod_lat, d_mod_ctx, d_norm_g, d_final_g, d_rpb, d_decay])
    g_all, g_sum = _small_allgather(packed, "allgather_small_grads")
    dml_sum, dmc_sum, grad_norm_g, grad_final_g, grad_na_rpb, grad_decay = _unpack(g_sum, small_shapes)
    grad_ada_b = dml_sum + dmc_sum
    dml_all = g_all.reshape(N_DEV, -1)[:, :depth * 3 * d].reshape(N_DEV, depth, 3 * d)

    def my_cols(a):
        return lax.dynamic_slice_in_dim(a, my_idx * ncol, ncol, axis=a.ndim - 1)

    dmod = jnp.concatenate(
        [my_cols(dml_all).transpose(1, 0, 2), my_cols(dmc_sum)[:, None, :], jnp.zeros((depth, 7, ncol), F32)], axis=1)
    grad_ada_w, dcs_part = _ada_bwd_call(cin, ada_w, dmod)
    _, dcs = _small_allgather(dcs_part, "allgather_dcsilu")
    sg = jax.nn.sigmoid(c_ctx)
    grad_c_ctx = dcs[8] * (sg * (1.0 + c_ctx * (1.0 - sg)))

    def flat2(a):
        return a.reshape(a.shape[0] * a.shape[1], a.shape[2])

    small_w = [c_ctx, ada_b, norm_g, na_rpb, ret_decay_logit, final_g]
    small_g = [grad_c_ctx, grad_ada_b, grad_norm_g, grad_na_rpb, grad_decay, grad_final_g]
    small_m = [m_c_ctx, m_ada_b, m_norm_g, m_na_rpb, m_ret_decay_logit, m_final_g]
    small_v = [v_c_ctx, v_ada_b, v_norm_g, v_na_rpb, v_ret_decay_logit, v_final_g]
    shp = [a.shape for a in small_w]
    ds_, nms_, nvs_ = _adamw_dense(_pack(small_w), _pack(small_g), _pack(small_m), _pack(small_v), "adamw_small")
    ds_, nms_, nvs_ = _unpack(ds_, shp), _unpack(nms_, shp), _unpack(nvs_, shp)

    d_ada, nm_ada, nv_ada = [a.reshape(ada_w.shape) for a in _adamw_dense(
        flat2(ada_w), flat2(grad_ada_w), flat2(m_ada_w), flat2(v_ada_w), "adamw_ada_w")]

    w_all = (w_in, w_proj_na, w_proj_ret, w_out)
    m_all = (m_w_in, m_w_proj_na, m_w_proj_ret, m_w_out)
    v_all = (v_w_in, v_w_proj_na, v_w_proj_ret, v_w_out)
    upd = [None] * 4
    after = d_ada
    for flight in scatters:
        mine, slabs = _push_wait(flight["name"], flight["sems"], flight["srcs"], flight["lands"], flight["axes"],
                                 _scatter_views, after)
        for (l, a), own, s in zip(flight["keys"], mine, slabs):
            upd[a] = _adamw_sharded(w_all[a], m_all[a], v_all[a], own, s, w_axes[a], idx_arr, l, upd[a],
                                    "adamw_" + w_names[a])
            after = upd[a][1]
    (g_w_in, d_w_in, nm_w_in, nv_w_in), (g_pna, d_pna, nm_pna, nv_pna) = upd[0], upd[1]
    (g_pret, d_pret, nm_pret, nv_pret), (g_out, d_out, nm_out, nv_out) = upd[2], upd[3]

    def order(cc, aw, ab, ng, wi, rp, dl, pn, pr, wo, fg):
        return [cc, aw, ab, ng, wi, rp, dl, pn, pr, wo, fg]

    grads_out = order(grad_c_ctx, grad_ada_w, grad_ada_b, grad_norm_g, g_w_in, grad_na_rpb, grad_decay, g_pna, g_pret, g_out, grad_final_g)
    delta_out = order(ds_[0], d_ada, ds_[1], ds_[2], d_w_in, ds_[3], ds_[4], d_pna, d_pret, d_out, ds_[5])
    m_out = order(nms_[0], nm_ada, nms_[1], nms_[2], nm_w_in, nms_[3], nms_[4], nm_pna, nm_pret, nm_out, nms_[5])
    v_out = order(nvs_[0], nv_ada, nvs_[1], nvs_[2], nv_w_in, nvs_[3], nvs_[4], nv_pna, nv_pret, nv_out, nvs_[5])
    return (loss, gx[None], *grads_out, *delta_out, *m_out, *v_out)
```

```python
import functools

import numpy as np
import jax
import jax.numpy as jnp
from jax import lax
from jax.experimental import pallas as pl
from jax.experimental.pallas import tpu as pltpu

F32 = jnp.float32
BF16 = jnp.bfloat16

N_DEV = 8
GRID_W = 64
NA_HEAD_DIM = 128
NA_WIN_ROWS = 8
NA_WIN_COLS = 16
RET_KEY_DIM = 128
RET_VAL_DIM = 256
RET_CHUNK = 128
ROPE_BASE = 10000.0
NORM_EPS = 1e-6
MASK_VALUE = -1e30

ADAM_LR = 0.001
ADAM_B1 = 0.9
ADAM_B2 = 0.999
ADAM_EPS = 1e-08
ADAM_WD = 0.01
ADAM_STEP = 10

VMEM_LIMIT = 48 * 1024 * 1024
MESH = pl.DeviceIdType.MESH
ANY = pl.BlockSpec(memory_space=pl.ANY)
VMEM_SPEC = pl.BlockSpec(memory_space=pltpu.VMEM)


def _params(sem=None):
    return pltpu.CompilerParams(dimension_semantics=sem, vmem_limit_bytes=VMEM_LIMIT)


def _pick(n, prefs):
    for p in prefs:
        if n % p == 0:
            return p
    return n


def _dot(a, b):
    return lax.dot_general(a, b, (((1,), (0,)), ((), ())), preferred_element_type=F32)


def _dot_nt(a, b):
    return lax.dot_general(a, b, (((1,), (1,)), ((), ())), preferred_element_type=F32)


def _dot_tn(a, b):
    return lax.dot_general(a, b, (((0,), (0,)), ((), ())), preferred_element_type=F32)


def _silu(x):
    return x * jax.nn.sigmoid(x)


def _matmul(a, b, *, trans_a=False, trans_b=False, out_dtype=F32, name="matmul", after=None):
    if trans_a:
        kdim, m = a.shape
    else:
        m, kdim = a.shape
    if trans_b:
        n, kb = b.shape
    else:
        kb, n = b.shape
    assert kdim == kb, (a.shape, b.shape, trans_a, trans_b)
    tm = _pick(m, (1152, 1024, 768, 512, 256, 128))
    tn = _pick(n, (512, 256, 128))
    tk = _pick(kdim, (2304, 2048, 1024, 512, 256, 128))
    nk = kdim // tk
    dn = (((0 if trans_a else 1,), (1 if trans_b else 0,)), ((), ()))

    def body(a_ref, b_ref, *rest):
        o_ref, acc_ref = rest[-2:]
        part = lax.dot_general(a_ref[...], b_ref[...], dn, preferred_element_type=F32)
        if nk == 1:
            o_ref[...] = part.astype(o_ref.dtype)
        else:
            k = pl.program_id(2)

            @pl.when(k == 0)
            def _():
                acc_ref[...] = part

            @pl.when(k > 0)
            def _():
                acc_ref[...] += part

            @pl.when(k == nk - 1)
            def _():
                o_ref[...] = acc_ref[...].astype(o_ref.dtype)

    a_spec = pl.BlockSpec((tk, tm), lambda i, j, k: (k, i)) if trans_a else pl.BlockSpec((tm, tk), lambda i, j, k: (i, k))
    b_spec = pl.BlockSpec((tn, tk), lambda i, j, k: (j, k)) if trans_b else pl.BlockSpec((tk, tn), lambda i, j, k: (k, j))
    return pl.pallas_call(
        body,
        name=name,
        grid=(m // tm, n // tn, nk),
        in_specs=[a_spec, b_spec] + ([] if after is None else [ANY]),
        out_specs=pl.BlockSpec((tm, tn), lambda i, j, k: (i, j)),
        out_shape=jax.ShapeDtypeStruct((m, n), out_dtype),
        scratch_shapes=[pltpu.VMEM((tm, tn) if nk > 1 else (8, 128), F32)],
        compiler_params=_params(("parallel", "parallel", "arbitrary")),
    )(*((a, b) if after is None else (a, b, after)))


def _make_rowwise(f, name, out_dtypes, out_cols, n_lat, tm, diff_rows):
    def tile_fn(*args):
        return tuple(o.astype(dt) for o, dt in zip(f(*args), out_dtypes))

    def fwd_call(rows, vecs):
        t = rows[0].shape[0]
        nr, nv = len(rows), len(vecs)
        nl = n_lat // tm

        def body(*refs):
            grp = (pl.program_id(0) >= nl).astype(jnp.int32)
            args = [r[...] for r in refs[:nr]] + [v[grp] for v in refs[nr:nr + nv]]
            for o_ref, o in zip(refs[nr + nv:], tile_fn(*args)):
                o_ref[...] = o

        return pl.pallas_call(
            body,
            name=name + "_fwd",
            grid=(t // tm,),
            in_specs=[pl.BlockSpec((tm, r.shape[1]), lambda i: (i, 0)) for r in rows]
            + [pl.BlockSpec(v.shape, lambda i: (0, 0, 0)) for v in vecs],
            out_specs=[pl.BlockSpec((tm, c), lambda i: (i, 0)) for c in out_cols],
            out_shape=[jax.ShapeDtypeStruct((t, c), dt) for c, dt in zip(out_cols, out_dtypes)],
            compiler_params=_params(("parallel",)),
        )(*rows, *vecs)

    def bwd_call(rows, vecs, gs):
        t = rows[0].shape[0]
        nr, nv, ng = len(rows), len(vecs), len(gs)
        nl = n_lat // tm
        nd = len(diff_rows)

        def body(*refs):
            i = pl.program_id(0)
            grp = (i >= nl).astype(jnp.int32)
            args = [r[...] for r in refs[:nr]] + [v[grp] for v in refs[nr:nr + nv]]
            g_refs = refs[nr + nv:nr + nv + ng]
            drow_refs = refs[nr + nv + ng:nr + nv + ng + nd]
            dvec_refs = refs[nr + nv + ng + nd:]
            _, vjp = jax.vjp(tile_fn, *args)
            grads = vjp(tuple(g[...] for g in g_refs))
            for d_ref, k in zip(drow_refs, diff_rows):
                d_ref[...] = grads[k].astype(d_ref.dtype)

            @pl.when(i == 0)
            def _():
                for d_ref in dvec_refs:
                    d_ref[...] = jnp.zeros_like(d_ref)

            for j, d_ref in enumerate(dvec_refs):
                d_ref[grp] += grads[nr + j]

        outs = pl.pallas_call(
            body,
            name=name + "_bwd",
            grid=(t // tm,),
            in_specs=[pl.BlockSpec((tm, r.shape[1]), lambda i: (i, 0)) for r in rows]
            + [pl.BlockSpec(v.shape, lambda i: (0, 0, 0)) for v in vecs]
            + [pl.BlockSpec((tm, g.shape[1]), lambda i: (i, 0)) for g in gs],
            out_specs=[pl.BlockSpec((tm, rows[k].shape[1]), lambda i: (i, 0)) for k in diff_rows]
            + [pl.BlockSpec(v.shape, lambda i: (0, 0, 0)) for v in vecs],
            out_shape=[jax.ShapeDtypeStruct(rows[k].shape, rows[k].dtype) for k in diff_rows]
            + [jax.ShapeDtypeStruct(v.shape, F32) for v in vecs],
            compiler_params=_params(("arbitrary",)),
        )(*rows, *vecs, *gs)
        return outs[:nd], outs[nd:]

    return fwd_call, bwd_call


def _f_norm_mod(x, g, scale, shift):
    r = lax.rsqrt(jnp.mean(x * x, axis=-1, keepdims=True) + NORM_EPS)
    return ((x * r * g) * (1.0 + scale) + shift,)


def _f_gate_na(o, z):
    return (o.astype(F32) * _silu(z.astype(F32)),)


def _f_merge(g_na, g_ret, y_na, y_ret):
    return (jax.nn.sigmoid(g_na.astype(F32)) * y_na + jax.nn.sigmoid(g_ret.astype(F32)) * y_ret,)


def _f_residual(x, out, gate):
    return (x + gate * out,)


def _f_loss(x, target, g):
    r = lax.rsqrt(jnp.mean(x * x, axis=-1, keepdims=True) + NORM_EPS)
    y = x * r * g
    e = 0.5 * jnp.mean(jnp.square(y - target), axis=-1, keepdims=True)
    return (jnp.broadcast_to(e * (1.0 / 128.0), (x.shape[0], 128)),)


def _gate_ret_fwd_call(of, ob, z, tm):
    t, w = of.shape
    nh = w // RET_VAL_DIM

    def body(of_ref, ob_ref, z_ref, a_ref):
        for hh in range(nh):
            sl = slice(hh * RET_VAL_DIM, (hh + 1) * RET_VAL_DIM)
            o = of_ref[:, sl] + ob_ref[:, sl]
            r = lax.rsqrt(jnp.mean(o * o, axis=-1, keepdims=True) + NORM_EPS)
            a_ref[:, sl] = ((o * r) * _silu(z_ref[:, sl].astype(F32))).astype(a_ref.dtype)

    spec = pl.BlockSpec((tm, w), lambda i: (i, 0))
    return pl.pallas_call(
        body, name="gate_ret_fwd", grid=(t // tm,), in_specs=[spec, spec, spec], out_specs=spec,
        out_shape=jax.ShapeDtypeStruct((t, w), BF16), compiler_params=_params(("parallel",)),
    )(of, ob, z)


def _gate_ret_bwd_call(of, ob, z, da, tm):
    t, w = of.shape
    nh = w // RET_VAL_DIM

    def body(of_ref, ob_ref, z_ref, da_ref, do_ref, dz_ref):
        for hh in range(nh):
            sl = slice(hh * RET_VAL_DIM, (hh + 1) * RET_VAL_DIM)
            o = of_ref[:, sl] + ob_ref[:, sl]
            r = lax.rsqrt(jnp.mean(o * o, axis=-1, keepdims=True) + NORM_EPS)
            n = o * r
            zf = z_ref[:, sl].astype(F32)
            sg = jax.nn.sigmoid(zf)
            g = da_ref[:, sl].astype(F32)
            dn = g * (zf * sg)
            dz_ref[:, sl] = (g * n * (sg * (1.0 + zf * (1.0 - sg)))).astype(dz_ref.dtype)
            do_ref[:, sl] = r * (dn - n * jnp.mean(dn * n, axis=-1, keepdims=True))

    spec = pl.BlockSpec((tm, w), lambda i: (i, 0))
    return pl.pallas_call(
        body, name="gate_ret_bwd", grid=(t // tm,), in_specs=[spec, spec, spec, spec], out_specs=[spec, spec],
        out_shape=[jax.ShapeDtypeStruct((t, w), F32), jax.ShapeDtypeStruct((t, w), z.dtype)],
        compiler_params=_params(("parallel",)),
    )(of, ob, z, da)


def _na_geometry(t, n_lat):
    rows = n_lat // GRID_W
    kh = min(NA_WIN_ROWS, rows)
    return rows, kh, kh * GRID_W, t - n_lat, t // GRID_W


def _na_row0(r, rows, kh):
    return jnp.clip(r - kh // 2, 0, rows - kh)


def _na_bias_idx(r, rows, kh):
    return jnp.clip(_na_row0(r, rows, kh) - r + (NA_WIN_ROWS - 1), 0, NA_WIN_ROWS - 1)


def _na_fwd_call(q, k, v, bt, n_lat):
    t, w = q.shape
    nh = w // NA_HEAD_DIM
    rows, kh, n_loc, n_ctx, nq = _na_geometry(t, n_lat)
    scale = NA_HEAD_DIM ** -0.5

    def body(q_ref, k_ref, v_ref, bt_ref, o_ref):
        r = pl.program_id(1)
        qb = q_ref[...]
        kc = k_ref[pl.ds(n_lat, n_ctx), :]
        vc = v_ref[pl.ds(n_lat, n_ctx), :]
        s_ctx = _dot_nt(qb, kc) * scale

        @pl.when(r < rows)
        def _():
            start = pl.multiple_of(_na_row0(r, rows, kh) * GRID_W, GRID_W)
            kw = k_ref[pl.ds(start, n_loc), :]
            vw = v_ref[pl.ds(start, n_loc), :]
            s_loc = _dot_nt(qb, kw) * scale + bt_ref[0, 0]
            m = jnp.maximum(jnp.max(s_loc, axis=-1, keepdims=True), jnp.max(s_ctx, axis=-1, keepdims=True))
            p_loc = jnp.exp(s_loc - m)
            p_ctx = jnp.exp(s_ctx - m)
            l = jnp.sum(p_loc, axis=-1, keepdims=True) + jnp.sum(p_ctx, axis=-1, keepdims=True)
            o = _dot(p_loc.astype(BF16), vw) + _dot(p_ctx.astype(BF16), vc)
            o_ref[...] = (o / l).astype(o_ref.dtype)

        @pl.when(r >= rows)
        def _():
            m = jnp.max(s_ctx, axis=-1, keepdims=True)
            p = jnp.exp(s_ctx - m)
            l = jnp.sum(p, axis=-1, keepdims=True)
            o_ref[...] = (_dot(p.astype(BF16), vc) / l).astype(o_ref.dtype)

    return pl.pallas_call(
        body,
        name="na_attn_fwd",
        grid=(nh, nq),
        in_specs=[
            pl.BlockSpec((GRID_W, NA_HEAD_DIM), lambda h, r: (r, h)),
            pl.BlockSpec((t, NA_HEAD_DIM), lambda h, r: (0, h)),
            pl.BlockSpec((t, NA_HEAD_DIM), lambda h, r: (0, h)),
            pl.BlockSpec((1, 1, GRID_W, n_loc), lambda h, r: (h, _na_bias_idx(r, rows, kh), 0, 0)),
        ],
        out_specs=pl.BlockSpec((GRID_W, NA_HEAD_DIM), lambda h, r: (r, h)),
        out_shape=jax.ShapeDtypeStruct((t, w), BF16),
        compiler_params=_params(("parallel", "arbitrary")),
    )(q, k, v, bt)


def _na_bwd_call(q, k, v, bt, do, n_lat):
    t, w = q.shape
    nh = w // NA_HEAD_DIM
    rows, kh, n_loc, n_ctx, nq = _na_geometry(t, n_lat)
    scale = NA_HEAD_DIM ** -0.5

    def body(q_ref, k_ref, v_ref, bt_ref, do_ref, dq_ref, dk_ref, dv_ref, dbt_ref):
        r = pl.program_id(1)

        @pl.when(r == 0)
        def _():
            dk_ref[...] = jnp.zeros_like(dk_ref)
            dv_ref[...] = jnp.zeros_like(dv_ref)

        qb = q_ref[...]
        dob = do_ref[...]
        kc = k_ref[pl.ds(n_lat, n_ctx), :]
        vc = v_ref[pl.ds(n_lat, n_ctx), :]
        s_ctx = _dot_nt(qb, kc) * scale
        dp_ctx = _dot_nt(dob, vc)

        @pl.when(r < rows)
        def _():
            start = pl.multiple_of(_na_row0(r, rows, kh) * GRID_W, GRID_W)
            kw = k_ref[pl.ds(start, n_loc), :]
            vw = v_ref[pl.ds(start, n_loc), :]
            s_loc = _dot_nt(qb, kw) * scale + bt_ref[0, 0]
            m = jnp.maximum(jnp.max(s_loc, axis=-1, keepdims=True), jnp.max(s_ctx, axis=-1, keepdims=True))
            p_loc = jnp.exp(s_loc - m)
            p_ctx = jnp.exp(s_ctx - m)
            inv = 1.0 / (jnp.sum(p_loc, axis=-1, keepdims=True) + jnp.sum(p_ctx, axis=-1, keepdims=True))
            p_loc = p_loc * inv
            p_ctx = p_ctx * inv
            dp_loc = _dot_nt(dob, vw)
            delta = jnp.sum(p_loc * dp_loc, axis=-1, keepdims=True) + jnp.sum(p_ctx * dp_ctx, axis=-1, keepdims=True)
            ds_loc = p_loc * (dp_loc - delta)
            ds_ctx = p_ctx * (dp_ctx - delta)
            first = jnp.logical_or(r == 0, _na_bias_idx(r, rows, kh) != _na_bias_idx(r - 1, rows, kh))

            @pl.when(first)
            def _():
                dbt_ref[0, 0] = ds_loc

            @pl.when(jnp.logical_not(first))
            def _():
                dbt_ref[0, 0] += ds_loc

            dsl = (ds_loc * scale).astype(BF16)
            dsc = (ds_ctx * scale).astype(BF16)
            dq_ref[...] = (_dot(dsl, kw) + _dot(dsc, kc)).astype(dq_ref.dtype)
            dk_ref[pl.ds(start, n_loc), :] += _dot_tn(dsl, qb)
            dv_ref[pl.ds(start, n_loc), :] += _dot_tn(p_loc.astype(BF16), dob)
            dk_ref[pl.ds(n_lat, n_ctx), :] += _dot_tn(dsc, qb)
            dv_ref[pl.ds(n_lat, n_ctx), :] += _dot_tn(p_ctx.astype(BF16), dob)

        @pl.when(r >= rows)
        def _():
            m = jnp.max(s_ctx, axis=-1, keepdims=True)
            p = jnp.exp(s_ctx - m)
            p = p * (1.0 / jnp.sum(p, axis=-1, keepdims=True))
            delta = jnp.sum(p * dp_ctx, axis=-1, keepdims=True)
            dsc = (p * (dp_ctx - delta) * scale).astype(BF16)
            dq_ref[...] = _dot(dsc, kc).astype(dq_ref.dtype)
            dk_ref[pl.ds(n_lat, n_ctx), :] += _dot_tn(dsc, qb)
            dv_ref[pl.ds(n_lat, n_ctx), :] += _dot_tn(p.astype(BF16), dob)

    qspec = pl.BlockSpec((GRID_W, NA_HEAD_DIM), lambda h, r: (r, h))
    kspec = pl.BlockSpec((t, NA_HEAD_DIM), lambda h, r: (0, h))
    bspec = pl.BlockSpec((1, 1, GRID_W, n_loc), lambda h, r: (h, _na_bias_idx(r, rows, kh), 0, 0))
    return pl.pallas_call(
        body,
        name="na_attn_bwd",
        grid=(nh, nq),
        in_specs=[qspec, kspec, kspec, bspec, qspec],
        out_specs=[qspec, kspec, kspec, bspec],
        out_shape=[
            jax.ShapeDtypeStruct((t, w), BF16),
            jax.ShapeDtypeStruct((t, w), F32),
            jax.ShapeDtypeStruct((t, w), F32),
            jax.ShapeDtypeStruct(bt.shape, F32),
        ],
        compiler_params=_params(("parallel", "arbitrary")),
    )(q, k, v, bt, do)


def _na_bias_table(rpb, rows):
    kh = min(NA_WIN_ROWS, rows)
    nj = NA_WIN_ROWS
    e1 = np.zeros((nj, kh, 2 * NA_WIN_ROWS - 1), np.float32)
    for j in range(nj):
        for kk in range(kh):
            if j + kk < 2 * NA_WIN_ROWS - 1:
                e1[j, kk, j + kk] = 1.0
    cidx = np.arange(GRID_W)
    dc = np.clip(cidx[None, :] - cidx[:, None] + (NA_WIN_COLS - 1), 0, 2 * NA_WIN_COLS - 2)
    e2 = np.zeros((GRID_W, GRID_W, 2 * NA_WIN_COLS - 1), np.float32)
    e2[np.arange(GRID_W)[:, None], np.arange(GRID_W)[None, :], dc] = 1.0
    c0 = np.clip(cidx - NA_WIN_COLS // 2, 0, GRID_W - NA_WIN_COLS)
    col_in = (cidx[None, :] >= c0[:, None]) & (cidx[None, :] < c0[:, None] + NA_WIN_COLS)
    t1 = jnp.einsum("hab,jka->hjkb", rpb, jnp.asarray(e1), precision=lax.Precision.HIGHEST)
    b = jnp.einsum("hjkb,cwb->hjckw", t1, jnp.asarray(e2), precision=lax.Precision.HIGHEST)
    b = jnp.where(jnp.asarray(col_in)[None, None, :, None, :], b, MASK_VALUE)
    return b.reshape(rpb.shape[0], nj, GRID_W, kh * GRID_W)


def _ret_decays(lam_s, reverse):
    c = RET_CHUNK
    ii = lax.broadcasted_iota(jnp.int32, (c, c), 0)
    jj = lax.broadcasted_iota(jnp.int32, (c, c), 1)
    d = (jj - ii) if reverse else (ii - jj)
    dpos = jnp.maximum(d.astype(F32), 0.0)
    mask = jnp.where(d >= 0, jnp.exp(dpos * lam_s), 0.0)
    pi = lax.broadcasted_iota(jnp.int32, (c, 1), 0).astype(F32)
    qpos = (c - pi) if reverse else (pi + 1.0)
    kpos = pi if reverse else (c - 1.0 - pi)
    qd = jnp.exp(qpos * lam_s)
    kd = jnp.exp(kpos * lam_s)
    g = jnp.exp(jnp.full((1, RET_VAL_DIM), c * lam_s, F32))
    return mask, dpos, qd, kd, qpos, kpos, g


def _ret_chunk_of(t, nt, nl, reverse):
    return (nt - 1 - t) if reverse else (t + nl) % nt


def _ret_fwd_call(qr, kr, v, lam, n_lat, reverse):
    t = qr.shape[0]
    nh = qr.shape[1] // RET_KEY_DIM
    c = RET_CHUNK
    nt, nl = t // c, n_lat // c

    def body(lam_ref, q_ref, k_ref, v_ref, o_ref, s_ref, state):
        h, step = pl.program_id(0), pl.program_id(1)

        @pl.when(step == 0)
        def _():
            state[...] = jnp.zeros_like(state)

        mask, _, qd, kd, _, _, g = _ret_decays(lam_ref[h], reverse)
        q, k, vv = q_ref[...], k_ref[...], v_ref[...]
        p = _dot_nt(q, k) * mask
        s = state[...]
        qs = (q.astype(F32) * qd).astype(BF16)
        o_ref[...] = _dot(p.astype(BF16), vv) + _dot(qs, s.astype(BF16))
        s_ref[0, 0] = s
        ks = (k.astype(F32) * kd).astype(BF16)
        state[...] = s * g + _dot_tn(ks, vv)

    def cmap(h, step, lam_ref):
        return (_ret_chunk_of(step, nt, nl, reverse), h)

    return pl.pallas_call(
        body,
        name="retention_rev_fwd" if reverse else "retention_fwd",
        grid_spec=pltpu.PrefetchScalarGridSpec(
            num_scalar_prefetch=1,
            grid=(nh, nt),
            in_specs=[
                pl.BlockSpec((c, RET_KEY_DIM), cmap),
                pl.BlockSpec((c, RET_KEY_DIM), cmap),
                pl.BlockSpec((c, RET_VAL_DIM), cmap),
            ],
            out_specs=[
                pl.BlockSpec((c, RET_VAL_DIM), cmap),
                pl.BlockSpec((1, 1, RET_KEY_DIM, RET_VAL_DIM), lambda h, step, lam_ref: (h, step, 0, 0)),
            ],
            scratch_shapes=[pltpu.VMEM((RET_KEY_DIM, RET_VAL_DIM), F32)],
        ),
        out_shape=[
            jax.ShapeDtypeStruct((t, nh * RET_VAL_DIM), F32),
            jax.ShapeDtypeStruct((nh, nt, RET_KEY_DIM, RET_VAL_DIM), F32),
        ],
        compiler_params=_params(("parallel", "arbitrary")),
    )(lam, qr, kr, v)


def _ret_bwd_call(qr, kr, v, lam, states, do, n_lat, reverse):
    t = qr.shape[0]
    nh = qr.shape[1] // RET_KEY_DIM
    c = RET_CHUNK
    nt, nl = t // c, n_lat // c

    def body(lam_ref, q_ref, k_ref, v_ref, s_ref, do_ref, dq_ref, dk_ref, dv_ref, dl_ref, dstate):
        h, rstep = pl.program_id(0), pl.program_id(1)

        @pl.when(rstep == 0)
        def _():
            dstate[...] = jnp.zeros_like(dstate)
            dl_ref[...] = jnp.zeros_like(dl_ref)

        mask, dpos, qd, kd, qpos, kpos, g = _ret_decays(lam_ref[h], reverse)
        q, k, vv = q_ref[...], k_ref[...], v_ref[...]
        qf, kf = q.astype(F32), k.astype(F32)
        s = s_ref[0, 0]
        ds = dstate[...]
        dob = do_ref[...].astype(BF16)
        sb, dsb = s.astype(BF16), ds.astype(BF16)
        a = _dot_nt(q, k)
        p = a * mask
        dp = _dot_nt(dob, vv)
        da = dp * mask
        dab = da.astype(BF16)
        dqc = _dot_nt(dob, sb)
        dkc = _dot_nt(vv, dsb)
        qs = (qf * qd).astype(BF16)
        ks = (kf * kd).astype(BF16)
        dq_ref[...] = (_dot(dab, k) + dqc * qd).astype(dq_ref.dtype)
        dk_ref[...] = (_dot_tn(dab, q) + dkc * kd).astype(dk_ref.dtype)
        dv_ref[...] = (_dot_tn(p.astype(BF16), dob) + _dot(ks, dsb)).astype(dv_ref.dtype)
        terms = (
            jnp.sum(jnp.sum(da * a * dpos, axis=1, keepdims=True), axis=0, keepdims=True)
            + jnp.sum(jnp.sum(dqc * qf * (qd * qpos), axis=1, keepdims=True), axis=0, keepdims=True)
            + jnp.sum(jnp.sum(dkc * kf * (kd * kpos), axis=1, keepdims=True), axis=0, keepdims=True)
            + jnp.sum(jnp.sum(ds * s * (g * c), axis=1, keepdims=True), axis=0, keepdims=True)
        )
        dl_ref[0] += jnp.broadcast_to(terms, (8, 128))
        dstate[...] = ds * g + _dot_tn(qs, dob)

    def cmap(h, rstep, lam_ref):
        return (_ret_chunk_of(nt - 1 - rstep, nt, nl, reverse), h)

    return pl.pallas_call(
        body,
        name="retention_rev_bwd" if reverse else "retention_bwd",
        grid_spec=pltpu.PrefetchScalarGridSpec(
            num_scalar_prefetch=1,
            grid=(nh, nt),
            in_specs=[
                pl.BlockSpec((c, RET_KEY_DIM), cmap),
                pl.BlockSpec((c, RET_KEY_DIM), cmap),
                pl.BlockSpec((c, RET_VAL_DIM), cmap),
                pl.BlockSpec((1, 1, RET_KEY_DIM, RET_VAL_DIM), lambda h, rstep, lam_ref: (h, nt - 1 - rstep, 0, 0)),
                pl.BlockSpec((c, RET_VAL_DIM), cmap),
            ],
            out_specs=[
                pl.BlockSpec((c, RET_KEY_DIM), cmap),
                pl.BlockSpec((c, RET_KEY_DIM), cmap),
                pl.BlockSpec((c, RET_VAL_DIM), cmap),
                pl.BlockSpec((1, 8, 128), lambda h, rstep, lam_ref: (h, 0, 0)),
            ],
            scratch_shapes=[pltpu.VMEM((RET_KEY_DIM, RET_VAL_DIM), F32)],
        ),
        out_shape=[
            jax.ShapeDtypeStruct(qr.shape, qr.dtype),
            jax.ShapeDtypeStruct(kr.shape, kr.dtype),
            jax.ShapeDtypeStruct(v.shape, v.dtype),
            jax.ShapeDtypeStruct((nh, 8, 128), F32),
        ],
        compiler_params=_params(("parallel", "arbitrary")),
    )(lam, qr, kr, v, states, do)


def _rope_tables(t, n_lat):
    nf = RET_KEY_DIM // 4
    tok = np.arange(n_lat)
    inv_freq = (ROPE_BASE ** (-np.arange(nf, dtype=np.float32) / nf)).astype(np.float32)
    row = (tok // GRID_W).astype(np.float32)
    col = (tok % GRID_W).astype(np.float32)
    ang = np.concatenate([row[:, None] * inv_freq, col[:, None] * inv_freq], axis=-1).astype(np.float32)
    cos = np.ones((t, 2 * nf), np.float32)
    sin = np.zeros((t, 2 * nf), np.float32)
    cos[:n_lat] = np.cos(ang)
    sin[:n_lat] = np.sin(ang)
    return jnp.asarray(cos), jnp.asarray(sin)


def _rope(xb, cos, sin, mult):
    t, w = xb.shape
    nh = w // RET_KEY_DIM
    half = RET_KEY_DIM // 2
    x = xb.astype(F32).reshape(t, nh, 2, half)
    x1, x2 = x[:, :, 0], x[:, :, 1]
    c, s = cos[:, None, :], sin[:, None, :]
    out = jnp.stack([x1 * c - x2 * s, x2 * c + x1 * s], axis=2) * mult
    return out.reshape(t, w).astype(BF16)


def _my_position():
    return lax.axis_index("x"), lax.axis_index("y"), lax.axis_index("c")


def _flip(pos, k):
    x, y, c = pos
    return (1 - x if k & 4 else x, 1 - y if k & 2 else y, 1 - c if k & 1 else c)


def _linear(pos):
    return 4 * pos[0] + 2 * pos[1] + pos[2]


def _slab(ref, axis, idx, size):
    start = pl.multiple_of(idx * size, size)
    return ref.at[pl.ds(start, size), :] if axis == 0 else ref.at[:, pl.ds(start, size)]


HBM_SPEC = pl.BlockSpec(memory_space=pltpu.HBM)
SEM_SPEC = pl.BlockSpec(memory_space=pltpu.SEMAPHORE)
DATAFLOW = pltpu.SideEffectType.DATAFLOW_SIDE_EFFECTING
PEER_BITS = (1, 2, 4, 6, 3, 5, 7)
GATHER_BITS = (1, 2, 4, 6)


def _in_hbm(a):
    return pltpu.with_memory_space_constraint(a, pltpu.HBM)


def _gather_views(me, k, a, src_refs, land_refs, axes):
    size = src_refs[a].shape[axes[a]]
    peer = _flip(me, k)
    return src_refs[a], _slab(land_refs[a], axes[a], _linear(me), size), _slab(land_refs[a], axes[a], _linear(peer), size)


def _scatter_views(me, k, a, src_refs, land_refs, axes):
    size = land_refs[a].shape[1 + axes[a]]
    peer = _flip(me, k)
    return _slab(src_refs[a], axes[a], _linear(peer), size), land_refs[a].at[k - 1], land_refs[a].at[k - 1]


def _slab_block(rows, cols, tm, axis):
    if axis == 0:
        return pl.BlockSpec((tm, cols), lambda i, idx: (idx[0] * (rows // tm) + i, 0))
    return pl.BlockSpec((tm, cols), lambda i, idx: (i, idx[0]))


def _place_shard(shard, land, axis, my_idx):
    r, c = shard.shape
    tm = _pick(r, (512, 256, 128, 64, 32, 16))

    def body(idx_ref, s_ref, land_ref, o_ref):
        o_ref[...] = s_ref[...]

    return pl.pallas_call(
        body, name="gather_place",
        grid_spec=pltpu.PrefetchScalarGridSpec(
            num_scalar_prefetch=1, grid=(r // tm,),
            in_specs=[pl.BlockSpec((tm, c), lambda i, idx: (i, 0)), ANY],
            out_specs=_slab_block(r, c, tm, axis)),
        out_shape=jax.ShapeDtypeStruct(land.shape, land.dtype),
        input_output_aliases={2: 0},
        compiler_params=_params(("parallel",)),
    )(my_idx, shard, land)


def _push_start(name, srcs, lands, axes, views, bits, deps):
    ns = len(srcs)

    def body(*refs):
        src_refs, land_refs = refs[:ns], refs[ns:2 * ns]
        send_sems, recv_sems = refs[2 * ns + len(deps):2 * ns + len(deps) + 2]
        token = refs[-1]
        me = _my_position()
        for k in bits:
            for a in range(ns):
                s, d, _ = views(me, k, a, src_refs, land_refs, axes)
                pltpu.make_async_remote_copy(
                    src_ref=s, dst_ref=d, send_sem=send_sems.at[7 * a + k - 1], recv_sem=recv_sems.at[7 * a + k - 1],
                    device_id=_flip(me, k), device_id_type=MESH).start()
        token[...] = jnp.zeros_like(token)

    thru = [pltpu.HBM(a.shape, a.dtype) for a in list(srcs) + list(lands)]
    outs = pl.pallas_call(
        body, name=name,
        in_specs=[HBM_SPEC] * (2 * ns) + [ANY] * len(deps),
        out_specs=[SEM_SPEC, SEM_SPEC] + [HBM_SPEC] * (2 * ns) + [VMEM_SPEC],
        out_shape=[pltpu.SemaphoreType.DMA((7 * ns,)), pltpu.SemaphoreType.DMA((7 * ns,))] + thru
        + [jax.ShapeDtypeStruct((8, 128), F32)],
        input_output_aliases={i: 2 + i for i in range(2 * ns)},
        compiler_params=pltpu.CompilerParams(has_side_effects=DATAFLOW),
    )(*[_in_hbm(a) for a in srcs], *[_in_hbm(a) for a in lands], *deps)
    return (outs[0], outs[1]), outs[2:2 + ns], outs[2 + ns:2 + 2 * ns], outs[-1]


def _gather_finish(lands, axes, sizes):
    ns = len(lands)
    chips = (2, 4, 6)

    def body(*refs):
        land_refs = refs[ns:2 * ns]
        send_sems, recv_sems = refs[2 * ns:]
        me = _my_position()
        sibling = _flip(me, 1)
        copies = []
        for j, kc in enumerate(chips):
            for a in range(ns):
                def slab_of(pos):
                    return _slab(land_refs[a], axes[a], _linear(pos), sizes[a])
                send = pltpu.make_async_remote_copy(
                    src_ref=slab_of(_flip(me, kc)), dst_ref=slab_of(_flip(me, kc)), send_sem=send_sems.at[3 * a + j],
                    recv_sem=recv_sems.at[3 * a + j], device_id=sibling, device_id_type=MESH)
                recv = pltpu.make_async_remote_copy(
                    src_ref=slab_of(_flip(me, kc)), dst_ref=slab_of(_flip(sibling, kc)), send_sem=send_sems.at[3 * a + j],
                    recv_sem=recv_sems.at[3 * a + j], device_id=sibling, device_id_type=MESH)
                send.start()
                copies.append((send, recv))
        for send, recv in copies:
            recv.wait_recv()
        for send, recv in copies:
            send.wait_send()

    return pl.pallas_call(
        body, name="gather_finish", in_specs=[ANY] * ns, out_specs=[ANY] * ns,
        out_shape=[jax.ShapeDtypeStruct(l.shape, l.dtype) for l in lands],
        input_output_aliases={a: a for a in range(ns)},
        scratch_shapes=[pltpu.SemaphoreType.DMA((3 * ns,)), pltpu.SemaphoreType.DMA((3 * ns,))],
        compiler_params=pltpu.CompilerParams(has_side_effects=True),
    )(*lands)


def _push_wait(name, sems, srcs, lands, axes, views, bits, after):
    ns = len(srcs)

    def body(*refs):
        src_refs, land_refs = refs[:ns], refs[ns:2 * ns]
        send_sems, recv_sems = refs[2 * ns:2 * ns + 2]
        me = _my_position()
        for k in bits:
            for a in range(ns):
                s, d, got = views(me, k, a, src_refs, land_refs, axes)
                cp = pltpu.make_async_remote_copy(
                    src_ref=s, dst_ref=got, send_sem=send_sems.at[7 * a + k - 1], recv_sem=recv_sems.at[7 * a + k - 1],
                    device_id=_flip(me, k), device_id_type=MESH)
                cp.wait_send()
                cp.wait_recv()

    thru = [pltpu.HBM(a.shape, a.dtype) for a in list(srcs) + list(lands)]
    outs = pl.pallas_call(
        body, name=name,
        in_specs=[HBM_SPEC] * (2 * ns) + [SEM_SPEC, SEM_SPEC, ANY],
        out_specs=[HBM_SPEC] * (2 * ns),
        out_shape=thru,
        input_output_aliases={i: i for i in range(2 * ns)},
        compiler_params=pltpu.CompilerParams(has_side_effects=DATAFLOW),
    )(*srcs, *lands, sems[0], sems[1], after)
    return outs[:ns], outs[ns:]


def _small_allgather(v, name):
    r, c = v.shape

    def body(v_ref, all_ref, sum_ref, send_sems, recv_sems):
        me = _my_position()
        all_ref[_linear(me)] = v_ref[...]
        copies = []
        for k in range(1, N_DEV):
            peer = _flip(me, k)
            copies.append(pltpu.make_async_remote_copy(
                src_ref=v_ref, dst_ref=all_ref.at[_linear(me)], send_sem=send_sems.at[k - 1], recv_sem=recv_sems.at[k - 1],
                device_id=peer, device_id_type=MESH))
        for cp in copies:
            cp.start()
        for k in range(1, N_DEV):
            peer = _flip(me, k)
            pltpu.make_async_remote_copy(
                src_ref=v_ref, dst_ref=all_ref.at[_linear(peer)], send_sem=send_sems.at[k - 1], recv_sem=recv_sems.at[k - 1],
                device_id=peer, device_id_type=MESH).wait_recv()
        for cp in copies:
            cp.wait_send()
        acc = all_ref[0]
        for d in range(1, N_DEV):
            acc = acc + all_ref[d]
        sum_ref[...] = acc

    return pl.pallas_call(
        body,
        name=name,
        in_specs=[VMEM_SPEC],
        out_specs=[VMEM_SPEC, VMEM_SPEC],
        out_shape=[jax.ShapeDtypeStruct((N_DEV, r, c), F32), jax.ShapeDtypeStruct((r, c), F32)],
        scratch_shapes=[pltpu.SemaphoreType.DMA((N_DEV - 1,)), pltpu.SemaphoreType.DMA((N_DEV - 1,))],
        compiler_params=pltpu.CompilerParams(has_side_effects=True, vmem_limit_bytes=VMEM_LIMIT),
    )(v)


def _ada_fwd_call(cin, ada_w, ada_b_cols):
    nl, d, ncol = ada_w.shape
    nrow = cin.shape[0]

    def body(c_ref, w_ref, b_ref, o_ref):
        cs = _silu(c_ref[...]).astype(BF16)
        for l in range(nl):
            o_ref[l] = _dot(cs, w_ref[l].astype(BF16)) + b_ref[l]

    return pl.pallas_call(
        body, name="ada_fwd", in_specs=[VMEM_SPEC] * 3, out_specs=VMEM_SPEC,
        out_shape=jax.ShapeDtypeStruct((nl, nrow, ncol), F32), compiler_params=_params(),
    )(cin, ada_w, ada_b_cols)


def _ada_bwd_call(cin, ada_w, dmod):
    nl, d, ncol = ada_w.shape
    nrow = cin.shape[0]

    def body(c_ref, w_ref, dm_ref, gw_ref, dcs_ref):
        cs = _silu(c_ref[...]).astype(BF16)
        acc = jnp.zeros((nrow, d), F32)
        for l in range(nl):
            dm = dm_ref[l].astype(BF16)
            gw_ref[l] = _dot_tn(cs, dm)
            acc = acc + _dot_nt(dm, w_ref[l].astype(BF16))
        dcs_ref[...] = acc

    return pl.pallas_call(
        body, name="ada_bwd", in_specs=[VMEM_SPEC] * 3, out_specs=[VMEM_SPEC, VMEM_SPEC],
        out_shape=[jax.ShapeDtypeStruct((nl, d, ncol), F32), jax.ShapeDtypeStruct((nrow, d), F32)],
        compiler_params=_params(),
    )(cin, ada_w, dmod)


def _adamw_math(w, g, m, v):
    m = ADAM_B1 * m + (1.0 - ADAM_B1) * g
    v = ADAM_B2 * v + (1.0 - ADAM_B2) * jnp.square(g)
    m_hat = m / (1.0 - ADAM_B1 ** ADAM_STEP)
    v_hat = v / (1.0 - ADAM_B2 ** ADAM_STEP)
    delta = -ADAM_LR * (m_hat / (jnp.sqrt(v_hat) + ADAM_EPS) + ADAM_WD * w)
    return delta, m, v


def _adamw_sharded(w, m, v, mine, slabs, axis, my_idx, layer, prev, name):
    nl, r, c = w.shape
    tm = _pick(r, (128, 64, 32, 16))
    nprev = 0 if prev is None else len(prev)

    def body(idx_ref, w_ref, m_ref, v_ref, mine_ref, s_ref, *rest):
        g_ref, d_ref, nm_ref, nv_ref = rest[nprev:]
        g = mine_ref[...].astype(F32)
        for k in range(N_DEV - 1):
            g = g + s_ref[k].astype(F32)
        delta, nm, nv = _adamw_math(w_ref[0], g, m_ref[0], v_ref[0])
        g_ref[0], d_ref[0], nm_ref[0], nv_ref[0] = g, delta, nm, nv

    spec = pl.BlockSpec((1, tm, c), lambda i, idx: (layer, i, 0))
    out = jax.ShapeDtypeStruct(w.shape, F32)
    return pl.pallas_call(
        body, name=name,
        grid_spec=pltpu.PrefetchScalarGridSpec(
            num_scalar_prefetch=1, grid=(r // tm,),
            in_specs=[spec, spec, spec, _slab_block(r, c, tm, axis),
                      pl.BlockSpec((N_DEV - 1, tm, c), lambda i, idx: (0, i, 0))] + [ANY] * nprev,
            out_specs=[spec] * 4),
        out_shape=[out] * 4,
        input_output_aliases={6 + j: j for j in range(nprev)},
        compiler_params=_params(("parallel",)),
    )(my_idx, w, m, v, mine, slabs, *(() if prev is None else prev))


def _adamw_dense(w, g, m, v, name):
    r, c = w.shape
    tm = _pick(r, (256, 128, 64, 32, 16, 8))

    def body(w_ref, g_ref, m_ref, v_ref, d_ref, nm_ref, nv_ref):
        d_ref[...], nm_ref[...], nv_ref[...] = _adamw_math(w_ref[...], g_ref[...], m_ref[...], v_ref[...])

    spec = pl.BlockSpec((tm, c), lambda i: (i, 0))
    out = jax.ShapeDtypeStruct(w.shape, F32)
    return pl.pallas_call(
        body, name=name, grid=(r // tm,), in_specs=[spec] * 4, out_specs=[spec] * 3, out_shape=[out] * 3,
        compiler_params=_params(("parallel",)),
    )(w, g, m, v)


def _pack(parts, width=128):
    flat = jnp.concatenate([p.reshape(-1).astype(F32) for p in parts])
    n = flat.shape[0]
    total = -(-n // (8 * width)) * (8 * width)
    return jnp.pad(flat, (0, total - n)).reshape(total // width, width)


def _unpack(buf, shapes):
    flat = buf.reshape(-1)
    out, off = [], 0
    for s in shapes:
        n = int(np.prod(s))
        out.append(flat[off:off + n].reshape(s))
        off += n
    return out


def kernel(x, c, ctx, c_ctx, ada_w, ada_b, norm_g, w_in, na_rpb, ret_decay_logit, w_proj_na, w_proj_ret, w_out, final_g, loss_target, m_c_ctx, m_ada_w, m_ada_b, m_norm_g, m_w_in, m_na_rpb, m_ret_decay_logit, m_w_proj_na, m_w_proj_ret, m_w_out, m_final_g, v_c_ctx, v_ada_w, v_ada_b, v_norm_g, v_w_in, v_na_rpb, v_ret_decay_logit, v_w_proj_na, v_w_proj_ret, v_w_out, v_final_g):
    depth = w_in.shape[0]
    n_lat, d = x.shape[1], x.shape[2]
    n_ctx = ctx.shape[1]
    t = n_lat + n_ctx
    w_na = w_proj_na.shape[1]
    w_retv = w_proj_ret.shape[1] * N_DEV
    in_cols = w_in.shape[2] * N_DEV
    w_qk = (in_cols - 4 * w_na - 2 * w_retv - 2 * d) // 2
    sizes = (w_na, w_na, w_na, w_na, w_qk, w_qk, w_retv, w_retv, d, d)
    off = tuple(int(o) for o in np.cumsum((0,) + sizes))
    NA_Q, NA_K, NA_V, NA_Z, RET_Q, RET_K, RET_V, RET_Z, G_NA, G_RET = range(10)
    rows = n_lat // GRID_W
    me = _my_position()
    my_idx = _linear(me)
    tm_row = _pick(n_ctx, (256, 128))

    idx_arr = jnp.reshape(my_idx, (1,)).astype(jnp.int32)

    ncol = ada_w.shape[2]
    c_all, _ = _small_allgather(jnp.pad(c, ((0, 7), (0, 0))), "allgather_c")
    cin = jnp.concatenate([c_all[:, 0, :], c_ctx[None, :], jnp.zeros((7, d), F32)], axis=0)
    ada_b_cols = lax.dynamic_slice_in_dim(ada_b, my_idx * ncol, ncol, axis=1)[:, None, :]
    mod_cols = _ada_fwd_call(cin, ada_w, ada_b_cols)
    mod_gathered, _ = _small_allgather(mod_cols.reshape(depth * 16, ncol), "allgather_mod")
    mod_all = mod_gathered.reshape(N_DEV, depth, 16, ncol).transpose(1, 2, 0, 3).reshape(depth, 16, N_DEV * ncol)
    mod_lat = lax.dynamic_index_in_dim(mod_all, my_idx, axis=1, keepdims=False)
    mod_ctx = mod_all[:, 8, :]

    w_axes = (1, 1, 0, 0)
    w_names = ("w_in", "w_proj_na", "w_proj_ret", "w_out")
    shard = [[w[l].astype(BF16) for w in (w_in, w_proj_na, w_proj_ret, w_out)] for l in range(depth)]
    groups = [[(0, 0)], [(0, 1), (0, 2), (0, 3)]] + [[(l, a) for a in range(4)] for l in range(1, depth)]
    gathers, token = {}, mod_gathered
    for gi, keys in enumerate(groups):
        srcs = [shard[l][a] for l, a in keys]
        axes = tuple(w_axes[a] for _, a in keys)
        lands = [_place_shard(s, lax.empty(tuple(n * (N_DEV if i == ax else 1) for i, n in enumerate(s.shape)), BF16),
                              ax, idx_arr) for s, ax in zip(srcs, axes)]
        sizes = tuple(s.shape[ax] for s, ax in zip(srcs, axes))
        sems, srcs, lands, token = _push_start(
            f"gather_start_{gi}", srcs, lands, axes, _gather_views, GATHER_BITS, (token,))
        flight = dict(name=f"gather_wait_{gi}", sems=sems, srcs=srcs, lands=lands, axes=axes, sizes=sizes, ready=None)
        for pos, key in enumerate(keys):
            gathers[key] = (flight, pos)

    def landed(l, a, act):
        flight, pos = gathers[(l, a)]
        if flight["ready"] is None:
            arrived = _push_wait(flight["name"], flight["sems"], flight["srcs"], flight["lands"],
                                 flight["axes"], _gather_views, GATHER_BITS, act)[1]
            flight["ready"] = _gather_finish(arrived, flight["axes"], flight["sizes"])
        return flight["ready"][pos]

    pending, scatters = {}, []

    def send_dw(l, a, dw):
        pending[(l, a)] = dw
        if a == 0:
            keys = [(0, 0)] if l == 0 else [(l, b) for b in range(4)]
        elif l == 0 and a == 1:
            keys = [(0, 1), (0, 2), (0, 3)]
        else:
            return None
        srcs = [pending[k] for k in keys]
        axes = tuple(w_axes[b] for _, b in keys)
        lands = [lax.empty((N_DEV - 1,) + tuple(n // (N_DEV if i == ax else 1) for i, n in enumerate(s.shape)), BF16)
                 for s, ax in zip(srcs, axes)]
        sems, srcs, lands, tok = _push_start(
            f"scatter_start_{len(scatters)}", srcs, lands, axes, _scatter_views, PEER_BITS, ())
        scatters.append(dict(name=f"scatter_wait_{len(scatters)}", sems=sems, srcs=srcs, lands=lands, axes=axes, keys=keys))
        return tok

    cos, sin = _rope_tables(t, n_lat)
    k_scale = RET_KEY_DIM ** -0.5
    norm_mod_fwd, norm_mod_bwd = _make_rowwise(_f_norm_mod, "norm_mod", (BF16,), (d,), n_lat, tm_row, (0,))
    gate_na_fwd, gate_na_bwd = _make_rowwise(_f_gate_na, "gate_na", (BF16,), (w_na,), n_lat, tm_row, (0, 1))
    merge_fwd, merge_bwd = _make_rowwise(_f_merge, "merge", (BF16,), (d,), n_lat, tm_row, (0, 1, 2, 3))
    residual_fwd, residual_bwd = _make_rowwise(_f_residual, "residual", (F32,), (d,), n_lat, tm_row, (0, 1))
    loss_fwd, loss_bwd = _make_rowwise(_f_loss, "loss_head", (F32,), (128,), n_lat, tm_row, (0,))

    def pair(a, b):
        return jnp.stack([a, b])[:, None, :]

    def mod_vectors(mod_lat_l, mod_ctx_l, norm_g_l):
        shift, scale, gate = jnp.split(mod_lat_l, 3)
        c_shift, c_scale, c_gate = jnp.split(mod_ctx_l, 3)
        return pair(norm_g_l, norm_g_l), pair(scale, c_scale), pair(shift, c_shift), pair(gate, c_gate)

    def split_u(u):
        blk = [u[:, off[i]:off[i + 1]] for i in range(10)]
        blk[RET_Q] = _rope(blk[RET_Q], cos, sin, 1.0)
        blk[RET_K] = _rope(blk[RET_K], cos, sin, k_scale)
        return tuple(blk)

    def log_decay(logit):
        return jax.nn.log_sigmoid(logit.astype(F32))

    xa = jnp.concatenate([x[0], ctx[0]], axis=0)
    saved = []
    for l in range(depth):
        vecs, vecs_vjp = jax.vjp(mod_vectors, mod_lat[l], mod_ctx[l], norm_g[l])
        (h,) = norm_mod_fwd((xa,), vecs[:3])
        wl_in = landed(l, 0, h)
        u = _matmul(h, wl_in, out_dtype=BF16, name="in_proj_fwd")
        blk, split_vjp = jax.vjp(split_u, u)
        bt, bt_vjp = jax.vjp(lambda r: _na_bias_table(r, rows), na_rpb[l])
        lam, lam_vjp = jax.vjp(log_decay, ret_decay_logit[l])
        o_na = _na_fwd_call(blk[NA_Q], blk[NA_K], blk[NA_V], bt, n_lat)
        o_f, st_f = _ret_fwd_call(blk[RET_Q], blk[RET_K], blk[RET_V], lam[0], n_lat, False)
        o_b, st_b = _ret_fwd_call(blk[RET_Q], blk[RET_K], blk[RET_V], lam[1], n_lat, True)
        (a_na,) = gate_na_fwd((o_na, blk[NA_Z]), ())
        a_ret = _gate_ret_fwd_call(o_f, o_b, blk[RET_Z], tm_row)
        wl_pna, wl_pret, wl_out = landed(l, 1, a_na), landed(l, 2, a_na), landed(l, 3, a_na)
        y_na = _matmul(a_na, wl_pna, out_dtype=F32, name="proj_na_fwd")
        y_ret = _matmul(a_ret, wl_pret, out_dtype=F32, name="proj_ret_fwd")
        (merged,) = merge_fwd((blk[G_NA], blk[G_RET], y_na, y_ret), ())
        out = _matmul(merged, wl_out, out_dtype=F32, name="out_proj_fwd")
        (xa_next,) = residual_fwd((xa, out), vecs[3:])
        saved.append(dict(xa=xa, vecs=vecs, vecs_vjp=vecs_vjp, h=h, w=(wl_in, wl_pna, wl_pret, wl_out), blk=blk,
                          split_vjp=split_vjp, bt=bt, bt_vjp=bt_vjp, lam=lam, lam_vjp=lam_vjp, o_na=o_na, o_f=o_f,
                          o_b=o_b, st_f=st_f, st_b=st_b, a_na=a_na, a_ret=a_ret, y_na=y_na, y_ret=y_ret,
                          merged=merged, out=out))
        xa = xa_next

    fg_pair, fg_vjp = jax.vjp(lambda g: pair(g, g), final_g)
    x_last = xa[:n_lat]
    (loss_rows,) = loss_fwd((x_last, loss_target[0]), (fg_pair,))
    loss = lax.psum(jnp.sum(loss_rows), ("x", "y", "c"))
    (dx_last,), (d_fg_pair,) = loss_bwd((x_last, loss_target[0]), (fg_pair,), (jnp.ones_like(loss_rows),))
    (d_final_g,) = fg_vjp(d_fg_pair)
    dxa = jnp.pad(dx_last, ((0, n_ctx), (0, 0)))

    d_mod_lat, d_mod_ctx, d_norm_g, d_rpb, d_decay = ([None] * depth for _ in range(5))
    for l in reversed(range(depth)):
        s = saved[l]
        blk = s["blk"]
        wl_in, wl_pna, wl_pret, wl_out = s["w"]
        (dxa_res, d_out), (d_gate,) = residual_bwd((s["xa"], s["out"]), s["vecs"][3:], (dxa,))
        d_out = d_out.astype(BF16)
        send_dw(l, 3, _matmul(s["merged"], d_out, trans_a=True, out_dtype=BF16, name="out_proj_dw"))
        d_merged = _matmul(d_out, wl_out, trans_b=True, out_dtype=BF16, name="out_proj_da")
        (dg_na, dg_ret, dy_na, dy_ret), _ = merge_bwd((blk[G_NA], blk[G_RET], s["y_na"], s["y_ret"]), (), (d_merged,))
        dy_na, dy_ret = dy_na.astype(BF16), dy_ret.astype(BF16)
        send_dw(l, 2, _matmul(s["a_ret"], dy_ret, trans_a=True, out_dtype=BF16, name="proj_ret_dw"))
        da_ret = _matmul(dy_ret, wl_pret, trans_b=True, out_dtype=BF16, name="proj_ret_da")
        tok = send_dw(l, 1, _matmul(s["a_na"], dy_na, trans_a=True, out_dtype=BF16, name="proj_na_dw"))
        da_na = _matmul(dy_na, wl_pna, trans_b=True, out_dtype=BF16, name="proj_na_da", after=tok)
        do_ret, dz_ret = _gate_ret_bwd_call(s["o_f"], s["o_b"], blk[RET_Z], da_ret, tm_row)
        (do_na, dz_na), _ = gate_na_bwd((s["o_na"], blk[NA_Z]), (), (da_na,))
        dq_f, dk_f, dv_f, dl_f = _ret_bwd_call(blk[RET_Q], blk[RET_K], blk[RET_V], s["lam"][0], s["st_f"], do_ret, n_lat, False)
        dq_b, dk_b, dv_b, dl_b = _ret_bwd_call(blk[RET_Q], blk[RET_K], blk[RET_V], s["lam"][1], s["st_b"], do_ret, n_lat, True)
        dq, dk, dv, dbt = _na_bwd_call(blk[NA_Q], blk[NA_K], blk[NA_V], s["bt"], do_na, n_lat)
        d_blk = [None] * 10
        d_blk[NA_Q], d_blk[NA_K], d_blk[NA_V], d_blk[NA_Z] = dq, dk.astype(BF16), dv.astype(BF16), dz_na
        d_blk[RET_Q], d_blk[RET_K], d_blk[RET_V], d_blk[RET_Z] = dq_f + dq_b, dk_f + dk_b, dv_f + dv_b, dz_ret
        d_blk[G_NA], d_blk[G_RET] = dg_na, dg_ret
        (du,) = s["split_vjp"](tuple(d_blk))
        (d_rpb[l],) = s["bt_vjp"](dbt)
        (d_decay[l],) = s["lam_vjp"](jnp.stack([dl_f[:, 0, 0], dl_b[:, 0, 0]]))
        tok = send_dw(l, 0, _matmul(s["h"], du, trans_a=True, out_dtype=BF16, name="in_proj_dw"))
        dh = _matmul(du, wl_in, trans_b=True, out_dtype=BF16, name="in_proj_da", after=tok)
        (dxa_norm,), d_vecs = norm_mod_bwd((s["xa"],), s["vecs"][:3], (dh,))
        dxa = dxa_res + dxa_norm
        d_mod_lat[l], d_mod_ctx[l], d_norm_g[l] = s["vecs_vjp"](tuple(d_vecs) + (d_gate,))
    gx = dxa[:n_lat]
    d_mod_lat, d_mod_ctx, d_norm_g, d_rpb, d_decay = (jnp.stack(a) for a in (d_mod_lat, d_mod_ctx, d_norm_g, d_rpb, d_decay))

    small_shapes = [d_mod_lat.shape, d_mod_ctx.shape, d_norm_g.shape, d_final_g.shape, d_rpb.shape, d_decay.shape]
    packed = _pack([d_mod_lat, d_mod_ctx, d_norm_g, d_final_g, d_rpb, d_decay])
    g_all, g_sum = _small_allgather(packed, "allgather_small_grads")
    dml_sum, dmc_sum, grad_norm_g, grad_final_g, grad_na_rpb, grad_decay = _unpack(g_sum, small_shapes)
    grad_ada_b = dml_sum + dmc_sum
    dml_all = g_all.reshape(N_DEV, -1)[:, :depth * 3 * d].reshape(N_DEV, depth, 3 * d)

    def my_cols(a):
        return lax.dynamic_slice_in_dim(a, my_idx * ncol, ncol, axis=a.ndim - 1)

    dmod = jnp.concatenate(
        [my_cols(dml_all).transpose(1, 0, 2), my_cols(dmc_sum)[:, None, :], jnp.zeros((depth, 7, ncol), F32)], axis=1)
    grad_ada_w, dcs_part = _ada_bwd_call(cin, ada_w, dmod)
    _, dcs = _small_allgather(dcs_part, "allgather_dcsilu")
    sg = jax.nn.sigmoid(c_ctx)
    grad_c_ctx = dcs[8] * (sg * (1.0 + c_ctx * (1.0 - sg)))

    def flat2(a):
        return a.reshape(a.shape[0] * a.shape[1], a.shape[2])

    small_w = [c_ctx, ada_b, norm_g, na_rpb, ret_decay_logit, final_g]
    small_g = [grad_c_ctx, grad_ada_b, grad_norm_g, grad_na_rpb, grad_decay, grad_final_g]
    small_m = [m_c_ctx, m_ada_b, m_norm_g, m_na_rpb, m_ret_decay_logit, m_final_g]
    small_v = [v_c_ctx, v_ada_b, v_norm_g, v_na_rpb, v_ret_decay_logit, v_final_g]
    shp = [a.shape for a in small_w]
    ds_, nms_, nvs_ = _adamw_dense(_pack(small_w), _pack(small_g), _pack(small_m), _pack(small_v), "adamw_small")
    ds_, nms_, nvs_ = _unpack(ds_, shp), _unpack(nms_, shp), _unpack(nvs_, shp)

    d_ada, nm_ada, nv_ada = [a.reshape(ada_w.shape) for a in _adamw_dense(
        flat2(ada_w), flat2(grad_ada_w), flat2(m_ada_w), flat2(v_ada_w), "adamw_ada_w")]

    w_all = (w_in, w_proj_na, w_proj_ret, w_out)
    m_all = (m_w_in, m_w_proj_na, m_w_proj_ret, m_w_out)
    v_all = (v_w_in, v_w_proj_na, v_w_proj_ret, v_w_out)
    upd = [None] * 4
    after = d_ada
    for flight in scatters:
        mine, slabs = _push_wait(flight["name"], flight["sems"], flight["srcs"], flight["lands"], flight["axes"],
                                 _scatter_views, PEER_BITS, after)
        for (l, a), own, s in zip(flight["keys"], mine, slabs):
            upd[a] = _adamw_sharded(w_all[a], m_all[a], v_all[a], own, s, w_axes[a], idx_arr, l, upd[a],
                                    "adamw_" + w_names[a])
            after = upd[a][1]
    (g_w_in, d_w_in, nm_w_in, nv_w_in), (g_pna, d_pna, nm_pna, nv_pna) = upd[0], upd[1]
    (g_pret, d_pret, nm_pret, nv_pret), (g_out, d_out, nm_out, nv_out) = upd[2], upd[3]

    def order(cc, aw, ab, ng, wi, rp, dl, pn, pr, wo, fg):
        return [cc, aw, ab, ng, wi, rp, dl, pn, pr, wo, fg]

    grads_out = order(grad_c_ctx, grad_ada_w, grad_ada_b, grad_norm_g, g_w_in, grad_na_rpb, grad_decay, g_pna, g_pret, g_out, grad_final_g)
    delta_out = order(ds_[0], d_ada, ds_[1], ds_[2], d_w_in, ds_[3], ds_[4], d_pna, d_pret, d_out, ds_[5])
    m_out = order(nms_[0], nm_ada, nms_[1], nms_[2], nm_w_in, nms_[3], nms_[4], nm_pna, nm_pret, nm_out, nms_[5])
    v_out = order(nvs_[0], nv_ada, nvs_[1], nvs_[2], nv_w_in, nvs_[3], nvs_[4], nv_pna, nv_pret, nv_out, nvs_[5])
    return (loss, gx[None], *grads_out, *delta_out, *m_out, *v_out)
```

```python
import functools

import numpy as np
import jax
import jax.numpy as jnp
from jax import lax
from jax.experimental import pallas as pl
from jax.experimental.pallas import tpu as pltpu

F32 = jnp.float32
BF16 = jnp.bfloat16

N_DEV = 8
GRID_W = 64
NA_HEAD_DIM = 128
NA_WIN_ROWS = 8
NA_WIN_COLS = 16
RET_KEY_DIM = 128
RET_VAL_DIM = 256
RET_CHUNK = 128
ROPE_BASE = 10000.0
NORM_EPS = 1e-6
MASK_VALUE = -1e30

ADAM_LR = 0.001
ADAM_B1 = 0.9
ADAM_B2 = 0.999
ADAM_EPS = 1e-08
ADAM_WD = 0.01
ADAM_STEP = 10

VMEM_LIMIT = 48 * 1024 * 1024
MESH = pl.DeviceIdType.MESH
ANY = pl.BlockSpec(memory_space=pl.ANY)
VMEM_SPEC = pl.BlockSpec(memory_space=pltpu.VMEM)


def _params(sem=None):
    return pltpu.CompilerParams(dimension_semantics=sem, vmem_limit_bytes=VMEM_LIMIT)


def _pick(n, prefs):
    for p in prefs:
        if n % p == 0:
            return p
    return n


def _dot(a, b):
    return lax.dot_general(a, b, (((1,), (0,)), ((), ())), preferred_element_type=F32)


def _dot_nt(a, b):
    return lax.dot_general(a, b, (((1,), (1,)), ((), ())), preferred_element_type=F32)


def _dot_tn(a, b):
    return lax.dot_general(a, b, (((0,), (0,)), ((), ())), preferred_element_type=F32)


def _silu(x):
    return x * jax.nn.sigmoid(x)


def _matmul(a, b, *, trans_a=False, trans_b=False, out_dtype=F32, name="matmul", after=None):
    if trans_a:
        kdim, m = a.shape
    else:
        m, kdim = a.shape
    if trans_b:
        n, kb = b.shape
    else:
        kb, n = b.shape
    assert kdim == kb, (a.shape, b.shape, trans_a, trans_b)
    tm = _pick(m, (1152, 1024, 768, 512, 256, 128))
    tn = _pick(n, (512, 256, 128))
    tk = _pick(kdim, (2304, 2048, 1024, 512, 256, 128))
    nk = kdim // tk
    dn = (((0 if trans_a else 1,), (1 if trans_b else 0,)), ((), ()))

    def body(a_ref, b_ref, *rest):
        o_ref, acc_ref = rest[-2:]
        part = lax.dot_general(a_ref[...], b_ref[...], dn, preferred_element_type=F32)
        if nk == 1:
            o_ref[...] = part.astype(o_ref.dtype)
        else:
            k = pl.program_id(2)

            @pl.when(k == 0)
            def _():
                acc_ref[...] = part

            @pl.when(k > 0)
            def _():
                acc_ref[...] += part

            @pl.when(k == nk - 1)
            def _():
                o_ref[...] = acc_ref[...].astype(o_ref.dtype)

    a_spec = pl.BlockSpec((tk, tm), lambda i, j, k: (k, i)) if trans_a else pl.BlockSpec((tm, tk), lambda i, j, k: (i, k))
    b_spec = pl.BlockSpec((tn, tk), lambda i, j, k: (j, k)) if trans_b else pl.BlockSpec((tk, tn), lambda i, j, k: (k, j))
    return pl.pallas_call(
        body,
        name=name,
        grid=(m // tm, n // tn, nk),
        in_specs=[a_spec, b_spec] + ([] if after is None else [ANY]),
        out_specs=pl.BlockSpec((tm, tn), lambda i, j, k: (i, j)),
        out_shape=jax.ShapeDtypeStruct((m, n), out_dtype),
        scratch_shapes=[pltpu.VMEM((tm, tn) if nk > 1 else (8, 128), F32)],
        compiler_params=_params(("parallel", "parallel", "arbitrary")),
    )(*((a, b) if after is None else (a, b, after)))


def _make_rowwise(f, name, out_dtypes, out_cols, n_lat, tm, diff_rows):
    def tile_fn(*args):
        return tuple(o.astype(dt) for o, dt in zip(f(*args), out_dtypes))

    def fwd_call(rows, vecs):
        t = rows[0].shape[0]
        nr, nv = len(rows), len(vecs)
        nl = n_lat // tm

        def body(*refs):
            grp = (pl.program_id(0) >= nl).astype(jnp.int32)
            args = [r[...] for r in refs[:nr]] + [v[grp] for v in refs[nr:nr + nv]]
            for o_ref, o in zip(refs[nr + nv:], tile_fn(*args)):
                o_ref[...] = o

        return pl.pallas_call(
            body,
            name=name + "_fwd",
            grid=(t // tm,),
            in_specs=[pl.BlockSpec((tm, r.shape[1]), lambda i: (i, 0)) for r in rows]
            + [pl.BlockSpec(v.shape, lambda i: (0, 0, 0)) for v in vecs],
            out_specs=[pl.BlockSpec((tm, c), lambda i: (i, 0)) for c in out_cols],
            out_shape=[jax.ShapeDtypeStruct((t, c), dt) for c, dt in zip(out_cols, out_dtypes)],
            compiler_params=_params(("parallel",)),
        )(*rows, *vecs)

    def bwd_call(rows, vecs, gs):
        t = rows[0].shape[0]
        nr, nv, ng = len(rows), len(vecs), len(gs)
        nl = n_lat // tm
        nd = len(diff_rows)

        def body(*refs):
            i = pl.program_id(0)
            grp = (i >= nl).astype(jnp.int32)
            args = [r[...] for r in refs[:nr]] + [v[grp] for v in refs[nr:nr + nv]]
            g_refs = refs[nr + nv:nr + nv + ng]
            drow_refs = refs[nr + nv + ng:nr + nv + ng + nd]
            dvec_refs = refs[nr + nv + ng + nd:]
            _, vjp = jax.vjp(tile_fn, *args)
            grads = vjp(tuple(g[...] for g in g_refs))
            for d_ref, k in zip(drow_refs, diff_rows):
                d_ref[...] = grads[k].astype(d_ref.dtype)

            @pl.when(i == 0)
            def _():
                for d_ref in dvec_refs:
                    d_ref[...] = jnp.zeros_like(d_ref)

            for j, d_ref in enumerate(dvec_refs):
                d_ref[grp] += grads[nr + j]

        outs = pl.pallas_call(
            body,
            name=name + "_bwd",
            grid=(t // tm,),
            in_specs=[pl.BlockSpec((tm, r.shape[1]), lambda i: (i, 0)) for r in rows]
            + [pl.BlockSpec(v.shape, lambda i: (0, 0, 0)) for v in vecs]
            + [pl.BlockSpec((tm, g.shape[1]), lambda i: (i, 0)) for g in gs],
            out_specs=[pl.BlockSpec((tm, rows[k].shape[1]), lambda i: (i, 0)) for k in diff_rows]
            + [pl.BlockSpec(v.shape, lambda i: (0, 0, 0)) for v in vecs],
            out_shape=[jax.ShapeDtypeStruct(rows[k].shape, rows[k].dtype) for k in diff_rows]
            + [jax.ShapeDtypeStruct(v.shape, F32) for v in vecs],
            compiler_params=_params(("arbitrary",)),
        )(*rows, *vecs, *gs)
        return outs[:nd], outs[nd:]

    return fwd_call, bwd_call


def _f_norm_mod(x, g, scale, shift):
    r = lax.rsqrt(jnp.mean(x * x, axis=-1, keepdims=True) + NORM_EPS)
    return ((x * r * g) * (1.0 + scale) + shift,)


def _f_gate_na(o, z):
    return (o.astype(F32) * _silu(z.astype(F32)),)


def _f_merge(g_na, g_ret, y_na, y_ret):
    return (jax.nn.sigmoid(g_na.astype(F32)) * y_na + jax.nn.sigmoid(g_ret.astype(F32)) * y_ret,)


def _f_residual(x, out, gate):
    return (x + gate * out,)


def _f_loss(x, target, g):
    r = lax.rsqrt(jnp.mean(x * x, axis=-1, keepdims=True) + NORM_EPS)
    y = x * r * g
    e = 0.5 * jnp.mean(jnp.square(y - target), axis=-1, keepdims=True)
    return (jnp.broadcast_to(e * (1.0 / 128.0), (x.shape[0], 128)),)


def _gate_ret_fwd_call(of, ob, z, tm):
    t, w = of.shape
    nh = w // RET_VAL_DIM

    def body(of_ref, ob_ref, z_ref, a_ref):
        for hh in range(nh):
            sl = slice(hh * RET_VAL_DIM, (hh + 1) * RET_VAL_DIM)
            o = of_ref[:, sl] + ob_ref[:, sl]
            r = lax.rsqrt(jnp.mean(o * o, axis=-1, keepdims=True) + NORM_EPS)
            a_ref[:, sl] = ((o * r) * _silu(z_ref[:, sl].astype(F32))).astype(a_ref.dtype)

    spec = pl.BlockSpec((tm, w), lambda i: (i, 0))
    return pl.pallas_call(
        body, name="gate_ret_fwd", grid=(t // tm,), in_specs=[spec, spec, spec], out_specs=spec,
        out_shape=jax.ShapeDtypeStruct((t, w), BF16), compiler_params=_params(("parallel",)),
    )(of, ob, z)


def _gate_ret_bwd_call(of, ob, z, da, tm):
    t, w = of.shape
    nh = w // RET_VAL_DIM

    def body(of_ref, ob_ref, z_ref, da_ref, do_ref, dz_ref):
        for hh in range(nh):
            sl = slice(hh * RET_VAL_DIM, (hh + 1) * RET_VAL_DIM)
            o = of_ref[:, sl] + ob_ref[:, sl]
            r = lax.rsqrt(jnp.mean(o * o, axis=-1, keepdims=True) + NORM_EPS)
            n = o * r
            zf = z_ref[:, sl].astype(F32)
            sg = jax.nn.sigmoid(zf)
            g = da_ref[:, sl].astype(F32)
            dn = g * (zf * sg)
            dz_ref[:, sl] = (g * n * (sg * (1.0 + zf * (1.0 - sg)))).astype(dz_ref.dtype)
            do_ref[:, sl] = r * (dn - n * jnp.mean(dn * n, axis=-1, keepdims=True))

    spec = pl.BlockSpec((tm, w), lambda i: (i, 0))
    return pl.pallas_call(
        body, name="gate_ret_bwd", grid=(t // tm,), in_specs=[spec, spec, spec, spec], out_specs=[spec, spec],
        out_shape=[jax.ShapeDtypeStruct((t, w), F32), jax.ShapeDtypeStruct((t, w), z.dtype)],
        compiler_params=_params(("parallel",)),
    )(of, ob, z, da)


def _na_geometry(t, n_lat):
    rows = n_lat // GRID_W
    kh = min(NA_WIN_ROWS, rows)
    return rows, kh, kh * GRID_W, t - n_lat, t // GRID_W


def _na_row0(r, rows, kh):
    return jnp.clip(r - kh // 2, 0, rows - kh)


def _na_bias_idx(r, rows, kh):
    return jnp.clip(_na_row0(r, rows, kh) - r + (NA_WIN_ROWS - 1), 0, NA_WIN_ROWS - 1)


def _na_fwd_call(q, k, v, bt, n_lat):
    t, w = q.shape
    nh = w // NA_HEAD_DIM
    rows, kh, n_loc, n_ctx, nq = _na_geometry(t, n_lat)
    scale = NA_HEAD_DIM ** -0.5

    def body(q_ref, k_ref, v_ref, bt_ref, o_ref):
        r = pl.program_id(1)
        qb = q_ref[...]
        kc = k_ref[pl.ds(n_lat, n_ctx), :]
        vc = v_ref[pl.ds(n_lat, n_ctx), :]
        s_ctx = _dot_nt(qb, kc) * scale

        @pl.when(r < rows)
        def _():
            start = pl.multiple_of(_na_row0(r, rows, kh) * GRID_W, GRID_W)
            kw = k_ref[pl.ds(start, n_loc), :]
            vw = v_ref[pl.ds(start, n_loc), :]
            s_loc = _dot_nt(qb, kw) * scale + bt_ref[0, 0]
            m = jnp.maximum(jnp.max(s_loc, axis=-1, keepdims=True), jnp.max(s_ctx, axis=-1, keepdims=True))
            p_loc = jnp.exp(s_loc - m)
            p_ctx = jnp.exp(s_ctx - m)
            l = jnp.sum(p_loc, axis=-1, keepdims=True) + jnp.sum(p_ctx, axis=-1, keepdims=True)
            o = _dot(p_loc.astype(BF16), vw) + _dot(p_ctx.astype(BF16), vc)
            o_ref[...] = (o / l).astype(o_ref.dtype)

        @pl.when(r >= rows)
        def _():
            m = jnp.max(s_ctx, axis=-1, keepdims=True)
            p = jnp.exp(s_ctx - m)
            l = jnp.sum(p, axis=-1, keepdims=True)
            o_ref[...] = (_dot(p.astype(BF16), vc) / l).astype(o_ref.dtype)

    return pl.pallas_call(
        body,
        name="na_attn_fwd",
        grid=(nh, nq),
        in_specs=[
            pl.BlockSpec((GRID_W, NA_HEAD_DIM), lambda h, r: (r, h)),
            pl.BlockSpec((t, NA_HEAD_DIM), lambda h, r: (0, h)),
            pl.BlockSpec((t, NA_HEAD_DIM), lambda h, r: (0, h)),
            pl.BlockSpec((1, 1, GRID_W, n_loc), lambda h, r: (h, _na_bias_idx(r, rows, kh), 0, 0)),
        ],
        out_specs=pl.BlockSpec((GRID_W, NA_HEAD_DIM), lambda h, r: (r, h)),
        out_shape=jax.ShapeDtypeStruct((t, w), BF16),
        compiler_params=_params(("parallel", "arbitrary")),
    )(q, k, v, bt)


def _na_bwd_call(q, k, v, bt, do, n_lat):
    t, w = q.shape
    nh = w // NA_HEAD_DIM
    rows, kh, n_loc, n_ctx, nq = _na_geometry(t, n_lat)
    scale = NA_HEAD_DIM ** -0.5

    def body(q_ref, k_ref, v_ref, bt_ref, do_ref, dq_ref, dk_ref, dv_ref, dbt_ref):
        r = pl.program_id(1)

        @pl.when(r == 0)
        def _():
            dk_ref[...] = jnp.zeros_like(dk_ref)
            dv_ref[...] = jnp.zeros_like(dv_ref)

        qb = q_ref[...]
        dob = do_ref[...]
        kc = k_ref[pl.ds(n_lat, n_ctx), :]
        vc = v_ref[pl.ds(n_lat, n_ctx), :]
        s_ctx = _dot_nt(qb, kc) * scale
        dp_ctx = _dot_nt(dob, vc)

        @pl.when(r < rows)
        def _():
            start = pl.multiple_of(_na_row0(r, rows, kh) * GRID_W, GRID_W)
            kw = k_ref[pl.ds(start, n_loc), :]
            vw = v_ref[pl.ds(start, n_loc), :]
            s_loc = _dot_nt(qb, kw) * scale + bt_ref[0, 0]
            m = jnp.maximum(jnp.max(s_loc, axis=-1, keepdims=True), jnp.max(s_ctx, axis=-1, keepdims=True))
            p_loc = jnp.exp(s_loc - m)
            p_ctx = jnp.exp(s_ctx - m)
            inv = 1.0 / (jnp.sum(p_loc, axis=-1, keepdims=True) + jnp.sum(p_ctx, axis=-1, keepdims=True))
            p_loc = p_loc * inv
            p_ctx = p_ctx * inv
            dp_loc = _dot_nt(dob, vw)
            delta = jnp.sum(p_loc * dp_loc, axis=-1, keepdims=True) + jnp.sum(p_ctx * dp_ctx, axis=-1, keepdims=True)
            ds_loc = p_loc * (dp_loc - delta)
            ds_ctx = p_ctx * (dp_ctx - delta)
            first = jnp.logical_or(r == 0, _na_bias_idx(r, rows, kh) != _na_bias_idx(r - 1, rows, kh))

            @pl.when(first)
            def _():
                dbt_ref[0, 0] = ds_loc

            @pl.when(jnp.logical_not(first))
            def _():
                dbt_ref[0, 0] += ds_loc

            dsl = (ds_loc * scale).astype(BF16)
            dsc = (ds_ctx * scale).astype(BF16)
            dq_ref[...] = (_dot(dsl, kw) + _dot(dsc, kc)).astype(dq_ref.dtype)
            dk_ref[pl.ds(start, n_loc), :] += _dot_tn(dsl, qb)
            dv_ref[pl.ds(start, n_loc), :] += _dot_tn(p_loc.astype(BF16), dob)
            dk_ref[pl.ds(n_lat, n_ctx), :] += _dot_tn(dsc, qb)
            dv_ref[pl.ds(n_lat, n_ctx), :] += _dot_tn(p_ctx.astype(BF16), dob)

        @pl.when(r >= rows)
        def _():
            m = jnp.max(s_ctx, axis=-1, keepdims=True)
            p = jnp.exp(s_ctx - m)
            p = p * (1.0 / jnp.sum(p, axis=-1, keepdims=True))
            delta = jnp.sum(p * dp_ctx, axis=-1, keepdims=True)
            dsc = (p * (dp_ctx - delta) * scale).astype(BF16)
            dq_ref[...] = _dot(dsc, kc).astype(dq_ref.dtype)
            dk_ref[pl.ds(n_lat, n_ctx), :] += _dot_tn(dsc, qb)
            dv_ref[pl.ds(n_lat, n_ctx), :] += _dot_tn(p.astype(BF16), dob)

    qspec = pl.BlockSpec((GRID_W, NA_HEAD_DIM), lambda h, r: (r, h))
    kspec = pl.BlockSpec((t, NA_HEAD_DIM), lambda h, r: (0, h))
    bspec = pl.BlockSpec((1, 1, GRID_W, n_loc), lambda h, r: (h, _na_bias_idx(r, rows, kh), 0, 0))
    return pl.pallas_call(
        body,
        name="na_attn_bwd",
        grid=(nh, nq),
        in_specs=[qspec, kspec, kspec, bspec, qspec],
        out_specs=[qspec, kspec, kspec, bspec],
        out_shape=[
            jax.ShapeDtypeStruct((t, w), BF16),
            jax.ShapeDtypeStruct((t, w), F32),
            jax.ShapeDtypeStruct((t, w), F32),
            jax.ShapeDtypeStruct(bt.shape, F32),
        ],
        compiler_params=_params(("parallel", "arbitrary")),
    )(q, k, v, bt, do)


def _na_bias_table(rpb, rows):
    kh = min(NA_WIN_ROWS, rows)
    nj = NA_WIN_ROWS
    e1 = np.zeros((nj, kh, 2 * NA_WIN_ROWS - 1), np.float32)
    for j in range(nj):
        for kk in range(kh):
            if j + kk < 2 * NA_WIN_ROWS - 1:
                e1[j, kk, j + kk] = 1.0
    cidx = np.arange(GRID_W)
    dc = np.clip(cidx[None, :] - cidx[:, None] + (NA_WIN_COLS - 1), 0, 2 * NA_WIN_COLS - 2)
    e2 = np.zeros((GRID_W, GRID_W, 2 * NA_WIN_COLS - 1), np.float32)
    e2[np.arange(GRID_W)[:, None], np.arange(GRID_W)[None, :], dc] = 1.0
    c0 = np.clip(cidx - NA_WIN_COLS // 2, 0, GRID_W - NA_WIN_COLS)
    col_in = (cidx[None, :] >= c0[:, None]) & (cidx[None, :] < c0[:, None] + NA_WIN_COLS)
    t1 = jnp.einsum("hab,jka->hjkb", rpb, jnp.asarray(e1), precision=lax.Precision.HIGHEST)
    b = jnp.einsum("hjkb,cwb->hjckw", t1, jnp.asarray(e2), precision=lax.Precision.HIGHEST)
    b = jnp.where(jnp.asarray(col_in)[None, None, :, None, :], b, MASK_VALUE)
    return b.reshape(rpb.shape[0], nj, GRID_W, kh * GRID_W)


def _ret_decays(lam_s, reverse):
    c = RET_CHUNK
    ii = lax.broadcasted_iota(jnp.int32, (c, c), 0)
    jj = lax.broadcasted_iota(jnp.int32, (c, c), 1)
    d = (jj - ii) if reverse else (ii - jj)
    dpos = jnp.maximum(d.astype(F32), 0.0)
    mask = jnp.where(d >= 0, jnp.exp(dpos * lam_s), 0.0)
    pi = lax.broadcasted_iota(jnp.int32, (c, 1), 0).astype(F32)
    qpos = (c - pi) if reverse else (pi + 1.0)
    kpos = pi if reverse else (c - 1.0 - pi)
    qd = jnp.exp(qpos * lam_s)
    kd = jnp.exp(kpos * lam_s)
    g = jnp.exp(jnp.full((1, RET_VAL_DIM), c * lam_s, F32))
    return mask, dpos, qd, kd, qpos, kpos, g


def _ret_chunk_of(t, nt, nl, reverse):
    return (nt - 1 - t) if reverse else (t + nl) % nt


def _ret_fwd_call(qr, kr, v, lam, n_lat, reverse):
    t = qr.shape[0]
    nh = qr.shape[1] // RET_KEY_DIM
    c = RET_CHUNK
    nt, nl = t // c, n_lat // c

    def body(lam_ref, q_ref, k_ref, v_ref, o_ref, s_ref, state):
        h, step = pl.program_id(0), pl.program_id(1)

        @pl.when(step == 0)
        def _():
            state[...] = jnp.zeros_like(state)

        mask, _, qd, kd, _, _, g = _ret_decays(lam_ref[h], reverse)
        q, k, vv = q_ref[...], k_ref[...], v_ref[...]
        p = _dot_nt(q, k) * mask
        s = state[...]
        qs = (q.astype(F32) * qd).astype(BF16)
        o_ref[...] = _dot(p.astype(BF16), vv) + _dot(qs, s.astype(BF16))
        s_ref[0, 0] = s
        ks = (k.astype(F32) * kd).astype(BF16)
        state[...] = s * g + _dot_tn(ks, vv)

    def cmap(h, step, lam_ref):
        return (_ret_chunk_of(step, nt, nl, reverse), h)

    return pl.pallas_call(
        body,
        name="retention_rev_fwd" if reverse else "retention_fwd",
        grid_spec=pltpu.PrefetchScalarGridSpec(
            num_scalar_prefetch=1,
            grid=(nh, nt),
            in_specs=[
                pl.BlockSpec((c, RET_KEY_DIM), cmap),
                pl.BlockSpec((c, RET_KEY_DIM), cmap),
                pl.BlockSpec((c, RET_VAL_DIM), cmap),
            ],
            out_specs=[
                pl.BlockSpec((c, RET_VAL_DIM), cmap),
                pl.BlockSpec((1, 1, RET_KEY_DIM, RET_VAL_DIM), lambda h, step, lam_ref: (h, step, 0, 0)),
            ],
            scratch_shapes=[pltpu.VMEM((RET_KEY_DIM, RET_VAL_DIM), F32)],
        ),
        out_shape=[
            jax.ShapeDtypeStruct((t, nh * RET_VAL_DIM), F32),
            jax.ShapeDtypeStruct((nh, nt, RET_KEY_DIM, RET_VAL_DIM), F32),
        ],
        compiler_params=_params(("parallel", "arbitrary")),
    )(lam, qr, kr, v)


def _ret_bwd_call(qr, kr, v, lam, states, do, n_lat, reverse):
    t = qr.shape[0]
    nh = qr.shape[1] // RET_KEY_DIM
    c = RET_CHUNK
    nt, nl = t // c, n_lat // c

    def body(lam_ref, q_ref, k_ref, v_ref, s_ref, do_ref, dq_ref, dk_ref, dv_ref, dl_ref, dstate):
        h, rstep = pl.program_id(0), pl.program_id(1)

        @pl.when(rstep == 0)
        def _():
            dstate[...] = jnp.zeros_like(dstate)
            dl_ref[...] = jnp.zeros_like(dl_ref)

        mask, dpos, qd, kd, qpos, kpos, g = _ret_decays(lam_ref[h], reverse)
        q, k, vv = q_ref[...], k_ref[...], v_ref[...]
        qf, kf = q.astype(F32), k.astype(F32)
        s = s_ref[0, 0]
        ds = dstate[...]
        dob = do_ref[...].astype(BF16)
        sb, dsb = s.astype(BF16), ds.astype(BF16)
        a = _dot_nt(q, k)
        p = a * mask
        dp = _dot_nt(dob, vv)
        da = dp * mask
        dab = da.astype(BF16)
        dqc = _dot_nt(dob, sb)
        dkc = _dot_nt(vv, dsb)
        qs = (qf * qd).astype(BF16)
        ks = (kf * kd).astype(BF16)
        dq_ref[...] = (_dot(dab, k) + dqc * qd).astype(dq_ref.dtype)
        dk_ref[...] = (_dot_tn(dab, q) + dkc * kd).astype(dk_ref.dtype)
        dv_ref[...] = (_dot_tn(p.astype(BF16), dob) + _dot(ks, dsb)).astype(dv_ref.dtype)
        terms = (
            jnp.sum(jnp.sum(da * a * dpos, axis=1, keepdims=True), axis=0, keepdims=True)
            + jnp.sum(jnp.sum(dqc * qf * (qd * qpos), axis=1, keepdims=True), axis=0, keepdims=True)
            + jnp.sum(jnp.sum(dkc * kf * (kd * kpos), axis=1, keepdims=True), axis=0, keepdims=True)
            + jnp.sum(jnp.sum(ds * s * (g * c), axis=1, keepdims=True), axis=0, keepdims=True)
        )
        dl_ref[0] += jnp.broadcast_to(terms, (8, 128))
        dstate[...] = ds * g + _dot_tn(qs, dob)

    def cmap(h, rstep, lam_ref):
        return (_ret_chunk_of(nt - 1 - rstep, nt, nl, reverse), h)

    return pl.pallas_call(
        body,
        name="retention_rev_bwd" if reverse else "retention_bwd",
        grid_spec=pltpu.PrefetchScalarGridSpec(
            num_scalar_prefetch=1,
            grid=(nh, nt),
            in_specs=[
                pl.BlockSpec((c, RET_KEY_DIM), cmap),
                pl.BlockSpec((c, RET_KEY_DIM), cmap),
                pl.BlockSpec((c, RET_VAL_DIM), cmap),
                pl.BlockSpec((1, 1, RET_KEY_DIM, RET_VAL_DIM), lambda h, rstep, lam_ref: (h, nt - 1 - rstep, 0, 0)),
                pl.BlockSpec((c, RET_VAL_DIM), cmap),
            ],
            out_specs=[
                pl.BlockSpec((c, RET_KEY_DIM), cmap),
                pl.BlockSpec((c, RET_KEY_DIM), cmap),
                pl.BlockSpec((c, RET_VAL_DIM), cmap),
                pl.BlockSpec((1, 8, 128), lambda h, rstep, lam_ref: (h, 0, 0)),
            ],
            scratch_shapes=[pltpu.VMEM((RET_KEY_DIM, RET_VAL_DIM), F32)],
        ),
        out_shape=[
            jax.ShapeDtypeStruct(qr.shape, qr.dtype),
            jax.ShapeDtypeStruct(kr.shape, kr.dtype),
            jax.ShapeDtypeStruct(v.shape, v.dtype),
            jax.ShapeDtypeStruct((nh, 8, 128), F32),
        ],
        compiler_params=_params(("parallel", "arbitrary")),
    )(lam, qr, kr, v, states, do)


def _rope_tables(t, n_lat):
    nf = RET_KEY_DIM // 4
    tok = np.arange(n_lat)
    inv_freq = (ROPE_BASE ** (-np.arange(nf, dtype=np.float32) / nf)).astype(np.float32)
    row = (tok // GRID_W).astype(np.float32)
    col = (tok % GRID_W).astype(np.float32)
    ang = np.concatenate([row[:, None] * inv_freq, col[:, None] * inv_freq], axis=-1).astype(np.float32)
    cos = np.ones((t, 2 * nf), np.float32)
    sin = np.zeros((t, 2 * nf), np.float32)
    cos[:n_lat] = np.cos(ang)
    sin[:n_lat] = np.sin(ang)
    return jnp.asarray(cos), jnp.asarray(sin)


def _rope(xb, cos, sin, mult):
    t, w = xb.shape
    nh = w // RET_KEY_DIM
    half = RET_KEY_DIM // 2
    x = xb.astype(F32).reshape(t, nh, 2, half)
    x1, x2 = x[:, :, 0], x[:, :, 1]
    c, s = cos[:, None, :], sin[:, None, :]
    out = jnp.stack([x1 * c - x2 * s, x2 * c + x1 * s], axis=2) * mult
    return out.reshape(t, w).astype(BF16)


def _my_position():
    return lax.axis_index("x"), lax.axis_index("y"), lax.axis_index("c")


def _flip(pos, k):
    x, y, c = pos
    return (1 - x if k & 4 else x, 1 - y if k & 2 else y, 1 - c if k & 1 else c)


def _linear(pos):
    return 4 * pos[0] + 2 * pos[1] + pos[2]


def _slab(ref, axis, idx, size):
    start = pl.multiple_of(idx * size, size)
    return ref.at[pl.ds(start, size), :] if axis == 0 else ref.at[:, pl.ds(start, size)]


HBM_SPEC = pl.BlockSpec(memory_space=pltpu.HBM)
SEM_SPEC = pl.BlockSpec(memory_space=pltpu.SEMAPHORE)
DATAFLOW = pltpu.SideEffectType.DATAFLOW_SIDE_EFFECTING
PEER_BITS = (1, 2, 4, 6, 3, 5, 7)
GATHER_BITS = (1, 2, 4, 6)


def _in_hbm(a):
    return pltpu.with_memory_space_constraint(a, pltpu.HBM)


def _gather_views(me, k, a, src_refs, land_refs, axes):
    size = src_refs[a].shape[axes[a]]
    peer = _flip(me, k)
    return src_refs[a], _slab(land_refs[a], axes[a], _linear(me), size), _slab(land_refs[a], axes[a], _linear(peer), size)


def _scatter_views(me, k, a, src_refs, land_refs, axes):
    size = land_refs[a].shape[1 + axes[a]]
    peer = _flip(me, k)
    return _slab(src_refs[a], axes[a], _linear(peer), size), land_refs[a].at[k - 1], land_refs[a].at[k - 1]


def _slab_block(rows, cols, tm, axis):
    if axis == 0:
        return pl.BlockSpec((tm, cols), lambda i, idx: (idx[0] * (rows // tm) + i, 0))
    return pl.BlockSpec((tm, cols), lambda i, idx: (i, idx[0]))


def _place_shard(shard, land, axis, my_idx):
    r, c = shard.shape
    tm = _pick(r, (512, 256, 128, 64, 32, 16))

    def body(idx_ref, s_ref, land_ref, o_ref):
        o_ref[...] = s_ref[...]

    return pl.pallas_call(
        body, name="gather_place",
        grid_spec=pltpu.PrefetchScalarGridSpec(
            num_scalar_prefetch=1, grid=(r // tm,),
            in_specs=[pl.BlockSpec((tm, c), lambda i, idx: (i, 0)), ANY],
            out_specs=_slab_block(r, c, tm, axis)),
        out_shape=jax.ShapeDtypeStruct(land.shape, land.dtype),
        input_output_aliases={2: 0},
        compiler_params=_params(("parallel",)),
    )(my_idx, shard, land)


def _push_start(name, srcs, lands, axes, views, bits, deps):
    ns = len(srcs)

    def body(*refs):
        src_refs, land_refs = refs[:ns], refs[ns:2 * ns]
        send_sems, recv_sems = refs[2 * ns + len(deps):2 * ns + len(deps) + 2]
        token = refs[-1]
        me = _my_position()
        for k in bits:
            for a in range(ns):
                s, d, _ = views(me, k, a, src_refs, land_refs, axes)
                pltpu.make_async_remote_copy(
                    src_ref=s, dst_ref=d, send_sem=send_sems.at[7 * a + k - 1], recv_sem=recv_sems.at[7 * a + k - 1],
                    device_id=_flip(me, k), device_id_type=MESH).start()
        token[...] = jnp.zeros_like(token)

    thru = [pltpu.HBM(a.shape, a.dtype) for a in list(srcs) + list(lands)]
    outs = pl.pallas_call(
        body, name=name,
        in_specs=[HBM_SPEC] * (2 * ns) + [ANY] * len(deps),
        out_specs=[SEM_SPEC, SEM_SPEC] + [HBM_SPEC] * (2 * ns) + [VMEM_SPEC],
        out_shape=[pltpu.SemaphoreType.DMA((7 * ns,)), pltpu.SemaphoreType.DMA((7 * ns,))] + thru
        + [jax.ShapeDtypeStruct((8, 128), F32)],
        input_output_aliases={i: 2 + i for i in range(2 * ns)},
        compiler_params=pltpu.CompilerParams(has_side_effects=DATAFLOW),
    )(*[_in_hbm(a) for a in srcs], *[_in_hbm(a) for a in lands], *deps)
    return (outs[0], outs[1]), outs[2:2 + ns], outs[2 + ns:2 + 2 * ns], outs[-1]


def _gather_finish(lands, axes, sizes):
    ns = len(lands)
    chips = (2, 4, 6)

    def body(*refs):
        land_refs = refs[ns:2 * ns]
        send_sems, recv_sems = refs[2 * ns:]
        me = _my_position()
        sibling = _flip(me, 1)
        copies = []
        for j, kc in enumerate(chips):
            for a in range(ns):
                def slab_of(pos):
                    return _slab(land_refs[a], axes[a], _linear(pos), sizes[a])
                send = pltpu.make_async_remote_copy(
                    src_ref=slab_of(_flip(me, kc)), dst_ref=slab_of(_flip(me, kc)), send_sem=send_sems.at[3 * a + j],
                    recv_sem=recv_sems.at[3 * a + j], device_id=sibling, device_id_type=MESH)
                recv = pltpu.make_async_remote_copy(
                    src_ref=slab_of(_flip(me, kc)), dst_ref=slab_of(_flip(sibling, kc)), send_sem=send_sems.at[3 * a + j],
                    recv_sem=recv_sems.at[3 * a + j], device_id=sibling, device_id_type=MESH)
                send.start()
                copies.append((send, recv))
        for send, recv in copies:
            recv.wait_recv()
        for send, recv in copies:
            send.wait_send()

    return pl.pallas_call(
        body, name="gather_finish", in_specs=[ANY] * ns, out_specs=[ANY] * ns,
        out_shape=[jax.ShapeDtypeStruct(l.shape, l.dtype) for l in lands],
        input_output_aliases={a: a for a in range(ns)},
        scratch_shapes=[pltpu.SemaphoreType.DMA((3 * ns,)), pltpu.SemaphoreType.DMA((3 * ns,))],
        compiler_params=pltpu.CompilerParams(has_side_effects=True),
    )(*lands)


def _push_wait(name, sems, srcs, lands, axes, views, bits, after):
    ns = len(srcs)

    def body(*refs):
        src_refs, land_refs = refs[:ns], refs[ns:2 * ns]
        send_sems, recv_sems = refs[2 * ns:2 * ns + 2]
        me = _my_position()
        for k in bits:
            for a in range(ns):
                s, d, got = views(me, k, a, src_refs, land_refs, axes)
                cp = pltpu.make_async_remote_copy(
                    src_ref=s, dst_ref=got, send_sem=send_sems.at[7 * a + k - 1], recv_sem=recv_sems.at[7 * a + k - 1],
                    device_id=_flip(me, k), device_id_type=MESH)
                cp.wait_send()
                cp.wait_recv()

    thru = [pltpu.HBM(a.shape, a.dtype) for a in list(srcs) + list(lands)]
    outs = pl.pallas_call(
        body, name=name,
        in_specs=[HBM_SPEC] * (2 * ns) + [SEM_SPEC, SEM_SPEC] + [ANY] * len(after),
        out_specs=[HBM_SPEC] * (2 * ns),
        out_shape=thru,
        input_output_aliases={i: i for i in range(2 * ns)},
        compiler_params=pltpu.CompilerParams(has_side_effects=DATAFLOW),
    )(*srcs, *lands, sems[0], sems[1], *after)
    return outs[:ns], outs[ns:]


def _small_allgather(v, name):
    r, c = v.shape

    def body(v_ref, all_ref, sum_ref, send_sems, recv_sems):
        me = _my_position()
        all_ref[_linear(me)] = v_ref[...]
        copies = []
        for k in range(1, N_DEV):
            peer = _flip(me, k)
            copies.append(pltpu.make_async_remote_copy(
                src_ref=v_ref, dst_ref=all_ref.at[_linear(me)], send_sem=send_sems.at[k - 1], recv_sem=recv_sems.at[k - 1],
                device_id=peer, device_id_type=MESH))
        for cp in copies:
            cp.start()
        for k in range(1, N_DEV):
            peer = _flip(me, k)
            pltpu.make_async_remote_copy(
                src_ref=v_ref, dst_ref=all_ref.at[_linear(peer)], send_sem=send_sems.at[k - 1], recv_sem=recv_sems.at[k - 1],
                device_id=peer, device_id_type=MESH).wait_recv()
        for cp in copies:
            cp.wait_send()
        acc = all_ref[0]
        for d in range(1, N_DEV):
            acc = acc + all_ref[d]
        sum_ref[...] = acc

    return pl.pallas_call(
        body,
        name=name,
        in_specs=[VMEM_SPEC],
        out_specs=[VMEM_SPEC, VMEM_SPEC],
        out_shape=[jax.ShapeDtypeStruct((N_DEV, r, c), F32), jax.ShapeDtypeStruct((r, c), F32)],
        scratch_shapes=[pltpu.SemaphoreType.DMA((N_DEV - 1,)), pltpu.SemaphoreType.DMA((N_DEV - 1,))],
        compiler_params=pltpu.CompilerParams(has_side_effects=True, vmem_limit_bytes=VMEM_LIMIT),
    )(v)


def _ada_fwd_call(cin, ada_w, ada_b_cols):
    nl, d, ncol = ada_w.shape
    nrow = cin.shape[0]

    def body(c_ref, w_ref, b_ref, o_ref):
        cs = _silu(c_ref[...]).astype(BF16)
        for l in range(nl):
            o_ref[l] = _dot(cs, w_ref[l].astype(BF16)) + b_ref[l]

    return pl.pallas_call(
        body, name="ada_fwd", in_specs=[VMEM_SPEC] * 3, out_specs=VMEM_SPEC,
        out_shape=jax.ShapeDtypeStruct((nl, nrow, ncol), F32), compiler_params=_params(),
    )(cin, ada_w, ada_b_cols)


def _ada_bwd_call(cin, ada_w, dmod):
    nl, d, ncol = ada_w.shape
    nrow = cin.shape[0]

    def body(c_ref, w_ref, dm_ref, gw_ref, dcs_ref):
        cs = _silu(c_ref[...]).astype(BF16)
        acc = jnp.zeros((nrow, d), F32)
        for l in range(nl):
            dm = dm_ref[l].astype(BF16)
            gw_ref[l] = _dot_tn(cs, dm)
            acc = acc + _dot_nt(dm, w_ref[l].astype(BF16))
        dcs_ref[...] = acc

    return pl.pallas_call(
        body, name="ada_bwd", in_specs=[VMEM_SPEC] * 3, out_specs=[VMEM_SPEC, VMEM_SPEC],
        out_shape=[jax.ShapeDtypeStruct((nl, d, ncol), F32), jax.ShapeDtypeStruct((nrow, d), F32)],
        compiler_params=_params(),
    )(cin, ada_w, dmod)


def _adamw_math(w, g, m, v):
    m = ADAM_B1 * m + (1.0 - ADAM_B1) * g
    v = ADAM_B2 * v + (1.0 - ADAM_B2) * jnp.square(g)
    m_hat = m / (1.0 - ADAM_B1 ** ADAM_STEP)
    v_hat = v / (1.0 - ADAM_B2 ** ADAM_STEP)
    delta = -ADAM_LR * (m_hat / (jnp.sqrt(v_hat) + ADAM_EPS) + ADAM_WD * w)
    return delta, m, v


def _adamw_sharded(w, m, v, mine, slabs, axis, my_idx, layer, prev, name):
    nl, r, c = w.shape
    tm = _pick(r, (128, 64, 32, 16))
    nprev = 0 if prev is None else len(prev)

    def body(idx_ref, w_ref, m_ref, v_ref, mine_ref, s_ref, *rest):
        g_ref, d_ref, nm_ref, nv_ref = rest[nprev:]
        g = mine_ref[...].astype(F32)
        for k in range(N_DEV - 1):
            g = g + s_ref[k].astype(F32)
        delta, nm, nv = _adamw_math(w_ref[0], g, m_ref[0], v_ref[0])
        g_ref[0], d_ref[0], nm_ref[0], nv_ref[0] = g, delta, nm, nv

    spec = pl.BlockSpec((1, tm, c), lambda i, idx: (layer, i, 0))
    out = jax.ShapeDtypeStruct(w.shape, F32)
    return pl.pallas_call(
        body, name=name,
        grid_spec=pltpu.PrefetchScalarGridSpec(
            num_scalar_prefetch=1, grid=(r // tm,),
            in_specs=[spec, spec, spec, _slab_block(r, c, tm, axis),
                      pl.BlockSpec((N_DEV - 1, tm, c), lambda i, idx: (0, i, 0))] + [ANY] * nprev,
            out_specs=[spec] * 4),
        out_shape=[out] * 4,
        input_output_aliases={6 + j: j for j in range(nprev)},
        compiler_params=_params(("parallel",)),
    )(my_idx, w, m, v, mine, slabs, *(() if prev is None else prev))


def _adamw_dense(w, g, m, v, name):
    r, c = w.shape
    tm = _pick(r, (256, 128, 64, 32, 16, 8))

    def body(w_ref, g_ref, m_ref, v_ref, d_ref, nm_ref, nv_ref):
        d_ref[...], nm_ref[...], nv_ref[...] = _adamw_math(w_ref[...], g_ref[...], m_ref[...], v_ref[...])

    spec = pl.BlockSpec((tm, c), lambda i: (i, 0))
    out = jax.ShapeDtypeStruct(w.shape, F32)
    return pl.pallas_call(
        body, name=name, grid=(r // tm,), in_specs=[spec] * 4, out_specs=[spec] * 3, out_shape=[out] * 3,
        compiler_params=_params(("parallel",)),
    )(w, g, m, v)


def _pack(parts, width=128):
    flat = jnp.concatenate([p.reshape(-1).astype(F32) for p in parts])
    n = flat.shape[0]
    total = -(-n // (8 * width)) * (8 * width)
    return jnp.pad(flat, (0, total - n)).reshape(total // width, width)


def _unpack(buf, shapes):
    flat = buf.reshape(-1)
    out, off = [], 0
    for s in shapes:
        n = int(np.prod(s))
        out.append(flat[off:off + n].reshape(s))
        off += n
    return out


def kernel(x, c, ctx, c_ctx, ada_w, ada_b, norm_g, w_in, na_rpb, ret_decay_logit, w_proj_na, w_proj_ret, w_out, final_g, loss_target, m_c_ctx, m_ada_w, m_ada_b, m_norm_g, m_w_in, m_na_rpb, m_ret_decay_logit, m_w_proj_na, m_w_proj_ret, m_w_out, m_final_g, v_c_ctx, v_ada_w, v_ada_b, v_norm_g, v_w_in, v_na_rpb, v_ret_decay_logit, v_w_proj_na, v_w_proj_ret, v_w_out, v_final_g):
    depth = w_in.shape[0]
    n_lat, d = x.shape[1], x.shape[2]
    n_ctx = ctx.shape[1]
    t = n_lat + n_ctx
    w_na = w_proj_na.shape[1]
    w_retv = w_proj_ret.shape[1] * N_DEV
    in_cols = w_in.shape[2] * N_DEV
    w_qk = (in_cols - 4 * w_na - 2 * w_retv - 2 * d) // 2
    sizes = (w_na, w_na, w_na, w_na, w_qk, w_qk, w_retv, w_retv, d, d)
    off = tuple(int(o) for o in np.cumsum((0,) + sizes))
    NA_Q, NA_K, NA_V, NA_Z, RET_Q, RET_K, RET_V, RET_Z, G_NA, G_RET = range(10)
    rows = n_lat // GRID_W
    me = _my_position()
    my_idx = _linear(me)
    tm_row = _pick(n_ctx, (256, 128))

    idx_arr = jnp.reshape(my_idx, (1,)).astype(jnp.int32)

    ncol = ada_w.shape[2]
    c_all, _ = _small_allgather(jnp.pad(c, ((0, 7), (0, 0))), "allgather_c")
    cin = jnp.concatenate([c_all[:, 0, :], c_ctx[None, :], jnp.zeros((7, d), F32)], axis=0)
    ada_b_cols = lax.dynamic_slice_in_dim(ada_b, my_idx * ncol, ncol, axis=1)[:, None, :]
    mod_cols = _ada_fwd_call(cin, ada_w, ada_b_cols)
    mod_gathered, _ = _small_allgather(mod_cols.reshape(depth * 16, ncol), "allgather_mod")
    mod_all = mod_gathered.reshape(N_DEV, depth, 16, ncol).transpose(1, 2, 0, 3).reshape(depth, 16, N_DEV * ncol)
    mod_lat = lax.dynamic_index_in_dim(mod_all, my_idx, axis=1, keepdims=False)
    mod_ctx = mod_all[:, 8, :]

    w_axes = (1, 1, 0, 0)
    w_names = ("w_in", "w_proj_na", "w_proj_ret", "w_out")
    shard = [[w[l].astype(BF16) for w in (w_in, w_proj_na, w_proj_ret, w_out)] for l in range(depth)]
    groups = [[(0, 0)], [(0, 1), (0, 2), (0, 3)]] + [[(l, a) for a in range(4)] for l in range(1, depth)]
    gathers, token = {}, mod_gathered
    for gi, keys in enumerate(groups):
        srcs = [shard[l][a] for l, a in keys]
        axes = tuple(w_axes[a] for _, a in keys)
        lands = [_place_shard(s, lax.empty(tuple(n * (N_DEV if i == ax else 1) for i, n in enumerate(s.shape)), BF16),
                              ax, idx_arr) for s, ax in zip(srcs, axes)]
        sizes = tuple(s.shape[ax] for s, ax in zip(srcs, axes))
        sems, srcs, lands, token = _push_start(
            f"gather_start_{gi}", srcs, lands, axes, _gather_views, GATHER_BITS, (token,))
        flight = dict(name=f"gather_wait_{gi}", sems=sems, srcs=srcs, lands=lands, axes=axes, sizes=sizes, ready=None)
        for pos, key in enumerate(keys):
            gathers[key] = (flight, pos)

    def landed(l, a, act):
        flight, pos = gathers[(l, a)]
        if flight["ready"] is None:
            arrived = _push_wait(flight["name"], flight["sems"], flight["srcs"], flight["lands"],
                                 flight["axes"], _gather_views, GATHER_BITS, (act, token))[1]
            flight["ready"] = _gather_finish(arrived, flight["axes"], flight["sizes"])
        return flight["ready"][pos]

    pending, scatters = {}, []

    def send_dw(l, a, dw):
        pending[(l, a)] = dw
        if a == 0:
            keys = [(0, 0)] if l == 0 else [(l, b) for b in range(4)]
        elif l == 0 and a == 1:
            keys = [(0, 1), (0, 2), (0, 3)]
        else:
            return None
        srcs = [pending[k] for k in keys]
        axes = tuple(w_axes[b] for _, b in keys)
        lands = [lax.empty((N_DEV - 1,) + tuple(n // (N_DEV if i == ax else 1) for i, n in enumerate(s.shape)), BF16)
                 for s, ax in zip(srcs, axes)]
        sems, srcs, lands, tok = _push_start(
            f"scatter_start_{len(scatters)}", srcs, lands, axes, _scatter_views, PEER_BITS, ())
        scatters.append(dict(name=f"scatter_wait_{len(scatters)}", sems=sems, srcs=srcs, lands=lands, axes=axes, keys=keys))
        return tok

    cos, sin = _rope_tables(t, n_lat)
    k_scale = RET_KEY_DIM ** -0.5
    norm_mod_fwd, norm_mod_bwd = _make_rowwise(_f_norm_mod, "norm_mod", (BF16,), (d,), n_lat, tm_row, (0,))
    gate_na_fwd, gate_na_bwd = _make_rowwise(_f_gate_na, "gate_na", (BF16,), (w_na,), n_lat, tm_row, (0, 1))
    merge_fwd, merge_bwd = _make_rowwise(_f_merge, "merge", (BF16,), (d,), n_lat, tm_row, (0, 1, 2, 3))
    residual_fwd, residual_bwd = _make_rowwise(_f_residual, "residual", (F32,), (d,), n_lat, tm_row, (0, 1))
    loss_fwd, loss_bwd = _make_rowwise(_f_loss, "loss_head", (F32,), (128,), n_lat, tm_row, (0,))

    def pair(a, b):
        return jnp.stack([a, b])[:, None, :]

    def mod_vectors(mod_lat_l, mod_ctx_l, norm_g_l):
        shift, scale, gate = jnp.split(mod_lat_l, 3)
        c_shift, c_scale, c_gate = jnp.split(mod_ctx_l, 3)
        return pair(norm_g_l, norm_g_l), pair(scale, c_scale), pair(shift, c_shift), pair(gate, c_gate)

    def split_u(u):
        blk = [u[:, off[i]:off[i + 1]] for i in range(10)]
        blk[RET_Q] = _rope(blk[RET_Q], cos, sin, 1.0)
        blk[RET_K] = _rope(blk[RET_K], cos, sin, k_scale)
        return tuple(blk)

    def log_decay(logit):
        return jax.nn.log_sigmoid(logit.astype(F32))

    xa = jnp.concatenate([x[0], ctx[0]], axis=0)
    saved = []
    for l in range(depth):
        vecs, vecs_vjp = jax.vjp(mod_vectors, mod_lat[l], mod_ctx[l], norm_g[l])
        (h,) = norm_mod_fwd((xa,), vecs[:3])
        wl_in = landed(l, 0, h)
        u = _matmul(h, wl_in, out_dtype=BF16, name="in_proj_fwd")
        blk, split_vjp = jax.vjp(split_u, u)
        bt, bt_vjp = jax.vjp(lambda r: _na_bias_table(r, rows), na_rpb[l])
        lam, lam_vjp = jax.vjp(log_decay, ret_decay_logit[l])
        o_na = _na_fwd_call(blk[NA_Q], blk[NA_K], blk[NA_V], bt, n_lat)
        o_f, st_f = _ret_fwd_call(blk[RET_Q], blk[RET_K], blk[RET_V], lam[0], n_lat, False)
        o_b, st_b = _ret_fwd_call(blk[RET_Q], blk[RET_K], blk[RET_V], lam[1], n_lat, True)
        (a_na,) = gate_na_fwd((o_na, blk[NA_Z]), ())
        a_ret = _gate_ret_fwd_call(o_f, o_b, blk[RET_Z], tm_row)
        wl_pna, wl_pret, wl_out = landed(l, 1, a_na), landed(l, 2, a_na), landed(l, 3, a_na)
        y_na = _matmul(a_na, wl_pna, out_dtype=F32, name="proj_na_fwd")
        y_ret = _matmul(a_ret, wl_pret, out_dtype=F32, name="proj_ret_fwd")
        (merged,) = merge_fwd((blk[G_NA], blk[G_RET], y_na, y_ret), ())
        out = _matmul(merged, wl_out, out_dtype=F32, name="out_proj_fwd")
        (xa_next,) = residual_fwd((xa, out), vecs[3:])
        saved.append(dict(xa=xa, vecs=vecs, vecs_vjp=vecs_vjp, h=h, w=(wl_in, wl_pna, wl_pret, wl_out), blk=blk,
                          split_vjp=split_vjp, bt=bt, bt_vjp=bt_vjp, lam=lam, lam_vjp=lam_vjp, o_na=o_na, o_f=o_f,
                          o_b=o_b, st_f=st_f, st_b=st_b, a_na=a_na, a_ret=a_ret, y_na=y_na, y_ret=y_ret,
                          merged=merged, out=out))
        xa = xa_next

    fg_pair, fg_vjp = jax.vjp(lambda g: pair(g, g), final_g)
    x_last = xa[:n_lat]
    (loss_rows,) = loss_fwd((x_last, loss_target[0]), (fg_pair,))
    loss = lax.psum(jnp.sum(loss_rows), ("x", "y", "c"))
    (dx_last,), (d_fg_pair,) = loss_bwd((x_last, loss_target[0]), (fg_pair,), (jnp.ones_like(loss_rows),))
    (d_final_g,) = fg_vjp(d_fg_pair)
    dxa = jnp.pad(dx_last, ((0, n_ctx), (0, 0)))

    d_mod_lat, d_mod_ctx, d_norm_g, d_rpb, d_decay = ([None] * depth for _ in range(5))
    for l in reversed(range(depth)):
        s = saved[l]
        blk = s["blk"]
        wl_in, wl_pna, wl_pret, wl_out = s["w"]
        (dxa_res, d_out), (d_gate,) = residual_bwd((s["xa"], s["out"]), s["vecs"][3:], (dxa,))
        d_out = d_out.astype(BF16)
        send_dw(l, 3, _matmul(s["merged"], d_out, trans_a=True, out_dtype=BF16, name="out_proj_dw"))
        d_merged = _matmul(d_out, wl_out, trans_b=True, out_dtype=BF16, name="out_proj_da")
        (dg_na, dg_ret, dy_na, dy_ret), _ = merge_bwd((blk[G_NA], blk[G_RET], s["y_na"], s["y_ret"]), (), (d_merged,))
        dy_na, dy_ret = dy_na.astype(BF16), dy_ret.astype(BF16)
        send_dw(l, 2, _matmul(s["a_ret"], dy_ret, trans_a=True, out_dtype=BF16, name="proj_ret_dw"))
        da_ret = _matmul(dy_ret, wl_pret, trans_b=True, out_dtype=BF16, name="proj_ret_da")
        tok = send_dw(l, 1, _matmul(s["a_na"], dy_na, trans_a=True, out_dtype=BF16, name="proj_na_dw"))
        da_na = _matmul(dy_na, wl_pna, trans_b=True, out_dtype=BF16, name="proj_na_da", after=tok)
        do_ret, dz_ret = _gate_ret_bwd_call(s["o_f"], s["o_b"], blk[RET_Z], da_ret, tm_row)
        (do_na, dz_na), _ = gate_na_bwd((s["o_na"], blk[NA_Z]), (), (da_na,))
        dq_f, dk_f, dv_f, dl_f = _ret_bwd_call(blk[RET_Q], blk[RET_K], blk[RET_V], s["lam"][0], s["st_f"], do_ret, n_lat, False)
        dq_b, dk_b, dv_b, dl_b = _ret_bwd_call(blk[RET_Q], blk[RET_K], blk[RET_V], s["lam"][1], s["st_b"], do_ret, n_lat, True)
        dq, dk, dv, dbt = _na_bwd_call(blk[NA_Q], blk[NA_K], blk[NA_V], s["bt"], do_na, n_lat)
        d_blk = [None] * 10
        d_blk[NA_Q], d_blk[NA_K], d_blk[NA_V], d_blk[NA_Z] = dq, dk.astype(BF16), dv.astype(BF16), dz_na
        d_blk[RET_Q], d_blk[RET_K], d_blk[RET_V], d_blk[RET_Z] = dq_f + dq_b, dk_f + dk_b, dv_f + dv_b, dz_ret
        d_blk[G_NA], d_blk[G_RET] = dg_na, dg_ret
        (du,) = s["split_vjp"](tuple(d_blk))
        (d_rpb[l],) = s["bt_vjp"](dbt)
        (d_decay[l],) = s["lam_vjp"](jnp.stack([dl_f[:, 0, 0], dl_b[:, 0, 0]]))
        tok = send_dw(l, 0, _matmul(s["h"], du, trans_a=True, out_dtype=BF16, name="in_proj_dw"))
        dh = _matmul(du, wl_in, trans_b=True, out_dtype=BF16, name="in_proj_da", after=tok)
        (dxa_norm,), d_vecs = norm_mod_bwd((s["xa"],), s["vecs"][:3], (dh,))
        dxa = dxa_res + dxa_norm
        d_mod_lat[l], d_mod_ctx[l], d_norm_g[l] = s["vecs_vjp"](tuple(d_vecs) + (d_gate,))
    gx = dxa[:n_lat]
    d_mod_lat, d_mod_ctx, d_norm_g, d_rpb, d_decay = (jnp.stack(a) for a in (d_mod_lat, d_mod_ctx, d_norm_g, d_rpb, d_decay))

    small_shapes = [d_mod_lat.shape, d_mod_ctx.shape, d_norm_g.shape, d_final_g.shape, d_rpb.shape, d_decay.shape]
    packed = _pack([d_mod_lat, d_mod_ctx, d_norm_g, d_final_g, d_rpb, d_decay])
    g_all, g_sum = _small_allgather(packed, "allgather_small_grads")
    dml_sum, dmc_sum, grad_norm_g, grad_final_g, grad_na_rpb, grad_decay = _unpack(g_sum, small_shapes)
    grad_ada_b = dml_sum + dmc_sum
    dml_all = g_all.reshape(N_DEV, -1)[:, :depth * 3 * d].reshape(N_DEV, depth, 3 * d)

    def my_cols(a):
        return lax.dynamic_slice_in_dim(a, my_idx * ncol, ncol, axis=a.ndim - 1)

    dmod = jnp.concatenate(
        [my_cols(dml_all).transpose(1, 0, 2), my_cols(dmc_sum)[:, None, :], jnp.zeros((depth, 7, ncol), F32)], axis=1)
    grad_ada_w, dcs_part = _ada_bwd_call(cin, ada_w, dmod)
    _, dcs = _small_allgather(dcs_part, "allgather_dcsilu")
    sg = jax.nn.sigmoid(c_ctx)
    grad_c_ctx = dcs[8] * (sg * (1.0 + c_ctx * (1.0 - sg)))

    def flat2(a):
        return a.reshape(a.shape[0] * a.shape[1], a.shape[2])

    small_w = [c_ctx, ada_b, norm_g, na_rpb, ret_decay_logit, final_g]
    small_g = [grad_c_ctx, grad_ada_b, grad_norm_g, grad_na_rpb, grad_decay, grad_final_g]
    small_m = [m_c_ctx, m_ada_b, m_norm_g, m_na_rpb, m_ret_decay_logit, m_final_g]
    small_v = [v_c_ctx, v_ada_b, v_norm_g, v_na_rpb, v_ret_decay_logit, v_final_g]
    shp = [a.shape for a in small_w]
    ds_, nms_, nvs_ = _adamw_dense(_pack(small_w), _pack(small_g), _pack(small_m), _pack(small_v), "adamw_small")
    ds_, nms_, nvs_ = _unpack(ds_, shp), _unpack(nms_, shp), _unpack(nvs_, shp)

    d_ada, nm_ada, nv_ada = [a.reshape(ada_w.shape) for a in _adamw_dense(
        flat2(ada_w), flat2(grad_ada_w), flat2(m_ada_w), flat2(v_ada_w), "adamw_ada_w")]

    w_all = (w_in, w_proj_na, w_proj_ret, w_out)
    m_all = (m_w_in, m_w_proj_na, m_w_proj_ret, m_w_out)
    v_all = (v_w_in, v_w_proj_na, v_w_proj_ret, v_w_out)
    upd = [None] * 4
    after = d_ada
    for flight in scatters:
        mine, slabs = _push_wait(flight["name"], flight["sems"], flight["srcs"], flight["lands"], flight["axes"],
                                 _scatter_views, PEER_BITS, (after,))
        for (l, a), own, s in zip(flight["keys"], mine, slabs):
            upd[a] = _adamw_sharded(w_all[a], m_all[a], v_all[a], own, s, w_axes[a], idx_arr, l, upd[a],
                                    "adamw_" + w_names[a])
            after = upd[a][1]
    (g_w_in, d_w_in, nm_w_in, nv_w_in), (g_pna, d_pna, nm_pna, nv_pna) = upd[0], upd[1]
    (g_pret, d_pret, nm_pret, nv_pret), (g_out, d_out, nm_out, nv_out) = upd[2], upd[3]

    def order(cc, aw, ab, ng, wi, rp, dl, pn, pr, wo, fg):
        return [cc, aw, ab, ng, wi, rp, dl, pn, pr, wo, fg]

    grads_out = order(grad_c_ctx, grad_ada_w, grad_ada_b, grad_norm_g, g_w_in, grad_na_rpb, grad_decay, g_pna, g_pret, g_out, grad_final_g)
    delta_out = order(ds_[0], d_ada, ds_[1], ds_[2], d_w_in, ds_[3], ds_[4], d_pna, d_pret, d_out, ds_[5])
    m_out = order(nms_[0], nm_ada, nms_[1], nms_[2], nm_w_in, nms_[3], nms_[4], nm_pna, nm_pret, nm_out, nms_[5])
    v_out = order(nvs_[0], nv_ada, nvs_[1], nvs_[2], nv_w_in, nvs_[3], nvs_[4], nv_pna, nv_pret, nv_out, nvs_[5])
    return (loss, gx[None], *grads_out, *delta_out, *m_out, *v_out)
```

```python
import functools

import numpy as np
import jax
import jax.numpy as jnp
from jax import lax
from jax.experimental import pallas as pl
from jax.experimental.pallas import tpu as pltpu

F32 = jnp.float32
BF16 = jnp.bfloat16

N_DEV = 8
GRID_W = 64
NA_HEAD_DIM = 128
NA_WIN_ROWS = 8
NA_WIN_COLS = 16
RET_KEY_DIM = 128
RET_VAL_DIM = 256
RET_CHUNK = 128
ROPE_BASE = 10000.0
NORM_EPS = 1e-6
MASK_VALUE = -1e30

ADAM_LR = 0.001
ADAM_B1 = 0.9
ADAM_B2 = 0.999
ADAM_EPS = 1e-08
ADAM_WD = 0.01
ADAM_STEP = 10

VMEM_LIMIT = 48 * 1024 * 1024
MESH = pl.DeviceIdType.MESH
ANY = pl.BlockSpec(memory_space=pl.ANY)
VMEM_SPEC = pl.BlockSpec(memory_space=pltpu.VMEM)


def _params(sem=None):
    return pltpu.CompilerParams(dimension_semantics=sem, vmem_limit_bytes=VMEM_LIMIT)


def _pick(n, prefs):
    for p in prefs:
        if n % p == 0:
            return p
    return n


def _dot(a, b):
    return lax.dot_general(a, b, (((1,), (0,)), ((), ())), preferred_element_type=F32)


def _dot_nt(a, b):
    return lax.dot_general(a, b, (((1,), (1,)), ((), ())), preferred_element_type=F32)


def _dot_tn(a, b):
    return lax.dot_general(a, b, (((0,), (0,)), ((), ())), preferred_element_type=F32)


def _silu(x):
    return x * jax.nn.sigmoid(x)


def _matmul(a, b, *, trans_a=False, trans_b=False, out_dtype=F32, name="matmul", after=None):
    if trans_a:
        kdim, m = a.shape
    else:
        m, kdim = a.shape
    if trans_b:
        n, kb = b.shape
    else:
        kb, n = b.shape
    assert kdim == kb, (a.shape, b.shape, trans_a, trans_b)
    tm = _pick(m, (1152, 1024, 768, 512, 256, 128))
    tn = _pick(n, (512, 256, 128))
    tk = _pick(kdim, (2304, 2048, 1024, 512, 256, 128))
    nk = kdim // tk
    dn = (((0 if trans_a else 1,), (1 if trans_b else 0,)), ((), ()))

    def body(a_ref, b_ref, *rest):
        o_ref, acc_ref = rest[-2:]
        part = lax.dot_general(a_ref[...], b_ref[...], dn, preferred_element_type=F32)
        if nk == 1:
            o_ref[...] = part.astype(o_ref.dtype)
        else:
            k = pl.program_id(2)

            @pl.when(k == 0)
            def _():
                acc_ref[...] = part

            @pl.when(k > 0)
            def _():
                acc_ref[...] += part

            @pl.when(k == nk - 1)
            def _():
                o_ref[...] = acc_ref[...].astype(o_ref.dtype)

    a_spec = pl.BlockSpec((tk, tm), lambda i, j, k: (k, i)) if trans_a else pl.BlockSpec((tm, tk), lambda i, j, k: (i, k))
    b_spec = pl.BlockSpec((tn, tk), lambda i, j, k: (j, k)) if trans_b else pl.BlockSpec((tk, tn), lambda i, j, k: (k, j))
    return pl.pallas_call(
        body,
        name=name,
        grid=(m // tm, n // tn, nk),
        in_specs=[a_spec, b_spec] + ([] if after is None else [ANY]),
        out_specs=pl.BlockSpec((tm, tn), lambda i, j, k: (i, j)),
        out_shape=jax.ShapeDtypeStruct((m, n), out_dtype),
        scratch_shapes=[pltpu.VMEM((tm, tn) if nk > 1 else (8, 128), F32)],
        compiler_params=_params(("parallel", "parallel", "arbitrary")),
    )(*((a, b) if after is None else (a, b, after)))


def _make_rowwise(f, name, out_dtypes, out_cols, n_lat, tm, diff_rows):
    def tile_fn(*args):
        return tuple(o.astype(dt) for o, dt in zip(f(*args), out_dtypes))

    def fwd_call(rows, vecs):
        t = rows[0].shape[0]
        nr, nv = len(rows), len(vecs)
        nl = n_lat // tm

        def body(*refs):
            grp = (pl.program_id(0) >= nl).astype(jnp.int32)
            args = [r[...] for r in refs[:nr]] + [v[grp] for v in refs[nr:nr + nv]]
            for o_ref, o in zip(refs[nr + nv:], tile_fn(*args)):
                o_ref[...] = o

        return pl.pallas_call(
            body,
            name=name + "_fwd",
            grid=(t // tm,),
            in_specs=[pl.BlockSpec((tm, r.shape[1]), lambda i: (i, 0)) for r in rows]
            + [pl.BlockSpec(v.shape, lambda i: (0, 0, 0)) for v in vecs],
            out_specs=[pl.BlockSpec((tm, c), lambda i: (i, 0)) for c in out_cols],
            out_shape=[jax.ShapeDtypeStruct((t, c), dt) for c, dt in zip(out_cols, out_dtypes)],
            compiler_params=_params(("parallel",)),
        )(*rows, *vecs)

    def bwd_call(rows, vecs, gs):
        t = rows[0].shape[0]
        nr, nv, ng = len(rows), len(vecs), len(gs)
        nl = n_lat // tm
        nd = len(diff_rows)

        def body(*refs):
            i = pl.program_id(0)
            grp = (i >= nl).astype(jnp.int32)
            args = [r[...] for r in refs[:nr]] + [v[grp] for v in refs[nr:nr + nv]]
            g_refs = refs[nr + nv:nr + nv + ng]
            drow_refs = refs[nr + nv + ng:nr + nv + ng + nd]
            dvec_refs = refs[nr + nv + ng + nd:]
            _, vjp = jax.vjp(tile_fn, *args)
            grads = vjp(tuple(g[...] for g in g_refs))
            for d_ref, k in zip(drow_refs, diff_rows):
                d_ref[...] = grads[k].astype(d_ref.dtype)

            @pl.when(i == 0)
            def _():
                for d_ref in dvec_refs:
                    d_ref[...] = jnp.zeros_like(d_ref)

            for j, d_ref in enumerate(dvec_refs):
                d_ref[grp] += grads[nr + j]

        outs = pl.pallas_call(
            body,
            name=name + "_bwd",
            grid=(t // tm,),
            in_specs=[pl.BlockSpec((tm, r.shape[1]), lambda i: (i, 0)) for r in rows]
            + [pl.BlockSpec(v.shape, lambda i: (0, 0, 0)) for v in vecs]
            + [pl.BlockSpec((tm, g.shape[1]), lambda i: (i, 0)) for g in gs],
            out_specs=[pl.BlockSpec((tm, rows[k].shape[1]), lambda i: (i, 0)) for k in diff_rows]
            + [pl.BlockSpec(v.shape, lambda i: (0, 0, 0)) for v in vecs],
            out_shape=[jax.ShapeDtypeStruct(rows[k].shape, rows[k].dtype) for k in diff_rows]
            + [jax.ShapeDtypeStruct(v.shape, F32) for v in vecs],
            compiler_params=_params(("arbitrary",)),
        )(*rows, *vecs, *gs)
        return outs[:nd], outs[nd:]

    return fwd_call, bwd_call


def _f_norm_mod(x, g, scale, shift):
    r = lax.rsqrt(jnp.mean(x * x, axis=-1, keepdims=True) + NORM_EPS)
    return ((x * r * g) * (1.0 + scale) + shift,)


def _f_gate_na(o, z):
    return (o.astype(F32) * _silu(z.astype(F32)),)


def _f_merge(g_na, g_ret, y_na, y_ret):
    return (jax.nn.sigmoid(g_na.astype(F32)) * y_na + jax.nn.sigmoid(g_ret.astype(F32)) * y_ret,)


def _f_residual(x, out, gate):
    return (x + gate * out,)


def _f_loss(x, target, g):
    r = lax.rsqrt(jnp.mean(x * x, axis=-1, keepdims=True) + NORM_EPS)
    y = x * r * g
    e = 0.5 * jnp.mean(jnp.square(y - target), axis=-1, keepdims=True)
    return (jnp.broadcast_to(e * (1.0 / 128.0), (x.shape[0], 128)),)


def _gate_ret_fwd_call(of, ob, z, tm):
    t, w = of.shape
    nh = w // RET_VAL_DIM

    def body(of_ref, ob_ref, z_ref, a_ref):
        for hh in range(nh):
            sl = slice(hh * RET_VAL_DIM, (hh + 1) * RET_VAL_DIM)
            o = of_ref[:, sl] + ob_ref[:, sl]
            r = lax.rsqrt(jnp.mean(o * o, axis=-1, keepdims=True) + NORM_EPS)
            a_ref[:, sl] = ((o * r) * _silu(z_ref[:, sl].astype(F32))).astype(a_ref.dtype)

    spec = pl.BlockSpec((tm, w), lambda i: (i, 0))
    return pl.pallas_call(
        body, name="gate_ret_fwd", grid=(t // tm,), in_specs=[spec, spec, spec], out_specs=spec,
        out_shape=jax.ShapeDtypeStruct((t, w), BF16), compiler_params=_params(("parallel",)),
    )(of, ob, z)


def _gate_ret_bwd_call(of, ob, z, da, tm):
    t, w = of.shape
    nh = w // RET_VAL_DIM

    def body(of_ref, ob_ref, z_ref, da_ref, do_ref, dz_ref):
        for hh in range(nh):
            sl = slice(hh * RET_VAL_DIM, (hh + 1) * RET_VAL_DIM)
            o = of_ref[:, sl] + ob_ref[:, sl]
            r = lax.rsqrt(jnp.mean(o * o, axis=-1, keepdims=True) + NORM_EPS)
            n = o * r
            zf = z_ref[:, sl].astype(F32)
            sg = jax.nn.sigmoid(zf)
            g = da_ref[:, sl].astype(F32)
            dn = g * (zf * sg)
            dz_ref[:, sl] = (g * n * (sg * (1.0 + zf * (1.0 - sg)))).astype(dz_ref.dtype)
            do_ref[:, sl] = r * (dn - n * jnp.mean(dn * n, axis=-1, keepdims=True))

    spec = pl.BlockSpec((tm, w), lambda i: (i, 0))
    return pl.pallas_call(
        body, name="gate_ret_bwd", grid=(t // tm,), in_specs=[spec, spec, spec, spec], out_specs=[spec, spec],
        out_shape=[jax.ShapeDtypeStruct((t, w), F32), jax.ShapeDtypeStruct((t, w), z.dtype)],
        compiler_params=_params(("parallel",)),
    )(of, ob, z, da)


def _na_geometry(t, n_lat):
    rows = n_lat // GRID_W
    kh = min(NA_WIN_ROWS, rows)
    return rows, kh, kh * GRID_W, t - n_lat, t // GRID_W


def _na_row0(r, rows, kh):
    return jnp.clip(r - kh // 2, 0, rows - kh)


def _na_bias_idx(r, rows, kh):
    return jnp.clip(_na_row0(r, rows, kh) - r + (NA_WIN_ROWS - 1), 0, NA_WIN_ROWS - 1)


def _na_group(rows, n_ctx):
    for g in (4, 2):
        if rows % g == 0 and (n_ctx // GRID_W) % g == 0:
            return g
    return 1


def _na_bias_spec(i, grp, rows, kh, n_loc):
    return pl.BlockSpec((1, 1, GRID_W, n_loc),
                        lambda h, g: (h, _na_bias_idx(jnp.minimum(g * grp + i, rows - grp + i), rows, kh), 0, 0))


def _na_fwd_call(q, k, v, bt, n_lat):
    t, w = q.shape
    nh = w // NA_HEAD_DIM
    rows, kh, n_loc, n_ctx, nq = _na_geometry(t, n_lat)
    grp = _na_group(rows, n_ctx)
    scale = NA_HEAD_DIM ** -0.5

    def body(q_ref, k_ref, v_ref, *rest):
        bt_refs, o_ref = rest[:grp], rest[grp]
        g = pl.program_id(1)
        kc = k_ref[pl.ds(n_lat, n_ctx), :]
        vc = v_ref[pl.ds(n_lat, n_ctx), :]

        @pl.when(g < rows // grp)
        def _():
            for i in range(grp):
                r = g * grp + i
                qb = q_ref[i * GRID_W:(i + 1) * GRID_W, :]
                s_ctx = _dot_nt(qb, kc) * scale
                start = pl.multiple_of(_na_row0(r, rows, kh) * GRID_W, GRID_W)
                kw = k_ref[pl.ds(start, n_loc), :]
                vw = v_ref[pl.ds(start, n_loc), :]
                s_loc = _dot_nt(qb, kw) * scale + bt_refs[i][0, 0]
                m = jnp.maximum(jnp.max(s_loc, axis=-1, keepdims=True), jnp.max(s_ctx, axis=-1, keepdims=True))
                p_loc = jnp.exp(s_loc - m)
                p_ctx = jnp.exp(s_ctx - m)
                l = jnp.sum(p_loc, axis=-1, keepdims=True) + jnp.sum(p_ctx, axis=-1, keepdims=True)
                o = _dot(p_loc.astype(BF16), vw) + _dot(p_ctx.astype(BF16), vc)
                o_ref[i * GRID_W:(i + 1) * GRID_W, :] = (o / l).astype(o_ref.dtype)

        @pl.when(g >= rows // grp)
        def _():
            s_ctx = _dot_nt(q_ref[...], kc) * scale
            m = jnp.max(s_ctx, axis=-1, keepdims=True)
            p = jnp.exp(s_ctx - m)
            l = jnp.sum(p, axis=-1, keepdims=True)
            o_ref[...] = (_dot(p.astype(BF16), vc) / l).astype(o_ref.dtype)

    qspec = pl.BlockSpec((grp * GRID_W, NA_HEAD_DIM), lambda h, g: (g, h))
    kspec = pl.BlockSpec((t, NA_HEAD_DIM), lambda h, g: (0, h))
    return pl.pallas_call(
        body,
        name="na_attn_fwd",
        grid=(nh, nq // grp),
        in_specs=[qspec, kspec, kspec] + [_na_bias_spec(i, grp, rows, kh, n_loc) for i in range(grp)],
        out_specs=qspec,
        out_shape=jax.ShapeDtypeStruct((t, w), BF16),
        compiler_params=_params(("parallel", "arbitrary")),
    )(q, k, v, *([bt] * grp))


def _na_bwd_call(q, k, v, bt, do, n_lat):
    t, w = q.shape
    nh = w // NA_HEAD_DIM
    rows, kh, n_loc, n_ctx, nq = _na_geometry(t, n_lat)
    scale = NA_HEAD_DIM ** -0.5

    grp = _na_group(rows, n_ctx)

    def body(q_ref, k_ref, v_ref, do_ref, *rest):
        bt_refs = rest[:grp]
        dq_ref, dk_ref, dv_ref = rest[2 * grp:2 * grp + 3]
        dbt_refs = rest[2 * grp + 3:]
        g = pl.program_id(1)

        @pl.when(g == 0)
        def _():
            dk_ref[...] = jnp.zeros_like(dk_ref)
            dv_ref[...] = jnp.zeros_like(dv_ref)

        kc = k_ref[pl.ds(n_lat, n_ctx), :]
        vc = v_ref[pl.ds(n_lat, n_ctx), :]

        @pl.when(g < rows // grp)
        def _():
            for i in range(grp):
                r = g * grp + i
                sl = slice(i * GRID_W, (i + 1) * GRID_W)
                qb = q_ref[sl, :]
                dob = do_ref[sl, :]
                s_ctx = _dot_nt(qb, kc) * scale
                dp_ctx = _dot_nt(dob, vc)
                start = pl.multiple_of(_na_row0(r, rows, kh) * GRID_W, GRID_W)
                kw = k_ref[pl.ds(start, n_loc), :]
                vw = v_ref[pl.ds(start, n_loc), :]
                s_loc = _dot_nt(qb, kw) * scale + bt_refs[i][0, 0]
                m = jnp.maximum(jnp.max(s_loc, axis=-1, keepdims=True), jnp.max(s_ctx, axis=-1, keepdims=True))
                p_loc = jnp.exp(s_loc - m)
                p_ctx = jnp.exp(s_ctx - m)
                inv = 1.0 / (jnp.sum(p_loc, axis=-1, keepdims=True) + jnp.sum(p_ctx, axis=-1, keepdims=True))
                p_loc = p_loc * inv
                p_ctx = p_ctx * inv
                dp_loc = _dot_nt(dob, vw)
                delta = (jnp.sum(p_loc * dp_loc, axis=-1, keepdims=True)
                         + jnp.sum(p_ctx * dp_ctx, axis=-1, keepdims=True))
                ds_loc = p_loc * (dp_loc - delta)
                ds_ctx = p_ctx * (dp_ctx - delta)
                first = jnp.logical_or(g == 0, _na_bias_idx(r, rows, kh) != _na_bias_idx(r - grp, rows, kh))
                dbt_ref = dbt_refs[i]

                @pl.when(first)
                def _():
                    dbt_ref[0, 0] = ds_loc

                @pl.when(jnp.logical_not(first))
                def _():
                    dbt_ref[0, 0] += ds_loc

                dsl = (ds_loc * scale).astype(BF16)
                dsc = (ds_ctx * scale).astype(BF16)
                dq_ref[sl, :] = (_dot(dsl, kw) + _dot(dsc, kc)).astype(dq_ref.dtype)
                dk_ref[pl.ds(start, n_loc), :] += _dot_tn(dsl, qb)
                dv_ref[pl.ds(start, n_loc), :] += _dot_tn(p_loc.astype(BF16), dob)
                dk_ref[pl.ds(n_lat, n_ctx), :] += _dot_tn(dsc, qb)
                dv_ref[pl.ds(n_lat, n_ctx), :] += _dot_tn(p_ctx.astype(BF16), dob)

        @pl.when(g >= rows // grp)
        def _():
            qb = q_ref[...]
            dob = do_ref[...]
            s_ctx = _dot_nt(qb, kc) * scale
            dp_ctx = _dot_nt(dob, vc)
            m = jnp.max(s_ctx, axis=-1, keepdims=True)
            p = jnp.exp(s_ctx - m)
            p = p * (1.0 / jnp.sum(p, axis=-1, keepdims=True))
            delta = jnp.sum(p * dp_ctx, axis=-1, keepdims=True)
            dsc = (p * (dp_ctx - delta) * scale).astype(BF16)
            dq_ref[...] = _dot(dsc, kc).astype(dq_ref.dtype)
            dk_ref[pl.ds(n_lat, n_ctx), :] += _dot_tn(dsc, qb)
            dv_ref[pl.ds(n_lat, n_ctx), :] += _dot_tn(p.astype(BF16), dob)

    qspec = pl.BlockSpec((grp * GRID_W, NA_HEAD_DIM), lambda h, g: (g, h))
    kspec = pl.BlockSpec((t, NA_HEAD_DIM), lambda h, g: (0, h))
    bspecs = [_na_bias_spec(i, grp, rows, kh, n_loc) for i in range(grp)]
    zeros = [jnp.zeros(bt.shape, F32) for _ in range(grp)]
    outs = pl.pallas_call(
        body,
        name="na_attn_bwd",
        grid=(nh, nq // grp),
        in_specs=[qspec, kspec, kspec, qspec] + bspecs + [ANY] * grp,
        out_specs=[qspec, kspec, kspec] + bspecs,
        out_shape=[
            jax.ShapeDtypeStruct((t, w), BF16),
            jax.ShapeDtypeStruct((t, w), F32),
            jax.ShapeDtypeStruct((t, w), F32),
        ] + [jax.ShapeDtypeStruct(bt.shape, F32)] * grp,
        input_output_aliases={4 + grp + i: 3 + i for i in range(grp)},
        compiler_params=_params(("parallel", "arbitrary")),
    )(q, k, v, do, *([bt] * grp), *zeros)
    dbt = outs[3]
    for extra in outs[4:]:
        dbt = dbt + extra
    return outs[0], outs[1], outs[2], dbt


def _na_bias_table(rpb, rows):
    kh = min(NA_WIN_ROWS, rows)
    nj = NA_WIN_ROWS
    e1 = np.zeros((nj, kh, 2 * NA_WIN_ROWS - 1), np.float32)
    for j in range(nj):
        for kk in range(kh):
            if j + kk < 2 * NA_WIN_ROWS - 1:
                e1[j, kk, j + kk] = 1.0
    cidx = np.arange(GRID_W)
    dc = np.clip(cidx[None, :] - cidx[:, None] + (NA_WIN_COLS - 1), 0, 2 * NA_WIN_COLS - 2)
    e2 = np.zeros((GRID_W, GRID_W, 2 * NA_WIN_COLS - 1), np.float32)
    e2[np.arange(GRID_W)[:, None], np.arange(GRID_W)[None, :], dc] = 1.0
    c0 = np.clip(cidx - NA_WIN_COLS // 2, 0, GRID_W - NA_WIN_COLS)
    col_in = (cidx[None, :] >= c0[:, None]) & (cidx[None, :] < c0[:, None] + NA_WIN_COLS)
    t1 = jnp.einsum("hab,jka->hjkb", rpb, jnp.asarray(e1), precision=lax.Precision.HIGHEST)
    b = jnp.einsum("hjkb,cwb->hjckw", t1, jnp.asarray(e2), precision=lax.Precision.HIGHEST)
    b = jnp.where(jnp.asarray(col_in)[None, None, :, None, :], b, MASK_VALUE)
    return b.reshape(rpb.shape[0], nj, GRID_W, kh * GRID_W)


def _ret_decays(lam_s, reverse):
    c = RET_CHUNK
    ii = lax.broadcasted_iota(jnp.int32, (c, c), 0)
    jj = lax.broadcasted_iota(jnp.int32, (c, c), 1)
    d = (jj - ii) if reverse else (ii - jj)
    dpos = jnp.maximum(d.astype(F32), 0.0)
    mask = jnp.where(d >= 0, jnp.exp(dpos * lam_s), 0.0)
    pi = lax.broadcasted_iota(jnp.int32, (c, 1), 0).astype(F32)
    qpos = (c - pi) if reverse else (pi + 1.0)
    kpos = pi if reverse else (c - 1.0 - pi)
    qd = jnp.exp(qpos * lam_s)
    kd = jnp.exp(kpos * lam_s)
    g = jnp.exp(jnp.full((1, RET_VAL_DIM), c * lam_s, F32))
    return mask, dpos, qd, kd, qpos, kpos, g


def _ret_chunk_of(t, nt, nl, reverse):
    return (nt - 1 - t) if reverse else (t + nl) % nt


def _ret_fwd_call(qr, kr, v, lam, n_lat, reverse):
    t = qr.shape[0]
    nh = qr.shape[1] // RET_KEY_DIM
    c = RET_CHUNK
    nt, nl = t // c, n_lat // c

    def body(lam_ref, q_ref, k_ref, v_ref, o_ref, s_ref, state):
        h, step = pl.program_id(0), pl.program_id(1)

        @pl.when(step == 0)
        def _():
            state[...] = jnp.zeros_like(state)

        mask, _, qd, kd, _, _, g = _ret_decays(lam_ref[h], reverse)
        q, k, vv = q_ref[...], k_ref[...], v_ref[...]
        p = _dot_nt(q, k) * mask
        s = state[...]
        qs = (q.astype(F32) * qd).astype(BF16)
        o_ref[...] = _dot(p.astype(BF16), vv) + _dot(qs, s.astype(BF16))
        s_ref[0, 0] = s
        ks = (k.astype(F32) * kd).astype(BF16)
        state[...] = s * g + _dot_tn(ks, vv)

    def cmap(h, step, lam_ref):
        return (_ret_chunk_of(step, nt, nl, reverse), h)

    return pl.pallas_call(
        body,
        name="retention_rev_fwd" if reverse else "retention_fwd",
        grid_spec=pltpu.PrefetchScalarGridSpec(
            num_scalar_prefetch=1,
            grid=(nh, nt),
            in_specs=[
                pl.BlockSpec((c, RET_KEY_DIM), cmap),
                pl.BlockSpec((c, RET_KEY_DIM), cmap),
                pl.BlockSpec((c, RET_VAL_DIM), cmap),
            ],
            out_specs=[
                pl.BlockSpec((c, RET_VAL_DIM), cmap),
                pl.BlockSpec((1, 1, RET_KEY_DIM, RET_VAL_DIM), lambda h, step, lam_ref: (h, step, 0, 0)),
            ],
            scratch_shapes=[pltpu.VMEM((RET_KEY_DIM, RET_VAL_DIM), F32)],
        ),
        out_shape=[
            jax.ShapeDtypeStruct((t, nh * RET_VAL_DIM), F32),
            jax.ShapeDtypeStruct((nh, nt, RET_KEY_DIM, RET_VAL_DIM), F32),
        ],
        compiler_params=_params(("parallel", "arbitrary")),
    )(lam, qr, kr, v)


def _ret_bwd_call(qr, kr, v, lam, states, do, n_lat, reverse):
    t = qr.shape[0]
    nh = qr.shape[1] // RET_KEY_DIM
    c = RET_CHUNK
    nt, nl = t // c, n_lat // c

    def body(lam_ref, q_ref, k_ref, v_ref, s_ref, do_ref, dq_ref, dk_ref, dv_ref, dl_ref, dstate):
        h, rstep = pl.program_id(0), pl.program_id(1)

        @pl.when(rstep == 0)
        def _():
            dstate[...] = jnp.zeros_like(dstate)
            dl_ref[...] = jnp.zeros_like(dl_ref)

        mask, dpos, qd, kd, qpos, kpos, g = _ret_decays(lam_ref[h], reverse)
        q, k, vv = q_ref[...], k_ref[...], v_ref[...]
        qf, kf = q.astype(F32), k.astype(F32)
        s = s_ref[0, 0]
        ds = dstate[...]
        dob = do_ref[...].astype(BF16)
        sb, dsb = s.astype(BF16), ds.astype(BF16)
        a = _dot_nt(q, k)
        p = a * mask
        dp = _dot_nt(dob, vv)
        da = dp * mask
        dab = da.astype(BF16)
        dqc = _dot_nt(dob, sb)
        dkc = _dot_nt(vv, dsb)
        qs = (qf * qd).astype(BF16)
        ks = (kf * kd).astype(BF16)
        dq_ref[...] = (_dot(dab, k) + dqc * qd).astype(dq_ref.dtype)
        dk_ref[...] = (_dot_tn(dab, q) + dkc * kd).astype(dk_ref.dtype)
        dv_ref[...] = (_dot_tn(p.astype(BF16), dob) + _dot(ks, dsb)).astype(dv_ref.dtype)
        terms = (
            jnp.sum(jnp.sum(da * a * dpos, axis=1, keepdims=True), axis=0, keepdims=True)
            + jnp.sum(jnp.sum(dqc * qf * (qd * qpos), axis=1, keepdims=True), axis=0, keepdims=True)
            + jnp.sum(jnp.sum(dkc * kf * (kd * kpos), axis=1, keepdims=True), axis=0, keepdims=True)
            + jnp.sum(jnp.sum(ds * s * (g * c), axis=1, keepdims=True), axis=0, keepdims=True)
        )
        dl_ref[0] += jnp.broadcast_to(terms, (8, 128))
        dstate[...] = ds * g + _dot_tn(qs, dob)

    def cmap(h, rstep, lam_ref):
        return (_ret_chunk_of(nt - 1 - rstep, nt, nl, reverse), h)

    return pl.pallas_call(
        body,
        name="retention_rev_bwd" if reverse else "retention_bwd",
        grid_spec=pltpu.PrefetchScalarGridSpec(
            num_scalar_prefetch=1,
            grid=(nh, nt),
            in_specs=[
                pl.BlockSpec((c, RET_KEY_DIM), cmap),
                pl.BlockSpec((c, RET_KEY_DIM), cmap),
                pl.BlockSpec((c, RET_VAL_DIM), cmap),
                pl.BlockSpec((1, 1, RET_KEY_DIM, RET_VAL_DIM), lambda h, rstep, lam_ref: (h, nt - 1 - rstep, 0, 0)),
                pl.BlockSpec((c, RET_VAL_DIM), cmap),
            ],
            out_specs=[
                pl.BlockSpec((c, RET_KEY_DIM), cmap),
                pl.BlockSpec((c, RET_KEY_DIM), cmap),
                pl.BlockSpec((c, RET_VAL_DIM), cmap),
                pl.BlockSpec((1, 8, 128), lambda h, rstep, lam_ref: (h, 0, 0)),
            ],
            scratch_shapes=[pltpu.VMEM((RET_KEY_DIM, RET_VAL_DIM), F32)],
        ),
        out_shape=[
            jax.ShapeDtypeStruct(qr.shape, qr.dtype),
            jax.ShapeDtypeStruct(kr.shape, kr.dtype),
            jax.ShapeDtypeStruct(v.shape, v.dtype),
            jax.ShapeDtypeStruct((nh, 8, 128), F32),
        ],
        compiler_params=_params(("parallel", "arbitrary")),
    )(lam, qr, kr, v, states, do)


def _rope_tables(t, n_lat):
    nf = RET_KEY_DIM // 4
    tok = np.arange(n_lat)
    inv_freq = (ROPE_BASE ** (-np.arange(nf, dtype=np.float32) / nf)).astype(np.float32)
    row = (tok // GRID_W).astype(np.float32)
    col = (tok % GRID_W).astype(np.float32)
    ang = np.concatenate([row[:, None] * inv_freq, col[:, None] * inv_freq], axis=-1).astype(np.float32)
    cos = np.ones((t, 2 * nf), np.float32)
    sin = np.zeros((t, 2 * nf), np.float32)
    cos[:n_lat] = np.cos(ang)
    sin[:n_lat] = np.sin(ang)
    return jnp.asarray(cos), jnp.asarray(sin)


def _rope(xb, cos, sin, mult):
    t, w = xb.shape
    nh = w // RET_KEY_DIM
    half = RET_KEY_DIM // 2
    x = xb.astype(F32).reshape(t, nh, 2, half)
    x1, x2 = x[:, :, 0], x[:, :, 1]
    c, s = cos[:, None, :], sin[:, None, :]
    out = jnp.stack([x1 * c - x2 * s, x2 * c + x1 * s], axis=2) * mult
    return out.reshape(t, w).astype(BF16)


def _my_position():
    return lax.axis_index("x"), lax.axis_index("y"), lax.axis_index("c")


def _flip(pos, k):
    x, y, c = pos
    return (1 - x if k & 4 else x, 1 - y if k & 2 else y, 1 - c if k & 1 else c)


def _linear(pos):
    return 4 * pos[0] + 2 * pos[1] + pos[2]


def _slab(ref, axis, idx, size):
    start = pl.multiple_of(idx * size, size)
    return ref.at[pl.ds(start, size), :] if axis == 0 else ref.at[:, pl.ds(start, size)]


HBM_SPEC = pl.BlockSpec(memory_space=pltpu.HBM)
SEM_SPEC = pl.BlockSpec(memory_space=pltpu.SEMAPHORE)
DATAFLOW = pltpu.SideEffectType.DATAFLOW_SIDE_EFFECTING
PEER_BITS = (1, 2, 4, 6, 3, 5, 7)
GATHER_BITS = (1, 2, 4, 6)


def _in_hbm(a):
    return pltpu.with_memory_space_constraint(a, pltpu.HBM)


def _gather_views(me, k, a, src_refs, land_refs, axes):
    size = src_refs[a].shape[axes[a]]
    peer = _flip(me, k)
    return src_refs[a], _slab(land_refs[a], axes[a], _linear(me), size), _slab(land_refs[a], axes[a], _linear(peer), size)


def _scatter_views(me, k, a, src_refs, land_refs, axes):
    size = land_refs[a].shape[1 + axes[a]]
    peer = _flip(me, k)
    return _slab(src_refs[a], axes[a], _linear(peer), size), land_refs[a].at[k - 1], land_refs[a].at[k - 1]


def _slab_block(rows, cols, tm, axis):
    if axis == 0:
        return pl.BlockSpec((tm, cols), lambda i, idx: (idx[0] * (rows // tm) + i, 0))
    return pl.BlockSpec((tm, cols), lambda i, idx: (i, idx[0]))


def _place_shard(shard, land, axis, my_idx):
    r, c = shard.shape
    tm = _pick(r, (512, 256, 128, 64, 32, 16))

    def body(idx_ref, s_ref, land_ref, o_ref):
        o_ref[...] = s_ref[...]

    return pl.pallas_call(
        body, name="gather_place",
        grid_spec=pltpu.PrefetchScalarGridSpec(
            num_scalar_prefetch=1, grid=(r // tm,),
            in_specs=[pl.BlockSpec((tm, c), lambda i, idx: (i, 0)), ANY],
            out_specs=_slab_block(r, c, tm, axis)),
        out_shape=jax.ShapeDtypeStruct(land.shape, land.dtype),
        input_output_aliases={2: 0},
        compiler_params=_params(("parallel",)),
    )(my_idx, shard, land)


def _push_start(name, srcs, lands, axes, views, bits, deps):
    ns = len(srcs)

    def body(*refs):
        src_refs, land_refs = refs[:ns], refs[ns:2 * ns]
        send_sems, recv_sems = refs[2 * ns + len(deps):2 * ns + len(deps) + 2]
        token = refs[-1]
        me = _my_position()
        for k in bits:
            for a in range(ns):
                s, d, _ = views(me, k, a, src_refs, land_refs, axes)
                pltpu.make_async_remote_copy(
                    src_ref=s, dst_ref=d, send_sem=send_sems.at[7 * a + k - 1], recv_sem=recv_sems.at[7 * a + k - 1],
                    device_id=_flip(me, k), device_id_type=MESH).start()
        token[...] = jnp.zeros_like(token)

    thru = [pltpu.HBM(a.shape, a.dtype) for a in list(srcs) + list(lands)]
    outs = pl.pallas_call(
        body, name=name,
        in_specs=[HBM_SPEC] * (2 * ns) + [ANY] * len(deps),
        out_specs=[SEM_SPEC, SEM_SPEC] + [HBM_SPEC] * (2 * ns) + [VMEM_SPEC],
        out_shape=[pltpu.SemaphoreType.DMA((7 * ns,)), pltpu.SemaphoreType.DMA((7 * ns,))] + thru
        + [jax.ShapeDtypeStruct((8, 128), F32)],
        input_output_aliases={i: 2 + i for i in range(2 * ns)},
        compiler_params=pltpu.CompilerParams(has_side_effects=DATAFLOW),
    )(*[_in_hbm(a) for a in srcs], *[_in_hbm(a) for a in lands], *deps)
    return (outs[0], outs[1]), outs[2:2 + ns], outs[2 + ns:2 + 2 * ns], outs[-1]


def _gather_finish(lands, axes, sizes):
    ns = len(lands)
    chips = (2, 4, 6)

    def body(*refs):
        land_refs = refs[ns:2 * ns]
        send_sems, recv_sems = refs[2 * ns:]
        me = _my_position()
        sibling = _flip(me, 1)
        copies = []
        for j, kc in enumerate(chips):
            for a in range(ns):
                def slab_of(pos):
                    return _slab(land_refs[a], axes[a], _linear(pos), sizes[a])
                send = pltpu.make_async_remote_copy(
                    src_ref=slab_of(_flip(me, kc)), dst_ref=slab_of(_flip(me, kc)), send_sem=send_sems.at[3 * a + j],
                    recv_sem=recv_sems.at[3 * a + j], device_id=sibling, device_id_type=MESH)
                recv = pltpu.make_async_remote_copy(
                    src_ref=slab_of(_flip(me, kc)), dst_ref=slab_of(_flip(sibling, kc)), send_sem=send_sems.at[3 * a + j],
                    recv_sem=recv_sems.at[3 * a + j], device_id=sibling, device_id_type=MESH)
                send.start()
                copies.append((send, recv))
        for send, recv in copies:
            recv.wait_recv()
        for send, recv in copies:
            send.wait_send()

    return pl.pallas_call(
        body, name="gather_finish", in_specs=[ANY] * ns, out_specs=[ANY] * ns,
        out_shape=[jax.ShapeDtypeStruct(l.shape, l.dtype) for l in lands],
        input_output_aliases={a: a for a in range(ns)},
        scratch_shapes=[pltpu.SemaphoreType.DMA((3 * ns,)), pltpu.SemaphoreType.DMA((3 * ns,))],
        compiler_params=pltpu.CompilerParams(has_side_effects=True),
    )(*lands)


def _push_wait(name, sems, srcs, lands, axes, views, bits, after):
    ns = len(srcs)

    def body(*refs):
        src_refs, land_refs = refs[:ns], refs[ns:2 * ns]
        send_sems, recv_sems = refs[2 * ns:2 * ns + 2]
        me = _my_position()
        for k in bits:
            for a in range(ns):
                s, d, got = views(me, k, a, src_refs, land_refs, axes)
                cp = pltpu.make_async_remote_copy(
                    src_ref=s, dst_ref=got, send_sem=send_sems.at[7 * a + k - 1], recv_sem=recv_sems.at[7 * a + k - 1],
                    device_id=_flip(me, k), device_id_type=MESH)
                cp.wait_send()
                cp.wait_recv()

    thru = [pltpu.HBM(a.shape, a.dtype) for a in list(srcs) + list(lands)]
    outs = pl.pallas_call(
        body, name=name,
        in_specs=[HBM_SPEC] * (2 * ns) + [SEM_SPEC, SEM_SPEC] + [ANY] * len(after),
        out_specs=[HBM_SPEC] * (2 * ns),
        out_shape=thru,
        input_output_aliases={i: i for i in range(2 * ns)},
        compiler_params=pltpu.CompilerParams(has_side_effects=DATAFLOW),
    )(*srcs, *lands, sems[0], sems[1], *after)
    return outs[:ns], outs[ns:]


def _small_allgather(v, name):
    r, c = v.shape

    def body(v_ref, all_ref, sum_ref, send_sems, recv_sems):
        me = _my_position()
        all_ref[_linear(me)] = v_ref[...]
        copies = []
        for k in range(1, N_DEV):
            peer = _flip(me, k)
            copies.append(pltpu.make_async_remote_copy(
                src_ref=v_ref, dst_ref=all_ref.at[_linear(me)], send_sem=send_sems.at[k - 1], recv_sem=recv_sems.at[k - 1],
                device_id=peer, device_id_type=MESH))
        for cp in copies:
            cp.start()
        for k in range(1, N_DEV):
            peer = _flip(me, k)
            pltpu.make_async_remote_copy(
                src_ref=v_ref, dst_ref=all_ref.at[_linear(peer)], send_sem=send_sems.at[k - 1], recv_sem=recv_sems.at[k - 1],
                device_id=peer, device_id_type=MESH).wait_recv()
        for cp in copies:
            cp.wait_send()
        acc = all_ref[0]
        for d in range(1, N_DEV):
            acc = acc + all_ref[d]
        sum_ref[...] = acc

    return pl.pallas_call(
        body,
        name=name,
        in_specs=[VMEM_SPEC],
        out_specs=[VMEM_SPEC, VMEM_SPEC],
        out_shape=[jax.ShapeDtypeStruct((N_DEV, r, c), F32), jax.ShapeDtypeStruct((r, c), F32)],
        scratch_shapes=[pltpu.SemaphoreType.DMA((N_DEV - 1,)), pltpu.SemaphoreType.DMA((N_DEV - 1,))],
        compiler_params=pltpu.CompilerParams(has_side_effects=True, vmem_limit_bytes=VMEM_LIMIT),
    )(v)


def _ada_fwd_call(cin, ada_w, ada_b_cols):
    nl, d, ncol = ada_w.shape
    nrow = cin.shape[0]

    def body(c_ref, w_ref, b_ref, o_ref):
        cs = _silu(c_ref[...]).astype(BF16)
        for l in range(nl):
            o_ref[l] = _dot(cs, w_ref[l].astype(BF16)) + b_ref[l]

    return pl.pallas_call(
        body, name="ada_fwd", in_specs=[VMEM_SPEC] * 3, out_specs=VMEM_SPEC,
        out_shape=jax.ShapeDtypeStruct((nl, nrow, ncol), F32), compiler_params=_params(),
    )(cin, ada_w, ada_b_cols)


def _ada_bwd_call(cin, ada_w, dmod):
    nl, d, ncol = ada_w.shape
    nrow = cin.shape[0]

    def body(c_ref, w_ref, dm_ref, gw_ref, dcs_ref):
        cs = _silu(c_ref[...]).astype(BF16)
        acc = jnp.zeros((nrow, d), F32)
        for l in range(nl):
            dm = dm_ref[l].astype(BF16)
            gw_ref[l] = _dot_tn(cs, dm)
            acc = acc + _dot_nt(dm, w_ref[l].astype(BF16))
        dcs_ref[...] = acc

    return pl.pallas_call(
        body, name="ada_bwd", in_specs=[VMEM_SPEC] * 3, out_specs=[VMEM_SPEC, VMEM_SPEC],
        out_shape=[jax.ShapeDtypeStruct((nl, d, ncol), F32), jax.ShapeDtypeStruct((nrow, d), F32)],
        compiler_params=_params(),
    )(cin, ada_w, dmod)


def _adamw_math(w, g, m, v):
    m = ADAM_B1 * m + (1.0 - ADAM_B1) * g
    v = ADAM_B2 * v + (1.0 - ADAM_B2) * jnp.square(g)
    m_hat = m / (1.0 - ADAM_B1 ** ADAM_STEP)
    v_hat = v / (1.0 - ADAM_B2 ** ADAM_STEP)
    delta = -ADAM_LR * (m_hat / (jnp.sqrt(v_hat) + ADAM_EPS) + ADAM_WD * w)
    return delta, m, v


def _adamw_sharded(w, m, v, mine, slabs, axis, my_idx, layer, prev, name):
    nl, r, c = w.shape
    tm = _pick(r, (128, 64, 32, 16))
    nprev = 0 if prev is None else len(prev)

    def body(idx_ref, w_ref, m_ref, v_ref, mine_ref, s_ref, *rest):
        g_ref, d_ref, nm_ref, nv_ref = rest[nprev:]
        g = mine_ref[...].astype(F32)
        for k in range(N_DEV - 1):
            g = g + s_ref[k].astype(F32)
        delta, nm, nv = _adamw_math(w_ref[0], g, m_ref[0], v_ref[0])
        g_ref[0], d_ref[0], nm_ref[0], nv_ref[0] = g, delta, nm, nv

    spec = pl.BlockSpec((1, tm, c), lambda i, idx: (layer, i, 0))
    out = jax.ShapeDtypeStruct(w.shape, F32)
    return pl.pallas_call(
        body, name=name,
        grid_spec=pltpu.PrefetchScalarGridSpec(
            num_scalar_prefetch=1, grid=(r // tm,),
            in_specs=[spec, spec, spec, _slab_block(r, c, tm, axis),
                      pl.BlockSpec((N_DEV - 1, tm, c), lambda i, idx: (0, i, 0))] + [ANY] * nprev,
            out_specs=[spec] * 4),
        out_shape=[out] * 4,
        input_output_aliases={6 + j: j for j in range(nprev)},
        compiler_params=_params(("parallel",)),
    )(my_idx, w, m, v, mine, slabs, *(() if prev is None else prev))


def _adamw_dense(w, g, m, v, name):
    r, c = w.shape
    tm = _pick(r, (256, 128, 64, 32, 16, 8))

    def body(w_ref, g_ref, m_ref, v_ref, d_ref, nm_ref, nv_ref):
        d_ref[...], nm_ref[...], nv_ref[...] = _adamw_math(w_ref[...], g_ref[...], m_ref[...], v_ref[...])

    spec = pl.BlockSpec((tm, c), lambda i: (i, 0))
    out = jax.ShapeDtypeStruct(w.shape, F32)
    return pl.pallas_call(
        body, name=name, grid=(r // tm,), in_specs=[spec] * 4, out_specs=[spec] * 3, out_shape=[out] * 3,
        compiler_params=_params(("parallel",)),
    )(w, g, m, v)


def _pack(parts, width=128):
    flat = jnp.concatenate([p.reshape(-1).astype(F32) for p in parts])
    n = flat.shape[0]
    total = -(-n // (8 * width)) * (8 * width)
    return jnp.pad(flat, (0, total - n)).reshape(total // width, width)


def _unpack(buf, shapes):
    flat = buf.reshape(-1)
    out, off = [], 0
    for s in shapes:
        n = int(np.prod(s))
        out.append(flat[off:off + n].reshape(s))
        off += n
    return out


def kernel(x, c, ctx, c_ctx, ada_w, ada_b, norm_g, w_in, na_rpb, ret_decay_logit, w_proj_na, w_proj_ret, w_out, final_g, loss_target, m_c_ctx, m_ada_w, m_ada_b, m_norm_g, m_w_in, m_na_rpb, m_ret_decay_logit, m_w_proj_na, m_w_proj_ret, m_w_out, m_final_g, v_c_ctx, v_ada_w, v_ada_b, v_norm_g, v_w_in, v_na_rpb, v_ret_decay_logit, v_w_proj_na, v_w_proj_ret, v_w_out, v_final_g):
    depth = w_in.shape[0]
    n_lat, d = x.shape[1], x.shape[2]
    n_ctx = ctx.shape[1]
    t = n_lat + n_ctx
    w_na = w_proj_na.shape[1]
    w_retv = w_proj_ret.shape[1] * N_DEV
    in_cols = w_in.shape[2] * N_DEV
    w_qk = (in_cols - 4 * w_na - 2 * w_retv - 2 * d) // 2
    sizes = (w_na, w_na, w_na, w_na, w_qk, w_qk, w_retv, w_retv, d, d)
    off = tuple(int(o) for o in np.cumsum((0,) + sizes))
    NA_Q, NA_K, NA_V, NA_Z, RET_Q, RET_K, RET_V, RET_Z, G_NA, G_RET = range(10)
    rows = n_lat // GRID_W
    me = _my_position()
    my_idx = _linear(me)
    tm_row = _pick(n_ctx, (256, 128))

    idx_arr = jnp.reshape(my_idx, (1,)).astype(jnp.int32)

    ncol = ada_w.shape[2]
    c_all, _ = _small_allgather(jnp.pad(c, ((0, 7), (0, 0))), "allgather_c")
    cin = jnp.concatenate([c_all[:, 0, :], c_ctx[None, :], jnp.zeros((7, d), F32)], axis=0)
    ada_b_cols = lax.dynamic_slice_in_dim(ada_b, my_idx * ncol, ncol, axis=1)[:, None, :]
    mod_cols = _ada_fwd_call(cin, ada_w, ada_b_cols)
    mod_gathered, _ = _small_allgather(mod_cols.reshape(depth * 16, ncol), "allgather_mod")
    mod_all = mod_gathered.reshape(N_DEV, depth, 16, ncol).transpose(1, 2, 0, 3).reshape(depth, 16, N_DEV * ncol)
    mod_lat = lax.dynamic_index_in_dim(mod_all, my_idx, axis=1, keepdims=False)
    mod_ctx = mod_all[:, 8, :]

    w_axes = (1, 1, 0, 0)
    w_names = ("w_in", "w_proj_na", "w_proj_ret", "w_out")
    shard = [[w[l].astype(BF16) for w in (w_in, w_proj_na, w_proj_ret, w_out)] for l in range(depth)]
    groups = [[(0, 0)], [(0, 1), (0, 2), (0, 3)]] + [[(l, a) for a in range(4)] for l in range(1, depth)]
    gathers, token = {}, mod_gathered
    for gi, keys in enumerate(groups):
        srcs = [shard[l][a] for l, a in keys]
        axes = tuple(w_axes[a] for _, a in keys)
        lands = [_place_shard(s, lax.empty(tuple(n * (N_DEV if i == ax else 1) for i, n in enumerate(s.shape)), BF16),
                              ax, idx_arr) for s, ax in zip(srcs, axes)]
        sizes = tuple(s.shape[ax] for s, ax in zip(srcs, axes))
        sems, srcs, lands, token = _push_start(
            f"gather_start_{gi}", srcs, lands, axes, _gather_views, GATHER_BITS, (token,))
        flight = dict(name=f"gather_wait_{gi}", sems=sems, srcs=srcs, lands=lands, axes=axes, sizes=sizes, ready=None)
        for pos, key in enumerate(keys):
            gathers[key] = (flight, pos)

    def landed(l, a, act):
        flight, pos = gathers[(l, a)]
        if flight["ready"] is None:
            arrived = _push_wait(flight["name"], flight["sems"], flight["srcs"], flight["lands"],
                                 flight["axes"], _gather_views, GATHER_BITS, (act, token))[1]
            flight["ready"] = _gather_finish(arrived, flight["axes"], flight["sizes"])
        return flight["ready"][pos]

    pending, scatters = {}, []

    def send_dw(l, a, dw):
        pending[(l, a)] = dw
        if a == 0:
            keys = [(0, 0)] if l == 0 else [(l, b) for b in range(4)]
        elif l == 0 and a == 1:
            keys = [(0, 1), (0, 2), (0, 3)]
        else:
            return None
        srcs = [pending[k] for k in keys]
        axes = tuple(w_axes[b] for _, b in keys)
        lands = [lax.empty((N_DEV - 1,) + tuple(n // (N_DEV if i == ax else 1) for i, n in enumerate(s.shape)), BF16)
                 for s, ax in zip(srcs, axes)]
        sems, srcs, lands, tok = _push_start(
            f"scatter_start_{len(scatters)}", srcs, lands, axes, _scatter_views, PEER_BITS, ())
        scatters.append(dict(name=f"scatter_wait_{len(scatters)}", sems=sems, srcs=srcs, lands=lands, axes=axes, keys=keys))
        return tok

    cos, sin = _rope_tables(t, n_lat)
    k_scale = RET_KEY_DIM ** -0.5
    norm_mod_fwd, norm_mod_bwd = _make_rowwise(_f_norm_mod, "norm_mod", (BF16,), (d,), n_lat, tm_row, (0,))
    gate_na_fwd, gate_na_bwd = _make_rowwise(_f_gate_na, "gate_na", (BF16,), (w_na,), n_lat, tm_row, (0, 1))
    merge_fwd, merge_bwd = _make_rowwise(_f_merge, "merge", (BF16,), (d,), n_lat, tm_row, (0, 1, 2, 3))
    residual_fwd, residual_bwd = _make_rowwise(_f_residual, "residual", (F32,), (d,), n_lat, tm_row, (0, 1))
    loss_fwd, loss_bwd = _make_rowwise(_f_loss, "loss_head", (F32,), (128,), n_lat, tm_row, (0,))

    def pair(a, b):
        return jnp.stack([a, b])[:, None, :]

    def mod_vectors(mod_lat_l, mod_ctx_l, norm_g_l):
        shift, scale, gate = jnp.split(mod_lat_l, 3)
        c_shift, c_scale, c_gate = jnp.split(mod_ctx_l, 3)
        return pair(norm_g_l, norm_g_l), pair(scale, c_scale), pair(shift, c_shift), pair(gate, c_gate)

    def split_u(u):
        blk = [u[:, off[i]:off[i + 1]] for i in range(10)]
        blk[RET_Q] = _rope(blk[RET_Q], cos, sin, 1.0)
        blk[RET_K] = _rope(blk[RET_K], cos, sin, k_scale)
        return tuple(blk)

    def log_decay(logit):
        return jax.nn.log_sigmoid(logit.astype(F32))

    xa = jnp.concatenate([x[0], ctx[0]], axis=0)
    saved = []
    for l in range(depth):
        vecs, vecs_vjp = jax.vjp(mod_vectors, mod_lat[l], mod_ctx[l], norm_g[l])
        (h,) = norm_mod_fwd((xa,), vecs[:3])
        wl_in = landed(l, 0, h)
        u = _matmul(h, wl_in, out_dtype=BF16, name="in_proj_fwd")
        blk, split_vjp = jax.vjp(split_u, u)
        bt, bt_vjp = jax.vjp(lambda r: _na_bias_table(r, rows), na_rpb[l])
        lam, lam_vjp = jax.vjp(log_decay, ret_decay_logit[l])
        o_na = _na_fwd_call(blk[NA_Q], blk[NA_K], blk[NA_V], bt, n_lat)
        o_f, st_f = _ret_fwd_call(blk[RET_Q], blk[RET_K], blk[RET_V], lam[0], n_lat, False)
        o_b, st_b = _ret_fwd_call(blk[RET_Q], blk[RET_K], blk[RET_V], lam[1], n_lat, True)
        (a_na,) = gate_na_fwd((o_na, blk[NA_Z]), ())
        a_ret = _gate_ret_fwd_call(o_f, o_b, blk[RET_Z], tm_row)
        wl_pna, wl_pret, wl_out = landed(l, 1, a_na), landed(l, 2, a_na), landed(l, 3, a_na)
        y_na = _matmul(a_na, wl_pna, out_dtype=F32, name="proj_na_fwd")
        y_ret = _matmul(a_ret, wl_pret, out_dtype=F32, name="proj_ret_fwd")
        (merged,) = merge_fwd((blk[G_NA], blk[G_RET], y_na, y_ret), ())
        out = _matmul(merged, wl_out, out_dtype=F32, name="out_proj_fwd")
        (xa_next,) = residual_fwd((xa, out), vecs[3:])
        saved.append(dict(xa=xa, vecs=vecs, vecs_vjp=vecs_vjp, h=h, w=(wl_in, wl_pna, wl_pret, wl_out), blk=blk,
                          split_vjp=split_vjp, bt=bt, bt_vjp=bt_vjp, lam=lam, lam_vjp=lam_vjp, o_na=o_na, o_f=o_f,
                          o_b=o_b, st_f=st_f, st_b=st_b, a_na=a_na, a_ret=a_ret, y_na=y_na, y_ret=y_ret,
                          merged=merged, out=out))
        xa = xa_next

    fg_pair, fg_vjp = jax.vjp(lambda g: pair(g, g), final_g)
    x_last = xa[:n_lat]
    (loss_rows,) = loss_fwd((x_last, loss_target[0]), (fg_pair,))
    loss = lax.psum(jnp.sum(loss_rows), ("x", "y", "c"))
    (dx_last,), (d_fg_pair,) = loss_bwd((x_last, loss_target[0]), (fg_pair,), (jnp.ones_like(loss_rows),))
    (d_final_g,) = fg_vjp(d_fg_pair)
    dxa = jnp.pad(dx_last, ((0, n_ctx), (0, 0)))

    d_mod_lat, d_mod_ctx, d_norm_g, d_rpb, d_decay = ([None] * depth for _ in range(5))
    for l in reversed(range(depth)):
        s = saved[l]
        blk = s["blk"]
        wl_in, wl_pna, wl_pret, wl_out = s["w"]
        (dxa_res, d_out), (d_gate,) = residual_bwd((s["xa"], s["out"]), s["vecs"][3:], (dxa,))
        d_out = d_out.astype(BF16)
        send_dw(l, 3, _matmul(s["merged"], d_out, trans_a=True, out_dtype=BF16, name="out_proj_dw"))
        d_merged = _matmul(d_out, wl_out, trans_b=True, out_dtype=BF16, name="out_proj_da")
        (dg_na, dg_ret, dy_na, dy_ret), _ = merge_bwd((blk[G_NA], blk[G_RET], s["y_na"], s["y_ret"]), (), (d_merged,))
        dy_na, dy_ret = dy_na.astype(BF16), dy_ret.astype(BF16)
        send_dw(l, 2, _matmul(s["a_ret"], dy_ret, trans_a=True, out_dtype=BF16, name="proj_ret_dw"))
        da_ret = _matmul(dy_ret, wl_pret, trans_b=True, out_dtype=BF16, name="proj_ret_da")
        tok = send_dw(l, 1, _matmul(s["a_na"], dy_na, trans_a=True, out_dtype=BF16, name="proj_na_dw"))
        da_na = _matmul(dy_na, wl_pna, trans_b=True, out_dtype=BF16, name="proj_na_da", after=tok)
        do_ret, dz_ret = _gate_ret_bwd_call(s["o_f"], s["o_b"], blk[RET_Z], da_ret, tm_row)
        (do_na, dz_na), _ = gate_na_bwd((s["o_na"], blk[NA_Z]), (), (da_na,))
        dq_f, dk_f, dv_f, dl_f = _ret_bwd_call(blk[RET_Q], blk[RET_K], blk[RET_V], s["lam"][0], s["st_f"], do_ret, n_lat, False)
        dq_b, dk_b, dv_b, dl_b = _ret_bwd_call(blk[RET_Q], blk[RET_K], blk[RET_V], s["lam"][1], s["st_b"], do_ret, n_lat, True)
        dq, dk, dv, dbt = _na_bwd_call(blk[NA_Q], blk[NA_K], blk[NA_V], s["bt"], do_na, n_lat)
        d_blk = [None] * 10
        d_blk[NA_Q], d_blk[NA_K], d_blk[NA_V], d_blk[NA_Z] = dq, dk.astype(BF16), dv.astype(BF16), dz_na
        d_blk[RET_Q], d_blk[RET_K], d_blk[RET_V], d_blk[RET_Z] = dq_f + dq_b, dk_f + dk_b, dv_f + dv_b, dz_ret
        d_blk[G_NA], d_blk[G_RET] = dg_na, dg_ret
        (du,) = s["split_vjp"](tuple(d_blk))
        (d_rpb[l],) = s["bt_vjp"](dbt)
        (d_decay[l],) = s["lam_vjp"](jnp.stack([dl_f[:, 0, 0], dl_b[:, 0, 0]]))
        tok = send_dw(l, 0, _matmul(s["h"], du, trans_a=True, out_dtype=BF16, name="in_proj_dw"))
        dh = _matmul(du, wl_in, trans_b=True, out_dtype=BF16, name="in_proj_da", after=tok)
        (dxa_norm,), d_vecs = norm_mod_bwd((s["xa"],), s["vecs"][:3], (dh,))
        dxa = dxa_res + dxa_norm
        d_mod_lat[l], d_mod_ctx[l], d_norm_g[l] = s["vecs_vjp"](tuple(d_vecs) + (d_gate,))
    gx = dxa[:n_lat]
    d_mod_lat, d_mod_ctx, d_norm_g, d_rpb, d_decay = (jnp.stack(a) for a in (d_mod_lat, d_mod_ctx, d_norm_g, d_rpb, d_decay))

    small_shapes = [d_mod_lat.shape, d_mod_ctx.shape, d_norm_g.shape, d_final_g.shape, d_rpb.shape, d_decay.shape]
    packed = _pack([d_mod_lat, d_mod_ctx, d_norm_g, d_final_g, d_rpb, d_decay])
    g_all, g_sum = _small_allgather(packed, "allgather_small_grads")
    dml_sum, dmc_sum, grad_norm_g, grad_final_g, grad_na_rpb, grad_decay = _unpack(g_sum, small_shapes)
    grad_ada_b = dml_sum + dmc_sum
    dml_all = g_all.reshape(N_DEV, -1)[:, :depth * 3 * d].reshape(N_DEV, depth, 3 * d)

    def my_cols(a):
        return lax.dynamic_slice_in_dim(a, my_idx * ncol, ncol, axis=a.ndim - 1)

    dmod = jnp.concatenate(
        [my_cols(dml_all).transpose(1, 0, 2), my_cols(dmc_sum)[:, None, :], jnp.zeros((depth, 7, ncol), F32)], axis=1)
    grad_ada_w, dcs_part = _ada_bwd_call(cin, ada_w, dmod)
    _, dcs = _small_allgather(dcs_part, "allgather_dcsilu")
    sg = jax.nn.sigmoid(c_ctx)
    grad_c_ctx = dcs[8] * (sg * (1.0 + c_ctx * (1.0 - sg)))

    def flat2(a):
        return a.reshape(a.shape[0] * a.shape[1], a.shape[2])

    small_w = [c_ctx, ada_b, norm_g, na_rpb, ret_decay_logit, final_g]
    small_g = [grad_c_ctx, grad_ada_b, grad_norm_g, grad_na_rpb, grad_decay, grad_final_g]
    small_m = [m_c_ctx, m_ada_b, m_norm_g, m_na_rpb, m_ret_decay_logit, m_final_g]
    small_v = [v_c_ctx, v_ada_b, v_norm_g, v_na_rpb, v_ret_decay_logit, v_final_g]
    shp = [a.shape for a in small_w]
    ds_, nms_, nvs_ = _adamw_dense(_pack(small_w), _pack(small_g), _pack(small_m), _pack(small_v), "adamw_small")
    ds_, nms_, nvs_ = _unpack(ds_, shp), _unpack(nms_, shp), _unpack(nvs_, shp)

    d_ada, nm_ada, nv_ada = [a.reshape(ada_w.shape) for a in _adamw_dense(
        flat2(ada_w), flat2(grad_ada_w), flat2(m_ada_w), flat2(v_ada_w), "adamw_ada_w")]

    w_all = (w_in, w_proj_na, w_proj_ret, w_out)
    m_all = (m_w_in, m_w_proj_na, m_w_proj_ret, m_w_out)
    v_all = (v_w_in, v_w_proj_na, v_w_proj_ret, v_w_out)
    upd = [None] * 4
    after = d_ada
    for flight in scatters:
        mine, slabs = _push_wait(flight["name"], flight["sems"], flight["srcs"], flight["lands"], flight["axes"],
                                 _scatter_views, PEER_BITS, (after,))
        for (l, a), own, s in zip(flight["keys"], mine, slabs):
            upd[a] = _adamw_sharded(w_all[a], m_all[a], v_all[a], own, s, w_axes[a], idx_arr, l, upd[a],
                                    "adamw_" + w_names[a])
            after = upd[a][1]
    (g_w_in, d_w_in, nm_w_in, nv_w_in), (g_pna, d_pna, nm_pna, nv_pna) = upd[0], upd[1]
    (g_pret, d_pret, nm_pret, nv_pret), (g_out, d_out, nm_out, nv_out) = upd[2], upd[3]

    def order(cc, aw, ab, ng, wi, rp, dl, pn, pr, wo, fg):
        return [cc, aw, ab, ng, wi, rp, dl, pn, pr, wo, fg]

    grads_out = order(grad_c_ctx, grad_ada_w, grad_ada_b, grad_norm_g, g_w_in, grad_na_rpb, grad_decay, g_pna, g_pret, g_out, grad_final_g)
    delta_out = order(ds_[0], d_ada, ds_[1], ds_[2], d_w_in, ds_[3], ds_[4], d_pna, d_pret, d_out, ds_[5])
    m_out = order(nms_[0], nm_ada, nms_[1], nms_[2], nm_w_in, nms_[3], nms_[4], nm_pna, nm_pret, nm_out, nms_[5])
    v_out = order(nvs_[0], nv_ada, nvs_[1], nvs_[2], nv_w_in, nvs_[3], nvs_[4], nv_pna, nv_pret, nv_out, nvs_[5])
    return (loss, gx[None], *grads_out, *delta_out, *m_out, *v_out)
```

```python
import functools

import numpy as np
import jax
import jax.numpy as jnp
from jax import lax
from jax.experimental import pallas as pl
from jax.experimental.pallas import tpu as pltpu

F32 = jnp.float32
BF16 = jnp.bfloat16

N_DEV = 8
GRID_W = 64
NA_HEAD_DIM = 128
NA_WIN_ROWS = 8
NA_WIN_COLS = 16
RET_KEY_DIM = 128
RET_VAL_DIM = 256
RET_CHUNK = 128
ROPE_BASE = 10000.0
NORM_EPS = 1e-6
MASK_VALUE = -1e30

ADAM_LR = 0.001
ADAM_B1 = 0.9
ADAM_B2 = 0.999
ADAM_EPS = 1e-08
ADAM_WD = 0.01
ADAM_STEP = 10

VMEM_LIMIT = 48 * 1024 * 1024
MESH = pl.DeviceIdType.MESH
ANY = pl.BlockSpec(memory_space=pl.ANY)
VMEM_SPEC = pl.BlockSpec(memory_space=pltpu.VMEM)


def _params(sem=None):
    return pltpu.CompilerParams(dimension_semantics=sem, vmem_limit_bytes=VMEM_LIMIT)


def _pick(n, prefs):
    for p in prefs:
        if n % p == 0:
            return p
    return n


def _dot(a, b):
    return lax.dot_general(a, b, (((1,), (0,)), ((), ())), preferred_element_type=F32)


def _dot_nt(a, b):
    return lax.dot_general(a, b, (((1,), (1,)), ((), ())), preferred_element_type=F32)


def _dot_tn(a, b):
    return lax.dot_general(a, b, (((0,), (0,)), ((), ())), preferred_element_type=F32)


def _silu(x):
    return x * jax.nn.sigmoid(x)


def _matmul(a, b, *, trans_a=False, trans_b=False, out_dtype=F32, name="matmul", after=None):
    if trans_a:
        kdim, m = a.shape
    else:
        m, kdim = a.shape
    if trans_b:
        n, kb = b.shape
    else:
        kb, n = b.shape
    assert kdim == kb, (a.shape, b.shape, trans_a, trans_b)
    tm = _pick(m, (1152, 1024, 768, 512, 256, 128))
    tn = _pick(n, (512, 256, 128))
    tk = _pick(kdim, (2304, 2048, 1024, 512, 256, 128))
    nk = kdim // tk
    dn = (((0 if trans_a else 1,), (1 if trans_b else 0,)), ((), ()))

    def body(a_ref, b_ref, *rest):
        o_ref, acc_ref = rest[-2:]
        part = lax.dot_general(a_ref[...], b_ref[...], dn, preferred_element_type=F32)
        if nk == 1:
            o_ref[...] = part.astype(o_ref.dtype)
        else:
            k = pl.program_id(2)

            @pl.when(k == 0)
            def _():
                acc_ref[...] = part

            @pl.when(k > 0)
            def _():
                acc_ref[...] += part

            @pl.when(k == nk - 1)
            def _():
                o_ref[...] = acc_ref[...].astype(o_ref.dtype)

    a_spec = pl.BlockSpec((tk, tm), lambda i, j, k: (k, i)) if trans_a else pl.BlockSpec((tm, tk), lambda i, j, k: (i, k))
    b_spec = pl.BlockSpec((tn, tk), lambda i, j, k: (j, k)) if trans_b else pl.BlockSpec((tk, tn), lambda i, j, k: (k, j))
    return pl.pallas_call(
        body,
        name=name,
        grid=(m // tm, n // tn, nk),
        in_specs=[a_spec, b_spec] + ([] if after is None else [ANY]),
        out_specs=pl.BlockSpec((tm, tn), lambda i, j, k: (i, j)),
        out_shape=jax.ShapeDtypeStruct((m, n), out_dtype),
        scratch_shapes=[pltpu.VMEM((tm, tn) if nk > 1 else (8, 128), F32)],
        compiler_params=_params(("parallel", "parallel", "arbitrary")),
    )(*((a, b) if after is None else (a, b, after)))


def _make_rowwise(f, name, out_dtypes, out_cols, n_lat, tm, diff_rows, col_blocks=None, drow_dtypes=None):
    drow_dtypes = drow_dtypes or {}

    def tile_fn(*args):
        return tuple(o.astype(dt) for o, dt in zip(f(*args), out_dtypes))

    def row_spec(k, arr):
        width, index = (col_blocks or {}).get(k, (arr.shape[1], 0))
        return pl.BlockSpec((tm, width), lambda i: (i, index))

    def row_width(k, arr):
        return (col_blocks or {}).get(k, (arr.shape[1], 0))[0]

    def fwd_call(rows, vecs):
        t = rows[0].shape[0]
        nr, nv = len(rows), len(vecs)
        nl = n_lat // tm

        def body(*refs):
            grp = (pl.program_id(0) >= nl).astype(jnp.int32)
            args = [r[...] for r in refs[:nr]] + [v[grp] for v in refs[nr:nr + nv]]
            for o_ref, o in zip(refs[nr + nv:], tile_fn(*args)):
                o_ref[...] = o

        return pl.pallas_call(
            body,
            name=name + "_fwd",
            grid=(t // tm,),
            in_specs=[row_spec(k, r) for k, r in enumerate(rows)]
            + [pl.BlockSpec(v.shape, lambda i: (0, 0, 0)) for v in vecs],
            out_specs=[pl.BlockSpec((tm, c), lambda i: (i, 0)) for c in out_cols],
            out_shape=[jax.ShapeDtypeStruct((t, c), dt) for c, dt in zip(out_cols, out_dtypes)],
            compiler_params=_params(("parallel",)),
        )(*rows, *vecs)

    def bwd_call(rows, vecs, gs, acc=None):
        t = rows[0].shape[0]
        nr, nv, ng = len(rows), len(vecs), len(gs)
        nl = n_lat // tm
        nd = len(diff_rows)
        acc = [None] * nd if acc is None else list(acc)
        acc_in = [a for a in acc if a is not None]

        def body(*refs):
            i = pl.program_id(0)
            grp = (i >= nl).astype(jnp.int32)
            args = [r[...] for r in refs[:nr]] + [v[grp] for v in refs[nr:nr + nv]]
            g_refs = refs[nr + nv:nr + nv + ng]
            acc_refs = list(refs[nr + nv + ng:nr + nv + ng + len(acc_in)])
            drow_refs = refs[nr + nv + ng + len(acc_in):nr + nv + ng + len(acc_in) + nd]
            dvec_refs = refs[nr + nv + ng + len(acc_in) + nd:]
            _, vjp = jax.vjp(tile_fn, *args)
            grads = vjp(tuple(g[...] for g in g_refs))
            for d_ref, k, a in zip(drow_refs, diff_rows, acc):
                gk = grads[k] if a is None else grads[k] + acc_refs.pop(0)[...]
                d_ref[...] = gk.astype(d_ref.dtype)

            @pl.when(i == 0)
            def _():
                for d_ref in dvec_refs:
                    d_ref[...] = jnp.zeros_like(d_ref)

            for j, d_ref in enumerate(dvec_refs):
                d_ref[grp] += grads[nr + j]

        outs = pl.pallas_call(
            body,
            name=name + "_bwd",
            grid=(t // tm,),
            in_specs=[row_spec(k, r) for k, r in enumerate(rows)]
            + [pl.BlockSpec(v.shape, lambda i: (0, 0, 0)) for v in vecs]
            + [pl.BlockSpec((tm, g.shape[1]), lambda i: (i, 0)) for g in gs]
            + [pl.BlockSpec((tm, a.shape[1]), lambda i: (i, 0)) for a in acc_in],
            out_specs=[pl.BlockSpec((tm, row_width(k, rows[k])), lambda i: (i, 0)) for k in diff_rows]
            + [pl.BlockSpec(v.shape, lambda i: (0, 0, 0)) for v in vecs],
            out_shape=[jax.ShapeDtypeStruct((t, row_width(k, rows[k])), drow_dtypes.get(k, rows[k].dtype))
                       for k in diff_rows]
            + [jax.ShapeDtypeStruct(v.shape, F32) for v in vecs],
            compiler_params=_params(("arbitrary",)),
        )(*rows, *vecs, *gs, *acc_in)
        return outs[:nd], outs[nd:]

    return fwd_call, bwd_call


def _f_norm_mod(x, g, scale, shift):
    r = lax.rsqrt(jnp.mean(x * x, axis=-1, keepdims=True) + NORM_EPS)
    return ((x * r * g) * (1.0 + scale) + shift,)


def _f_gate_na(o, z):
    return (o.astype(F32) * _silu(z.astype(F32)),)


def _f_merge(g_na, g_ret, y_na, y_ret):
    return (jax.nn.sigmoid(g_na.astype(F32)) * y_na + jax.nn.sigmoid(g_ret.astype(F32)) * y_ret,)


def _f_residual(x, out, gate):
    return (x + gate * out,)


def _f_loss(x, target, g):
    r = lax.rsqrt(jnp.mean(x * x, axis=-1, keepdims=True) + NORM_EPS)
    y = x * r * g
    e = 0.5 * jnp.mean(jnp.square(y - target), axis=-1, keepdims=True)
    return (jnp.broadcast_to(e * (1.0 / 128.0), (x.shape[0], 128)),)


def _gate_ret_fwd_call(of, ob, z, zblk, tm):
    t, w = of.shape
    nh = w // RET_VAL_DIM

    def body(of_ref, ob_ref, z_ref, a_ref):
        for hh in range(nh):
            sl = slice(hh * RET_VAL_DIM, (hh + 1) * RET_VAL_DIM)
            o = of_ref[:, sl] + ob_ref[:, sl]
            r = lax.rsqrt(jnp.mean(o * o, axis=-1, keepdims=True) + NORM_EPS)
            a_ref[:, sl] = ((o * r) * _silu(z_ref[:, sl].astype(F32))).astype(a_ref.dtype)

    spec = pl.BlockSpec((tm, w), lambda i: (i, 0))
    zspec = pl.BlockSpec((tm, w), lambda i: (i, zblk))
    return pl.pallas_call(
        body, name="gate_ret_fwd", grid=(t // tm,), in_specs=[spec, spec, zspec], out_specs=spec,
        out_shape=jax.ShapeDtypeStruct((t, w), BF16), compiler_params=_params(("parallel",)),
    )(of, ob, z)


def _gate_ret_bwd_call(of, ob, z, zblk, da, tm):
    t, w = of.shape
    nh = w // RET_VAL_DIM

    def body(of_ref, ob_ref, z_ref, da_ref, do_ref, dz_ref):
        for hh in range(nh):
            sl = slice(hh * RET_VAL_DIM, (hh + 1) * RET_VAL_DIM)
            o = of_ref[:, sl] + ob_ref[:, sl]
            r = lax.rsqrt(jnp.mean(o * o, axis=-1, keepdims=True) + NORM_EPS)
            n = o * r
            zf = z_ref[:, sl].astype(F32)
            sg = jax.nn.sigmoid(zf)
            g = da_ref[:, sl].astype(F32)
            dn = g * (zf * sg)
            dz_ref[:, sl] = (g * n * (sg * (1.0 + zf * (1.0 - sg)))).astype(dz_ref.dtype)
            do_ref[:, sl] = r * (dn - n * jnp.mean(dn * n, axis=-1, keepdims=True))

    spec = pl.BlockSpec((tm, w), lambda i: (i, 0))
    zspec = pl.BlockSpec((tm, w), lambda i: (i, zblk))
    return pl.pallas_call(
        body, name="gate_ret_bwd", grid=(t // tm,), in_specs=[spec, spec, zspec, spec], out_specs=[spec, spec],
        out_shape=[jax.ShapeDtypeStruct((t, w), F32), jax.ShapeDtypeStruct((t, w), z.dtype)],
        compiler_params=_params(("parallel",)),
    )(of, ob, z, da)


def _na_geometry(t, n_lat):
    rows = n_lat // GRID_W
    kh = min(NA_WIN_ROWS, rows)
    return rows, kh, kh * GRID_W, t - n_lat, t // GRID_W


def _na_row0(r, rows, kh):
    return jnp.clip(r - kh // 2, 0, rows - kh)


def _na_bias_idx(r, rows, kh):
    return jnp.clip(_na_row0(r, rows, kh) - r + (NA_WIN_ROWS - 1), 0, NA_WIN_ROWS - 1)


def _na_group(rows, n_ctx):
    for g in (4, 2):
        if rows % g == 0 and (n_ctx // GRID_W) % g == 0:
            return g
    return 1


def _na_bias_spec(i, grp, rows, kh, n_loc):
    return pl.BlockSpec((1, 1, GRID_W, n_loc),
                        lambda h, g: (h, _na_bias_idx(jnp.minimum(g * grp + i, rows - grp + i), rows, kh), 0, 0))


def _na_fwd_call(q, k, v, col0, w, bt, n_lat):
    t = q.shape[0]
    nh = w // NA_HEAD_DIM
    rows, kh, n_loc, n_ctx, nq = _na_geometry(t, n_lat)
    grp = _na_group(rows, n_ctx)
    scale = NA_HEAD_DIM ** -0.5

    def body(q_ref, k_ref, v_ref, *rest):
        bt_refs, o_ref = rest[:grp], rest[grp]
        g = pl.program_id(1)
        kc = k_ref[pl.ds(n_lat, n_ctx), :]
        vc = v_ref[pl.ds(n_lat, n_ctx), :]

        @pl.when(g < rows // grp)
        def _():
            for i in range(grp):
                r = g * grp + i
                qb = q_ref[i * GRID_W:(i + 1) * GRID_W, :]
                s_ctx = _dot_nt(qb, kc) * scale
                start = pl.multiple_of(_na_row0(r, rows, kh) * GRID_W, GRID_W)
                kw = k_ref[pl.ds(start, n_loc), :]
                vw = v_ref[pl.ds(start, n_loc), :]
                s_loc = _dot_nt(qb, kw) * scale + bt_refs[i][0, 0]
                m = jnp.maximum(jnp.max(s_loc, axis=-1, keepdims=True), jnp.max(s_ctx, axis=-1, keepdims=True))
                p_loc = jnp.exp(s_loc - m)
                p_ctx = jnp.exp(s_ctx - m)
                l = jnp.sum(p_loc, axis=-1, keepdims=True) + jnp.sum(p_ctx, axis=-1, keepdims=True)
                o = _dot(p_loc.astype(BF16), vw) + _dot(p_ctx.astype(BF16), vc)
                o_ref[i * GRID_W:(i + 1) * GRID_W, :] = (o / l).astype(o_ref.dtype)

        @pl.when(g >= rows // grp)
        def _():
            s_ctx = _dot_nt(q_ref[...], kc) * scale
            m = jnp.max(s_ctx, axis=-1, keepdims=True)
            p = jnp.exp(s_ctx - m)
            l = jnp.sum(p, axis=-1, keepdims=True)
            o_ref[...] = (_dot(p.astype(BF16), vc) / l).astype(o_ref.dtype)

    qspec = pl.BlockSpec((grp * GRID_W, NA_HEAD_DIM), lambda h, g: (g, h))
    in_q = pl.BlockSpec((grp * GRID_W, NA_HEAD_DIM), lambda h, g: (g, col0[0] + h))
    in_k = pl.BlockSpec((t, NA_HEAD_DIM), lambda h, g: (0, col0[1] + h))
    in_v = pl.BlockSpec((t, NA_HEAD_DIM), lambda h, g: (0, col0[2] + h))
    return pl.pallas_call(
        body,
        name="na_attn_fwd",
        grid=(nh, nq // grp),
        in_specs=[in_q, in_k, in_v] + [_na_bias_spec(i, grp, rows, kh, n_loc) for i in range(grp)],
        out_specs=qspec,
        out_shape=jax.ShapeDtypeStruct((t, w), BF16),
        compiler_params=_params(("parallel", "arbitrary")),
    )(q, k, v, *([bt] * grp))


def _na_bwd_call(q, k, v, col0, w, bt, do, n_lat):
    t = q.shape[0]
    nh = w // NA_HEAD_DIM
    rows, kh, n_loc, n_ctx, nq = _na_geometry(t, n_lat)
    scale = NA_HEAD_DIM ** -0.5

    grp = _na_group(rows, n_ctx)

    def body(q_ref, k_ref, v_ref, do_ref, *rest):
        bt_refs = rest[:grp]
        dq_ref, dk_ref, dv_ref = rest[2 * grp:2 * grp + 3]
        dbt_refs = rest[2 * grp + 3:]
        g = pl.program_id(1)

        @pl.when(g == 0)
        def _():
            dk_ref[...] = jnp.zeros_like(dk_ref)
            dv_ref[...] = jnp.zeros_like(dv_ref)

        kc = k_ref[pl.ds(n_lat, n_ctx), :]
        vc = v_ref[pl.ds(n_lat, n_ctx), :]

        @pl.when(g < rows // grp)
        def _():
            for i in range(grp):
                r = g * grp + i
                sl = slice(i * GRID_W, (i + 1) * GRID_W)
                qb = q_ref[sl, :]
                dob = do_ref[sl, :]
                s_ctx = _dot_nt(qb, kc) * scale
                dp_ctx = _dot_nt(dob, vc)
                start = pl.multiple_of(_na_row0(r, rows, kh) * GRID_W, GRID_W)
                kw = k_ref[pl.ds(start, n_loc), :]
                vw = v_ref[pl.ds(start, n_loc), :]
                s_loc = _dot_nt(qb, kw) * scale + bt_refs[i][0, 0]
                m = jnp.maximum(jnp.max(s_loc, axis=-1, keepdims=True), jnp.max(s_ctx, axis=-1, keepdims=True))
                p_loc = jnp.exp(s_loc - m)
                p_ctx = jnp.exp(s_ctx - m)
                inv = 1.0 / (jnp.sum(p_loc, axis=-1, keepdims=True) + jnp.sum(p_ctx, axis=-1, keepdims=True))
                p_loc = p_loc * inv
                p_ctx = p_ctx * inv
                dp_loc = _dot_nt(dob, vw)
                delta = (jnp.sum(p_loc * dp_loc, axis=-1, keepdims=True)
                         + jnp.sum(p_ctx * dp_ctx, axis=-1, keepdims=True))
                ds_loc = p_loc * (dp_loc - delta)
                ds_ctx = p_ctx * (dp_ctx - delta)
                first = jnp.logical_or(g == 0, _na_bias_idx(r, rows, kh) != _na_bias_idx(r - grp, rows, kh))
                dbt_ref = dbt_refs[i]

                @pl.when(first)
                def _():
                    dbt_ref[0, 0] = ds_loc

                @pl.when(jnp.logical_not(first))
                def _():
                    dbt_ref[0, 0] += ds_loc

                dsl = (ds_loc * scale).astype(BF16)
                dsc = (ds_ctx * scale).astype(BF16)
                dq_ref[sl, :] = (_dot(dsl, kw) + _dot(dsc, kc)).astype(dq_ref.dtype)
                dk_ref[pl.ds(start, n_loc), :] += _dot_tn(dsl, qb)
                dv_ref[pl.ds(start, n_loc), :] += _dot_tn(p_loc.astype(BF16), dob)
                dk_ref[pl.ds(n_lat, n_ctx), :] += _dot_tn(dsc, qb)
                dv_ref[pl.ds(n_lat, n_ctx), :] += _dot_tn(p_ctx.astype(BF16), dob)

        @pl.when(g >= rows // grp)
        def _():
            qb = q_ref[...]
            dob = do_ref[...]
            s_ctx = _dot_nt(qb, kc) * scale
            dp_ctx = _dot_nt(dob, vc)
            m = jnp.max(s_ctx, axis=-1, keepdims=True)
            p = jnp.exp(s_ctx - m)
            p = p * (1.0 / jnp.sum(p, axis=-1, keepdims=True))
            delta = jnp.sum(p * dp_ctx, axis=-1, keepdims=True)
            dsc = (p * (dp_ctx - delta) * scale).astype(BF16)
            dq_ref[...] = _dot(dsc, kc).astype(dq_ref.dtype)
            dk_ref[pl.ds(n_lat, n_ctx), :] += _dot_tn(dsc, qb)
            dv_ref[pl.ds(n_lat, n_ctx), :] += _dot_tn(p.astype(BF16), dob)

    qspec = pl.BlockSpec((grp * GRID_W, NA_HEAD_DIM), lambda h, g: (g, h))
    kspec = pl.BlockSpec((t, NA_HEAD_DIM), lambda h, g: (0, h))
    bspecs = [_na_bias_spec(i, grp, rows, kh, n_loc) for i in range(grp)]
    zeros = [jnp.zeros(bt.shape, F32) for _ in range(grp)]
    outs = pl.pallas_call(
        body,
        name="na_attn_bwd",
        grid=(nh, nq // grp),
        in_specs=[pl.BlockSpec((grp * GRID_W, NA_HEAD_DIM), lambda h, g: (g, col0[0] + h)),
                  pl.BlockSpec((t, NA_HEAD_DIM), lambda h, g: (0, col0[1] + h)),
                  pl.BlockSpec((t, NA_HEAD_DIM), lambda h, g: (0, col0[2] + h)), qspec] + bspecs + [ANY] * grp,
        out_specs=[qspec, kspec, kspec] + bspecs,
        out_shape=[
            jax.ShapeDtypeStruct((t, w), BF16),
            jax.ShapeDtypeStruct((t, w), F32),
            jax.ShapeDtypeStruct((t, w), F32),
        ] + [jax.ShapeDtypeStruct(bt.shape, F32)] * grp,
        input_output_aliases={4 + grp + i: 3 + i for i in range(grp)},
        compiler_params=_params(("parallel", "arbitrary")),
    )(q, k, v, do, *([bt] * grp), *zeros)
    dbt = outs[3]
    for extra in outs[4:]:
        dbt = dbt + extra
    return outs[0], outs[1], outs[2], dbt


def _na_bias_table(rpb, rows):
    kh = min(NA_WIN_ROWS, rows)
    nj = NA_WIN_ROWS
    e1 = np.zeros((nj, kh, 2 * NA_WIN_ROWS - 1), np.float32)
    for j in range(nj):
        for kk in range(kh):
            if j + kk < 2 * NA_WIN_ROWS - 1:
                e1[j, kk, j + kk] = 1.0
    cidx = np.arange(GRID_W)
    dc = np.clip(cidx[None, :] - cidx[:, None] + (NA_WIN_COLS - 1), 0, 2 * NA_WIN_COLS - 2)
    e2 = np.zeros((GRID_W, GRID_W, 2 * NA_WIN_COLS - 1), np.float32)
    e2[np.arange(GRID_W)[:, None], np.arange(GRID_W)[None, :], dc] = 1.0
    c0 = np.clip(cidx - NA_WIN_COLS // 2, 0, GRID_W - NA_WIN_COLS)
    col_in = (cidx[None, :] >= c0[:, None]) & (cidx[None, :] < c0[:, None] + NA_WIN_COLS)
    t1 = jnp.einsum("hab,jka->hjkb", rpb, jnp.asarray(e1), precision=lax.Precision.HIGHEST)
    b = jnp.einsum("hjkb,cwb->hjckw", t1, jnp.asarray(e2), precision=lax.Precision.HIGHEST)
    b = jnp.where(jnp.asarray(col_in)[None, None, :, None, :], b, MASK_VALUE)
    return b.reshape(rpb.shape[0], nj, GRID_W, kh * GRID_W)


def _ret_decays(lam_s, reverse):
    c = RET_CHUNK
    ii = lax.broadcasted_iota(jnp.int32, (c, c), 0)
    jj = lax.broadcasted_iota(jnp.int32, (c, c), 1)
    d = (jj - ii) if reverse else (ii - jj)
    dpos = jnp.maximum(d.astype(F32), 0.0)
    mask = jnp.where(d >= 0, jnp.exp(dpos * lam_s), 0.0)
    pi = lax.broadcasted_iota(jnp.int32, (c, 1), 0).astype(F32)
    qpos = (c - pi) if reverse else (pi + 1.0)
    kpos = pi if reverse else (c - 1.0 - pi)
    qd = jnp.exp(qpos * lam_s)
    kd = jnp.exp(kpos * lam_s)
    g = jnp.exp(jnp.full((1, RET_VAL_DIM), c * lam_s, F32))
    return mask, dpos, qd, kd, qpos, kpos, g


def _ret_head_group(nh):
    return _pick(nh, (4, 2))


def _ret_chunk_of(t, nt, nl, reverse):
    return (nt - 1 - t) if reverse else (t + nl) % nt


def _ret_fwd_call(qr, kr, v, vcol, lam, n_lat, reverse):
    t = qr.shape[0]
    nh = qr.shape[1] // RET_KEY_DIM
    c = RET_CHUNK
    nt, nl = t // c, n_lat // c

    hg = _ret_head_group(nh)
    dk, dv = RET_KEY_DIM, RET_VAL_DIM

    def body(lam_ref, q_ref, k_ref, v_ref, o_ref, s_ref, state):
        hb, step = pl.program_id(0), pl.program_id(1)

        @pl.when(step == 0)
        def _():
            state[...] = jnp.zeros_like(state)

        for j in range(hg):
            mask, _, qd, kd, _, _, g = _ret_decays(lam_ref[hb * hg + j], reverse)
            q, k, vv = q_ref[:, j * dk:(j + 1) * dk], k_ref[:, j * dk:(j + 1) * dk], v_ref[:, j * dv:(j + 1) * dv]
            p = _dot_nt(q, k) * mask
            s = state[j]
            qs = (q.astype(F32) * qd).astype(BF16)
            o_ref[:, j * dv:(j + 1) * dv] = _dot(p.astype(BF16), vv) + _dot(qs, s.astype(BF16))
            s_ref[j, 0] = s
            ks = (k.astype(F32) * kd).astype(BF16)
            state[j] = s * g + _dot_tn(ks, vv)

    def cmap(hb, step, lam_ref):
        return (_ret_chunk_of(step, nt, nl, reverse), hb)

    def vmap(hb, step, lam_ref):
        return (_ret_chunk_of(step, nt, nl, reverse), vcol // (hg * dv) + hb)

    return pl.pallas_call(
        body,
        name="retention_rev_fwd" if reverse else "retention_fwd",
        grid_spec=pltpu.PrefetchScalarGridSpec(
            num_scalar_prefetch=1,
            grid=(nh // hg, nt),
            in_specs=[
                pl.BlockSpec((c, hg * dk), cmap),
                pl.BlockSpec((c, hg * dk), cmap),
                pl.BlockSpec((c, hg * dv), vmap),
            ],
            out_specs=[
                pl.BlockSpec((c, hg * dv), cmap),
                pl.BlockSpec((hg, 1, dk, dv), lambda hb, step, lam_ref: (hb, step, 0, 0)),
            ],
            scratch_shapes=[pltpu.VMEM((hg, dk, dv), F32)],
        ),
        out_shape=[
            jax.ShapeDtypeStruct((t, nh * RET_VAL_DIM), F32),
            jax.ShapeDtypeStruct((nh, nt, RET_KEY_DIM, RET_VAL_DIM), F32),
        ],
        compiler_params=_params(("parallel", "arbitrary")),
    )(lam, qr, kr, v)


def _ret_bwd_call(qr, kr, v, vcol, lam, states, do, n_lat, reverse):
    t = qr.shape[0]
    nh = qr.shape[1] // RET_KEY_DIM
    c = RET_CHUNK
    nt, nl = t // c, n_lat // c

    hg = _ret_head_group(nh)
    dk, dv = RET_KEY_DIM, RET_VAL_DIM

    def body(lam_ref, q_ref, k_ref, v_ref, s_ref, do_ref, dq_ref, dk_ref, dv_ref, dl_ref, dstate):
        hb, rstep = pl.program_id(0), pl.program_id(1)

        @pl.when(rstep == 0)
        def _():
            dstate[...] = jnp.zeros_like(dstate)
            dl_ref[...] = jnp.zeros_like(dl_ref)

        for j in range(hg):
            mask, dpos, qd, kd, qpos, kpos, g = _ret_decays(lam_ref[hb * hg + j], reverse)
            ksl, vsl = slice(j * dk, (j + 1) * dk), slice(j * dv, (j + 1) * dv)
            q, k, vv = q_ref[:, ksl], k_ref[:, ksl], v_ref[:, vsl]
            qf, kf = q.astype(F32), k.astype(F32)
            s = s_ref[j, 0]
            ds = dstate[j]
            dob = do_ref[:, vsl].astype(BF16)
            sb, dsb = s.astype(BF16), ds.astype(BF16)
            a = _dot_nt(q, k)
            p = a * mask
            dp = _dot_nt(dob, vv)
            da = dp * mask
            dab = da.astype(BF16)
            dqc = _dot_nt(dob, sb)
            dkc = _dot_nt(vv, dsb)
            qs = (qf * qd).astype(BF16)
            ks = (kf * kd).astype(BF16)
            dq_ref[:, ksl] = (_dot(dab, k) + dqc * qd).astype(dq_ref.dtype)
            dk_ref[:, ksl] = (_dot_tn(dab, q) + dkc * kd).astype(dk_ref.dtype)
            dv_ref[:, vsl] = (_dot_tn(p.astype(BF16), dob) + _dot(ks, dsb)).astype(dv_ref.dtype)
            terms = (
                jnp.sum(jnp.sum(da * a * dpos, axis=1, keepdims=True), axis=0, keepdims=True)
                + jnp.sum(jnp.sum(dqc * qf * (qd * qpos), axis=1, keepdims=True), axis=0, keepdims=True)
                + jnp.sum(jnp.sum(dkc * kf * (kd * kpos), axis=1, keepdims=True), axis=0, keepdims=True)
                + jnp.sum(jnp.sum(ds * s * (g * c), axis=1, keepdims=True), axis=0, keepdims=True)
            )
            dl_ref[j] += jnp.broadcast_to(terms, (8, 128))
            dstate[j] = ds * g + _dot_tn(qs, dob)

    def cmap(hb, rstep, lam_ref):
        return (_ret_chunk_of(nt - 1 - rstep, nt, nl, reverse), hb)

    def vmap(hb, rstep, lam_ref):
        return (_ret_chunk_of(nt - 1 - rstep, nt, nl, reverse), vcol // (hg * dv) + hb)

    return pl.pallas_call(
        body,
        name="retention_rev_bwd" if reverse else "retention_bwd",
        grid_spec=pltpu.PrefetchScalarGridSpec(
            num_scalar_prefetch=1,
            grid=(nh // hg, nt),
            in_specs=[
                pl.BlockSpec((c, hg * dk), cmap),
                pl.BlockSpec((c, hg * dk), cmap),
                pl.BlockSpec((c, hg * dv), vmap),
                pl.BlockSpec((hg, 1, dk, dv), lambda hb, rstep, lam_ref: (hb, nt - 1 - rstep, 0, 0)),
                pl.BlockSpec((c, hg * dv), cmap),
            ],
            out_specs=[
                pl.BlockSpec((c, hg * dk), cmap),
                pl.BlockSpec((c, hg * dk), cmap),
                pl.BlockSpec((c, hg * dv), cmap),
                pl.BlockSpec((hg, 8, 128), lambda hb, rstep, lam_ref: (hb, 0, 0)),
            ],
            scratch_shapes=[pltpu.VMEM((hg, dk, dv), F32)],
        ),
        out_shape=[
            jax.ShapeDtypeStruct(qr.shape, qr.dtype),
            jax.ShapeDtypeStruct(kr.shape, kr.dtype),
            jax.ShapeDtypeStruct((t, nh * dv), v.dtype),
            jax.ShapeDtypeStruct((nh, 8, 128), F32),
        ],
        compiler_params=_params(("parallel", "arbitrary")),
    )(lam, qr, kr, v, states, do)


def _rope_tables(t, n_lat):
    nf = RET_KEY_DIM // 4
    tok = np.arange(n_lat)
    inv_freq = (ROPE_BASE ** (-np.arange(nf, dtype=np.float32) / nf)).astype(np.float32)
    row = (tok // GRID_W).astype(np.float32)
    col = (tok % GRID_W).astype(np.float32)
    ang = np.concatenate([row[:, None] * inv_freq, col[:, None] * inv_freq], axis=-1).astype(np.float32)
    cos = np.ones((t, 2 * nf), np.float32)
    sin = np.zeros((t, 2 * nf), np.float32)
    cos[:n_lat] = np.cos(ang)
    sin[:n_lat] = np.sin(ang)
    return jnp.asarray(cos), jnp.asarray(sin)


def _rope(xb, cos, sin, mult):
    t, w = xb.shape
    nh = w // RET_KEY_DIM
    half = RET_KEY_DIM // 2
    x = xb.astype(F32).reshape(t, nh, 2, half)
    x1, x2 = x[:, :, 0], x[:, :, 1]
    c, s = cos[:, None, :], sin[:, None, :]
    out = jnp.stack([x1 * c - x2 * s, x2 * c + x1 * s], axis=2) * mult
    return out.reshape(t, w).astype(BF16)


def _my_position():
    return lax.axis_index("x"), lax.axis_index("y"), lax.axis_index("c")


def _flip(pos, k):
    x, y, c = pos
    return (1 - x if k & 4 else x, 1 - y if k & 2 else y, 1 - c if k & 1 else c)


def _linear(pos):
    return 4 * pos[0] + 2 * pos[1] + pos[2]


def _slab(ref, axis, idx, size):
    start = pl.multiple_of(idx * size, size)
    return ref.at[pl.ds(start, size), :] if axis == 0 else ref.at[:, pl.ds(start, size)]


HBM_SPEC = pl.BlockSpec(memory_space=pltpu.HBM)
SEM_SPEC = pl.BlockSpec(memory_space=pltpu.SEMAPHORE)
DATAFLOW = pltpu.SideEffectType.DATAFLOW_SIDE_EFFECTING
PEER_BITS = (1, 2, 4, 6, 3, 5, 7)
GATHER_BITS = (1, 2, 4, 6)


def _in_hbm(a):
    return pltpu.with_memory_space_constraint(a, pltpu.HBM)


def _gather_views(me, k, a, src_refs, land_refs, axes):
    size = src_refs[a].shape[axes[a]]
    peer = _flip(me, k)
    return src_refs[a], _slab(land_refs[a], axes[a], _linear(me), size), _slab(land_refs[a], axes[a], _linear(peer), size)


def _scatter_views(me, k, a, src_refs, land_refs, axes):
    size = land_refs[a].shape[1 + axes[a]]
    peer = _flip(me, k)
    return _slab(src_refs[a], axes[a], _linear(peer), size), land_refs[a].at[k - 1], land_refs[a].at[k - 1]


def _slab_block(rows, cols, tm, axis):
    if axis == 0:
        return pl.BlockSpec((tm, cols), lambda i, idx: (idx[0] * (rows // tm) + i, 0))
    return pl.BlockSpec((tm, cols), lambda i, idx: (i, idx[0]))


def _place_shard(shard, land, axis, my_idx):
    r, c = shard.shape
    tm = _pick(r, (512, 256, 128, 64, 32, 16))

    def body(idx_ref, s_ref, land_ref, o_ref):
        o_ref[...] = s_ref[...]

    return pl.pallas_call(
        body, name="gather_place",
        grid_spec=pltpu.PrefetchScalarGridSpec(
            num_scalar_prefetch=1, grid=(r // tm,),
            in_specs=[pl.BlockSpec((tm, c), lambda i, idx: (i, 0)), ANY],
            out_specs=_slab_block(r, c, tm, axis)),
        out_shape=jax.ShapeDtypeStruct(land.shape, land.dtype),
        input_output_aliases={2: 0},
        compiler_params=_params(("parallel",)),
    )(my_idx, shard, land)


def _push_start(name, srcs, lands, axes, views, bits, deps):
    ns = len(srcs)

    def body(*refs):
        src_refs, land_refs = refs[:ns], refs[ns:2 * ns]
        send_sems, recv_sems = refs[2 * ns + len(deps):2 * ns + len(deps) + 2]
        token = refs[-1]
        me = _my_position()
        for k in bits:
            for a in range(ns):
                s, d, _ = views(me, k, a, src_refs, land_refs, axes)
                pltpu.make_async_remote_copy(
                    src_ref=s, dst_ref=d, send_sem=send_sems.at[7 * a + k - 1], recv_sem=recv_sems.at[7 * a + k - 1],
                    device_id=_flip(me, k), device_id_type=MESH).start()
        token[...] = jnp.zeros_like(token)

    thru = [pltpu.HBM(a.shape, a.dtype) for a in list(srcs) + list(lands)]
    outs = pl.pallas_call(
        body, name=name,
        in_specs=[HBM_SPEC] * (2 * ns) + [ANY] * len(deps),
        out_specs=[SEM_SPEC, SEM_SPEC] + [HBM_SPEC] * (2 * ns) + [VMEM_SPEC],
        out_shape=[pltpu.SemaphoreType.DMA((7 * ns,)), pltpu.SemaphoreType.DMA((7 * ns,))] + thru
        + [jax.ShapeDtypeStruct((8, 128), F32)],
        input_output_aliases={i: 2 + i for i in range(2 * ns)},
        compiler_params=pltpu.CompilerParams(has_side_effects=DATAFLOW),
    )(*[_in_hbm(a) for a in srcs], *[_in_hbm(a) for a in lands], *deps)
    return (outs[0], outs[1]), outs[2:2 + ns], outs[2 + ns:2 + 2 * ns], outs[-1]


def _gather_finish(lands, axes, sizes):
    ns = len(lands)
    chips = (2, 4, 6)

    def body(*refs):
        land_refs = refs[ns:2 * ns]
        send_sems, recv_sems = refs[2 * ns:]
        me = _my_position()
        sibling = _flip(me, 1)
        copies = []
        for j, kc in enumerate(chips):
            for a in range(ns):
                def slab_of(pos):
                    return _slab(land_refs[a], axes[a], _linear(pos), sizes[a])
                send = pltpu.make_async_remote_copy(
                    src_ref=slab_of(_flip(me, kc)), dst_ref=slab_of(_flip(me, kc)), send_sem=send_sems.at[3 * a + j],
                    recv_sem=recv_sems.at[3 * a + j], device_id=sibling, device_id_type=MESH)
                recv = pltpu.make_async_remote_copy(
                    src_ref=slab_of(_flip(me, kc)), dst_ref=slab_of(_flip(sibling, kc)), send_sem=send_sems.at[3 * a + j],
                    recv_sem=recv_sems.at[3 * a + j], device_id=sibling, device_id_type=MESH)
                send.start()
                copies.append((send, recv))
        for send, recv in copies:
            recv.wait_recv()
        for send, recv in copies:
            send.wait_send()

    return pl.pallas_call(
        body, name="gather_finish", in_specs=[ANY] * ns, out_specs=[ANY] * ns,
        out_shape=[jax.ShapeDtypeStruct(l.shape, l.dtype) for l in lands],
        input_output_aliases={a: a for a in range(ns)},
        scratch_shapes=[pltpu.SemaphoreType.DMA((3 * ns,)), pltpu.SemaphoreType.DMA((3 * ns,))],
        compiler_params=pltpu.CompilerParams(has_side_effects=True),
    )(*lands)


def _push_wait(name, sems, srcs, lands, axes, views, bits, after):
    ns = len(srcs)

    def body(*refs):
        src_refs, land_refs = refs[:ns], refs[ns:2 * ns]
        send_sems, recv_sems = refs[2 * ns:2 * ns + 2]
        me = _my_position()
        for k in bits:
            for a in range(ns):
                s, d, got = views(me, k, a, src_refs, land_refs, axes)
                cp = pltpu.make_async_remote_copy(
                    src_ref=s, dst_ref=got, send_sem=send_sems.at[7 * a + k - 1], recv_sem=recv_sems.at[7 * a + k - 1],
                    device_id=_flip(me, k), device_id_type=MESH)
                cp.wait_send()
                cp.wait_recv()

    thru = [pltpu.HBM(a.shape, a.dtype) for a in list(srcs) + list(lands)]
    outs = pl.pallas_call(
        body, name=name,
        in_specs=[HBM_SPEC] * (2 * ns) + [SEM_SPEC, SEM_SPEC] + [ANY] * len(after),
        out_specs=[HBM_SPEC] * (2 * ns),
        out_shape=thru,
        input_output_aliases={i: i for i in range(2 * ns)},
        compiler_params=pltpu.CompilerParams(has_side_effects=DATAFLOW),
    )(*srcs, *lands, sems[0], sems[1], *after)
    return outs[:ns], outs[ns:]


def _small_allgather(v, name):
    r, c = v.shape

    def body(v_ref, all_ref, sum_ref, send_sems, recv_sems):
        me = _my_position()
        all_ref[_linear(me)] = v_ref[...]
        copies = []
        for k in range(1, N_DEV):
            peer = _flip(me, k)
            copies.append(pltpu.make_async_remote_copy(
                src_ref=v_ref, dst_ref=all_ref.at[_linear(me)], send_sem=send_sems.at[k - 1], recv_sem=recv_sems.at[k - 1],
                device_id=peer, device_id_type=MESH))
        for cp in copies:
            cp.start()
        for k in range(1, N_DEV):
            peer = _flip(me, k)
            pltpu.make_async_remote_copy(
                src_ref=v_ref, dst_ref=all_ref.at[_linear(peer)], send_sem=send_sems.at[k - 1], recv_sem=recv_sems.at[k - 1],
                device_id=peer, device_id_type=MESH).wait_recv()
        for cp in copies:
            cp.wait_send()
        acc = all_ref[0]
        for d in range(1, N_DEV):
            acc = acc + all_ref[d]
        sum_ref[...] = acc

    return pl.pallas_call(
        body,
        name=name,
        in_specs=[VMEM_SPEC],
        out_specs=[VMEM_SPEC, VMEM_SPEC],
        out_shape=[jax.ShapeDtypeStruct((N_DEV, r, c), F32), jax.ShapeDtypeStruct((r, c), F32)],
        scratch_shapes=[pltpu.SemaphoreType.DMA((N_DEV - 1,)), pltpu.SemaphoreType.DMA((N_DEV - 1,))],
        compiler_params=pltpu.CompilerParams(has_side_effects=True, vmem_limit_bytes=VMEM_LIMIT),
    )(v)


def _ada_fwd_call(cin, ada_w, ada_b_cols):
    nl, d, ncol = ada_w.shape
    nrow = cin.shape[0]

    def body(c_ref, w_ref, b_ref, o_ref):
        cs = _silu(c_ref[...]).astype(BF16)
        for l in range(nl):
            o_ref[l] = _dot(cs, w_ref[l].astype(BF16)) + b_ref[l]

    return pl.pallas_call(
        body, name="ada_fwd", in_specs=[VMEM_SPEC] * 3, out_specs=VMEM_SPEC,
        out_shape=jax.ShapeDtypeStruct((nl, nrow, ncol), F32), compiler_params=_params(),
    )(cin, ada_w, ada_b_cols)


def _ada_bwd_call(cin, ada_w, dmod):
    nl, d, ncol = ada_w.shape
    nrow = cin.shape[0]

    def body(c_ref, w_ref, dm_ref, gw_ref, dcs_ref):
        cs = _silu(c_ref[...]).astype(BF16)
        acc = jnp.zeros((nrow, d), F32)
        for l in range(nl):
            dm = dm_ref[l].astype(BF16)
            gw_ref[l] = _dot_tn(cs, dm)
            acc = acc + _dot_nt(dm, w_ref[l].astype(BF16))
        dcs_ref[...] = acc

    return pl.pallas_call(
        body, name="ada_bwd", in_specs=[VMEM_SPEC] * 3, out_specs=[VMEM_SPEC, VMEM_SPEC],
        out_shape=[jax.ShapeDtypeStruct((nl, d, ncol), F32), jax.ShapeDtypeStruct((nrow, d), F32)],
        compiler_params=_params(),
    )(cin, ada_w, dmod)


def _adamw_math(w, g, m, v):
    m = ADAM_B1 * m + (1.0 - ADAM_B1) * g
    v = ADAM_B2 * v + (1.0 - ADAM_B2) * jnp.square(g)
    m_hat = m / (1.0 - ADAM_B1 ** ADAM_STEP)
    v_hat = v / (1.0 - ADAM_B2 ** ADAM_STEP)
    delta = -ADAM_LR * (m_hat / (jnp.sqrt(v_hat) + ADAM_EPS) + ADAM_WD * w)
    return delta, m, v


def _adamw_sharded(w, m, v, mine, slabs, axis, my_idx, layer, prev, name):
    nl, r, c = w.shape
    tm = _pick(r, (128, 64, 32, 16))
    nprev = 0 if prev is None else len(prev)

    def body(idx_ref, w_ref, m_ref, v_ref, mine_ref, s_ref, *rest):
        g_ref, d_ref, nm_ref, nv_ref = rest[nprev:]
        g = mine_ref[...].astype(F32)
        for k in range(N_DEV - 1):
            g = g + s_ref[k].astype(F32)
        delta, nm, nv = _adamw_math(w_ref[0], g, m_ref[0], v_ref[0])
        g_ref[0], d_ref[0], nm_ref[0], nv_ref[0] = g, delta, nm, nv

    spec = pl.BlockSpec((1, tm, c), lambda i, idx: (layer, i, 0))
    out = jax.ShapeDtypeStruct(w.shape, F32)
    return pl.pallas_call(
        body, name=name,
        grid_spec=pltpu.PrefetchScalarGridSpec(
            num_scalar_prefetch=1, grid=(r // tm,),
            in_specs=[spec, spec, spec, _slab_block(r, c, tm, axis),
                      pl.BlockSpec((N_DEV - 1, tm, c), lambda i, idx: (0, i, 0))] + [ANY] * nprev,
            out_specs=[spec] * 4),
        out_shape=[out] * 4,
        input_output_aliases={6 + j: j for j in range(nprev)},
        compiler_params=_params(("parallel",)),
    )(my_idx, w, m, v, mine, slabs, *(() if prev is None else prev))


def _adamw_dense(w, g, m, v, name):
    r, c = w.shape
    tm = _pick(r, (256, 128, 64, 32, 16, 8))

    def body(w_ref, g_ref, m_ref, v_ref, d_ref, nm_ref, nv_ref):
        d_ref[...], nm_ref[...], nv_ref[...] = _adamw_math(w_ref[...], g_ref[...], m_ref[...], v_ref[...])

    spec = pl.BlockSpec((tm, c), lambda i: (i, 0))
    out = jax.ShapeDtypeStruct(w.shape, F32)
    return pl.pallas_call(
        body, name=name, grid=(r // tm,), in_specs=[spec] * 4, out_specs=[spec] * 3, out_shape=[out] * 3,
        compiler_params=_params(("parallel",)),
    )(w, g, m, v)


def _pack(parts, width=128):
    flat = jnp.concatenate([p.reshape(-1).astype(F32) for p in parts])
    n = flat.shape[0]
    total = -(-n // (8 * width)) * (8 * width)
    return jnp.pad(flat, (0, total - n)).reshape(total // width, width)


def _unpack(buf, shapes):
    flat = buf.reshape(-1)
    out, off = [], 0
    for s in shapes:
        n = int(np.prod(s))
        out.append(flat[off:off + n].reshape(s))
        off += n
    return out


def kernel(x, c, ctx, c_ctx, ada_w, ada_b, norm_g, w_in, na_rpb, ret_decay_logit, w_proj_na, w_proj_ret, w_out, final_g, loss_target, m_c_ctx, m_ada_w, m_ada_b, m_norm_g, m_w_in, m_na_rpb, m_ret_decay_logit, m_w_proj_na, m_w_proj_ret, m_w_out, m_final_g, v_c_ctx, v_ada_w, v_ada_b, v_norm_g, v_w_in, v_na_rpb, v_ret_decay_logit, v_w_proj_na, v_w_proj_ret, v_w_out, v_final_g):
    depth = w_in.shape[0]
    n_lat, d = x.shape[1], x.shape[2]
    n_ctx = ctx.shape[1]
    t = n_lat + n_ctx
    w_na = w_proj_na.shape[1]
    w_retv = w_proj_ret.shape[1] * N_DEV
    in_cols = w_in.shape[2] * N_DEV
    w_qk = (in_cols - 4 * w_na - 2 * w_retv - 2 * d) // 2
    sizes = (w_na, w_na, w_na, w_na, w_qk, w_qk, w_retv, w_retv, d, d)
    off = tuple(int(o) for o in np.cumsum((0,) + sizes))
    NA_Q, NA_K, NA_V, NA_Z, RET_Q, RET_K, RET_V, RET_Z, G_NA, G_RET = range(10)
    rows = n_lat // GRID_W
    me = _my_position()
    my_idx = _linear(me)
    tm_row = _pick(n_ctx, (256, 128))

    idx_arr = jnp.reshape(my_idx, (1,)).astype(jnp.int32)

    ncol = ada_w.shape[2]
    c_all, _ = _small_allgather(jnp.pad(c, ((0, 7), (0, 0))), "allgather_c")
    cin = jnp.concatenate([c_all[:, 0, :], c_ctx[None, :], jnp.zeros((7, d), F32)], axis=0)
    ada_b_cols = lax.dynamic_slice_in_dim(ada_b, my_idx * ncol, ncol, axis=1)[:, None, :]
    mod_cols = _ada_fwd_call(cin, ada_w, ada_b_cols)
    mod_gathered, _ = _small_allgather(mod_cols.reshape(depth * 16, ncol), "allgather_mod")
    mod_all = mod_gathered.reshape(N_DEV, depth, 16, ncol).transpose(1, 2, 0, 3).reshape(depth, 16, N_DEV * ncol)
    mod_lat = lax.dynamic_index_in_dim(mod_all, my_idx, axis=1, keepdims=False)
    mod_ctx = mod_all[:, 8, :]

    w_axes = (1, 1, 0, 0)
    w_names = ("w_in", "w_proj_na", "w_proj_ret", "w_out")
    shard = [[w[l].astype(BF16) for w in (w_in, w_proj_na, w_proj_ret, w_out)] for l in range(depth)]
    groups = [[(0, 0)], [(0, 1), (0, 2), (0, 3)]] + [[(l, a) for a in range(4)] for l in range(1, depth)]
    gathers, token = {}, mod_gathered
    for gi, keys in enumerate(groups):
        srcs = [shard[l][a] for l, a in keys]
        axes = tuple(w_axes[a] for _, a in keys)
        lands = [_place_shard(s, lax.empty(tuple(n * (N_DEV if i == ax else 1) for i, n in enumerate(s.shape)), BF16),
                              ax, idx_arr) for s, ax in zip(srcs, axes)]
        sizes = tuple(s.shape[ax] for s, ax in zip(srcs, axes))
        sems, srcs, lands, token = _push_start(
            f"gather_start_{gi}", srcs, lands, axes, _gather_views, GATHER_BITS, (token,))
        flight = dict(name=f"gather_wait_{gi}", sems=sems, srcs=srcs, lands=lands, axes=axes, sizes=sizes, ready=None)
        for pos, key in enumerate(keys):
            gathers[key] = (flight, pos)

    def landed(l, a, act):
        flight, pos = gathers[(l, a)]
        if flight["ready"] is None:
            arrived = _push_wait(flight["name"], flight["sems"], flight["srcs"], flight["lands"],
                                 flight["axes"], _gather_views, GATHER_BITS, (act, token))[1]
            flight["ready"] = _gather_finish(arrived, flight["axes"], flight["sizes"])
        return flight["ready"][pos]

    pending, scatters = {}, []

    def send_dw(l, a, dw):
        pending[(l, a)] = dw
        if a == 0:
            keys = [(0, 0)] if l == 0 else [(l, b) for b in range(4)]
        elif l == 0 and a == 1:
            keys = [(0, 1), (0, 2), (0, 3)]
        else:
            return None
        srcs = [pending[k] for k in keys]
        axes = tuple(w_axes[b] for _, b in keys)
        lands = [lax.empty((N_DEV - 1,) + tuple(n // (N_DEV if i == ax else 1) for i, n in enumerate(s.shape)), BF16)
                 for s, ax in zip(srcs, axes)]
        sems, srcs, lands, tok = _push_start(
            f"scatter_start_{len(scatters)}", srcs, lands, axes, _scatter_views, PEER_BITS, ())
        scatters.append(dict(name=f"scatter_wait_{len(scatters)}", sems=sems, srcs=srcs, lands=lands, axes=axes, keys=keys))
        return tok

    cos, sin = _rope_tables(t, n_lat)
    k_scale = RET_KEY_DIM ** -0.5
    assert off[NA_Z] % w_na == 0 and off[G_NA] % d == 0 and off[G_RET] % d == 0 and off[RET_Z] % w_retv == 0
    assert off[RET_V] % (_ret_head_group(w_retv // RET_VAL_DIM) * RET_VAL_DIM) == 0
    na_cols = tuple(off[i] // NA_HEAD_DIM for i in (NA_Q, NA_K, NA_V))
    norm_mod_fwd, norm_mod_bwd = _make_rowwise(_f_norm_mod, "norm_mod", (BF16,), (d,), n_lat, tm_row, (0,))
    gate_na_fwd, gate_na_bwd = _make_rowwise(_f_gate_na, "gate_na", (BF16,), (w_na,), n_lat, tm_row, (0, 1),
                                             col_blocks={1: (w_na, off[NA_Z] // w_na)})
    merge_fwd, merge_bwd = _make_rowwise(_f_merge, "merge", (BF16,), (d,), n_lat, tm_row, (0, 1, 2, 3),
                                         col_blocks={0: (d, off[G_NA] // d), 1: (d, off[G_RET] // d)},
                                         drow_dtypes={2: BF16, 3: BF16})
    residual_fwd, residual_bwd = _make_rowwise(_f_residual, "residual", (F32,), (d,), n_lat, tm_row, (0, 1),
                                               drow_dtypes={1: BF16})
    loss_fwd, loss_bwd = _make_rowwise(_f_loss, "loss_head", (F32,), (128,), n_lat, tm_row, (0,))

    def pair(a, b):
        return jnp.stack([a, b])[:, None, :]

    def mod_vectors(mod_lat_l, mod_ctx_l, norm_g_l):
        shift, scale, gate = jnp.split(mod_lat_l, 3)
        c_shift, c_scale, c_gate = jnp.split(mod_ctx_l, 3)
        return pair(norm_g_l, norm_g_l), pair(scale, c_scale), pair(shift, c_shift), pair(gate, c_gate)

    def rotary_qk(uq, uk):
        return _rope(uq, cos, sin, 1.0), _rope(uk, cos, sin, k_scale)

    def log_decay(logit):
        return jax.nn.log_sigmoid(logit.astype(F32))

    xa = jnp.concatenate([x[0], ctx[0]], axis=0)
    saved = []
    for l in range(depth):
        vecs, vecs_vjp = jax.vjp(mod_vectors, mod_lat[l], mod_ctx[l], norm_g[l])
        (h,) = norm_mod_fwd((xa,), vecs[:3])
        wl_in = landed(l, 0, h)
        u = _matmul(h, wl_in, out_dtype=BF16, name="in_proj_fwd")
        (qr, kr), rotary_vjp = jax.vjp(rotary_qk, u[:, off[RET_Q]:off[RET_Q + 1]], u[:, off[RET_K]:off[RET_K + 1]])
        bt, bt_vjp = jax.vjp(lambda r: _na_bias_table(r, rows), na_rpb[l])
        lam, lam_vjp = jax.vjp(log_decay, ret_decay_logit[l])
        o_na = _na_fwd_call(u, u, u, na_cols, w_na, bt, n_lat)
        o_f, st_f = _ret_fwd_call(qr, kr, u, off[RET_V], lam[0], n_lat, False)
        o_b, st_b = _ret_fwd_call(qr, kr, u, off[RET_V], lam[1], n_lat, True)
        (a_na,) = gate_na_fwd((o_na, u), ())
        a_ret = _gate_ret_fwd_call(o_f, o_b, u, off[RET_Z] // w_retv, tm_row)
        wl_pna, wl_pret, wl_out = landed(l, 1, a_na), landed(l, 2, a_na), landed(l, 3, a_na)
        y_na = _matmul(a_na, wl_pna, out_dtype=F32, name="proj_na_fwd")
        y_ret = _matmul(a_ret, wl_pret, out_dtype=F32, name="proj_ret_fwd")
        (merged,) = merge_fwd((u, u, y_na, y_ret), ())
        out = _matmul(merged, wl_out, out_dtype=F32, name="out_proj_fwd")
        (xa_next,) = residual_fwd((xa, out), vecs[3:])
        saved.append(dict(xa=xa, vecs=vecs, vecs_vjp=vecs_vjp, h=h, w=(wl_in, wl_pna, wl_pret, wl_out), u=u, qr=qr, kr=kr,
                          rotary_vjp=rotary_vjp, bt=bt, bt_vjp=bt_vjp, lam=lam, lam_vjp=lam_vjp, o_na=o_na, o_f=o_f,
                          o_b=o_b, st_f=st_f, st_b=st_b, a_na=a_na, a_ret=a_ret, y_na=y_na, y_ret=y_ret,
                          merged=merged, out=out))
        xa = xa_next

    fg_pair, fg_vjp = jax.vjp(lambda g: pair(g, g), final_g)
    x_last = xa[:n_lat]
    (loss_rows,) = loss_fwd((x_last, loss_target[0]), (fg_pair,))
    loss = lax.psum(jnp.sum(loss_rows), ("x", "y", "c"))
    (dx_last,), (d_fg_pair,) = loss_bwd((x_last, loss_target[0]), (fg_pair,), (jnp.ones_like(loss_rows),))
    (d_final_g,) = fg_vjp(d_fg_pair)
    dxa = jnp.pad(dx_last, ((0, n_ctx), (0, 0)))

    d_mod_lat, d_mod_ctx, d_norm_g, d_rpb, d_decay = ([None] * depth for _ in range(5))
    for l in reversed(range(depth)):
        s = saved[l]
        u, qr, kr = s["u"], s["qr"], s["kr"]
        wl_in, wl_pna, wl_pret, wl_out = s["w"]
        (dxa_res, d_out), (d_gate,) = residual_bwd((s["xa"], s["out"]), s["vecs"][3:], (dxa,))
        send_dw(l, 3, _matmul(s["merged"], d_out, trans_a=True, out_dtype=BF16, name="out_proj_dw"))
        d_merged = _matmul(d_out, wl_out, trans_b=True, out_dtype=BF16, name="out_proj_da")
        (dg_na, dg_ret, dy_na, dy_ret), _ = merge_bwd((u, u, s["y_na"], s["y_ret"]), (), (d_merged,))
        send_dw(l, 2, _matmul(s["a_ret"], dy_ret, trans_a=True, out_dtype=BF16, name="proj_ret_dw"))
        da_ret = _matmul(dy_ret, wl_pret, trans_b=True, out_dtype=BF16, name="proj_ret_da")
        tok = send_dw(l, 1, _matmul(s["a_na"], dy_na, trans_a=True, out_dtype=BF16, name="proj_na_dw"))
        da_na = _matmul(dy_na, wl_pna, trans_b=True, out_dtype=BF16, name="proj_na_da", after=tok)
        do_ret, dz_ret = _gate_ret_bwd_call(s["o_f"], s["o_b"], u, off[RET_Z] // w_retv, da_ret, tm_row)
        (do_na, dz_na), _ = gate_na_bwd((s["o_na"], u), (), (da_na,))
        dq_f, dk_f, dv_f, dl_f = _ret_bwd_call(qr, kr, u, off[RET_V], s["lam"][0], s["st_f"], do_ret, n_lat, False)
        dq_b, dk_b, dv_b, dl_b = _ret_bwd_call(qr, kr, u, off[RET_V], s["lam"][1], s["st_b"], do_ret, n_lat, True)
        dq, dk, dv, dbt = _na_bwd_call(u, u, u, na_cols, w_na, s["bt"], do_na, n_lat)
        d_uq, d_uk = s["rotary_vjp"]((dq_f + dq_b, dk_f + dk_b))
        du = jnp.concatenate([dq, dk.astype(BF16), dv.astype(BF16), dz_na, d_uq, d_uk, dv_f + dv_b, dz_ret,
                              dg_na, dg_ret], axis=1)
        (d_rpb[l],) = s["bt_vjp"](dbt)
        (d_decay[l],) = s["lam_vjp"](jnp.stack([dl_f[:, 0, 0], dl_b[:, 0, 0]]))
        tok = send_dw(l, 0, _matmul(s["h"], du, trans_a=True, out_dtype=BF16, name="in_proj_dw"))
        dh = _matmul(du, wl_in, trans_b=True, out_dtype=BF16, name="in_proj_da", after=tok)
        (dxa,), d_vecs = norm_mod_bwd((s["xa"],), s["vecs"][:3], (dh,), acc=(dxa_res,))
        d_mod_lat[l], d_mod_ctx[l], d_norm_g[l] = s["vecs_vjp"](tuple(d_vecs) + (d_gate,))
    gx = dxa[:n_lat]
    d_mod_lat, d_mod_ctx, d_norm_g, d_rpb, d_decay = (jnp.stack(a) for a in (d_mod_lat, d_mod_ctx, d_norm_g, d_rpb, d_decay))

    small_shapes = [d_mod_lat.shape, d_mod_ctx.shape, d_norm_g.shape, d_final_g.shape, d_rpb.shape, d_decay.shape]
    packed = _pack([d_mod_lat, d_mod_ctx, d_norm_g, d_final_g, d_rpb, d_decay])
    g_all, g_sum = _small_allgather(packed, "allgather_small_grads")
    dml_sum, dmc_sum, grad_norm_g, grad_final_g, grad_na_rpb, grad_decay = _unpack(g_sum, small_shapes)
    grad_ada_b = dml_sum + dmc_sum
    dml_all = g_all.reshape(N_DEV, -1)[:, :depth * 3 * d].reshape(N_DEV, depth, 3 * d)

    def my_cols(a):
        return lax.dynamic_slice_in_dim(a, my_idx * ncol, ncol, axis=a.ndim - 1)

    dmod = jnp.concatenate(
        [my_cols(dml_all).transpose(1, 0, 2), my_cols(dmc_sum)[:, None, :], jnp.zeros((depth, 7, ncol), F32)], axis=1)
    grad_ada_w, dcs_part = _ada_bwd_call(cin, ada_w, dmod)
    _, dcs = _small_allgather(dcs_part, "allgather_dcsilu")
    sg = jax.nn.sigmoid(c_ctx)
    grad_c_ctx = dcs[8] * (sg * (1.0 + c_ctx * (1.0 - sg)))

    def flat2(a):
        return a.reshape(a.shape[0] * a.shape[1], a.shape[2])

    small_w = [c_ctx, ada_b, norm_g, na_rpb, ret_decay_logit, final_g]
    small_g = [grad_c_ctx, grad_ada_b, grad_norm_g, grad_na_rpb, grad_decay, grad_final_g]
    small_m = [m_c_ctx, m_ada_b, m_norm_g, m_na_rpb, m_ret_decay_logit, m_final_g]
    small_v = [v_c_ctx, v_ada_b, v_norm_g, v_na_rpb, v_ret_decay_logit, v_final_g]
    shp = [a.shape for a in small_w]
    ds_, nms_, nvs_ = _adamw_dense(_pack(small_w), _pack(small_g), _pack(small_m), _pack(small_v), "adamw_small")
    ds_, nms_, nvs_ = _unpack(ds_, shp), _unpack(nms_, shp), _unpack(nvs_, shp)

    d_ada, nm_ada, nv_ada = [a.reshape(ada_w.shape) for a in _adamw_dense(
        flat2(ada_w), flat2(grad_ada_w), flat2(m_ada_w), flat2(v_ada_w), "adamw_ada_w")]

    w_all = (w_in, w_proj_na, w_proj_ret, w_out)
    m_all = (m_w_in, m_w_proj_na, m_w_proj_ret, m_w_out)
    v_all = (v_w_in, v_w_proj_na, v_w_proj_ret, v_w_out)
    upd = [None] * 4
    after = d_ada
    for flight in scatters:
        mine, slabs = _push_wait(flight["name"], flight["sems"], flight["srcs"], flight["lands"], flight["axes"],
                                 _scatter_views, PEER_BITS, (after,))
        for (l, a), own, s in zip(flight["keys"], mine, slabs):
            upd[a] = _adamw_sharded(w_all[a], m_all[a], v_all[a], own, s, w_axes[a], idx_arr, l, upd[a],
                                    "adamw_" + w_names[a])
            after = upd[a][1]
    (g_w_in, d_w_in, nm_w_in, nv_w_in), (g_pna, d_pna, nm_pna, nv_pna) = upd[0], upd[1]
    (g_pret, d_pret, nm_pret, nv_pret), (g_out, d_out, nm_out, nv_out) = upd[2], upd[3]

    def order(cc, aw, ab, ng, wi, rp, dl, pn, pr, wo, fg):
        return [cc, aw, ab, ng, wi, rp, dl, pn, pr, wo, fg]

    grads_out = order(grad_c_ctx, grad_ada_w, grad_ada_b, grad_norm_g, g_w_in, grad_na_rpb, grad_decay, g_pna, g_pret, g_out, grad_final_g)
    delta_out = order(ds_[0], d_ada, ds_[1], ds_[2], d_w_in, ds_[3], ds_[4], d_pna, d_pret, d_out, ds_[5])
    m_out = order(nms_[0], nm_ada, nms_[1], nms_[2], nm_w_in, nms_[3], nms_[4], nm_pna, nm_pret, nm_out, nms_[5])
    v_out = order(nvs_[0], nv_ada, nvs_[1], nvs_[2], nv_w_in, nvs_[3], nvs_[4], nv_pna, nv_pret, nv_out, nvs_[5])
    return (loss, gx[None], *grads_out, *delta_out, *m_out, *v_out)
```

```python
import functools

import numpy as np
import jax
import jax.numpy as jnp
from jax import lax
from jax.experimental import pallas as pl
from jax.experimental.pallas import tpu as pltpu

F32 = jnp.float32
BF16 = jnp.bfloat16

N_DEV = 8
GRID_W = 64
NA_HEAD_DIM = 128
NA_WIN_ROWS = 8
NA_WIN_COLS = 16
RET_KEY_DIM = 128
RET_VAL_DIM = 256
RET_CHUNK = 128
ROPE_BASE = 10000.0
NORM_EPS = 1e-6
MASK_VALUE = -1e30

ADAM_LR = 0.001
ADAM_B1 = 0.9
ADAM_B2 = 0.999
ADAM_EPS = 1e-08
ADAM_WD = 0.01
ADAM_STEP = 10

VMEM_LIMIT = 48 * 1024 * 1024
MESH = pl.DeviceIdType.MESH
ANY = pl.BlockSpec(memory_space=pl.ANY)
VMEM_SPEC = pl.BlockSpec(memory_space=pltpu.VMEM)


def _params(sem=None):
    return pltpu.CompilerParams(dimension_semantics=sem, vmem_limit_bytes=VMEM_LIMIT)


def _pick(n, prefs):
    for p in prefs:
        if n % p == 0:
            return p
    return n


def _dot(a, b):
    return lax.dot_general(a, b, (((1,), (0,)), ((), ())), preferred_element_type=F32)


def _dot_nt(a, b):
    return lax.dot_general(a, b, (((1,), (1,)), ((), ())), preferred_element_type=F32)


def _dot_tn(a, b):
    return lax.dot_general(a, b, (((0,), (0,)), ((), ())), preferred_element_type=F32)


def _silu(x):
    return x * jax.nn.sigmoid(x)


def _matmul(a, b, *, trans_a=False, trans_b=False, out_dtype=F32, name="matmul", after=None):
    if trans_a:
        kdim, m = a.shape
    else:
        m, kdim = a.shape
    if trans_b:
        n, kb = b.shape
    else:
        kb, n = b.shape
    assert kdim == kb, (a.shape, b.shape, trans_a, trans_b)
    tm = _pick(m, (1152, 1024, 768, 512, 256, 128))
    tn = _pick(n, (512, 256, 128))
    tk = _pick(kdim, (2304, 2048, 1024, 512, 256, 128))
    nk = kdim // tk
    dn = (((0 if trans_a else 1,), (1 if trans_b else 0,)), ((), ()))

    def body(a_ref, b_ref, *rest):
        o_ref, acc_ref = rest[-2:]
        part = lax.dot_general(a_ref[...], b_ref[...], dn, preferred_element_type=F32)
        if nk == 1:
            o_ref[...] = part.astype(o_ref.dtype)
        else:
            k = pl.program_id(2)

            @pl.when(k == 0)
            def _():
                acc_ref[...] = part

            @pl.when(k > 0)
            def _():
                acc_ref[...] += part

            @pl.when(k == nk - 1)
            def _():
                o_ref[...] = acc_ref[...].astype(o_ref.dtype)

    a_spec = pl.BlockSpec((tk, tm), lambda i, j, k: (k, i)) if trans_a else pl.BlockSpec((tm, tk), lambda i, j, k: (i, k))
    b_spec = pl.BlockSpec((tn, tk), lambda i, j, k: (j, k)) if trans_b else pl.BlockSpec((tk, tn), lambda i, j, k: (k, j))
    return pl.pallas_call(
        body,
        name=name,
        grid=(m // tm, n // tn, nk),
        in_specs=[a_spec, b_spec] + ([] if after is None else [ANY]),
        out_specs=pl.BlockSpec((tm, tn), lambda i, j, k: (i, j)),
        out_shape=jax.ShapeDtypeStruct((m, n), out_dtype),
        scratch_shapes=[pltpu.VMEM((tm, tn) if nk > 1 else (8, 128), F32)],
        compiler_params=_params(("parallel", "parallel", "arbitrary")),
    )(*((a, b) if after is None else (a, b, after)))


def _make_rowwise(f, name, out_dtypes, out_cols, n_lat, tm, diff_rows, col_blocks=None, drow_dtypes=None):
    drow_dtypes = drow_dtypes or {}

    def tile_fn(*args):
        return tuple(o.astype(dt) for o, dt in zip(f(*args), out_dtypes))

    def row_spec(k, arr):
        width, index = (col_blocks or {}).get(k, (arr.shape[1], 0))
        return pl.BlockSpec((tm, width), lambda i: (i, index))

    def row_width(k, arr):
        return (col_blocks or {}).get(k, (arr.shape[1], 0))[0]

    def fwd_call(rows, vecs):
        t = rows[0].shape[0]
        nr, nv = len(rows), len(vecs)
        nl = n_lat // tm

        def body(*refs):
            grp = (pl.program_id(0) >= nl).astype(jnp.int32)
            args = [r[...] for r in refs[:nr]] + [v[grp] for v in refs[nr:nr + nv]]
            for o_ref, o in zip(refs[nr + nv:], tile_fn(*args)):
                o_ref[...] = o

        return pl.pallas_call(
            body,
            name=name + "_fwd",
            grid=(t // tm,),
            in_specs=[row_spec(k, r) for k, r in enumerate(rows)]
            + [pl.BlockSpec(v.shape, lambda i: (0, 0, 0)) for v in vecs],
            out_specs=[pl.BlockSpec((tm, c), lambda i: (i, 0)) for c in out_cols],
            out_shape=[jax.ShapeDtypeStruct((t, c), dt) for c, dt in zip(out_cols, out_dtypes)],
            compiler_params=_params(("parallel",)),
        )(*rows, *vecs)

    def bwd_call(rows, vecs, gs, acc=None):
        t = rows[0].shape[0]
        nr, nv, ng = len(rows), len(vecs), len(gs)
        nl = n_lat // tm
        nd = len(diff_rows)
        acc = [None] * nd if acc is None else list(acc)
        acc_in = [a for a in acc if a is not None]

        def body(*refs):
            i = pl.program_id(0)
            grp = (i >= nl).astype(jnp.int32)
            args = [r[...] for r in refs[:nr]] + [v[grp] for v in refs[nr:nr + nv]]
            g_refs = refs[nr + nv:nr + nv + ng]
            acc_refs = list(refs[nr + nv + ng:nr + nv + ng + len(acc_in)])
            drow_refs = refs[nr + nv + ng + len(acc_in):nr + nv + ng + len(acc_in) + nd]
            dvec_refs = refs[nr + nv + ng + len(acc_in) + nd:]
            _, vjp = jax.vjp(tile_fn, *args)
            grads = vjp(tuple(g[...] for g in g_refs))
            for d_ref, k, a in zip(drow_refs, diff_rows, acc):
                gk = grads[k] if a is None else grads[k] + acc_refs.pop(0)[...]
                d_ref[...] = gk.astype(d_ref.dtype)

            @pl.when(i == 0)
            def _():
                for d_ref in dvec_refs:
                    d_ref[...] = jnp.zeros_like(d_ref)

            for j, d_ref in enumerate(dvec_refs):
                d_ref[grp] += grads[nr + j]

        outs = pl.pallas_call(
            body,
            name=name + "_bwd",
            grid=(t // tm,),
            in_specs=[row_spec(k, r) for k, r in enumerate(rows)]
            + [pl.BlockSpec(v.shape, lambda i: (0, 0, 0)) for v in vecs]
            + [pl.BlockSpec((tm, g.shape[1]), lambda i: (i, 0)) for g in gs]
            + [pl.BlockSpec((tm, a.shape[1]), lambda i: (i, 0)) for a in acc_in],
            out_specs=[pl.BlockSpec((tm, row_width(k, rows[k])), lambda i: (i, 0)) for k in diff_rows]
            + [pl.BlockSpec(v.shape, lambda i: (0, 0, 0)) for v in vecs],
            out_shape=[jax.ShapeDtypeStruct((t, row_width(k, rows[k])), drow_dtypes.get(k, rows[k].dtype))
                       for k in diff_rows]
            + [jax.ShapeDtypeStruct(v.shape, F32) for v in vecs],
            compiler_params=_params(("arbitrary",)),
        )(*rows, *vecs, *gs, *acc_in)
        return outs[:nd], outs[nd:]

    return fwd_call, bwd_call


def _f_norm_mod(x, g, scale, shift):
    r = lax.rsqrt(jnp.mean(x * x, axis=-1, keepdims=True) + NORM_EPS)
    return ((x * r * g) * (1.0 + scale) + shift,)


def _f_gate_na(o, z):
    return (o.astype(F32) * _silu(z.astype(F32)),)


def _f_merge(g_na, g_ret, y_na, y_ret):
    return (jax.nn.sigmoid(g_na.astype(F32)) * y_na + jax.nn.sigmoid(g_ret.astype(F32)) * y_ret,)


def _f_residual(x, out, gate):
    return (x + gate * out,)


def _f_loss(x, target, g):
    r = lax.rsqrt(jnp.mean(x * x, axis=-1, keepdims=True) + NORM_EPS)
    y = x * r * g
    e = 0.5 * jnp.mean(jnp.square(y - target), axis=-1, keepdims=True)
    return (jnp.broadcast_to(e * (1.0 / 128.0), (x.shape[0], 128)),)


def _gate_ret_fwd_call(of, ob, z, zblk, tm):
    t, w = of.shape
    nh = w // RET_VAL_DIM

    def body(of_ref, ob_ref, z_ref, a_ref):
        for hh in range(nh):
            sl = slice(hh * RET_VAL_DIM, (hh + 1) * RET_VAL_DIM)
            o = of_ref[:, sl] + ob_ref[:, sl]
            r = lax.rsqrt(jnp.mean(o * o, axis=-1, keepdims=True) + NORM_EPS)
            a_ref[:, sl] = ((o * r) * _silu(z_ref[:, sl].astype(F32))).astype(a_ref.dtype)

    spec = pl.BlockSpec((tm, w), lambda i: (i, 0))
    zspec = pl.BlockSpec((tm, w), lambda i: (i, zblk))
    return pl.pallas_call(
        body, name="gate_ret_fwd", grid=(t // tm,), in_specs=[spec, spec, zspec], out_specs=spec,
        out_shape=jax.ShapeDtypeStruct((t, w), BF16), compiler_params=_params(("parallel",)),
    )(of, ob, z)


def _gate_ret_bwd_call(of, ob, z, zblk, da, tm):
    t, w = of.shape
    nh = w // RET_VAL_DIM

    def body(of_ref, ob_ref, z_ref, da_ref, do_ref, dz_ref):
        for hh in range(nh):
            sl = slice(hh * RET_VAL_DIM, (hh + 1) * RET_VAL_DIM)
            o = of_ref[:, sl] + ob_ref[:, sl]
            r = lax.rsqrt(jnp.mean(o * o, axis=-1, keepdims=True) + NORM_EPS)
            n = o * r
            zf = z_ref[:, sl].astype(F32)
            sg = jax.nn.sigmoid(zf)
            g = da_ref[:, sl].astype(F32)
            dn = g * (zf * sg)
            dz_ref[:, sl] = (g * n * (sg * (1.0 + zf * (1.0 - sg)))).astype(dz_ref.dtype)
            do_ref[:, sl] = r * (dn - n * jnp.mean(dn * n, axis=-1, keepdims=True))

    spec = pl.BlockSpec((tm, w), lambda i: (i, 0))
    zspec = pl.BlockSpec((tm, w), lambda i: (i, zblk))
    return pl.pallas_call(
        body, name="gate_ret_bwd", grid=(t // tm,), in_specs=[spec, spec, zspec, spec], out_specs=[spec, spec],
        out_shape=[jax.ShapeDtypeStruct((t, w), F32), jax.ShapeDtypeStruct((t, w), z.dtype)],
        compiler_params=_params(("parallel",)),
    )(of, ob, z, da)


def _na_geometry(t, n_lat):
    rows = n_lat // GRID_W
    kh = min(NA_WIN_ROWS, rows)
    return rows, kh, kh * GRID_W, t - n_lat, t // GRID_W


def _na_row0(r, rows, kh):
    return jnp.clip(r - kh // 2, 0, rows - kh)


def _na_bias_idx(r, rows, kh):
    return jnp.clip(_na_row0(r, rows, kh) - r + (NA_WIN_ROWS - 1), 0, NA_WIN_ROWS - 1)


def _na_group(rows, n_ctx):
    for g in (4, 2):
        if rows % g == 0 and (n_ctx // GRID_W) % g == 0:
            return g
    return 1


def _na_bias_spec(i, grp, rows, kh, n_loc):
    return pl.BlockSpec((1, 1, GRID_W, n_loc),
                        lambda h, g: (h, _na_bias_idx(jnp.minimum(g * grp + i, rows - grp + i), rows, kh), 0, 0))


def _na_fwd_call(q, k, v, col0, w, bt, n_lat):
    t = q.shape[0]
    nh = w // NA_HEAD_DIM
    rows, kh, n_loc, n_ctx, nq = _na_geometry(t, n_lat)
    grp = _na_group(rows, n_ctx)
    scale = NA_HEAD_DIM ** -0.5

    def body(q_ref, k_ref, v_ref, *rest):
        bt_refs, o_ref = rest[:grp], rest[grp]
        g = pl.program_id(1)
        kc = k_ref[pl.ds(n_lat, n_ctx), :]
        vc = v_ref[pl.ds(n_lat, n_ctx), :]

        @pl.when(g < rows // grp)
        def _():
            for i in range(grp):
                r = g * grp + i
                qb = q_ref[i * GRID_W:(i + 1) * GRID_W, :]
                s_ctx = _dot_nt(qb, kc) * scale
                start = pl.multiple_of(_na_row0(r, rows, kh) * GRID_W, GRID_W)
                kw = k_ref[pl.ds(start, n_loc), :]
                vw = v_ref[pl.ds(start, n_loc), :]
                s_loc = _dot_nt(qb, kw) * scale + bt_refs[i][0, 0]
                m = jnp.maximum(jnp.max(s_loc, axis=-1, keepdims=True), jnp.max(s_ctx, axis=-1, keepdims=True))
                p_loc = jnp.exp(s_loc - m)
                p_ctx = jnp.exp(s_ctx - m)
                l = jnp.sum(p_loc, axis=-1, keepdims=True) + jnp.sum(p_ctx, axis=-1, keepdims=True)
                o = _dot(p_loc.astype(BF16), vw) + _dot(p_ctx.astype(BF16), vc)
                o_ref[i * GRID_W:(i + 1) * GRID_W, :] = (o / l).astype(o_ref.dtype)

        @pl.when(g >= rows // grp)
        def _():
            s_ctx = _dot_nt(q_ref[...], kc) * scale
            m = jnp.max(s_ctx, axis=-1, keepdims=True)
            p = jnp.exp(s_ctx - m)
            l = jnp.sum(p, axis=-1, keepdims=True)
            o_ref[...] = (_dot(p.astype(BF16), vc) / l).astype(o_ref.dtype)

    qspec = pl.BlockSpec((grp * GRID_W, NA_HEAD_DIM), lambda h, g: (g, h))
    in_q = pl.BlockSpec((grp * GRID_W, NA_HEAD_DIM), lambda h, g: (g, col0[0] + h))
    in_k = pl.BlockSpec((t, NA_HEAD_DIM), lambda h, g: (0, col0[1] + h))
    in_v = pl.BlockSpec((t, NA_HEAD_DIM), lambda h, g: (0, col0[2] + h))
    return pl.pallas_call(
        body,
        name="na_attn_fwd",
        grid=(nh, nq // grp),
        in_specs=[in_q, in_k, in_v] + [_na_bias_spec(i, grp, rows, kh, n_loc) for i in range(grp)],
        out_specs=qspec,
        out_shape=jax.ShapeDtypeStruct((t, w), BF16),
        compiler_params=_params(("parallel", "arbitrary")),
    )(q, k, v, *([bt] * grp))


def _na_bwd_call(q, k, v, col0, w, bt, do, n_lat):
    t = q.shape[0]
    nh = w // NA_HEAD_DIM
    rows, kh, n_loc, n_ctx, nq = _na_geometry(t, n_lat)
    scale = NA_HEAD_DIM ** -0.5

    grp = _na_group(rows, n_ctx)

    def body(q_ref, k_ref, v_ref, do_ref, *rest):
        bt_refs = rest[:grp]
        dq_ref, dk_ref, dv_ref = rest[2 * grp:2 * grp + 3]
        dbt_refs = rest[2 * grp + 3:]
        g = pl.program_id(1)

        @pl.when(g == 0)
        def _():
            dk_ref[...] = jnp.zeros_like(dk_ref)
            dv_ref[...] = jnp.zeros_like(dv_ref)

        kc = k_ref[pl.ds(n_lat, n_ctx), :]
        vc = v_ref[pl.ds(n_lat, n_ctx), :]

        @pl.when(g < rows // grp)
        def _():
            for i in range(grp):
                r = g * grp + i
                sl = slice(i * GRID_W, (i + 1) * GRID_W)
                qb = q_ref[sl, :]
                dob = do_ref[sl, :]
                s_ctx = _dot_nt(qb, kc) * scale
                dp_ctx = _dot_nt(dob, vc)
                start = pl.multiple_of(_na_row0(r, rows, kh) * GRID_W, GRID_W)
                kw = k_ref[pl.ds(start, n_loc), :]
                vw = v_ref[pl.ds(start, n_loc), :]
                s_loc = _dot_nt(qb, kw) * scale + bt_refs[i][0, 0]
                m = jnp.maximum(jnp.max(s_loc, axis=-1, keepdims=True), jnp.max(s_ctx, axis=-1, keepdims=True))
                p_loc = jnp.exp(s_loc - m)
                p_ctx = jnp.exp(s_ctx - m)
                inv = 1.0 / (jnp.sum(p_loc, axis=-1, keepdims=True) + jnp.sum(p_ctx, axis=-1, keepdims=True))
                p_loc = p_loc * inv
                p_ctx = p_ctx * inv
                dp_loc = _dot_nt(dob, vw)
                delta = (jnp.sum(p_loc * dp_loc, axis=-1, keepdims=True)
                         + jnp.sum(p_ctx * dp_ctx, axis=-1, keepdims=True))
                ds_loc = p_loc * (dp_loc - delta)
                ds_ctx = p_ctx * (dp_ctx - delta)
                first = jnp.logical_or(g == 0, _na_bias_idx(r, rows, kh) != _na_bias_idx(r - grp, rows, kh))
                dbt_ref = dbt_refs[i]

                @pl.when(first)
                def _():
                    dbt_ref[0, 0] = ds_loc

                @pl.when(jnp.logical_not(first))
                def _():
                    dbt_ref[0, 0] += ds_loc

                dsl = (ds_loc * scale).astype(BF16)
                dsc = (ds_ctx * scale).astype(BF16)
                dq_ref[sl, :] = (_dot(dsl, kw) + _dot(dsc, kc)).astype(dq_ref.dtype)
                dk_ref[pl.ds(start, n_loc), :] += _dot_tn(dsl, qb)
                dv_ref[pl.ds(start, n_loc), :] += _dot_tn(p_loc.astype(BF16), dob)
                dk_ref[pl.ds(n_lat, n_ctx), :] += _dot_tn(dsc, qb)
                dv_ref[pl.ds(n_lat, n_ctx), :] += _dot_tn(p_ctx.astype(BF16), dob)

        @pl.when(g >= rows // grp)
        def _():
            qb = q_ref[...]
            dob = do_ref[...]
            s_ctx = _dot_nt(qb, kc) * scale
            dp_ctx = _dot_nt(dob, vc)
            m = jnp.max(s_ctx, axis=-1, keepdims=True)
            p = jnp.exp(s_ctx - m)
            p = p * (1.0 / jnp.sum(p, axis=-1, keepdims=True))
            delta = jnp.sum(p * dp_ctx, axis=-1, keepdims=True)
            dsc = (p * (dp_ctx - delta) * scale).astype(BF16)
            dq_ref[...] = _dot(dsc, kc).astype(dq_ref.dtype)
            dk_ref[pl.ds(n_lat, n_ctx), :] += _dot_tn(dsc, qb)
            dv_ref[pl.ds(n_lat, n_ctx), :] += _dot_tn(p.astype(BF16), dob)

    qspec = pl.BlockSpec((grp * GRID_W, NA_HEAD_DIM), lambda h, g: (g, h))
    kspec = pl.BlockSpec((t, NA_HEAD_DIM), lambda h, g: (0, h))
    bspecs = [_na_bias_spec(i, grp, rows, kh, n_loc) for i in range(grp)]
    zeros = [jnp.zeros(bt.shape, F32) for _ in range(grp)]
    outs = pl.pallas_call(
        body,
        name="na_attn_bwd",
        grid=(nh, nq // grp),
        in_specs=[pl.BlockSpec((grp * GRID_W, NA_HEAD_DIM), lambda h, g: (g, col0[0] + h)),
                  pl.BlockSpec((t, NA_HEAD_DIM), lambda h, g: (0, col0[1] + h)),
                  pl.BlockSpec((t, NA_HEAD_DIM), lambda h, g: (0, col0[2] + h)), qspec] + bspecs + [ANY] * grp,
        out_specs=[qspec, kspec, kspec] + bspecs,
        out_shape=[
            jax.ShapeDtypeStruct((t, w), BF16),
            jax.ShapeDtypeStruct((t, w), F32),
            jax.ShapeDtypeStruct((t, w), F32),
        ] + [jax.ShapeDtypeStruct(bt.shape, F32)] * grp,
        input_output_aliases={4 + grp + i: 3 + i for i in range(grp)},
        compiler_params=_params(("parallel", "arbitrary")),
    )(q, k, v, do, *([bt] * grp), *zeros)
    dbt = outs[3]
    for extra in outs[4:]:
        dbt = dbt + extra
    return outs[0], outs[1], outs[2], dbt


def _na_bias_table(rpb, rows):
    kh = min(NA_WIN_ROWS, rows)
    nj = NA_WIN_ROWS
    e1 = np.zeros((nj, kh, 2 * NA_WIN_ROWS - 1), np.float32)
    for j in range(nj):
        for kk in range(kh):
            if j + kk < 2 * NA_WIN_ROWS - 1:
                e1[j, kk, j + kk] = 1.0
    cidx = np.arange(GRID_W)
    dc = np.clip(cidx[None, :] - cidx[:, None] + (NA_WIN_COLS - 1), 0, 2 * NA_WIN_COLS - 2)
    e2 = np.zeros((GRID_W, GRID_W, 2 * NA_WIN_COLS - 1), np.float32)
    e2[np.arange(GRID_W)[:, None], np.arange(GRID_W)[None, :], dc] = 1.0
    c0 = np.clip(cidx - NA_WIN_COLS // 2, 0, GRID_W - NA_WIN_COLS)
    col_in = (cidx[None, :] >= c0[:, None]) & (cidx[None, :] < c0[:, None] + NA_WIN_COLS)
    t1 = jnp.einsum("hab,jka->hjkb", rpb, jnp.asarray(e1), precision=lax.Precision.HIGHEST)
    b = jnp.einsum("hjkb,cwb->hjckw", t1, jnp.asarray(e2), precision=lax.Precision.HIGHEST)
    b = jnp.where(jnp.asarray(col_in)[None, None, :, None, :], b, MASK_VALUE)
    return b.reshape(rpb.shape[0], nj, GRID_W, kh * GRID_W)


def _ret_decays(lam_s, reverse):
    c = RET_CHUNK
    ii = lax.broadcasted_iota(jnp.int32, (c, c), 0)
    jj = lax.broadcasted_iota(jnp.int32, (c, c), 1)
    d = (jj - ii) if reverse else (ii - jj)
    dpos = jnp.maximum(d.astype(F32), 0.0)
    mask = jnp.where(d >= 0, jnp.exp(dpos * lam_s), 0.0)
    pi = lax.broadcasted_iota(jnp.int32, (c, 1), 0).astype(F32)
    qpos = (c - pi) if reverse else (pi + 1.0)
    kpos = pi if reverse else (c - 1.0 - pi)
    qd = jnp.exp(qpos * lam_s)
    kd = jnp.exp(kpos * lam_s)
    g = jnp.exp(jnp.full((1, RET_VAL_DIM), c * lam_s, F32))
    return mask, dpos, qd, kd, qpos, kpos, g


def _ret_head_group(nh):
    return _pick(nh, (4, 2))


def _ret_chunk_of(t, nt, nl, reverse):
    return (nt - 1 - t) if reverse else (t + nl) % nt


def _ret_fwd_call(qr, kr, v, vcol, lam, n_lat, reverse):
    t = qr.shape[0]
    nh = qr.shape[1] // RET_KEY_DIM
    c = RET_CHUNK
    nt, nl = t // c, n_lat // c

    hg = _ret_head_group(nh)
    dk, dv = RET_KEY_DIM, RET_VAL_DIM

    def body(lam_ref, q_ref, k_ref, v_ref, o_ref, s_ref, state):
        hb, step = pl.program_id(0), pl.program_id(1)

        @pl.when(step == 0)
        def _():
            state[...] = jnp.zeros_like(state)

        for j in range(hg):
            mask, _, qd, kd, _, _, g = _ret_decays(lam_ref[hb * hg + j], reverse)
            q, k, vv = q_ref[:, j * dk:(j + 1) * dk], k_ref[:, j * dk:(j + 1) * dk], v_ref[:, j * dv:(j + 1) * dv]
            p = _dot_nt(q, k) * mask
            s = state[j]
            qs = (q.astype(F32) * qd).astype(BF16)
            o_ref[:, j * dv:(j + 1) * dv] = _dot(p.astype(BF16), vv) + _dot(qs, s.astype(BF16))
            s_ref[j, 0] = s
            ks = (k.astype(F32) * kd).astype(BF16)
            state[j] = s * g + _dot_tn(ks, vv)

    def cmap(hb, step, lam_ref):
        return (_ret_chunk_of(step, nt, nl, reverse), hb)

    def vmap(hb, step, lam_ref):
        return (_ret_chunk_of(step, nt, nl, reverse), vcol // (hg * dv) + hb)

    return pl.pallas_call(
        body,
        name="retention_rev_fwd" if reverse else "retention_fwd",
        grid_spec=pltpu.PrefetchScalarGridSpec(
            num_scalar_prefetch=1,
            grid=(nh // hg, nt),
            in_specs=[
                pl.BlockSpec((c, hg * dk), cmap),
                pl.BlockSpec((c, hg * dk), cmap),
                pl.BlockSpec((c, hg * dv), vmap),
            ],
            out_specs=[
                pl.BlockSpec((c, hg * dv), cmap),
                pl.BlockSpec((hg, 1, dk, dv), lambda hb, step, lam_ref: (hb, step, 0, 0)),
            ],
            scratch_shapes=[pltpu.VMEM((hg, dk, dv), F32)],
        ),
        out_shape=[
            jax.ShapeDtypeStruct((t, nh * RET_VAL_DIM), F32),
            jax.ShapeDtypeStruct((nh, nt, RET_KEY_DIM, RET_VAL_DIM), F32),
        ],
        compiler_params=_params(("parallel", "arbitrary")),
    )(lam, qr, kr, v)


def _ret_bwd_call(qr, kr, v, vcol, lam, states, do, n_lat, reverse):
    t = qr.shape[0]
    nh = qr.shape[1] // RET_KEY_DIM
    c = RET_CHUNK
    nt, nl = t // c, n_lat // c

    hg = _ret_head_group(nh)
    dk, dv = RET_KEY_DIM, RET_VAL_DIM

    def body(lam_ref, q_ref, k_ref, v_ref, s_ref, do_ref, dq_ref, dk_ref, dv_ref, dl_ref, dstate):
        hb, rstep = pl.program_id(0), pl.program_id(1)

        @pl.when(rstep == 0)
        def _():
            dstate[...] = jnp.zeros_like(dstate)
            dl_ref[...] = jnp.zeros_like(dl_ref)

        for j in range(hg):
            mask, dpos, qd, kd, qpos, kpos, g = _ret_decays(lam_ref[hb * hg + j], reverse)
            ksl, vsl = slice(j * dk, (j + 1) * dk), slice(j * dv, (j + 1) * dv)
            q, k, vv = q_ref[:, ksl], k_ref[:, ksl], v_ref[:, vsl]
            qf, kf = q.astype(F32), k.astype(F32)
            s = s_ref[j, 0]
            ds = dstate[j]
            dob = do_ref[:, vsl].astype(BF16)
            sb, dsb = s.astype(BF16), ds.astype(BF16)
            a = _dot_nt(q, k)
            p = a * mask
            dp = _dot_nt(dob, vv)
            da = dp * mask
            dab = da.astype(BF16)
            dqc = _dot_nt(dob, sb)
            dkc = _dot_nt(vv, dsb)
            qs = (qf * qd).astype(BF16)
            ks = (kf * kd).astype(BF16)
            dq_ref[:, ksl] = (_dot(dab, k) + dqc * qd).astype(dq_ref.dtype)
            dk_ref[:, ksl] = (_dot_tn(dab, q) + dkc * kd).astype(dk_ref.dtype)
            dv_ref[:, vsl] = (_dot_tn(p.astype(BF16), dob) + _dot(ks, dsb)).astype(dv_ref.dtype)
            terms = (
                jnp.sum(jnp.sum(da * a * dpos, axis=1, keepdims=True), axis=0, keepdims=True)
                + jnp.sum(jnp.sum(dqc * qf * (qd * qpos), axis=1, keepdims=True), axis=0, keepdims=True)
                + jnp.sum(jnp.sum(dkc * kf * (kd * kpos), axis=1, keepdims=True), axis=0, keepdims=True)
                + jnp.sum(jnp.sum(ds * s * (g * c), axis=1, keepdims=True), axis=0, keepdims=True)
            )
            dl_ref[j] += jnp.broadcast_to(terms, (8, 128))
            dstate[j] = ds * g + _dot_tn(qs, dob)

    def cmap(hb, rstep, lam_ref):
        return (_ret_chunk_of(nt - 1 - rstep, nt, nl, reverse), hb)

    def vmap(hb, rstep, lam_ref):
        return (_ret_chunk_of(nt - 1 - rstep, nt, nl, reverse), vcol // (hg * dv) + hb)

    return pl.pallas_call(
        body,
        name="retention_rev_bwd" if reverse else "retention_bwd",
        grid_spec=pltpu.PrefetchScalarGridSpec(
            num_scalar_prefetch=1,
            grid=(nh // hg, nt),
            in_specs=[
                pl.BlockSpec((c, hg * dk), cmap),
                pl.BlockSpec((c, hg * dk), cmap),
                pl.BlockSpec((c, hg * dv), vmap),
                pl.BlockSpec((hg, 1, dk, dv), lambda hb, rstep, lam_ref: (hb, nt - 1 - rstep, 0, 0)),
                pl.BlockSpec((c, hg * dv), cmap),
            ],
            out_specs=[
                pl.BlockSpec((c, hg * dk), cmap),
                pl.BlockSpec((c, hg * dk), cmap),
                pl.BlockSpec((c, hg * dv), cmap),
                pl.BlockSpec((hg, 8, 128), lambda hb, rstep, lam_ref: (hb, 0, 0)),
            ],
            scratch_shapes=[pltpu.VMEM((hg, dk, dv), F32)],
        ),
        out_shape=[
            jax.ShapeDtypeStruct(qr.shape, qr.dtype),
            jax.ShapeDtypeStruct(kr.shape, kr.dtype),
            jax.ShapeDtypeStruct((t, nh * dv), v.dtype),
            jax.ShapeDtypeStruct((nh, 8, 128), F32),
        ],
        compiler_params=_params(("parallel", "arbitrary")),
    )(lam, qr, kr, v, states, do)


def _rope_tables(t, n_lat):
    nf = RET_KEY_DIM // 4
    tok = np.arange(n_lat)
    inv_freq = (ROPE_BASE ** (-np.arange(nf, dtype=np.float32) / nf)).astype(np.float32)
    row = (tok // GRID_W).astype(np.float32)
    col = (tok % GRID_W).astype(np.float32)
    ang = np.concatenate([row[:, None] * inv_freq, col[:, None] * inv_freq], axis=-1).astype(np.float32)
    cos = np.ones((t, 2 * nf), np.float32)
    sin = np.zeros((t, 2 * nf), np.float32)
    cos[:n_lat] = np.cos(ang)
    sin[:n_lat] = np.sin(ang)
    return jnp.asarray(cos), jnp.asarray(sin)


def _rope(xb, cos, sin, mult):
    t, w = xb.shape
    nh = w // RET_KEY_DIM
    half = RET_KEY_DIM // 2
    x = xb.astype(F32).reshape(t, nh, 2, half)
    x1, x2 = x[:, :, 0], x[:, :, 1]
    c, s = cos[:, None, :], sin[:, None, :]
    out = jnp.stack([x1 * c - x2 * s, x2 * c + x1 * s], axis=2) * mult
    return out.reshape(t, w).astype(BF16)


def _my_position():
    return lax.axis_index("x"), lax.axis_index("y"), lax.axis_index("c")


def _flip(pos, k):
    x, y, c = pos
    return (1 - x if k & 4 else x, 1 - y if k & 2 else y, 1 - c if k & 1 else c)


def _linear(pos):
    return 4 * pos[0] + 2 * pos[1] + pos[2]


def _slab(ref, axis, idx, size):
    start = pl.multiple_of(idx * size, size)
    return ref.at[pl.ds(start, size), :] if axis == 0 else ref.at[:, pl.ds(start, size)]


HBM_SPEC = pl.BlockSpec(memory_space=pltpu.HBM)
SEM_SPEC = pl.BlockSpec(memory_space=pltpu.SEMAPHORE)
DATAFLOW = pltpu.SideEffectType.DATAFLOW_SIDE_EFFECTING
PEER_BITS = (1, 2, 4, 6, 3, 5, 7)
GATHER_BITS = (1, 2, 4, 6)


def _in_hbm(a):
    return pltpu.with_memory_space_constraint(a, pltpu.HBM)


def _gather_views(me, k, a, src_refs, land_refs, axes):
    size = src_refs[a].shape[axes[a]]
    peer = _flip(me, k)
    return src_refs[a], _slab(land_refs[a], axes[a], _linear(me), size), _slab(land_refs[a], axes[a], _linear(peer), size)


def _scatter_views(me, k, a, src_refs, land_refs, axes):
    size = land_refs[a].shape[1 + axes[a]]
    peer = _flip(me, k)
    return _slab(src_refs[a], axes[a], _linear(peer), size), land_refs[a].at[k - 1], land_refs[a].at[k - 1]


CHIP_BITS = (0, 2, 4, 6)


def _chip_views(me, k, a, src_refs, land_refs, axes):
    j = CHIP_BITS.index(k)
    return src_refs[a].at[j], land_refs[a].at[j - 1], land_refs[a].at[j - 1]


def _pair_exchange(grads, axes, sizes):
    ns = len(grads)

    def slab_shape(a):
        s = grads[a].shape
        return (sizes[a], s[1]) if axes[a] == 0 else (s[0], sizes[a])

    def body(*refs):
        g_refs, p_refs = refs[:ns], refs[ns:2 * ns]
        send_sems, recv_sems = refs[2 * ns:]
        me = _my_position()
        sibling = _flip(me, 1)
        copies = []
        for j, kc in enumerate(CHIP_BITS):
            for a in range(ns):
                cp = pltpu.make_async_remote_copy(
                    src_ref=_slab(g_refs[a], axes[a], _linear(_flip(me, kc | 1)), sizes[a]), dst_ref=p_refs[a].at[j],
                    send_sem=send_sems.at[4 * a + j], recv_sem=recv_sems.at[4 * a + j],
                    device_id=sibling, device_id_type=MESH)
                cp.start()
                copies.append(cp)
        for cp in copies:
            cp.wait_recv()
        for cp in copies:
            cp.wait_send()

    return pl.pallas_call(
        body, name="scatter_pair_exchange", in_specs=[ANY] * ns, out_specs=[ANY] * ns,
        out_shape=[jax.ShapeDtypeStruct((4,) + slab_shape(a), grads[a].dtype) for a in range(ns)],
        scratch_shapes=[pltpu.SemaphoreType.DMA((4 * ns,)), pltpu.SemaphoreType.DMA((4 * ns,))],
        compiler_params=pltpu.CompilerParams(has_side_effects=True),
    )(*grads)


def _pair_add(grad, theirs, axis, chip_idx):
    _, r, c = theirs.shape
    tm = _pick(r, (256, 128, 64, 32, 16))
    if axis == 0:
        mine_spec = pl.BlockSpec((tm, c), lambda j, i, idx: (idx[j] * (r // tm) + i, 0))
    else:
        mine_spec = pl.BlockSpec((tm, c), lambda j, i, idx: (i, idx[j]))

    def body(idx_ref, mine_ref, theirs_ref, o_ref):
        o_ref[0] = (mine_ref[...].astype(F32) + theirs_ref[0].astype(F32)).astype(o_ref.dtype)

    spec = pl.BlockSpec((1, tm, c), lambda j, i, idx: (j, i, 0))
    return pl.pallas_call(
        body, name="scatter_pair_add",
        grid_spec=pltpu.PrefetchScalarGridSpec(
            num_scalar_prefetch=1, grid=(4, r // tm), in_specs=[mine_spec, spec], out_specs=spec),
        out_shape=jax.ShapeDtypeStruct(theirs.shape, theirs.dtype),
        compiler_params=_params(("parallel", "parallel")),
    )(chip_idx, grad, theirs)


def _slab_block(rows, cols, tm, axis):
    if axis == 0:
        return pl.BlockSpec((tm, cols), lambda i, idx: (idx[0] * (rows // tm) + i, 0))
    return pl.BlockSpec((tm, cols), lambda i, idx: (i, idx[0]))


def _place_shard(shard, land, axis, my_idx):
    r, c = shard.shape
    tm = _pick(r, (512, 256, 128, 64, 32, 16))

    def body(idx_ref, s_ref, land_ref, o_ref):
        o_ref[...] = s_ref[...]

    return pl.pallas_call(
        body, name="gather_place",
        grid_spec=pltpu.PrefetchScalarGridSpec(
            num_scalar_prefetch=1, grid=(r // tm,),
            in_specs=[pl.BlockSpec((tm, c), lambda i, idx: (i, 0)), ANY],
            out_specs=_slab_block(r, c, tm, axis)),
        out_shape=jax.ShapeDtypeStruct(land.shape, land.dtype),
        input_output_aliases={2: 0},
        compiler_params=_params(("parallel",)),
    )(my_idx, shard, land)


def _push_start(name, srcs, lands, axes, views, bits, deps):
    ns = len(srcs)

    def body(*refs):
        src_refs, land_refs = refs[:ns], refs[ns:2 * ns]
        send_sems, recv_sems = refs[2 * ns + len(deps):2 * ns + len(deps) + 2]
        token = refs[-1]
        me = _my_position()
        for k in bits:
            for a in range(ns):
                s, d, _ = views(me, k, a, src_refs, land_refs, axes)
                pltpu.make_async_remote_copy(
                    src_ref=s, dst_ref=d, send_sem=send_sems.at[7 * a + k - 1], recv_sem=recv_sems.at[7 * a + k - 1],
                    device_id=_flip(me, k), device_id_type=MESH).start()
        token[...] = jnp.zeros_like(token)

    thru = [pltpu.HBM(a.shape, a.dtype) for a in list(srcs) + list(lands)]
    outs = pl.pallas_call(
        body, name=name,
        in_specs=[HBM_SPEC] * (2 * ns) + [ANY] * len(deps),
        out_specs=[SEM_SPEC, SEM_SPEC] + [HBM_SPEC] * (2 * ns) + [VMEM_SPEC],
        out_shape=[pltpu.SemaphoreType.DMA((7 * ns,)), pltpu.SemaphoreType.DMA((7 * ns,))] + thru
        + [jax.ShapeDtypeStruct((8, 128), F32)],
        input_output_aliases={i: 2 + i for i in range(2 * ns)},
        compiler_params=pltpu.CompilerParams(has_side_effects=DATAFLOW),
    )(*[_in_hbm(a) for a in srcs], *[_in_hbm(a) for a in lands], *deps)
    return (outs[0], outs[1]), outs[2:2 + ns], outs[2 + ns:2 + 2 * ns], outs[-1]


def _gather_finish(lands, axes, sizes):
    ns = len(lands)
    chips = (2, 4, 6)

    def body(*refs):
        land_refs = refs[ns:2 * ns]
        send_sems, recv_sems = refs[2 * ns:]
        me = _my_position()
        sibling = _flip(me, 1)
        copies = []
        for j, kc in enumerate(chips):
            for a in range(ns):
                def slab_of(pos):
                    return _slab(land_refs[a], axes[a], _linear(pos), sizes[a])
                send = pltpu.make_async_remote_copy(
                    src_ref=slab_of(_flip(me, kc)), dst_ref=slab_of(_flip(me, kc)), send_sem=send_sems.at[3 * a + j],
                    recv_sem=recv_sems.at[3 * a + j], device_id=sibling, device_id_type=MESH)
                recv = pltpu.make_async_remote_copy(
                    src_ref=slab_of(_flip(me, kc)), dst_ref=slab_of(_flip(sibling, kc)), send_sem=send_sems.at[3 * a + j],
                    recv_sem=recv_sems.at[3 * a + j], device_id=sibling, device_id_type=MESH)
                send.start()
                copies.append((send, recv))
        for send, recv in copies:
            recv.wait_recv()
        for send, recv in copies:
            send.wait_send()

    return pl.pallas_call(
        body, name="gather_finish", in_specs=[ANY] * ns, out_specs=[ANY] * ns,
        out_shape=[jax.ShapeDtypeStruct(l.shape, l.dtype) for l in lands],
        input_output_aliases={a: a for a in range(ns)},
        scratch_shapes=[pltpu.SemaphoreType.DMA((3 * ns,)), pltpu.SemaphoreType.DMA((3 * ns,))],
        compiler_params=pltpu.CompilerParams(has_side_effects=True),
    )(*lands)


def _push_wait(name, sems, srcs, lands, axes, views, bits, after):
    ns = len(srcs)

    def body(*refs):
        src_refs, land_refs = refs[:ns], refs[ns:2 * ns]
        send_sems, recv_sems = refs[2 * ns:2 * ns + 2]
        me = _my_position()
        for k in bits:
            for a in range(ns):
                s, d, got = views(me, k, a, src_refs, land_refs, axes)
                cp = pltpu.make_async_remote_copy(
                    src_ref=s, dst_ref=got, send_sem=send_sems.at[7 * a + k - 1], recv_sem=recv_sems.at[7 * a + k - 1],
                    device_id=_flip(me, k), device_id_type=MESH)
                cp.wait_send()
                cp.wait_recv()

    thru = [pltpu.HBM(a.shape, a.dtype) for a in list(srcs) + list(lands)]
    outs = pl.pallas_call(
        body, name=name,
        in_specs=[HBM_SPEC] * (2 * ns) + [SEM_SPEC, SEM_SPEC] + [ANY] * len(after),
        out_specs=[HBM_SPEC] * (2 * ns),
        out_shape=thru,
        input_output_aliases={i: i for i in range(2 * ns)},
        compiler_params=pltpu.CompilerParams(has_side_effects=DATAFLOW),
    )(*srcs, *lands, sems[0], sems[1], *after)
    return outs[:ns], outs[ns:]


def _small_allgather(v, name):
    r, c = v.shape

    def body(v_ref, all_ref, sum_ref, send_sems, recv_sems):
        me = _my_position()
        all_ref[_linear(me)] = v_ref[...]
        copies = []
        for k in range(1, N_DEV):
            peer = _flip(me, k)
            copies.append(pltpu.make_async_remote_copy(
                src_ref=v_ref, dst_ref=all_ref.at[_linear(me)], send_sem=send_sems.at[k - 1], recv_sem=recv_sems.at[k - 1],
                device_id=peer, device_id_type=MESH))
        for cp in copies:
            cp.start()
        for k in range(1, N_DEV):
            peer = _flip(me, k)
            pltpu.make_async_remote_copy(
                src_ref=v_ref, dst_ref=all_ref.at[_linear(peer)], send_sem=send_sems.at[k - 1], recv_sem=recv_sems.at[k - 1],
                device_id=peer, device_id_type=MESH).wait_recv()
        for cp in copies:
            cp.wait_send()
        acc = all_ref[0]
        for d in range(1, N_DEV):
            acc = acc + all_ref[d]
        sum_ref[...] = acc

    return pl.pallas_call(
        body,
        name=name,
        in_specs=[VMEM_SPEC],
        out_specs=[VMEM_SPEC, VMEM_SPEC],
        out_shape=[jax.ShapeDtypeStruct((N_DEV, r, c), F32), jax.ShapeDtypeStruct((r, c), F32)],
        scratch_shapes=[pltpu.SemaphoreType.DMA((N_DEV - 1,)), pltpu.SemaphoreType.DMA((N_DEV - 1,))],
        compiler_params=pltpu.CompilerParams(has_side_effects=True, vmem_limit_bytes=VMEM_LIMIT),
    )(v)


def _ada_fwd_call(cin, ada_w, ada_b_cols):
    nl, d, ncol = ada_w.shape
    nrow = cin.shape[0]

    def body(c_ref, w_ref, b_ref, o_ref):
        cs = _silu(c_ref[...]).astype(BF16)
        for l in range(nl):
            o_ref[l] = _dot(cs, w_ref[l].astype(BF16)) + b_ref[l]

    return pl.pallas_call(
        body, name="ada_fwd", in_specs=[VMEM_SPEC] * 3, out_specs=VMEM_SPEC,
        out_shape=jax.ShapeDtypeStruct((nl, nrow, ncol), F32), compiler_params=_params(),
    )(cin, ada_w, ada_b_cols)


def _ada_bwd_call(cin, ada_w, dmod):
    nl, d, ncol = ada_w.shape
    nrow = cin.shape[0]

    def body(c_ref, w_ref, dm_ref, gw_ref, dcs_ref):
        cs = _silu(c_ref[...]).astype(BF16)
        acc = jnp.zeros((nrow, d), F32)
        for l in range(nl):
            dm = dm_ref[l].astype(BF16)
            gw_ref[l] = _dot_tn(cs, dm)
            acc = acc + _dot_nt(dm, w_ref[l].astype(BF16))
        dcs_ref[...] = acc

    return pl.pallas_call(
        body, name="ada_bwd", in_specs=[VMEM_SPEC] * 3, out_specs=[VMEM_SPEC, VMEM_SPEC],
        out_shape=[jax.ShapeDtypeStruct((nl, d, ncol), F32), jax.ShapeDtypeStruct((nrow, d), F32)],
        compiler_params=_params(),
    )(cin, ada_w, dmod)


def _adamw_math(w, g, m, v):
    m = ADAM_B1 * m + (1.0 - ADAM_B1) * g
    v = ADAM_B2 * v + (1.0 - ADAM_B2) * jnp.square(g)
    m_hat = m / (1.0 - ADAM_B1 ** ADAM_STEP)
    v_hat = v / (1.0 - ADAM_B2 ** ADAM_STEP)
    delta = -ADAM_LR * (m_hat / (jnp.sqrt(v_hat) + ADAM_EPS) + ADAM_WD * w)
    return delta, m, v


def _adamw_sharded(w, m, v, mine, slabs, axis, my_idx, layer, prev, name):
    nl, r, c = w.shape
    tm = _pick(r, (128, 64, 32, 16))
    nprev = 0 if prev is None else len(prev)
    nslab = slabs.shape[0]
    if axis is None:
        mine_spec = pl.BlockSpec((1, tm, c), lambda i, idx: (0, i, 0))
    else:
        mine_spec = _slab_block(r, c, tm, axis)

    def body(idx_ref, w_ref, m_ref, v_ref, mine_ref, s_ref, *rest):
        g_ref, d_ref, nm_ref, nv_ref = rest[nprev:]
        g = (mine_ref[0] if axis is None else mine_ref[...]).astype(F32)
        for k in range(nslab):
            g = g + s_ref[k].astype(F32)
        delta, nm, nv = _adamw_math(w_ref[0], g, m_ref[0], v_ref[0])
        g_ref[0], d_ref[0], nm_ref[0], nv_ref[0] = g, delta, nm, nv

    spec = pl.BlockSpec((1, tm, c), lambda i, idx: (layer, i, 0))
    out = jax.ShapeDtypeStruct(w.shape, F32)
    return pl.pallas_call(
        body, name=name,
        grid_spec=pltpu.PrefetchScalarGridSpec(
            num_scalar_prefetch=1, grid=(r // tm,),
            in_specs=[spec, spec, spec, mine_spec,
                      pl.BlockSpec((nslab, tm, c), lambda i, idx: (0, i, 0))] + [ANY] * nprev,
            out_specs=[spec] * 4),
        out_shape=[out] * 4,
        input_output_aliases={6 + j: j for j in range(nprev)},
        compiler_params=_params(("parallel",)),
    )(my_idx, w, m, v, mine, slabs, *(() if prev is None else prev))


def _adamw_dense(w, g, m, v, name):
    r, c = w.shape
    tm = _pick(r, (256, 128, 64, 32, 16, 8))

    def body(w_ref, g_ref, m_ref, v_ref, d_ref, nm_ref, nv_ref):
        d_ref[...], nm_ref[...], nv_ref[...] = _adamw_math(w_ref[...], g_ref[...], m_ref[...], v_ref[...])

    spec = pl.BlockSpec((tm, c), lambda i: (i, 0))
    out = jax.ShapeDtypeStruct(w.shape, F32)
    return pl.pallas_call(
        body, name=name, grid=(r // tm,), in_specs=[spec] * 4, out_specs=[spec] * 3, out_shape=[out] * 3,
        compiler_params=_params(("parallel",)),
    )(w, g, m, v)


def _pack(parts, width=128):
    flat = jnp.concatenate([p.reshape(-1).astype(F32) for p in parts])
    n = flat.shape[0]
    total = -(-n // (8 * width)) * (8 * width)
    return jnp.pad(flat, (0, total - n)).reshape(total // width, width)


def _unpack(buf, shapes):
    flat = buf.reshape(-1)
    out, off = [], 0
    for s in shapes:
        n = int(np.prod(s))
        out.append(flat[off:off + n].reshape(s))
        off += n
    return out


def kernel(x, c, ctx, c_ctx, ada_w, ada_b, norm_g, w_in, na_rpb, ret_decay_logit, w_proj_na, w_proj_ret, w_out, final_g, loss_target, m_c_ctx, m_ada_w, m_ada_b, m_norm_g, m_w_in, m_na_rpb, m_ret_decay_logit, m_w_proj_na, m_w_proj_ret, m_w_out, m_final_g, v_c_ctx, v_ada_w, v_ada_b, v_norm_g, v_w_in, v_na_rpb, v_ret_decay_logit, v_w_proj_na, v_w_proj_ret, v_w_out, v_final_g):
    depth = w_in.shape[0]
    n_lat, d = x.shape[1], x.shape[2]
    n_ctx = ctx.shape[1]
    t = n_lat + n_ctx
    w_na = w_proj_na.shape[1]
    w_retv = w_proj_ret.shape[1] * N_DEV
    in_cols = w_in.shape[2] * N_DEV
    w_qk = (in_cols - 4 * w_na - 2 * w_retv - 2 * d) // 2
    sizes = (w_na, w_na, w_na, w_na, w_qk, w_qk, w_retv, w_retv, d, d)
    off = tuple(int(o) for o in np.cumsum((0,) + sizes))
    NA_Q, NA_K, NA_V, NA_Z, RET_Q, RET_K, RET_V, RET_Z, G_NA, G_RET = range(10)
    rows = n_lat // GRID_W
    me = _my_position()
    my_idx = _linear(me)
    tm_row = _pick(n_ctx, (256, 128))

    idx_arr = jnp.reshape(my_idx, (1,)).astype(jnp.int32)
    chip_idx = jnp.stack([_linear(_flip(me, kc)) for kc in CHIP_BITS]).astype(jnp.int32)

    ncol = ada_w.shape[2]
    c_all, _ = _small_allgather(jnp.pad(c, ((0, 7), (0, 0))), "allgather_c")
    cin = jnp.concatenate([c_all[:, 0, :], c_ctx[None, :], jnp.zeros((7, d), F32)], axis=0)
    ada_b_cols = lax.dynamic_slice_in_dim(ada_b, my_idx * ncol, ncol, axis=1)[:, None, :]
    mod_cols = _ada_fwd_call(cin, ada_w, ada_b_cols)
    mod_gathered, _ = _small_allgather(mod_cols.reshape(depth * 16, ncol), "allgather_mod")
    mod_all = mod_gathered.reshape(N_DEV, depth, 16, ncol).transpose(1, 2, 0, 3).reshape(depth, 16, N_DEV * ncol)
    mod_lat = lax.dynamic_index_in_dim(mod_all, my_idx, axis=1, keepdims=False)
    mod_ctx = mod_all[:, 8, :]

    w_axes = (1, 1, 0, 0)
    w_names = ("w_in", "w_proj_na", "w_proj_ret", "w_out")
    shard = [[w[l].astype(BF16) for w in (w_in, w_proj_na, w_proj_ret, w_out)] for l in range(depth)]
    groups = [[(0, 0)], [(0, 1), (0, 2), (0, 3)]] + [[(l, a) for a in range(4)] for l in range(1, depth)]
    gathers, token = {}, mod_gathered
    for gi, keys in enumerate(groups):
        srcs = [shard[l][a] for l, a in keys]
        axes = tuple(w_axes[a] for _, a in keys)
        lands = [_place_shard(s, lax.empty(tuple(n * (N_DEV if i == ax else 1) for i, n in enumerate(s.shape)), BF16),
                              ax, idx_arr) for s, ax in zip(srcs, axes)]
        sizes = tuple(s.shape[ax] for s, ax in zip(srcs, axes))
        sems, srcs, lands, token = _push_start(
            f"gather_start_{gi}", srcs, lands, axes, _gather_views, GATHER_BITS, (token,))
        flight = dict(name=f"gather_wait_{gi}", sems=sems, srcs=srcs, lands=lands, axes=axes, sizes=sizes, ready=None)
        for pos, key in enumerate(keys):
            gathers[key] = (flight, pos)

    def landed(l, a, act):
        flight, pos = gathers[(l, a)]
        if flight["ready"] is None:
            arrived = _push_wait(flight["name"], flight["sems"], flight["srcs"], flight["lands"],
                                 flight["axes"], _gather_views, GATHER_BITS, (act, token))[1]
            flight["ready"] = _gather_finish(arrived, flight["axes"], flight["sizes"])
        return flight["ready"][pos]

    pending, scatters = {}, []

    def send_dw(l, a, dw):
        pending[(l, a)] = dw
        if a == 0:
            keys = [(0, 0)] if l == 0 else [(l, b) for b in range(4)]
        elif l == 0 and a == 1:
            keys = [(0, 1), (0, 2), (0, 3)]
        else:
            return None
        srcs = [pending[k] for k in keys]
        axes = tuple(w_axes[b] for _, b in keys)
        sizes = tuple(s.shape[ax] // N_DEV for s, ax in zip(srcs, axes))
        slab_shapes = [tuple(n // (N_DEV if i == ax else 1) for i, n in enumerate(s.shape)) for s, ax in zip(srcs, axes)]
        by_chip = keys == [(0, 0)]
        if by_chip:
            theirs = _pair_exchange(srcs, axes, sizes)
            srcs = [_pair_add(g, p, ax, chip_idx) for g, p, ax in zip(srcs, theirs, axes)]
            lands = [lax.empty((3,) + shp, BF16) for shp in slab_shapes]
            views, bits = _chip_views, CHIP_BITS[1:]
        else:
            lands = [lax.empty((N_DEV - 1,) + shp, BF16) for shp in slab_shapes]
            views, bits = _scatter_views, PEER_BITS
        sems, srcs, lands, tok = _push_start(f"scatter_start_{len(scatters)}", srcs, lands, axes, views, bits, ())
        scatters.append(dict(name=f"scatter_wait_{len(scatters)}", sems=sems, srcs=srcs, lands=lands, axes=axes, keys=keys,
                             views=views, bits=bits, by_chip=by_chip))
        return tok

    cos, sin = _rope_tables(t, n_lat)
    k_scale = RET_KEY_DIM ** -0.5
    assert off[NA_Z] % w_na == 0 and off[G_NA] % d == 0 and off[G_RET] % d == 0 and off[RET_Z] % w_retv == 0
    assert off[RET_V] % (_ret_head_group(w_retv // RET_VAL_DIM) * RET_VAL_DIM) == 0
    na_cols = tuple(off[i] // NA_HEAD_DIM for i in (NA_Q, NA_K, NA_V))
    norm_mod_fwd, norm_mod_bwd = _make_rowwise(_f_norm_mod, "norm_mod", (BF16,), (d,), n_lat, tm_row, (0,))
    gate_na_fwd, gate_na_bwd = _make_rowwise(_f_gate_na, "gate_na", (BF16,), (w_na,), n_lat, tm_row, (0, 1),
                                             col_blocks={1: (w_na, off[NA_Z] // w_na)})
    merge_fwd, merge_bwd = _make_rowwise(_f_merge, "merge", (BF16,), (d,), n_lat, tm_row, (0, 1, 2, 3),
                                         col_blocks={0: (d, off[G_NA] // d), 1: (d, off[G_RET] // d)},
                                         drow_dtypes={2: BF16, 3: BF16})
    residual_fwd, residual_bwd = _make_rowwise(_f_residual, "residual", (F32,), (d,), n_lat, tm_row, (0, 1),
                                               drow_dtypes={1: BF16})
    loss_fwd, loss_bwd = _make_rowwise(_f_loss, "loss_head", (F32,), (128,), n_lat, tm_row, (0,))

    def pair(a, b):
        return jnp.stack([a, b])[:, None, :]

    def mod_vectors(mod_lat_l, mod_ctx_l, norm_g_l):
        shift, scale, gate = jnp.split(mod_lat_l, 3)
        c_shift, c_scale, c_gate = jnp.split(mod_ctx_l, 3)
        return pair(norm_g_l, norm_g_l), pair(scale, c_scale), pair(shift, c_shift), pair(gate, c_gate)

    def rotary_qk(uq, uk):
        return _rope(uq, cos, sin, 1.0), _rope(uk, cos, sin, k_scale)

    def log_decay(logit):
        return jax.nn.log_sigmoid(logit.astype(F32))

    xa = jnp.concatenate([x[0], ctx[0]], axis=0)
    saved = []
    for l in range(depth):
        vecs, vecs_vjp = jax.vjp(mod_vectors, mod_lat[l], mod_ctx[l], norm_g[l])
        (h,) = norm_mod_fwd((xa,), vecs[:3])
        wl_in = landed(l, 0, h)
        u = _matmul(h, wl_in, out_dtype=BF16, name="in_proj_fwd")
        (qr, kr), rotary_vjp = jax.vjp(rotary_qk, u[:, off[RET_Q]:off[RET_Q + 1]], u[:, off[RET_K]:off[RET_K + 1]])
        bt, bt_vjp = jax.vjp(lambda r: _na_bias_table(r, rows), na_rpb[l])
        lam, lam_vjp = jax.vjp(log_decay, ret_decay_logit[l])
        o_na = _na_fwd_call(u, u, u, na_cols, w_na, bt, n_lat)
        o_f, st_f = _ret_fwd_call(qr, kr, u, off[RET_V], lam[0], n_lat, False)
        o_b, st_b = _ret_fwd_call(qr, kr, u, off[RET_V], lam[1], n_lat, True)
        (a_na,) = gate_na_fwd((o_na, u), ())
        a_ret = _gate_ret_fwd_call(o_f, o_b, u, off[RET_Z] // w_retv, tm_row)
        wl_pna, wl_pret, wl_out = landed(l, 1, a_na), landed(l, 2, a_na), landed(l, 3, a_na)
        y_na = _matmul(a_na, wl_pna, out_dtype=F32, name="proj_na_fwd")
        y_ret = _matmul(a_ret, wl_pret, out_dtype=F32, name="proj_ret_fwd")
        (merged,) = merge_fwd((u, u, y_na, y_ret), ())
        out = _matmul(merged, wl_out, out_dtype=F32, name="out_proj_fwd")
        (xa_next,) = residual_fwd((xa, out), vecs[3:])
        saved.append(dict(xa=xa, vecs=vecs, vecs_vjp=vecs_vjp, h=h, w=(wl_in, wl_pna, wl_pret, wl_out), u=u, qr=qr, kr=kr,
                          rotary_vjp=rotary_vjp, bt=bt, bt_vjp=bt_vjp, lam=lam, lam_vjp=lam_vjp, o_na=o_na, o_f=o_f,
                          o_b=o_b, st_f=st_f, st_b=st_b, a_na=a_na, a_ret=a_ret, y_na=y_na, y_ret=y_ret,
                          merged=merged, out=out))
        xa = xa_next

    fg_pair, fg_vjp = jax.vjp(lambda g: pair(g, g), final_g)
    x_last = xa[:n_lat]
    (loss_rows,) = loss_fwd((x_last, loss_target[0]), (fg_pair,))
    loss = lax.psum(jnp.sum(loss_rows), ("x", "y", "c"))
    (dx_last,), (d_fg_pair,) = loss_bwd((x_last, loss_target[0]), (fg_pair,), (jnp.ones_like(loss_rows),))
    (d_final_g,) = fg_vjp(d_fg_pair)
    dxa = jnp.pad(dx_last, ((0, n_ctx), (0, 0)))

    d_mod_lat, d_mod_ctx, d_norm_g, d_rpb, d_decay = ([None] * depth for _ in range(5))
    for l in reversed(range(depth)):
        s = saved[l]
        u, qr, kr = s["u"], s["qr"], s["kr"]
        wl_in, wl_pna, wl_pret, wl_out = s["w"]
        (dxa_res, d_out), (d_gate,) = residual_bwd((s["xa"], s["out"]), s["vecs"][3:], (dxa,))
        send_dw(l, 3, _matmul(s["merged"], d_out, trans_a=True, out_dtype=BF16, name="out_proj_dw"))
        d_merged = _matmul(d_out, wl_out, trans_b=True, out_dtype=BF16, name="out_proj_da")
        (dg_na, dg_ret, dy_na, dy_ret), _ = merge_bwd((u, u, s["y_na"], s["y_ret"]), (), (d_merged,))
        send_dw(l, 2, _matmul(s["a_ret"], dy_ret, trans_a=True, out_dtype=BF16, name="proj_ret_dw"))
        da_ret = _matmul(dy_ret, wl_pret, trans_b=True, out_dtype=BF16, name="proj_ret_da")
        tok = send_dw(l, 1, _matmul(s["a_na"], dy_na, trans_a=True, out_dtype=BF16, name="proj_na_dw"))
        da_na = _matmul(dy_na, wl_pna, trans_b=True, out_dtype=BF16, name="proj_na_da", after=tok)
        do_ret, dz_ret = _gate_ret_bwd_call(s["o_f"], s["o_b"], u, off[RET_Z] // w_retv, da_ret, tm_row)
        (do_na, dz_na), _ = gate_na_bwd((s["o_na"], u), (), (da_na,))
        dq_f, dk_f, dv_f, dl_f = _ret_bwd_call(qr, kr, u, off[RET_V], s["lam"][0], s["st_f"], do_ret, n_lat, False)
        dq_b, dk_b, dv_b, dl_b = _ret_bwd_call(qr, kr, u, off[RET_V], s["lam"][1], s["st_b"], do_ret, n_lat, True)
        dq, dk, dv, dbt = _na_bwd_call(u, u, u, na_cols, w_na, s["bt"], do_na, n_lat)
        d_uq, d_uk = s["rotary_vjp"]((dq_f + dq_b, dk_f + dk_b))
        du = jnp.concatenate([dq, dk.astype(BF16), dv.astype(BF16), dz_na, d_uq, d_uk, dv_f + dv_b, dz_ret,
                              dg_na, dg_ret], axis=1)
        (d_rpb[l],) = s["bt_vjp"](dbt)
        (d_decay[l],) = s["lam_vjp"](jnp.stack([dl_f[:, 0, 0], dl_b[:, 0, 0]]))
        tok = send_dw(l, 0, _matmul(s["h"], du, trans_a=True, out_dtype=BF16, name="in_proj_dw"))
        dh = _matmul(du, wl_in, trans_b=True, out_dtype=BF16, name="in_proj_da", after=tok)
        (dxa,), d_vecs = norm_mod_bwd((s["xa"],), s["vecs"][:3], (dh,), acc=(dxa_res,))
        d_mod_lat[l], d_mod_ctx[l], d_norm_g[l] = s["vecs_vjp"](tuple(d_vecs) + (d_gate,))
    gx = dxa[:n_lat]
    d_mod_lat, d_mod_ctx, d_norm_g, d_rpb, d_decay = (jnp.stack(a) for a in (d_mod_lat, d_mod_ctx, d_norm_g, d_rpb, d_decay))

    small_shapes = [d_mod_lat.shape, d_mod_ctx.shape, d_norm_g.shape, d_final_g.shape, d_rpb.shape, d_decay.shape]
    packed = _pack([d_mod_lat, d_mod_ctx, d_norm_g, d_final_g, d_rpb, d_decay])
    g_all, g_sum = _small_allgather(packed, "allgather_small_grads")
    dml_sum, dmc_sum, grad_norm_g, grad_final_g, grad_na_rpb, grad_decay = _unpack(g_sum, small_shapes)
    grad_ada_b = dml_sum + dmc_sum
    dml_all = g_all.reshape(N_DEV, -1)[:, :depth * 3 * d].reshape(N_DEV, depth, 3 * d)

    def my_cols(a):
        return lax.dynamic_slice_in_dim(a, my_idx * ncol, ncol, axis=a.ndim - 1)

    dmod = jnp.concatenate(
        [my_cols(dml_all).transpose(1, 0, 2), my_cols(dmc_sum)[:, None, :], jnp.zeros((depth, 7, ncol), F32)], axis=1)
    grad_ada_w, dcs_part = _ada_bwd_call(cin, ada_w, dmod)
    _, dcs = _small_allgather(dcs_part, "allgather_dcsilu")
    sg = jax.nn.sigmoid(c_ctx)
    grad_c_ctx = dcs[8] * (sg * (1.0 + c_ctx * (1.0 - sg)))

    def flat2(a):
        return a.reshape(a.shape[0] * a.shape[1], a.shape[2])

    small_w = [c_ctx, ada_b, norm_g, na_rpb, ret_decay_logit, final_g]
    small_g = [grad_c_ctx, grad_ada_b, grad_norm_g, grad_na_rpb, grad_decay, grad_final_g]
    small_m = [m_c_ctx, m_ada_b, m_norm_g, m_na_rpb, m_ret_decay_logit, m_final_g]
    small_v = [v_c_ctx, v_ada_b, v_norm_g, v_na_rpb, v_ret_decay_logit, v_final_g]
    shp = [a.shape for a in small_w]
    ds_, nms_, nvs_ = _adamw_dense(_pack(small_w), _pack(small_g), _pack(small_m), _pack(small_v), "adamw_small")
    ds_, nms_, nvs_ = _unpack(ds_, shp), _unpack(nms_, shp), _unpack(nvs_, shp)

    d_ada, nm_ada, nv_ada = [a.reshape(ada_w.shape) for a in _adamw_dense(
        flat2(ada_w), flat2(grad_ada_w), flat2(m_ada_w), flat2(v_ada_w), "adamw_ada_w")]

    w_all = (w_in, w_proj_na, w_proj_ret, w_out)
    m_all = (m_w_in, m_w_proj_na, m_w_proj_ret, m_w_out)
    v_all = (v_w_in, v_w_proj_na, v_w_proj_ret, v_w_out)
    upd = [None] * 4
    after = d_ada
    for flight in scatters:
        mine, slabs = _push_wait(flight["name"], flight["sems"], flight["srcs"], flight["lands"], flight["axes"],
                                 flight["views"], flight["bits"], (after,))
        for (l, a), own, s in zip(flight["keys"], mine, slabs):
            upd[a] = _adamw_sharded(w_all[a], m_all[a], v_all[a], own, s, None if flight["by_chip"] else w_axes[a],
                                    idx_arr, l, upd[a], "adamw_" + w_names[a])
            after = upd[a][1]
    (g_w_in, d_w_in, nm_w_in, nv_w_in), (g_pna, d_pna, nm_pna, nv_pna) = upd[0], upd[1]
    (g_pret, d_pret, nm_pret, nv_pret), (g_out, d_out, nm_out, nv_out) = upd[2], upd[3]

    def order(cc, aw, ab, ng, wi, rp, dl, pn, pr, wo, fg):
        return [cc, aw, ab, ng, wi, rp, dl, pn, pr, wo, fg]

    grads_out = order(grad_c_ctx, grad_ada_w, grad_ada_b, grad_norm_g, g_w_in, grad_na_rpb, grad_decay, g_pna, g_pret, g_out, grad_final_g)
    delta_out = order(ds_[0], d_ada, ds_[1], ds_[2], d_w_in, ds_[3], ds_[4], d_pna, d_pret, d_out, ds_[5])
    m_out = order(nms_[0], nm_ada, nms_[1], nms_[2], nm_w_in, nms_[3], nms_[4], nm_pna, nm_pret, nm_out, nms_[5])
    v_out = order(nvs_[0], nv_ada, nvs_[1], nvs_[2], nv_w_in, nvs_[3], nvs_[4], nv_pna, nv_pret, nv_out, nvs_[5])
    return (loss, gx[None], *grads_out, *delta_out, *m_out, *v_out)
```

```python
import functools

import numpy as np
import jax
import jax.numpy as jnp
from jax import lax
from jax.experimental import pallas as pl
from jax.experimental.pallas import tpu as pltpu

F32 = jnp.float32
BF16 = jnp.bfloat16

N_DEV = 8
GRID_W = 64
NA_HEAD_DIM = 128
NA_WIN_ROWS = 8
NA_WIN_COLS = 16
RET_KEY_DIM = 128
RET_VAL_DIM = 256
RET_CHUNK = 128
ROPE_BASE = 10000.0
NORM_EPS = 1e-6
MASK_VALUE = -1e30

ADAM_LR = 0.001
ADAM_B1 = 0.9
ADAM_B2 = 0.999
ADAM_EPS = 1e-08
ADAM_WD = 0.01
ADAM_STEP = 10

VMEM_LIMIT = 48 * 1024 * 1024
MESH = pl.DeviceIdType.MESH
ANY = pl.BlockSpec(memory_space=pl.ANY)
VMEM_SPEC = pl.BlockSpec(memory_space=pltpu.VMEM)


def _params(sem=None):
    return pltpu.CompilerParams(dimension_semantics=sem, vmem_limit_bytes=VMEM_LIMIT)


def _pick(n, prefs):
    for p in prefs:
        if n % p == 0:
            return p
    return n


def _dot(a, b):
    return lax.dot_general(a, b, (((1,), (0,)), ((), ())), preferred_element_type=F32)


def _dot_nt(a, b):
    return lax.dot_general(a, b, (((1,), (1,)), ((), ())), preferred_element_type=F32)


def _dot_tn(a, b):
    return lax.dot_general(a, b, (((0,), (0,)), ((), ())), preferred_element_type=F32)


def _silu(x):
    return x * jax.nn.sigmoid(x)


def _matmul(a, b, *, trans_a=False, trans_b=False, out_dtype=F32, name="matmul", after=None):
    if trans_a:
        kdim, m = a.shape
    else:
        m, kdim = a.shape
    if trans_b:
        n, kb = b.shape
    else:
        kb, n = b.shape
    assert kdim == kb, (a.shape, b.shape, trans_a, trans_b)
    tm = _pick(m, (1152, 1024, 768, 512, 256, 128))
    tn = _pick(n, (1024, 512, 256, 128) if trans_b else (512, 256, 128))
    tk = _pick(kdim, (2304, 2048, 1024, 512, 256, 128))
    nk = kdim // tk
    dn = (((0 if trans_a else 1,), (1 if trans_b else 0,)), ((), ()))

    def body(a_ref, b_ref, *rest):
        o_ref, acc_ref = rest[-2:]
        part = lax.dot_general(a_ref[...], b_ref[...], dn, preferred_element_type=F32)
        if nk == 1:
            o_ref[...] = part.astype(o_ref.dtype)
        else:
            k = pl.program_id(2)

            @pl.when(k == 0)
            def _():
                acc_ref[...] = part

            @pl.when(k > 0)
            def _():
                acc_ref[...] += part

            @pl.when(k == nk - 1)
            def _():
                o_ref[...] = acc_ref[...].astype(o_ref.dtype)

    a_spec = pl.BlockSpec((tk, tm), lambda i, j, k: (k, i)) if trans_a else pl.BlockSpec((tm, tk), lambda i, j, k: (i, k))
    b_spec = pl.BlockSpec((tn, tk), lambda i, j, k: (j, k)) if trans_b else pl.BlockSpec((tk, tn), lambda i, j, k: (k, j))
    return pl.pallas_call(
        body,
        name=name,
        grid=(m // tm, n // tn, nk),
        in_specs=[a_spec, b_spec] + ([] if after is None else [ANY]),
        out_specs=pl.BlockSpec((tm, tn), lambda i, j, k: (i, j)),
        out_shape=jax.ShapeDtypeStruct((m, n), out_dtype),
        scratch_shapes=[pltpu.VMEM((tm, tn) if nk > 1 else (8, 128), F32)],
        compiler_params=_params(("parallel", "parallel", "arbitrary")),
    )(*((a, b) if after is None else (a, b, after)))


def _make_rowwise(f, name, out_dtypes, out_cols, n_lat, tm, diff_rows, col_blocks=None, drow_dtypes=None):
    drow_dtypes = drow_dtypes or {}

    def tile_fn(*args):
        return tuple(o.astype(dt) for o, dt in zip(f(*args), out_dtypes))

    def row_spec(k, arr):
        width, index = (col_blocks or {}).get(k, (arr.shape[1], 0))
        return pl.BlockSpec((tm, width), lambda i: (i, index))

    def row_width(k, arr):
        return (col_blocks or {}).get(k, (arr.shape[1], 0))[0]

    def fwd_call(rows, vecs):
        t = rows[0].shape[0]
        nr, nv = len(rows), len(vecs)
        nl = n_lat // tm

        def body(*refs):
            grp = (pl.program_id(0) >= nl).astype(jnp.int32)
            args = [r[...] for r in refs[:nr]] + [v[grp] for v in refs[nr:nr + nv]]
            for o_ref, o in zip(refs[nr + nv:], tile_fn(*args)):
                o_ref[...] = o

        return pl.pallas_call(
            body,
            name=name + "_fwd",
            grid=(t // tm,),
            in_specs=[row_spec(k, r) for k, r in enumerate(rows)]
            + [pl.BlockSpec(v.shape, lambda i: (0, 0, 0)) for v in vecs],
            out_specs=[pl.BlockSpec((tm, c), lambda i: (i, 0)) for c in out_cols],
            out_shape=[jax.ShapeDtypeStruct((t, c), dt) for c, dt in zip(out_cols, out_dtypes)],
            compiler_params=_params(("parallel",)),
        )(*rows, *vecs)

    def bwd_call(rows, vecs, gs, acc=None):
        t = rows[0].shape[0]
        nr, nv, ng = len(rows), len(vecs), len(gs)
        nl = n_lat // tm
        nd = len(diff_rows)
        acc = [None] * nd if acc is None else list(acc)
        acc_in = [a for a in acc if a is not None]

        def body(*refs):
            i = pl.program_id(0)
            grp = (i >= nl).astype(jnp.int32)
            args = [r[...] for r in refs[:nr]] + [v[grp] for v in refs[nr:nr + nv]]
            g_refs = refs[nr + nv:nr + nv + ng]
            acc_refs = list(refs[nr + nv + ng:nr + nv + ng + len(acc_in)])
            drow_refs = refs[nr + nv + ng + len(acc_in):nr + nv + ng + len(acc_in) + nd]
            dvec_refs = refs[nr + nv + ng + len(acc_in) + nd:]
            _, vjp = jax.vjp(tile_fn, *args)
            grads = vjp(tuple(g[...] for g in g_refs))
            for d_ref, k, a in zip(drow_refs, diff_rows, acc):
                gk = grads[k] if a is None else grads[k] + acc_refs.pop(0)[...]
                d_ref[...] = gk.astype(d_ref.dtype)

            @pl.when(i == 0)
            def _():
                for d_ref in dvec_refs:
                    d_ref[...] = jnp.zeros_like(d_ref)

            for j, d_ref in enumerate(dvec_refs):
                d_ref[grp] += grads[nr + j]

        outs = pl.pallas_call(
            body,
            name=name + "_bwd",
            grid=(t // tm,),
            in_specs=[row_spec(k, r) for k, r in enumerate(rows)]
            + [pl.BlockSpec(v.shape, lambda i: (0, 0, 0)) for v in vecs]
            + [pl.BlockSpec((tm, g.shape[1]), lambda i: (i, 0)) for g in gs]
            + [pl.BlockSpec((tm, a.shape[1]), lambda i: (i, 0)) for a in acc_in],
            out_specs=[pl.BlockSpec((tm, row_width(k, rows[k])), lambda i: (i, 0)) for k in diff_rows]
            + [pl.BlockSpec(v.shape, lambda i: (0, 0, 0)) for v in vecs],
            out_shape=[jax.ShapeDtypeStruct((t, row_width(k, rows[k])), drow_dtypes.get(k, rows[k].dtype))
                       for k in diff_rows]
            + [jax.ShapeDtypeStruct(v.shape, F32) for v in vecs],
            compiler_params=_params(("arbitrary",)),
        )(*rows, *vecs, *gs, *acc_in)
        return outs[:nd], outs[nd:]

    return fwd_call, bwd_call


def _f_norm_mod(x, g, scale, shift):
    r = lax.rsqrt(jnp.mean(x * x, axis=-1, keepdims=True) + NORM_EPS)
    return ((x * r * g) * (1.0 + scale) + shift,)


def _f_gate_na(o, z):
    return (o.astype(F32) * _silu(z.astype(F32)),)


def _f_merge(g_na, g_ret, y_na, y_ret):
    return (jax.nn.sigmoid(g_na.astype(F32)) * y_na + jax.nn.sigmoid(g_ret.astype(F32)) * y_ret,)


def _f_residual(x, out, gate):
    return (x + gate * out,)


def _f_loss(x, target, g):
    r = lax.rsqrt(jnp.mean(x * x, axis=-1, keepdims=True) + NORM_EPS)
    y = x * r * g
    e = 0.5 * jnp.mean(jnp.square(y - target), axis=-1, keepdims=True)
    return (jnp.broadcast_to(e * (1.0 / 128.0), (x.shape[0], 128)),)


def _gate_ret_fwd_call(of, ob, z, zblk, tm):
    t, w = of.shape
    nh = w // RET_VAL_DIM

    def body(of_ref, ob_ref, z_ref, a_ref):
        for hh in range(nh):
            sl = slice(hh * RET_VAL_DIM, (hh + 1) * RET_VAL_DIM)
            o = of_ref[:, sl] + ob_ref[:, sl]
            r = lax.rsqrt(jnp.mean(o * o, axis=-1, keepdims=True) + NORM_EPS)
            a_ref[:, sl] = ((o * r) * _silu(z_ref[:, sl].astype(F32))).astype(a_ref.dtype)

    spec = pl.BlockSpec((tm, w), lambda i: (i, 0))
    zspec = pl.BlockSpec((tm, w), lambda i: (i, zblk))
    return pl.pallas_call(
        body, name="gate_ret_fwd", grid=(t // tm,), in_specs=[spec, spec, zspec], out_specs=spec,
        out_shape=jax.ShapeDtypeStruct((t, w), BF16), compiler_params=_params(("parallel",)),
    )(of, ob, z)


def _gate_ret_bwd_call(of, ob, z, zblk, da, tm):
    t, w = of.shape
    nh = w // RET_VAL_DIM

    def body(of_ref, ob_ref, z_ref, da_ref, do_ref, dz_ref):
        for hh in range(nh):
            sl = slice(hh * RET_VAL_DIM, (hh + 1) * RET_VAL_DIM)
            o = of_ref[:, sl] + ob_ref[:, sl]
            r = lax.rsqrt(jnp.mean(o * o, axis=-1, keepdims=True) + NORM_EPS)
            n = o * r
            zf = z_ref[:, sl].astype(F32)
            sg = jax.nn.sigmoid(zf)
            g = da_ref[:, sl].astype(F32)
            dn = g * (zf * sg)
            dz_ref[:, sl] = (g * n * (sg * (1.0 + zf * (1.0 - sg)))).astype(dz_ref.dtype)
            do_ref[:, sl] = r * (dn - n * jnp.mean(dn * n, axis=-1, keepdims=True))

    spec = pl.BlockSpec((tm, w), lambda i: (i, 0))
    zspec = pl.BlockSpec((tm, w), lambda i: (i, zblk))
    return pl.pallas_call(
        body, name="gate_ret_bwd", grid=(t // tm,), in_specs=[spec, spec, zspec, spec], out_specs=[spec, spec],
        out_shape=[jax.ShapeDtypeStruct((t, w), F32), jax.ShapeDtypeStruct((t, w), z.dtype)],
        compiler_params=_params(("parallel",)),
    )(of, ob, z, da)


NA_PAIR = 2 * GRID_W
NA_KEY_ROWS = NA_WIN_ROWS + 2
NA_CLASSES = 5


def _na_geometry(t, n_lat):
    rows = n_lat // GRID_W
    assert rows % 2 == 0 and rows >= NA_KEY_ROWS + 2, rows
    return rows, rows // 2, NA_KEY_ROWS * GRID_W, t - n_lat, t // NA_PAIR


def _na_base(p, rows):
    return jnp.clip(2 * p - NA_WIN_ROWS // 2, 0, rows - NA_KEY_ROWS)


def _na_class(p, rows):
    return p - _na_base(p, rows) // 2


def _na_group(pairs, n_ctx):
    assert n_ctx % NA_PAIR == 0, n_ctx
    return 2 if pairs % 2 == 0 and (n_ctx // NA_PAIR) % 2 == 0 else 1


def _na_bias_spec(i, grp, rows, pairs, n_loc):
    return pl.BlockSpec((1, 1, NA_PAIR, n_loc),
                        lambda h, g: (h, _na_class(jnp.minimum(g * grp + i, pairs - grp + i), rows), 0, 0))


def _na_fwd_call(q, k, v, col0, w, bt, n_lat):
    t = q.shape[0]
    nh = w // NA_HEAD_DIM
    rows, pairs, n_loc, n_ctx, nq = _na_geometry(t, n_lat)
    grp = _na_group(pairs, n_ctx)
    scale = NA_HEAD_DIM ** -0.5

    def body(q_ref, k_ref, v_ref, *rest):
        bt_refs, o_ref = rest[:grp], rest[grp]
        g = pl.program_id(1)
        kc = k_ref[pl.ds(n_lat, n_ctx), :]
        vc = v_ref[pl.ds(n_lat, n_ctx), :]

        @pl.when(g < pairs // grp)
        def _():
            for i in range(grp):
                sl = slice(i * NA_PAIR, (i + 1) * NA_PAIR)
                qb = q_ref[sl, :]
                s_ctx = _dot_nt(qb, kc) * scale
                start = pl.multiple_of(_na_base(g * grp + i, rows) * GRID_W, GRID_W)
                kw = k_ref[pl.ds(start, n_loc), :]
                vw = v_ref[pl.ds(start, n_loc), :]
                s_loc = _dot_nt(qb, kw) * scale + bt_refs[i][0, 0]
                m = jnp.maximum(jnp.max(s_loc, axis=-1, keepdims=True), jnp.max(s_ctx, axis=-1, keepdims=True))
                p_loc = jnp.exp(s_loc - m)
                p_ctx = jnp.exp(s_ctx - m)
                l = jnp.sum(p_loc, axis=-1, keepdims=True) + jnp.sum(p_ctx, axis=-1, keepdims=True)
                o = _dot(p_loc.astype(BF16), vw) + _dot(p_ctx.astype(BF16), vc)
                o_ref[sl, :] = (o / l).astype(o_ref.dtype)

        @pl.when(g >= pairs // grp)
        def _():
            s_ctx = _dot_nt(q_ref[...], kc) * scale
            m = jnp.max(s_ctx, axis=-1, keepdims=True)
            p = jnp.exp(s_ctx - m)
            l = jnp.sum(p, axis=-1, keepdims=True)
            o_ref[...] = (_dot(p.astype(BF16), vc) / l).astype(o_ref.dtype)

    qspec = pl.BlockSpec((grp * NA_PAIR, NA_HEAD_DIM), lambda h, g: (g, h))
    in_q = pl.BlockSpec((grp * NA_PAIR, NA_HEAD_DIM), lambda h, g: (g, col0[0] + h))
    in_k = pl.BlockSpec((t, NA_HEAD_DIM), lambda h, g: (0, col0[1] + h))
    in_v = pl.BlockSpec((t, NA_HEAD_DIM), lambda h, g: (0, col0[2] + h))
    return pl.pallas_call(
        body,
        name="na_attn_fwd",
        grid=(nh, nq // grp),
        in_specs=[in_q, in_k, in_v] + [_na_bias_spec(i, grp, rows, pairs, n_loc) for i in range(grp)],
        out_specs=qspec,
        out_shape=jax.ShapeDtypeStruct((t, w), BF16),
        compiler_params=_params(("parallel", "arbitrary")),
    )(q, k, v, *([bt] * grp))


def _na_bwd_call(q, k, v, col0, w, bt, do, n_lat):
    t = q.shape[0]
    nh = w // NA_HEAD_DIM
    rows, pairs, n_loc, n_ctx, nq = _na_geometry(t, n_lat)
    scale = NA_HEAD_DIM ** -0.5
    grp = _na_group(pairs, n_ctx)

    def body(q_ref, k_ref, v_ref, do_ref, *rest):
        bt_refs = rest[:grp]
        dq_ref, dk_ref, dv_ref = rest[2 * grp:2 * grp + 3]
        dbt_refs = rest[2 * grp + 3:]
        g = pl.program_id(1)

        @pl.when(g == 0)
        def _():
            dk_ref[...] = jnp.zeros_like(dk_ref)
            dv_ref[...] = jnp.zeros_like(dv_ref)

        kc = k_ref[pl.ds(n_lat, n_ctx), :]
        vc = v_ref[pl.ds(n_lat, n_ctx), :]

        @pl.when(g < pairs // grp)
        def _():
            for i in range(grp):
                p = g * grp + i
                sl = slice(i * NA_PAIR, (i + 1) * NA_PAIR)
                qb = q_ref[sl, :]
                dob = do_ref[sl, :]
                s_ctx = _dot_nt(qb, kc) * scale
                dp_ctx = _dot_nt(dob, vc)
                start = pl.multiple_of(_na_base(p, rows) * GRID_W, GRID_W)
                kw = k_ref[pl.ds(start, n_loc), :]
                vw = v_ref[pl.ds(start, n_loc), :]
                s_loc = _dot_nt(qb, kw) * scale + bt_refs[i][0, 0]
                m = jnp.maximum(jnp.max(s_loc, axis=-1, keepdims=True), jnp.max(s_ctx, axis=-1, keepdims=True))
                p_loc = jnp.exp(s_loc - m)
                p_ctx = jnp.exp(s_ctx - m)
                inv = 1.0 / (jnp.sum(p_loc, axis=-1, keepdims=True) + jnp.sum(p_ctx, axis=-1, keepdims=True))
                p_loc = p_loc * inv
                p_ctx = p_ctx * inv
                dp_loc = _dot_nt(dob, vw)
                delta = (jnp.sum(p_loc * dp_loc, axis=-1, keepdims=True)
                         + jnp.sum(p_ctx * dp_ctx, axis=-1, keepdims=True))
                ds_loc = p_loc * (dp_loc - delta)
                ds_ctx = p_ctx * (dp_ctx - delta)
                first = jnp.logical_or(g == 0, _na_class(p, rows) != _na_class(p - grp, rows))
                dbt_ref = dbt_refs[i]

                @pl.when(first)
                def _():
                    dbt_ref[0, 0] = ds_loc

                @pl.when(jnp.logical_not(first))
                def _():
                    dbt_ref[0, 0] += ds_loc

                dsl = (ds_loc * scale).astype(BF16)
                dsc = (ds_ctx * scale).astype(BF16)
                dq_ref[sl, :] = (_dot(dsl, kw) + _dot(dsc, kc)).astype(dq_ref.dtype)
                dk_ref[pl.ds(start, n_loc), :] += _dot_tn(dsl, qb)
                dv_ref[pl.ds(start, n_loc), :] += _dot_tn(p_loc.astype(BF16), dob)
                dk_ref[pl.ds(n_lat, n_ctx), :] += _dot_tn(dsc, qb)
                dv_ref[pl.ds(n_lat, n_ctx), :] += _dot_tn(p_ctx.astype(BF16), dob)

        @pl.when(g >= pairs // grp)
        def _():
            qb = q_ref[...]
            dob = do_ref[...]
            s_ctx = _dot_nt(qb, kc) * scale
            dp_ctx = _dot_nt(dob, vc)
            m = jnp.max(s_ctx, axis=-1, keepdims=True)
            p = jnp.exp(s_ctx - m)
            p = p * (1.0 / jnp.sum(p, axis=-1, keepdims=True))
            delta = jnp.sum(p * dp_ctx, axis=-1, keepdims=True)
            dsc = (p * (dp_ctx - delta) * scale).astype(BF16)
            dq_ref[...] = _dot(dsc, kc).astype(dq_ref.dtype)
            dk_ref[pl.ds(n_lat, n_ctx), :] += _dot_tn(dsc, qb)
            dv_ref[pl.ds(n_lat, n_ctx), :] += _dot_tn(p.astype(BF16), dob)

    qspec = pl.BlockSpec((grp * NA_PAIR, NA_HEAD_DIM), lambda h, g: (g, h))
    kspec = pl.BlockSpec((t, NA_HEAD_DIM), lambda h, g: (0, h))
    bspecs = [_na_bias_spec(i, grp, rows, pairs, n_loc) for i in range(grp)]
    zeros = [jnp.zeros(bt.shape, F32) for _ in range(grp)]
    outs = pl.pallas_call(
        body,
        name="na_attn_bwd",
        grid=(nh, nq // grp),
        in_specs=[pl.BlockSpec((grp * NA_PAIR, NA_HEAD_DIM), lambda h, g: (g, col0[0] + h)),
                  pl.BlockSpec((t, NA_HEAD_DIM), lambda h, g: (0, col0[1] + h)),
                  pl.BlockSpec((t, NA_HEAD_DIM), lambda h, g: (0, col0[2] + h)), qspec] + bspecs + [ANY] * grp,
        out_specs=[qspec, kspec, kspec] + bspecs,
        out_shape=[
            jax.ShapeDtypeStruct((t, w), BF16),
            jax.ShapeDtypeStruct((t, w), F32),
            jax.ShapeDtypeStruct((t, w), F32),
        ] + [jax.ShapeDtypeStruct(bt.shape, F32)] * grp,
        input_output_aliases={4 + grp + i: 3 + i for i in range(grp)},
        compiler_params=_params(("parallel", "arbitrary")),
    )(q, k, v, do, *([bt] * grp), *zeros)
    dbt = outs[3]
    for extra in outs[4:]:
        dbt = dbt + extra
    return outs[0], outs[1], outs[2], dbt


def _na_bias_table(rpb, rows):
    pairs = rows // 2
    nb = 2 * NA_WIN_COLS - 1
    nq = NA_KEY_ROWS // 2
    e1 = np.zeros((NA_CLASSES, 2, nq, 2, 2 * NA_WIN_ROWS - 1), np.float32)
    valid = np.zeros((NA_CLASSES, 2, nq, 2), bool)
    for cls, p in enumerate((0, 1, 2, pairs - 2, pairs - 1)):
        base = int(np.clip(2 * p - NA_WIN_ROWS // 2, 0, rows - NA_KEY_ROWS))
        assert p - base // 2 == cls, (rows, cls, p, base)
        for i in range(2):
            r = 2 * p + i
            r0 = int(np.clip(r - NA_WIN_ROWS // 2, 0, rows - NA_WIN_ROWS))
            for kk in range(NA_KEY_ROWS):
                if r0 <= base + kk < r0 + NA_WIN_ROWS:
                    valid[cls, i, kk // 2, kk % 2] = True
                    e1[cls, i, kk // 2, kk % 2, base + kk - r + NA_WIN_ROWS - 1] = 1.0
    cidx = np.arange(GRID_W)
    dc = np.clip(cidx[None, :] - cidx[:, None] + (NA_WIN_COLS - 1), 0, nb - 1)
    c0 = np.clip(cidx - NA_WIN_COLS // 2, 0, GRID_W - NA_WIN_COLS)
    col_in = (cidx[None, :] >= c0[:, None]) & (cidx[None, :] < c0[:, None] + NA_WIN_COLS)
    e2 = np.zeros((GRID_W, 2, GRID_W, 2, nb), np.float32)
    for par in range(2):
        e2[np.arange(GRID_W)[:, None], par, np.arange(GRID_W)[None, :], par, dc] = 1.0
    mask = valid[:, :, None, :, :, None] & col_in[None, None, :, None, None, :]
    t1 = jnp.einsum("hab,xiqpa->hxiqpb", rpb, jnp.asarray(e1), precision=lax.Precision.HIGHEST)
    t1 = t1.reshape(t1.shape[:4] + (2 * nb,))
    b = jnp.einsum("hxiqm,cwm->hxicqw", t1, jnp.asarray(e2.reshape(GRID_W, 2 * GRID_W, 2 * nb)),
                   precision=lax.Precision.HIGHEST)
    b = jnp.where(jnp.asarray(mask.reshape(NA_CLASSES, 2, GRID_W, nq, 2 * GRID_W))[None], b, MASK_VALUE)
    return b.reshape(rpb.shape[0], NA_CLASSES, NA_PAIR, NA_KEY_ROWS * GRID_W)


def _ret_decays(lam_s, reverse):
    c = RET_CHUNK
    ii = lax.broadcasted_iota(jnp.int32, (c, c), 0)
    jj = lax.broadcasted_iota(jnp.int32, (c, c), 1)
    d = (jj - ii) if reverse else (ii - jj)
    dpos = jnp.maximum(d.astype(F32), 0.0)
    mask = jnp.where(d >= 0, jnp.exp(dpos * lam_s), 0.0)
    pi = lax.broadcasted_iota(jnp.int32, (c, 1), 0).astype(F32)
    qpos = (c - pi) if reverse else (pi + 1.0)
    kpos = pi if reverse else (c - 1.0 - pi)
    qd = jnp.exp(qpos * lam_s)
    kd = jnp.exp(kpos * lam_s)
    g = jnp.exp(jnp.full((1, RET_VAL_DIM), c * lam_s, F32))
    return mask, dpos, qd, kd, qpos, kpos, g


def _ret_head_group(nh):
    return _pick(nh, (4, 2))


def _ret_chunk_of(t, nt, nl, reverse):
    return (nt - 1 - t) if reverse else (t + nl) % nt


def _ret_fwd_call(qr, kr, v, vcol, lam, n_lat, reverse):
    t = qr.shape[0]
    nh = qr.shape[1] // RET_KEY_DIM
    c = RET_CHUNK
    nt, nl = t // c, n_lat // c

    hg = _ret_head_group(nh)
    dk, dv = RET_KEY_DIM, RET_VAL_DIM

    def body(lam_ref, q_ref, k_ref, v_ref, o_ref, s_ref, state):
        hb, step = pl.program_id(0), pl.program_id(1)

        @pl.when(step == 0)
        def _():
            state[...] = jnp.zeros_like(state)

        for j in range(hg):
            mask, _, qd, kd, _, _, g = _ret_decays(lam_ref[hb * hg + j], reverse)
            q, k, vv = q_ref[:, j * dk:(j + 1) * dk], k_ref[:, j * dk:(j + 1) * dk], v_ref[:, j * dv:(j + 1) * dv]
            p = _dot_nt(q, k) * mask
            s = state[j]
            qs = (q.astype(F32) * qd).astype(BF16)
            o_ref[:, j * dv:(j + 1) * dv] = _dot(p.astype(BF16), vv) + _dot(qs, s.astype(BF16))
            s_ref[j, 0] = s
            ks = (k.astype(F32) * kd).astype(BF16)
            state[j] = s * g + _dot_tn(ks, vv)

    def cmap(hb, step, lam_ref):
        return (_ret_chunk_of(step, nt, nl, reverse), hb)

    def vmap(hb, step, lam_ref):
        return (_ret_chunk_of(step, nt, nl, reverse), vcol // (hg * dv) + hb)

    return pl.pallas_call(
        body,
        name="retention_rev_fwd" if reverse else "retention_fwd",
        grid_spec=pltpu.PrefetchScalarGridSpec(
            num_scalar_prefetch=1,
            grid=(nh // hg, nt),
            in_specs=[
                pl.BlockSpec((c, hg * dk), cmap),
                pl.BlockSpec((c, hg * dk), cmap),
                pl.BlockSpec((c, hg * dv), vmap),
            ],
            out_specs=[
                pl.BlockSpec((c, hg * dv), cmap),
                pl.BlockSpec((hg, 1, dk, dv), lambda hb, step, lam_ref: (hb, step, 0, 0)),
            ],
            scratch_shapes=[pltpu.VMEM((hg, dk, dv), F32)],
        ),
        out_shape=[
            jax.ShapeDtypeStruct((t, nh * RET_VAL_DIM), F32),
            jax.ShapeDtypeStruct((nh, nt, RET_KEY_DIM, RET_VAL_DIM), F32),
        ],
        compiler_params=_params(("parallel", "arbitrary")),
    )(lam, qr, kr, v)


def _ret_bwd_call(qr, kr, v, vcol, lam, states, do, n_lat, reverse):
    t = qr.shape[0]
    nh = qr.shape[1] // RET_KEY_DIM
    c = RET_CHUNK
    nt, nl = t // c, n_lat // c

    hg = _ret_head_group(nh)
    dk, dv = RET_KEY_DIM, RET_VAL_DIM

    def body(lam_ref, q_ref, k_ref, v_ref, s_ref, do_ref, dq_ref, dk_ref, dv_ref, dl_ref, dstate):
        hb, rstep = pl.program_id(0), pl.program_id(1)

        @pl.when(rstep == 0)
        def _():
            dstate[...] = jnp.zeros_like(dstate)
            dl_ref[...] = jnp.zeros_like(dl_ref)

        for j in range(hg):
            mask, dpos, qd, kd, qpos, kpos, g = _ret_decays(lam_ref[hb * hg + j], reverse)
            ksl, vsl = slice(j * dk, (j + 1) * dk), slice(j * dv, (j + 1) * dv)
            q, k, vv = q_ref[:, ksl], k_ref[:, ksl], v_ref[:, vsl]
            qf, kf = q.astype(F32), k.astype(F32)
            s = s_ref[j, 0]
            ds = dstate[j]
            dob = do_ref[:, vsl].astype(BF16)
            sb, dsb = s.astype(BF16), ds.astype(BF16)
            a = _dot_nt(q, k)
            p = a * mask
            dp = _dot_nt(dob, vv)
            da = dp * mask
            dab = da.astype(BF16)
            dqc = _dot_nt(dob, sb)
            dkc = _dot_nt(vv, dsb)
            qs = (qf * qd).astype(BF16)
            ks = (kf * kd).astype(BF16)
            dq_ref[:, ksl] = (_dot(dab, k) + dqc * qd).astype(dq_ref.dtype)
            dk_ref[:, ksl] = (_dot_tn(dab, q) + dkc * kd).astype(dk_ref.dtype)
            dv_ref[:, vsl] = (_dot_tn(p.astype(BF16), dob) + _dot(ks, dsb)).astype(dv_ref.dtype)
            terms = (
                jnp.sum(jnp.sum(da * a * dpos, axis=1, keepdims=True), axis=0, keepdims=True)
                + jnp.sum(jnp.sum(dqc * qf * (qd * qpos), axis=1, keepdims=True), axis=0, keepdims=True)
                + jnp.sum(jnp.sum(dkc * kf * (kd * kpos), axis=1, keepdims=True), axis=0, keepdims=True)
                + jnp.sum(jnp.sum(ds * s * (g * c), axis=1, keepdims=True), axis=0, keepdims=True)
            )
            dl_ref[j] += jnp.broadcast_to(terms, (8, 128))
            dstate[j] = ds * g + _dot_tn(qs, dob)

    def cmap(hb, rstep, lam_ref):
        return (_ret_chunk_of(nt - 1 - rstep, nt, nl, reverse), hb)

    def vmap(hb, rstep, lam_ref):
        return (_ret_chunk_of(nt - 1 - rstep, nt, nl, reverse), vcol // (hg * dv) + hb)

    return pl.pallas_call(
        body,
        name="retention_rev_bwd" if reverse else "retention_bwd",
        grid_spec=pltpu.PrefetchScalarGridSpec(
            num_scalar_prefetch=1,
            grid=(nh // hg, nt),
            in_specs=[
                pl.BlockSpec((c, hg * dk), cmap),
                pl.BlockSpec((c, hg * dk), cmap),
                pl.BlockSpec((c, hg * dv), vmap),
                pl.BlockSpec((hg, 1, dk, dv), lambda hb, rstep, lam_ref: (hb, nt - 1 - rstep, 0, 0)),
                pl.BlockSpec((c, hg * dv), cmap),
            ],
            out_specs=[
                pl.BlockSpec((c, hg * dk), cmap),
                pl.BlockSpec((c, hg * dk), cmap),
                pl.BlockSpec((c, hg * dv), cmap),
                pl.BlockSpec((hg, 8, 128), lambda hb, rstep, lam_ref: (hb, 0, 0)),
            ],
            scratch_shapes=[pltpu.VMEM((hg, dk, dv), F32)],
        ),
        out_shape=[
            jax.ShapeDtypeStruct(qr.shape, qr.dtype),
            jax.ShapeDtypeStruct(kr.shape, kr.dtype),
            jax.ShapeDtypeStruct((t, nh * dv), v.dtype),
            jax.ShapeDtypeStruct((nh, 8, 128), F32),
        ],
        compiler_params=_params(("parallel", "arbitrary")),
    )(lam, qr, kr, v, states, do)


def _rope_tables(t, n_lat):
    nf = RET_KEY_DIM // 4
    tok = np.arange(n_lat)
    inv_freq = (ROPE_BASE ** (-np.arange(nf, dtype=np.float32) / nf)).astype(np.float32)
    row = (tok // GRID_W).astype(np.float32)
    col = (tok % GRID_W).astype(np.float32)
    ang = np.concatenate([row[:, None] * inv_freq, col[:, None] * inv_freq], axis=-1).astype(np.float32)
    cos = np.ones((t, 2 * nf), np.float32)
    sin = np.zeros((t, 2 * nf), np.float32)
    cos[:n_lat] = np.cos(ang)
    sin[:n_lat] = np.sin(ang)
    return jnp.asarray(cos), jnp.asarray(sin)


def _rope(xb, cos, sin, mult):
    t, w = xb.shape
    nh = w // RET_KEY_DIM
    half = RET_KEY_DIM // 2
    x = xb.astype(F32).reshape(t, nh, 2, half)
    x1, x2 = x[:, :, 0], x[:, :, 1]
    c, s = cos[:, None, :], sin[:, None, :]
    out = jnp.stack([x1 * c - x2 * s, x2 * c + x1 * s], axis=2) * mult
    return out.reshape(t, w).astype(BF16)


def _my_position():
    return lax.axis_index("x"), lax.axis_index("y"), lax.axis_index("c")


def _flip(pos, k):
    x, y, c = pos
    return (1 - x if k & 4 else x, 1 - y if k & 2 else y, 1 - c if k & 1 else c)


def _linear(pos):
    return 4 * pos[0] + 2 * pos[1] + pos[2]


def _slab(ref, axis, idx, size):
    start = pl.multiple_of(idx * size, size)
    return ref.at[pl.ds(start, size), :] if axis == 0 else ref.at[:, pl.ds(start, size)]


HBM_SPEC = pl.BlockSpec(memory_space=pltpu.HBM)
SEM_SPEC = pl.BlockSpec(memory_space=pltpu.SEMAPHORE)
DATAFLOW = pltpu.SideEffectType.DATAFLOW_SIDE_EFFECTING
PEER_BITS = (1, 2, 4, 6, 3, 5, 7)
GATHER_BITS = (1, 2, 4, 6)


def _in_hbm(a):
    return pltpu.with_memory_space_constraint(a, pltpu.HBM)


def _gather_views(me, k, a, src_refs, land_refs, axes):
    size = src_refs[a].shape[axes[a]]
    peer = _flip(me, k)
    return src_refs[a], _slab(land_refs[a], axes[a], _linear(me), size), _slab(land_refs[a], axes[a], _linear(peer), size)


def _scatter_views(me, k, a, src_refs, land_refs, axes):
    size = land_refs[a].shape[1 + axes[a]]
    peer = _flip(me, k)
    return _slab(src_refs[a], axes[a], _linear(peer), size), land_refs[a].at[k - 1], land_refs[a].at[k - 1]


CHIP_BITS = (0, 2, 4, 6)


def _chip_views(me, k, a, src_refs, land_refs, axes):
    j = CHIP_BITS.index(k)
    return src_refs[a].at[j], land_refs[a].at[j - 1], land_refs[a].at[j - 1]


def _pair_exchange(grads, axes, sizes):
    ns = len(grads)

    def slab_shape(a):
        s = grads[a].shape
        return (sizes[a], s[1]) if axes[a] == 0 else (s[0], sizes[a])

    def body(*refs):
        g_refs, p_refs = refs[:ns], refs[ns:2 * ns]
        send_sems, recv_sems = refs[2 * ns:]
        me = _my_position()
        sibling = _flip(me, 1)
        copies = []
        for j, kc in enumerate(CHIP_BITS):
            for a in range(ns):
                cp = pltpu.make_async_remote_copy(
                    src_ref=_slab(g_refs[a], axes[a], _linear(_flip(me, kc | 1)), sizes[a]), dst_ref=p_refs[a].at[j],
                    send_sem=send_sems.at[4 * a + j], recv_sem=recv_sems.at[4 * a + j],
                    device_id=sibling, device_id_type=MESH)
                cp.start()
                copies.append(cp)
        for cp in copies:
            cp.wait_recv()
        for cp in copies:
            cp.wait_send()

    return pl.pallas_call(
        body, name="scatter_pair_exchange", in_specs=[ANY] * ns, out_specs=[ANY] * ns,
        out_shape=[jax.ShapeDtypeStruct((4,) + slab_shape(a), grads[a].dtype) for a in range(ns)],
        scratch_shapes=[pltpu.SemaphoreType.DMA((4 * ns,)), pltpu.SemaphoreType.DMA((4 * ns,))],
        compiler_params=pltpu.CompilerParams(has_side_effects=True),
    )(*grads)


def _pair_add(grad, theirs, axis, chip_idx):
    _, r, c = theirs.shape
    tm = _pick(r, (256, 128, 64, 32, 16))
    if axis == 0:
        mine_spec = pl.BlockSpec((tm, c), lambda j, i, idx: (idx[j] * (r // tm) + i, 0))
    else:
        mine_spec = pl.BlockSpec((tm, c), lambda j, i, idx: (i, idx[j]))

    def body(idx_ref, mine_ref, theirs_ref, o_ref):
        o_ref[0] = (mine_ref[...].astype(F32) + theirs_ref[0].astype(F32)).astype(o_ref.dtype)

    spec = pl.BlockSpec((1, tm, c), lambda j, i, idx: (j, i, 0))
    return pl.pallas_call(
        body, name="scatter_pair_add",
        grid_spec=pltpu.PrefetchScalarGridSpec(
            num_scalar_prefetch=1, grid=(4, r // tm), in_specs=[mine_spec, spec], out_specs=spec),
        out_shape=jax.ShapeDtypeStruct(theirs.shape, theirs.dtype),
        compiler_params=_params(("parallel", "parallel")),
    )(chip_idx, grad, theirs)


def _slab_block(rows, cols, tm, axis):
    if axis == 0:
        return pl.BlockSpec((tm, cols), lambda i, idx: (idx[0] * (rows // tm) + i, 0))
    return pl.BlockSpec((tm, cols), lambda i, idx: (i, idx[0]))


def _place_shard(shard, land, axis, my_idx):
    r, c = shard.shape
    tm = _pick(r, (512, 256, 128, 64, 32, 16))

    def body(idx_ref, s_ref, land_ref, o_ref):
        o_ref[...] = s_ref[...]

    return pl.pallas_call(
        body, name="gather_place",
        grid_spec=pltpu.PrefetchScalarGridSpec(
            num_scalar_prefetch=1, grid=(r // tm,),
            in_specs=[pl.BlockSpec((tm, c), lambda i, idx: (i, 0)), ANY],
            out_specs=_slab_block(r, c, tm, axis)),
        out_shape=jax.ShapeDtypeStruct(land.shape, land.dtype),
        input_output_aliases={2: 0},
        compiler_params=_params(("parallel",)),
    )(my_idx, shard, land)


def _push_start(name, srcs, lands, axes, views, bits, deps):
    ns = len(srcs)

    def body(*refs):
        src_refs, land_refs = refs[:ns], refs[ns:2 * ns]
        send_sems, recv_sems = refs[2 * ns + len(deps):2 * ns + len(deps) + 2]
        token = refs[-1]
        me = _my_position()
        for k in bits:
            for a in range(ns):
                s, d, _ = views(me, k, a, src_refs, land_refs, axes)
                pltpu.make_async_remote_copy(
                    src_ref=s, dst_ref=d, send_sem=send_sems.at[7 * a + k - 1], recv_sem=recv_sems.at[7 * a + k - 1],
                    device_id=_flip(me, k), device_id_type=MESH).start()
        token[...] = jnp.zeros_like(token)

    thru = [pltpu.HBM(a.shape, a.dtype) for a in list(srcs) + list(lands)]
    outs = pl.pallas_call(
        body, name=name,
        in_specs=[HBM_SPEC] * (2 * ns) + [ANY] * len(deps),
        out_specs=[SEM_SPEC, SEM_SPEC] + [HBM_SPEC] * (2 * ns) + [VMEM_SPEC],
        out_shape=[pltpu.SemaphoreType.DMA((7 * ns,)), pltpu.SemaphoreType.DMA((7 * ns,))] + thru
        + [jax.ShapeDtypeStruct((8, 128), F32)],
        input_output_aliases={i: 2 + i for i in range(2 * ns)},
        compiler_params=pltpu.CompilerParams(has_side_effects=DATAFLOW),
    )(*[_in_hbm(a) for a in srcs], *[_in_hbm(a) for a in lands], *deps)
    return (outs[0], outs[1]), outs[2:2 + ns], outs[2 + ns:2 + 2 * ns], outs[-1]


def _gather_finish(lands, axes, sizes):
    ns = len(lands)
    chips = (2, 4, 6)

    def body(*refs):
        land_refs = refs[ns:2 * ns]
        send_sems, recv_sems = refs[2 * ns:]
        me = _my_position()
        sibling = _flip(me, 1)
        copies = []
        for j, kc in enumerate(chips):
            for a in range(ns):
                def slab_of(pos):
                    return _slab(land_refs[a], axes[a], _linear(pos), sizes[a])
                send = pltpu.make_async_remote_copy(
                    src_ref=slab_of(_flip(me, kc)), dst_ref=slab_of(_flip(me, kc)), send_sem=send_sems.at[3 * a + j],
                    recv_sem=recv_sems.at[3 * a + j], device_id=sibling, device_id_type=MESH)
                recv = pltpu.make_async_remote_copy(
                    src_ref=slab_of(_flip(me, kc)), dst_ref=slab_of(_flip(sibling, kc)), send_sem=send_sems.at[3 * a + j],
                    recv_sem=recv_sems.at[3 * a + j], device_id=sibling, device_id_type=MESH)
                send.start()
                copies.append((send, recv))
        for send, recv in copies:
            recv.wait_recv()
        for send, recv in copies:
            send.wait_send()

    return pl.pallas_call(
        body, name="gather_finish", in_specs=[ANY] * ns, out_specs=[ANY] * ns,
        out_shape=[jax.ShapeDtypeStruct(l.shape, l.dtype) for l in lands],
        input_output_aliases={a: a for a in range(ns)},
        scratch_shapes=[pltpu.SemaphoreType.DMA((3 * ns,)), pltpu.SemaphoreType.DMA((3 * ns,))],
        compiler_params=pltpu.CompilerParams(has_side_effects=True),
    )(*lands)


def _push_wait(name, sems, srcs, lands, axes, views, bits, after):
    ns = len(srcs)

    def body(*refs):
        src_refs, land_refs = refs[:ns], refs[ns:2 * ns]
        send_sems, recv_sems = refs[2 * ns:2 * ns + 2]
        me = _my_position()
        for k in bits:
            for a in range(ns):
                s, d, got = views(me, k, a, src_refs, land_refs, axes)
                cp = pltpu.make_async_remote_copy(
                    src_ref=s, dst_ref=got, send_sem=send_sems.at[7 * a + k - 1], recv_sem=recv_sems.at[7 * a + k - 1],
                    device_id=_flip(me, k), device_id_type=MESH)
                cp.wait_send()
                cp.wait_recv()

    thru = [pltpu.HBM(a.shape, a.dtype) for a in list(srcs) + list(lands)]
    outs = pl.pallas_call(
        body, name=name,
        in_specs=[HBM_SPEC] * (2 * ns) + [SEM_SPEC, SEM_SPEC] + [ANY] * len(after),
        out_specs=[HBM_SPEC] * (2 * ns),
        out_shape=thru,
        input_output_aliases={i: i for i in range(2 * ns)},
        compiler_params=pltpu.CompilerParams(has_side_effects=DATAFLOW),
    )(*srcs, *lands, sems[0], sems[1], *after)
    return outs[:ns], outs[ns:]


def _small_allgather(v, name):
    r, c = v.shape

    def body(v_ref, all_ref, sum_ref, send_sems, recv_sems):
        me = _my_position()
        all_ref[_linear(me)] = v_ref[...]
        copies = []
        for k in range(1, N_DEV):
            peer = _flip(me, k)
            copies.append(pltpu.make_async_remote_copy(
                src_ref=v_ref, dst_ref=all_ref.at[_linear(me)], send_sem=send_sems.at[k - 1], recv_sem=recv_sems.at[k - 1],
                device_id=peer, device_id_type=MESH))
        for cp in copies:
            cp.start()
        for k in range(1, N_DEV):
            peer = _flip(me, k)
            pltpu.make_async_remote_copy(
                src_ref=v_ref, dst_ref=all_ref.at[_linear(peer)], send_sem=send_sems.at[k - 1], recv_sem=recv_sems.at[k - 1],
                device_id=peer, device_id_type=MESH).wait_recv()
        for cp in copies:
            cp.wait_send()
        acc = all_ref[0]
        for d in range(1, N_DEV):
            acc = acc + all_ref[d]
        sum_ref[...] = acc

    return pl.pallas_call(
        body,
        name=name,
        in_specs=[VMEM_SPEC],
        out_specs=[VMEM_SPEC, VMEM_SPEC],
        out_shape=[jax.ShapeDtypeStruct((N_DEV, r, c), F32), jax.ShapeDtypeStruct((r, c), F32)],
        scratch_shapes=[pltpu.SemaphoreType.DMA((N_DEV - 1,)), pltpu.SemaphoreType.DMA((N_DEV - 1,))],
        compiler_params=pltpu.CompilerParams(has_side_effects=True, vmem_limit_bytes=VMEM_LIMIT),
    )(v)


def _ada_fwd_call(cin, ada_w, ada_b_cols):
    nl, d, ncol = ada_w.shape
    nrow = cin.shape[0]

    def body(c_ref, w_ref, b_ref, o_ref):
        cs = _silu(c_ref[...]).astype(BF16)
        for l in range(nl):
            o_ref[l] = _dot(cs, w_ref[l].astype(BF16)) + b_ref[l]

    return pl.pallas_call(
        body, name="ada_fwd", in_specs=[VMEM_SPEC] * 3, out_specs=VMEM_SPEC,
        out_shape=jax.ShapeDtypeStruct((nl, nrow, ncol), F32), compiler_params=_params(),
    )(cin, ada_w, ada_b_cols)


def _ada_bwd_call(cin, ada_w, dmod):
    nl, d, ncol = ada_w.shape
    nrow = cin.shape[0]

    def body(c_ref, w_ref, dm_ref, gw_ref, dcs_ref):
        cs = _silu(c_ref[...]).astype(BF16)
        acc = jnp.zeros((nrow, d), F32)
        for l in range(nl):
            dm = dm_ref[l].astype(BF16)
            gw_ref[l] = _dot_tn(cs, dm)
            acc = acc + _dot_nt(dm, w_ref[l].astype(BF16))
        dcs_ref[...] = acc

    return pl.pallas_call(
        body, name="ada_bwd", in_specs=[VMEM_SPEC] * 3, out_specs=[VMEM_SPEC, VMEM_SPEC],
        out_shape=[jax.ShapeDtypeStruct((nl, d, ncol), F32), jax.ShapeDtypeStruct((nrow, d), F32)],
        compiler_params=_params(),
    )(cin, ada_w, dmod)


def _adamw_math(w, g, m, v):
    m = ADAM_B1 * m + (1.0 - ADAM_B1) * g
    v = ADAM_B2 * v + (1.0 - ADAM_B2) * jnp.square(g)
    m_hat = m / (1.0 - ADAM_B1 ** ADAM_STEP)
    v_hat = v / (1.0 - ADAM_B2 ** ADAM_STEP)
    delta = -ADAM_LR * (m_hat / (jnp.sqrt(v_hat) + ADAM_EPS) + ADAM_WD * w)
    return delta, m, v


def _adamw_sharded(w, m, v, mine, slabs, axis, my_idx, layer, prev, name):
    nl, r, c = w.shape
    tm = _pick(r, (128, 64, 32, 16))
    nprev = 0 if prev is None else len(prev)
    nslab = slabs.shape[0]
    if axis is None:
        mine_spec = pl.BlockSpec((1, tm, c), lambda i, idx: (0, i, 0))
    else:
        mine_spec = _slab_block(r, c, tm, axis)

    def body(idx_ref, w_ref, m_ref, v_ref, mine_ref, s_ref, *rest):
        g_ref, d_ref, nm_ref, nv_ref = rest[nprev:]
        g = (mine_ref[0] if axis is None else mine_ref[...]).astype(F32)
        for k in range(nslab):
            g = g + s_ref[k].astype(F32)
        delta, nm, nv = _adamw_math(w_ref[0], g, m_ref[0], v_ref[0])
        g_ref[0], d_ref[0], nm_ref[0], nv_ref[0] = g, delta, nm, nv

    spec = pl.BlockSpec((1, tm, c), lambda i, idx: (layer, i, 0))
    out = jax.ShapeDtypeStruct(w.shape, F32)
    return pl.pallas_call(
        body, name=name,
        grid_spec=pltpu.PrefetchScalarGridSpec(
            num_scalar_prefetch=1, grid=(r // tm,),
            in_specs=[spec, spec, spec, mine_spec,
                      pl.BlockSpec((nslab, tm, c), lambda i, idx: (0, i, 0))] + [ANY] * nprev,
            out_specs=[spec] * 4),
        out_shape=[out] * 4,
        input_output_aliases={6 + j: j for j in range(nprev)},
        compiler_params=_params(("parallel",)),
    )(my_idx, w, m, v, mine, slabs, *(() if prev is None else prev))


def _adamw_dense(w, g, m, v, name):
    r, c = w.shape
    tm = _pick(r, (256, 128, 64, 32, 16, 8))

    def body(w_ref, g_ref, m_ref, v_ref, d_ref, nm_ref, nv_ref):
        d_ref[...], nm_ref[...], nv_ref[...] = _adamw_math(w_ref[...], g_ref[...], m_ref[...], v_ref[...])

    spec = pl.BlockSpec((tm, c), lambda i: (i, 0))
    out = jax.ShapeDtypeStruct(w.shape, F32)
    return pl.pallas_call(
        body, name=name, grid=(r // tm,), in_specs=[spec] * 4, out_specs=[spec] * 3, out_shape=[out] * 3,
        compiler_params=_params(("parallel",)),
    )(w, g, m, v)


def _pack(parts, width=128):
    flat = jnp.concatenate([p.reshape(-1).astype(F32) for p in parts])
    n = flat.shape[0]
    total = -(-n // (8 * width)) * (8 * width)
    return jnp.pad(flat, (0, total - n)).reshape(total // width, width)


def _unpack(buf, shapes):
    flat = buf.reshape(-1)
    out, off = [], 0
    for s in shapes:
        n = int(np.prod(s))
        out.append(flat[off:off + n].reshape(s))
        off += n
    return out


def kernel(x, c, ctx, c_ctx, ada_w, ada_b, norm_g, w_in, na_rpb, ret_decay_logit, w_proj_na, w_proj_ret, w_out, final_g, loss_target, m_c_ctx, m_ada_w, m_ada_b, m_norm_g, m_w_in, m_na_rpb, m_ret_decay_logit, m_w_proj_na, m_w_proj_ret, m_w_out, m_final_g, v_c_ctx, v_ada_w, v_ada_b, v_norm_g, v_w_in, v_na_rpb, v_ret_decay_logit, v_w_proj_na, v_w_proj_ret, v_w_out, v_final_g):
    depth = w_in.shape[0]
    n_lat, d = x.shape[1], x.shape[2]
    n_ctx = ctx.shape[1]
    t = n_lat + n_ctx
    w_na = w_proj_na.shape[1]
    w_retv = w_proj_ret.shape[1] * N_DEV
    in_cols = w_in.shape[2] * N_DEV
    w_qk = (in_cols - 4 * w_na - 2 * w_retv - 2 * d) // 2
    sizes = (w_na, w_na, w_na, w_na, w_qk, w_qk, w_retv, w_retv, d, d)
    off = tuple(int(o) for o in np.cumsum((0,) + sizes))
    NA_Q, NA_K, NA_V, NA_Z, RET_Q, RET_K, RET_V, RET_Z, G_NA, G_RET = range(10)
    rows = n_lat // GRID_W
    me = _my_position()
    my_idx = _linear(me)
    tm_row = _pick(n_ctx, (256, 128))

    idx_arr = jnp.reshape(my_idx, (1,)).astype(jnp.int32)
    chip_idx = jnp.stack([_linear(_flip(me, kc)) for kc in CHIP_BITS]).astype(jnp.int32)

    ncol = ada_w.shape[2]
    c_all, _ = _small_allgather(jnp.pad(c, ((0, 7), (0, 0))), "allgather_c")
    cin = jnp.concatenate([c_all[:, 0, :], c_ctx[None, :], jnp.zeros((7, d), F32)], axis=0)
    ada_b_cols = lax.dynamic_slice_in_dim(ada_b, my_idx * ncol, ncol, axis=1)[:, None, :]
    mod_cols = _ada_fwd_call(cin, ada_w, ada_b_cols)
    mod_gathered, _ = _small_allgather(mod_cols.reshape(depth * 16, ncol), "allgather_mod")
    mod_all = mod_gathered.reshape(N_DEV, depth, 16, ncol).transpose(1, 2, 0, 3).reshape(depth, 16, N_DEV * ncol)
    mod_lat = lax.dynamic_index_in_dim(mod_all, my_idx, axis=1, keepdims=False)
    mod_ctx = mod_all[:, 8, :]

    w_axes = (1, 1, 0, 0)
    w_names = ("w_in", "w_proj_na", "w_proj_ret", "w_out")
    shard = [[w[l].astype(BF16) for w in (w_in, w_proj_na, w_proj_ret, w_out)] for l in range(depth)]
    groups = [[(0, 0)], [(0, 1), (0, 2), (0, 3)]] + [[(l, a) for a in range(4)] for l in range(1, depth)]
    gathers, token = {}, mod_gathered
    for gi, keys in enumerate(groups):
        srcs = [shard[l][a] for l, a in keys]
        axes = tuple(w_axes[a] for _, a in keys)
        lands = [_place_shard(s, lax.empty(tuple(n * (N_DEV if i == ax else 1) for i, n in enumerate(s.shape)), BF16),
                              ax, idx_arr) for s, ax in zip(srcs, axes)]
        sizes = tuple(s.shape[ax] for s, ax in zip(srcs, axes))
        sems, srcs, lands, token = _push_start(
            f"gather_start_{gi}", srcs, lands, axes, _gather_views, GATHER_BITS, (token,))
        flight = dict(name=f"gather_wait_{gi}", sems=sems, srcs=srcs, lands=lands, axes=axes, sizes=sizes, ready=None)
        for pos, key in enumerate(keys):
            gathers[key] = (flight, pos)

    def landed(l, a, act):
        flight, pos = gathers[(l, a)]
        if flight["ready"] is None:
            arrived = _push_wait(flight["name"], flight["sems"], flight["srcs"], flight["lands"],
                                 flight["axes"], _gather_views, GATHER_BITS, (act, token))[1]
            flight["ready"] = _gather_finish(arrived, flight["axes"], flight["sizes"])
        return flight["ready"][pos]

    pending, scatters = {}, []

    def send_dw(l, a, dw):
        pending[(l, a)] = dw
        if a == 0:
            keys = [(0, 0)] if l == 0 else [(l, b) for b in range(4)]
        elif l == 0 and a == 1:
            keys = [(0, 1), (0, 2), (0, 3)]
        else:
            return None
        srcs = [pending[k] for k in keys]
        axes = tuple(w_axes[b] for _, b in keys)
        sizes = tuple(s.shape[ax] // N_DEV for s, ax in zip(srcs, axes))
        slab_shapes = [tuple(n // (N_DEV if i == ax else 1) for i, n in enumerate(s.shape)) for s, ax in zip(srcs, axes)]
        by_chip = keys == [(0, 0)]
        if by_chip:
            theirs = _pair_exchange(srcs, axes, sizes)
            srcs = [_pair_add(g, p, ax, chip_idx) for g, p, ax in zip(srcs, theirs, axes)]
            lands = [lax.empty((3,) + shp, BF16) for shp in slab_shapes]
            views, bits = _chip_views, CHIP_BITS[1:]
        else:
            lands = [lax.empty((N_DEV - 1,) + shp, BF16) for shp in slab_shapes]
            views, bits = _scatter_views, PEER_BITS
        sems, srcs, lands, tok = _push_start(f"scatter_start_{len(scatters)}", srcs, lands, axes, views, bits, ())
        scatters.append(dict(name=f"scatter_wait_{len(scatters)}", sems=sems, srcs=srcs, lands=lands, axes=axes, keys=keys,
                             views=views, bits=bits, by_chip=by_chip))
        return tok

    cos, sin = _rope_tables(t, n_lat)
    k_scale = RET_KEY_DIM ** -0.5
    assert off[NA_Z] % w_na == 0 and off[G_NA] % d == 0 and off[G_RET] % d == 0 and off[RET_Z] % w_retv == 0
    assert off[RET_V] % (_ret_head_group(w_retv // RET_VAL_DIM) * RET_VAL_DIM) == 0
    na_cols = tuple(off[i] // NA_HEAD_DIM for i in (NA_Q, NA_K, NA_V))
    norm_mod_fwd, norm_mod_bwd = _make_rowwise(_f_norm_mod, "norm_mod", (BF16,), (d,), n_lat, tm_row, (0,))
    gate_na_fwd, gate_na_bwd = _make_rowwise(_f_gate_na, "gate_na", (BF16,), (w_na,), n_lat, tm_row, (0, 1),
                                             col_blocks={1: (w_na, off[NA_Z] // w_na)})
    merge_fwd, merge_bwd = _make_rowwise(_f_merge, "merge", (BF16,), (d,), n_lat, tm_row, (0, 1, 2, 3),
                                         col_blocks={0: (d, off[G_NA] // d), 1: (d, off[G_RET] // d)},
                                         drow_dtypes={2: BF16, 3: BF16})
    residual_fwd, residual_bwd = _make_rowwise(_f_residual, "residual", (F32,), (d,), n_lat, tm_row, (0, 1),
                                               drow_dtypes={1: BF16})
    loss_fwd, loss_bwd = _make_rowwise(_f_loss, "loss_head", (F32,), (128,), n_lat, tm_row, (0,))

    def pair(a, b):
        return jnp.stack([a, b])[:, None, :]

    def mod_vectors(mod_lat_l, mod_ctx_l, norm_g_l):
        shift, scale, gate = jnp.split(mod_lat_l, 3)
        c_shift, c_scale, c_gate = jnp.split(mod_ctx_l, 3)
        return pair(norm_g_l, norm_g_l), pair(scale, c_scale), pair(shift, c_shift), pair(gate, c_gate)

    def rotary_qk(uq, uk):
        return _rope(uq, cos, sin, 1.0), _rope(uk, cos, sin, k_scale)

    def log_decay(logit):
        return jax.nn.log_sigmoid(logit.astype(F32))

    xa = jnp.concatenate([x[0], ctx[0]], axis=0)
    saved = []
    for l in range(depth):
        vecs, vecs_vjp = jax.vjp(mod_vectors, mod_lat[l], mod_ctx[l], norm_g[l])
        (h,) = norm_mod_fwd((xa,), vecs[:3])
        wl_in = landed(l, 0, h)
        u = _matmul(h, wl_in, out_dtype=BF16, name="in_proj_fwd")
        (qr, kr), rotary_vjp = jax.vjp(rotary_qk, u[:, off[RET_Q]:off[RET_Q + 1]], u[:, off[RET_K]:off[RET_K + 1]])
        bt, bt_vjp = jax.vjp(lambda r: _na_bias_table(r, rows), na_rpb[l])
        lam, lam_vjp = jax.vjp(log_decay, ret_decay_logit[l])
        o_na = _na_fwd_call(u, u, u, na_cols, w_na, bt, n_lat)
        o_f, st_f = _ret_fwd_call(qr, kr, u, off[RET_V], lam[0], n_lat, False)
        o_b, st_b = _ret_fwd_call(qr, kr, u, off[RET_V], lam[1], n_lat, True)
        (a_na,) = gate_na_fwd((o_na, u), ())
        a_ret = _gate_ret_fwd_call(o_f, o_b, u, off[RET_Z] // w_retv, tm_row)
        wl_pna, wl_pret, wl_out = landed(l, 1, a_na), landed(l, 2, a_na), landed(l, 3, a_na)
        y_na = _matmul(a_na, wl_pna, out_dtype=F32, name="proj_na_fwd")
        y_ret = _matmul(a_ret, wl_pret, out_dtype=F32, name="proj_ret_fwd")
        (merged,) = merge_fwd((u, u, y_na, y_ret), ())
        out = _matmul(merged, wl_out, out_dtype=F32, name="out_proj_fwd")
        (xa_next,) = residual_fwd((xa, out), vecs[3:])
        saved.append(dict(xa=xa, vecs=vecs, vecs_vjp=vecs_vjp, h=h, w=(wl_in, wl_pna, wl_pret, wl_out), u=u, qr=qr, kr=kr,
                          rotary_vjp=rotary_vjp, bt=bt, bt_vjp=bt_vjp, lam=lam, lam_vjp=lam_vjp, o_na=o_na, o_f=o_f,
                          o_b=o_b, st_f=st_f, st_b=st_b, a_na=a_na, a_ret=a_ret, y_na=y_na, y_ret=y_ret,
                          merged=merged, out=out))
        xa = xa_next

    fg_pair, fg_vjp = jax.vjp(lambda g: pair(g, g), final_g)
    x_last = xa[:n_lat]
    (loss_rows,) = loss_fwd((x_last, loss_target[0]), (fg_pair,))
    loss = lax.psum(jnp.sum(loss_rows), ("x", "y", "c"))
    (dx_last,), (d_fg_pair,) = loss_bwd((x_last, loss_target[0]), (fg_pair,), (jnp.ones_like(loss_rows),))
    (d_final_g,) = fg_vjp(d_fg_pair)
    dxa = jnp.pad(dx_last, ((0, n_ctx), (0, 0)))

    d_mod_lat, d_mod_ctx, d_norm_g, d_rpb, d_decay = ([None] * depth for _ in range(5))
    for l in reversed(range(depth)):
        s = saved[l]
        u, qr, kr = s["u"], s["qr"], s["kr"]
        wl_in, wl_pna, wl_pret, wl_out = s["w"]
        (dxa_res, d_out), (d_gate,) = residual_bwd((s["xa"], s["out"]), s["vecs"][3:], (dxa,))
        send_dw(l, 3, _matmul(s["merged"], d_out, trans_a=True, out_dtype=BF16, name="out_proj_dw"))
        d_merged = _matmul(d_out, wl_out, trans_b=True, out_dtype=BF16, name="out_proj_da")
        (dg_na, dg_ret, dy_na, dy_ret), _ = merge_bwd((u, u, s["y_na"], s["y_ret"]), (), (d_merged,))
        send_dw(l, 2, _matmul(s["a_ret"], dy_ret, trans_a=True, out_dtype=BF16, name="proj_ret_dw"))
        da_ret = _matmul(dy_ret, wl_pret, trans_b=True, out_dtype=BF16, name="proj_ret_da")
        tok = send_dw(l, 1, _matmul(s["a_na"], dy_na, trans_a=True, out_dtype=BF16, name="proj_na_dw"))
        da_na = _matmul(dy_na, wl_pna, trans_b=True, out_dtype=BF16, name="proj_na_da", after=tok)
        do_ret, dz_ret = _gate_ret_bwd_call(s["o_f"], s["o_b"], u, off[RET_Z] // w_retv, da_ret, tm_row)
        (do_na, dz_na), _ = gate_na_bwd((s["o_na"], u), (), (da_na,))
        dq_f, dk_f, dv_f, dl_f = _ret_bwd_call(qr, kr, u, off[RET_V], s["lam"][0], s["st_f"], do_ret, n_lat, False)
        dq_b, dk_b, dv_b, dl_b = _ret_bwd_call(qr, kr, u, off[RET_V], s["lam"][1], s["st_b"], do_ret, n_lat, True)
        dq, dk, dv, dbt = _na_bwd_call(u, u, u, na_cols, w_na, s["bt"], do_na, n_lat)
        d_uq, d_uk = s["rotary_vjp"]((dq_f + dq_b, dk_f + dk_b))
        du = jnp.concatenate([dq, dk.astype(BF16), dv.astype(BF16), dz_na, d_uq, d_uk, dv_f + dv_b, dz_ret,
                              dg_na, dg_ret], axis=1)
        (d_rpb[l],) = s["bt_vjp"](dbt)
        (d_decay[l],) = s["lam_vjp"](jnp.stack([dl_f[:, 0, 0], dl_b[:, 0, 0]]))
        tok = send_dw(l, 0, _matmul(s["h"], du, trans_a=True, out_dtype=BF16, name="in_proj_dw"))
        dh = _matmul(du, wl_in, trans_b=True, out_dtype=BF16, name="in_proj_da", after=tok)
        (dxa,), d_vecs = norm_mod_bwd((s["xa"],), s["vecs"][:3], (dh,), acc=(dxa_res,))
        d_mod_lat[l], d_mod_ctx[l], d_norm_g[l] = s["vecs_vjp"](tuple(d_vecs) + (d_gate,))
    gx = dxa[:n_lat]
    d_mod_lat, d_mod_ctx, d_norm_g, d_rpb, d_decay = (jnp.stack(a) for a in (d_mod_lat, d_mod_ctx, d_norm_g, d_rpb, d_decay))

    small_shapes = [d_mod_lat.shape, d_mod_ctx.shape, d_norm_g.shape, d_final_g.shape, d_rpb.shape, d_decay.shape]
    packed = _pack([d_mod_lat, d_mod_ctx, d_norm_g, d_final_g, d_rpb, d_decay])
    g_all, g_sum = _small_allgather(packed, "allgather_small_grads")
    dml_sum, dmc_sum, grad_norm_g, grad_final_g, grad_na_rpb, grad_decay = _unpack(g_sum, small_shapes)
    grad_ada_b = dml_sum + dmc_sum
    dml_all = g_all.reshape(N_DEV, -1)[:, :depth * 3 * d].reshape(N_DEV, depth, 3 * d)

    def my_cols(a):
        return lax.dynamic_slice_in_dim(a, my_idx * ncol, ncol, axis=a.ndim - 1)

    dmod = jnp.concatenate(
        [my_cols(dml_all).transpose(1, 0, 2), my_cols(dmc_sum)[:, None, :], jnp.zeros((depth, 7, ncol), F32)], axis=1)
    grad_ada_w, dcs_part = _ada_bwd_call(cin, ada_w, dmod)
    _, dcs = _small_allgather(dcs_part, "allgather_dcsilu")
    sg = jax.nn.sigmoid(c_ctx)
    grad_c_ctx = dcs[8] * (sg * (1.0 + c_ctx * (1.0 - sg)))

    def flat2(a):
        return a.reshape(a.shape[0] * a.shape[1], a.shape[2])

    small_w = [c_ctx, ada_b, norm_g, na_rpb, ret_decay_logit, final_g]
    small_g = [grad_c_ctx, grad_ada_b, grad_norm_g, grad_na_rpb, grad_decay, grad_final_g]
    small_m = [m_c_ctx, m_ada_b, m_norm_g, m_na_rpb, m_ret_decay_logit, m_final_g]
    small_v = [v_c_ctx, v_ada_b, v_norm_g, v_na_rpb, v_ret_decay_logit, v_final_g]
    shp = [a.shape for a in small_w]
    ds_, nms_, nvs_ = _adamw_dense(_pack(small_w), _pack(small_g), _pack(small_m), _pack(small_v), "adamw_small")
    ds_, nms_, nvs_ = _unpack(ds_, shp), _unpack(nms_, shp), _unpack(nvs_, shp)

    d_ada, nm_ada, nv_ada = [a.reshape(ada_w.shape) for a in _adamw_dense(
        flat2(ada_w), flat2(grad_ada_w), flat2(m_ada_w), flat2(v_ada_w), "adamw_ada_w")]

    w_all = (w_in, w_proj_na, w_proj_ret, w_out)
    m_all = (m_w_in, m_w_proj_na, m_w_proj_ret, m_w_out)
    v_all = (v_w_in, v_w_proj_na, v_w_proj_ret, v_w_out)
    upd = [None] * 4
    after = d_ada
    for flight in scatters:
        mine, slabs = _push_wait(flight["name"], flight["sems"], flight["srcs"], flight["lands"], flight["axes"],
                                 flight["views"], flight["bits"], (after,))
        for (l, a), own, s in zip(flight["keys"], mine, slabs):
            upd[a] = _adamw_sharded(w_all[a], m_all[a], v_all[a], own, s, None if flight["by_chip"] else w_axes[a],
                                    idx_arr, l, upd[a], "adamw_" + w_names[a])
            after = upd[a][1]
    (g_w_in, d_w_in, nm_w_in, nv_w_in), (g_pna, d_pna, nm_pna, nv_pna) = upd[0], upd[1]
    (g_pret, d_pret, nm_pret, nv_pret), (g_out, d_out, nm_out, nv_out) = upd[2], upd[3]

    def order(cc, aw, ab, ng, wi, rp, dl, pn, pr, wo, fg):
        return [cc, aw, ab, ng, wi, rp, dl, pn, pr, wo, fg]

    grads_out = order(grad_c_ctx, grad_ada_w, grad_ada_b, grad_norm_g, g_w_in, grad_na_rpb, grad_decay, g_pna, g_pret, g_out, grad_final_g)
    delta_out = order(ds_[0], d_ada, ds_[1], ds_[2], d_w_in, ds_[3], ds_[4], d_pna, d_pret, d_out, ds_[5])
    m_out = order(nms_[0], nm_ada, nms_[1], nms_[2], nm_w_in, nms_[3], nms_[4], nm_pna, nm_pret, nm_out, nms_[5])
    v_out = order(nvs_[0], nv_ada, nvs_[1], nvs_[2], nv_w_in, nvs_[3], nvs_[4], nv_pna, nv_pret, nv_out, nvs_[5])
    return (loss, gx[None], *grads_out, *delta_out, *m_out, *v_out)
```

```python
import functools

import numpy as np
import jax
import jax.numpy as jnp
from jax import lax
from jax.experimental import pallas as pl
from jax.experimental.pallas import tpu as pltpu

F32 = jnp.float32
BF16 = jnp.bfloat16

N_DEV = 8
GRID_W = 64
NA_HEAD_DIM = 128
NA_WIN_ROWS = 8
NA_WIN_COLS = 16
RET_KEY_DIM = 128
RET_VAL_DIM = 256
RET_CHUNK = 128
ROPE_BASE = 10000.0
NORM_EPS = 1e-6
MASK_VALUE = -1e30

ADAM_LR = 0.001
ADAM_B1 = 0.9
ADAM_B2 = 0.999
ADAM_EPS = 1e-08
ADAM_WD = 0.01
ADAM_STEP = 10

VMEM_LIMIT = 48 * 1024 * 1024
MESH = pl.DeviceIdType.MESH
ANY = pl.BlockSpec(memory_space=pl.ANY)
VMEM_SPEC = pl.BlockSpec(memory_space=pltpu.VMEM)


def _params(sem=None):
    return pltpu.CompilerParams(dimension_semantics=sem, vmem_limit_bytes=VMEM_LIMIT)


def _pick(n, prefs):
    for p in prefs:
        if n % p == 0:
            return p
    return n


def _dot(a, b):
    return lax.dot_general(a, b, (((1,), (0,)), ((), ())), preferred_element_type=F32)


def _dot_nt(a, b):
    return lax.dot_general(a, b, (((1,), (1,)), ((), ())), preferred_element_type=F32)


def _dot_tn(a, b):
    return lax.dot_general(a, b, (((0,), (0,)), ((), ())), preferred_element_type=F32)


def _silu(x):
    return x * jax.nn.sigmoid(x)


def _matmul(a, b, *, trans_a=False, trans_b=False, out_dtype=F32, name="matmul", after=None):
    if trans_a:
        kdim, m = a.shape
    else:
        m, kdim = a.shape
    if trans_b:
        n, kb = b.shape
    else:
        kb, n = b.shape
    assert kdim == kb, (a.shape, b.shape, trans_a, trans_b)
    tm = _pick(m, (1152, 1024, 768, 512, 256, 128))
    tn = _pick(n, (1024, 512, 256, 128) if trans_b else (512, 256, 128))
    tk = _pick(kdim, (2304, 2048, 1024, 512, 256, 128))
    nk = kdim // tk
    dn = (((0 if trans_a else 1,), (1 if trans_b else 0,)), ((), ()))

    def body(a_ref, b_ref, *rest):
        o_ref, acc_ref = rest[-2:]
        part = lax.dot_general(a_ref[...], b_ref[...], dn, preferred_element_type=F32)
        if nk == 1:
            o_ref[...] = part.astype(o_ref.dtype)
        else:
            k = pl.program_id(2)

            @pl.when(k == 0)
            def _():
                acc_ref[...] = part

            @pl.when(k > 0)
            def _():
                acc_ref[...] += part

            @pl.when(k == nk - 1)
            def _():
                o_ref[...] = acc_ref[...].astype(o_ref.dtype)

    a_spec = pl.BlockSpec((tk, tm), lambda i, j, k: (k, i)) if trans_a else pl.BlockSpec((tm, tk), lambda i, j, k: (i, k))
    b_spec = pl.BlockSpec((tn, tk), lambda i, j, k: (j, k)) if trans_b else pl.BlockSpec((tk, tn), lambda i, j, k: (k, j))
    return pl.pallas_call(
        body,
        name=name,
        grid=(m // tm, n // tn, nk),
        in_specs=[a_spec, b_spec] + ([] if after is None else [ANY]),
        out_specs=pl.BlockSpec((tm, tn), lambda i, j, k: (i, j)),
        out_shape=jax.ShapeDtypeStruct((m, n), out_dtype),
        scratch_shapes=[pltpu.VMEM((tm, tn) if nk > 1 else (8, 128), F32)],
        compiler_params=_params(("parallel", "parallel", "arbitrary")),
    )(*((a, b) if after is None else (a, b, after)))


def _make_rowwise(f, name, out_dtypes, out_cols, n_lat, tm, diff_rows, col_blocks=None, drow_dtypes=None):
    drow_dtypes = drow_dtypes or {}

    def tile_fn(*args):
        return tuple(o.astype(dt) for o, dt in zip(f(*args), out_dtypes))

    def row_spec(k, arr):
        width, index = (col_blocks or {}).get(k, (arr.shape[1], 0))
        return pl.BlockSpec((tm, width), lambda i: (i, index))

    def row_width(k, arr):
        return (col_blocks or {}).get(k, (arr.shape[1], 0))[0]

    def fwd_call(rows, vecs):
        t = rows[0].shape[0]
        nr, nv = len(rows), len(vecs)
        nl = n_lat // tm

        def body(*refs):
            grp = (pl.program_id(0) >= nl).astype(jnp.int32)
            args = [r[...] for r in refs[:nr]] + [v[grp] for v in refs[nr:nr + nv]]
            for o_ref, o in zip(refs[nr + nv:], tile_fn(*args)):
                o_ref[...] = o

        return pl.pallas_call(
            body,
            name=name + "_fwd",
            grid=(t // tm,),
            in_specs=[row_spec(k, r) for k, r in enumerate(rows)]
            + [pl.BlockSpec(v.shape, lambda i: (0, 0, 0)) for v in vecs],
            out_specs=[pl.BlockSpec((tm, c), lambda i: (i, 0)) for c in out_cols],
            out_shape=[jax.ShapeDtypeStruct((t, c), dt) for c, dt in zip(out_cols, out_dtypes)],
            compiler_params=_params(("parallel",)),
        )(*rows, *vecs)

    def bwd_call(rows, vecs, gs, acc=None):
        t = rows[0].shape[0]
        nr, nv, ng = len(rows), len(vecs), len(gs)
        nl = n_lat // tm
        nd = len(diff_rows)
        acc = [None] * nd if acc is None else list(acc)
        acc_in = [a for a in acc if a is not None]

        def body(*refs):
            i = pl.program_id(0)
            grp = (i >= nl).astype(jnp.int32)
            args = [r[...] for r in refs[:nr]] + [v[grp] for v in refs[nr:nr + nv]]
            g_refs = refs[nr + nv:nr + nv + ng]
            acc_refs = list(refs[nr + nv + ng:nr + nv + ng + len(acc_in)])
            drow_refs = refs[nr + nv + ng + len(acc_in):nr + nv + ng + len(acc_in) + nd]
            dvec_refs = refs[nr + nv + ng + len(acc_in) + nd:]
            _, vjp = jax.vjp(tile_fn, *args)
            grads = vjp(tuple(g[...] for g in g_refs))
            for d_ref, k, a in zip(drow_refs, diff_rows, acc):
                gk = grads[k] if a is None else grads[k] + acc_refs.pop(0)[...]
                d_ref[...] = gk.astype(d_ref.dtype)

            @pl.when(i == 0)
            def _():
                for d_ref in dvec_refs:
                    d_ref[...] = jnp.zeros_like(d_ref)

            for j, d_ref in enumerate(dvec_refs):
                d_ref[grp] += grads[nr + j]

        outs = pl.pallas_call(
            body,
            name=name + "_bwd",
            grid=(t // tm,),
            in_specs=[row_spec(k, r) for k, r in enumerate(rows)]
            + [pl.BlockSpec(v.shape, lambda i: (0, 0, 0)) for v in vecs]
            + [pl.BlockSpec((tm, g.shape[1]), lambda i: (i, 0)) for g in gs]
            + [pl.BlockSpec((tm, a.shape[1]), lambda i: (i, 0)) for a in acc_in],
            out_specs=[pl.BlockSpec((tm, row_width(k, rows[k])), lambda i: (i, 0)) for k in diff_rows]
            + [pl.BlockSpec(v.shape, lambda i: (0, 0, 0)) for v in vecs],
            out_shape=[jax.ShapeDtypeStruct((t, row_width(k, rows[k])), drow_dtypes.get(k, rows[k].dtype))
                       for k in diff_rows]
            + [jax.ShapeDtypeStruct(v.shape, F32) for v in vecs],
            compiler_params=_params(("arbitrary",)),
        )(*rows, *vecs, *gs, *acc_in)
        return outs[:nd], outs[nd:]

    return fwd_call, bwd_call


def _f_norm_mod(x, g, scale, shift):
    r = lax.rsqrt(jnp.mean(x * x, axis=-1, keepdims=True) + NORM_EPS)
    return ((x * r * g) * (1.0 + scale) + shift,)


def _f_gate_na(o, z):
    return (o.astype(F32) * _silu(z.astype(F32)),)


def _f_merge(g_na, g_ret, y_na, y_ret):
    return (jax.nn.sigmoid(g_na.astype(F32)) * y_na + jax.nn.sigmoid(g_ret.astype(F32)) * y_ret,)


def _f_residual(x, out, gate):
    return (x + gate * out,)


def _f_loss(x, target, g):
    r = lax.rsqrt(jnp.mean(x * x, axis=-1, keepdims=True) + NORM_EPS)
    y = x * r * g
    e = 0.5 * jnp.mean(jnp.square(y - target), axis=-1, keepdims=True)
    return (jnp.broadcast_to(e * (1.0 / 128.0), (x.shape[0], 128)),)


def _gate_ret_fwd_call(of, ob, z, zblk, tm):
    t, w = of.shape
    nh = w // RET_VAL_DIM

    def body(of_ref, ob_ref, z_ref, a_ref):
        for hh in range(nh):
            sl = slice(hh * RET_VAL_DIM, (hh + 1) * RET_VAL_DIM)
            o = of_ref[:, sl] + ob_ref[:, sl]
            r = lax.rsqrt(jnp.mean(o * o, axis=-1, keepdims=True) + NORM_EPS)
            a_ref[:, sl] = ((o * r) * _silu(z_ref[:, sl].astype(F32))).astype(a_ref.dtype)

    spec = pl.BlockSpec((tm, w), lambda i: (i, 0))
    zspec = pl.BlockSpec((tm, w), lambda i: (i, zblk))
    return pl.pallas_call(
        body, name="gate_ret_fwd", grid=(t // tm,), in_specs=[spec, spec, zspec], out_specs=spec,
        out_shape=jax.ShapeDtypeStruct((t, w), BF16), compiler_params=_params(("parallel",)),
    )(of, ob, z)


def _gate_ret_bwd_call(of, ob, z, zblk, da, tm):
    t, w = of.shape
    nh = w // RET_VAL_DIM

    def body(of_ref, ob_ref, z_ref, da_ref, do_ref, dz_ref):
        for hh in range(nh):
            sl = slice(hh * RET_VAL_DIM, (hh + 1) * RET_VAL_DIM)
            o = of_ref[:, sl] + ob_ref[:, sl]
            r = lax.rsqrt(jnp.mean(o * o, axis=-1, keepdims=True) + NORM_EPS)
            n = o * r
            zf = z_ref[:, sl].astype(F32)
            sg = jax.nn.sigmoid(zf)
            g = da_ref[:, sl].astype(F32)
            dn = g * (zf * sg)
            dz_ref[:, sl] = (g * n * (sg * (1.0 + zf * (1.0 - sg)))).astype(dz_ref.dtype)
            do_ref[:, sl] = r * (dn - n * jnp.mean(dn * n, axis=-1, keepdims=True))

    spec = pl.BlockSpec((tm, w), lambda i: (i, 0))
    zspec = pl.BlockSpec((tm, w), lambda i: (i, zblk))
    return pl.pallas_call(
        body, name="gate_ret_bwd", grid=(t // tm,), in_specs=[spec, spec, zspec, spec], out_specs=[spec, spec],
        out_shape=[jax.ShapeDtypeStruct((t, w), F32), jax.ShapeDtypeStruct((t, w), z.dtype)],
        compiler_params=_params(("parallel",)),
    )(of, ob, z, da)


NA_PAIR = 2 * GRID_W
NA_KEY_ROWS = NA_WIN_ROWS + 2
NA_CLASSES = 5


def _na_geometry(t, n_lat):
    rows = n_lat // GRID_W
    assert rows % 2 == 0 and rows >= NA_KEY_ROWS + 2, rows
    return rows, rows // 2, NA_KEY_ROWS * GRID_W, t - n_lat, t // NA_PAIR


def _na_base(p, rows):
    return jnp.clip(2 * p - NA_WIN_ROWS // 2, 0, rows - NA_KEY_ROWS)


def _na_class(p, rows):
    return p - _na_base(p, rows) // 2


def _na_group(pairs, n_ctx):
    assert n_ctx % NA_PAIR == 0, n_ctx
    return 2 if pairs % 2 == 0 and (n_ctx // NA_PAIR) % 2 == 0 else 1


def _na_bias_spec(i, grp, rows, pairs, n_loc):
    return pl.BlockSpec((1, 1, NA_PAIR, n_loc),
                        lambda h, g: (h, _na_class(jnp.minimum(g * grp + i, pairs - grp + i), rows), 0, 0))


def _na_fwd_call(q, k, v, col0, w, bt, n_lat):
    t = q.shape[0]
    nh = w // NA_HEAD_DIM
    rows, pairs, n_loc, n_ctx, nq = _na_geometry(t, n_lat)
    grp = _na_group(pairs, n_ctx)
    scale = NA_HEAD_DIM ** -0.5

    def body(q_ref, k_ref, v_ref, *rest):
        bt_refs, o_ref = rest[:grp], rest[grp]
        g = pl.program_id(1)
        kc = k_ref[pl.ds(n_lat, n_ctx), :]
        vc = v_ref[pl.ds(n_lat, n_ctx), :]

        @pl.when(g < pairs // grp)
        def _():
            for i in range(grp):
                sl = slice(i * NA_PAIR, (i + 1) * NA_PAIR)
                qb = q_ref[sl, :]
                s_ctx = _dot_nt(qb, kc) * scale
                start = pl.multiple_of(_na_base(g * grp + i, rows) * GRID_W, GRID_W)
                kw = k_ref[pl.ds(start, n_loc), :]
                vw = v_ref[pl.ds(start, n_loc), :]
                s_loc = _dot_nt(qb, kw) * scale + bt_refs[i][0, 0]
                m = jnp.maximum(jnp.max(s_loc, axis=-1, keepdims=True), jnp.max(s_ctx, axis=-1, keepdims=True))
                p_loc = jnp.exp(s_loc - m)
                p_ctx = jnp.exp(s_ctx - m)
                l = jnp.sum(p_loc, axis=-1, keepdims=True) + jnp.sum(p_ctx, axis=-1, keepdims=True)
                o = _dot(p_loc.astype(BF16), vw) + _dot(p_ctx.astype(BF16), vc)
                o_ref[sl, :] = (o / l).astype(o_ref.dtype)

        @pl.when(g >= pairs // grp)
        def _():
            s_ctx = _dot_nt(q_ref[...], kc) * scale
            m = jnp.max(s_ctx, axis=-1, keepdims=True)
            p = jnp.exp(s_ctx - m)
            l = jnp.sum(p, axis=-1, keepdims=True)
            o_ref[...] = (_dot(p.astype(BF16), vc) / l).astype(o_ref.dtype)

    qspec = pl.BlockSpec((grp * NA_PAIR, NA_HEAD_DIM), lambda h, g: (g, h))
    in_q = pl.BlockSpec((grp * NA_PAIR, NA_HEAD_DIM), lambda h, g: (g, col0[0] + h))
    in_k = pl.BlockSpec((t, NA_HEAD_DIM), lambda h, g: (0, col0[1] + h))
    in_v = pl.BlockSpec((t, NA_HEAD_DIM), lambda h, g: (0, col0[2] + h))
    return pl.pallas_call(
        body,
        name="na_attn_fwd",
        grid=(nh, nq // grp),
        in_specs=[in_q, in_k, in_v] + [_na_bias_spec(i, grp, rows, pairs, n_loc) for i in range(grp)],
        out_specs=qspec,
        out_shape=jax.ShapeDtypeStruct((t, w), BF16),
        compiler_params=_params(("parallel", "arbitrary")),
    )(q, k, v, *([bt] * grp))


def _na_bwd_call(q, k, v, col0, w, bt, do, n_lat):
    t = q.shape[0]
    nh = w // NA_HEAD_DIM
    rows, pairs, n_loc, n_ctx, nq = _na_geometry(t, n_lat)
    scale = NA_HEAD_DIM ** -0.5
    grp = _na_group(pairs, n_ctx)

    def body(q_ref, k_ref, v_ref, do_ref, *rest):
        bt_refs = rest[:grp]
        dq_ref, dk_ref, dv_ref = rest[2 * grp:2 * grp + 3]
        dbt_refs = rest[2 * grp + 3:]
        g = pl.program_id(1)

        @pl.when(g == 0)
        def _():
            dk_ref[...] = jnp.zeros_like(dk_ref)
            dv_ref[...] = jnp.zeros_like(dv_ref)

        kc = k_ref[pl.ds(n_lat, n_ctx), :]
        vc = v_ref[pl.ds(n_lat, n_ctx), :]

        @pl.when(g < pairs // grp)
        def _():
            for i in range(grp):
                p = g * grp + i
                sl = slice(i * NA_PAIR, (i + 1) * NA_PAIR)
                qb = q_ref[sl, :]
                dob = do_ref[sl, :]
                s_ctx = _dot_nt(qb, kc) * scale
                dp_ctx = _dot_nt(dob, vc)
                start = pl.multiple_of(_na_base(p, rows) * GRID_W, GRID_W)
                kw = k_ref[pl.ds(start, n_loc), :]
                vw = v_ref[pl.ds(start, n_loc), :]
                s_loc = _dot_nt(qb, kw) * scale + bt_refs[i][0, 0]
                m = jnp.maximum(jnp.max(s_loc, axis=-1, keepdims=True), jnp.max(s_ctx, axis=-1, keepdims=True))
                p_loc = jnp.exp(s_loc - m)
                p_ctx = jnp.exp(s_ctx - m)
                inv = 1.0 / (jnp.sum(p_loc, axis=-1, keepdims=True) + jnp.sum(p_ctx, axis=-1, keepdims=True))
                p_loc = p_loc * inv
                p_ctx = p_ctx * inv
                dp_loc = _dot_nt(dob, vw)
                delta = (jnp.sum(p_loc * dp_loc, axis=-1, keepdims=True)
                         + jnp.sum(p_ctx * dp_ctx, axis=-1, keepdims=True))
                ds_loc = p_loc * (dp_loc - delta)
                ds_ctx = p_ctx * (dp_ctx - delta)
                first = jnp.logical_or(g == 0, _na_class(p, rows) != _na_class(p - grp, rows))
                dbt_ref = dbt_refs[i]

                @pl.when(first)
                def _():
                    dbt_ref[0, 0] = ds_loc

                @pl.when(jnp.logical_not(first))
                def _():
                    dbt_ref[0, 0] += ds_loc

                dsl = (ds_loc * scale).astype(BF16)
                dsc = (ds_ctx * scale).astype(BF16)
                dq_ref[sl, :] = (_dot(dsl, kw) + _dot(dsc, kc)).astype(dq_ref.dtype)
                dk_ref[pl.ds(start, n_loc), :] += _dot_tn(dsl, qb)
                dv_ref[pl.ds(start, n_loc), :] += _dot_tn(p_loc.astype(BF16), dob)
                dk_ref[pl.ds(n_lat, n_ctx), :] += _dot_tn(dsc, qb)
                dv_ref[pl.ds(n_lat, n_ctx), :] += _dot_tn(p_ctx.astype(BF16), dob)

        @pl.when(g >= pairs // grp)
        def _():
            qb = q_ref[...]
            dob = do_ref[...]
            s_ctx = _dot_nt(qb, kc) * scale
            dp_ctx = _dot_nt(dob, vc)
            m = jnp.max(s_ctx, axis=-1, keepdims=True)
            p = jnp.exp(s_ctx - m)
            p = p * (1.0 / jnp.sum(p, axis=-1, keepdims=True))
            delta = jnp.sum(p * dp_ctx, axis=-1, keepdims=True)
            dsc = (p * (dp_ctx - delta) * scale).astype(BF16)
            dq_ref[...] = _dot(dsc, kc).astype(dq_ref.dtype)
            dk_ref[pl.ds(n_lat, n_ctx), :] += _dot_tn(dsc, qb)
            dv_ref[pl.ds(n_lat, n_ctx), :] += _dot_tn(p.astype(BF16), dob)

    qspec = pl.BlockSpec((grp * NA_PAIR, NA_HEAD_DIM), lambda h, g: (g, h))
    kspec = pl.BlockSpec((t, NA_HEAD_DIM), lambda h, g: (0, h))
    bspecs = [_na_bias_spec(i, grp, rows, pairs, n_loc) for i in range(grp)]
    zeros = [jnp.zeros(bt.shape, F32) for _ in range(grp)]
    outs = pl.pallas_call(
        body,
        name="na_attn_bwd",
        grid=(nh, nq // grp),
        in_specs=[pl.BlockSpec((grp * NA_PAIR, NA_HEAD_DIM), lambda h, g: (g, col0[0] + h)),
                  pl.BlockSpec((t, NA_HEAD_DIM), lambda h, g: (0, col0[1] + h)),
                  pl.BlockSpec((t, NA_HEAD_DIM), lambda h, g: (0, col0[2] + h)), qspec] + bspecs + [ANY] * grp,
        out_specs=[qspec, kspec, kspec] + bspecs,
        out_shape=[
            jax.ShapeDtypeStruct((t, w), BF16),
            jax.ShapeDtypeStruct((t, w), F32),
            jax.ShapeDtypeStruct((t, w), F32),
        ] + [jax.ShapeDtypeStruct(bt.shape, F32)] * grp,
        input_output_aliases={4 + grp + i: 3 + i for i in range(grp)},
        compiler_params=_params(("parallel", "arbitrary")),
    )(q, k, v, do, *([bt] * grp), *zeros)
    dbt = outs[3]
    for extra in outs[4:]:
        dbt = dbt + extra
    return outs[0], outs[1], outs[2], dbt


def _na_bias_table(rpb, rows):
    pairs = rows // 2
    nb = 2 * NA_WIN_COLS - 1
    nq = NA_KEY_ROWS // 2
    e1 = np.zeros((NA_CLASSES, 2, nq, 2, 2 * NA_WIN_ROWS - 1), np.float32)
    valid = np.zeros((NA_CLASSES, 2, nq, 2), bool)
    for cls, p in enumerate((0, 1, 2, pairs - 2, pairs - 1)):
        base = int(np.clip(2 * p - NA_WIN_ROWS // 2, 0, rows - NA_KEY_ROWS))
        assert p - base // 2 == cls, (rows, cls, p, base)
        for i in range(2):
            r = 2 * p + i
            r0 = int(np.clip(r - NA_WIN_ROWS // 2, 0, rows - NA_WIN_ROWS))
            for kk in range(NA_KEY_ROWS):
                if r0 <= base + kk < r0 + NA_WIN_ROWS:
                    valid[cls, i, kk // 2, kk % 2] = True
                    e1[cls, i, kk // 2, kk % 2, base + kk - r + NA_WIN_ROWS - 1] = 1.0
    cidx = np.arange(GRID_W)
    dc = np.clip(cidx[None, :] - cidx[:, None] + (NA_WIN_COLS - 1), 0, nb - 1)
    c0 = np.clip(cidx - NA_WIN_COLS // 2, 0, GRID_W - NA_WIN_COLS)
    col_in = (cidx[None, :] >= c0[:, None]) & (cidx[None, :] < c0[:, None] + NA_WIN_COLS)
    e2 = np.zeros((GRID_W, 2, GRID_W, 2, nb), np.float32)
    for par in range(2):
        e2[np.arange(GRID_W)[:, None], par, np.arange(GRID_W)[None, :], par, dc] = 1.0
    mask = valid[:, :, None, :, :, None] & col_in[None, None, :, None, None, :]
    t1 = jnp.einsum("hab,xiqpa->hxiqpb", rpb, jnp.asarray(e1), precision=lax.Precision.HIGHEST)
    t1 = t1.reshape(t1.shape[:4] + (2 * nb,))
    b = jnp.einsum("hxiqm,cwm->hxicqw", t1, jnp.asarray(e2.reshape(GRID_W, 2 * GRID_W, 2 * nb)),
                   precision=lax.Precision.HIGHEST)
    b = jnp.where(jnp.asarray(mask.reshape(NA_CLASSES, 2, GRID_W, nq, 2 * GRID_W))[None], b, MASK_VALUE)
    return b.reshape(rpb.shape[0], NA_CLASSES, NA_PAIR, NA_KEY_ROWS * GRID_W)


def _ret_decays(lam_s, reverse):
    c = RET_CHUNK
    ii = lax.broadcasted_iota(jnp.int32, (c, c), 0)
    jj = lax.broadcasted_iota(jnp.int32, (c, c), 1)
    d = (jj - ii) if reverse else (ii - jj)
    dpos = jnp.maximum(d.astype(F32), 0.0)
    mask = jnp.where(d >= 0, jnp.exp(dpos * lam_s), 0.0)
    pi = lax.broadcasted_iota(jnp.int32, (c, 1), 0).astype(F32)
    qpos = (c - pi) if reverse else (pi + 1.0)
    kpos = pi if reverse else (c - 1.0 - pi)
    qd = jnp.exp(qpos * lam_s)
    kd = jnp.exp(kpos * lam_s)
    g = jnp.exp(jnp.full((1, RET_VAL_DIM), c * lam_s, F32))
    return mask, dpos, qd, kd, qpos, kpos, g


def _ret_head_group(nh):
    return _pick(nh, (4, 2))


def _ret_chunk_of(t, nt, nl, reverse):
    return (nt - 1 - t) if reverse else (t + nl) % nt


def _ret_fwd_call(qr, kr, v, vcol, lam, n_lat, reverse):
    t = qr.shape[0]
    nh = qr.shape[1] // RET_KEY_DIM
    c = RET_CHUNK
    nt, nl = t // c, n_lat // c

    hg = _ret_head_group(nh)
    dk, dv = RET_KEY_DIM, RET_VAL_DIM

    def body(lam_ref, q_ref, k_ref, v_ref, o_ref, s_ref, state):
        hb, step = pl.program_id(0), pl.program_id(1)

        @pl.when(step == 0)
        def _():
            state[...] = jnp.zeros_like(state)

        for j in range(hg):
            mask, _, qd, kd, _, _, g = _ret_decays(lam_ref[hb * hg + j], reverse)
            q, k, vv = q_ref[:, j * dk:(j + 1) * dk], k_ref[:, j * dk:(j + 1) * dk], v_ref[:, j * dv:(j + 1) * dv]
            p = _dot_nt(q, k) * mask
            s = state[j]
            qs = (q.astype(F32) * qd).astype(BF16)
            o_ref[:, j * dv:(j + 1) * dv] = _dot(p.astype(BF16), vv) + _dot(qs, s.astype(BF16))
            s_ref[j, 0] = s
            ks = (k.astype(F32) * kd).astype(BF16)
            state[j] = s * g + _dot_tn(ks, vv)

    def cmap(hb, step, lam_ref):
        return (_ret_chunk_of(step, nt, nl, reverse), hb)

    def vmap(hb, step, lam_ref):
        return (_ret_chunk_of(step, nt, nl, reverse), vcol // (hg * dv) + hb)

    return pl.pallas_call(
        body,
        name="retention_rev_fwd" if reverse else "retention_fwd",
        grid_spec=pltpu.PrefetchScalarGridSpec(
            num_scalar_prefetch=1,
            grid=(nh // hg, nt),
            in_specs=[
                pl.BlockSpec((c, hg * dk), cmap),
                pl.BlockSpec((c, hg * dk), cmap),
                pl.BlockSpec((c, hg * dv), vmap),
            ],
            out_specs=[
                pl.BlockSpec((c, hg * dv), cmap),
                pl.BlockSpec((hg, 1, dk, dv), lambda hb, step, lam_ref: (hb, step, 0, 0)),
            ],
            scratch_shapes=[pltpu.VMEM((hg, dk, dv), F32)],
        ),
        out_shape=[
            jax.ShapeDtypeStruct((t, nh * RET_VAL_DIM), F32),
            jax.ShapeDtypeStruct((nh, nt, RET_KEY_DIM, RET_VAL_DIM), F32),
        ],
        compiler_params=_params(("parallel", "arbitrary")),
    )(lam, qr, kr, v)


def _ret_bwd_call(qr, kr, v, vcol, lam, states, do, n_lat, reverse):
    t = qr.shape[0]
    nh = qr.shape[1] // RET_KEY_DIM
    c = RET_CHUNK
    nt, nl = t // c, n_lat // c

    hg = _ret_head_group(nh)
    dk, dv = RET_KEY_DIM, RET_VAL_DIM

    def body(lam_ref, q_ref, k_ref, v_ref, s_ref, do_ref, dq_ref, dk_ref, dv_ref, dl_ref, dstate):
        hb, rstep = pl.program_id(0), pl.program_id(1)

        @pl.when(rstep == 0)
        def _():
            dstate[...] = jnp.zeros_like(dstate)
            dl_ref[...] = jnp.zeros_like(dl_ref)

        for j in range(hg):
            mask, dpos, qd, kd, qpos, kpos, g = _ret_decays(lam_ref[hb * hg + j], reverse)
            ksl, vsl = slice(j * dk, (j + 1) * dk), slice(j * dv, (j + 1) * dv)
            q, k, vv = q_ref[:, ksl], k_ref[:, ksl], v_ref[:, vsl]
            qf, kf = q.astype(F32), k.astype(F32)
            s = s_ref[j, 0]
            ds = dstate[j]
            dob = do_ref[:, vsl].astype(BF16)
            sb, dsb = s.astype(BF16), ds.astype(BF16)
            a = _dot_nt(q, k)
            p = a * mask
            dp = _dot_nt(dob, vv)
            da = dp * mask
            dab = da.astype(BF16)
            dqc = _dot_nt(dob, sb)
            dkc = _dot_nt(vv, dsb)
            qs = (qf * qd).astype(BF16)
            ks = (kf * kd).astype(BF16)
            dq_ref[:, ksl] = (_dot(dab, k) + dqc * qd).astype(dq_ref.dtype)
            dk_ref[:, ksl] = (_dot_tn(dab, q) + dkc * kd).astype(dk_ref.dtype)
            dv_ref[:, vsl] = (_dot_tn(p.astype(BF16), dob) + _dot(ks, dsb)).astype(dv_ref.dtype)
            terms = (
                jnp.sum(jnp.sum(da * a * dpos, axis=1, keepdims=True), axis=0, keepdims=True)
                + jnp.sum(jnp.sum(dqc * qf * (qd * qpos), axis=1, keepdims=True), axis=0, keepdims=True)
                + jnp.sum(jnp.sum(dkc * kf * (kd * kpos), axis=1, keepdims=True), axis=0, keepdims=True)
                + jnp.sum(jnp.sum(ds * s * (g * c), axis=1, keepdims=True), axis=0, keepdims=True)
            )
            dl_ref[j] += jnp.broadcast_to(terms, (8, 128))
            dstate[j] = ds * g + _dot_tn(qs, dob)

    def cmap(hb, rstep, lam_ref):
        return (_ret_chunk_of(nt - 1 - rstep, nt, nl, reverse), hb)

    def vmap(hb, rstep, lam_ref):
        return (_ret_chunk_of(nt - 1 - rstep, nt, nl, reverse), vcol // (hg * dv) + hb)

    return pl.pallas_call(
        body,
        name="retention_rev_bwd" if reverse else "retention_bwd",
        grid_spec=pltpu.PrefetchScalarGridSpec(
            num_scalar_prefetch=1,
            grid=(nh // hg, nt),
            in_specs=[
                pl.BlockSpec((c, hg * dk), cmap),
                pl.BlockSpec((c, hg * dk), cmap),
                pl.BlockSpec((c, hg * dv), vmap),
                pl.BlockSpec((hg, 1, dk, dv), lambda hb, rstep, lam_ref: (hb, nt - 1 - rstep, 0, 0)),
                pl.BlockSpec((c, hg * dv), cmap),
            ],
            out_specs=[
                pl.BlockSpec((c, hg * dk), cmap),
                pl.BlockSpec((c, hg * dk), cmap),
                pl.BlockSpec((c, hg * dv), cmap),
                pl.BlockSpec((hg, 8, 128), lambda hb, rstep, lam_ref: (hb, 0, 0)),
            ],
            scratch_shapes=[pltpu.VMEM((hg, dk, dv), F32)],
        ),
        out_shape=[
            jax.ShapeDtypeStruct(qr.shape, qr.dtype),
            jax.ShapeDtypeStruct(kr.shape, kr.dtype),
            jax.ShapeDtypeStruct((t, nh * dv), v.dtype),
            jax.ShapeDtypeStruct((nh, 8, 128), F32),
        ],
        compiler_params=_params(("parallel", "arbitrary")),
    )(lam, qr, kr, v, states, do)


def _rope_tables(t, n_lat):
    nf = RET_KEY_DIM // 4
    tok = np.arange(n_lat)
    inv_freq = (ROPE_BASE ** (-np.arange(nf, dtype=np.float32) / nf)).astype(np.float32)
    row = (tok // GRID_W).astype(np.float32)
    col = (tok % GRID_W).astype(np.float32)
    ang = np.concatenate([row[:, None] * inv_freq, col[:, None] * inv_freq], axis=-1).astype(np.float32)
    cos = np.ones((t, 2 * nf), np.float32)
    sin = np.zeros((t, 2 * nf), np.float32)
    cos[:n_lat] = np.cos(ang)
    sin[:n_lat] = np.sin(ang)
    return jnp.asarray(np.concatenate([cos, cos], axis=1)), jnp.asarray(np.concatenate([-sin, sin], axis=1))


def _rotate(x, cos2, sin2):
    return x * cos2 + pltpu.roll(x, RET_KEY_DIM // 2, 1) * sin2


def _rope_fwd_call(u, qblk, kblk, w_qk, cos2, sin2, k_scale, tm):
    t = u.shape[0]
    nh = w_qk // RET_KEY_DIM

    def body(q_ref, k_ref, c_ref, s_ref, qr_ref, kr_ref):
        c, s = c_ref[...], s_ref[...]
        for hh in range(nh):
            sl = slice(hh * RET_KEY_DIM, (hh + 1) * RET_KEY_DIM)
            qr_ref[:, sl] = _rotate(q_ref[:, sl].astype(F32), c, s).astype(qr_ref.dtype)
            kr_ref[:, sl] = (_rotate(k_ref[:, sl].astype(F32), c, s) * k_scale).astype(kr_ref.dtype)

    tab = pl.BlockSpec((tm, RET_KEY_DIM), lambda i: (i, 0))
    out = pl.BlockSpec((tm, w_qk), lambda i: (i, 0))
    return pl.pallas_call(
        body, name="rope_fwd", grid=(t // tm,),
        in_specs=[pl.BlockSpec((tm, w_qk), lambda i: (i, qblk)), pl.BlockSpec((tm, w_qk), lambda i: (i, kblk)), tab, tab],
        out_specs=[out, out], out_shape=[jax.ShapeDtypeStruct((t, w_qk), BF16)] * 2,
        compiler_params=_params(("parallel",)),
    )(u, u, cos2, sin2)


def _assemble_du_call(pieces, off, cos2, sin2, k_scale, tm):
    flat = [a for p in pieces for a in (p if isinstance(p, tuple) else (p,))]
    t = flat[0].shape[0]
    n = len(flat)

    def body(*refs):
        c, s = refs[n][...], -refs[n + 1][...]
        o_ref = refs[n + 2]
        it = iter(refs[:n])
        for blk, p in enumerate(pieces):
            lo = off[blk]
            if not isinstance(p, tuple):
                o_ref[:, lo:off[blk + 1]] = next(it)[...].astype(o_ref.dtype)
                continue
            fwd_ref, rev_ref = next(it), next(it)
            if blk == 6:
                o_ref[:, lo:off[blk + 1]] = (fwd_ref[...].astype(F32) + rev_ref[...].astype(F32)).astype(o_ref.dtype)
                continue
            mult = k_scale if blk == 5 else 1.0
            for hh in range((off[blk + 1] - lo) // RET_KEY_DIM):
                sl = slice(hh * RET_KEY_DIM, (hh + 1) * RET_KEY_DIM)
                dy = (fwd_ref[:, sl].astype(F32) + rev_ref[:, sl].astype(F32)) * mult
                o_ref[:, lo + hh * RET_KEY_DIM:lo + (hh + 1) * RET_KEY_DIM] = _rotate(dy, c, s).astype(o_ref.dtype)

    tab = pl.BlockSpec((tm, RET_KEY_DIM), lambda i: (i, 0))
    return pl.pallas_call(
        body, name="assemble_du", grid=(t // tm,),
        in_specs=[pl.BlockSpec((tm, a.shape[1]), lambda i: (i, 0)) for a in flat] + [tab, tab],
        out_specs=pl.BlockSpec((tm, off[-1]), lambda i: (i, 0)),
        out_shape=jax.ShapeDtypeStruct((t, off[-1]), BF16),
        compiler_params=_params(("parallel",)),
    )(*flat, cos2, sin2)


def _my_position():
    return lax.axis_index("x"), lax.axis_index("y"), lax.axis_index("c")


def _flip(pos, k):
    x, y, c = pos
    return (1 - x if k & 4 else x, 1 - y if k & 2 else y, 1 - c if k & 1 else c)


def _linear(pos):
    return 4 * pos[0] + 2 * pos[1] + pos[2]


def _slab(ref, axis, idx, size):
    start = pl.multiple_of(idx * size, size)
    return ref.at[pl.ds(start, size), :] if axis == 0 else ref.at[:, pl.ds(start, size)]


HBM_SPEC = pl.BlockSpec(memory_space=pltpu.HBM)
SEM_SPEC = pl.BlockSpec(memory_space=pltpu.SEMAPHORE)
DATAFLOW = pltpu.SideEffectType.DATAFLOW_SIDE_EFFECTING
PEER_BITS = (1, 2, 4, 6, 3, 5, 7)
GATHER_BITS = (1, 2, 4, 6)


def _in_hbm(a):
    return pltpu.with_memory_space_constraint(a, pltpu.HBM)


def _gather_views(me, k, a, src_refs, land_refs, axes):
    size = src_refs[a].shape[axes[a]]
    peer = _flip(me, k)
    return src_refs[a], _slab(land_refs[a], axes[a], _linear(me), size), _slab(land_refs[a], axes[a], _linear(peer), size)


def _scatter_views(me, k, a, src_refs, land_refs, axes):
    size = land_refs[a].shape[1 + axes[a]]
    peer = _flip(me, k)
    return _slab(src_refs[a], axes[a], _linear(peer), size), land_refs[a].at[k - 1], land_refs[a].at[k - 1]


CHIP_BITS = (0, 2, 4, 6)


def _chip_views(me, k, a, src_refs, land_refs, axes):
    j = CHIP_BITS.index(k)
    return src_refs[a].at[j], land_refs[a].at[j - 1], land_refs[a].at[j - 1]


def _pair_exchange(grads, axes, sizes):
    ns = len(grads)

    def slab_shape(a):
        s = grads[a].shape
        return (sizes[a], s[1]) if axes[a] == 0 else (s[0], sizes[a])

    def body(*refs):
        g_refs, p_refs = refs[:ns], refs[ns:2 * ns]
        send_sems, recv_sems = refs[2 * ns:]
        me = _my_position()
        sibling = _flip(me, 1)
        copies = []
        for j, kc in enumerate(CHIP_BITS):
            for a in range(ns):
                cp = pltpu.make_async_remote_copy(
                    src_ref=_slab(g_refs[a], axes[a], _linear(_flip(me, kc | 1)), sizes[a]), dst_ref=p_refs[a].at[j],
                    send_sem=send_sems.at[4 * a + j], recv_sem=recv_sems.at[4 * a + j],
                    device_id=sibling, device_id_type=MESH)
                cp.start()
                copies.append(cp)
        for cp in copies:
            cp.wait_recv()
        for cp in copies:
            cp.wait_send()

    return pl.pallas_call(
        body, name="scatter_pair_exchange", in_specs=[ANY] * ns, out_specs=[ANY] * ns,
        out_shape=[jax.ShapeDtypeStruct((4,) + slab_shape(a), grads[a].dtype) for a in range(ns)],
        scratch_shapes=[pltpu.SemaphoreType.DMA((4 * ns,)), pltpu.SemaphoreType.DMA((4 * ns,))],
        compiler_params=pltpu.CompilerParams(has_side_effects=True),
    )(*grads)


def _pair_add(grad, theirs, axis, chip_idx):
    _, r, c = theirs.shape
    tm = _pick(r, (256, 128, 64, 32, 16))
    if axis == 0:
        mine_spec = pl.BlockSpec((tm, c), lambda j, i, idx: (idx[j] * (r // tm) + i, 0))
    else:
        mine_spec = pl.BlockSpec((tm, c), lambda j, i, idx: (i, idx[j]))

    def body(idx_ref, mine_ref, theirs_ref, o_ref):
        o_ref[0] = (mine_ref[...].astype(F32) + theirs_ref[0].astype(F32)).astype(o_ref.dtype)

    spec = pl.BlockSpec((1, tm, c), lambda j, i, idx: (j, i, 0))
    return pl.pallas_call(
        body, name="scatter_pair_add",
        grid_spec=pltpu.PrefetchScalarGridSpec(
            num_scalar_prefetch=1, grid=(4, r // tm), in_specs=[mine_spec, spec], out_specs=spec),
        out_shape=jax.ShapeDtypeStruct(theirs.shape, theirs.dtype),
        compiler_params=_params(("parallel", "parallel")),
    )(chip_idx, grad, theirs)


def _slab_block(rows, cols, tm, axis):
    if axis == 0:
        return pl.BlockSpec((tm, cols), lambda i, idx: (idx[0] * (rows // tm) + i, 0))
    return pl.BlockSpec((tm, cols), lambda i, idx: (i, idx[0]))


def _place_shard(shard, land, axis, my_idx):
    r, c = shard.shape
    tm = _pick(r, (512, 256, 128, 64, 32, 16))

    def body(idx_ref, s_ref, land_ref, o_ref):
        o_ref[...] = s_ref[...]

    return pl.pallas_call(
        body, name="gather_place",
        grid_spec=pltpu.PrefetchScalarGridSpec(
            num_scalar_prefetch=1, grid=(r // tm,),
            in_specs=[pl.BlockSpec((tm, c), lambda i, idx: (i, 0)), ANY],
            out_specs=_slab_block(r, c, tm, axis)),
        out_shape=jax.ShapeDtypeStruct(land.shape, land.dtype),
        input_output_aliases={2: 0},
        compiler_params=_params(("parallel",)),
    )(my_idx, shard, land)


def _push_start(name, srcs, lands, axes, views, bits, deps):
    ns = len(srcs)

    def body(*refs):
        src_refs, land_refs = refs[:ns], refs[ns:2 * ns]
        send_sems, recv_sems = refs[2 * ns + len(deps):2 * ns + len(deps) + 2]
        token = refs[-1]
        me = _my_position()
        for k in bits:
            for a in range(ns):
                s, d, _ = views(me, k, a, src_refs, land_refs, axes)
                pltpu.make_async_remote_copy(
                    src_ref=s, dst_ref=d, send_sem=send_sems.at[7 * a + k - 1], recv_sem=recv_sems.at[7 * a + k - 1],
                    device_id=_flip(me, k), device_id_type=MESH).start()
        token[...] = jnp.zeros_like(token)

    thru = [pltpu.HBM(a.shape, a.dtype) for a in list(srcs) + list(lands)]
    outs = pl.pallas_call(
        body, name=name,
        in_specs=[HBM_SPEC] * (2 * ns) + [ANY] * len(deps),
        out_specs=[SEM_SPEC, SEM_SPEC] + [HBM_SPEC] * (2 * ns) + [VMEM_SPEC],
        out_shape=[pltpu.SemaphoreType.DMA((7 * ns,)), pltpu.SemaphoreType.DMA((7 * ns,))] + thru
        + [jax.ShapeDtypeStruct((8, 128), F32)],
        input_output_aliases={i: 2 + i for i in range(2 * ns)},
        compiler_params=pltpu.CompilerParams(has_side_effects=DATAFLOW),
    )(*[_in_hbm(a) for a in srcs], *[_in_hbm(a) for a in lands], *deps)
    return (outs[0], outs[1]), outs[2:2 + ns], outs[2 + ns:2 + 2 * ns], outs[-1]


def _gather_finish(lands, axes, sizes):
    ns = len(lands)
    chips = (2, 4, 6)

    def body(*refs):
        land_refs = refs[ns:2 * ns]
        send_sems, recv_sems = refs[2 * ns:]
        me = _my_position()
        sibling = _flip(me, 1)
        copies = []
        for j, kc in enumerate(chips):
            for a in range(ns):
                def slab_of(pos):
                    return _slab(land_refs[a], axes[a], _linear(pos), sizes[a])
                send = pltpu.make_async_remote_copy(
                    src_ref=slab_of(_flip(me, kc)), dst_ref=slab_of(_flip(me, kc)), send_sem=send_sems.at[3 * a + j],
                    recv_sem=recv_sems.at[3 * a + j], device_id=sibling, device_id_type=MESH)
                recv = pltpu.make_async_remote_copy(
                    src_ref=slab_of(_flip(me, kc)), dst_ref=slab_of(_flip(sibling, kc)), send_sem=send_sems.at[3 * a + j],
                    recv_sem=recv_sems.at[3 * a + j], device_id=sibling, device_id_type=MESH)
                send.start()
                copies.append((send, recv))
        for send, recv in copies:
            recv.wait_recv()
        for send, recv in copies:
            send.wait_send()

    return pl.pallas_call(
        body, name="gather_finish", in_specs=[ANY] * ns, out_specs=[ANY] * ns,
        out_shape=[jax.ShapeDtypeStruct(l.shape, l.dtype) for l in lands],
        input_output_aliases={a: a for a in range(ns)},
        scratch_shapes=[pltpu.SemaphoreType.DMA((3 * ns,)), pltpu.SemaphoreType.DMA((3 * ns,))],
        compiler_params=pltpu.CompilerParams(has_side_effects=True),
    )(*lands)


def _push_wait(name, sems, srcs, lands, axes, views, bits, after):
    ns = len(srcs)

    def body(*refs):
        src_refs, land_refs = refs[:ns], refs[ns:2 * ns]
        send_sems, recv_sems = refs[2 * ns:2 * ns + 2]
        me = _my_position()
        for k in bits:
            for a in range(ns):
                s, d, got = views(me, k, a, src_refs, land_refs, axes)
                cp = pltpu.make_async_remote_copy(
                    src_ref=s, dst_ref=got, send_sem=send_sems.at[7 * a + k - 1], recv_sem=recv_sems.at[7 * a + k - 1],
                    device_id=_flip(me, k), device_id_type=MESH)
                cp.wait_send()
                cp.wait_recv()

    thru = [pltpu.HBM(a.shape, a.dtype) for a in list(srcs) + list(lands)]
    outs = pl.pallas_call(
        body, name=name,
        in_specs=[HBM_SPEC] * (2 * ns) + [SEM_SPEC, SEM_SPEC] + [ANY] * len(after),
        out_specs=[HBM_SPEC] * (2 * ns),
        out_shape=thru,
        input_output_aliases={i: i for i in range(2 * ns)},
        compiler_params=pltpu.CompilerParams(has_side_effects=DATAFLOW),
    )(*srcs, *lands, sems[0], sems[1], *after)
    return outs[:ns], outs[ns:]


def _small_allgather(v, name):
    r, c = v.shape

    def body(v_ref, all_ref, sum_ref, send_sems, recv_sems):
        me = _my_position()
        all_ref[_linear(me)] = v_ref[...]
        copies = []
        for k in range(1, N_DEV):
            peer = _flip(me, k)
            copies.append(pltpu.make_async_remote_copy(
                src_ref=v_ref, dst_ref=all_ref.at[_linear(me)], send_sem=send_sems.at[k - 1], recv_sem=recv_sems.at[k - 1],
                device_id=peer, device_id_type=MESH))
        for cp in copies:
            cp.start()
        for k in range(1, N_DEV):
            peer = _flip(me, k)
            pltpu.make_async_remote_copy(
                src_ref=v_ref, dst_ref=all_ref.at[_linear(peer)], send_sem=send_sems.at[k - 1], recv_sem=recv_sems.at[k - 1],
                device_id=peer, device_id_type=MESH).wait_recv()
        for cp in copies:
            cp.wait_send()
        acc = all_ref[0]
        for d in range(1, N_DEV):
            acc = acc + all_ref[d]
        sum_ref[...] = acc

    return pl.pallas_call(
        body,
        name=name,
        in_specs=[VMEM_SPEC],
        out_specs=[VMEM_SPEC, VMEM_SPEC],
        out_shape=[jax.ShapeDtypeStruct((N_DEV, r, c), F32), jax.ShapeDtypeStruct((r, c), F32)],
        scratch_shapes=[pltpu.SemaphoreType.DMA((N_DEV - 1,)), pltpu.SemaphoreType.DMA((N_DEV - 1,))],
        compiler_params=pltpu.CompilerParams(has_side_effects=True, vmem_limit_bytes=VMEM_LIMIT),
    )(v)


def _ada_fwd_call(cin, ada_w, ada_b_cols):
    nl, d, ncol = ada_w.shape
    nrow = cin.shape[0]

    def body(c_ref, w_ref, b_ref, o_ref):
        cs = _silu(c_ref[...]).astype(BF16)
        for l in range(nl):
            o_ref[l] = _dot(cs, w_ref[l].astype(BF16)) + b_ref[l]

    return pl.pallas_call(
        body, name="ada_fwd", in_specs=[VMEM_SPEC] * 3, out_specs=VMEM_SPEC,
        out_shape=jax.ShapeDtypeStruct((nl, nrow, ncol), F32), compiler_params=_params(),
    )(cin, ada_w, ada_b_cols)


def _ada_bwd_call(cin, ada_w, dmod):
    nl, d, ncol = ada_w.shape
    nrow = cin.shape[0]

    def body(c_ref, w_ref, dm_ref, gw_ref, dcs_ref):
        cs = _silu(c_ref[...]).astype(BF16)
        acc = jnp.zeros((nrow, d), F32)
        for l in range(nl):
            dm = dm_ref[l].astype(BF16)
            gw_ref[l] = _dot_tn(cs, dm)
            acc = acc + _dot_nt(dm, w_ref[l].astype(BF16))
        dcs_ref[...] = acc

    return pl.pallas_call(
        body, name="ada_bwd", in_specs=[VMEM_SPEC] * 3, out_specs=[VMEM_SPEC, VMEM_SPEC],
        out_shape=[jax.ShapeDtypeStruct((nl, d, ncol), F32), jax.ShapeDtypeStruct((nrow, d), F32)],
        compiler_params=_params(),
    )(cin, ada_w, dmod)


def _adamw_math(w, g, m, v):
    m = ADAM_B1 * m + (1.0 - ADAM_B1) * g
    v = ADAM_B2 * v + (1.0 - ADAM_B2) * jnp.square(g)
    m_hat = m / (1.0 - ADAM_B1 ** ADAM_STEP)
    v_hat = v / (1.0 - ADAM_B2 ** ADAM_STEP)
    delta = -ADAM_LR * (m_hat / (jnp.sqrt(v_hat) + ADAM_EPS) + ADAM_WD * w)
    return delta, m, v


def _adamw_sharded(w, m, v, mine, slabs, axis, my_idx, layer, prev, name):
    nl, r, c = w.shape
    tm = _pick(r, (128, 64, 32, 16))
    nprev = 0 if prev is None else len(prev)
    nslab = slabs.shape[0]
    if axis is None:
        mine_spec = pl.BlockSpec((1, tm, c), lambda i, idx: (0, i, 0))
    else:
        mine_spec = _slab_block(r, c, tm, axis)

    def body(idx_ref, w_ref, m_ref, v_ref, mine_ref, s_ref, *rest):
        g_ref, d_ref, nm_ref, nv_ref = rest[nprev:]
        g = (mine_ref[0] if axis is None else mine_ref[...]).astype(F32)
        for k in range(nslab):
            g = g + s_ref[k].astype(F32)
        delta, nm, nv = _adamw_math(w_ref[0], g, m_ref[0], v_ref[0])
        g_ref[0], d_ref[0], nm_ref[0], nv_ref[0] = g, delta, nm, nv

    spec = pl.BlockSpec((1, tm, c), lambda i, idx: (layer, i, 0))
    out = jax.ShapeDtypeStruct(w.shape, F32)
    return pl.pallas_call(
        body, name=name,
        grid_spec=pltpu.PrefetchScalarGridSpec(
            num_scalar_prefetch=1, grid=(r // tm,),
            in_specs=[spec, spec, spec, mine_spec,
                      pl.BlockSpec((nslab, tm, c), lambda i, idx: (0, i, 0))] + [ANY] * nprev,
            out_specs=[spec] * 4),
        out_shape=[out] * 4,
        input_output_aliases={6 + j: j for j in range(nprev)},
        compiler_params=_params(("parallel",)),
    )(my_idx, w, m, v, mine, slabs, *(() if prev is None else prev))


def _adamw_dense(w, g, m, v, name):
    r, c = w.shape
    tm = _pick(r, (256, 128, 64, 32, 16, 8))

    def body(w_ref, g_ref, m_ref, v_ref, d_ref, nm_ref, nv_ref):
        d_ref[...], nm_ref[...], nv_ref[...] = _adamw_math(w_ref[...], g_ref[...], m_ref[...], v_ref[...])

    spec = pl.BlockSpec((tm, c), lambda i: (i, 0))
    out = jax.ShapeDtypeStruct(w.shape, F32)
    return pl.pallas_call(
        body, name=name, grid=(r // tm,), in_specs=[spec] * 4, out_specs=[spec] * 3, out_shape=[out] * 3,
        compiler_params=_params(("parallel",)),
    )(w, g, m, v)


def _pack(parts, width=128):
    flat = jnp.concatenate([p.reshape(-1).astype(F32) for p in parts])
    n = flat.shape[0]
    total = -(-n // (8 * width)) * (8 * width)
    return jnp.pad(flat, (0, total - n)).reshape(total // width, width)


def _unpack(buf, shapes):
    flat = buf.reshape(-1)
    out, off = [], 0
    for s in shapes:
        n = int(np.prod(s))
        out.append(flat[off:off + n].reshape(s))
        off += n
    return out


def kernel(x, c, ctx, c_ctx, ada_w, ada_b, norm_g, w_in, na_rpb, ret_decay_logit, w_proj_na, w_proj_ret, w_out, final_g, loss_target, m_c_ctx, m_ada_w, m_ada_b, m_norm_g, m_w_in, m_na_rpb, m_ret_decay_logit, m_w_proj_na, m_w_proj_ret, m_w_out, m_final_g, v_c_ctx, v_ada_w, v_ada_b, v_norm_g, v_w_in, v_na_rpb, v_ret_decay_logit, v_w_proj_na, v_w_proj_ret, v_w_out, v_final_g):
    depth = w_in.shape[0]
    n_lat, d = x.shape[1], x.shape[2]
    n_ctx = ctx.shape[1]
    t = n_lat + n_ctx
    w_na = w_proj_na.shape[1]
    w_retv = w_proj_ret.shape[1] * N_DEV
    in_cols = w_in.shape[2] * N_DEV
    w_qk = (in_cols - 4 * w_na - 2 * w_retv - 2 * d) // 2
    sizes = (w_na, w_na, w_na, w_na, w_qk, w_qk, w_retv, w_retv, d, d)
    off = tuple(int(o) for o in np.cumsum((0,) + sizes))
    NA_Q, NA_K, NA_V, NA_Z, RET_Q, RET_K, RET_V, RET_Z, G_NA, G_RET = range(10)
    rows = n_lat // GRID_W
    me = _my_position()
    my_idx = _linear(me)
    tm_row = _pick(n_ctx, (256, 128))

    idx_arr = jnp.reshape(my_idx, (1,)).astype(jnp.int32)
    chip_idx = jnp.stack([_linear(_flip(me, kc)) for kc in CHIP_BITS]).astype(jnp.int32)

    ncol = ada_w.shape[2]
    c_all, _ = _small_allgather(jnp.pad(c, ((0, 7), (0, 0))), "allgather_c")
    cin = jnp.concatenate([c_all[:, 0, :], c_ctx[None, :], jnp.zeros((7, d), F32)], axis=0)
    ada_b_cols = lax.dynamic_slice_in_dim(ada_b, my_idx * ncol, ncol, axis=1)[:, None, :]
    mod_cols = _ada_fwd_call(cin, ada_w, ada_b_cols)
    mod_gathered, _ = _small_allgather(mod_cols.reshape(depth * 16, ncol), "allgather_mod")
    mod_all = mod_gathered.reshape(N_DEV, depth, 16, ncol).transpose(1, 2, 0, 3).reshape(depth, 16, N_DEV * ncol)
    mod_lat = lax.dynamic_index_in_dim(mod_all, my_idx, axis=1, keepdims=False)
    mod_ctx = mod_all[:, 8, :]

    w_axes = (1, 1, 0, 0)
    w_names = ("w_in", "w_proj_na", "w_proj_ret", "w_out")
    shard = [[w[l].astype(BF16) for w in (w_in, w_proj_na, w_proj_ret, w_out)] for l in range(depth)]
    groups = [[(0, 0)], [(0, 1), (0, 2), (0, 3)]] + [[(l, a) for a in range(4)] for l in range(1, depth)]
    gathers, token = {}, mod_gathered
    for gi, keys in enumerate(groups):
        srcs = [shard[l][a] for l, a in keys]
        axes = tuple(w_axes[a] for _, a in keys)
        lands = [_place_shard(s, lax.empty(tuple(n * (N_DEV if i == ax else 1) for i, n in enumerate(s.shape)), BF16),
                              ax, idx_arr) for s, ax in zip(srcs, axes)]
        sizes = tuple(s.shape[ax] for s, ax in zip(srcs, axes))
        sems, srcs, lands, token = _push_start(
            f"gather_start_{gi}", srcs, lands, axes, _gather_views, GATHER_BITS, (token,))
        flight = dict(name=f"gather_wait_{gi}", sems=sems, srcs=srcs, lands=lands, axes=axes, sizes=sizes, ready=None)
        for pos, key in enumerate(keys):
            gathers[key] = (flight, pos)

    def landed(l, a, act):
        flight, pos = gathers[(l, a)]
        if flight["ready"] is None:
            arrived = _push_wait(flight["name"], flight["sems"], flight["srcs"], flight["lands"],
                                 flight["axes"], _gather_views, GATHER_BITS, (act, token))[1]
            flight["ready"] = _gather_finish(arrived, flight["axes"], flight["sizes"])
        return flight["ready"][pos]

    pending, scatters = {}, []

    def send_dw(l, a, dw):
        pending[(l, a)] = dw
        if a == 0:
            keys = [(0, 0)] if l == 0 else [(l, b) for b in range(4)]
        elif l == 0 and a == 1:
            keys = [(0, 1), (0, 2), (0, 3)]
        else:
            return None
        srcs = [pending[k] for k in keys]
        axes = tuple(w_axes[b] for _, b in keys)
        sizes = tuple(s.shape[ax] // N_DEV for s, ax in zip(srcs, axes))
        slab_shapes = [tuple(n // (N_DEV if i == ax else 1) for i, n in enumerate(s.shape)) for s, ax in zip(srcs, axes)]
        by_chip = keys == [(0, 0)]
        if by_chip:
            theirs = _pair_exchange(srcs, axes, sizes)
            srcs = [_pair_add(g, p, ax, chip_idx) for g, p, ax in zip(srcs, theirs, axes)]
            lands = [lax.empty((3,) + shp, BF16) for shp in slab_shapes]
            views, bits = _chip_views, CHIP_BITS[1:]
        else:
            lands = [lax.empty((N_DEV - 1,) + shp, BF16) for shp in slab_shapes]
            views, bits = _scatter_views, PEER_BITS
        sems, srcs, lands, tok = _push_start(f"scatter_start_{len(scatters)}", srcs, lands, axes, views, bits, ())
        scatters.append(dict(name=f"scatter_wait_{len(scatters)}", sems=sems, srcs=srcs, lands=lands, axes=axes, keys=keys,
                             views=views, bits=bits, by_chip=by_chip))
        return tok

    cos2, sin2 = _rope_tables(t, n_lat)
    k_scale = RET_KEY_DIM ** -0.5
    assert off[RET_Q] % w_qk == 0 and off[RET_K] % w_qk == 0
    assert off[NA_Z] % w_na == 0 and off[G_NA] % d == 0 and off[G_RET] % d == 0 and off[RET_Z] % w_retv == 0
    assert off[RET_V] % (_ret_head_group(w_retv // RET_VAL_DIM) * RET_VAL_DIM) == 0
    na_cols = tuple(off[i] // NA_HEAD_DIM for i in (NA_Q, NA_K, NA_V))
    norm_mod_fwd, norm_mod_bwd = _make_rowwise(_f_norm_mod, "norm_mod", (BF16,), (d,), n_lat, tm_row, (0,))
    gate_na_fwd, gate_na_bwd = _make_rowwise(_f_gate_na, "gate_na", (BF16,), (w_na,), n_lat, tm_row, (0, 1),
                                             col_blocks={1: (w_na, off[NA_Z] // w_na)})
    merge_fwd, merge_bwd = _make_rowwise(_f_merge, "merge", (BF16,), (d,), n_lat, tm_row, (0, 1, 2, 3),
                                         col_blocks={0: (d, off[G_NA] // d), 1: (d, off[G_RET] // d)},
                                         drow_dtypes={2: BF16, 3: BF16})
    residual_fwd, residual_bwd = _make_rowwise(_f_residual, "residual", (F32,), (d,), n_lat, tm_row, (0, 1),
                                               drow_dtypes={1: BF16})
    loss_fwd, loss_bwd = _make_rowwise(_f_loss, "loss_head", (F32,), (128,), n_lat, tm_row, (0,))

    def pair(a, b):
        return jnp.stack([a, b])[:, None, :]

    def mod_vectors(mod_lat_l, mod_ctx_l, norm_g_l):
        shift, scale, gate = jnp.split(mod_lat_l, 3)
        c_shift, c_scale, c_gate = jnp.split(mod_ctx_l, 3)
        return pair(norm_g_l, norm_g_l), pair(scale, c_scale), pair(shift, c_shift), pair(gate, c_gate)

    def log_decay(logit):
        return jax.nn.log_sigmoid(logit.astype(F32))

    xa = jnp.concatenate([x[0], ctx[0]], axis=0)
    saved = []
    for l in range(depth):
        vecs, vecs_vjp = jax.vjp(mod_vectors, mod_lat[l], mod_ctx[l], norm_g[l])
        (h,) = norm_mod_fwd((xa,), vecs[:3])
        wl_in = landed(l, 0, h)
        u = _matmul(h, wl_in, out_dtype=BF16, name="in_proj_fwd")
        qr, kr = _rope_fwd_call(u, off[RET_Q] // w_qk, off[RET_K] // w_qk, w_qk, cos2, sin2, k_scale, tm_row)
        bt, bt_vjp = jax.vjp(lambda r: _na_bias_table(r, rows), na_rpb[l])
        lam, lam_vjp = jax.vjp(log_decay, ret_decay_logit[l])
        o_na = _na_fwd_call(u, u, u, na_cols, w_na, bt, n_lat)
        o_f, st_f = _ret_fwd_call(qr, kr, u, off[RET_V], lam[0], n_lat, False)
        o_b, st_b = _ret_fwd_call(qr, kr, u, off[RET_V], lam[1], n_lat, True)
        (a_na,) = gate_na_fwd((o_na, u), ())
        a_ret = _gate_ret_fwd_call(o_f, o_b, u, off[RET_Z] // w_retv, tm_row)
        wl_pna, wl_pret, wl_out = landed(l, 1, a_na), landed(l, 2, a_na), landed(l, 3, a_na)
        y_na = _matmul(a_na, wl_pna, out_dtype=F32, name="proj_na_fwd")
        y_ret = _matmul(a_ret, wl_pret, out_dtype=F32, name="proj_ret_fwd")
        (merged,) = merge_fwd((u, u, y_na, y_ret), ())
        out = _matmul(merged, wl_out, out_dtype=F32, name="out_proj_fwd")
        (xa_next,) = residual_fwd((xa, out), vecs[3:])
        saved.append(dict(xa=xa, vecs=vecs, vecs_vjp=vecs_vjp, h=h, w=(wl_in, wl_pna, wl_pret, wl_out), u=u, qr=qr, kr=kr,
                          bt=bt, bt_vjp=bt_vjp, lam=lam, lam_vjp=lam_vjp, o_na=o_na, o_f=o_f,
                          o_b=o_b, st_f=st_f, st_b=st_b, a_na=a_na, a_ret=a_ret, y_na=y_na, y_ret=y_ret,
                          merged=merged, out=out))
        xa = xa_next

    fg_pair, fg_vjp = jax.vjp(lambda g: pair(g, g), final_g)
    x_last = xa[:n_lat]
    (loss_rows,) = loss_fwd((x_last, loss_target[0]), (fg_pair,))
    loss = lax.psum(jnp.sum(loss_rows), ("x", "y", "c"))
    (dx_last,), (d_fg_pair,) = loss_bwd((x_last, loss_target[0]), (fg_pair,), (jnp.ones_like(loss_rows),))
    (d_final_g,) = fg_vjp(d_fg_pair)
    dxa = jnp.pad(dx_last, ((0, n_ctx), (0, 0)))

    d_mod_lat, d_mod_ctx, d_norm_g, d_rpb, d_decay = ([None] * depth for _ in range(5))
    for l in reversed(range(depth)):
        s = saved[l]
        u, qr, kr = s["u"], s["qr"], s["kr"]
        wl_in, wl_pna, wl_pret, wl_out = s["w"]
        (dxa_res, d_out), (d_gate,) = residual_bwd((s["xa"], s["out"]), s["vecs"][3:], (dxa,))
        send_dw(l, 3, _matmul(s["merged"], d_out, trans_a=True, out_dtype=BF16, name="out_proj_dw"))
        d_merged = _matmul(d_out, wl_out, trans_b=True, out_dtype=BF16, name="out_proj_da")
        (dg_na, dg_ret, dy_na, dy_ret), _ = merge_bwd((u, u, s["y_na"], s["y_ret"]), (), (d_merged,))
        send_dw(l, 2, _matmul(s["a_ret"], dy_ret, trans_a=True, out_dtype=BF16, name="proj_ret_dw"))
        da_ret = _matmul(dy_ret, wl_pret, trans_b=True, out_dtype=BF16, name="proj_ret_da")
        tok = send_dw(l, 1, _matmul(s["a_na"], dy_na, trans_a=True, out_dtype=BF16, name="proj_na_dw"))
        da_na = _matmul(dy_na, wl_pna, trans_b=True, out_dtype=BF16, name="proj_na_da", after=tok)
        do_ret, dz_ret = _gate_ret_bwd_call(s["o_f"], s["o_b"], u, off[RET_Z] // w_retv, da_ret, tm_row)
        (do_na, dz_na), _ = gate_na_bwd((s["o_na"], u), (), (da_na,))
        dq_f, dk_f, dv_f, dl_f = _ret_bwd_call(qr, kr, u, off[RET_V], s["lam"][0], s["st_f"], do_ret, n_lat, False)
        dq_b, dk_b, dv_b, dl_b = _ret_bwd_call(qr, kr, u, off[RET_V], s["lam"][1], s["st_b"], do_ret, n_lat, True)
        dq, dk, dv, dbt = _na_bwd_call(u, u, u, na_cols, w_na, s["bt"], do_na, n_lat)
        du = _assemble_du_call([dq, dk, dv, dz_na, (dq_f, dq_b), (dk_f, dk_b), (dv_f, dv_b), dz_ret, dg_na, dg_ret],
                               off, cos2, sin2, k_scale, _pick(n_ctx, (128,)))
        (d_rpb[l],) = s["bt_vjp"](dbt)
        (d_decay[l],) = s["lam_vjp"](jnp.stack([dl_f[:, 0, 0], dl_b[:, 0, 0]]))
        tok = send_dw(l, 0, _matmul(s["h"], du, trans_a=True, out_dtype=BF16, name="in_proj_dw"))
        dh = _matmul(du, wl_in, trans_b=True, out_dtype=BF16, name="in_proj_da", after=tok)
        (dxa,), d_vecs = norm_mod_bwd((s["xa"],), s["vecs"][:3], (dh,), acc=(dxa_res,))
        d_mod_lat[l], d_mod_ctx[l], d_norm_g[l] = s["vecs_vjp"](tuple(d_vecs) + (d_gate,))
    gx = dxa[:n_lat]
    d_mod_lat, d_mod_ctx, d_norm_g, d_rpb, d_decay = (jnp.stack(a) for a in (d_mod_lat, d_mod_ctx, d_norm_g, d_rpb, d_decay))

    small_shapes = [d_mod_lat.shape, d_mod_ctx.shape, d_norm_g.shape, d_final_g.shape, d_rpb.shape, d_decay.shape]
    packed = _pack([d_mod_lat, d_mod_ctx, d_norm_g, d_final_g, d_rpb, d_decay])
    g_all, g_sum = _small_allgather(packed, "allgather_small_grads")
    dml_sum, dmc_sum, grad_norm_g, grad_final_g, grad_na_rpb, grad_decay = _unpack(g_sum, small_shapes)
    grad_ada_b = dml_sum + dmc_sum
    dml_all = g_all.reshape(N_DEV, -1)[:, :depth * 3 * d].reshape(N_DEV, depth, 3 * d)

    def my_cols(a):
        return lax.dynamic_slice_in_dim(a, my_idx * ncol, ncol, axis=a.ndim - 1)

    dmod = jnp.concatenate(
        [my_cols(dml_all).transpose(1, 0, 2), my_cols(dmc_sum)[:, None, :], jnp.zeros((depth, 7, ncol), F32)], axis=1)
    grad_ada_w, dcs_part = _ada_bwd_call(cin, ada_w, dmod)
    _, dcs = _small_allgather(dcs_part, "allgather_dcsilu")
    sg = jax.nn.sigmoid(c_ctx)
    grad_c_ctx = dcs[8] * (sg * (1.0 + c_ctx * (1.0 - sg)))

    def flat2(a):
        return a.reshape(a.shape[0] * a.shape[1], a.shape[2])

    small_w = [c_ctx, ada_b, norm_g, na_rpb, ret_decay_logit, final_g]
    small_g = [grad_c_ctx, grad_ada_b, grad_norm_g, grad_na_rpb, grad_decay, grad_final_g]
    small_m = [m_c_ctx, m_ada_b, m_norm_g, m_na_rpb, m_ret_decay_logit, m_final_g]
    small_v = [v_c_ctx, v_ada_b, v_norm_g, v_na_rpb, v_ret_decay_logit, v_final_g]
    shp = [a.shape for a in small_w]
    ds_, nms_, nvs_ = _adamw_dense(_pack(small_w), _pack(small_g), _pack(small_m), _pack(small_v), "adamw_small")
    ds_, nms_, nvs_ = _unpack(ds_, shp), _unpack(nms_, shp), _unpack(nvs_, shp)

    d_ada, nm_ada, nv_ada = [a.reshape(ada_w.shape) for a in _adamw_dense(
        flat2(ada_w), flat2(grad_ada_w), flat2(m_ada_w), flat2(v_ada_w), "adamw_ada_w")]

    w_all = (w_in, w_proj_na, w_proj_ret, w_out)
    m_all = (m_w_in, m_w_proj_na, m_w_proj_ret, m_w_out)
    v_all = (v_w_in, v_w_proj_na, v_w_proj_ret, v_w_out)
    upd = [None] * 4
    after = d_ada
    for flight in scatters:
        mine, slabs = _push_wait(flight["name"], flight["sems"], flight["srcs"], flight["lands"], flight["axes"],
                                 flight["views"], flight["bits"], (after,))
        for (l, a), own, s in zip(flight["keys"], mine, slabs):
            upd[a] = _adamw_sharded(w_all[a], m_all[a], v_all[a], own, s, None if flight["by_chip"] else w_axes[a],
                                    idx_arr, l, upd[a], "adamw_" + w_names[a])
            after = upd[a][1]
    (g_w_in, d_w_in, nm_w_in, nv_w_in), (g_pna, d_pna, nm_pna, nv_pna) = upd[0], upd[1]
    (g_pret, d_pret, nm_pret, nv_pret), (g_out, d_out, nm_out, nv_out) = upd[2], upd[3]

    def order(cc, aw, ab, ng, wi, rp, dl, pn, pr, wo, fg):
        return [cc, aw, ab, ng, wi, rp, dl, pn, pr, wo, fg]

    grads_out = order(grad_c_ctx, grad_ada_w, grad_ada_b, grad_norm_g, g_w_in, grad_na_rpb, grad_decay, g_pna, g_pret, g_out, grad_final_g)
    delta_out = order(ds_[0], d_ada, ds_[1], ds_[2], d_w_in, ds_[3], ds_[4], d_pna, d_pret, d_out, ds_[5])
    m_out = order(nms_[0], nm_ada, nms_[1], nms_[2], nm_w_in, nms_[3], nms_[4], nm_pna, nm_pret, nm_out, nms_[5])
    v_out = order(nvs_[0], nv_ada, nvs_[1], nvs_[2], nv_w_in, nvs_[3], nvs_[4], nv_pna, nv_pret, nv_out, nvs_[5])
    return (loss, gx[None], *grads_out, *delta_out, *m_out, *v_out)
```

```python
import functools

import numpy as np
import jax
import jax.numpy as jnp
from jax import lax
from jax.experimental import pallas as pl
from jax.experimental.pallas import tpu as pltpu

F32 = jnp.float32
BF16 = jnp.bfloat16

N_DEV = 8
GRID_W = 64
NA_HEAD_DIM = 128
NA_WIN_ROWS = 8
NA_WIN_COLS = 16
RET_KEY_DIM = 128
RET_VAL_DIM = 256
RET_CHUNK = 128
ROPE_BASE = 10000.0
NORM_EPS = 1e-6
MASK_VALUE = -1e30

ADAM_LR = 0.001
ADAM_B1 = 0.9
ADAM_B2 = 0.999
ADAM_EPS = 1e-08
ADAM_WD = 0.01
ADAM_STEP = 10

VMEM_LIMIT = 48 * 1024 * 1024
MESH = pl.DeviceIdType.MESH
ANY = pl.BlockSpec(memory_space=pl.ANY)
VMEM_SPEC = pl.BlockSpec(memory_space=pltpu.VMEM)


def _params(sem=None):
    return pltpu.CompilerParams(dimension_semantics=sem, vmem_limit_bytes=VMEM_LIMIT)


def _pick(n, prefs):
    for p in prefs:
        if n % p == 0:
            return p
    return n


def _dot(a, b):
    return lax.dot_general(a, b, (((1,), (0,)), ((), ())), preferred_element_type=F32)


def _dot_nt(a, b):
    return lax.dot_general(a, b, (((1,), (1,)), ((), ())), preferred_element_type=F32)


def _dot_tn(a, b):
    return lax.dot_general(a, b, (((0,), (0,)), ((), ())), preferred_element_type=F32)


def _silu(x):
    return x * jax.nn.sigmoid(x)


def _matmul(a, b, *, trans_a=False, trans_b=False, out_dtype=F32, name="matmul", after=None):
    if trans_a:
        kdim, m = a.shape
    else:
        m, kdim = a.shape
    if trans_b:
        n, kb = b.shape
    else:
        kb, n = b.shape
    assert kdim == kb, (a.shape, b.shape, trans_a, trans_b)
    tm = _pick(m, (1152, 1024, 768, 512, 256, 128))
    tn = _pick(n, (1024, 512, 256, 128) if trans_b else (512, 256, 128))
    tk = _pick(kdim, (2304, 2048, 1024, 512, 256, 128))
    nk = kdim // tk
    dn = (((0 if trans_a else 1,), (1 if trans_b else 0,)), ((), ()))

    def body(a_ref, b_ref, *rest):
        o_ref, acc_ref = rest[-2:]
        part = lax.dot_general(a_ref[...], b_ref[...], dn, preferred_element_type=F32)
        if nk == 1:
            o_ref[...] = part.astype(o_ref.dtype)
        else:
            k = pl.program_id(2)

            @pl.when(k == 0)
            def _():
                acc_ref[...] = part

            @pl.when(k > 0)
            def _():
                acc_ref[...] += part

            @pl.when(k == nk - 1)
            def _():
                o_ref[...] = acc_ref[...].astype(o_ref.dtype)

    a_spec = pl.BlockSpec((tk, tm), lambda i, j, k: (k, i)) if trans_a else pl.BlockSpec((tm, tk), lambda i, j, k: (i, k))
    b_spec = pl.BlockSpec((tn, tk), lambda i, j, k: (j, k)) if trans_b else pl.BlockSpec((tk, tn), lambda i, j, k: (k, j))
    return pl.pallas_call(
        body,
        name=name,
        grid=(m // tm, n // tn, nk),
        in_specs=[a_spec, b_spec] + ([] if after is None else [ANY]),
        out_specs=pl.BlockSpec((tm, tn), lambda i, j, k: (i, j)),
        out_shape=jax.ShapeDtypeStruct((m, n), out_dtype),
        scratch_shapes=[pltpu.VMEM((tm, tn) if nk > 1 else (8, 128), F32)],
        compiler_params=_params(("parallel", "parallel", "arbitrary")),
    )(*((a, b) if after is None else (a, b, after)))


def _make_rowwise(f, name, out_dtypes, out_cols, n_lat, tm, diff_rows, col_blocks=None, drow_dtypes=None):
    drow_dtypes = drow_dtypes or {}

    def tile_fn(*args):
        return tuple(o.astype(dt) for o, dt in zip(f(*args), out_dtypes))

    def row_spec(k, arr):
        width, index = (col_blocks or {}).get(k, (arr.shape[1], 0))
        return pl.BlockSpec((tm, width), lambda i: (i, index))

    def row_width(k, arr):
        return (col_blocks or {}).get(k, (arr.shape[1], 0))[0]

    def fwd_call(rows, vecs):
        t = rows[0].shape[0]
        nr, nv = len(rows), len(vecs)
        nl = n_lat // tm

        def body(*refs):
            grp = (pl.program_id(0) >= nl).astype(jnp.int32)
            args = [r[...] for r in refs[:nr]] + [v[grp] for v in refs[nr:nr + nv]]
            for o_ref, o in zip(refs[nr + nv:], tile_fn(*args)):
                o_ref[...] = o

        return pl.pallas_call(
            body,
            name=name + "_fwd",
            grid=(t // tm,),
            in_specs=[row_spec(k, r) for k, r in enumerate(rows)]
            + [pl.BlockSpec(v.shape, lambda i: (0, 0, 0)) for v in vecs],
            out_specs=[pl.BlockSpec((tm, c), lambda i: (i, 0)) for c in out_cols],
            out_shape=[jax.ShapeDtypeStruct((t, c), dt) for c, dt in zip(out_cols, out_dtypes)],
            compiler_params=_params(("parallel",)),
        )(*rows, *vecs)

    def bwd_call(rows, vecs, gs, acc=None):
        t = rows[0].shape[0]
        nr, nv, ng = len(rows), len(vecs), len(gs)
        nl = n_lat // tm
        nd = len(diff_rows)
        acc = [None] * nd if acc is None else list(acc)
        acc_in = [a for a in acc if a is not None]

        def body(*refs):
            i = pl.program_id(0)
            grp = (i >= nl).astype(jnp.int32)
            args = [r[...] for r in refs[:nr]] + [v[grp] for v in refs[nr:nr + nv]]
            g_refs = refs[nr + nv:nr + nv + ng]
            acc_refs = list(refs[nr + nv + ng:nr + nv + ng + len(acc_in)])
            drow_refs = refs[nr + nv + ng + len(acc_in):nr + nv + ng + len(acc_in) + nd]
            dvec_refs = refs[nr + nv + ng + len(acc_in) + nd:]
            _, vjp = jax.vjp(tile_fn, *args)
            grads = vjp(tuple(g[...] for g in g_refs))
            for d_ref, k, a in zip(drow_refs, diff_rows, acc):
                gk = grads[k] if a is None else grads[k] + acc_refs.pop(0)[...]
                d_ref[...] = gk.astype(d_ref.dtype)

            @pl.when(i == 0)
            def _():
                for d_ref in dvec_refs:
                    d_ref[...] = jnp.zeros_like(d_ref)

            for j, d_ref in enumerate(dvec_refs):
                d_ref[grp] += grads[nr + j]

        outs = pl.pallas_call(
            body,
            name=name + "_bwd",
            grid=(t // tm,),
            in_specs=[row_spec(k, r) for k, r in enumerate(rows)]
            + [pl.BlockSpec(v.shape, lambda i: (0, 0, 0)) for v in vecs]
            + [pl.BlockSpec((tm, g.shape[1]), lambda i: (i, 0)) for g in gs]
            + [pl.BlockSpec((tm, a.shape[1]), lambda i: (i, 0)) for a in acc_in],
            out_specs=[pl.BlockSpec((tm, row_width(k, rows[k])), lambda i: (i, 0)) for k in diff_rows]
            + [pl.BlockSpec(v.shape, lambda i: (0, 0, 0)) for v in vecs],
            out_shape=[jax.ShapeDtypeStruct((t, row_width(k, rows[k])), drow_dtypes.get(k, rows[k].dtype))
                       for k in diff_rows]
            + [jax.ShapeDtypeStruct(v.shape, F32) for v in vecs],
            compiler_params=_params(("arbitrary",)),
        )(*rows, *vecs, *gs, *acc_in)
        return outs[:nd], outs[nd:]

    return fwd_call, bwd_call


def _f_norm_mod(x, g, scale, shift):
    r = lax.rsqrt(jnp.mean(x * x, axis=-1, keepdims=True) + NORM_EPS)
    return ((x * r * g) * (1.0 + scale) + shift,)


def _f_gate_na(o, z):
    return (o.astype(F32) * _silu(z.astype(F32)),)


def _f_merge(g_na, g_ret, y_na, y_ret):
    return (jax.nn.sigmoid(g_na.astype(F32)) * y_na.astype(F32) + jax.nn.sigmoid(g_ret.astype(F32)) * y_ret.astype(F32),)


def _f_residual(x, out, gate):
    return (x + gate * out,)


def _f_loss(x, target, g):
    r = lax.rsqrt(jnp.mean(x * x, axis=-1, keepdims=True) + NORM_EPS)
    y = x * r * g
    e = 0.5 * jnp.mean(jnp.square(y - target), axis=-1, keepdims=True)
    return (jnp.broadcast_to(e * (1.0 / 128.0), (x.shape[0], 128)),)


def _gate_ret_fwd_call(of, ob, z, zblk, tm):
    t, w = of.shape
    nh = w // RET_VAL_DIM

    def body(of_ref, ob_ref, z_ref, a_ref):
        for hh in range(nh):
            sl = slice(hh * RET_VAL_DIM, (hh + 1) * RET_VAL_DIM)
            o = of_ref[:, sl].astype(F32) + ob_ref[:, sl].astype(F32)
            r = lax.rsqrt(jnp.mean(o * o, axis=-1, keepdims=True) + NORM_EPS)
            a_ref[:, sl] = ((o * r) * _silu(z_ref[:, sl].astype(F32))).astype(a_ref.dtype)

    spec = pl.BlockSpec((tm, w), lambda i: (i, 0))
    zspec = pl.BlockSpec((tm, w), lambda i: (i, zblk))
    return pl.pallas_call(
        body, name="gate_ret_fwd", grid=(t // tm,), in_specs=[spec, spec, zspec], out_specs=spec,
        out_shape=jax.ShapeDtypeStruct((t, w), BF16), compiler_params=_params(("parallel",)),
    )(of, ob, z)


def _gate_ret_bwd_call(of, ob, z, zblk, da, tm):
    t, w = of.shape
    nh = w // RET_VAL_DIM

    def body(of_ref, ob_ref, z_ref, da_ref, do_ref, dz_ref):
        for hh in range(nh):
            sl = slice(hh * RET_VAL_DIM, (hh + 1) * RET_VAL_DIM)
            o = of_ref[:, sl].astype(F32) + ob_ref[:, sl].astype(F32)
            r = lax.rsqrt(jnp.mean(o * o, axis=-1, keepdims=True) + NORM_EPS)
            n = o * r
            zf = z_ref[:, sl].astype(F32)
            sg = jax.nn.sigmoid(zf)
            g = da_ref[:, sl].astype(F32)
            dn = g * (zf * sg)
            dz_ref[:, sl] = (g * n * (sg * (1.0 + zf * (1.0 - sg)))).astype(dz_ref.dtype)
            do_ref[:, sl] = (r * (dn - n * jnp.mean(dn * n, axis=-1, keepdims=True))).astype(do_ref.dtype)

    spec = pl.BlockSpec((tm, w), lambda i: (i, 0))
    zspec = pl.BlockSpec((tm, w), lambda i: (i, zblk))
    return pl.pallas_call(
        body, name="gate_ret_bwd", grid=(t // tm,), in_specs=[spec, spec, zspec, spec], out_specs=[spec, spec],
        out_shape=[jax.ShapeDtypeStruct((t, w), BF16), jax.ShapeDtypeStruct((t, w), z.dtype)],
        compiler_params=_params(("parallel",)),
    )(of, ob, z, da)


NA_PAIR = 2 * GRID_W
NA_KEY_ROWS = NA_WIN_ROWS + 2
NA_CLASSES = 5


def _na_geometry(t, n_lat):
    rows = n_lat // GRID_W
    assert rows % 2 == 0 and rows >= NA_KEY_ROWS + 2, rows
    return rows, rows // 2, NA_KEY_ROWS * GRID_W, t - n_lat, t // NA_PAIR


def _na_base(p, rows):
    return jnp.clip(2 * p - NA_WIN_ROWS // 2, 0, rows - NA_KEY_ROWS)


def _na_class(p, rows):
    return p - _na_base(p, rows) // 2


def _na_group(pairs, n_ctx):
    assert n_ctx % NA_PAIR == 0, n_ctx
    return 2 if pairs % 2 == 0 and (n_ctx // NA_PAIR) % 2 == 0 else 1


def _na_bias_spec(i, grp, rows, pairs, n_loc):
    return pl.BlockSpec((1, 1, 2, NA_KEY_ROWS // 2, GRID_W, NA_PAIR),
                        lambda h, g: (h, _na_class(jnp.minimum(g * grp + i, pairs - grp + i), rows), 0, 0, 0, 0))


def _na_bias_tile(bt_ref):
    return jnp.concatenate(
        [jnp.concatenate([bt_ref[0, 0, i, q] for q in range(NA_KEY_ROWS // 2)], axis=1) for i in range(2)], axis=0)


def _na_store_bias_grad(dbt_ref, ds, accumulate):
    for i in range(2):
        for q in range(NA_KEY_ROWS // 2):
            tile = ds[i * GRID_W:(i + 1) * GRID_W, q * NA_PAIR:(q + 1) * NA_PAIR]
            if accumulate:
                dbt_ref[0, 0, i, q] += tile
            else:
                dbt_ref[0, 0, i, q] = tile


def _na_fwd_call(q, k, v, col0, w, bt, n_lat):
    t = q.shape[0]
    nh = w // NA_HEAD_DIM
    rows, pairs, n_loc, n_ctx, nq = _na_geometry(t, n_lat)
    grp = _na_group(pairs, n_ctx)
    scale = NA_HEAD_DIM ** -0.5

    def body(q_ref, k_ref, v_ref, *rest):
        bt_refs, o_ref = rest[:grp], rest[grp]
        g = pl.program_id(1)
        kc = k_ref[pl.ds(n_lat, n_ctx), :]
        vc = v_ref[pl.ds(n_lat, n_ctx), :]

        @pl.when(g < pairs // grp)
        def _():
            for i in range(grp):
                sl = slice(i * NA_PAIR, (i + 1) * NA_PAIR)
                qb = q_ref[sl, :]
                s_ctx = _dot_nt(qb, kc) * scale
                start = pl.multiple_of(_na_base(g * grp + i, rows) * GRID_W, GRID_W)
                kw = k_ref[pl.ds(start, n_loc), :]
                vw = v_ref[pl.ds(start, n_loc), :]
                s_loc = _dot_nt(qb, kw) * scale + _na_bias_tile(bt_refs[i])
                m = jnp.maximum(jnp.max(s_loc, axis=-1, keepdims=True), jnp.max(s_ctx, axis=-1, keepdims=True))
                p_loc = jnp.exp(s_loc - m)
                p_ctx = jnp.exp(s_ctx - m)
                l = jnp.sum(p_loc, axis=-1, keepdims=True) + jnp.sum(p_ctx, axis=-1, keepdims=True)
                o = _dot(p_loc.astype(BF16), vw) + _dot(p_ctx.astype(BF16), vc)
                o_ref[sl, :] = (o / l).astype(o_ref.dtype)

        @pl.when(g >= pairs // grp)
        def _():
            s_ctx = _dot_nt(q_ref[...], kc) * scale
            m = jnp.max(s_ctx, axis=-1, keepdims=True)
            p = jnp.exp(s_ctx - m)
            l = jnp.sum(p, axis=-1, keepdims=True)
            o_ref[...] = (_dot(p.astype(BF16), vc) / l).astype(o_ref.dtype)

    qspec = pl.BlockSpec((grp * NA_PAIR, NA_HEAD_DIM), lambda h, g: (g, h))
    in_q = pl.BlockSpec((grp * NA_PAIR, NA_HEAD_DIM), lambda h, g: (g, col0[0] + h))
    in_k = pl.BlockSpec((t, NA_HEAD_DIM), lambda h, g: (0, col0[1] + h))
    in_v = pl.BlockSpec((t, NA_HEAD_DIM), lambda h, g: (0, col0[2] + h))
    return pl.pallas_call(
        body,
        name="na_attn_fwd",
        grid=(nh, nq // grp),
        in_specs=[in_q, in_k, in_v] + [_na_bias_spec(i, grp, rows, pairs, n_loc) for i in range(grp)],
        out_specs=qspec,
        out_shape=jax.ShapeDtypeStruct((t, w), BF16),
        compiler_params=_params(("parallel", "arbitrary")),
    )(q, k, v, *([bt] * grp))


def _na_bwd_call(q, k, v, col0, w, bt, do, n_lat):
    t = q.shape[0]
    nh = w // NA_HEAD_DIM
    rows, pairs, n_loc, n_ctx, nq = _na_geometry(t, n_lat)
    scale = NA_HEAD_DIM ** -0.5
    grp = _na_group(pairs, n_ctx)

    def body(q_ref, k_ref, v_ref, do_ref, *rest):
        bt_refs = rest[:grp]
        dq_ref, dk_ref, dv_ref = rest[2 * grp:2 * grp + 3]
        dbt_refs = rest[2 * grp + 3:]
        g = pl.program_id(1)

        @pl.when(g == 0)
        def _():
            dk_ref[...] = jnp.zeros_like(dk_ref)
            dv_ref[...] = jnp.zeros_like(dv_ref)

        kc = k_ref[pl.ds(n_lat, n_ctx), :]
        vc = v_ref[pl.ds(n_lat, n_ctx), :]

        @pl.when(g < pairs // grp)
        def _():
            for i in range(grp):
                p = g * grp + i
                sl = slice(i * NA_PAIR, (i + 1) * NA_PAIR)
                qb = q_ref[sl, :]
                dob = do_ref[sl, :]
                s_ctx = _dot_nt(qb, kc) * scale
                dp_ctx = _dot_nt(dob, vc)
                start = pl.multiple_of(_na_base(p, rows) * GRID_W, GRID_W)
                kw = k_ref[pl.ds(start, n_loc), :]
                vw = v_ref[pl.ds(start, n_loc), :]
                s_loc = _dot_nt(qb, kw) * scale + _na_bias_tile(bt_refs[i])
                m = jnp.maximum(jnp.max(s_loc, axis=-1, keepdims=True), jnp.max(s_ctx, axis=-1, keepdims=True))
                p_loc = jnp.exp(s_loc - m)
                p_ctx = jnp.exp(s_ctx - m)
                inv = 1.0 / (jnp.sum(p_loc, axis=-1, keepdims=True) + jnp.sum(p_ctx, axis=-1, keepdims=True))
                p_loc = p_loc * inv
                p_ctx = p_ctx * inv
                dp_loc = _dot_nt(dob, vw)
                delta = (jnp.sum(p_loc * dp_loc, axis=-1, keepdims=True)
                         + jnp.sum(p_ctx * dp_ctx, axis=-1, keepdims=True))
                ds_loc = p_loc * (dp_loc - delta)
                ds_ctx = p_ctx * (dp_ctx - delta)
                first = jnp.logical_or(g == 0, _na_class(p, rows) != _na_class(p - grp, rows))
                dbt_ref = dbt_refs[i]

                @pl.when(first)
                def _():
                    _na_store_bias_grad(dbt_ref, ds_loc, False)

                @pl.when(jnp.logical_not(first))
                def _():
                    _na_store_bias_grad(dbt_ref, ds_loc, True)

                dsl = (ds_loc * scale).astype(BF16)
                dsc = (ds_ctx * scale).astype(BF16)
                dq_ref[sl, :] = (_dot(dsl, kw) + _dot(dsc, kc)).astype(dq_ref.dtype)
                dk_ref[pl.ds(start, n_loc), :] += _dot_tn(dsl, qb)
                dv_ref[pl.ds(start, n_loc), :] += _dot_tn(p_loc.astype(BF16), dob)
                dk_ref[pl.ds(n_lat, n_ctx), :] += _dot_tn(dsc, qb)
                dv_ref[pl.ds(n_lat, n_ctx), :] += _dot_tn(p_ctx.astype(BF16), dob)

        @pl.when(g >= pairs // grp)
        def _():
            qb = q_ref[...]
            dob = do_ref[...]
            s_ctx = _dot_nt(qb, kc) * scale
            dp_ctx = _dot_nt(dob, vc)
            m = jnp.max(s_ctx, axis=-1, keepdims=True)
            p = jnp.exp(s_ctx - m)
            p = p * (1.0 / jnp.sum(p, axis=-1, keepdims=True))
            delta = jnp.sum(p * dp_ctx, axis=-1, keepdims=True)
            dsc = (p * (dp_ctx - delta) * scale).astype(BF16)
            dq_ref[...] = _dot(dsc, kc).astype(dq_ref.dtype)
            dk_ref[pl.ds(n_lat, n_ctx), :] += _dot_tn(dsc, qb)
            dv_ref[pl.ds(n_lat, n_ctx), :] += _dot_tn(p.astype(BF16), dob)

    qspec = pl.BlockSpec((grp * NA_PAIR, NA_HEAD_DIM), lambda h, g: (g, h))
    kspec = pl.BlockSpec((t, NA_HEAD_DIM), lambda h, g: (0, h))
    bspecs = [_na_bias_spec(i, grp, rows, pairs, n_loc) for i in range(grp)]
    zeros = [jnp.zeros(bt.shape, F32) for _ in range(grp)]
    outs = pl.pallas_call(
        body,
        name="na_attn_bwd",
        grid=(nh, nq // grp),
        in_specs=[pl.BlockSpec((grp * NA_PAIR, NA_HEAD_DIM), lambda h, g: (g, col0[0] + h)),
                  pl.BlockSpec((t, NA_HEAD_DIM), lambda h, g: (0, col0[1] + h)),
                  pl.BlockSpec((t, NA_HEAD_DIM), lambda h, g: (0, col0[2] + h)), qspec] + bspecs + [ANY] * grp,
        out_specs=[qspec, kspec, kspec] + bspecs,
        out_shape=[
            jax.ShapeDtypeStruct((t, w), BF16),
            jax.ShapeDtypeStruct((t, w), F32),
            jax.ShapeDtypeStruct((t, w), F32),
        ] + [jax.ShapeDtypeStruct(bt.shape, F32)] * grp,
        input_output_aliases={4 + grp + i: 3 + i for i in range(grp)},
        compiler_params=_params(("parallel", "arbitrary")),
    )(q, k, v, do, *([bt] * grp), *zeros)
    dbt = outs[3]
    for extra in outs[4:]:
        dbt = dbt + extra
    return outs[0], outs[1], outs[2], dbt


def _na_bias_table(rpb, rows):
    pairs = rows // 2
    nb = 2 * NA_WIN_COLS - 1
    nq = NA_KEY_ROWS // 2
    e1 = np.zeros((NA_CLASSES, 2, nq, 2, 2 * NA_WIN_ROWS - 1), np.float32)
    valid = np.zeros((NA_CLASSES, 2, nq, 2), bool)
    for cls, p in enumerate((0, 1, 2, pairs - 2, pairs - 1)):
        base = int(np.clip(2 * p - NA_WIN_ROWS // 2, 0, rows - NA_KEY_ROWS))
        assert p - base // 2 == cls, (rows, cls, p, base)
        for i in range(2):
            r = 2 * p + i
            r0 = int(np.clip(r - NA_WIN_ROWS // 2, 0, rows - NA_WIN_ROWS))
            for kk in range(NA_KEY_ROWS):
                if r0 <= base + kk < r0 + NA_WIN_ROWS:
                    valid[cls, i, kk // 2, kk % 2] = True
                    e1[cls, i, kk // 2, kk % 2, base + kk - r + NA_WIN_ROWS - 1] = 1.0
    cidx = np.arange(GRID_W)
    dc = np.clip(cidx[None, :] - cidx[:, None] + (NA_WIN_COLS - 1), 0, nb - 1)
    c0 = np.clip(cidx - NA_WIN_COLS // 2, 0, GRID_W - NA_WIN_COLS)
    col_in = (cidx[None, :] >= c0[:, None]) & (cidx[None, :] < c0[:, None] + NA_WIN_COLS)
    e2 = np.zeros((GRID_W, 2, GRID_W, 2, nb), np.float32)
    for par in range(2):
        e2[np.arange(GRID_W)[:, None], par, np.arange(GRID_W)[None, :], par, dc] = 1.0
    mask = valid[:, :, :, None, :, None] & col_in[None, None, None, :, None, :]
    t1 = jnp.einsum("hab,xiqpa->hxiqpb", rpb, jnp.asarray(e1), precision=lax.Precision.HIGHEST)
    t1 = t1.reshape(t1.shape[:4] + (2 * nb,))
    b = jnp.einsum("hxiqm,cwm->hxiqcw", t1, jnp.asarray(e2.reshape(GRID_W, 2 * GRID_W, 2 * nb)),
                   precision=lax.Precision.HIGHEST)
    return jnp.where(jnp.asarray(mask.reshape(NA_CLASSES, 2, nq, GRID_W, 2 * GRID_W))[None], b, MASK_VALUE)


def _ret_decays(lam_s, reverse):
    c = RET_CHUNK
    ii = lax.broadcasted_iota(jnp.int32, (c, c), 0)
    jj = lax.broadcasted_iota(jnp.int32, (c, c), 1)
    d = (jj - ii) if reverse else (ii - jj)
    dpos = jnp.maximum(d.astype(F32), 0.0)
    mask = jnp.where(d >= 0, jnp.exp(dpos * lam_s), 0.0)
    pi = lax.broadcasted_iota(jnp.int32, (c, 1), 0).astype(F32)
    qpos = (c - pi) if reverse else (pi + 1.0)
    kpos = pi if reverse else (c - 1.0 - pi)
    qd = jnp.exp(qpos * lam_s)
    kd = jnp.exp(kpos * lam_s)
    g = jnp.exp(jnp.full((1, RET_VAL_DIM), c * lam_s, F32))
    return mask, dpos, qd, kd, qpos, kpos, g


def _ret_head_group(nh):
    return _pick(nh, (4, 2))


def _ret_chunk_of(t, nt, nl, reverse):
    return (nt - 1 - t) if reverse else (t + nl) % nt


def _ret_fwd_call(qr, kr, v, vcol, lam, n_lat, reverse):
    t = qr.shape[0]
    nh = qr.shape[1] // RET_KEY_DIM
    c = RET_CHUNK
    nt, nl = t // c, n_lat // c

    hg = _ret_head_group(nh)
    dk, dv = RET_KEY_DIM, RET_VAL_DIM

    def body(lam_ref, q_ref, k_ref, v_ref, o_ref, s_ref, state):
        hb, step = pl.program_id(0), pl.program_id(1)

        @pl.when(step == 0)
        def _():
            state[...] = jnp.zeros_like(state)

        for j in range(hg):
            mask, _, qd, kd, _, _, g = _ret_decays(lam_ref[hb * hg + j], reverse)
            q, k, vv = q_ref[:, j * dk:(j + 1) * dk], k_ref[:, j * dk:(j + 1) * dk], v_ref[:, j * dv:(j + 1) * dv]
            p = _dot_nt(q, k) * mask
            s = state[j]
            qs = (q.astype(F32) * qd).astype(BF16)
            o_ref[:, j * dv:(j + 1) * dv] = (_dot(p.astype(BF16), vv) + _dot(qs, s.astype(BF16))).astype(o_ref.dtype)
            s_ref[j, 0] = s
            ks = (k.astype(F32) * kd).astype(BF16)
            state[j] = s * g + _dot_tn(ks, vv)

    def cmap(hb, step, lam_ref):
        return (_ret_chunk_of(step, nt, nl, reverse), hb)

    def vmap(hb, step, lam_ref):
        return (_ret_chunk_of(step, nt, nl, reverse), vcol // (hg * dv) + hb)

    return pl.pallas_call(
        body,
        name="retention_rev_fwd" if reverse else "retention_fwd",
        grid_spec=pltpu.PrefetchScalarGridSpec(
            num_scalar_prefetch=1,
            grid=(nh // hg, nt),
            in_specs=[
                pl.BlockSpec((c, hg * dk), cmap),
                pl.BlockSpec((c, hg * dk), cmap),
                pl.BlockSpec((c, hg * dv), vmap),
            ],
            out_specs=[
                pl.BlockSpec((c, hg * dv), cmap),
                pl.BlockSpec((hg, 1, dk, dv), lambda hb, step, lam_ref: (hb, step, 0, 0)),
            ],
            scratch_shapes=[pltpu.VMEM((hg, dk, dv), F32)],
        ),
        out_shape=[
            jax.ShapeDtypeStruct((t, nh * RET_VAL_DIM), BF16),
            jax.ShapeDtypeStruct((nh, nt, RET_KEY_DIM, RET_VAL_DIM), F32),
        ],
        compiler_params=_params(("parallel", "arbitrary")),
    )(lam, qr, kr, v)


def _ret_bwd_call(qr, kr, v, vcol, lam, states, do, n_lat, reverse):
    t = qr.shape[0]
    nh = qr.shape[1] // RET_KEY_DIM
    c = RET_CHUNK
    nt, nl = t // c, n_lat // c

    hg = _ret_head_group(nh)
    dk, dv = RET_KEY_DIM, RET_VAL_DIM

    def body(lam_ref, q_ref, k_ref, v_ref, s_ref, do_ref, dq_ref, dk_ref, dv_ref, dl_ref, dstate):
        hb, rstep = pl.program_id(0), pl.program_id(1)

        @pl.when(rstep == 0)
        def _():
            dstate[...] = jnp.zeros_like(dstate)
            dl_ref[...] = jnp.zeros_like(dl_ref)

        for j in range(hg):
            mask, dpos, qd, kd, qpos, kpos, g = _ret_decays(lam_ref[hb * hg + j], reverse)
            ksl, vsl = slice(j * dk, (j + 1) * dk), slice(j * dv, (j + 1) * dv)
            q, k, vv = q_ref[:, ksl], k_ref[:, ksl], v_ref[:, vsl]
            qf, kf = q.astype(F32), k.astype(F32)
            s = s_ref[j, 0]
            ds = dstate[j]
            dob = do_ref[:, vsl].astype(BF16)
            sb, dsb = s.astype(BF16), ds.astype(BF16)
            a = _dot_nt(q, k)
            p = a * mask
            dp = _dot_nt(dob, vv)
            da = dp * mask
            dab = da.astype(BF16)
            dqc = _dot_nt(dob, sb)
            dkc = _dot_nt(vv, dsb)
            qs = (qf * qd).astype(BF16)
            ks = (kf * kd).astype(BF16)
            dq_ref[:, ksl] = (_dot(dab, k) + dqc * qd).astype(dq_ref.dtype)
            dk_ref[:, ksl] = (_dot_tn(dab, q) + dkc * kd).astype(dk_ref.dtype)
            dv_ref[:, vsl] = (_dot_tn(p.astype(BF16), dob) + _dot(ks, dsb)).astype(dv_ref.dtype)
            terms = (
                jnp.sum(jnp.sum(da * a * dpos, axis=1, keepdims=True), axis=0, keepdims=True)
                + jnp.sum(jnp.sum(dqc * qf * (qd * qpos), axis=1, keepdims=True), axis=0, keepdims=True)
                + jnp.sum(jnp.sum(dkc * kf * (kd * kpos), axis=1, keepdims=True), axis=0, keepdims=True)
                + jnp.sum(jnp.sum(ds * s * (g * c), axis=1, keepdims=True), axis=0, keepdims=True)
            )
            dl_ref[j] += jnp.broadcast_to(terms, (8, 128))
            dstate[j] = ds * g + _dot_tn(qs, dob)

    def cmap(hb, rstep, lam_ref):
        return (_ret_chunk_of(nt - 1 - rstep, nt, nl, reverse), hb)

    def vmap(hb, rstep, lam_ref):
        return (_ret_chunk_of(nt - 1 - rstep, nt, nl, reverse), vcol // (hg * dv) + hb)

    return pl.pallas_call(
        body,
        name="retention_rev_bwd" if reverse else "retention_bwd",
        grid_spec=pltpu.PrefetchScalarGridSpec(
            num_scalar_prefetch=1,
            grid=(nh // hg, nt),
            in_specs=[
                pl.BlockSpec((c, hg * dk), cmap),
                pl.BlockSpec((c, hg * dk), cmap),
                pl.BlockSpec((c, hg * dv), vmap),
                pl.BlockSpec((hg, 1, dk, dv), lambda hb, rstep, lam_ref: (hb, nt - 1 - rstep, 0, 0)),
                pl.BlockSpec((c, hg * dv), cmap),
            ],
            out_specs=[
                pl.BlockSpec((c, hg * dk), cmap),
                pl.BlockSpec((c, hg * dk), cmap),
                pl.BlockSpec((c, hg * dv), cmap),
                pl.BlockSpec((hg, 8, 128), lambda hb, rstep, lam_ref: (hb, 0, 0)),
            ],
            scratch_shapes=[pltpu.VMEM((hg, dk, dv), F32)],
        ),
        out_shape=[
            jax.ShapeDtypeStruct(qr.shape, qr.dtype),
            jax.ShapeDtypeStruct(kr.shape, kr.dtype),
            jax.ShapeDtypeStruct((t, nh * dv), v.dtype),
            jax.ShapeDtypeStruct((nh, 8, 128), F32),
        ],
        compiler_params=_params(("parallel", "arbitrary")),
    )(lam, qr, kr, v, states, do)


def _rope_tables(t, n_lat):
    nf = RET_KEY_DIM // 4
    tok = np.arange(n_lat)
    inv_freq = (ROPE_BASE ** (-np.arange(nf, dtype=np.float32) / nf)).astype(np.float32)
    row = (tok // GRID_W).astype(np.float32)
    col = (tok % GRID_W).astype(np.float32)
    ang = np.concatenate([row[:, None] * inv_freq, col[:, None] * inv_freq], axis=-1).astype(np.float32)
    cos = np.ones((t, 2 * nf), np.float32)
    sin = np.zeros((t, 2 * nf), np.float32)
    cos[:n_lat] = np.cos(ang)
    sin[:n_lat] = np.sin(ang)
    return jnp.asarray(np.concatenate([cos, cos], axis=1)), jnp.asarray(np.concatenate([-sin, sin], axis=1))


def _rotate(x, cos2, sin2):
    return x * cos2 + pltpu.roll(x, RET_KEY_DIM // 2, 1) * sin2


def _rope_fwd_call(u, qblk, kblk, w_qk, cos2, sin2, k_scale, tm):
    t = u.shape[0]
    nh = w_qk // RET_KEY_DIM

    def body(q_ref, k_ref, c_ref, s_ref, qr_ref, kr_ref):
        c, s = c_ref[...], s_ref[...]
        for hh in range(nh):
            sl = slice(hh * RET_KEY_DIM, (hh + 1) * RET_KEY_DIM)
            qr_ref[:, sl] = _rotate(q_ref[:, sl].astype(F32), c, s).astype(qr_ref.dtype)
            kr_ref[:, sl] = (_rotate(k_ref[:, sl].astype(F32), c, s) * k_scale).astype(kr_ref.dtype)

    tab = pl.BlockSpec((tm, RET_KEY_DIM), lambda i: (i, 0))
    out = pl.BlockSpec((tm, w_qk), lambda i: (i, 0))
    return pl.pallas_call(
        body, name="rope_fwd", grid=(t // tm,),
        in_specs=[pl.BlockSpec((tm, w_qk), lambda i: (i, qblk)), pl.BlockSpec((tm, w_qk), lambda i: (i, kblk)), tab, tab],
        out_specs=[out, out], out_shape=[jax.ShapeDtypeStruct((t, w_qk), BF16)] * 2,
        compiler_params=_params(("parallel",)),
    )(u, u, cos2, sin2)


def _assemble_du_call(pieces, off, cos2, sin2, k_scale, tm):
    flat = [a for p in pieces for a in (p if isinstance(p, tuple) else (p,))]
    t = flat[0].shape[0]
    n = len(flat)

    def body(*refs):
        c, s = refs[n][...], -refs[n + 1][...]
        o_ref = refs[n + 2]
        it = iter(refs[:n])
        for blk, p in enumerate(pieces):
            lo = off[blk]
            if not isinstance(p, tuple):
                o_ref[:, lo:off[blk + 1]] = next(it)[...].astype(o_ref.dtype)
                continue
            fwd_ref, rev_ref = next(it), next(it)
            if blk == 6:
                o_ref[:, lo:off[blk + 1]] = (fwd_ref[...].astype(F32) + rev_ref[...].astype(F32)).astype(o_ref.dtype)
                continue
            mult = k_scale if blk == 5 else 1.0
            for hh in range((off[blk + 1] - lo) // RET_KEY_DIM):
                sl = slice(hh * RET_KEY_DIM, (hh + 1) * RET_KEY_DIM)
                dy = (fwd_ref[:, sl].astype(F32) + rev_ref[:, sl].astype(F32)) * mult
                o_ref[:, lo + hh * RET_KEY_DIM:lo + (hh + 1) * RET_KEY_DIM] = _rotate(dy, c, s).astype(o_ref.dtype)

    tab = pl.BlockSpec((tm, RET_KEY_DIM), lambda i: (i, 0))
    return pl.pallas_call(
        body, name="assemble_du", grid=(t // tm,),
        in_specs=[pl.BlockSpec((tm, a.shape[1]), lambda i: (i, 0)) for a in flat] + [tab, tab],
        out_specs=pl.BlockSpec((tm, off[-1]), lambda i: (i, 0)),
        out_shape=jax.ShapeDtypeStruct((t, off[-1]), BF16),
        compiler_params=_params(("parallel",)),
    )(*flat, cos2, sin2)


def _my_position():
    return lax.axis_index("x"), lax.axis_index("y"), lax.axis_index("c")


def _flip(pos, k):
    x, y, c = pos
    return (1 - x if k & 4 else x, 1 - y if k & 2 else y, 1 - c if k & 1 else c)


def _linear(pos):
    return 4 * pos[0] + 2 * pos[1] + pos[2]


def _slab(ref, axis, idx, size):
    start = pl.multiple_of(idx * size, size)
    return ref.at[pl.ds(start, size), :] if axis == 0 else ref.at[:, pl.ds(start, size)]


HBM_SPEC = pl.BlockSpec(memory_space=pltpu.HBM)
SEM_SPEC = pl.BlockSpec(memory_space=pltpu.SEMAPHORE)
DATAFLOW = pltpu.SideEffectType.DATAFLOW_SIDE_EFFECTING
PEER_BITS = (1, 2, 4, 6, 3, 5, 7)
GATHER_BITS = (1, 2, 4, 6)


def _in_hbm(a):
    return pltpu.with_memory_space_constraint(a, pltpu.HBM)


def _gather_views(me, k, a, src_refs, land_refs, axes):
    size = src_refs[a].shape[axes[a]]
    peer = _flip(me, k)
    return src_refs[a], _slab(land_refs[a], axes[a], _linear(me), size), _slab(land_refs[a], axes[a], _linear(peer), size)


def _scatter_views(me, k, a, src_refs, land_refs, axes):
    size = land_refs[a].shape[1 + axes[a]]
    peer = _flip(me, k)
    return _slab(src_refs[a], axes[a], _linear(peer), size), land_refs[a].at[k - 1], land_refs[a].at[k - 1]


CHIP_BITS = (0, 2, 4, 6)


def _chip_views(me, k, a, src_refs, land_refs, axes):
    j = CHIP_BITS.index(k)
    return src_refs[a].at[j], land_refs[a].at[j - 1], land_refs[a].at[j - 1]


def _pair_exchange(grads, axes, sizes):
    ns = len(grads)

    def slab_shape(a):
        s = grads[a].shape
        return (sizes[a], s[1]) if axes[a] == 0 else (s[0], sizes[a])

    def body(*refs):
        g_refs, p_refs = refs[:ns], refs[ns:2 * ns]
        send_sems, recv_sems = refs[2 * ns:]
        me = _my_position()
        sibling = _flip(me, 1)
        copies = []
        for j, kc in enumerate(CHIP_BITS):
            for a in range(ns):
                cp = pltpu.make_async_remote_copy(
                    src_ref=_slab(g_refs[a], axes[a], _linear(_flip(me, kc | 1)), sizes[a]), dst_ref=p_refs[a].at[j],
                    send_sem=send_sems.at[4 * a + j], recv_sem=recv_sems.at[4 * a + j],
                    device_id=sibling, device_id_type=MESH)
                cp.start()
                copies.append(cp)
        for cp in copies:
            cp.wait_recv()
        for cp in copies:
            cp.wait_send()

    return pl.pallas_call(
        body, name="scatter_pair_exchange", in_specs=[ANY] * ns, out_specs=[ANY] * ns,
        out_shape=[jax.ShapeDtypeStruct((4,) + slab_shape(a), grads[a].dtype) for a in range(ns)],
        scratch_shapes=[pltpu.SemaphoreType.DMA((4 * ns,)), pltpu.SemaphoreType.DMA((4 * ns,))],
        compiler_params=pltpu.CompilerParams(has_side_effects=True),
    )(*grads)


def _pair_add(grad, theirs, axis, chip_idx):
    _, r, c = theirs.shape
    tm = _pick(r, (256, 128, 64, 32, 16))
    if axis == 0:
        mine_spec = pl.BlockSpec((tm, c), lambda j, i, idx: (idx[j] * (r // tm) + i, 0))
    else:
        mine_spec = pl.BlockSpec((tm, c), lambda j, i, idx: (i, idx[j]))

    def body(idx_ref, mine_ref, theirs_ref, o_ref):
        o_ref[0] = (mine_ref[...].astype(F32) + theirs_ref[0].astype(F32)).astype(o_ref.dtype)

    spec = pl.BlockSpec((1, tm, c), lambda j, i, idx: (j, i, 0))
    return pl.pallas_call(
        body, name="scatter_pair_add",
        grid_spec=pltpu.PrefetchScalarGridSpec(
            num_scalar_prefetch=1, grid=(4, r // tm), in_specs=[mine_spec, spec], out_specs=spec),
        out_shape=jax.ShapeDtypeStruct(theirs.shape, theirs.dtype),
        compiler_params=_params(("parallel", "parallel")),
    )(chip_idx, grad, theirs)


def _slab_block(rows, cols, tm, axis):
    if axis == 0:
        return pl.BlockSpec((tm, cols), lambda i, idx: (idx[0] * (rows // tm) + i, 0))
    return pl.BlockSpec((tm, cols), lambda i, idx: (i, idx[0]))


def _place_shard(shard, land, axis, my_idx):
    r, c = shard.shape
    tm = _pick(r, (512, 256, 128, 64, 32, 16))

    def body(idx_ref, s_ref, land_ref, o_ref):
        o_ref[...] = s_ref[...]

    return pl.pallas_call(
        body, name="gather_place",
        grid_spec=pltpu.PrefetchScalarGridSpec(
            num_scalar_prefetch=1, grid=(r // tm,),
            in_specs=[pl.BlockSpec((tm, c), lambda i, idx: (i, 0)), ANY],
            out_specs=_slab_block(r, c, tm, axis)),
        out_shape=jax.ShapeDtypeStruct(land.shape, land.dtype),
        input_output_aliases={2: 0},
        compiler_params=_params(("parallel",)),
    )(my_idx, shard, land)


def _push_start(name, srcs, lands, axes, views, bits, deps):
    ns = len(srcs)

    def body(*refs):
        src_refs, land_refs = refs[:ns], refs[ns:2 * ns]
        send_sems, recv_sems = refs[2 * ns + len(deps):2 * ns + len(deps) + 2]
        token = refs[-1]
        me = _my_position()
        for k in bits:
            for a in range(ns):
                s, d, _ = views(me, k, a, src_refs, land_refs, axes)
                pltpu.make_async_remote_copy(
                    src_ref=s, dst_ref=d, send_sem=send_sems.at[7 * a + k - 1], recv_sem=recv_sems.at[7 * a + k - 1],
                    device_id=_flip(me, k), device_id_type=MESH).start()
        token[...] = jnp.zeros_like(token)

    thru = [pltpu.HBM(a.shape, a.dtype) for a in list(srcs) + list(lands)]
    outs = pl.pallas_call(
        body, name=name,
        in_specs=[HBM_SPEC] * (2 * ns) + [ANY] * len(deps),
        out_specs=[SEM_SPEC, SEM_SPEC] + [HBM_SPEC] * (2 * ns) + [VMEM_SPEC],
        out_shape=[pltpu.SemaphoreType.DMA((7 * ns,)), pltpu.SemaphoreType.DMA((7 * ns,))] + thru
        + [jax.ShapeDtypeStruct((8, 128), F32)],
        input_output_aliases={i: 2 + i for i in range(2 * ns)},
        compiler_params=pltpu.CompilerParams(has_side_effects=DATAFLOW),
    )(*[_in_hbm(a) for a in srcs], *[_in_hbm(a) for a in lands], *deps)
    return (outs[0], outs[1]), outs[2:2 + ns], outs[2 + ns:2 + 2 * ns], outs[-1]


def _gather_finish(lands, axes, sizes):
    ns = len(lands)
    chips = (2, 4, 6)

    def body(*refs):
        land_refs = refs[ns:2 * ns]
        send_sems, recv_sems = refs[2 * ns:]
        me = _my_position()
        sibling = _flip(me, 1)
        copies = []
        for j, kc in enumerate(chips):
            for a in range(ns):
                def slab_of(pos):
                    return _slab(land_refs[a], axes[a], _linear(pos), sizes[a])
                send = pltpu.make_async_remote_copy(
                    src_ref=slab_of(_flip(me, kc)), dst_ref=slab_of(_flip(me, kc)), send_sem=send_sems.at[3 * a + j],
                    recv_sem=recv_sems.at[3 * a + j], device_id=sibling, device_id_type=MESH)
                recv = pltpu.make_async_remote_copy(
                    src_ref=slab_of(_flip(me, kc)), dst_ref=slab_of(_flip(sibling, kc)), send_sem=send_sems.at[3 * a + j],
                    recv_sem=recv_sems.at[3 * a + j], device_id=sibling, device_id_type=MESH)
                send.start()
                copies.append((send, recv))
        for send, recv in copies:
            recv.wait_recv()
        for send, recv in copies:
            send.wait_send()

    return pl.pallas_call(
        body, name="gather_finish", in_specs=[ANY] * ns, out_specs=[ANY] * ns,
        out_shape=[jax.ShapeDtypeStruct(l.shape, l.dtype) for l in lands],
        input_output_aliases={a: a for a in range(ns)},
        scratch_shapes=[pltpu.SemaphoreType.DMA((3 * ns,)), pltpu.SemaphoreType.DMA((3 * ns,))],
        compiler_params=pltpu.CompilerParams(has_side_effects=True),
    )(*lands)


def _push_wait(name, sems, srcs, lands, axes, views, bits, after):
    ns = len(srcs)

    def body(*refs):
        src_refs, land_refs = refs[:ns], refs[ns:2 * ns]
        send_sems, recv_sems = refs[2 * ns:2 * ns + 2]
        me = _my_position()
        for k in bits:
            for a in range(ns):
                s, d, got = views(me, k, a, src_refs, land_refs, axes)
                cp = pltpu.make_async_remote_copy(
                    src_ref=s, dst_ref=got, send_sem=send_sems.at[7 * a + k - 1], recv_sem=recv_sems.at[7 * a + k - 1],
                    device_id=_flip(me, k), device_id_type=MESH)
                cp.wait_send()
                cp.wait_recv()

    thru = [pltpu.HBM(a.shape, a.dtype) for a in list(srcs) + list(lands)]
    outs = pl.pallas_call(
        body, name=name,
        in_specs=[HBM_SPEC] * (2 * ns) + [SEM_SPEC, SEM_SPEC] + [ANY] * len(after),
        out_specs=[HBM_SPEC] * (2 * ns),
        out_shape=thru,
        input_output_aliases={i: i for i in range(2 * ns)},
        compiler_params=pltpu.CompilerParams(has_side_effects=DATAFLOW),
    )(*srcs, *lands, sems[0], sems[1], *after)
    return outs[:ns], outs[ns:]


def _small_allgather(v, name):
    r, c = v.shape

    def body(v_ref, all_ref, sum_ref, send_sems, recv_sems):
        me = _my_position()
        all_ref[_linear(me)] = v_ref[...]
        copies = []
        for k in range(1, N_DEV):
            peer = _flip(me, k)
            copies.append(pltpu.make_async_remote_copy(
                src_ref=v_ref, dst_ref=all_ref.at[_linear(me)], send_sem=send_sems.at[k - 1], recv_sem=recv_sems.at[k - 1],
                device_id=peer, device_id_type=MESH))
        for cp in copies:
            cp.start()
        for k in range(1, N_DEV):
            peer = _flip(me, k)
            pltpu.make_async_remote_copy(
                src_ref=v_ref, dst_ref=all_ref.at[_linear(peer)], send_sem=send_sems.at[k - 1], recv_sem=recv_sems.at[k - 1],
                device_id=peer, device_id_type=MESH).wait_recv()
        for cp in copies:
            cp.wait_send()
        acc = all_ref[0]
        for d in range(1, N_DEV):
            acc = acc + all_ref[d]
        sum_ref[...] = acc

    return pl.pallas_call(
        body,
        name=name,
        in_specs=[VMEM_SPEC],
        out_specs=[VMEM_SPEC, VMEM_SPEC],
        out_shape=[jax.ShapeDtypeStruct((N_DEV, r, c), F32), jax.ShapeDtypeStruct((r, c), F32)],
        scratch_shapes=[pltpu.SemaphoreType.DMA((N_DEV - 1,)), pltpu.SemaphoreType.DMA((N_DEV - 1,))],
        compiler_params=pltpu.CompilerParams(has_side_effects=True, vmem_limit_bytes=VMEM_LIMIT),
    )(v)


def _ada_fwd_call(cin, ada_w, ada_b_cols):
    nl, d, ncol = ada_w.shape
    nrow = cin.shape[0]

    def body(c_ref, w_ref, b_ref, o_ref):
        cs = _silu(c_ref[...]).astype(BF16)
        for l in range(nl):
            o_ref[l] = _dot(cs, w_ref[l].astype(BF16)) + b_ref[l]

    return pl.pallas_call(
        body, name="ada_fwd", in_specs=[VMEM_SPEC] * 3, out_specs=VMEM_SPEC,
        out_shape=jax.ShapeDtypeStruct((nl, nrow, ncol), F32), compiler_params=_params(),
    )(cin, ada_w, ada_b_cols)


def _ada_bwd_call(cin, ada_w, dmod):
    nl, d, ncol = ada_w.shape
    nrow = cin.shape[0]

    def body(c_ref, w_ref, dm_ref, gw_ref, dcs_ref):
        cs = _silu(c_ref[...]).astype(BF16)
        acc = jnp.zeros((nrow, d), F32)
        for l in range(nl):
            dm = dm_ref[l].astype(BF16)
            gw_ref[l] = _dot_tn(cs, dm)
            acc = acc + _dot_nt(dm, w_ref[l].astype(BF16))
        dcs_ref[...] = acc

    return pl.pallas_call(
        body, name="ada_bwd", in_specs=[VMEM_SPEC] * 3, out_specs=[VMEM_SPEC, VMEM_SPEC],
        out_shape=[jax.ShapeDtypeStruct((nl, d, ncol), F32), jax.ShapeDtypeStruct((nrow, d), F32)],
        compiler_params=_params(),
    )(cin, ada_w, dmod)


def _adamw_math(w, g, m, v):
    m = ADAM_B1 * m + (1.0 - ADAM_B1) * g
    v = ADAM_B2 * v + (1.0 - ADAM_B2) * jnp.square(g)
    m_hat = m / (1.0 - ADAM_B1 ** ADAM_STEP)
    v_hat = v / (1.0 - ADAM_B2 ** ADAM_STEP)
    delta = -ADAM_LR * (m_hat / (jnp.sqrt(v_hat) + ADAM_EPS) + ADAM_WD * w)
    return delta, m, v


def _adamw_sharded(w, m, v, mine, slabs, axis, my_idx, layer, prev, name):
    nl, r, c = w.shape
    tm = _pick(r, (128, 64, 32, 16))
    nprev = 0 if prev is None else len(prev)
    nslab = slabs.shape[0]
    if axis is None:
        mine_spec = pl.BlockSpec((1, tm, c), lambda i, idx: (0, i, 0))
    else:
        mine_spec = _slab_block(r, c, tm, axis)

    def body(idx_ref, w_ref, m_ref, v_ref, mine_ref, s_ref, *rest):
        g_ref, d_ref, nm_ref, nv_ref = rest[nprev:]
        g = (mine_ref[0] if axis is None else mine_ref[...]).astype(F32)
        for k in range(nslab):
            g = g + s_ref[k].astype(F32)
        delta, nm, nv = _adamw_math(w_ref[0], g, m_ref[0], v_ref[0])
        g_ref[0], d_ref[0], nm_ref[0], nv_ref[0] = g, delta, nm, nv

    spec = pl.BlockSpec((1, tm, c), lambda i, idx: (layer, i, 0))
    out = jax.ShapeDtypeStruct(w.shape, F32)
    return pl.pallas_call(
        body, name=name,
        grid_spec=pltpu.PrefetchScalarGridSpec(
            num_scalar_prefetch=1, grid=(r // tm,),
            in_specs=[spec, spec, spec, mine_spec,
                      pl.BlockSpec((nslab, tm, c), lambda i, idx: (0, i, 0))] + [ANY] * nprev,
            out_specs=[spec] * 4),
        out_shape=[out] * 4,
        input_output_aliases={6 + j: j for j in range(nprev)},
        compiler_params=_params(("parallel",)),
    )(my_idx, w, m, v, mine, slabs, *(() if prev is None else prev))


def _adamw_dense(w, g, m, v, name):
    r, c = w.shape
    tm = _pick(r, (256, 128, 64, 32, 16, 8))

    def body(w_ref, g_ref, m_ref, v_ref, d_ref, nm_ref, nv_ref):
        d_ref[...], nm_ref[...], nv_ref[...] = _adamw_math(w_ref[...], g_ref[...], m_ref[...], v_ref[...])

    spec = pl.BlockSpec((tm, c), lambda i: (i, 0))
    out = jax.ShapeDtypeStruct(w.shape, F32)
    return pl.pallas_call(
        body, name=name, grid=(r // tm,), in_specs=[spec] * 4, out_specs=[spec] * 3, out_shape=[out] * 3,
        compiler_params=_params(("parallel",)),
    )(w, g, m, v)


def _pack(parts, width=128):
    flat = jnp.concatenate([p.reshape(-1).astype(F32) for p in parts])
    n = flat.shape[0]
    total = -(-n // (8 * width)) * (8 * width)
    return jnp.pad(flat, (0, total - n)).reshape(total // width, width)


def _unpack(buf, shapes):
    flat = buf.reshape(-1)
    out, off = [], 0
    for s in shapes:
        n = int(np.prod(s))
        out.append(flat[off:off + n].reshape(s))
        off += n
    return out


def kernel(x, c, ctx, c_ctx, ada_w, ada_b, norm_g, w_in, na_rpb, ret_decay_logit, w_proj_na, w_proj_ret, w_out, final_g, loss_target, m_c_ctx, m_ada_w, m_ada_b, m_norm_g, m_w_in, m_na_rpb, m_ret_decay_logit, m_w_proj_na, m_w_proj_ret, m_w_out, m_final_g, v_c_ctx, v_ada_w, v_ada_b, v_norm_g, v_w_in, v_na_rpb, v_ret_decay_logit, v_w_proj_na, v_w_proj_ret, v_w_out, v_final_g):
    depth = w_in.shape[0]
    n_lat, d = x.shape[1], x.shape[2]
    n_ctx = ctx.shape[1]
    t = n_lat + n_ctx
    w_na = w_proj_na.shape[1]
    w_retv = w_proj_ret.shape[1] * N_DEV
    in_cols = w_in.shape[2] * N_DEV
    w_qk = (in_cols - 4 * w_na - 2 * w_retv - 2 * d) // 2
    sizes = (w_na, w_na, w_na, w_na, w_qk, w_qk, w_retv, w_retv, d, d)
    off = tuple(int(o) for o in np.cumsum((0,) + sizes))
    NA_Q, NA_K, NA_V, NA_Z, RET_Q, RET_K, RET_V, RET_Z, G_NA, G_RET = range(10)
    rows = n_lat // GRID_W
    me = _my_position()
    my_idx = _linear(me)
    tm_row = _pick(n_ctx, (256, 128))

    idx_arr = jnp.reshape(my_idx, (1,)).astype(jnp.int32)
    chip_idx = jnp.stack([_linear(_flip(me, kc)) for kc in CHIP_BITS]).astype(jnp.int32)

    w_axes = (1, 1, 0, 0)
    w_names = ("w_in", "w_proj_na", "w_proj_ret", "w_out")
    shard = [[w[l].astype(BF16) for w in (w_in, w_proj_na, w_proj_ret, w_out)] for l in range(depth)]
    groups = [[(0, 0)], [(0, 1), (0, 2), (0, 3)]] + [[(l, a) for a in range(4)] for l in range(1, depth)]
    gathers = {}

    def start_gather(gi, deps):
        keys = groups[gi]
        srcs = [shard[l][a] for l, a in keys]
        axes = tuple(w_axes[a] for _, a in keys)
        lands = [_place_shard(s, lax.empty(tuple(n * (N_DEV if i == ax else 1) for i, n in enumerate(s.shape)), BF16),
                              ax, idx_arr) for s, ax in zip(srcs, axes)]
        sizes = tuple(s.shape[ax] for s, ax in zip(srcs, axes))
        sems, srcs, lands, tok = _push_start(f"gather_start_{gi}", srcs, lands, axes, _gather_views, GATHER_BITS, deps)
        flight = dict(name=f"gather_wait_{gi}", sems=sems, srcs=srcs, lands=lands, axes=axes, sizes=sizes, ready=None)
        for pos, key in enumerate(keys):
            gathers[key] = (flight, pos)
        return tok

    token = start_gather(0, ())

    ncol = ada_w.shape[2]
    c_all, _ = _small_allgather(jnp.pad(c, ((0, 7), (0, 0))) + token[:, :1], "allgather_c")
    cin = jnp.concatenate([c_all[:, 0, :], c_ctx[None, :], jnp.zeros((7, d), F32)], axis=0)
    ada_b_cols = lax.dynamic_slice_in_dim(ada_b, my_idx * ncol, ncol, axis=1)[:, None, :]
    mod_cols = _ada_fwd_call(cin, ada_w, ada_b_cols)
    mod_gathered, _ = _small_allgather(mod_cols.reshape(depth * 16, ncol), "allgather_mod")
    mod_all = mod_gathered.reshape(N_DEV, depth, 16, ncol).transpose(1, 2, 0, 3).reshape(depth, 16, N_DEV * ncol)
    mod_lat = lax.dynamic_index_in_dim(mod_all, my_idx, axis=1, keepdims=False)
    mod_ctx = mod_all[:, 8, :]
    token = mod_gathered
    for gi in range(1, len(groups)):
        token = start_gather(gi, (token,))

    def landed(l, a, act):
        flight, pos = gathers[(l, a)]
        if flight["ready"] is None:
            arrived = _push_wait(flight["name"], flight["sems"], flight["srcs"], flight["lands"],
                                 flight["axes"], _gather_views, GATHER_BITS, (act, token))[1]
            flight["ready"] = _gather_finish(arrived, flight["axes"], flight["sizes"])
        return flight["ready"][pos]

    pending, scatters = {}, []

    def send_dw(l, a, dw):
        pending[(l, a)] = dw
        if a == 0:
            keys = [(0, 0)] if l == 0 else [(l, b) for b in range(4)]
        elif l == 0 and a == 1:
            keys = [(0, 1), (0, 2), (0, 3)]
        else:
            return None
        srcs = [pending[k] for k in keys]
        axes = tuple(w_axes[b] for _, b in keys)
        sizes = tuple(s.shape[ax] // N_DEV for s, ax in zip(srcs, axes))
        slab_shapes = [tuple(n // (N_DEV if i == ax else 1) for i, n in enumerate(s.shape)) for s, ax in zip(srcs, axes)]
        by_chip = keys == [(0, 0)]
        if by_chip:
            theirs = _pair_exchange(srcs, axes, sizes)
            srcs = [_pair_add(g, p, ax, chip_idx) for g, p, ax in zip(srcs, theirs, axes)]
            lands = [lax.empty((3,) + shp, BF16) for shp in slab_shapes]
            views, bits = _chip_views, CHIP_BITS[1:]
        else:
            lands = [lax.empty((N_DEV - 1,) + shp, BF16) for shp in slab_shapes]
            views, bits = _scatter_views, PEER_BITS
        sems, srcs, lands, tok = _push_start(f"scatter_start_{len(scatters)}", srcs, lands, axes, views, bits, ())
        scatters.append(dict(name=f"scatter_wait_{len(scatters)}", sems=sems, srcs=srcs, lands=lands, axes=axes, keys=keys,
                             views=views, bits=bits, by_chip=by_chip))
        return tok

    cos2, sin2 = _rope_tables(t, n_lat)
    k_scale = RET_KEY_DIM ** -0.5
    assert off[RET_Q] % w_qk == 0 and off[RET_K] % w_qk == 0
    assert off[NA_Z] % w_na == 0 and off[G_NA] % d == 0 and off[G_RET] % d == 0 and off[RET_Z] % w_retv == 0
    assert off[RET_V] % (_ret_head_group(w_retv // RET_VAL_DIM) * RET_VAL_DIM) == 0
    na_cols = tuple(off[i] // NA_HEAD_DIM for i in (NA_Q, NA_K, NA_V))
    norm_mod_fwd, norm_mod_bwd = _make_rowwise(_f_norm_mod, "norm_mod", (BF16,), (d,), n_lat, tm_row, (0,))
    gate_na_fwd, gate_na_bwd = _make_rowwise(_f_gate_na, "gate_na", (BF16,), (w_na,), n_lat, tm_row, (0, 1),
                                             col_blocks={1: (w_na, off[NA_Z] // w_na)})
    merge_fwd, merge_bwd = _make_rowwise(_f_merge, "merge", (BF16,), (d,), n_lat, tm_row, (0, 1, 2, 3),
                                         col_blocks={0: (d, off[G_NA] // d), 1: (d, off[G_RET] // d)},
                                         drow_dtypes={2: BF16, 3: BF16})
    residual_fwd, residual_bwd = _make_rowwise(_f_residual, "residual", (F32,), (d,), n_lat, tm_row, (0, 1),
                                               drow_dtypes={1: BF16})
    loss_fwd, loss_bwd = _make_rowwise(_f_loss, "loss_head", (F32,), (128,), n_lat, tm_row, (0,))

    def pair(a, b):
        return jnp.stack([a, b])[:, None, :]

    def mod_vectors(mod_lat_l, mod_ctx_l, norm_g_l):
        shift, scale, gate = jnp.split(mod_lat_l, 3)
        c_shift, c_scale, c_gate = jnp.split(mod_ctx_l, 3)
        return pair(norm_g_l, norm_g_l), pair(scale, c_scale), pair(shift, c_shift), pair(gate, c_gate)

    def log_decay(logit):
        return jax.nn.log_sigmoid(logit.astype(F32))

    xa = jnp.concatenate([x[0], ctx[0]], axis=0)
    saved = []
    for l in range(depth):
        vecs, vecs_vjp = jax.vjp(mod_vectors, mod_lat[l], mod_ctx[l], norm_g[l])
        (h,) = norm_mod_fwd((xa,), vecs[:3])
        wl_in = landed(l, 0, h)
        u = _matmul(h, wl_in, out_dtype=BF16, name="in_proj_fwd")
        qr, kr = _rope_fwd_call(u, off[RET_Q] // w_qk, off[RET_K] // w_qk, w_qk, cos2, sin2, k_scale, tm_row)
        bt, bt_vjp = jax.vjp(lambda r: _na_bias_table(r, rows), na_rpb[l])
        lam, lam_vjp = jax.vjp(log_decay, ret_decay_logit[l])
        o_na = _na_fwd_call(u, u, u, na_cols, w_na, bt, n_lat)
        o_f, st_f = _ret_fwd_call(qr, kr, u, off[RET_V], lam[0], n_lat, False)
        o_b, st_b = _ret_fwd_call(qr, kr, u, off[RET_V], lam[1], n_lat, True)
        (a_na,) = gate_na_fwd((o_na, u), ())
        a_ret = _gate_ret_fwd_call(o_f, o_b, u, off[RET_Z] // w_retv, tm_row)
        wl_pna, wl_pret, wl_out = landed(l, 1, a_na), landed(l, 2, a_na), landed(l, 3, a_na)
        y_na = _matmul(a_na, wl_pna, out_dtype=BF16, name="proj_na_fwd")
        y_ret = _matmul(a_ret, wl_pret, out_dtype=BF16, name="proj_ret_fwd")
        (merged,) = merge_fwd((u, u, y_na, y_ret), ())
        out = _matmul(merged, wl_out, out_dtype=F32, name="out_proj_fwd")
        (xa_next,) = residual_fwd((xa, out), vecs[3:])
        saved.append(dict(xa=xa, vecs=vecs, vecs_vjp=vecs_vjp, h=h, w=(wl_in, wl_pna, wl_pret, wl_out), u=u, qr=qr, kr=kr,
                          bt=bt, bt_vjp=bt_vjp, lam=lam, lam_vjp=lam_vjp, o_na=o_na, o_f=o_f,
                          o_b=o_b, st_f=st_f, st_b=st_b, a_na=a_na, a_ret=a_ret, y_na=y_na, y_ret=y_ret,
                          merged=merged, out=out))
        xa = xa_next

    fg_pair, fg_vjp = jax.vjp(lambda g: pair(g, g), final_g)
    x_last = xa[:n_lat]
    (loss_rows,) = loss_fwd((x_last, loss_target[0]), (fg_pair,))
    loss = lax.psum(jnp.sum(loss_rows), ("x", "y", "c"))
    (dx_last,), (d_fg_pair,) = loss_bwd((x_last, loss_target[0]), (fg_pair,), (jnp.ones_like(loss_rows),))
    (d_final_g,) = fg_vjp(d_fg_pair)
    dxa = jnp.pad(dx_last, ((0, n_ctx), (0, 0)))

    d_mod_lat, d_mod_ctx, d_norm_g, d_rpb, d_decay = ([None] * depth for _ in range(5))
    for l in reversed(range(depth)):
        s = saved[l]
        u, qr, kr = s["u"], s["qr"], s["kr"]
        wl_in, wl_pna, wl_pret, wl_out = s["w"]
        (dxa_res, d_out), (d_gate,) = residual_bwd((s["xa"], s["out"]), s["vecs"][3:], (dxa,))
        send_dw(l, 3, _matmul(s["merged"], d_out, trans_a=True, out_dtype=BF16, name="out_proj_dw"))
        d_merged = _matmul(d_out, wl_out, trans_b=True, out_dtype=BF16, name="out_proj_da")
        (dg_na, dg_ret, dy_na, dy_ret), _ = merge_bwd((u, u, s["y_na"], s["y_ret"]), (), (d_merged,))
        send_dw(l, 2, _matmul(s["a_ret"], dy_ret, trans_a=True, out_dtype=BF16, name="proj_ret_dw"))
        da_ret = _matmul(dy_ret, wl_pret, trans_b=True, out_dtype=BF16, name="proj_ret_da")
        tok = send_dw(l, 1, _matmul(s["a_na"], dy_na, trans_a=True, out_dtype=BF16, name="proj_na_dw"))
        da_na = _matmul(dy_na, wl_pna, trans_b=True, out_dtype=BF16, name="proj_na_da", after=tok)
        do_ret, dz_ret = _gate_ret_bwd_call(s["o_f"], s["o_b"], u, off[RET_Z] // w_retv, da_ret, tm_row)
        (do_na, dz_na), _ = gate_na_bwd((s["o_na"], u), (), (da_na,))
        dq_f, dk_f, dv_f, dl_f = _ret_bwd_call(qr, kr, u, off[RET_V], s["lam"][0], s["st_f"], do_ret, n_lat, False)
        dq_b, dk_b, dv_b, dl_b = _ret_bwd_call(qr, kr, u, off[RET_V], s["lam"][1], s["st_b"], do_ret, n_lat, True)
        dq, dk, dv, dbt = _na_bwd_call(u, u, u, na_cols, w_na, s["bt"], do_na, n_lat)
        du = _assemble_du_call([dq, dk, dv, dz_na, (dq_f, dq_b), (dk_f, dk_b), (dv_f, dv_b), dz_ret, dg_na, dg_ret],
                               off, cos2, sin2, k_scale, _pick(n_ctx, (128,)))
        (d_rpb[l],) = s["bt_vjp"](dbt)
        (d_decay[l],) = s["lam_vjp"](jnp.stack([dl_f[:, 0, 0], dl_b[:, 0, 0]]))
        tok = send_dw(l, 0, _matmul(s["h"], du, trans_a=True, out_dtype=BF16, name="in_proj_dw"))
        dh = _matmul(du, wl_in, trans_b=True, out_dtype=BF16, name="in_proj_da", after=tok)
        (dxa,), d_vecs = norm_mod_bwd((s["xa"],), s["vecs"][:3], (dh,), acc=(dxa_res,))
        d_mod_lat[l], d_mod_ctx[l], d_norm_g[l] = s["vecs_vjp"](tuple(d_vecs) + (d_gate,))
    gx = dxa[:n_lat]
    d_mod_lat, d_mod_ctx, d_norm_g, d_rpb, d_decay = (jnp.stack(a) for a in (d_mod_lat, d_mod_ctx, d_norm_g, d_rpb, d_decay))

    small_shapes = [d_mod_lat.shape, d_mod_ctx.shape, d_norm_g.shape, d_final_g.shape, d_rpb.shape, d_decay.shape]
    packed = _pack([d_mod_lat, d_mod_ctx, d_norm_g, d_final_g, d_rpb, d_decay])
    g_all, g_sum = _small_allgather(packed, "allgather_small_grads")
    dml_sum, dmc_sum, grad_norm_g, grad_final_g, grad_na_rpb, grad_decay = _unpack(g_sum, small_shapes)
    grad_ada_b = dml_sum + dmc_sum
    dml_all = g_all.reshape(N_DEV, -1)[:, :depth * 3 * d].reshape(N_DEV, depth, 3 * d)

    def my_cols(a):
        return lax.dynamic_slice_in_dim(a, my_idx * ncol, ncol, axis=a.ndim - 1)

    dmod = jnp.concatenate(
        [my_cols(dml_all).transpose(1, 0, 2), my_cols(dmc_sum)[:, None, :], jnp.zeros((depth, 7, ncol), F32)], axis=1)
    grad_ada_w, dcs_part = _ada_bwd_call(cin, ada_w, dmod)
    _, dcs = _small_allgather(dcs_part, "allgather_dcsilu")
    sg = jax.nn.sigmoid(c_ctx)
    grad_c_ctx = dcs[8] * (sg * (1.0 + c_ctx * (1.0 - sg)))

    def flat2(a):
        return a.reshape(a.shape[0] * a.shape[1], a.shape[2])

    small_w = [c_ctx, ada_b, norm_g, na_rpb, ret_decay_logit, final_g]
    small_g = [grad_c_ctx, grad_ada_b, grad_norm_g, grad_na_rpb, grad_decay, grad_final_g]
    small_m = [m_c_ctx, m_ada_b, m_norm_g, m_na_rpb, m_ret_decay_logit, m_final_g]
    small_v = [v_c_ctx, v_ada_b, v_norm_g, v_na_rpb, v_ret_decay_logit, v_final_g]
    shp = [a.shape for a in small_w]
    ds_, nms_, nvs_ = _adamw_dense(_pack(small_w), _pack(small_g), _pack(small_m), _pack(small_v), "adamw_small")
    ds_, nms_, nvs_ = _unpack(ds_, shp), _unpack(nms_, shp), _unpack(nvs_, shp)

    d_ada, nm_ada, nv_ada = [a.reshape(ada_w.shape) for a in _adamw_dense(
        flat2(ada_w), flat2(grad_ada_w), flat2(m_ada_w), flat2(v_ada_w), "adamw_ada_w")]

    w_all = (w_in, w_proj_na, w_proj_ret, w_out)
    m_all = (m_w_in, m_w_proj_na, m_w_proj_ret, m_w_out)
    v_all = (v_w_in, v_w_proj_na, v_w_proj_ret, v_w_out)
    upd = [None] * 4
    after = d_ada
    for flight in scatters:
        mine, slabs = _push_wait(flight["name"], flight["sems"], flight["srcs"], flight["lands"], flight["axes"],
                                 flight["views"], flight["bits"], (after,))
        for (l, a), own, s in zip(flight["keys"], mine, slabs):
            upd[a] = _adamw_sharded(w_all[a], m_all[a], v_all[a], own, s, None if flight["by_chip"] else w_axes[a],
                                    idx_arr, l, upd[a], "adamw_" + w_names[a])
            after = upd[a][1]
    (g_w_in, d_w_in, nm_w_in, nv_w_in), (g_pna, d_pna, nm_pna, nv_pna) = upd[0], upd[1]
    (g_pret, d_pret, nm_pret, nv_pret), (g_out, d_out, nm_out, nv_out) = upd[2], upd[3]

    def order(cc, aw, ab, ng, wi, rp, dl, pn, pr, wo, fg):
        return [cc, aw, ab, ng, wi, rp, dl, pn, pr, wo, fg]

    grads_out = order(grad_c_ctx, grad_ada_w, grad_ada_b, grad_norm_g, g_w_in, grad_na_rpb, grad_decay, g_pna, g_pret, g_out, grad_final_g)
    delta_out = order(ds_[0], d_ada, ds_[1], ds_[2], d_w_in, ds_[3], ds_[4], d_pna, d_pret, d_out, ds_[5])
    m_out = order(nms_[0], nm_ada, nms_[1], nms_[2], nm_w_in, nms_[3], nms_[4], nm_pna, nm_pret, nm_out, nms_[5])
    v_out = order(nvs_[0], nv_ada, nvs_[1], nvs_[2], nv_w_in, nvs_[3], nvs_[4], nv_pna, nv_pret, nv_out, nvs_[5])
    return (loss, gx[None], *grads_out, *delta_out, *m_out, *v_out)
```

```python
import functools

import numpy as np
import jax
import jax.numpy as jnp
from jax import lax
from jax.experimental import pallas as pl
from jax.experimental.pallas import tpu as pltpu

F32 = jnp.float32
BF16 = jnp.bfloat16

N_DEV = 8
GRID_W = 64
NA_HEAD_DIM = 128
NA_WIN_ROWS = 8
NA_WIN_COLS = 16
RET_KEY_DIM = 128
RET_VAL_DIM = 256
RET_CHUNK = 128
ROPE_BASE = 10000.0
NORM_EPS = 1e-6
MASK_VALUE = -1e30

ADAM_LR = 0.001
ADAM_B1 = 0.9
ADAM_B2 = 0.999
ADAM_EPS = 1e-08
ADAM_WD = 0.01
ADAM_STEP = 10

VMEM_LIMIT = 48 * 1024 * 1024
MESH = pl.DeviceIdType.MESH
ANY = pl.BlockSpec(memory_space=pl.ANY)
VMEM_SPEC = pl.BlockSpec(memory_space=pltpu.VMEM)


def _params(sem=None):
    return pltpu.CompilerParams(dimension_semantics=sem, vmem_limit_bytes=VMEM_LIMIT)


def _pick(n, prefs):
    for p in prefs:
        if n % p == 0:
            return p
    return n


def _dot(a, b):
    return lax.dot_general(a, b, (((1,), (0,)), ((), ())), preferred_element_type=F32)


def _dot_nt(a, b):
    return lax.dot_general(a, b, (((1,), (1,)), ((), ())), preferred_element_type=F32)


def _dot_tn(a, b):
    return lax.dot_general(a, b, (((0,), (0,)), ((), ())), preferred_element_type=F32)


def _silu(x):
    return x * jax.nn.sigmoid(x)


def _matmul(a, b, *, trans_a=False, trans_b=False, out_dtype=F32, name="matmul", after=None):
    if trans_a:
        kdim, m = a.shape
    else:
        m, kdim = a.shape
    if trans_b:
        n, kb = b.shape
    else:
        kb, n = b.shape
    assert kdim == kb, (a.shape, b.shape, trans_a, trans_b)
    tm = _pick(m, (1152, 1024, 768, 512, 256, 128))
    tn = _pick(n, (1024, 512, 256, 128) if trans_b else (512, 256, 128))
    tk = _pick(kdim, (2304, 2048, 1024, 512, 256, 128))
    nk = kdim // tk
    dn = (((0 if trans_a else 1,), (1 if trans_b else 0,)), ((), ()))

    def body(a_ref, b_ref, *rest):
        o_ref, acc_ref = rest[-2:]
        part = lax.dot_general(a_ref[...], b_ref[...], dn, preferred_element_type=F32)
        if nk == 1:
            o_ref[...] = part.astype(o_ref.dtype)
        else:
            k = pl.program_id(2)

            @pl.when(k == 0)
            def _():
                acc_ref[...] = part

            @pl.when(k > 0)
            def _():
                acc_ref[...] += part

            @pl.when(k == nk - 1)
            def _():
                o_ref[...] = acc_ref[...].astype(o_ref.dtype)

    a_spec = pl.BlockSpec((tk, tm), lambda i, j, k: (k, i)) if trans_a else pl.BlockSpec((tm, tk), lambda i, j, k: (i, k))
    b_spec = pl.BlockSpec((tn, tk), lambda i, j, k: (j, k)) if trans_b else pl.BlockSpec((tk, tn), lambda i, j, k: (k, j))
    return pl.pallas_call(
        body,
        name=name,
        grid=(m // tm, n // tn, nk),
        in_specs=[a_spec, b_spec] + ([] if after is None else [ANY]),
        out_specs=pl.BlockSpec((tm, tn), lambda i, j, k: (i, j)),
        out_shape=jax.ShapeDtypeStruct((m, n), out_dtype),
        scratch_shapes=[pltpu.VMEM((tm, tn) if nk > 1 else (8, 128), F32)],
        compiler_params=_params(("parallel", "parallel", "arbitrary")),
    )(*((a, b) if after is None else (a, b, after)))


def _make_rowwise(f, name, out_dtypes, out_cols, n_lat, tm, diff_rows, col_blocks=None, drow_dtypes=None):
    drow_dtypes = drow_dtypes or {}

    def tile_fn(*args):
        return tuple(o.astype(dt) for o, dt in zip(f(*args), out_dtypes))

    def row_spec(k, arr):
        width, index = (col_blocks or {}).get(k, (arr.shape[1], 0))
        return pl.BlockSpec((tm, width), lambda i: (i, index))

    def row_width(k, arr):
        return (col_blocks or {}).get(k, (arr.shape[1], 0))[0]

    def fwd_call(rows, vecs):
        t = rows[0].shape[0]
        nr, nv = len(rows), len(vecs)
        nl = n_lat // tm

        def body(*refs):
            grp = (pl.program_id(0) >= nl).astype(jnp.int32)
            args = [r[...] for r in refs[:nr]] + [v[grp] for v in refs[nr:nr + nv]]
            for o_ref, o in zip(refs[nr + nv:], tile_fn(*args)):
                o_ref[...] = o

        return pl.pallas_call(
            body,
            name=name + "_fwd",
            grid=(t // tm,),
            in_specs=[row_spec(k, r) for k, r in enumerate(rows)]
            + [pl.BlockSpec(v.shape, lambda i: (0, 0, 0)) for v in vecs],
            out_specs=[pl.BlockSpec((tm, c), lambda i: (i, 0)) for c in out_cols],
            out_shape=[jax.ShapeDtypeStruct((t, c), dt) for c, dt in zip(out_cols, out_dtypes)],
            compiler_params=_params(("parallel",)),
        )(*rows, *vecs)

    def bwd_call(rows, vecs, gs, acc=None):
        t = rows[0].shape[0]
        nr, nv, ng = len(rows), len(vecs), len(gs)
        nl = n_lat // tm
        nd = len(diff_rows)
        acc = [None] * nd if acc is None else list(acc)
        acc_in = [a for a in acc if a is not None]

        def body(*refs):
            i = pl.program_id(0)
            grp = (i >= nl).astype(jnp.int32)
            args = [r[...] for r in refs[:nr]] + [v[grp] for v in refs[nr:nr + nv]]
            g_refs = refs[nr + nv:nr + nv + ng]
            acc_refs = list(refs[nr + nv + ng:nr + nv + ng + len(acc_in)])
            drow_refs = refs[nr + nv + ng + len(acc_in):nr + nv + ng + len(acc_in) + nd]
            dvec_refs = refs[nr + nv + ng + len(acc_in) + nd:]
            _, vjp = jax.vjp(tile_fn, *args)
            grads = vjp(tuple(g[...] for g in g_refs))
            for d_ref, k, a in zip(drow_refs, diff_rows, acc):
                gk = grads[k] if a is None else grads[k] + acc_refs.pop(0)[...]
                d_ref[...] = gk.astype(d_ref.dtype)

            @pl.when(i == 0)
            def _():
                for d_ref in dvec_refs:
                    d_ref[...] = jnp.zeros_like(d_ref)

            for j, d_ref in enumerate(dvec_refs):
                d_ref[grp] += grads[nr + j]

        outs = pl.pallas_call(
            body,
            name=name + "_bwd",
            grid=(t // tm,),
            in_specs=[row_spec(k, r) for k, r in enumerate(rows)]
            + [pl.BlockSpec(v.shape, lambda i: (0, 0, 0)) for v in vecs]
            + [pl.BlockSpec((tm, g.shape[1]), lambda i: (i, 0)) for g in gs]
            + [pl.BlockSpec((tm, a.shape[1]), lambda i: (i, 0)) for a in acc_in],
            out_specs=[pl.BlockSpec((tm, row_width(k, rows[k])), lambda i: (i, 0)) for k in diff_rows]
            + [pl.BlockSpec(v.shape, lambda i: (0, 0, 0)) for v in vecs],
            out_shape=[jax.ShapeDtypeStruct((t, row_width(k, rows[k])), drow_dtypes.get(k, rows[k].dtype))
                       for k in diff_rows]
            + [jax.ShapeDtypeStruct(v.shape, F32) for v in vecs],
            compiler_params=_params(("arbitrary",)),
        )(*rows, *vecs, *gs, *acc_in)
        return outs[:nd], outs[nd:]

    return fwd_call, bwd_call


def _f_norm_mod(x, g, scale, shift):
    r = lax.rsqrt(jnp.mean(x * x, axis=-1, keepdims=True) + NORM_EPS)
    return ((x * r * g) * (1.0 + scale) + shift,)


def _f_gate_na(o, z):
    return (o.astype(F32) * _silu(z.astype(F32)),)


def _f_merge(g_na, g_ret, y_na, y_ret):
    return (jax.nn.sigmoid(g_na.astype(F32)) * y_na.astype(F32) + jax.nn.sigmoid(g_ret.astype(F32)) * y_ret.astype(F32),)


def _f_residual(x, out, gate):
    return (x + gate * out,)


def _f_loss(x, target, g):
    r = lax.rsqrt(jnp.mean(x * x, axis=-1, keepdims=True) + NORM_EPS)
    y = x * r * g
    e = 0.5 * jnp.mean(jnp.square(y - target), axis=-1, keepdims=True)
    return (jnp.broadcast_to(e * (1.0 / 128.0), (x.shape[0], 128)),)


def _gate_ret_fwd_call(of, ob, z, zblk, tm):
    t, w = of.shape
    nh = w // RET_VAL_DIM

    def body(of_ref, ob_ref, z_ref, a_ref):
        for hh in range(nh):
            sl = slice(hh * RET_VAL_DIM, (hh + 1) * RET_VAL_DIM)
            o = of_ref[:, sl].astype(F32) + ob_ref[:, sl].astype(F32)
            r = lax.rsqrt(jnp.mean(o * o, axis=-1, keepdims=True) + NORM_EPS)
            a_ref[:, sl] = ((o * r) * _silu(z_ref[:, sl].astype(F32))).astype(a_ref.dtype)

    spec = pl.BlockSpec((tm, w), lambda i: (i, 0))
    zspec = pl.BlockSpec((tm, w), lambda i: (i, zblk))
    return pl.pallas_call(
        body, name="gate_ret_fwd", grid=(t // tm,), in_specs=[spec, spec, zspec], out_specs=spec,
        out_shape=jax.ShapeDtypeStruct((t, w), BF16), compiler_params=_params(("parallel",)),
    )(of, ob, z)


def _gate_ret_bwd_call(of, ob, z, zblk, da, tm):
    t, w = of.shape
    nh = w // RET_VAL_DIM

    def body(of_ref, ob_ref, z_ref, da_ref, do_ref, dz_ref):
        for hh in range(nh):
            sl = slice(hh * RET_VAL_DIM, (hh + 1) * RET_VAL_DIM)
            o = of_ref[:, sl].astype(F32) + ob_ref[:, sl].astype(F32)
            r = lax.rsqrt(jnp.mean(o * o, axis=-1, keepdims=True) + NORM_EPS)
            n = o * r
            zf = z_ref[:, sl].astype(F32)
            sg = jax.nn.sigmoid(zf)
            g = da_ref[:, sl].astype(F32)
            dn = g * (zf * sg)
            dz_ref[:, sl] = (g * n * (sg * (1.0 + zf * (1.0 - sg)))).astype(dz_ref.dtype)
            do_ref[:, sl] = (r * (dn - n * jnp.mean(dn * n, axis=-1, keepdims=True))).astype(do_ref.dtype)

    spec = pl.BlockSpec((tm, w), lambda i: (i, 0))
    zspec = pl.BlockSpec((tm, w), lambda i: (i, zblk))
    return pl.pallas_call(
        body, name="gate_ret_bwd", grid=(t // tm,), in_specs=[spec, spec, zspec, spec], out_specs=[spec, spec],
        out_shape=[jax.ShapeDtypeStruct((t, w), BF16), jax.ShapeDtypeStruct((t, w), z.dtype)],
        compiler_params=_params(("parallel",)),
    )(of, ob, z, da)


NA_PAIR = 2 * GRID_W
NA_KEY_ROWS = NA_WIN_ROWS + 2
NA_CLASSES = 5


def _na_geometry(t, n_lat):
    rows = n_lat // GRID_W
    assert rows % 2 == 0 and rows >= NA_KEY_ROWS + 2, rows
    return rows, rows // 2, NA_KEY_ROWS * GRID_W, t - n_lat, t // NA_PAIR


def _na_base(p, rows):
    return jnp.clip(2 * p - NA_WIN_ROWS // 2, 0, rows - NA_KEY_ROWS)


def _na_class(p, rows):
    return p - _na_base(p, rows) // 2


def _na_group(pairs, n_ctx):
    assert n_ctx % NA_PAIR == 0, n_ctx
    return 2 if pairs % 2 == 0 and (n_ctx // NA_PAIR) % 2 == 0 else 1


def _na_bias_spec(i, grp, rows, pairs, n_loc):
    return pl.BlockSpec((1, 1, 2, NA_KEY_ROWS // 2, GRID_W, NA_PAIR),
                        lambda h, g: (h, _na_class(jnp.minimum(g * grp + i, pairs - grp + i), rows), 0, 0, 0, 0))


def _na_bias_tile(bt_ref):
    return jnp.concatenate(
        [jnp.concatenate([bt_ref[0, 0, i, q] for q in range(NA_KEY_ROWS // 2)], axis=1) for i in range(2)], axis=0)


def _na_store_bias_grad(dbt_ref, ds, accumulate):
    for i in range(2):
        for q in range(NA_KEY_ROWS // 2):
            tile = ds[i * GRID_W:(i + 1) * GRID_W, q * NA_PAIR:(q + 1) * NA_PAIR]
            if accumulate:
                dbt_ref[0, 0, i, q] += tile
            else:
                dbt_ref[0, 0, i, q] = tile


def _na_fwd_call(q, k, v, col0, w, bt, n_lat):
    t = q.shape[0]
    nh = w // NA_HEAD_DIM
    rows, pairs, n_loc, n_ctx, nq = _na_geometry(t, n_lat)
    grp = _na_group(pairs, n_ctx)
    scale = NA_HEAD_DIM ** -0.5

    def body(q_ref, k_ref, v_ref, *rest):
        bt_refs, o_ref = rest[:grp], rest[grp]
        g = pl.program_id(1)
        kc = k_ref[pl.ds(n_lat, n_ctx), :]
        vc = v_ref[pl.ds(n_lat, n_ctx), :]

        @pl.when(g < pairs // grp)
        def _():
            for i in range(grp):
                sl = slice(i * NA_PAIR, (i + 1) * NA_PAIR)
                qb = q_ref[sl, :]
                s_ctx = _dot_nt(qb, kc) * scale
                start = pl.multiple_of(_na_base(g * grp + i, rows) * GRID_W, GRID_W)
                kw = k_ref[pl.ds(start, n_loc), :]
                vw = v_ref[pl.ds(start, n_loc), :]
                s_loc = _dot_nt(qb, kw) * scale + _na_bias_tile(bt_refs[i])
                m = jnp.maximum(jnp.max(s_loc, axis=-1, keepdims=True), jnp.max(s_ctx, axis=-1, keepdims=True))
                p_loc = jnp.exp(s_loc - m)
                p_ctx = jnp.exp(s_ctx - m)
                l = jnp.sum(p_loc, axis=-1, keepdims=True) + jnp.sum(p_ctx, axis=-1, keepdims=True)
                o = _dot(p_loc.astype(BF16), vw) + _dot(p_ctx.astype(BF16), vc)
                o_ref[sl, :] = (o / l).astype(o_ref.dtype)

        @pl.when(g >= pairs // grp)
        def _():
            s_ctx = _dot_nt(q_ref[...], kc) * scale
            m = jnp.max(s_ctx, axis=-1, keepdims=True)
            p = jnp.exp(s_ctx - m)
            l = jnp.sum(p, axis=-1, keepdims=True)
            o_ref[...] = (_dot(p.astype(BF16), vc) / l).astype(o_ref.dtype)

    qspec = pl.BlockSpec((grp * NA_PAIR, NA_HEAD_DIM), lambda h, g: (g, h))
    in_q = pl.BlockSpec((grp * NA_PAIR, NA_HEAD_DIM), lambda h, g: (g, col0[0] + h))
    in_k = pl.BlockSpec((t, NA_HEAD_DIM), lambda h, g: (0, col0[1] + h))
    in_v = pl.BlockSpec((t, NA_HEAD_DIM), lambda h, g: (0, col0[2] + h))
    return pl.pallas_call(
        body,
        name="na_attn_fwd",
        grid=(nh, nq // grp),
        in_specs=[in_q, in_k, in_v] + [_na_bias_spec(i, grp, rows, pairs, n_loc) for i in range(grp)],
        out_specs=qspec,
        out_shape=jax.ShapeDtypeStruct((t, w), BF16),
        compiler_params=_params(("parallel", "arbitrary")),
    )(q, k, v, *([bt] * grp))


def _na_bwd_call(q, k, v, col0, w, bt, do, n_lat):
    t = q.shape[0]
    nh = w // NA_HEAD_DIM
    rows, pairs, n_loc, n_ctx, nq = _na_geometry(t, n_lat)
    scale = NA_HEAD_DIM ** -0.5
    grp = _na_group(pairs, n_ctx)

    def body(q_ref, k_ref, v_ref, do_ref, *rest):
        bt_refs = rest[:grp]
        dq_ref, dk_ref, dv_ref = rest[2 * grp:2 * grp + 3]
        dbt_refs = rest[2 * grp + 3:]
        g = pl.program_id(1)

        @pl.when(g == 0)
        def _():
            dk_ref[...] = jnp.zeros_like(dk_ref)
            dv_ref[...] = jnp.zeros_like(dv_ref)

        kc = k_ref[pl.ds(n_lat, n_ctx), :]
        vc = v_ref[pl.ds(n_lat, n_ctx), :]

        @pl.when(g < pairs // grp)
        def _():
            for i in range(grp):
                p = g * grp + i
                sl = slice(i * NA_PAIR, (i + 1) * NA_PAIR)
                qb = q_ref[sl, :]
                dob = do_ref[sl, :]
                s_ctx = _dot_nt(qb, kc) * scale
                dp_ctx = _dot_nt(dob, vc)
                start = pl.multiple_of(_na_base(p, rows) * GRID_W, GRID_W)
                kw = k_ref[pl.ds(start, n_loc), :]
                vw = v_ref[pl.ds(start, n_loc), :]
                s_loc = _dot_nt(qb, kw) * scale + _na_bias_tile(bt_refs[i])
                m = jnp.maximum(jnp.max(s_loc, axis=-1, keepdims=True), jnp.max(s_ctx, axis=-1, keepdims=True))
                p_loc = jnp.exp(s_loc - m)
                p_ctx = jnp.exp(s_ctx - m)
                inv = 1.0 / (jnp.sum(p_loc, axis=-1, keepdims=True) + jnp.sum(p_ctx, axis=-1, keepdims=True))
                p_loc = p_loc * inv
                p_ctx = p_ctx * inv
                dp_loc = _dot_nt(dob, vw)
                delta = (jnp.sum(p_loc * dp_loc, axis=-1, keepdims=True)
                         + jnp.sum(p_ctx * dp_ctx, axis=-1, keepdims=True))
                ds_loc = p_loc * (dp_loc - delta)
                ds_ctx = p_ctx * (dp_ctx - delta)
                first = jnp.logical_or(g == 0, _na_class(p, rows) != _na_class(p - grp, rows))
                dbt_ref = dbt_refs[i]

                @pl.when(first)
                def _():
                    _na_store_bias_grad(dbt_ref, ds_loc, False)

                @pl.when(jnp.logical_not(first))
                def _():
                    _na_store_bias_grad(dbt_ref, ds_loc, True)

                dsl = (ds_loc * scale).astype(BF16)
                dsc = (ds_ctx * scale).astype(BF16)
                dq_ref[sl, :] = (_dot(dsl, kw) + _dot(dsc, kc)).astype(dq_ref.dtype)
                dk_ref[pl.ds(start, n_loc), :] += _dot_tn(dsl, qb)
                dv_ref[pl.ds(start, n_loc), :] += _dot_tn(p_loc.astype(BF16), dob)
                dk_ref[pl.ds(n_lat, n_ctx), :] += _dot_tn(dsc, qb)
                dv_ref[pl.ds(n_lat, n_ctx), :] += _dot_tn(p_ctx.astype(BF16), dob)

        @pl.when(g >= pairs // grp)
        def _():
            qb = q_ref[...]
            dob = do_ref[...]
            s_ctx = _dot_nt(qb, kc) * scale
            dp_ctx = _dot_nt(dob, vc)
            m = jnp.max(s_ctx, axis=-1, keepdims=True)
            p = jnp.exp(s_ctx - m)
            p = p * (1.0 / jnp.sum(p, axis=-1, keepdims=True))
            delta = jnp.sum(p * dp_ctx, axis=-1, keepdims=True)
            dsc = (p * (dp_ctx - delta) * scale).astype(BF16)
            dq_ref[...] = _dot(dsc, kc).astype(dq_ref.dtype)
            dk_ref[pl.ds(n_lat, n_ctx), :] += _dot_tn(dsc, qb)
            dv_ref[pl.ds(n_lat, n_ctx), :] += _dot_tn(p.astype(BF16), dob)

    qspec = pl.BlockSpec((grp * NA_PAIR, NA_HEAD_DIM), lambda h, g: (g, h))
    kspec = pl.BlockSpec((t, NA_HEAD_DIM), lambda h, g: (0, h))
    bspecs = [_na_bias_spec(i, grp, rows, pairs, n_loc) for i in range(grp)]
    zeros = [jnp.zeros(bt.shape, F32) for _ in range(grp)]
    outs = pl.pallas_call(
        body,
        name="na_attn_bwd",
        grid=(nh, nq // grp),
        in_specs=[pl.BlockSpec((grp * NA_PAIR, NA_HEAD_DIM), lambda h, g: (g, col0[0] + h)),
                  pl.BlockSpec((t, NA_HEAD_DIM), lambda h, g: (0, col0[1] + h)),
                  pl.BlockSpec((t, NA_HEAD_DIM), lambda h, g: (0, col0[2] + h)), qspec] + bspecs + [ANY] * grp,
        out_specs=[qspec, kspec, kspec] + bspecs,
        out_shape=[
            jax.ShapeDtypeStruct((t, w), BF16),
            jax.ShapeDtypeStruct((t, w), F32),
            jax.ShapeDtypeStruct((t, w), F32),
        ] + [jax.ShapeDtypeStruct(bt.shape, F32)] * grp,
        input_output_aliases={4 + grp + i: 3 + i for i in range(grp)},
        compiler_params=_params(("parallel", "arbitrary")),
    )(q, k, v, do, *([bt] * grp), *zeros)
    dbt = outs[3]
    for extra in outs[4:]:
        dbt = dbt + extra
    return outs[0], outs[1], outs[2], dbt


def _na_bias_table(rpb, rows):
    pairs = rows // 2
    nb = 2 * NA_WIN_COLS - 1
    nq = NA_KEY_ROWS // 2
    e1 = np.zeros((NA_CLASSES, 2, nq, 2, 2 * NA_WIN_ROWS - 1), np.float32)
    valid = np.zeros((NA_CLASSES, 2, nq, 2), bool)
    for cls, p in enumerate((0, 1, 2, pairs - 2, pairs - 1)):
        base = int(np.clip(2 * p - NA_WIN_ROWS // 2, 0, rows - NA_KEY_ROWS))
        assert p - base // 2 == cls, (rows, cls, p, base)
        for i in range(2):
            r = 2 * p + i
            r0 = int(np.clip(r - NA_WIN_ROWS // 2, 0, rows - NA_WIN_ROWS))
            for kk in range(NA_KEY_ROWS):
                if r0 <= base + kk < r0 + NA_WIN_ROWS:
                    valid[cls, i, kk // 2, kk % 2] = True
                    e1[cls, i, kk // 2, kk % 2, base + kk - r + NA_WIN_ROWS - 1] = 1.0
    cidx = np.arange(GRID_W)
    dc = np.clip(cidx[None, :] - cidx[:, None] + (NA_WIN_COLS - 1), 0, nb - 1)
    c0 = np.clip(cidx - NA_WIN_COLS // 2, 0, GRID_W - NA_WIN_COLS)
    col_in = (cidx[None, :] >= c0[:, None]) & (cidx[None, :] < c0[:, None] + NA_WIN_COLS)
    e2 = np.zeros((GRID_W, 2, GRID_W, 2, nb), np.float32)
    for par in range(2):
        e2[np.arange(GRID_W)[:, None], par, np.arange(GRID_W)[None, :], par, dc] = 1.0
    mask = valid[:, :, :, None, :, None] & col_in[None, None, None, :, None, :]
    t1 = jnp.einsum("hab,xiqpa->hxiqpb", rpb, jnp.asarray(e1), precision=lax.Precision.HIGHEST)
    t1 = t1.reshape(t1.shape[:4] + (2 * nb,))
    b = jnp.einsum("hxiqm,cwm->hxiqcw", t1, jnp.asarray(e2.reshape(GRID_W, 2 * GRID_W, 2 * nb)),
                   precision=lax.Precision.HIGHEST)
    return jnp.where(jnp.asarray(mask.reshape(NA_CLASSES, 2, nq, GRID_W, 2 * GRID_W))[None], b, MASK_VALUE)


def _ret_decays(lam_s, reverse):
    c = RET_CHUNK
    ii = lax.broadcasted_iota(jnp.int32, (c, c), 0)
    jj = lax.broadcasted_iota(jnp.int32, (c, c), 1)
    d = (jj - ii) if reverse else (ii - jj)
    dpos = jnp.maximum(d.astype(F32), 0.0)
    mask = jnp.where(d >= 0, jnp.exp(dpos * lam_s), 0.0)
    pi = lax.broadcasted_iota(jnp.int32, (c, 1), 0).astype(F32)
    qpos = (c - pi) if reverse else (pi + 1.0)
    kpos = pi if reverse else (c - 1.0 - pi)
    qd = jnp.exp(qpos * lam_s)
    kd = jnp.exp(kpos * lam_s)
    g = jnp.exp(jnp.full((1, RET_VAL_DIM), c * lam_s, F32))
    return mask, dpos, qd, kd, qpos, kpos, g


def _ret_head_group(nh):
    return _pick(nh, (8, 4, 2))


def _ret_chunk_of(t, nt, nl, reverse):
    return (nt - 1 - t) if reverse else (t + nl) % nt


def _ret_fwd_call(qr, kr, v, vcol, lam, n_lat, reverse):
    t = qr.shape[0]
    nh = qr.shape[1] // RET_KEY_DIM
    c = RET_CHUNK
    nt, nl = t // c, n_lat // c

    hg = _ret_head_group(nh)
    dk, dv = RET_KEY_DIM, RET_VAL_DIM

    def body(lam_ref, q_ref, k_ref, v_ref, o_ref, s_ref, state):
        hb, step = pl.program_id(0), pl.program_id(1)

        @pl.when(step == 0)
        def _():
            state[...] = jnp.zeros_like(state)

        for j in range(hg):
            mask, _, qd, kd, _, _, g = _ret_decays(lam_ref[hb * hg + j], reverse)
            q, k, vv = q_ref[:, j * dk:(j + 1) * dk], k_ref[:, j * dk:(j + 1) * dk], v_ref[:, j * dv:(j + 1) * dv]
            p = _dot_nt(q, k) * mask
            s = state[j]
            qs = (q.astype(F32) * qd).astype(BF16)
            o_ref[:, j * dv:(j + 1) * dv] = (_dot(p.astype(BF16), vv) + _dot(qs, s.astype(BF16))).astype(o_ref.dtype)
            s_ref[j, 0] = s
            ks = (k.astype(F32) * kd).astype(BF16)
            state[j] = s * g + _dot_tn(ks, vv)

    def cmap(hb, step, lam_ref):
        return (_ret_chunk_of(step, nt, nl, reverse), hb)

    def vmap(hb, step, lam_ref):
        return (_ret_chunk_of(step, nt, nl, reverse), vcol // (hg * dv) + hb)

    return pl.pallas_call(
        body,
        name="retention_rev_fwd" if reverse else "retention_fwd",
        grid_spec=pltpu.PrefetchScalarGridSpec(
            num_scalar_prefetch=1,
            grid=(nh // hg, nt),
            in_specs=[
                pl.BlockSpec((c, hg * dk), cmap),
                pl.BlockSpec((c, hg * dk), cmap),
                pl.BlockSpec((c, hg * dv), vmap),
            ],
            out_specs=[
                pl.BlockSpec((c, hg * dv), cmap),
                pl.BlockSpec((hg, 1, dk, dv), lambda hb, step, lam_ref: (hb, step, 0, 0)),
            ],
            scratch_shapes=[pltpu.VMEM((hg, dk, dv), F32)],
        ),
        out_shape=[
            jax.ShapeDtypeStruct((t, nh * RET_VAL_DIM), BF16),
            jax.ShapeDtypeStruct((nh, nt, RET_KEY_DIM, RET_VAL_DIM), F32),
        ],
        compiler_params=_params(("parallel", "arbitrary")),
    )(lam, qr, kr, v)


def _ret_bwd_call(qr, kr, v, vcol, lam, states, do, n_lat, reverse):
    t = qr.shape[0]
    nh = qr.shape[1] // RET_KEY_DIM
    c = RET_CHUNK
    nt, nl = t // c, n_lat // c

    hg = _ret_head_group(nh)
    dk, dv = RET_KEY_DIM, RET_VAL_DIM

    def body(lam_ref, q_ref, k_ref, v_ref, s_ref, do_ref, dq_ref, dk_ref, dv_ref, dl_ref, dstate):
        hb, rstep = pl.program_id(0), pl.program_id(1)

        @pl.when(rstep == 0)
        def _():
            dstate[...] = jnp.zeros_like(dstate)
            dl_ref[...] = jnp.zeros_like(dl_ref)

        for j in range(hg):
            mask, dpos, qd, kd, qpos, kpos, g = _ret_decays(lam_ref[hb * hg + j], reverse)
            ksl, vsl = slice(j * dk, (j + 1) * dk), slice(j * dv, (j + 1) * dv)
            q, k, vv = q_ref[:, ksl], k_ref[:, ksl], v_ref[:, vsl]
            qf, kf = q.astype(F32), k.astype(F32)
            s = s_ref[j, 0]
            ds = dstate[j]
            dob = do_ref[:, vsl].astype(BF16)
            sb, dsb = s.astype(BF16), ds.astype(BF16)
            a = _dot_nt(q, k)
            p = a * mask
            dp = _dot_nt(dob, vv)
            da = dp * mask
            dab = da.astype(BF16)
            dqc = _dot_nt(dob, sb)
            dkc = _dot_nt(vv, dsb)
            qs = (qf * qd).astype(BF16)
            ks = (kf * kd).astype(BF16)
            dq_ref[:, ksl] = (_dot(dab, k) + dqc * qd).astype(dq_ref.dtype)
            dk_ref[:, ksl] = (_dot_tn(dab, q) + dkc * kd).astype(dk_ref.dtype)
            dv_ref[:, vsl] = (_dot_tn(p.astype(BF16), dob) + _dot(ks, dsb)).astype(dv_ref.dtype)
            terms = (
                jnp.sum(jnp.sum(da * a * dpos, axis=1, keepdims=True), axis=0, keepdims=True)
                + jnp.sum(jnp.sum(dqc * qf * (qd * qpos), axis=1, keepdims=True), axis=0, keepdims=True)
                + jnp.sum(jnp.sum(dkc * kf * (kd * kpos), axis=1, keepdims=True), axis=0, keepdims=True)
                + jnp.sum(jnp.sum(ds * s * (g * c), axis=1, keepdims=True), axis=0, keepdims=True)
            )
            dl_ref[j] += jnp.broadcast_to(terms, (8, 128))
            dstate[j] = ds * g + _dot_tn(qs, dob)

    def cmap(hb, rstep, lam_ref):
        return (_ret_chunk_of(nt - 1 - rstep, nt, nl, reverse), hb)

    def vmap(hb, rstep, lam_ref):
        return (_ret_chunk_of(nt - 1 - rstep, nt, nl, reverse), vcol // (hg * dv) + hb)

    return pl.pallas_call(
        body,
        name="retention_rev_bwd" if reverse else "retention_bwd",
        grid_spec=pltpu.PrefetchScalarGridSpec(
            num_scalar_prefetch=1,
            grid=(nh // hg, nt),
            in_specs=[
                pl.BlockSpec((c, hg * dk), cmap),
                pl.BlockSpec((c, hg * dk), cmap),
                pl.BlockSpec((c, hg * dv), vmap),
                pl.BlockSpec((hg, 1, dk, dv), lambda hb, rstep, lam_ref: (hb, nt - 1 - rstep, 0, 0)),
                pl.BlockSpec((c, hg * dv), cmap),
            ],
            out_specs=[
                pl.BlockSpec((c, hg * dk), cmap),
                pl.BlockSpec((c, hg * dk), cmap),
                pl.BlockSpec((c, hg * dv), cmap),
                pl.BlockSpec((hg, 8, 128), lambda hb, rstep, lam_ref: (hb, 0, 0)),
            ],
            scratch_shapes=[pltpu.VMEM((hg, dk, dv), F32)],
        ),
        out_shape=[
            jax.ShapeDtypeStruct(qr.shape, qr.dtype),
            jax.ShapeDtypeStruct(kr.shape, kr.dtype),
            jax.ShapeDtypeStruct((t, nh * dv), v.dtype),
            jax.ShapeDtypeStruct((nh, 8, 128), F32),
        ],
        compiler_params=_params(("parallel", "arbitrary")),
    )(lam, qr, kr, v, states, do)


def _rope_tables(t, n_lat):
    nf = RET_KEY_DIM // 4
    tok = np.arange(n_lat)
    inv_freq = (ROPE_BASE ** (-np.arange(nf, dtype=np.float32) / nf)).astype(np.float32)
    row = (tok // GRID_W).astype(np.float32)
    col = (tok % GRID_W).astype(np.float32)
    ang = np.concatenate([row[:, None] * inv_freq, col[:, None] * inv_freq], axis=-1).astype(np.float32)
    cos = np.ones((t, 2 * nf), np.float32)
    sin = np.zeros((t, 2 * nf), np.float32)
    cos[:n_lat] = np.cos(ang)
    sin[:n_lat] = np.sin(ang)
    return jnp.asarray(np.concatenate([cos, cos], axis=1)), jnp.asarray(np.concatenate([-sin, sin], axis=1))


def _rotate(x, cos2, sin2):
    return x * cos2 + pltpu.roll(x, RET_KEY_DIM // 2, 1) * sin2


def _rope_fwd_call(u, qblk, kblk, w_qk, cos2, sin2, k_scale, tm):
    t = u.shape[0]
    nh = w_qk // RET_KEY_DIM

    def body(q_ref, k_ref, c_ref, s_ref, qr_ref, kr_ref):
        c, s = c_ref[...], s_ref[...]
        for hh in range(nh):
            sl = slice(hh * RET_KEY_DIM, (hh + 1) * RET_KEY_DIM)
            qr_ref[:, sl] = _rotate(q_ref[:, sl].astype(F32), c, s).astype(qr_ref.dtype)
            kr_ref[:, sl] = (_rotate(k_ref[:, sl].astype(F32), c, s) * k_scale).astype(kr_ref.dtype)

    tab = pl.BlockSpec((tm, RET_KEY_DIM), lambda i: (i, 0))
    out = pl.BlockSpec((tm, w_qk), lambda i: (i, 0))
    return pl.pallas_call(
        body, name="rope_fwd", grid=(t // tm,),
        in_specs=[pl.BlockSpec((tm, w_qk), lambda i: (i, qblk)), pl.BlockSpec((tm, w_qk), lambda i: (i, kblk)), tab, tab],
        out_specs=[out, out], out_shape=[jax.ShapeDtypeStruct((t, w_qk), BF16)] * 2,
        compiler_params=_params(("parallel",)),
    )(u, u, cos2, sin2)


def _assemble_du_call(pieces, off, cos2, sin2, k_scale, tm):
    flat = [a for p in pieces for a in (p if isinstance(p, tuple) else (p,))]
    t = flat[0].shape[0]
    n = len(flat)

    def body(*refs):
        c, s = refs[n][...], -refs[n + 1][...]
        o_ref = refs[n + 2]
        it = iter(refs[:n])
        for blk, p in enumerate(pieces):
            lo = off[blk]
            if not isinstance(p, tuple):
                o_ref[:, lo:off[blk + 1]] = next(it)[...].astype(o_ref.dtype)
                continue
            fwd_ref, rev_ref = next(it), next(it)
            if blk == 6:
                o_ref[:, lo:off[blk + 1]] = (fwd_ref[...].astype(F32) + rev_ref[...].astype(F32)).astype(o_ref.dtype)
                continue
            mult = k_scale if blk == 5 else 1.0
            for hh in range((off[blk + 1] - lo) // RET_KEY_DIM):
                sl = slice(hh * RET_KEY_DIM, (hh + 1) * RET_KEY_DIM)
                dy = (fwd_ref[:, sl].astype(F32) + rev_ref[:, sl].astype(F32)) * mult
                o_ref[:, lo + hh * RET_KEY_DIM:lo + (hh + 1) * RET_KEY_DIM] = _rotate(dy, c, s).astype(o_ref.dtype)

    tab = pl.BlockSpec((tm, RET_KEY_DIM), lambda i: (i, 0))
    return pl.pallas_call(
        body, name="assemble_du", grid=(t // tm,),
        in_specs=[pl.BlockSpec((tm, a.shape[1]), lambda i: (i, 0)) for a in flat] + [tab, tab],
        out_specs=pl.BlockSpec((tm, off[-1]), lambda i: (i, 0)),
        out_shape=jax.ShapeDtypeStruct((t, off[-1]), BF16),
        compiler_params=_params(("parallel",)),
    )(*flat, cos2, sin2)


def _my_position():
    return lax.axis_index("x"), lax.axis_index("y"), lax.axis_index("c")


def _flip(pos, k):
    x, y, c = pos
    return (1 - x if k & 4 else x, 1 - y if k & 2 else y, 1 - c if k & 1 else c)


def _linear(pos):
    return 4 * pos[0] + 2 * pos[1] + pos[2]


def _slab(ref, axis, idx, size):
    start = pl.multiple_of(idx * size, size)
    return ref.at[pl.ds(start, size), :] if axis == 0 else ref.at[:, pl.ds(start, size)]


HBM_SPEC = pl.BlockSpec(memory_space=pltpu.HBM)
SEM_SPEC = pl.BlockSpec(memory_space=pltpu.SEMAPHORE)
DATAFLOW = pltpu.SideEffectType.DATAFLOW_SIDE_EFFECTING
PEER_BITS = (1, 2, 4, 6, 3, 5, 7)
GATHER_BITS = (1, 2, 4, 6)


def _in_hbm(a):
    return pltpu.with_memory_space_constraint(a, pltpu.HBM)


def _gather_views(me, k, a, src_refs, land_refs, axes):
    size = src_refs[a].shape[axes[a]]
    peer = _flip(me, k)
    return src_refs[a], _slab(land_refs[a], axes[a], _linear(me), size), _slab(land_refs[a], axes[a], _linear(peer), size)


def _scatter_views(me, k, a, src_refs, land_refs, axes):
    size = land_refs[a].shape[1 + axes[a]]
    peer = _flip(me, k)
    return _slab(src_refs[a], axes[a], _linear(peer), size), land_refs[a].at[k - 1], land_refs[a].at[k - 1]


CHIP_BITS = (0, 2, 4, 6)


def _chip_views(me, k, a, src_refs, land_refs, axes):
    j = CHIP_BITS.index(k)
    return src_refs[a].at[j], land_refs[a].at[j - 1], land_refs[a].at[j - 1]


def _pair_exchange(grads, axes, sizes):
    ns = len(grads)

    def slab_shape(a):
        s = grads[a].shape
        return (sizes[a], s[1]) if axes[a] == 0 else (s[0], sizes[a])

    def body(*refs):
        g_refs, p_refs = refs[:ns], refs[ns:2 * ns]
        send_sems, recv_sems = refs[2 * ns:]
        me = _my_position()
        sibling = _flip(me, 1)
        copies = []
        for j, kc in enumerate(CHIP_BITS):
            for a in range(ns):
                cp = pltpu.make_async_remote_copy(
                    src_ref=_slab(g_refs[a], axes[a], _linear(_flip(me, kc | 1)), sizes[a]), dst_ref=p_refs[a].at[j],
                    send_sem=send_sems.at[4 * a + j], recv_sem=recv_sems.at[4 * a + j],
                    device_id=sibling, device_id_type=MESH)
                cp.start()
                copies.append(cp)
        for cp in copies:
            cp.wait_recv()
        for cp in copies:
            cp.wait_send()

    return pl.pallas_call(
        body, name="scatter_pair_exchange", in_specs=[ANY] * ns, out_specs=[ANY] * ns,
        out_shape=[jax.ShapeDtypeStruct((4,) + slab_shape(a), grads[a].dtype) for a in range(ns)],
        scratch_shapes=[pltpu.SemaphoreType.DMA((4 * ns,)), pltpu.SemaphoreType.DMA((4 * ns,))],
        compiler_params=pltpu.CompilerParams(has_side_effects=True),
    )(*grads)


def _pair_add(grad, theirs, axis, chip_idx):
    _, r, c = theirs.shape
    tm = _pick(r, (256, 128, 64, 32, 16))
    if axis == 0:
        mine_spec = pl.BlockSpec((tm, c), lambda j, i, idx: (idx[j] * (r // tm) + i, 0))
    else:
        mine_spec = pl.BlockSpec((tm, c), lambda j, i, idx: (i, idx[j]))

    def body(idx_ref, mine_ref, theirs_ref, o_ref):
        o_ref[0] = (mine_ref[...].astype(F32) + theirs_ref[0].astype(F32)).astype(o_ref.dtype)

    spec = pl.BlockSpec((1, tm, c), lambda j, i, idx: (j, i, 0))
    return pl.pallas_call(
        body, name="scatter_pair_add",
        grid_spec=pltpu.PrefetchScalarGridSpec(
            num_scalar_prefetch=1, grid=(4, r // tm), in_specs=[mine_spec, spec], out_specs=spec),
        out_shape=jax.ShapeDtypeStruct(theirs.shape, theirs.dtype),
        compiler_params=_params(("parallel", "parallel")),
    )(chip_idx, grad, theirs)


def _slab_block(rows, cols, tm, axis):
    if axis == 0:
        return pl.BlockSpec((tm, cols), lambda i, idx: (idx[0] * (rows // tm) + i, 0))
    return pl.BlockSpec((tm, cols), lambda i, idx: (i, idx[0]))


def _place_shard(shard, land, axis, my_idx):
    r, c = shard.shape
    tm = _pick(r, (512, 256, 128, 64, 32, 16))

    def body(idx_ref, s_ref, land_ref, o_ref):
        o_ref[...] = s_ref[...]

    return pl.pallas_call(
        body, name="gather_place",
        grid_spec=pltpu.PrefetchScalarGridSpec(
            num_scalar_prefetch=1, grid=(r // tm,),
            in_specs=[pl.BlockSpec((tm, c), lambda i, idx: (i, 0)), ANY],
            out_specs=_slab_block(r, c, tm, axis)),
        out_shape=jax.ShapeDtypeStruct(land.shape, land.dtype),
        input_output_aliases={2: 0},
        compiler_params=_params(("parallel",)),
    )(my_idx, shard, land)


def _push_start(name, srcs, lands, axes, views, bits, deps):
    ns = len(srcs)

    def body(*refs):
        src_refs, land_refs = refs[:ns], refs[ns:2 * ns]
        send_sems, recv_sems = refs[2 * ns + len(deps):2 * ns + len(deps) + 2]
        token = refs[-1]
        me = _my_position()
        for k in bits:
            for a in range(ns):
                s, d, _ = views(me, k, a, src_refs, land_refs, axes)
                pltpu.make_async_remote_copy(
                    src_ref=s, dst_ref=d, send_sem=send_sems.at[7 * a + k - 1], recv_sem=recv_sems.at[7 * a + k - 1],
                    device_id=_flip(me, k), device_id_type=MESH).start()
        token[...] = jnp.zeros_like(token)

    thru = [pltpu.HBM(a.shape, a.dtype) for a in list(srcs) + list(lands)]
    outs = pl.pallas_call(
        body, name=name,
        in_specs=[HBM_SPEC] * (2 * ns) + [ANY] * len(deps),
        out_specs=[SEM_SPEC, SEM_SPEC] + [HBM_SPEC] * (2 * ns) + [VMEM_SPEC],
        out_shape=[pltpu.SemaphoreType.DMA((7 * ns,)), pltpu.SemaphoreType.DMA((7 * ns,))] + thru
        + [jax.ShapeDtypeStruct((8, 128), F32)],
        input_output_aliases={i: 2 + i for i in range(2 * ns)},
        compiler_params=pltpu.CompilerParams(has_side_effects=DATAFLOW),
    )(*[_in_hbm(a) for a in srcs], *[_in_hbm(a) for a in lands], *deps)
    return (outs[0], outs[1]), outs[2:2 + ns], outs[2 + ns:2 + 2 * ns], outs[-1]


def _gather_finish(lands, axes, sizes):
    ns = len(lands)
    chips = (2, 4, 6)

    def body(*refs):
        land_refs = refs[ns:2 * ns]
        send_sems, recv_sems = refs[2 * ns:]
        me = _my_position()
        sibling = _flip(me, 1)
        copies = []
        for j, kc in enumerate(chips):
            for a in range(ns):
                def slab_of(pos):
                    return _slab(land_refs[a], axes[a], _linear(pos), sizes[a])
                send = pltpu.make_async_remote_copy(
                    src_ref=slab_of(_flip(me, kc)), dst_ref=slab_of(_flip(me, kc)), send_sem=send_sems.at[3 * a + j],
                    recv_sem=recv_sems.at[3 * a + j], device_id=sibling, device_id_type=MESH)
                recv = pltpu.make_async_remote_copy(
                    src_ref=slab_of(_flip(me, kc)), dst_ref=slab_of(_flip(sibling, kc)), send_sem=send_sems.at[3 * a + j],
                    recv_sem=recv_sems.at[3 * a + j], device_id=sibling, device_id_type=MESH)
                send.start()
                copies.append((send, recv))
        for send, recv in copies:
            recv.wait_recv()
        for send, recv in copies:
            send.wait_send()

    return pl.pallas_call(
        body, name="gather_finish", in_specs=[ANY] * ns, out_specs=[ANY] * ns,
        out_shape=[jax.ShapeDtypeStruct(l.shape, l.dtype) for l in lands],
        input_output_aliases={a: a for a in range(ns)},
        scratch_shapes=[pltpu.SemaphoreType.DMA((3 * ns,)), pltpu.SemaphoreType.DMA((3 * ns,))],
        compiler_params=pltpu.CompilerParams(has_side_effects=True),
    )(*lands)


def _push_wait(name, sems, srcs, lands, axes, views, bits, after):
    ns = len(srcs)

    def body(*refs):
        src_refs, land_refs = refs[:ns], refs[ns:2 * ns]
        send_sems, recv_sems = refs[2 * ns:2 * ns + 2]
        me = _my_position()
        for k in bits:
            for a in range(ns):
                s, d, got = views(me, k, a, src_refs, land_refs, axes)
                cp = pltpu.make_async_remote_copy(
                    src_ref=s, dst_ref=got, send_sem=send_sems.at[7 * a + k - 1], recv_sem=recv_sems.at[7 * a + k - 1],
                    device_id=_flip(me, k), device_id_type=MESH)
                cp.wait_send()
                cp.wait_recv()

    thru = [pltpu.HBM(a.shape, a.dtype) for a in list(srcs) + list(lands)]
    outs = pl.pallas_call(
        body, name=name,
        in_specs=[HBM_SPEC] * (2 * ns) + [SEM_SPEC, SEM_SPEC] + [ANY] * len(after),
        out_specs=[HBM_SPEC] * (2 * ns),
        out_shape=thru,
        input_output_aliases={i: i for i in range(2 * ns)},
        compiler_params=pltpu.CompilerParams(has_side_effects=DATAFLOW),
    )(*srcs, *lands, sems[0], sems[1], *after)
    return outs[:ns], outs[ns:]


def _small_allgather(v, name):
    r, c = v.shape

    def body(v_ref, all_ref, sum_ref, send_sems, recv_sems):
        me = _my_position()
        all_ref[_linear(me)] = v_ref[...]
        copies = []
        for k in range(1, N_DEV):
            peer = _flip(me, k)
            copies.append(pltpu.make_async_remote_copy(
                src_ref=v_ref, dst_ref=all_ref.at[_linear(me)], send_sem=send_sems.at[k - 1], recv_sem=recv_sems.at[k - 1],
                device_id=peer, device_id_type=MESH))
        for cp in copies:
            cp.start()
        for k in range(1, N_DEV):
            peer = _flip(me, k)
            pltpu.make_async_remote_copy(
                src_ref=v_ref, dst_ref=all_ref.at[_linear(peer)], send_sem=send_sems.at[k - 1], recv_sem=recv_sems.at[k - 1],
                device_id=peer, device_id_type=MESH).wait_recv()
        for cp in copies:
            cp.wait_send()
        acc = all_ref[0]
        for d in range(1, N_DEV):
            acc = acc + all_ref[d]
        sum_ref[...] = acc

    return pl.pallas_call(
        body,
        name=name,
        in_specs=[VMEM_SPEC],
        out_specs=[VMEM_SPEC, VMEM_SPEC],
        out_shape=[jax.ShapeDtypeStruct((N_DEV, r, c), F32), jax.ShapeDtypeStruct((r, c), F32)],
        scratch_shapes=[pltpu.SemaphoreType.DMA((N_DEV - 1,)), pltpu.SemaphoreType.DMA((N_DEV - 1,))],
        compiler_params=pltpu.CompilerParams(has_side_effects=True, vmem_limit_bytes=VMEM_LIMIT),
    )(v)


def _ada_fwd_call(cin, ada_w, ada_b_cols):
    nl, d, ncol = ada_w.shape
    nrow = cin.shape[0]

    def body(c_ref, w_ref, b_ref, o_ref):
        cs = _silu(c_ref[...]).astype(BF16)
        for l in range(nl):
            o_ref[l] = _dot(cs, w_ref[l].astype(BF16)) + b_ref[l]

    return pl.pallas_call(
        body, name="ada_fwd", in_specs=[VMEM_SPEC] * 3, out_specs=VMEM_SPEC,
        out_shape=jax.ShapeDtypeStruct((nl, nrow, ncol), F32), compiler_params=_params(),
    )(cin, ada_w, ada_b_cols)


def _ada_bwd_call(cin, ada_w, dmod):
    nl, d, ncol = ada_w.shape
    nrow = cin.shape[0]

    def body(c_ref, w_ref, dm_ref, gw_ref, dcs_ref):
        cs = _silu(c_ref[...]).astype(BF16)
        acc = jnp.zeros((nrow, d), F32)
        for l in range(nl):
            dm = dm_ref[l].astype(BF16)
            gw_ref[l] = _dot_tn(cs, dm)
            acc = acc + _dot_nt(dm, w_ref[l].astype(BF16))
        dcs_ref[...] = acc

    return pl.pallas_call(
        body, name="ada_bwd", in_specs=[VMEM_SPEC] * 3, out_specs=[VMEM_SPEC, VMEM_SPEC],
        out_shape=[jax.ShapeDtypeStruct((nl, d, ncol), F32), jax.ShapeDtypeStruct((nrow, d), F32)],
        compiler_params=_params(),
    )(cin, ada_w, dmod)


def _adamw_math(w, g, m, v):
    m = ADAM_B1 * m + (1.0 - ADAM_B1) * g
    v = ADAM_B2 * v + (1.0 - ADAM_B2) * jnp.square(g)
    m_hat = m / (1.0 - ADAM_B1 ** ADAM_STEP)
    v_hat = v / (1.0 - ADAM_B2 ** ADAM_STEP)
    delta = -ADAM_LR * (m_hat / (jnp.sqrt(v_hat) + ADAM_EPS) + ADAM_WD * w)
    return delta, m, v


def _adamw_sharded(w, m, v, mine, slabs, axis, my_idx, layer, prev, name):
    nl, r, c = w.shape
    tm = _pick(r, (128, 64, 32, 16))
    nprev = 0 if prev is None else len(prev)
    nslab = slabs.shape[0]
    if axis is None:
        mine_spec = pl.BlockSpec((1, tm, c), lambda i, idx: (0, i, 0))
    else:
        mine_spec = _slab_block(r, c, tm, axis)

    def body(idx_ref, w_ref, m_ref, v_ref, mine_ref, s_ref, *rest):
        g_ref, d_ref, nm_ref, nv_ref = rest[nprev:]
        g = (mine_ref[0] if axis is None else mine_ref[...]).astype(F32)
        for k in range(nslab):
            g = g + s_ref[k].astype(F32)
        delta, nm, nv = _adamw_math(w_ref[0], g, m_ref[0], v_ref[0])
        g_ref[0], d_ref[0], nm_ref[0], nv_ref[0] = g, delta, nm, nv

    spec = pl.BlockSpec((1, tm, c), lambda i, idx: (layer, i, 0))
    out = jax.ShapeDtypeStruct(w.shape, F32)
    return pl.pallas_call(
        body, name=name,
        grid_spec=pltpu.PrefetchScalarGridSpec(
            num_scalar_prefetch=1, grid=(r // tm,),
            in_specs=[spec, spec, spec, mine_spec,
                      pl.BlockSpec((nslab, tm, c), lambda i, idx: (0, i, 0))] + [ANY] * nprev,
            out_specs=[spec] * 4),
        out_shape=[out] * 4,
        input_output_aliases={6 + j: j for j in range(nprev)},
        compiler_params=_params(("parallel",)),
    )(my_idx, w, m, v, mine, slabs, *(() if prev is None else prev))


def _adamw_dense(w, g, m, v, name):
    r, c = w.shape
    tm = _pick(r, (256, 128, 64, 32, 16, 8))

    def body(w_ref, g_ref, m_ref, v_ref, d_ref, nm_ref, nv_ref):
        d_ref[...], nm_ref[...], nv_ref[...] = _adamw_math(w_ref[...], g_ref[...], m_ref[...], v_ref[...])

    spec = pl.BlockSpec((tm, c), lambda i: (i, 0))
    out = jax.ShapeDtypeStruct(w.shape, F32)
    return pl.pallas_call(
        body, name=name, grid=(r // tm,), in_specs=[spec] * 4, out_specs=[spec] * 3, out_shape=[out] * 3,
        compiler_params=_params(("parallel",)),
    )(w, g, m, v)


def _pack(parts, width=128):
    flat = jnp.concatenate([p.reshape(-1).astype(F32) for p in parts])
    n = flat.shape[0]
    total = -(-n // (8 * width)) * (8 * width)
    return jnp.pad(flat, (0, total - n)).reshape(total // width, width)


def _unpack(buf, shapes):
    flat = buf.reshape(-1)
    out, off = [], 0
    for s in shapes:
        n = int(np.prod(s))
        out.append(flat[off:off + n].reshape(s))
        off += n
    return out


def kernel(x, c, ctx, c_ctx, ada_w, ada_b, norm_g, w_in, na_rpb, ret_decay_logit, w_proj_na, w_proj_ret, w_out, final_g, loss_target, m_c_ctx, m_ada_w, m_ada_b, m_norm_g, m_w_in, m_na_rpb, m_ret_decay_logit, m_w_proj_na, m_w_proj_ret, m_w_out, m_final_g, v_c_ctx, v_ada_w, v_ada_b, v_norm_g, v_w_in, v_na_rpb, v_ret_decay_logit, v_w_proj_na, v_w_proj_ret, v_w_out, v_final_g):
    depth = w_in.shape[0]
    n_lat, d = x.shape[1], x.shape[2]
    n_ctx = ctx.shape[1]
    t = n_lat + n_ctx
    w_na = w_proj_na.shape[1]
    w_retv = w_proj_ret.shape[1] * N_DEV
    in_cols = w_in.shape[2] * N_DEV
    w_qk = (in_cols - 4 * w_na - 2 * w_retv - 2 * d) // 2
    sizes = (w_na, w_na, w_na, w_na, w_qk, w_qk, w_retv, w_retv, d, d)
    off = tuple(int(o) for o in np.cumsum((0,) + sizes))
    NA_Q, NA_K, NA_V, NA_Z, RET_Q, RET_K, RET_V, RET_Z, G_NA, G_RET = range(10)
    rows = n_lat // GRID_W
    me = _my_position()
    my_idx = _linear(me)
    tm_row = _pick(n_ctx, (256, 128))

    idx_arr = jnp.reshape(my_idx, (1,)).astype(jnp.int32)
    chip_idx = jnp.stack([_linear(_flip(me, kc)) for kc in CHIP_BITS]).astype(jnp.int32)

    w_axes = (1, 1, 0, 0)
    w_names = ("w_in", "w_proj_na", "w_proj_ret", "w_out")
    shard = [[w[l].astype(BF16) for w in (w_in, w_proj_na, w_proj_ret, w_out)] for l in range(depth)]
    groups = [[(0, 0)], [(0, 1), (0, 2), (0, 3)]] + [[(l, a) for a in range(4)] for l in range(1, depth)]
    gathers = {}

    def start_gather(gi, deps):
        keys = groups[gi]
        srcs = [shard[l][a] for l, a in keys]
        axes = tuple(w_axes[a] for _, a in keys)
        lands = [_place_shard(s, lax.empty(tuple(n * (N_DEV if i == ax else 1) for i, n in enumerate(s.shape)), BF16),
                              ax, idx_arr) for s, ax in zip(srcs, axes)]
        sizes = tuple(s.shape[ax] for s, ax in zip(srcs, axes))
        sems, srcs, lands, tok = _push_start(f"gather_start_{gi}", srcs, lands, axes, _gather_views, GATHER_BITS, deps)
        flight = dict(name=f"gather_wait_{gi}", sems=sems, srcs=srcs, lands=lands, axes=axes, sizes=sizes, ready=None)
        for pos, key in enumerate(keys):
            gathers[key] = (flight, pos)
        return tok

    token = start_gather(0, ())

    ncol = ada_w.shape[2]
    c_all, _ = _small_allgather(jnp.pad(c, ((0, 7), (0, 0))) + token[:, :1], "allgather_c")
    cin = jnp.concatenate([c_all[:, 0, :], c_ctx[None, :], jnp.zeros((7, d), F32)], axis=0)
    ada_b_cols = lax.dynamic_slice_in_dim(ada_b, my_idx * ncol, ncol, axis=1)[:, None, :]
    mod_cols = _ada_fwd_call(cin, ada_w, ada_b_cols)
    mod_gathered, _ = _small_allgather(mod_cols.reshape(depth * 16, ncol), "allgather_mod")
    mod_all = mod_gathered.reshape(N_DEV, depth, 16, ncol).transpose(1, 2, 0, 3).reshape(depth, 16, N_DEV * ncol)
    mod_lat = lax.dynamic_index_in_dim(mod_all, my_idx, axis=1, keepdims=False)
    mod_ctx = mod_all[:, 8, :]
    token = mod_gathered
    for gi in range(1, len(groups)):
        token = start_gather(gi, (token,))

    def landed(l, a, act):
        flight, pos = gathers[(l, a)]
        if flight["ready"] is None:
            arrived = _push_wait(flight["name"], flight["sems"], flight["srcs"], flight["lands"],
                                 flight["axes"], _gather_views, GATHER_BITS, (act, token))[1]
            flight["ready"] = _gather_finish(arrived, flight["axes"], flight["sizes"])
        return flight["ready"][pos]

    pending, scatters = {}, []

    def send_dw(l, a, dw):
        pending[(l, a)] = dw
        if a == 0:
            keys = [(0, 0)] if l == 0 else [(l, b) for b in range(4)]
        elif l == 0 and a == 1:
            keys = [(0, 1), (0, 2), (0, 3)]
        else:
            return None
        srcs = [pending[k] for k in keys]
        axes = tuple(w_axes[b] for _, b in keys)
        sizes = tuple(s.shape[ax] // N_DEV for s, ax in zip(srcs, axes))
        slab_shapes = [tuple(n // (N_DEV if i == ax else 1) for i, n in enumerate(s.shape)) for s, ax in zip(srcs, axes)]
        by_chip = keys == [(0, 0)]
        if by_chip:
            theirs = _pair_exchange(srcs, axes, sizes)
            srcs = [_pair_add(g, p, ax, chip_idx) for g, p, ax in zip(srcs, theirs, axes)]
            lands = [lax.empty((3,) + shp, BF16) for shp in slab_shapes]
            views, bits = _chip_views, CHIP_BITS[1:]
        else:
            lands = [lax.empty((N_DEV - 1,) + shp, BF16) for shp in slab_shapes]
            views, bits = _scatter_views, PEER_BITS
        sems, srcs, lands, tok = _push_start(f"scatter_start_{len(scatters)}", srcs, lands, axes, views, bits, ())
        scatters.append(dict(name=f"scatter_wait_{len(scatters)}", sems=sems, srcs=srcs, lands=lands, axes=axes, keys=keys,
                             views=views, bits=bits, by_chip=by_chip))
        return tok

    cos2, sin2 = _rope_tables(t, n_lat)
    k_scale = RET_KEY_DIM ** -0.5
    assert off[RET_Q] % w_qk == 0 and off[RET_K] % w_qk == 0
    assert off[NA_Z] % w_na == 0 and off[G_NA] % d == 0 and off[G_RET] % d == 0 and off[RET_Z] % w_retv == 0
    assert off[RET_V] % (_ret_head_group(w_retv // RET_VAL_DIM) * RET_VAL_DIM) == 0
    na_cols = tuple(off[i] // NA_HEAD_DIM for i in (NA_Q, NA_K, NA_V))
    norm_mod_fwd, norm_mod_bwd = _make_rowwise(_f_norm_mod, "norm_mod", (BF16,), (d,), n_lat, tm_row, (0,))
    gate_na_fwd, gate_na_bwd = _make_rowwise(_f_gate_na, "gate_na", (BF16,), (w_na,), n_lat, tm_row, (0, 1),
                                             col_blocks={1: (w_na, off[NA_Z] // w_na)})
    merge_fwd, merge_bwd = _make_rowwise(_f_merge, "merge", (BF16,), (d,), n_lat, tm_row, (0, 1, 2, 3),
                                         col_blocks={0: (d, off[G_NA] // d), 1: (d, off[G_RET] // d)},
                                         drow_dtypes={2: BF16, 3: BF16})
    residual_fwd, _ = _make_rowwise(_f_residual, "residual", (F32,), (d,), n_lat, tm_row, (0, 1))
    _, residual_bwd = _make_rowwise(lambda out, gate: (gate * out,), "residual", (F32,), (d,), n_lat, tm_row, (0,),
                                    drow_dtypes={0: BF16})
    loss_fwd, loss_bwd = _make_rowwise(_f_loss, "loss_head", (F32,), (128,), n_lat, tm_row, (0,))

    def pair(a, b):
        return jnp.stack([a, b])[:, None, :]

    def mod_vectors(mod_lat_l, mod_ctx_l, norm_g_l):
        shift, scale, gate = jnp.split(mod_lat_l, 3)
        c_shift, c_scale, c_gate = jnp.split(mod_ctx_l, 3)
        return pair(norm_g_l, norm_g_l), pair(scale, c_scale), pair(shift, c_shift), pair(gate, c_gate)

    def log_decay(logit):
        return jax.nn.log_sigmoid(logit.astype(F32))

    xa = jnp.concatenate([x[0], ctx[0]], axis=0)
    saved = []
    for l in range(depth):
        vecs, vecs_vjp = jax.vjp(mod_vectors, mod_lat[l], mod_ctx[l], norm_g[l])
        (h,) = norm_mod_fwd((xa,), vecs[:3])
        wl_in = landed(l, 0, h)
        u = _matmul(h, wl_in, out_dtype=BF16, name="in_proj_fwd")
        qr, kr = _rope_fwd_call(u, off[RET_Q] // w_qk, off[RET_K] // w_qk, w_qk, cos2, sin2, k_scale, tm_row)
        bt, bt_vjp = jax.vjp(lambda r: _na_bias_table(r, rows), na_rpb[l])
        lam, lam_vjp = jax.vjp(log_decay, ret_decay_logit[l])
        o_na = _na_fwd_call(u, u, u, na_cols, w_na, bt, n_lat)
        o_f, st_f = _ret_fwd_call(qr, kr, u, off[RET_V], lam[0], n_lat, False)
        o_b, st_b = _ret_fwd_call(qr, kr, u, off[RET_V], lam[1], n_lat, True)
        (a_na,) = gate_na_fwd((o_na, u), ())
        a_ret = _gate_ret_fwd_call(o_f, o_b, u, off[RET_Z] // w_retv, tm_row)
        wl_pna, wl_pret, wl_out = landed(l, 1, a_na), landed(l, 2, a_na), landed(l, 3, a_na)
        y_na = _matmul(a_na, wl_pna, out_dtype=BF16, name="proj_na_fwd")
        y_ret = _matmul(a_ret, wl_pret, out_dtype=BF16, name="proj_ret_fwd")
        (merged,) = merge_fwd((u, u, y_na, y_ret), ())
        out = _matmul(merged, wl_out, out_dtype=F32, name="out_proj_fwd")
        (xa_next,) = residual_fwd((xa, out), vecs[3:])
        saved.append(dict(xa=xa, vecs=vecs, vecs_vjp=vecs_vjp, h=h, w=(wl_in, wl_pna, wl_pret, wl_out), u=u, qr=qr, kr=kr,
                          bt=bt, bt_vjp=bt_vjp, lam=lam, lam_vjp=lam_vjp, o_na=o_na, o_f=o_f,
                          o_b=o_b, st_f=st_f, st_b=st_b, a_na=a_na, a_ret=a_ret, y_na=y_na, y_ret=y_ret,
                          merged=merged, out=out))
        xa = xa_next

    fg_pair, fg_vjp = jax.vjp(lambda g: pair(g, g), final_g)
    x_last = xa[:n_lat]
    (loss_rows,) = loss_fwd((x_last, loss_target[0]), (fg_pair,))
    loss = lax.psum(jnp.sum(loss_rows), ("x", "y", "c"))
    (dx_last,), (d_fg_pair,) = loss_bwd((x_last, loss_target[0]), (fg_pair,), (jnp.ones_like(loss_rows),))
    (d_final_g,) = fg_vjp(d_fg_pair)
    dxa = jnp.pad(dx_last, ((0, n_ctx), (0, 0)))

    d_mod_lat, d_mod_ctx, d_norm_g, d_rpb, d_decay = ([None] * depth for _ in range(5))
    for l in reversed(range(depth)):
        s = saved[l]
        u, qr, kr = s["u"], s["qr"], s["kr"]
        wl_in, wl_pna, wl_pret, wl_out = s["w"]
        (d_out,), (d_gate,) = residual_bwd((s["out"],), s["vecs"][3:], (dxa,))
        dxa_res = dxa
        send_dw(l, 3, _matmul(s["merged"], d_out, trans_a=True, out_dtype=BF16, name="out_proj_dw"))
        d_merged = _matmul(d_out, wl_out, trans_b=True, out_dtype=BF16, name="out_proj_da")
        (dg_na, dg_ret, dy_na, dy_ret), _ = merge_bwd((u, u, s["y_na"], s["y_ret"]), (), (d_merged,))
        send_dw(l, 2, _matmul(s["a_ret"], dy_ret, trans_a=True, out_dtype=BF16, name="proj_ret_dw"))
        da_ret = _matmul(dy_ret, wl_pret, trans_b=True, out_dtype=BF16, name="proj_ret_da")
        tok = send_dw(l, 1, _matmul(s["a_na"], dy_na, trans_a=True, out_dtype=BF16, name="proj_na_dw"))
        da_na = _matmul(dy_na, wl_pna, trans_b=True, out_dtype=BF16, name="proj_na_da", after=tok)
        do_ret, dz_ret = _gate_ret_bwd_call(s["o_f"], s["o_b"], u, off[RET_Z] // w_retv, da_ret, tm_row)
        (do_na, dz_na), _ = gate_na_bwd((s["o_na"], u), (), (da_na,))
        dq_f, dk_f, dv_f, dl_f = _ret_bwd_call(qr, kr, u, off[RET_V], s["lam"][0], s["st_f"], do_ret, n_lat, False)
        dq_b, dk_b, dv_b, dl_b = _ret_bwd_call(qr, kr, u, off[RET_V], s["lam"][1], s["st_b"], do_ret, n_lat, True)
        dq, dk, dv, dbt = _na_bwd_call(u, u, u, na_cols, w_na, s["bt"], do_na, n_lat)
        du = _assemble_du_call([dq, dk, dv, dz_na, (dq_f, dq_b), (dk_f, dk_b), (dv_f, dv_b), dz_ret, dg_na, dg_ret],
                               off, cos2, sin2, k_scale, _pick(n_ctx, (128,)))
        (d_rpb[l],) = s["bt_vjp"](dbt)
        (d_decay[l],) = s["lam_vjp"](jnp.stack([dl_f[:, 0, 0], dl_b[:, 0, 0]]))
        tok = send_dw(l, 0, _matmul(s["h"], du, trans_a=True, out_dtype=BF16, name="in_proj_dw"))
        dh = _matmul(du, wl_in, trans_b=True, out_dtype=BF16, name="in_proj_da", after=tok)
        (dxa,), d_vecs = norm_mod_bwd((s["xa"],), s["vecs"][:3], (dh,), acc=(dxa_res,))
        d_mod_lat[l], d_mod_ctx[l], d_norm_g[l] = s["vecs_vjp"](tuple(d_vecs) + (d_gate,))
    gx = dxa[:n_lat]
    d_mod_lat, d_mod_ctx, d_norm_g, d_rpb, d_decay = (jnp.stack(a) for a in (d_mod_lat, d_mod_ctx, d_norm_g, d_rpb, d_decay))

    small_shapes = [d_mod_lat.shape, d_mod_ctx.shape, d_norm_g.shape, d_final_g.shape, d_rpb.shape, d_decay.shape]
    packed = _pack([d_mod_lat, d_mod_ctx, d_norm_g, d_final_g, d_rpb, d_decay])
    g_all, g_sum = _small_allgather(packed, "allgather_small_grads")
    dml_sum, dmc_sum, grad_norm_g, grad_final_g, grad_na_rpb, grad_decay = _unpack(g_sum, small_shapes)
    grad_ada_b = dml_sum + dmc_sum
    dml_all = g_all.reshape(N_DEV, -1)[:, :depth * 3 * d].reshape(N_DEV, depth, 3 * d)

    def my_cols(a):
        return lax.dynamic_slice_in_dim(a, my_idx * ncol, ncol, axis=a.ndim - 1)

    dmod = jnp.concatenate(
        [my_cols(dml_all).transpose(1, 0, 2), my_cols(dmc_sum)[:, None, :], jnp.zeros((depth, 7, ncol), F32)], axis=1)
    grad_ada_w, dcs_part = _ada_bwd_call(cin, ada_w, dmod)
    _, dcs = _small_allgather(dcs_part, "allgather_dcsilu")
    sg = jax.nn.sigmoid(c_ctx)
    grad_c_ctx = dcs[8] * (sg * (1.0 + c_ctx * (1.0 - sg)))

    def flat2(a):
        return a.reshape(a.shape[0] * a.shape[1], a.shape[2])

    small_w = [c_ctx, ada_b, norm_g, na_rpb, ret_decay_logit, final_g]
    small_g = [grad_c_ctx, grad_ada_b, grad_norm_g, grad_na_rpb, grad_decay, grad_final_g]
    small_m = [m_c_ctx, m_ada_b, m_norm_g, m_na_rpb, m_ret_decay_logit, m_final_g]
    small_v = [v_c_ctx, v_ada_b, v_norm_g, v_na_rpb, v_ret_decay_logit, v_final_g]
    shp = [a.shape for a in small_w]
    ds_, nms_, nvs_ = _adamw_dense(_pack(small_w), _pack(small_g), _pack(small_m), _pack(small_v), "adamw_small")
    ds_, nms_, nvs_ = _unpack(ds_, shp), _unpack(nms_, shp), _unpack(nvs_, shp)

    d_ada, nm_ada, nv_ada = [a.reshape(ada_w.shape) for a in _adamw_dense(
        flat2(ada_w), flat2(grad_ada_w), flat2(m_ada_w), flat2(v_ada_w), "adamw_ada_w")]

    w_all = (w_in, w_proj_na, w_proj_ret, w_out)
    m_all = (m_w_in, m_w_proj_na, m_w_proj_ret, m_w_out)
    v_all = (v_w_in, v_w_proj_na, v_w_proj_ret, v_w_out)
    upd = [None] * 4
    after = d_ada
    for flight in scatters:
        mine, slabs = _push_wait(flight["name"], flight["sems"], flight["srcs"], flight["lands"], flight["axes"],
                                 flight["views"], flight["bits"], (after,))
        for (l, a), own, s in zip(flight["keys"], mine, slabs):
            upd[a] = _adamw_sharded(w_all[a], m_all[a], v_all[a], own, s, None if flight["by_chip"] else w_axes[a],
                                    idx_arr, l, upd[a], "adamw_" + w_names[a])
            after = upd[a][1]
    (g_w_in, d_w_in, nm_w_in, nv_w_in), (g_pna, d_pna, nm_pna, nv_pna) = upd[0], upd[1]
    (g_pret, d_pret, nm_pret, nv_pret), (g_out, d_out, nm_out, nv_out) = upd[2], upd[3]

    def order(cc, aw, ab, ng, wi, rp, dl, pn, pr, wo, fg):
        return [cc, aw, ab, ng, wi, rp, dl, pn, pr, wo, fg]

    grads_out = order(grad_c_ctx, grad_ada_w, grad_ada_b, grad_norm_g, g_w_in, grad_na_rpb, grad_decay, g_pna, g_pret, g_out, grad_final_g)
    delta_out = order(ds_[0], d_ada, ds_[1], ds_[2], d_w_in, ds_[3], ds_[4], d_pna, d_pret, d_out, ds_[5])
    m_out = order(nms_[0], nm_ada, nms_[1], nms_[2], nm_w_in, nms_[3], nms_[4], nm_pna, nm_pret, nm_out, nms_[5])
    v_out = order(nvs_[0], nv_ada, nvs_[1], nvs_[2], nv_w_in, nvs_[3], nvs_[4], nv_pna, nv_pret, nv_out, nvs_[5])
    return (loss, gx[None], *grads_out, *delta_out, *m_out, *v_out)
```

```python
import functools

import numpy as np
import jax
import jax.numpy as jnp
from jax import lax
from jax.experimental import pallas as pl
from jax.experimental.pallas import tpu as pltpu

F32 = jnp.float32
BF16 = jnp.bfloat16

N_DEV = 8
GRID_W = 64
NA_HEAD_DIM = 128
NA_WIN_ROWS = 8
NA_WIN_COLS = 16
RET_KEY_DIM = 128
RET_VAL_DIM = 256
RET_CHUNK = 128
ROPE_BASE = 10000.0
NORM_EPS = 1e-6
MASK_VALUE = -1e30

ADAM_LR = 0.001
ADAM_B1 = 0.9
ADAM_B2 = 0.999
ADAM_EPS = 1e-08
ADAM_WD = 0.01
ADAM_STEP = 10

VMEM_LIMIT = 48 * 1024 * 1024
MESH = pl.DeviceIdType.MESH
ANY = pl.BlockSpec(memory_space=pl.ANY)
VMEM_SPEC = pl.BlockSpec(memory_space=pltpu.VMEM)


def _params(sem=None):
    return pltpu.CompilerParams(dimension_semantics=sem, vmem_limit_bytes=VMEM_LIMIT)


def _pick(n, prefs):
    for p in prefs:
        if n % p == 0:
            return p
    return n


def _dot(a, b):
    return lax.dot_general(a, b, (((1,), (0,)), ((), ())), preferred_element_type=F32)


def _dot_nt(a, b):
    return lax.dot_general(a, b, (((1,), (1,)), ((), ())), preferred_element_type=F32)


def _dot_tn(a, b):
    return lax.dot_general(a, b, (((0,), (0,)), ((), ())), preferred_element_type=F32)


def _silu(x):
    return x * jax.nn.sigmoid(x)


def _matmul(a, b, *, trans_a=False, trans_b=False, out_dtype=F32, name="matmul", after=None):
    if trans_a:
        kdim, m = a.shape
    else:
        m, kdim = a.shape
    if trans_b:
        n, kb = b.shape
    else:
        kb, n = b.shape
    assert kdim == kb, (a.shape, b.shape, trans_a, trans_b)
    tm = _pick(m, (1152, 1024, 768, 512, 256, 128))
    tn = _pick(n, (1024, 512, 256, 128) if trans_b else (512, 256, 128))
    tk = _pick(kdim, (2304, 2048, 1024, 512, 256, 128))
    nk = kdim // tk
    dn = (((0 if trans_a else 1,), (1 if trans_b else 0,)), ((), ()))

    def body(a_ref, b_ref, *rest):
        o_ref, acc_ref = rest[-2:]
        part = lax.dot_general(a_ref[...], b_ref[...], dn, preferred_element_type=F32)
        if nk == 1:
            o_ref[...] = part.astype(o_ref.dtype)
        else:
            k = pl.program_id(2)

            @pl.when(k == 0)
            def _():
                acc_ref[...] = part

            @pl.when(k > 0)
            def _():
                acc_ref[...] += part

            @pl.when(k == nk - 1)
            def _():
                o_ref[...] = acc_ref[...].astype(o_ref.dtype)

    a_spec = pl.BlockSpec((tk, tm), lambda i, j, k: (k, i)) if trans_a else pl.BlockSpec((tm, tk), lambda i, j, k: (i, k))
    b_spec = pl.BlockSpec((tn, tk), lambda i, j, k: (j, k)) if trans_b else pl.BlockSpec((tk, tn), lambda i, j, k: (k, j))
    return pl.pallas_call(
        body,
        name=name,
        grid=(m // tm, n // tn, nk),
        in_specs=[a_spec, b_spec] + ([] if after is None else [ANY]),
        out_specs=pl.BlockSpec((tm, tn), lambda i, j, k: (i, j)),
        out_shape=jax.ShapeDtypeStruct((m, n), out_dtype),
        scratch_shapes=[pltpu.VMEM((tm, tn) if nk > 1 else (8, 128), F32)],
        compiler_params=_params(("parallel", "parallel", "arbitrary")),
    )(*((a, b) if after is None else (a, b, after)))


def _make_rowwise(f, name, out_dtypes, out_cols, n_lat, tm, diff_rows, col_blocks=None, drow_dtypes=None):
    drow_dtypes = drow_dtypes or {}

    def tile_fn(*args):
        return tuple(o.astype(dt) for o, dt in zip(f(*args), out_dtypes))

    def row_spec(k, arr):
        width, index = (col_blocks or {}).get(k, (arr.shape[1], 0))
        return pl.BlockSpec((tm, width), lambda i: (i, index))

    def row_width(k, arr):
        return (col_blocks or {}).get(k, (arr.shape[1], 0))[0]

    def fwd_call(rows, vecs):
        t = min(r.shape[0] for r in rows)
        nr, nv = len(rows), len(vecs)
        nl = n_lat // tm

        def body(*refs):
            grp = (pl.program_id(0) >= nl).astype(jnp.int32)
            args = [r[...] for r in refs[:nr]] + [v[grp] for v in refs[nr:nr + nv]]
            for o_ref, o in zip(refs[nr + nv:], tile_fn(*args)):
                o_ref[...] = o

        return pl.pallas_call(
            body,
            name=name + "_fwd",
            grid=(t // tm,),
            in_specs=[row_spec(k, r) for k, r in enumerate(rows)]
            + [pl.BlockSpec(v.shape, lambda i: (0, 0, 0)) for v in vecs],
            out_specs=[pl.BlockSpec((tm, c), lambda i: (i, 0)) for c in out_cols],
            out_shape=[jax.ShapeDtypeStruct((t, c), dt) for c, dt in zip(out_cols, out_dtypes)],
            compiler_params=_params(("parallel",)),
        )(*rows, *vecs)

    def bwd_call(rows, vecs, gs, acc=None):
        t = min(r.shape[0] for r in rows)
        nr, nv, ng = len(rows), len(vecs), len(gs)
        nl = n_lat // tm
        nd = len(diff_rows)
        acc = [None] * nd if acc is None else list(acc)
        acc_in = [a for a in acc if a is not None]

        def body(*refs):
            i = pl.program_id(0)
            grp = (i >= nl).astype(jnp.int32)
            args = [r[...] for r in refs[:nr]] + [v[grp] for v in refs[nr:nr + nv]]
            g_refs = refs[nr + nv:nr + nv + ng]
            acc_refs = list(refs[nr + nv + ng:nr + nv + ng + len(acc_in)])
            drow_refs = refs[nr + nv + ng + len(acc_in):nr + nv + ng + len(acc_in) + nd]
            dvec_refs = refs[nr + nv + ng + len(acc_in) + nd:]
            _, vjp = jax.vjp(tile_fn, *args)
            grads = vjp(tuple(g[...] for g in g_refs))
            for d_ref, k, a in zip(drow_refs, diff_rows, acc):
                gk = grads[k] if a is None else grads[k] + acc_refs.pop(0)[...]
                d_ref[...] = gk.astype(d_ref.dtype)

            @pl.when(i == 0)
            def _():
                for d_ref in dvec_refs:
                    d_ref[...] = jnp.zeros_like(d_ref)

            for j, d_ref in enumerate(dvec_refs):
                d_ref[grp] += grads[nr + j]

        outs = pl.pallas_call(
            body,
            name=name + "_bwd",
            grid=(t // tm,),
            in_specs=[row_spec(k, r) for k, r in enumerate(rows)]
            + [pl.BlockSpec(v.shape, lambda i: (0, 0, 0)) for v in vecs]
            + [pl.BlockSpec((tm, g.shape[1]), lambda i: (i, 0)) for g in gs]
            + [pl.BlockSpec((tm, a.shape[1]), lambda i: (i, 0)) for a in acc_in],
            out_specs=[pl.BlockSpec((tm, row_width(k, rows[k])), lambda i: (i, 0)) for k in diff_rows]
            + [pl.BlockSpec(v.shape, lambda i: (0, 0, 0)) for v in vecs],
            out_shape=[jax.ShapeDtypeStruct((t, row_width(k, rows[k])), drow_dtypes.get(k, rows[k].dtype))
                       for k in diff_rows]
            + [jax.ShapeDtypeStruct(v.shape, F32) for v in vecs],
            compiler_params=_params(("arbitrary",)),
        )(*rows, *vecs, *gs, *acc_in)
        return outs[:nd], outs[nd:]

    return fwd_call, bwd_call


def _f_norm_mod(x, g, scale, shift):
    r = lax.rsqrt(jnp.mean(x * x, axis=-1, keepdims=True) + NORM_EPS)
    return ((x * r * g) * (1.0 + scale) + shift,)


def _f_gate_na(o, z):
    return (o.astype(F32) * _silu(z.astype(F32)),)


def _f_merge(g_na, g_ret, y_na, y_ret):
    return (jax.nn.sigmoid(g_na.astype(F32)) * y_na.astype(F32) + jax.nn.sigmoid(g_ret.astype(F32)) * y_ret.astype(F32),)


def _f_residual(x, out, gate):
    return (x + gate * out,)


def _f_loss(x, target, g):
    r = lax.rsqrt(jnp.mean(x * x, axis=-1, keepdims=True) + NORM_EPS)
    y = x * r * g
    e = 0.5 * jnp.mean(jnp.square(y - target), axis=-1, keepdims=True)
    return (jnp.broadcast_to(e * (1.0 / 128.0), (x.shape[0], 128)),)


def _gate_ret_fwd_call(of, ob, z, zblk, tm):
    t, w = of.shape
    nh = w // RET_VAL_DIM

    def body(of_ref, ob_ref, z_ref, a_ref):
        for hh in range(nh):
            sl = slice(hh * RET_VAL_DIM, (hh + 1) * RET_VAL_DIM)
            o = of_ref[:, sl].astype(F32) + ob_ref[:, sl].astype(F32)
            r = lax.rsqrt(jnp.mean(o * o, axis=-1, keepdims=True) + NORM_EPS)
            a_ref[:, sl] = ((o * r) * _silu(z_ref[:, sl].astype(F32))).astype(a_ref.dtype)

    spec = pl.BlockSpec((tm, w), lambda i: (i, 0))
    zspec = pl.BlockSpec((tm, w), lambda i: (i, zblk))
    return pl.pallas_call(
        body, name="gate_ret_fwd", grid=(t // tm,), in_specs=[spec, spec, zspec], out_specs=spec,
        out_shape=jax.ShapeDtypeStruct((t, w), BF16), compiler_params=_params(("parallel",)),
    )(of, ob, z)


def _gate_ret_bwd_call(of, ob, z, zblk, da, tm):
    t, w = of.shape
    nh = w // RET_VAL_DIM

    def body(of_ref, ob_ref, z_ref, da_ref, do_ref, dz_ref):
        for hh in range(nh):
            sl = slice(hh * RET_VAL_DIM, (hh + 1) * RET_VAL_DIM)
            o = of_ref[:, sl].astype(F32) + ob_ref[:, sl].astype(F32)
            r = lax.rsqrt(jnp.mean(o * o, axis=-1, keepdims=True) + NORM_EPS)
            n = o * r
            zf = z_ref[:, sl].astype(F32)
            sg = jax.nn.sigmoid(zf)
            g = da_ref[:, sl].astype(F32)
            dn = g * (zf * sg)
            dz_ref[:, sl] = (g * n * (sg * (1.0 + zf * (1.0 - sg)))).astype(dz_ref.dtype)
            do_ref[:, sl] = (r * (dn - n * jnp.mean(dn * n, axis=-1, keepdims=True))).astype(do_ref.dtype)

    spec = pl.BlockSpec((tm, w), lambda i: (i, 0))
    zspec = pl.BlockSpec((tm, w), lambda i: (i, zblk))
    return pl.pallas_call(
        body, name="gate_ret_bwd", grid=(t // tm,), in_specs=[spec, spec, zspec, spec], out_specs=[spec, spec],
        out_shape=[jax.ShapeDtypeStruct((t, w), BF16), jax.ShapeDtypeStruct((t, w), z.dtype)],
        compiler_params=_params(("parallel",)),
    )(of, ob, z, da)


NA_PAIR = 2 * GRID_W
NA_KEY_ROWS = NA_WIN_ROWS + 2
NA_CLASSES = 5


def _na_geometry(t, n_lat):
    rows = n_lat // GRID_W
    assert rows % 2 == 0 and rows >= NA_KEY_ROWS + 2, rows
    return rows, rows // 2, NA_KEY_ROWS * GRID_W, t - n_lat, t // NA_PAIR


def _na_base(p, rows):
    return jnp.clip(2 * p - NA_WIN_ROWS // 2, 0, rows - NA_KEY_ROWS)


def _na_class(p, rows):
    return p - _na_base(p, rows) // 2


def _na_group(pairs, n_ctx):
    assert n_ctx % NA_PAIR == 0, n_ctx
    return 2 if pairs % 2 == 0 and (n_ctx // NA_PAIR) % 2 == 0 else 1


def _na_bias_spec(i, grp, rows, pairs, n_loc):
    return pl.BlockSpec((1, 1, 2, NA_KEY_ROWS // 2, GRID_W, NA_PAIR),
                        lambda h, g: (h, _na_class(jnp.minimum(g * grp + i, pairs - grp + i), rows), 0, 0, 0, 0))


def _na_bias_tile(bt_ref):
    return jnp.concatenate(
        [jnp.concatenate([bt_ref[0, 0, i, q] for q in range(NA_KEY_ROWS // 2)], axis=1) for i in range(2)], axis=0)


def _na_store_bias_grad(dbt_ref, ds, accumulate):
    for i in range(2):
        for q in range(NA_KEY_ROWS // 2):
            tile = ds[i * GRID_W:(i + 1) * GRID_W, q * NA_PAIR:(q + 1) * NA_PAIR]
            if accumulate:
                dbt_ref[0, 0, i, q] += tile
            else:
                dbt_ref[0, 0, i, q] = tile


def _na_fwd_call(q, k, v, col0, w, bt, n_lat):
    t = q.shape[0]
    nh = w // NA_HEAD_DIM
    rows, pairs, n_loc, n_ctx, nq = _na_geometry(t, n_lat)
    grp = _na_group(pairs, n_ctx)
    scale = NA_HEAD_DIM ** -0.5

    def body(q_ref, k_ref, v_ref, *rest):
        bt_refs, o_ref = rest[:grp], rest[grp]
        g = pl.program_id(1)
        kc = k_ref[pl.ds(n_lat, n_ctx), :]
        vc = v_ref[pl.ds(n_lat, n_ctx), :]

        @pl.when(g < pairs // grp)
        def _():
            for i in range(grp):
                sl = slice(i * NA_PAIR, (i + 1) * NA_PAIR)
                qb = q_ref[sl, :]
                s_ctx = _dot_nt(qb, kc) * scale
                start = pl.multiple_of(_na_base(g * grp + i, rows) * GRID_W, GRID_W)
                kw = k_ref[pl.ds(start, n_loc), :]
                vw = v_ref[pl.ds(start, n_loc), :]
                s_loc = _dot_nt(qb, kw) * scale + _na_bias_tile(bt_refs[i])
                m = jnp.maximum(jnp.max(s_loc, axis=-1, keepdims=True), jnp.max(s_ctx, axis=-1, keepdims=True))
                p_loc = jnp.exp(s_loc - m)
                p_ctx = jnp.exp(s_ctx - m)
                l = jnp.sum(p_loc, axis=-1, keepdims=True) + jnp.sum(p_ctx, axis=-1, keepdims=True)
                o = _dot(p_loc.astype(BF16), vw) + _dot(p_ctx.astype(BF16), vc)
                o_ref[sl, :] = (o / l).astype(o_ref.dtype)

        @pl.when(g >= pairs // grp)
        def _():
            s_ctx = _dot_nt(q_ref[...], kc) * scale
            m = jnp.max(s_ctx, axis=-1, keepdims=True)
            p = jnp.exp(s_ctx - m)
            l = jnp.sum(p, axis=-1, keepdims=True)
            o_ref[...] = (_dot(p.astype(BF16), vc) / l).astype(o_ref.dtype)

    qspec = pl.BlockSpec((grp * NA_PAIR, NA_HEAD_DIM), lambda h, g: (g, h))
    in_q = pl.BlockSpec((grp * NA_PAIR, NA_HEAD_DIM), lambda h, g: (g, col0[0] + h))
    in_k = pl.BlockSpec((t, NA_HEAD_DIM), lambda h, g: (0, col0[1] + h))
    in_v = pl.BlockSpec((t, NA_HEAD_DIM), lambda h, g: (0, col0[2] + h))
    return pl.pallas_call(
        body,
        name="na_attn_fwd",
        grid=(nh, nq // grp),
        in_specs=[in_q, in_k, in_v] + [_na_bias_spec(i, grp, rows, pairs, n_loc) for i in range(grp)],
        out_specs=qspec,
        out_shape=jax.ShapeDtypeStruct((t, w), BF16),
        compiler_params=_params(("parallel", "arbitrary")),
    )(q, k, v, *([bt] * grp))


def _na_bwd_call(q, k, v, col0, w, bt, do, n_lat):
    t = q.shape[0]
    nh = w // NA_HEAD_DIM
    rows, pairs, n_loc, n_ctx, nq = _na_geometry(t, n_lat)
    scale = NA_HEAD_DIM ** -0.5
    grp = _na_group(pairs, n_ctx)

    def body(q_ref, k_ref, v_ref, do_ref, *rest):
        bt_refs = rest[:grp]
        dq_ref, dk_ref, dv_ref = rest[2 * grp:2 * grp + 3]
        dbt_refs = rest[2 * grp + 3:]
        g = pl.program_id(1)

        @pl.when(g == 0)
        def _():
            dk_ref[...] = jnp.zeros_like(dk_ref)
            dv_ref[...] = jnp.zeros_like(dv_ref)

        kc = k_ref[pl.ds(n_lat, n_ctx), :]
        vc = v_ref[pl.ds(n_lat, n_ctx), :]

        @pl.when(g < pairs // grp)
        def _():
            for i in range(grp):
                p = g * grp + i
                sl = slice(i * NA_PAIR, (i + 1) * NA_PAIR)
                qb = q_ref[sl, :]
                dob = do_ref[sl, :]
                s_ctx = _dot_nt(qb, kc) * scale
                dp_ctx = _dot_nt(dob, vc)
                start = pl.multiple_of(_na_base(p, rows) * GRID_W, GRID_W)
                kw = k_ref[pl.ds(start, n_loc), :]
                vw = v_ref[pl.ds(start, n_loc), :]
                s_loc = _dot_nt(qb, kw) * scale + _na_bias_tile(bt_refs[i])
                m = jnp.maximum(jnp.max(s_loc, axis=-1, keepdims=True), jnp.max(s_ctx, axis=-1, keepdims=True))
                p_loc = jnp.exp(s_loc - m)
                p_ctx = jnp.exp(s_ctx - m)
                inv = 1.0 / (jnp.sum(p_loc, axis=-1, keepdims=True) + jnp.sum(p_ctx, axis=-1, keepdims=True))
                p_loc = p_loc * inv
                p_ctx = p_ctx * inv
                dp_loc = _dot_nt(dob, vw)
                delta = (jnp.sum(p_loc * dp_loc, axis=-1, keepdims=True)
                         + jnp.sum(p_ctx * dp_ctx, axis=-1, keepdims=True))
                ds_loc = p_loc * (dp_loc - delta)
                ds_ctx = p_ctx * (dp_ctx - delta)
                first = jnp.logical_or(g == 0, _na_class(p, rows) != _na_class(p - grp, rows))
                dbt_ref = dbt_refs[i]

                @pl.when(first)
                def _():
                    _na_store_bias_grad(dbt_ref, ds_loc, False)

                @pl.when(jnp.logical_not(first))
                def _():
                    _na_store_bias_grad(dbt_ref, ds_loc, True)

                dsl = (ds_loc * scale).astype(BF16)
                dsc = (ds_ctx * scale).astype(BF16)
                dq_ref[sl, :] = (_dot(dsl, kw) + _dot(dsc, kc)).astype(dq_ref.dtype)
                dk_ref[pl.ds(start, n_loc), :] += _dot_tn(dsl, qb)
                dv_ref[pl.ds(start, n_loc), :] += _dot_tn(p_loc.astype(BF16), dob)
                dk_ref[pl.ds(n_lat, n_ctx), :] += _dot_tn(dsc, qb)
                dv_ref[pl.ds(n_lat, n_ctx), :] += _dot_tn(p_ctx.astype(BF16), dob)

        @pl.when(g >= pairs // grp)
        def _():
            qb = q_ref[...]
            dob = do_ref[...]
            s_ctx = _dot_nt(qb, kc) * scale
            dp_ctx = _dot_nt(dob, vc)
            m = jnp.max(s_ctx, axis=-1, keepdims=True)
            p = jnp.exp(s_ctx - m)
            p = p * (1.0 / jnp.sum(p, axis=-1, keepdims=True))
            delta = jnp.sum(p * dp_ctx, axis=-1, keepdims=True)
            dsc = (p * (dp_ctx - delta) * scale).astype(BF16)
            dq_ref[...] = _dot(dsc, kc).astype(dq_ref.dtype)
            dk_ref[pl.ds(n_lat, n_ctx), :] += _dot_tn(dsc, qb)
            dv_ref[pl.ds(n_lat, n_ctx), :] += _dot_tn(p.astype(BF16), dob)

    qspec = pl.BlockSpec((grp * NA_PAIR, NA_HEAD_DIM), lambda h, g: (g, h))
    kspec = pl.BlockSpec((t, NA_HEAD_DIM), lambda h, g: (0, h))
    bspecs = [_na_bias_spec(i, grp, rows, pairs, n_loc) for i in range(grp)]
    zeros = [jnp.zeros(bt.shape, F32) for _ in range(grp)]
    outs = pl.pallas_call(
        body,
        name="na_attn_bwd",
        grid=(nh, nq // grp),
        in_specs=[pl.BlockSpec((grp * NA_PAIR, NA_HEAD_DIM), lambda h, g: (g, col0[0] + h)),
                  pl.BlockSpec((t, NA_HEAD_DIM), lambda h, g: (0, col0[1] + h)),
                  pl.BlockSpec((t, NA_HEAD_DIM), lambda h, g: (0, col0[2] + h)), qspec] + bspecs + [ANY] * grp,
        out_specs=[qspec, kspec, kspec] + bspecs,
        out_shape=[
            jax.ShapeDtypeStruct((t, w), BF16),
            jax.ShapeDtypeStruct((t, w), F32),
            jax.ShapeDtypeStruct((t, w), F32),
        ] + [jax.ShapeDtypeStruct(bt.shape, F32)] * grp,
        input_output_aliases={4 + grp + i: 3 + i for i in range(grp)},
        compiler_params=_params(("parallel", "arbitrary")),
    )(q, k, v, do, *([bt] * grp), *zeros)
    dbt = outs[3]
    for extra in outs[4:]:
        dbt = dbt + extra
    return outs[0], outs[1], outs[2], dbt


def _na_bias_table(rpb, rows):
    pairs = rows // 2
    nb = 2 * NA_WIN_COLS - 1
    nq = NA_KEY_ROWS // 2
    e1 = np.zeros((NA_CLASSES, 2, nq, 2, 2 * NA_WIN_ROWS - 1), np.float32)
    valid = np.zeros((NA_CLASSES, 2, nq, 2), bool)
    for cls, p in enumerate((0, 1, 2, pairs - 2, pairs - 1)):
        base = int(np.clip(2 * p - NA_WIN_ROWS // 2, 0, rows - NA_KEY_ROWS))
        assert p - base // 2 == cls, (rows, cls, p, base)
        for i in range(2):
            r = 2 * p + i
            r0 = int(np.clip(r - NA_WIN_ROWS // 2, 0, rows - NA_WIN_ROWS))
            for kk in range(NA_KEY_ROWS):
                if r0 <= base + kk < r0 + NA_WIN_ROWS:
                    valid[cls, i, kk // 2, kk % 2] = True
                    e1[cls, i, kk // 2, kk % 2, base + kk - r + NA_WIN_ROWS - 1] = 1.0
    cidx = np.arange(GRID_W)
    dc = np.clip(cidx[None, :] - cidx[:, None] + (NA_WIN_COLS - 1), 0, nb - 1)
    c0 = np.clip(cidx - NA_WIN_COLS // 2, 0, GRID_W - NA_WIN_COLS)
    col_in = (cidx[None, :] >= c0[:, None]) & (cidx[None, :] < c0[:, None] + NA_WIN_COLS)
    e2 = np.zeros((GRID_W, 2, GRID_W, 2, nb), np.float32)
    for par in range(2):
        e2[np.arange(GRID_W)[:, None], par, np.arange(GRID_W)[None, :], par, dc] = 1.0
    mask = valid[:, :, :, None, :, None] & col_in[None, None, None, :, None, :]
    t1 = jnp.einsum("hab,xiqpa->hxiqpb", rpb, jnp.asarray(e1), precision=lax.Precision.HIGHEST)
    t1 = t1.reshape(t1.shape[:4] + (2 * nb,))
    b = jnp.einsum("hxiqm,cwm->hxiqcw", t1, jnp.asarray(e2.reshape(GRID_W, 2 * GRID_W, 2 * nb)),
                   precision=lax.Precision.HIGHEST)
    return jnp.where(jnp.asarray(mask.reshape(NA_CLASSES, 2, nq, GRID_W, 2 * GRID_W))[None], b, MASK_VALUE)


def _ret_decays(lam_s, reverse):
    c = RET_CHUNK
    ii = lax.broadcasted_iota(jnp.int32, (c, c), 0)
    jj = lax.broadcasted_iota(jnp.int32, (c, c), 1)
    d = (jj - ii) if reverse else (ii - jj)
    dpos = jnp.maximum(d.astype(F32), 0.0)
    mask = jnp.where(d >= 0, jnp.exp(dpos * lam_s), 0.0)
    pi = lax.broadcasted_iota(jnp.int32, (c, 1), 0).astype(F32)
    qpos = (c - pi) if reverse else (pi + 1.0)
    kpos = pi if reverse else (c - 1.0 - pi)
    qd = jnp.exp(qpos * lam_s)
    kd = jnp.exp(kpos * lam_s)
    g = jnp.exp(jnp.full((1, RET_VAL_DIM), c * lam_s, F32))
    return mask, dpos, qd, kd, qpos, kpos, g


def _ret_head_group(nh):
    return _pick(nh, (8, 4, 2))


def _ret_chunk_of(t, nt, nl, reverse):
    return (nt - 1 - t) if reverse else (t + nl) % nt


def _ret_fwd_call(qr, kr, v, vcol, lam, n_lat, reverse):
    t = qr.shape[0]
    nh = qr.shape[1] // RET_KEY_DIM
    c = RET_CHUNK
    nt, nl = t // c, n_lat // c

    hg = _ret_head_group(nh)
    dk, dv = RET_KEY_DIM, RET_VAL_DIM

    def body(lam_ref, q_ref, k_ref, v_ref, o_ref, s_ref, state):
        hb, step = pl.program_id(0), pl.program_id(1)

        @pl.when(step == 0)
        def _():
            state[...] = jnp.zeros_like(state)

        for j in range(hg):
            mask, _, qd, kd, _, _, g = _ret_decays(lam_ref[hb * hg + j], reverse)
            q, k, vv = q_ref[:, j * dk:(j + 1) * dk], k_ref[:, j * dk:(j + 1) * dk], v_ref[:, j * dv:(j + 1) * dv]
            p = _dot_nt(q, k) * mask
            s = state[j]
            qs = (q.astype(F32) * qd).astype(BF16)
            o_ref[:, j * dv:(j + 1) * dv] = (_dot(p.astype(BF16), vv) + _dot(qs, s.astype(BF16))).astype(o_ref.dtype)
            s_ref[j, 0] = s
            ks = (k.astype(F32) * kd).astype(BF16)
            state[j] = s * g + _dot_tn(ks, vv)

    def cmap(hb, step, lam_ref):
        return (_ret_chunk_of(step, nt, nl, reverse), hb)

    def vmap(hb, step, lam_ref):
        return (_ret_chunk_of(step, nt, nl, reverse), vcol // (hg * dv) + hb)

    return pl.pallas_call(
        body,
        name="retention_rev_fwd" if reverse else "retention_fwd",
        grid_spec=pltpu.PrefetchScalarGridSpec(
            num_scalar_prefetch=1,
            grid=(nh // hg, nt),
            in_specs=[
                pl.BlockSpec((c, hg * dk), cmap),
                pl.BlockSpec((c, hg * dk), cmap),
                pl.BlockSpec((c, hg * dv), vmap),
            ],
            out_specs=[
                pl.BlockSpec((c, hg * dv), cmap),
                pl.BlockSpec((hg, 1, dk, dv), lambda hb, step, lam_ref: (hb, step, 0, 0)),
            ],
            scratch_shapes=[pltpu.VMEM((hg, dk, dv), F32)],
        ),
        out_shape=[
            jax.ShapeDtypeStruct((t, nh * RET_VAL_DIM), BF16),
            jax.ShapeDtypeStruct((nh, nt, RET_KEY_DIM, RET_VAL_DIM), F32),
        ],
        compiler_params=_params(("parallel", "arbitrary")),
    )(lam, qr, kr, v)


def _ret_bwd_call(qr, kr, v, vcol, lam, states, do, n_lat, reverse):
    assert RET_CHUNK == RET_KEY_DIM and RET_VAL_DIM % RET_KEY_DIM == 0
    t = qr.shape[0]
    nh = qr.shape[1] // RET_KEY_DIM
    c = RET_CHUNK
    nt, nl = t // c, n_lat // c

    hg = _ret_head_group(nh)
    dk, dv = RET_KEY_DIM, RET_VAL_DIM

    def body(lam_ref, q_ref, k_ref, v_ref, s_ref, do_ref, dq_ref, dk_ref, dv_ref, dl_ref, dstate):
        hb, rstep = pl.program_id(0), pl.program_id(1)

        @pl.when(rstep == 0)
        def _():
            dstate[...] = jnp.zeros_like(dstate)
            dl_ref[...] = jnp.zeros_like(dl_ref)

        for j in range(hg):
            mask, dpos, qd, kd, qpos, kpos, g = _ret_decays(lam_ref[hb * hg + j], reverse)
            ksl, vsl = slice(j * dk, (j + 1) * dk), slice(j * dv, (j + 1) * dv)
            q, k, vv = q_ref[:, ksl], k_ref[:, ksl], v_ref[:, vsl]
            qf, kf = q.astype(F32), k.astype(F32)
            s = s_ref[j, 0]
            ds = dstate[j]
            dob = do_ref[:, vsl].astype(BF16)
            sb, dsb = s.astype(BF16), ds.astype(BF16)
            a = _dot_nt(q, k)
            p = a * mask
            dp = _dot_nt(dob, vv)
            da = dp * mask
            dab = da.astype(BF16)
            dqc = _dot_nt(dob, sb)
            dkc = _dot_nt(vv, dsb)
            qs = (qf * qd).astype(BF16)
            ks = (kf * kd).astype(BF16)
            dq_ref[:, ksl] = (_dot(dab, k) + dqc * qd).astype(dq_ref.dtype)
            dk_ref[:, ksl] = (_dot_tn(dab, q) + dkc * kd).astype(dk_ref.dtype)
            dv_ref[:, vsl] = (_dot_tn(p.astype(BF16), dob) + _dot(ks, dsb)).astype(dv_ref.dtype)
            dsg = ds * s * (g * c)
            terms = da * a * dpos + dqc * qf * (qd * qpos) + dkc * kf * (kd * kpos)
            for half in range(dv // dk):
                terms = terms + dsg[:, half * dk:(half + 1) * dk]
            total = jnp.sum(jnp.sum(terms, axis=0, keepdims=True), axis=1, keepdims=True)
            dl_ref[j] += jnp.broadcast_to(total, (8, 128))
            dstate[j] = ds * g + _dot_tn(qs, dob)

    def cmap(hb, rstep, lam_ref):
        return (_ret_chunk_of(nt - 1 - rstep, nt, nl, reverse), hb)

    def vmap(hb, rstep, lam_ref):
        return (_ret_chunk_of(nt - 1 - rstep, nt, nl, reverse), vcol // (hg * dv) + hb)

    return pl.pallas_call(
        body,
        name="retention_rev_bwd" if reverse else "retention_bwd",
        grid_spec=pltpu.PrefetchScalarGridSpec(
            num_scalar_prefetch=1,
            grid=(nh // hg, nt),
            in_specs=[
                pl.BlockSpec((c, hg * dk), cmap),
                pl.BlockSpec((c, hg * dk), cmap),
                pl.BlockSpec((c, hg * dv), vmap),
                pl.BlockSpec((hg, 1, dk, dv), lambda hb, rstep, lam_ref: (hb, nt - 1 - rstep, 0, 0)),
                pl.BlockSpec((c, hg * dv), cmap),
            ],
            out_specs=[
                pl.BlockSpec((c, hg * dk), cmap),
                pl.BlockSpec((c, hg * dk), cmap),
                pl.BlockSpec((c, hg * dv), cmap),
                pl.BlockSpec((hg, 8, 128), lambda hb, rstep, lam_ref: (hb, 0, 0)),
            ],
            scratch_shapes=[pltpu.VMEM((hg, dk, dv), F32)],
        ),
        out_shape=[
            jax.ShapeDtypeStruct(qr.shape, qr.dtype),
            jax.ShapeDtypeStruct(kr.shape, kr.dtype),
            jax.ShapeDtypeStruct((t, nh * dv), v.dtype),
            jax.ShapeDtypeStruct((nh, 8, 128), F32),
        ],
        compiler_params=_params(("parallel", "arbitrary")),
    )(lam, qr, kr, v, states, do)


def _rope_tables(t, n_lat):
    nf = RET_KEY_DIM // 4
    tok = np.arange(n_lat)
    inv_freq = (ROPE_BASE ** (-np.arange(nf, dtype=np.float32) / nf)).astype(np.float32)
    row = (tok // GRID_W).astype(np.float32)
    col = (tok % GRID_W).astype(np.float32)
    ang = np.concatenate([row[:, None] * inv_freq, col[:, None] * inv_freq], axis=-1).astype(np.float32)
    cos = np.ones((t, 2 * nf), np.float32)
    sin = np.zeros((t, 2 * nf), np.float32)
    cos[:n_lat] = np.cos(ang)
    sin[:n_lat] = np.sin(ang)
    return jnp.asarray(np.concatenate([cos, cos], axis=1)), jnp.asarray(np.concatenate([-sin, sin], axis=1))


def _rotate(x, cos2, sin2):
    return x * cos2 + pltpu.roll(x, RET_KEY_DIM // 2, 1) * sin2


def _rope_fwd_call(u, qblk, kblk, w_qk, cos2, sin2, k_scale, tm):
    t = u.shape[0]
    nh = w_qk // RET_KEY_DIM

    def body(q_ref, k_ref, c_ref, s_ref, qr_ref, kr_ref):
        c, s = c_ref[...], s_ref[...]
        for hh in range(nh):
            sl = slice(hh * RET_KEY_DIM, (hh + 1) * RET_KEY_DIM)
            qr_ref[:, sl] = _rotate(q_ref[:, sl].astype(F32), c, s).astype(qr_ref.dtype)
            kr_ref[:, sl] = (_rotate(k_ref[:, sl].astype(F32), c, s) * k_scale).astype(kr_ref.dtype)

    tab = pl.BlockSpec((tm, RET_KEY_DIM), lambda i: (i, 0))
    out = pl.BlockSpec((tm, w_qk), lambda i: (i, 0))
    return pl.pallas_call(
        body, name="rope_fwd", grid=(t // tm,),
        in_specs=[pl.BlockSpec((tm, w_qk), lambda i: (i, qblk)), pl.BlockSpec((tm, w_qk), lambda i: (i, kblk)), tab, tab],
        out_specs=[out, out], out_shape=[jax.ShapeDtypeStruct((t, w_qk), BF16)] * 2,
        compiler_params=_params(("parallel",)),
    )(u, u, cos2, sin2)


def _assemble_du_call(pieces, off, cos2, sin2, k_scale, tm):
    flat = [a for p in pieces for a in (p if isinstance(p, tuple) else (p,))]
    t = flat[0].shape[0]
    n = len(flat)

    def body(*refs):
        c, s = refs[n][...], -refs[n + 1][...]
        o_ref = refs[n + 2]
        it = iter(refs[:n])
        for blk, p in enumerate(pieces):
            lo = off[blk]
            if not isinstance(p, tuple):
                o_ref[:, lo:off[blk + 1]] = next(it)[...].astype(o_ref.dtype)
                continue
            fwd_ref, rev_ref = next(it), next(it)
            if blk == 6:
                o_ref[:, lo:off[blk + 1]] = (fwd_ref[...].astype(F32) + rev_ref[...].astype(F32)).astype(o_ref.dtype)
                continue
            mult = k_scale if blk == 5 else 1.0
            for hh in range((off[blk + 1] - lo) // RET_KEY_DIM):
                sl = slice(hh * RET_KEY_DIM, (hh + 1) * RET_KEY_DIM)
                dy = (fwd_ref[:, sl].astype(F32) + rev_ref[:, sl].astype(F32)) * mult
                o_ref[:, lo + hh * RET_KEY_DIM:lo + (hh + 1) * RET_KEY_DIM] = _rotate(dy, c, s).astype(o_ref.dtype)

    tab = pl.BlockSpec((tm, RET_KEY_DIM), lambda i: (i, 0))
    return pl.pallas_call(
        body, name="assemble_du", grid=(t // tm,),
        in_specs=[pl.BlockSpec((tm, a.shape[1]), lambda i: (i, 0)) for a in flat] + [tab, tab],
        out_specs=pl.BlockSpec((tm, off[-1]), lambda i: (i, 0)),
        out_shape=jax.ShapeDtypeStruct((t, off[-1]), BF16),
        compiler_params=_params(("parallel",)),
    )(*flat, cos2, sin2)


def _my_position():
    return lax.axis_index("x"), lax.axis_index("y"), lax.axis_index("c")


def _flip(pos, k):
    x, y, c = pos
    return (1 - x if k & 4 else x, 1 - y if k & 2 else y, 1 - c if k & 1 else c)


def _linear(pos):
    return 4 * pos[0] + 2 * pos[1] + pos[2]


def _slab(ref, axis, idx, size):
    start = pl.multiple_of(idx * size, size)
    return ref.at[pl.ds(start, size), :] if axis == 0 else ref.at[:, pl.ds(start, size)]


HBM_SPEC = pl.BlockSpec(memory_space=pltpu.HBM)
SEM_SPEC = pl.BlockSpec(memory_space=pltpu.SEMAPHORE)
DATAFLOW = pltpu.SideEffectType.DATAFLOW_SIDE_EFFECTING
PEER_BITS = (1, 2, 4, 6, 3, 5, 7)
GATHER_BITS = (1, 2, 4, 6)


def _in_hbm(a):
    return pltpu.with_memory_space_constraint(a, pltpu.HBM)


def _gather_views(me, k, a, src_refs, land_refs, axes):
    size = src_refs[a].shape[axes[a]]
    peer = _flip(me, k)
    return src_refs[a], _slab(land_refs[a], axes[a], _linear(me), size), _slab(land_refs[a], axes[a], _linear(peer), size)


def _scatter_views(me, k, a, src_refs, land_refs, axes):
    size = land_refs[a].shape[1 + axes[a]]
    peer = _flip(me, k)
    return _slab(src_refs[a], axes[a], _linear(peer), size), land_refs[a].at[k - 1], land_refs[a].at[k - 1]


CHIP_BITS = (0, 2, 4, 6)


def _chip_views(me, k, a, src_refs, land_refs, axes):
    j = CHIP_BITS.index(k)
    return src_refs[a].at[j], land_refs[a].at[j - 1], land_refs[a].at[j - 1]


def _pair_exchange(grads, axes, sizes):
    ns = len(grads)

    def slab_shape(a):
        s = grads[a].shape
        return (sizes[a], s[1]) if axes[a] == 0 else (s[0], sizes[a])

    def body(*refs):
        g_refs, p_refs = refs[:ns], refs[ns:2 * ns]
        send_sems, recv_sems = refs[2 * ns:]
        me = _my_position()
        sibling = _flip(me, 1)
        copies = []
        for j, kc in enumerate(CHIP_BITS):
            for a in range(ns):
                cp = pltpu.make_async_remote_copy(
                    src_ref=_slab(g_refs[a], axes[a], _linear(_flip(me, kc | 1)), sizes[a]), dst_ref=p_refs[a].at[j],
                    send_sem=send_sems.at[4 * a + j], recv_sem=recv_sems.at[4 * a + j],
                    device_id=sibling, device_id_type=MESH)
                cp.start()
                copies.append(cp)
        for cp in copies:
            cp.wait_recv()
        for cp in copies:
            cp.wait_send()

    return pl.pallas_call(
        body, name="scatter_pair_exchange", in_specs=[ANY] * ns, out_specs=[ANY] * ns,
        out_shape=[jax.ShapeDtypeStruct((4,) + slab_shape(a), grads[a].dtype) for a in range(ns)],
        scratch_shapes=[pltpu.SemaphoreType.DMA((4 * ns,)), pltpu.SemaphoreType.DMA((4 * ns,))],
        compiler_params=pltpu.CompilerParams(has_side_effects=True),
    )(*grads)


def _pair_add(grad, theirs, axis, chip_idx):
    _, r, c = theirs.shape
    tm = _pick(r, (256, 128, 64, 32, 16))
    if axis == 0:
        mine_spec = pl.BlockSpec((tm, c), lambda j, i, idx: (idx[j] * (r // tm) + i, 0))
    else:
        mine_spec = pl.BlockSpec((tm, c), lambda j, i, idx: (i, idx[j]))

    def body(idx_ref, mine_ref, theirs_ref, o_ref):
        o_ref[0] = (mine_ref[...].astype(F32) + theirs_ref[0].astype(F32)).astype(o_ref.dtype)

    spec = pl.BlockSpec((1, tm, c), lambda j, i, idx: (j, i, 0))
    return pl.pallas_call(
        body, name="scatter_pair_add",
        grid_spec=pltpu.PrefetchScalarGridSpec(
            num_scalar_prefetch=1, grid=(4, r // tm), in_specs=[mine_spec, spec], out_specs=spec),
        out_shape=jax.ShapeDtypeStruct(theirs.shape, theirs.dtype),
        compiler_params=_params(("parallel", "parallel")),
    )(chip_idx, grad, theirs)


def _slab_block(rows, cols, tm, axis):
    if axis == 0:
        return pl.BlockSpec((tm, cols), lambda i, idx: (idx[0] * (rows // tm) + i, 0))
    return pl.BlockSpec((tm, cols), lambda i, idx: (i, idx[0]))


def _place_shard(shard, land, axis, my_idx):
    r, c = shard.shape
    tm = _pick(r, (512, 256, 128, 64, 32, 16))

    def body(idx_ref, s_ref, land_ref, o_ref):
        o_ref[...] = s_ref[...]

    return pl.pallas_call(
        body, name="gather_place",
        grid_spec=pltpu.PrefetchScalarGridSpec(
            num_scalar_prefetch=1, grid=(r // tm,),
            in_specs=[pl.BlockSpec((tm, c), lambda i, idx: (i, 0)), ANY],
            out_specs=_slab_block(r, c, tm, axis)),
        out_shape=jax.ShapeDtypeStruct(land.shape, land.dtype),
        input_output_aliases={2: 0},
        compiler_params=_params(("parallel",)),
    )(my_idx, shard, land)


def _push_start(name, srcs, lands, axes, views, bits, deps):
    ns = len(srcs)

    def body(*refs):
        src_refs, land_refs = refs[:ns], refs[ns:2 * ns]
        send_sems, recv_sems = refs[2 * ns + len(deps):2 * ns + len(deps) + 2]
        token = refs[-1]
        me = _my_position()
        for k in bits:
            for a in range(ns):
                s, d, _ = views(me, k, a, src_refs, land_refs, axes)
                pltpu.make_async_remote_copy(
                    src_ref=s, dst_ref=d, send_sem=send_sems.at[7 * a + k - 1], recv_sem=recv_sems.at[7 * a + k - 1],
                    device_id=_flip(me, k), device_id_type=MESH).start()
        token[...] = jnp.zeros_like(token)

    thru = [pltpu.HBM(a.shape, a.dtype) for a in list(srcs) + list(lands)]
    outs = pl.pallas_call(
        body, name=name,
        in_specs=[HBM_SPEC] * (2 * ns) + [ANY] * len(deps),
        out_specs=[SEM_SPEC, SEM_SPEC] + [HBM_SPEC] * (2 * ns) + [VMEM_SPEC],
        out_shape=[pltpu.SemaphoreType.DMA((7 * ns,)), pltpu.SemaphoreType.DMA((7 * ns,))] + thru
        + [jax.ShapeDtypeStruct((8, 128), F32)],
        input_output_aliases={i: 2 + i for i in range(2 * ns)},
        compiler_params=pltpu.CompilerParams(has_side_effects=DATAFLOW),
    )(*[_in_hbm(a) for a in srcs], *[_in_hbm(a) for a in lands], *deps)
    return (outs[0], outs[1]), outs[2:2 + ns], outs[2 + ns:2 + 2 * ns], outs[-1]


def _gather_finish(lands, axes, sizes):
    ns = len(lands)
    chips = (2, 4, 6)

    def body(*refs):
        land_refs = refs[ns:2 * ns]
        send_sems, recv_sems = refs[2 * ns:]
        me = _my_position()
        sibling = _flip(me, 1)
        copies = []
        for j, kc in enumerate(chips):
            for a in range(ns):
                def slab_of(pos):
                    return _slab(land_refs[a], axes[a], _linear(pos), sizes[a])
                send = pltpu.make_async_remote_copy(
                    src_ref=slab_of(_flip(me, kc)), dst_ref=slab_of(_flip(me, kc)), send_sem=send_sems.at[3 * a + j],
                    recv_sem=recv_sems.at[3 * a + j], device_id=sibling, device_id_type=MESH)
                recv = pltpu.make_async_remote_copy(
                    src_ref=slab_of(_flip(me, kc)), dst_ref=slab_of(_flip(sibling, kc)), send_sem=send_sems.at[3 * a + j],
                    recv_sem=recv_sems.at[3 * a + j], device_id=sibling, device_id_type=MESH)
                send.start()
                copies.append((send, recv))
        for send, recv in copies:
            recv.wait_recv()
        for send, recv in copies:
            send.wait_send()

    return pl.pallas_call(
        body, name="gather_finish", in_specs=[ANY] * ns, out_specs=[ANY] * ns,
        out_shape=[jax.ShapeDtypeStruct(l.shape, l.dtype) for l in lands],
        input_output_aliases={a: a for a in range(ns)},
        scratch_shapes=[pltpu.SemaphoreType.DMA((3 * ns,)), pltpu.SemaphoreType.DMA((3 * ns,))],
        compiler_params=pltpu.CompilerParams(has_side_effects=True),
    )(*lands)


def _push_wait(name, sems, srcs, lands, axes, views, bits, after):
    ns = len(srcs)

    def body(*refs):
        src_refs, land_refs = refs[:ns], refs[ns:2 * ns]
        send_sems, recv_sems = refs[2 * ns:2 * ns + 2]
        me = _my_position()
        for k in bits:
            for a in range(ns):
                s, d, got = views(me, k, a, src_refs, land_refs, axes)
                cp = pltpu.make_async_remote_copy(
                    src_ref=s, dst_ref=got, send_sem=send_sems.at[7 * a + k - 1], recv_sem=recv_sems.at[7 * a + k - 1],
                    device_id=_flip(me, k), device_id_type=MESH)
                cp.wait_send()
                cp.wait_recv()

    thru = [pltpu.HBM(a.shape, a.dtype) for a in list(srcs) + list(lands)]
    outs = pl.pallas_call(
        body, name=name,
        in_specs=[HBM_SPEC] * (2 * ns) + [SEM_SPEC, SEM_SPEC] + [ANY] * len(after),
        out_specs=[HBM_SPEC] * (2 * ns),
        out_shape=thru,
        input_output_aliases={i: i for i in range(2 * ns)},
        compiler_params=pltpu.CompilerParams(has_side_effects=DATAFLOW),
    )(*srcs, *lands, sems[0], sems[1], *after)
    return outs[:ns], outs[ns:]


def _small_allgather(v, name):
    r, c = v.shape

    def body(v_ref, all_ref, sum_ref, send_sems, recv_sems):
        me = _my_position()
        all_ref[_linear(me)] = v_ref[...]
        copies = []
        for k in range(1, N_DEV):
            peer = _flip(me, k)
            copies.append(pltpu.make_async_remote_copy(
                src_ref=v_ref, dst_ref=all_ref.at[_linear(me)], send_sem=send_sems.at[k - 1], recv_sem=recv_sems.at[k - 1],
                device_id=peer, device_id_type=MESH))
        for cp in copies:
            cp.start()
        for k in range(1, N_DEV):
            peer = _flip(me, k)
            pltpu.make_async_remote_copy(
                src_ref=v_ref, dst_ref=all_ref.at[_linear(peer)], send_sem=send_sems.at[k - 1], recv_sem=recv_sems.at[k - 1],
                device_id=peer, device_id_type=MESH).wait_recv()
        for cp in copies:
            cp.wait_send()
        acc = all_ref[0]
        for d in range(1, N_DEV):
            acc = acc + all_ref[d]
        sum_ref[...] = acc

    return pl.pallas_call(
        body,
        name=name,
        in_specs=[VMEM_SPEC],
        out_specs=[VMEM_SPEC, VMEM_SPEC],
        out_shape=[jax.ShapeDtypeStruct((N_DEV, r, c), F32), jax.ShapeDtypeStruct((r, c), F32)],
        scratch_shapes=[pltpu.SemaphoreType.DMA((N_DEV - 1,)), pltpu.SemaphoreType.DMA((N_DEV - 1,))],
        compiler_params=pltpu.CompilerParams(has_side_effects=True, vmem_limit_bytes=VMEM_LIMIT),
    )(v)


def _ada_fwd_call(cin, ada_w, ada_b_cols):
    nl, d, ncol = ada_w.shape
    nrow = cin.shape[0]

    def body(c_ref, w_ref, b_ref, o_ref):
        cs = _silu(c_ref[...]).astype(BF16)
        for l in range(nl):
            o_ref[l] = _dot(cs, w_ref[l].astype(BF16)) + b_ref[l]

    return pl.pallas_call(
        body, name="ada_fwd", in_specs=[VMEM_SPEC] * 3, out_specs=VMEM_SPEC,
        out_shape=jax.ShapeDtypeStruct((nl, nrow, ncol), F32), compiler_params=_params(),
    )(cin, ada_w, ada_b_cols)


def _ada_bwd_call(cin, ada_w, dmod):
    nl, d, ncol = ada_w.shape
    nrow = cin.shape[0]

    def body(c_ref, w_ref, dm_ref, gw_ref, dcs_ref):
        cs = _silu(c_ref[...]).astype(BF16)
        acc = jnp.zeros((nrow, d), F32)
        for l in range(nl):
            dm = dm_ref[l].astype(BF16)
            gw_ref[l] = _dot_tn(cs, dm)
            acc = acc + _dot_nt(dm, w_ref[l].astype(BF16))
        dcs_ref[...] = acc

    return pl.pallas_call(
        body, name="ada_bwd", in_specs=[VMEM_SPEC] * 3, out_specs=[VMEM_SPEC, VMEM_SPEC],
        out_shape=[jax.ShapeDtypeStruct((nl, d, ncol), F32), jax.ShapeDtypeStruct((nrow, d), F32)],
        compiler_params=_params(),
    )(cin, ada_w, dmod)


def _adamw_math(w, g, m, v):
    m = ADAM_B1 * m + (1.0 - ADAM_B1) * g
    v = ADAM_B2 * v + (1.0 - ADAM_B2) * jnp.square(g)
    m_hat = m / (1.0 - ADAM_B1 ** ADAM_STEP)
    v_hat = v / (1.0 - ADAM_B2 ** ADAM_STEP)
    delta = -ADAM_LR * (m_hat / (jnp.sqrt(v_hat) + ADAM_EPS) + ADAM_WD * w)
    return delta, m, v


def _adamw_sharded(w, m, v, mine, slabs, axis, my_idx, layer, prev, name):
    nl, r, c = w.shape
    tm = _pick(r, (128, 64, 32, 16))
    nprev = 0 if prev is None else len(prev)
    nslab = slabs.shape[0]
    if axis is None:
        mine_spec = pl.BlockSpec((1, tm, c), lambda i, idx: (0, i, 0))
    else:
        mine_spec = _slab_block(r, c, tm, axis)

    def body(idx_ref, w_ref, m_ref, v_ref, mine_ref, s_ref, *rest):
        g_ref, d_ref, nm_ref, nv_ref = rest[nprev:]
        g = (mine_ref[0] if axis is None else mine_ref[...]).astype(F32)
        for k in range(nslab):
            g = g + s_ref[k].astype(F32)
        delta, nm, nv = _adamw_math(w_ref[0], g, m_ref[0], v_ref[0])
        g_ref[0], d_ref[0], nm_ref[0], nv_ref[0] = g, delta, nm, nv

    spec = pl.BlockSpec((1, tm, c), lambda i, idx: (layer, i, 0))
    out = jax.ShapeDtypeStruct(w.shape, F32)
    return pl.pallas_call(
        body, name=name,
        grid_spec=pltpu.PrefetchScalarGridSpec(
            num_scalar_prefetch=1, grid=(r // tm,),
            in_specs=[spec, spec, spec, mine_spec,
                      pl.BlockSpec((nslab, tm, c), lambda i, idx: (0, i, 0))] + [ANY] * nprev,
            out_specs=[spec] * 4),
        out_shape=[out] * 4,
        input_output_aliases={6 + j: j for j in range(nprev)},
        compiler_params=_params(("parallel",)),
    )(my_idx, w, m, v, mine, slabs, *(() if prev is None else prev))


def _adamw_dense(w, g, m, v, name):
    r, c = w.shape
    tm = _pick(r, (256, 128, 64, 32, 16, 8))

    def body(w_ref, g_ref, m_ref, v_ref, d_ref, nm_ref, nv_ref):
        d_ref[...], nm_ref[...], nv_ref[...] = _adamw_math(w_ref[...], g_ref[...], m_ref[...], v_ref[...])

    spec = pl.BlockSpec((tm, c), lambda i: (i, 0))
    out = jax.ShapeDtypeStruct(w.shape, F32)
    return pl.pallas_call(
        body, name=name, grid=(r // tm,), in_specs=[spec] * 4, out_specs=[spec] * 3, out_shape=[out] * 3,
        compiler_params=_params(("parallel",)),
    )(w, g, m, v)


def _pack(parts, width=128):
    flat = jnp.concatenate([p.reshape(-1).astype(F32) for p in parts])
    n = flat.shape[0]
    total = -(-n // (8 * width)) * (8 * width)
    return jnp.pad(flat, (0, total - n)).reshape(total // width, width)


def _unpack(buf, shapes):
    flat = buf.reshape(-1)
    out, off = [], 0
    for s in shapes:
        n = int(np.prod(s))
        out.append(flat[off:off + n].reshape(s))
        off += n
    return out


def kernel(x, c, ctx, c_ctx, ada_w, ada_b, norm_g, w_in, na_rpb, ret_decay_logit, w_proj_na, w_proj_ret, w_out, final_g, loss_target, m_c_ctx, m_ada_w, m_ada_b, m_norm_g, m_w_in, m_na_rpb, m_ret_decay_logit, m_w_proj_na, m_w_proj_ret, m_w_out, m_final_g, v_c_ctx, v_ada_w, v_ada_b, v_norm_g, v_w_in, v_na_rpb, v_ret_decay_logit, v_w_proj_na, v_w_proj_ret, v_w_out, v_final_g):
    depth = w_in.shape[0]
    n_lat, d = x.shape[1], x.shape[2]
    n_ctx = ctx.shape[1]
    t = n_lat + n_ctx
    w_na = w_proj_na.shape[1]
    w_retv = w_proj_ret.shape[1] * N_DEV
    in_cols = w_in.shape[2] * N_DEV
    w_qk = (in_cols - 4 * w_na - 2 * w_retv - 2 * d) // 2
    sizes = (w_na, w_na, w_na, w_na, w_qk, w_qk, w_retv, w_retv, d, d)
    off = tuple(int(o) for o in np.cumsum((0,) + sizes))
    NA_Q, NA_K, NA_V, NA_Z, RET_Q, RET_K, RET_V, RET_Z, G_NA, G_RET = range(10)
    rows = n_lat // GRID_W
    me = _my_position()
    my_idx = _linear(me)
    tm_row = _pick(n_ctx, (256, 128))

    idx_arr = jnp.reshape(my_idx, (1,)).astype(jnp.int32)
    chip_idx = jnp.stack([_linear(_flip(me, kc)) for kc in CHIP_BITS]).astype(jnp.int32)

    w_axes = (1, 1, 0, 0)
    w_names = ("w_in", "w_proj_na", "w_proj_ret", "w_out")
    shard = [[w[l].astype(BF16) for w in (w_in, w_proj_na, w_proj_ret, w_out)] for l in range(depth)]
    groups = [[(0, 0)], [(0, 1), (0, 2), (0, 3)]] + [[(l, a) for a in range(4)] for l in range(1, depth)]
    gathers = {}

    def start_gather(gi, deps):
        keys = groups[gi]
        srcs = [shard[l][a] for l, a in keys]
        axes = tuple(w_axes[a] for _, a in keys)
        lands = [_place_shard(s, lax.empty(tuple(n * (N_DEV if i == ax else 1) for i, n in enumerate(s.shape)), BF16),
                              ax, idx_arr) for s, ax in zip(srcs, axes)]
        sizes = tuple(s.shape[ax] for s, ax in zip(srcs, axes))
        sems, srcs, lands, tok = _push_start(f"gather_start_{gi}", srcs, lands, axes, _gather_views, GATHER_BITS, deps)
        flight = dict(name=f"gather_wait_{gi}", sems=sems, srcs=srcs, lands=lands, axes=axes, sizes=sizes, ready=None)
        for pos, key in enumerate(keys):
            gathers[key] = (flight, pos)
        return tok

    token = start_gather(0, ())

    ncol = ada_w.shape[2]
    c_all, _ = _small_allgather(jnp.pad(c, ((0, 7), (0, 0))) + token[:, :1], "allgather_c")
    cin = jnp.concatenate([c_all[:, 0, :], c_ctx[None, :], jnp.zeros((7, d), F32)], axis=0)
    ada_b_cols = lax.dynamic_slice_in_dim(ada_b, my_idx * ncol, ncol, axis=1)[:, None, :]
    mod_cols = _ada_fwd_call(cin, ada_w, ada_b_cols)
    mod_gathered, _ = _small_allgather(mod_cols.reshape(depth * 16, ncol), "allgather_mod")
    mod_all = mod_gathered.reshape(N_DEV, depth, 16, ncol).transpose(1, 2, 0, 3).reshape(depth, 16, N_DEV * ncol)
    mod_lat = lax.dynamic_index_in_dim(mod_all, my_idx, axis=1, keepdims=False)
    mod_ctx = mod_all[:, 8, :]
    token = mod_gathered
    for gi in range(1, len(groups)):
        token = start_gather(gi, (token,))

    def landed(l, a, act):
        flight, pos = gathers[(l, a)]
        if flight["ready"] is None:
            arrived = _push_wait(flight["name"], flight["sems"], flight["srcs"], flight["lands"],
                                 flight["axes"], _gather_views, GATHER_BITS, (act, token))[1]
            flight["ready"] = _gather_finish(arrived, flight["axes"], flight["sizes"])
        return flight["ready"][pos]

    pending, scatters = {}, []

    def send_dw(l, a, dw):
        pending[(l, a)] = dw
        if a == 0:
            keys = [(0, 0)] if l == 0 else [(l, b) for b in range(4)]
        elif l == 0 and a == 1:
            keys = [(0, 1), (0, 2), (0, 3)]
        else:
            return None
        srcs = [pending[k] for k in keys]
        axes = tuple(w_axes[b] for _, b in keys)
        sizes = tuple(s.shape[ax] // N_DEV for s, ax in zip(srcs, axes))
        slab_shapes = [tuple(n // (N_DEV if i == ax else 1) for i, n in enumerate(s.shape)) for s, ax in zip(srcs, axes)]
        by_chip = keys == [(0, 0)]
        if by_chip:
            theirs = _pair_exchange(srcs, axes, sizes)
            srcs = [_pair_add(g, p, ax, chip_idx) for g, p, ax in zip(srcs, theirs, axes)]
            lands = [lax.empty((3,) + shp, BF16) for shp in slab_shapes]
            views, bits = _chip_views, CHIP_BITS[1:]
        else:
            lands = [lax.empty((N_DEV - 1,) + shp, BF16) for shp in slab_shapes]
            views, bits = _scatter_views, PEER_BITS
        sems, srcs, lands, tok = _push_start(f"scatter_start_{len(scatters)}", srcs, lands, axes, views, bits, ())
        scatters.append(dict(name=f"scatter_wait_{len(scatters)}", sems=sems, srcs=srcs, lands=lands, axes=axes, keys=keys,
                             views=views, bits=bits, by_chip=by_chip))
        return tok

    cos2, sin2 = _rope_tables(t, n_lat)
    k_scale = RET_KEY_DIM ** -0.5
    assert off[RET_Q] % w_qk == 0 and off[RET_K] % w_qk == 0
    assert off[NA_Z] % w_na == 0 and off[G_NA] % d == 0 and off[G_RET] % d == 0 and off[RET_Z] % w_retv == 0
    assert off[RET_V] % (_ret_head_group(w_retv // RET_VAL_DIM) * RET_VAL_DIM) == 0
    na_cols = tuple(off[i] // NA_HEAD_DIM for i in (NA_Q, NA_K, NA_V))
    norm_mod_fwd, norm_mod_bwd = _make_rowwise(_f_norm_mod, "norm_mod", (BF16,), (d,), n_lat, tm_row, (0,))
    gate_na_fwd, gate_na_bwd = _make_rowwise(_f_gate_na, "gate_na", (BF16,), (w_na,), n_lat, tm_row, (0, 1),
                                             col_blocks={1: (w_na, off[NA_Z] // w_na)})
    merge_fwd, merge_bwd = _make_rowwise(_f_merge, "merge", (BF16,), (d,), n_lat, tm_row, (0, 1, 2, 3),
                                         col_blocks={0: (d, off[G_NA] // d), 1: (d, off[G_RET] // d)},
                                         drow_dtypes={2: BF16, 3: BF16})
    residual_fwd, _ = _make_rowwise(_f_residual, "residual", (F32,), (d,), n_lat, tm_row, (0, 1))
    _, residual_bwd = _make_rowwise(lambda out, gate: (gate * out,), "residual", (F32,), (d,), n_lat, tm_row, (0,),
                                    drow_dtypes={0: BF16})
    loss_fwd, loss_bwd = _make_rowwise(_f_loss, "loss_head", (F32,), (128,), n_lat, tm_row, (0,))

    def pair(a, b):
        return jnp.stack([a, b])[:, None, :]

    def mod_vectors(mod_lat_l, mod_ctx_l, norm_g_l):
        shift, scale, gate = jnp.split(mod_lat_l, 3)
        c_shift, c_scale, c_gate = jnp.split(mod_ctx_l, 3)
        return pair(norm_g_l, norm_g_l), pair(scale, c_scale), pair(shift, c_shift), pair(gate, c_gate)

    def log_decay(logit):
        return jax.nn.log_sigmoid(logit.astype(F32))

    xa = jnp.concatenate([x[0], ctx[0]], axis=0)
    saved = []
    for l in range(depth):
        vecs, vecs_vjp = jax.vjp(mod_vectors, mod_lat[l], mod_ctx[l], norm_g[l])
        (h,) = norm_mod_fwd((xa,), vecs[:3])
        wl_in = landed(l, 0, h)
        u = _matmul(h, wl_in, out_dtype=BF16, name="in_proj_fwd")
        qr, kr = _rope_fwd_call(u, off[RET_Q] // w_qk, off[RET_K] // w_qk, w_qk, cos2, sin2, k_scale, tm_row)
        bt, bt_vjp = jax.vjp(lambda r: _na_bias_table(r, rows), na_rpb[l])
        lam, lam_vjp = jax.vjp(log_decay, ret_decay_logit[l])
        o_na = _na_fwd_call(u, u, u, na_cols, w_na, bt, n_lat)
        o_f, st_f = _ret_fwd_call(qr, kr, u, off[RET_V], lam[0], n_lat, False)
        o_b, st_b = _ret_fwd_call(qr, kr, u, off[RET_V], lam[1], n_lat, True)
        (a_na,) = gate_na_fwd((o_na, u), ())
        a_ret = _gate_ret_fwd_call(o_f, o_b, u, off[RET_Z] // w_retv, tm_row)
        wl_pna, wl_pret, wl_out = landed(l, 1, a_na), landed(l, 2, a_na), landed(l, 3, a_na)
        y_na = _matmul(a_na, wl_pna, out_dtype=BF16, name="proj_na_fwd")
        y_ret = _matmul(a_ret, wl_pret, out_dtype=BF16, name="proj_ret_fwd")
        (merged,) = merge_fwd((u, u, y_na, y_ret), ())
        out = _matmul(merged, wl_out, out_dtype=F32, name="out_proj_fwd")
        (xa_next,) = residual_fwd((xa, out), vecs[3:])
        saved.append(dict(xa=xa, vecs=vecs, vecs_vjp=vecs_vjp, h=h, w=(wl_in, wl_pna, wl_pret, wl_out), u=u, qr=qr, kr=kr,
                          bt=bt, bt_vjp=bt_vjp, lam=lam, lam_vjp=lam_vjp, o_na=o_na, o_f=o_f,
                          o_b=o_b, st_f=st_f, st_b=st_b, a_na=a_na, a_ret=a_ret, y_na=y_na, y_ret=y_ret,
                          merged=merged, out=out))
        xa = xa_next

    fg_pair, fg_vjp = jax.vjp(lambda g: pair(g, g), final_g)
    (loss_rows,) = loss_fwd((xa, loss_target[0]), (fg_pair,))
    loss = lax.psum(jnp.sum(loss_rows), ("x", "y", "c"))
    (dx_last,), (d_fg_pair,) = loss_bwd((xa, loss_target[0]), (fg_pair,), (jnp.ones_like(loss_rows),))
    (d_final_g,) = fg_vjp(d_fg_pair)
    dxa = jnp.pad(dx_last, ((0, n_ctx), (0, 0)))

    d_mod_lat, d_mod_ctx, d_norm_g, d_rpb, d_decay = ([None] * depth for _ in range(5))
    for l in reversed(range(depth)):
        s = saved[l]
        u, qr, kr = s["u"], s["qr"], s["kr"]
        wl_in, wl_pna, wl_pret, wl_out = s["w"]
        (d_out,), (d_gate,) = residual_bwd((s["out"],), s["vecs"][3:], (dxa,))
        dxa_res = dxa
        send_dw(l, 3, _matmul(s["merged"], d_out, trans_a=True, out_dtype=BF16, name="out_proj_dw"))
        d_merged = _matmul(d_out, wl_out, trans_b=True, out_dtype=BF16, name="out_proj_da")
        (dg_na, dg_ret, dy_na, dy_ret), _ = merge_bwd((u, u, s["y_na"], s["y_ret"]), (), (d_merged,))
        send_dw(l, 2, _matmul(s["a_ret"], dy_ret, trans_a=True, out_dtype=BF16, name="proj_ret_dw"))
        da_ret = _matmul(dy_ret, wl_pret, trans_b=True, out_dtype=BF16, name="proj_ret_da")
        tok = send_dw(l, 1, _matmul(s["a_na"], dy_na, trans_a=True, out_dtype=BF16, name="proj_na_dw"))
        da_na = _matmul(dy_na, wl_pna, trans_b=True, out_dtype=BF16, name="proj_na_da", after=tok)
        do_ret, dz_ret = _gate_ret_bwd_call(s["o_f"], s["o_b"], u, off[RET_Z] // w_retv, da_ret, tm_row)
        (do_na, dz_na), _ = gate_na_bwd((s["o_na"], u), (), (da_na,))
        dq_f, dk_f, dv_f, dl_f = _ret_bwd_call(qr, kr, u, off[RET_V], s["lam"][0], s["st_f"], do_ret, n_lat, False)
        dq_b, dk_b, dv_b, dl_b = _ret_bwd_call(qr, kr, u, off[RET_V], s["lam"][1], s["st_b"], do_ret, n_lat, True)
        dq, dk, dv, dbt = _na_bwd_call(u, u, u, na_cols, w_na, s["bt"], do_na, n_lat)
        du = _assemble_du_call([dq, dk, dv, dz_na, (dq_f, dq_b), (dk_f, dk_b), (dv_f, dv_b), dz_ret, dg_na, dg_ret],
                               off, cos2, sin2, k_scale, _pick(n_ctx, (128,)))
        (d_rpb[l],) = s["bt_vjp"](dbt)
        (d_decay[l],) = s["lam_vjp"](jnp.stack([dl_f[:, 0, 0], dl_b[:, 0, 0]]))
        tok = send_dw(l, 0, _matmul(s["h"], du, trans_a=True, out_dtype=BF16, name="in_proj_dw"))
        dh = _matmul(du, wl_in, trans_b=True, out_dtype=BF16, name="in_proj_da", after=tok)
        (dxa,), d_vecs = norm_mod_bwd((s["xa"],), s["vecs"][:3], (dh,), acc=(dxa_res,))
        d_mod_lat[l], d_mod_ctx[l], d_norm_g[l] = s["vecs_vjp"](tuple(d_vecs) + (d_gate,))
    gx = dxa[:n_lat]
    d_mod_lat, d_mod_ctx, d_norm_g, d_rpb, d_decay = (jnp.stack(a) for a in (d_mod_lat, d_mod_ctx, d_norm_g, d_rpb, d_decay))

    small_shapes = [d_mod_lat.shape, d_mod_ctx.shape, d_norm_g.shape, d_final_g.shape, d_rpb.shape, d_decay.shape]
    packed = _pack([d_mod_lat, d_mod_ctx, d_norm_g, d_final_g, d_rpb, d_decay])
    g_all, g_sum = _small_allgather(packed, "allgather_small_grads")
    dml_sum, dmc_sum, grad_norm_g, grad_final_g, grad_na_rpb, grad_decay = _unpack(g_sum, small_shapes)
    grad_ada_b = dml_sum + dmc_sum
    dml_all = g_all.reshape(N_DEV, -1)[:, :depth * 3 * d].reshape(N_DEV, depth, 3 * d)

    def my_cols(a):
        return lax.dynamic_slice_in_dim(a, my_idx * ncol, ncol, axis=a.ndim - 1)

    dmod = jnp.concatenate(
        [my_cols(dml_all).transpose(1, 0, 2), my_cols(dmc_sum)[:, None, :], jnp.zeros((depth, 7, ncol), F32)], axis=1)
    grad_ada_w, dcs_part = _ada_bwd_call(cin, ada_w, dmod)
    _, dcs = _small_allgather(dcs_part, "allgather_dcsilu")
    sg = jax.nn.sigmoid(c_ctx)
    grad_c_ctx = dcs[8] * (sg * (1.0 + c_ctx * (1.0 - sg)))

    def flat2(a):
        return a.reshape(a.shape[0] * a.shape[1], a.shape[2])

    small_w = [c_ctx, ada_b, norm_g, na_rpb, ret_decay_logit, final_g]
    small_g = [grad_c_ctx, grad_ada_b, grad_norm_g, grad_na_rpb, grad_decay, grad_final_g]
    small_m = [m_c_ctx, m_ada_b, m_norm_g, m_na_rpb, m_ret_decay_logit, m_final_g]
    small_v = [v_c_ctx, v_ada_b, v_norm_g, v_na_rpb, v_ret_decay_logit, v_final_g]
    shp = [a.shape for a in small_w]
    ds_, nms_, nvs_ = _adamw_dense(_pack(small_w), _pack(small_g), _pack(small_m), _pack(small_v), "adamw_small")
    ds_, nms_, nvs_ = _unpack(ds_, shp), _unpack(nms_, shp), _unpack(nvs_, shp)

    d_ada, nm_ada, nv_ada = [a.reshape(ada_w.shape) for a in _adamw_dense(
        flat2(ada_w), flat2(grad_ada_w), flat2(m_ada_w), flat2(v_ada_w), "adamw_ada_w")]

    w_all = (w_in, w_proj_na, w_proj_ret, w_out)
    m_all = (m_w_in, m_w_proj_na, m_w_proj_ret, m_w_out)
    v_all = (v_w_in, v_w_proj_na, v_w_proj_ret, v_w_out)
    upd = [None] * 4
    after = d_ada
    for flight in scatters:
        mine, slabs = _push_wait(flight["name"], flight["sems"], flight["srcs"], flight["lands"], flight["axes"],
                                 flight["views"], flight["bits"], (after,))
        for (l, a), own, s in zip(flight["keys"], mine, slabs):
            upd[a] = _adamw_sharded(w_all[a], m_all[a], v_all[a], own, s, None if flight["by_chip"] else w_axes[a],
                                    idx_arr, l, upd[a], "adamw_" + w_names[a])
            after = upd[a][1]
    (g_w_in, d_w_in, nm_w_in, nv_w_in), (g_pna, d_pna, nm_pna, nv_pna) = upd[0], upd[1]
    (g_pret, d_pret, nm_pret, nv_pret), (g_out, d_out, nm_out, nv_out) = upd[2], upd[3]

    def order(cc, aw, ab, ng, wi, rp, dl, pn, pr, wo, fg):
        return [cc, aw, ab, ng, wi, rp, dl, pn, pr, wo, fg]

    grads_out = order(grad_c_ctx, grad_ada_w, grad_ada_b, grad_norm_g, g_w_in, grad_na_rpb, grad_decay, g_pna, g_pret, g_out, grad_final_g)
    delta_out = order(ds_[0], d_ada, ds_[1], ds_[2], d_w_in, ds_[3], ds_[4], d_pna, d_pret, d_out, ds_[5])
    m_out = order(nms_[0], nm_ada, nms_[1], nms_[2], nm_w_in, nms_[3], nms_[4], nm_pna, nm_pret, nm_out, nms_[5])
    v_out = order(nvs_[0], nv_ada, nvs_[1], nvs_[2], nv_w_in, nvs_[3], nvs_[4], nv_pna, nv_pret, nv_out, nvs_[5])
    return (loss, gx[None], *grads_out, *delta_out, *m_out, *v_out)
```

```python
import functools

import numpy as np
import jax
import jax.numpy as jnp
from jax import lax
from jax.experimental import pallas as pl
from jax.experimental.pallas import tpu as pltpu

F32 = jnp.float32
BF16 = jnp.bfloat16

N_DEV = 8
GRID_W = 64
NA_HEAD_DIM = 128
NA_WIN_ROWS = 8
NA_WIN_COLS = 16
RET_KEY_DIM = 128
RET_VAL_DIM = 256
RET_CHUNK = 128
ROPE_BASE = 10000.0
NORM_EPS = 1e-6
MASK_VALUE = -1e30

ADAM_LR = 0.001
ADAM_B1 = 0.9
ADAM_B2 = 0.999
ADAM_EPS = 1e-08
ADAM_WD = 0.01
ADAM_STEP = 10

VMEM_LIMIT = 48 * 1024 * 1024
MESH = pl.DeviceIdType.MESH
ANY = pl.BlockSpec(memory_space=pl.ANY)
VMEM_SPEC = pl.BlockSpec(memory_space=pltpu.VMEM)


def _params(sem=None):
    return pltpu.CompilerParams(dimension_semantics=sem, vmem_limit_bytes=VMEM_LIMIT)


def _pick(n, prefs):
    for p in prefs:
        if n % p == 0:
            return p
    return n


def _dot(a, b):
    return lax.dot_general(a, b, (((1,), (0,)), ((), ())), preferred_element_type=F32)


def _dot_nt(a, b):
    return lax.dot_general(a, b, (((1,), (1,)), ((), ())), preferred_element_type=F32)


def _dot_tn(a, b):
    return lax.dot_general(a, b, (((0,), (0,)), ((), ())), preferred_element_type=F32)


def _silu(x):
    return x * jax.nn.sigmoid(x)


def _matmul(a, b, *, trans_a=False, trans_b=False, out_dtype=F32, name="matmul", after=None):
    if trans_a:
        kdim, m = a.shape
    else:
        m, kdim = a.shape
    if trans_b:
        n, kb = b.shape
    else:
        kb, n = b.shape
    assert kdim == kb, (a.shape, b.shape, trans_a, trans_b)
    tm = _pick(m, (1152, 1024, 768, 512, 256, 128))
    tn = _pick(n, (1024, 512, 256, 128) if trans_b else (512, 256, 128))
    tk = _pick(kdim, (2304, 2048, 1024, 512, 256, 128))
    nk = kdim // tk
    dn = (((0 if trans_a else 1,), (1 if trans_b else 0,)), ((), ()))

    def body(a_ref, b_ref, *rest):
        o_ref, acc_ref = rest[-2:]
        part = lax.dot_general(a_ref[...], b_ref[...], dn, preferred_element_type=F32)
        if nk == 1:
            o_ref[...] = part.astype(o_ref.dtype)
        else:
            k = pl.program_id(2)

            @pl.when(k == 0)
            def _():
                acc_ref[...] = part

            @pl.when(k > 0)
            def _():
                acc_ref[...] += part

            @pl.when(k == nk - 1)
            def _():
                o_ref[...] = acc_ref[...].astype(o_ref.dtype)

    a_spec = pl.BlockSpec((tk, tm), lambda i, j, k: (k, i)) if trans_a else pl.BlockSpec((tm, tk), lambda i, j, k: (i, k))
    b_spec = pl.BlockSpec((tn, tk), lambda i, j, k: (j, k)) if trans_b else pl.BlockSpec((tk, tn), lambda i, j, k: (k, j))
    return pl.pallas_call(
        body,
        name=name,
        grid=(m // tm, n // tn, nk),
        in_specs=[a_spec, b_spec] + ([] if after is None else [ANY]),
        out_specs=pl.BlockSpec((tm, tn), lambda i, j, k: (i, j)),
        out_shape=jax.ShapeDtypeStruct((m, n), out_dtype),
        scratch_shapes=[pltpu.VMEM((tm, tn) if nk > 1 else (8, 128), F32)],
        compiler_params=_params(("parallel", "parallel", "arbitrary")),
    )(*((a, b) if after is None else (a, b, after)))


def _make_rowwise(f, name, out_dtypes, out_cols, n_lat, tm, diff_rows, col_blocks=None, drow_dtypes=None):
    drow_dtypes = drow_dtypes or {}

    def tile_fn(*args):
        return tuple(o.astype(dt) for o, dt in zip(f(*args), out_dtypes))

    def row_spec(k, arr):
        width, index = (col_blocks or {}).get(k, (arr.shape[1], 0))
        return pl.BlockSpec((tm, width), lambda i: (i, index))

    def row_width(k, arr):
        return (col_blocks or {}).get(k, (arr.shape[1], 0))[0]

    def fwd_call(rows, vecs):
        t = min(r.shape[0] for r in rows)
        nr, nv = len(rows), len(vecs)
        nl = n_lat // tm

        def body(*refs):
            grp = (pl.program_id(0) >= nl).astype(jnp.int32)
            args = [r[...] for r in refs[:nr]] + [v[grp] for v in refs[nr:nr + nv]]
            for o_ref, o in zip(refs[nr + nv:], tile_fn(*args)):
                o_ref[...] = o

        return pl.pallas_call(
            body,
            name=name + "_fwd",
            grid=(t // tm,),
            in_specs=[row_spec(k, r) for k, r in enumerate(rows)]
            + [pl.BlockSpec(v.shape, lambda i: (0, 0, 0)) for v in vecs],
            out_specs=[pl.BlockSpec((tm, c), lambda i: (i, 0)) for c in out_cols],
            out_shape=[jax.ShapeDtypeStruct((t, c), dt) for c, dt in zip(out_cols, out_dtypes)],
            compiler_params=_params(("parallel",)),
        )(*rows, *vecs)

    def bwd_call(rows, vecs, gs, acc=None):
        t = min(r.shape[0] for r in rows)
        nr, nv, ng = len(rows), len(vecs), len(gs)
        nl = n_lat // tm
        nd = len(diff_rows)
        acc = [None] * nd if acc is None else list(acc)
        acc_in = [a for a in acc if a is not None]

        def body(*refs):
            i = pl.program_id(0)
            grp = (i >= nl).astype(jnp.int32)
            args = [r[...] for r in refs[:nr]] + [v[grp] for v in refs[nr:nr + nv]]
            g_refs = refs[nr + nv:nr + nv + ng]
            acc_refs = list(refs[nr + nv + ng:nr + nv + ng + len(acc_in)])
            drow_refs = refs[nr + nv + ng + len(acc_in):nr + nv + ng + len(acc_in) + nd]
            dvec_refs = refs[nr + nv + ng + len(acc_in) + nd:]
            _, vjp = jax.vjp(tile_fn, *args)
            grads = vjp(tuple(g[...] for g in g_refs))
            for d_ref, k, a in zip(drow_refs, diff_rows, acc):
                gk = grads[k] if a is None else grads[k] + acc_refs.pop(0)[...]
                d_ref[...] = gk.astype(d_ref.dtype)

            @pl.when(i == 0)
            def _():
                for d_ref in dvec_refs:
                    d_ref[...] = jnp.zeros_like(d_ref)

            for j, d_ref in enumerate(dvec_refs):
                d_ref[grp] += grads[nr + j]

        outs = pl.pallas_call(
            body,
            name=name + "_bwd",
            grid=(t // tm,),
            in_specs=[row_spec(k, r) for k, r in enumerate(rows)]
            + [pl.BlockSpec(v.shape, lambda i: (0, 0, 0)) for v in vecs]
            + [pl.BlockSpec((tm, g.shape[1]), lambda i: (i, 0)) for g in gs]
            + [pl.BlockSpec((tm, a.shape[1]), lambda i: (i, 0)) for a in acc_in],
            out_specs=[pl.BlockSpec((tm, row_width(k, rows[k])), lambda i: (i, 0)) for k in diff_rows]
            + [pl.BlockSpec(v.shape, lambda i: (0, 0, 0)) for v in vecs],
            out_shape=[jax.ShapeDtypeStruct((t, row_width(k, rows[k])), drow_dtypes.get(k, rows[k].dtype))
                       for k in diff_rows]
            + [jax.ShapeDtypeStruct(v.shape, F32) for v in vecs],
            compiler_params=_params(("arbitrary",)),
        )(*rows, *vecs, *gs, *acc_in)
        return outs[:nd], outs[nd:]

    return fwd_call, bwd_call


def _f_norm_mod(x, g, scale, shift):
    r = lax.rsqrt(jnp.mean(x * x, axis=-1, keepdims=True) + NORM_EPS)
    return ((x * r * g) * (1.0 + scale) + shift,)


def _f_gate_na(o, z):
    return (o.astype(F32) * _silu(z.astype(F32)),)


def _f_merge(g_na, g_ret, y_na, y_ret):
    return (jax.nn.sigmoid(g_na.astype(F32)) * y_na.astype(F32) + jax.nn.sigmoid(g_ret.astype(F32)) * y_ret.astype(F32),)


def _f_residual(x, out, gate):
    return (x + gate * out,)


def _f_loss(x, target, g):
    r = lax.rsqrt(jnp.mean(x * x, axis=-1, keepdims=True) + NORM_EPS)
    y = x * r * g
    e = 0.5 * jnp.mean(jnp.square(y - target), axis=-1, keepdims=True)
    return (jnp.broadcast_to(e * (1.0 / 128.0), (x.shape[0], 128)),)


def _gate_ret_fwd_call(of, ob, z, zblk, tm):
    t, w = of.shape
    nh = w // RET_VAL_DIM

    def body(of_ref, ob_ref, z_ref, a_ref):
        for hh in range(nh):
            sl = slice(hh * RET_VAL_DIM, (hh + 1) * RET_VAL_DIM)
            o = of_ref[:, sl].astype(F32) + ob_ref[:, sl].astype(F32)
            r = lax.rsqrt(jnp.mean(o * o, axis=-1, keepdims=True) + NORM_EPS)
            a_ref[:, sl] = ((o * r) * _silu(z_ref[:, sl].astype(F32))).astype(a_ref.dtype)

    spec = pl.BlockSpec((tm, w), lambda i: (i, 0))
    zspec = pl.BlockSpec((tm, w), lambda i: (i, zblk))
    return pl.pallas_call(
        body, name="gate_ret_fwd", grid=(t // tm,), in_specs=[spec, spec, zspec], out_specs=spec,
        out_shape=jax.ShapeDtypeStruct((t, w), BF16), compiler_params=_params(("parallel",)),
    )(of, ob, z)


def _gate_ret_bwd_call(of, ob, z, zblk, da, tm):
    t, w = of.shape
    nh = w // RET_VAL_DIM

    def body(of_ref, ob_ref, z_ref, da_ref, do_ref, dz_ref):
        for hh in range(nh):
            sl = slice(hh * RET_VAL_DIM, (hh + 1) * RET_VAL_DIM)
            o = of_ref[:, sl].astype(F32) + ob_ref[:, sl].astype(F32)
            r = lax.rsqrt(jnp.mean(o * o, axis=-1, keepdims=True) + NORM_EPS)
            n = o * r
            zf = z_ref[:, sl].astype(F32)
            sg = jax.nn.sigmoid(zf)
            g = da_ref[:, sl].astype(F32)
            dn = g * (zf * sg)
            dz_ref[:, sl] = (g * n * (sg * (1.0 + zf * (1.0 - sg)))).astype(dz_ref.dtype)
            do_ref[:, sl] = (r * (dn - n * jnp.mean(dn * n, axis=-1, keepdims=True))).astype(do_ref.dtype)

    spec = pl.BlockSpec((tm, w), lambda i: (i, 0))
    zspec = pl.BlockSpec((tm, w), lambda i: (i, zblk))
    return pl.pallas_call(
        body, name="gate_ret_bwd", grid=(t // tm,), in_specs=[spec, spec, zspec, spec], out_specs=[spec, spec],
        out_shape=[jax.ShapeDtypeStruct((t, w), BF16), jax.ShapeDtypeStruct((t, w), z.dtype)],
        compiler_params=_params(("parallel",)),
    )(of, ob, z, da)


NA_PAIR = 2 * GRID_W
NA_KEY_ROWS = NA_WIN_ROWS + 2
NA_CLASSES = 5


def _na_geometry(t, n_lat):
    rows = n_lat // GRID_W
    assert rows % 2 == 0 and rows >= NA_KEY_ROWS + 2, rows
    return rows, rows // 2, NA_KEY_ROWS * GRID_W, t - n_lat, t // NA_PAIR


def _na_base(p, rows):
    return jnp.clip(2 * p - NA_WIN_ROWS // 2, 0, rows - NA_KEY_ROWS)


def _na_class(p, rows):
    return p - _na_base(p, rows) // 2


def _na_group(pairs, n_ctx):
    assert n_ctx % NA_PAIR == 0, n_ctx
    return 2 if pairs % 2 == 0 and (n_ctx // NA_PAIR) % 2 == 0 else 1


def _na_bias_spec():
    return pl.BlockSpec((1, NA_CLASSES, 2, NA_KEY_ROWS // 2, GRID_W, NA_PAIR), lambda h, g: (h, 0, 0, 0, 0, 0))


def _na_bias_tile(bt_ref, cls):
    return jnp.concatenate(
        [jnp.concatenate([bt_ref[0, cls, i, q] for q in range(NA_KEY_ROWS // 2)], axis=1) for i in range(2)], axis=0)


def _na_add_bias_grad(dbt_ref, cls, ds):
    for i in range(2):
        for q in range(NA_KEY_ROWS // 2):
            dbt_ref[0, cls, i, q] += ds[i * GRID_W:(i + 1) * GRID_W, q * NA_PAIR:(q + 1) * NA_PAIR]


def _na_fwd_call(q, k, v, col0, w, bt, n_lat):
    t = q.shape[0]
    nh = w // NA_HEAD_DIM
    rows, pairs, n_loc, n_ctx, nq = _na_geometry(t, n_lat)
    grp = _na_group(pairs, n_ctx)
    scale = NA_HEAD_DIM ** -0.5

    def body(q_ref, k_ref, v_ref, bt_ref, o_ref):
        g = pl.program_id(1)
        kc = k_ref[pl.ds(n_lat, n_ctx), :]
        vc = v_ref[pl.ds(n_lat, n_ctx), :]

        @pl.when(g < pairs // grp)
        def _():
            for i in range(grp):
                p = g * grp + i
                sl = slice(i * NA_PAIR, (i + 1) * NA_PAIR)
                qb = q_ref[sl, :]
                s_ctx = _dot_nt(qb, kc) * scale
                start = pl.multiple_of(_na_base(p, rows) * GRID_W, GRID_W)
                kw = k_ref[pl.ds(start, n_loc), :]
                vw = v_ref[pl.ds(start, n_loc), :]
                s_loc = _dot_nt(qb, kw) * scale + _na_bias_tile(bt_ref, _na_class(p, rows))
                m = jnp.maximum(jnp.max(s_loc, axis=-1, keepdims=True), jnp.max(s_ctx, axis=-1, keepdims=True))
                p_loc = jnp.exp(s_loc - m)
                p_ctx = jnp.exp(s_ctx - m)
                l = jnp.sum(p_loc, axis=-1, keepdims=True) + jnp.sum(p_ctx, axis=-1, keepdims=True)
                o = _dot(p_loc.astype(BF16), vw) + _dot(p_ctx.astype(BF16), vc)
                o_ref[sl, :] = (o / l).astype(o_ref.dtype)

        @pl.when(g >= pairs // grp)
        def _():
            s_ctx = _dot_nt(q_ref[...], kc) * scale
            m = jnp.max(s_ctx, axis=-1, keepdims=True)
            p = jnp.exp(s_ctx - m)
            l = jnp.sum(p, axis=-1, keepdims=True)
            o_ref[...] = (_dot(p.astype(BF16), vc) / l).astype(o_ref.dtype)

    qspec = pl.BlockSpec((grp * NA_PAIR, NA_HEAD_DIM), lambda h, g: (g, h))
    in_q = pl.BlockSpec((grp * NA_PAIR, NA_HEAD_DIM), lambda h, g: (g, col0[0] + h))
    in_k = pl.BlockSpec((t, NA_HEAD_DIM), lambda h, g: (0, col0[1] + h))
    in_v = pl.BlockSpec((t, NA_HEAD_DIM), lambda h, g: (0, col0[2] + h))
    return pl.pallas_call(
        body,
        name="na_attn_fwd",
        grid=(nh, nq // grp),
        in_specs=[in_q, in_k, in_v, _na_bias_spec()],
        out_specs=qspec,
        out_shape=jax.ShapeDtypeStruct((t, w), BF16),
        compiler_params=_params(("parallel", "arbitrary")),
    )(q, k, v, bt)


def _na_bwd_call(q, k, v, col0, w, bt, do, n_lat):
    t = q.shape[0]
    nh = w // NA_HEAD_DIM
    rows, pairs, n_loc, n_ctx, nq = _na_geometry(t, n_lat)
    scale = NA_HEAD_DIM ** -0.5
    grp = _na_group(pairs, n_ctx)

    def body(q_ref, k_ref, v_ref, do_ref, bt_ref, dq_ref, dk_ref, dv_ref, dbt_ref):
        g = pl.program_id(1)

        @pl.when(g == 0)
        def _():
            dk_ref[...] = jnp.zeros_like(dk_ref)
            dv_ref[...] = jnp.zeros_like(dv_ref)
            dbt_ref[...] = jnp.zeros_like(dbt_ref)

        kc = k_ref[pl.ds(n_lat, n_ctx), :]
        vc = v_ref[pl.ds(n_lat, n_ctx), :]

        @pl.when(g < pairs // grp)
        def _():
            for i in range(grp):
                p = g * grp + i
                sl = slice(i * NA_PAIR, (i + 1) * NA_PAIR)
                qb = q_ref[sl, :]
                dob = do_ref[sl, :]
                s_ctx = _dot_nt(qb, kc) * scale
                dp_ctx = _dot_nt(dob, vc)
                start = pl.multiple_of(_na_base(p, rows) * GRID_W, GRID_W)
                kw = k_ref[pl.ds(start, n_loc), :]
                vw = v_ref[pl.ds(start, n_loc), :]
                cls = _na_class(p, rows)
                s_loc = _dot_nt(qb, kw) * scale + _na_bias_tile(bt_ref, cls)
                m = jnp.maximum(jnp.max(s_loc, axis=-1, keepdims=True), jnp.max(s_ctx, axis=-1, keepdims=True))
                p_loc = jnp.exp(s_loc - m)
                p_ctx = jnp.exp(s_ctx - m)
                inv = 1.0 / (jnp.sum(p_loc, axis=-1, keepdims=True) + jnp.sum(p_ctx, axis=-1, keepdims=True))
                p_loc = p_loc * inv
                p_ctx = p_ctx * inv
                dp_loc = _dot_nt(dob, vw)
                delta = (jnp.sum(p_loc * dp_loc, axis=-1, keepdims=True)
                         + jnp.sum(p_ctx * dp_ctx, axis=-1, keepdims=True))
                ds_loc = p_loc * (dp_loc - delta)
                ds_ctx = p_ctx * (dp_ctx - delta)
                _na_add_bias_grad(dbt_ref, cls, ds_loc)
                dsl = (ds_loc * scale).astype(BF16)
                dsc = (ds_ctx * scale).astype(BF16)
                dq_ref[sl, :] = (_dot(dsl, kw) + _dot(dsc, kc)).astype(dq_ref.dtype)
                dk_ref[pl.ds(start, n_loc), :] += _dot_tn(dsl, qb)
                dv_ref[pl.ds(start, n_loc), :] += _dot_tn(p_loc.astype(BF16), dob)
                dk_ref[pl.ds(n_lat, n_ctx), :] += _dot_tn(dsc, qb)
                dv_ref[pl.ds(n_lat, n_ctx), :] += _dot_tn(p_ctx.astype(BF16), dob)

        @pl.when(g >= pairs // grp)
        def _():
            qb = q_ref[...]
            dob = do_ref[...]
            s_ctx = _dot_nt(qb, kc) * scale
            dp_ctx = _dot_nt(dob, vc)
            m = jnp.max(s_ctx, axis=-1, keepdims=True)
            p = jnp.exp(s_ctx - m)
            p = p * (1.0 / jnp.sum(p, axis=-1, keepdims=True))
            delta = jnp.sum(p * dp_ctx, axis=-1, keepdims=True)
            dsc = (p * (dp_ctx - delta) * scale).astype(BF16)
            dq_ref[...] = _dot(dsc, kc).astype(dq_ref.dtype)
            dk_ref[pl.ds(n_lat, n_ctx), :] += _dot_tn(dsc, qb)
            dv_ref[pl.ds(n_lat, n_ctx), :] += _dot_tn(p.astype(BF16), dob)

    qspec = pl.BlockSpec((grp * NA_PAIR, NA_HEAD_DIM), lambda h, g: (g, h))
    kspec = pl.BlockSpec((t, NA_HEAD_DIM), lambda h, g: (0, h))
    return pl.pallas_call(
        body,
        name="na_attn_bwd",
        grid=(nh, nq // grp),
        in_specs=[pl.BlockSpec((grp * NA_PAIR, NA_HEAD_DIM), lambda h, g: (g, col0[0] + h)),
                  pl.BlockSpec((t, NA_HEAD_DIM), lambda h, g: (0, col0[1] + h)),
                  pl.BlockSpec((t, NA_HEAD_DIM), lambda h, g: (0, col0[2] + h)), qspec, _na_bias_spec()],
        out_specs=[qspec, kspec, kspec, _na_bias_spec()],
        out_shape=[
            jax.ShapeDtypeStruct((t, w), BF16),
            jax.ShapeDtypeStruct((t, w), F32),
            jax.ShapeDtypeStruct((t, w), F32),
            jax.ShapeDtypeStruct(bt.shape, F32),
        ],
        compiler_params=_params(("parallel", "arbitrary")),
    )(q, k, v, do, bt)


def _na_bias_table(rpb, rows):
    pairs = rows // 2
    nb = 2 * NA_WIN_COLS - 1
    nq = NA_KEY_ROWS // 2
    e1 = np.zeros((NA_CLASSES, 2, nq, 2, 2 * NA_WIN_ROWS - 1), np.float32)
    valid = np.zeros((NA_CLASSES, 2, nq, 2), bool)
    for cls, p in enumerate((0, 1, 2, pairs - 2, pairs - 1)):
        base = int(np.clip(2 * p - NA_WIN_ROWS // 2, 0, rows - NA_KEY_ROWS))
        assert p - base // 2 == cls, (rows, cls, p, base)
        for i in range(2):
            r = 2 * p + i
            r0 = int(np.clip(r - NA_WIN_ROWS // 2, 0, rows - NA_WIN_ROWS))
            for kk in range(NA_KEY_ROWS):
                if r0 <= base + kk < r0 + NA_WIN_ROWS:
                    valid[cls, i, kk // 2, kk % 2] = True
                    e1[cls, i, kk // 2, kk % 2, base + kk - r + NA_WIN_ROWS - 1] = 1.0
    cidx = np.arange(GRID_W)
    dc = np.clip(cidx[None, :] - cidx[:, None] + (NA_WIN_COLS - 1), 0, nb - 1)
    c0 = np.clip(cidx - NA_WIN_COLS // 2, 0, GRID_W - NA_WIN_COLS)
    col_in = (cidx[None, :] >= c0[:, None]) & (cidx[None, :] < c0[:, None] + NA_WIN_COLS)
    e2 = np.zeros((GRID_W, 2, GRID_W, 2, nb), np.float32)
    for par in range(2):
        e2[np.arange(GRID_W)[:, None], par, np.arange(GRID_W)[None, :], par, dc] = 1.0
    mask = valid[:, :, :, None, :, None] & col_in[None, None, None, :, None, :]
    t1 = jnp.einsum("hab,xiqpa->hxiqpb", rpb, jnp.asarray(e1), precision=lax.Precision.HIGHEST)
    t1 = t1.reshape(t1.shape[:4] + (2 * nb,))
    b = jnp.einsum("hxiqm,cwm->hxiqcw", t1, jnp.asarray(e2.reshape(GRID_W, 2 * GRID_W, 2 * nb)),
                   precision=lax.Precision.HIGHEST)
    return jnp.where(jnp.asarray(mask.reshape(NA_CLASSES, 2, nq, GRID_W, 2 * GRID_W))[None], b, MASK_VALUE)


def _ret_decays(lam_s, reverse):
    c = RET_CHUNK
    ii = lax.broadcasted_iota(jnp.int32, (c, c), 0)
    jj = lax.broadcasted_iota(jnp.int32, (c, c), 1)
    d = (jj - ii) if reverse else (ii - jj)
    dpos = jnp.maximum(d.astype(F32), 0.0)
    mask = jnp.where(d >= 0, jnp.exp(dpos * lam_s), 0.0)
    pi = lax.broadcasted_iota(jnp.int32, (c, 1), 0).astype(F32)
    qpos = (c - pi) if reverse else (pi + 1.0)
    kpos = pi if reverse else (c - 1.0 - pi)
    qd = jnp.exp(qpos * lam_s)
    kd = jnp.exp(kpos * lam_s)
    g = jnp.exp(jnp.full((1, RET_VAL_DIM), c * lam_s, F32))
    return mask, dpos, qd, kd, qpos, kpos, g


def _ret_head_group(nh):
    return _pick(nh, (8, 4, 2))


def _ret_chunk_of(t, nt, nl, reverse):
    return (nt - 1 - t) if reverse else (t + nl) % nt


def _ret_fwd_call(qr, kr, v, vcol, lam, n_lat, reverse):
    t = qr.shape[0]
    nh = qr.shape[1] // RET_KEY_DIM
    c = RET_CHUNK
    nt, nl = t // c, n_lat // c

    hg = _ret_head_group(nh)
    dk, dv = RET_KEY_DIM, RET_VAL_DIM

    def body(lam_ref, q_ref, k_ref, v_ref, o_ref, s_ref, state):
        hb, step = pl.program_id(0), pl.program_id(1)

        @pl.when(step == 0)
        def _():
            state[...] = jnp.zeros_like(state)

        for j in range(hg):
            mask, _, qd, kd, _, _, g = _ret_decays(lam_ref[hb * hg + j], reverse)
            q, k, vv = q_ref[:, j * dk:(j + 1) * dk], k_ref[:, j * dk:(j + 1) * dk], v_ref[:, j * dv:(j + 1) * dv]
            p = _dot_nt(q, k) * mask
            s = state[j]
            qs = (q.astype(F32) * qd).astype(BF16)
            o_ref[:, j * dv:(j + 1) * dv] = (_dot(p.astype(BF16), vv) + _dot(qs, s.astype(BF16))).astype(o_ref.dtype)
            s_ref[j, 0] = s
            ks = (k.astype(F32) * kd).astype(BF16)
            state[j] = s * g + _dot_tn(ks, vv)

    def cmap(hb, step, lam_ref):
        return (_ret_chunk_of(step, nt, nl, reverse), hb)

    def vmap(hb, step, lam_ref):
        return (_ret_chunk_of(step, nt, nl, reverse), vcol // (hg * dv) + hb)

    return pl.pallas_call(
        body,
        name="retention_rev_fwd" if reverse else "retention_fwd",
        grid_spec=pltpu.PrefetchScalarGridSpec(
            num_scalar_prefetch=1,
            grid=(nh // hg, nt),
            in_specs=[
                pl.BlockSpec((c, hg * dk), cmap),
                pl.BlockSpec((c, hg * dk), cmap),
                pl.BlockSpec((c, hg * dv), vmap),
            ],
            out_specs=[
                pl.BlockSpec((c, hg * dv), cmap),
                pl.BlockSpec((hg, 1, dk, dv), lambda hb, step, lam_ref: (hb, step, 0, 0)),
            ],
            scratch_shapes=[pltpu.VMEM((hg, dk, dv), F32)],
        ),
        out_shape=[
            jax.ShapeDtypeStruct((t, nh * RET_VAL_DIM), BF16),
            jax.ShapeDtypeStruct((nh, nt, RET_KEY_DIM, RET_VAL_DIM), F32),
        ],
        compiler_params=_params(("parallel", "arbitrary")),
    )(lam, qr, kr, v)


def _ret_bwd_call(qr, kr, v, vcol, lam, states, do, n_lat, reverse):
    assert RET_CHUNK == RET_KEY_DIM and RET_VAL_DIM % RET_KEY_DIM == 0
    t = qr.shape[0]
    nh = qr.shape[1] // RET_KEY_DIM
    c = RET_CHUNK
    nt, nl = t // c, n_lat // c

    hg = _ret_head_group(nh)
    dk, dv = RET_KEY_DIM, RET_VAL_DIM

    def body(lam_ref, q_ref, k_ref, v_ref, s_ref, do_ref, dq_ref, dk_ref, dv_ref, dl_ref, dstate):
        hb, rstep = pl.program_id(0), pl.program_id(1)

        @pl.when(rstep == 0)
        def _():
            dstate[...] = jnp.zeros_like(dstate)
            dl_ref[...] = jnp.zeros_like(dl_ref)

        for j in range(hg):
            mask, dpos, qd, kd, qpos, kpos, g = _ret_decays(lam_ref[hb * hg + j], reverse)
            ksl, vsl = slice(j * dk, (j + 1) * dk), slice(j * dv, (j + 1) * dv)
            q, k, vv = q_ref[:, ksl], k_ref[:, ksl], v_ref[:, vsl]
            qf, kf = q.astype(F32), k.astype(F32)
            s = s_ref[j, 0]
            ds = dstate[j]
            dob = do_ref[:, vsl].astype(BF16)
            sb, dsb = s.astype(BF16), ds.astype(BF16)
            a = _dot_nt(q, k)
            p = a * mask
            dp = _dot_nt(dob, vv)
            da = dp * mask
            dab = da.astype(BF16)
            dqc = _dot_nt(dob, sb)
            dkc = _dot_nt(vv, dsb)
            qs = (qf * qd).astype(BF16)
            ks = (kf * kd).astype(BF16)
            dq_ref[:, ksl] = (_dot(dab, k) + dqc * qd).astype(dq_ref.dtype)
            dk_ref[:, ksl] = (_dot_tn(dab, q) + dkc * kd).astype(dk_ref.dtype)
            dv_ref[:, vsl] = (_dot_tn(p.astype(BF16), dob) + _dot(ks, dsb)).astype(dv_ref.dtype)
            dsg = ds * s * (g * c)
            terms = da * a * dpos + dqc * qf * (qd * qpos) + dkc * kf * (kd * kpos)
            for half in range(dv // dk):
                terms = terms + dsg[:, half * dk:(half + 1) * dk]
            total = jnp.sum(jnp.sum(terms, axis=0, keepdims=True), axis=1, keepdims=True)
            dl_ref[j] += jnp.broadcast_to(total, (8, 128))
            dstate[j] = ds * g + _dot_tn(qs, dob)

    def cmap(hb, rstep, lam_ref):
        return (_ret_chunk_of(nt - 1 - rstep, nt, nl, reverse), hb)

    def vmap(hb, rstep, lam_ref):
        return (_ret_chunk_of(nt - 1 - rstep, nt, nl, reverse), vcol // (hg * dv) + hb)

    return pl.pallas_call(
        body,
        name="retention_rev_bwd" if reverse else "retention_bwd",
        grid_spec=pltpu.PrefetchScalarGridSpec(
            num_scalar_prefetch=1,
            grid=(nh // hg, nt),
            in_specs=[
                pl.BlockSpec((c, hg * dk), cmap),
                pl.BlockSpec((c, hg * dk), cmap),
                pl.BlockSpec((c, hg * dv), vmap),
                pl.BlockSpec((hg, 1, dk, dv), lambda hb, rstep, lam_ref: (hb, nt - 1 - rstep, 0, 0)),
                pl.BlockSpec((c, hg * dv), cmap),
            ],
            out_specs=[
                pl.BlockSpec((c, hg * dk), cmap),
                pl.BlockSpec((c, hg * dk), cmap),
                pl.BlockSpec((c, hg * dv), cmap),
                pl.BlockSpec((hg, 8, 128), lambda hb, rstep, lam_ref: (hb, 0, 0)),
            ],
            scratch_shapes=[pltpu.VMEM((hg, dk, dv), F32)],
        ),
        out_shape=[
            jax.ShapeDtypeStruct(qr.shape, qr.dtype),
            jax.ShapeDtypeStruct(kr.shape, kr.dtype),
            jax.ShapeDtypeStruct((t, nh * dv), v.dtype),
            jax.ShapeDtypeStruct((nh, 8, 128), F32),
        ],
        compiler_params=_params(("parallel", "arbitrary")),
    )(lam, qr, kr, v, states, do)


def _rope_tables(t, n_lat):
    nf = RET_KEY_DIM // 4
    tok = np.arange(n_lat)
    inv_freq = (ROPE_BASE ** (-np.arange(nf, dtype=np.float32) / nf)).astype(np.float32)
    row = (tok // GRID_W).astype(np.float32)
    col = (tok % GRID_W).astype(np.float32)
    ang = np.concatenate([row[:, None] * inv_freq, col[:, None] * inv_freq], axis=-1).astype(np.float32)
    cos = np.ones((t, 2 * nf), np.float32)
    sin = np.zeros((t, 2 * nf), np.float32)
    cos[:n_lat] = np.cos(ang)
    sin[:n_lat] = np.sin(ang)
    return jnp.asarray(np.concatenate([cos, cos], axis=1)), jnp.asarray(np.concatenate([-sin, sin], axis=1))


def _rotate(x, cos2, sin2):
    return x * cos2 + pltpu.roll(x, RET_KEY_DIM // 2, 1) * sin2


def _rope_fwd_call(u, qblk, kblk, w_qk, cos2, sin2, k_scale, tm):
    t = u.shape[0]
    nh = w_qk // RET_KEY_DIM

    def body(q_ref, k_ref, c_ref, s_ref, qr_ref, kr_ref):
        c, s = c_ref[...], s_ref[...]
        for hh in range(nh):
            sl = slice(hh * RET_KEY_DIM, (hh + 1) * RET_KEY_DIM)
            qr_ref[:, sl] = _rotate(q_ref[:, sl].astype(F32), c, s).astype(qr_ref.dtype)
            kr_ref[:, sl] = (_rotate(k_ref[:, sl].astype(F32), c, s) * k_scale).astype(kr_ref.dtype)

    tab = pl.BlockSpec((tm, RET_KEY_DIM), lambda i: (i, 0))
    out = pl.BlockSpec((tm, w_qk), lambda i: (i, 0))
    return pl.pallas_call(
        body, name="rope_fwd", grid=(t // tm,),
        in_specs=[pl.BlockSpec((tm, w_qk), lambda i: (i, qblk)), pl.BlockSpec((tm, w_qk), lambda i: (i, kblk)), tab, tab],
        out_specs=[out, out], out_shape=[jax.ShapeDtypeStruct((t, w_qk), BF16)] * 2,
        compiler_params=_params(("parallel",)),
    )(u, u, cos2, sin2)


def _assemble_du_call(pieces, off, cos2, sin2, k_scale, tm):
    flat = [a for p in pieces for a in (p if isinstance(p, tuple) else (p,))]
    t = flat[0].shape[0]
    n = len(flat)

    def body(*refs):
        c, s = refs[n][...], -refs[n + 1][...]
        o_ref = refs[n + 2]
        it = iter(refs[:n])
        for blk, p in enumerate(pieces):
            lo = off[blk]
            if not isinstance(p, tuple):
                o_ref[:, lo:off[blk + 1]] = next(it)[...].astype(o_ref.dtype)
                continue
            fwd_ref, rev_ref = next(it), next(it)
            if blk == 6:
                o_ref[:, lo:off[blk + 1]] = (fwd_ref[...].astype(F32) + rev_ref[...].astype(F32)).astype(o_ref.dtype)
                continue
            mult = k_scale if blk == 5 else 1.0
            for hh in range((off[blk + 1] - lo) // RET_KEY_DIM):
                sl = slice(hh * RET_KEY_DIM, (hh + 1) * RET_KEY_DIM)
                dy = (fwd_ref[:, sl].astype(F32) + rev_ref[:, sl].astype(F32)) * mult
                o_ref[:, lo + hh * RET_KEY_DIM:lo + (hh + 1) * RET_KEY_DIM] = _rotate(dy, c, s).astype(o_ref.dtype)

    tab = pl.BlockSpec((tm, RET_KEY_DIM), lambda i: (i, 0))
    return pl.pallas_call(
        body, name="assemble_du", grid=(t // tm,),
        in_specs=[pl.BlockSpec((tm, a.shape[1]), lambda i: (i, 0)) for a in flat] + [tab, tab],
        out_specs=pl.BlockSpec((tm, off[-1]), lambda i: (i, 0)),
        out_shape=jax.ShapeDtypeStruct((t, off[-1]), BF16),
        compiler_params=_params(("parallel",)),
    )(*flat, cos2, sin2)


def _my_position():
    return lax.axis_index("x"), lax.axis_index("y"), lax.axis_index("c")


def _flip(pos, k):
    x, y, c = pos
    return (1 - x if k & 4 else x, 1 - y if k & 2 else y, 1 - c if k & 1 else c)


def _linear(pos):
    return 4 * pos[0] + 2 * pos[1] + pos[2]


def _slab(ref, axis, idx, size):
    start = pl.multiple_of(idx * size, size)
    return ref.at[pl.ds(start, size), :] if axis == 0 else ref.at[:, pl.ds(start, size)]


HBM_SPEC = pl.BlockSpec(memory_space=pltpu.HBM)
SEM_SPEC = pl.BlockSpec(memory_space=pltpu.SEMAPHORE)
DATAFLOW = pltpu.SideEffectType.DATAFLOW_SIDE_EFFECTING
PEER_BITS = (1, 2, 4, 6, 3, 5, 7)
GATHER_BITS = (1, 2, 4, 6)


def _in_hbm(a):
    return pltpu.with_memory_space_constraint(a, pltpu.HBM)


def _gather_views(me, k, a, src_refs, land_refs, axes):
    size = src_refs[a].shape[axes[a]]
    peer = _flip(me, k)
    return src_refs[a], _slab(land_refs[a], axes[a], _linear(me), size), _slab(land_refs[a], axes[a], _linear(peer), size)


def _scatter_views(me, k, a, src_refs, land_refs, axes):
    size = land_refs[a].shape[1 + axes[a]]
    peer = _flip(me, k)
    return _slab(src_refs[a], axes[a], _linear(peer), size), land_refs[a].at[k - 1], land_refs[a].at[k - 1]


CHIP_BITS = (0, 2, 4, 6)


def _chip_views(me, k, a, src_refs, land_refs, axes):
    j = CHIP_BITS.index(k)
    return src_refs[a].at[j], land_refs[a].at[j - 1], land_refs[a].at[j - 1]


def _pair_exchange(grads, axes, sizes):
    ns = len(grads)

    def slab_shape(a):
        s = grads[a].shape
        return (sizes[a], s[1]) if axes[a] == 0 else (s[0], sizes[a])

    def body(*refs):
        g_refs, p_refs = refs[:ns], refs[ns:2 * ns]
        send_sems, recv_sems = refs[2 * ns:]
        me = _my_position()
        sibling = _flip(me, 1)
        copies = []
        for j, kc in enumerate(CHIP_BITS):
            for a in range(ns):
                cp = pltpu.make_async_remote_copy(
                    src_ref=_slab(g_refs[a], axes[a], _linear(_flip(me, kc | 1)), sizes[a]), dst_ref=p_refs[a].at[j],
                    send_sem=send_sems.at[4 * a + j], recv_sem=recv_sems.at[4 * a + j],
                    device_id=sibling, device_id_type=MESH)
                cp.start()
                copies.append(cp)
        for cp in copies:
            cp.wait_recv()
        for cp in copies:
            cp.wait_send()

    return pl.pallas_call(
        body, name="scatter_pair_exchange", in_specs=[ANY] * ns, out_specs=[ANY] * ns,
        out_shape=[jax.ShapeDtypeStruct((4,) + slab_shape(a), grads[a].dtype) for a in range(ns)],
        scratch_shapes=[pltpu.SemaphoreType.DMA((4 * ns,)), pltpu.SemaphoreType.DMA((4 * ns,))],
        compiler_params=pltpu.CompilerParams(has_side_effects=True),
    )(*grads)


def _pair_add(grad, theirs, axis, chip_idx):
    _, r, c = theirs.shape
    tm = _pick(r, (256, 128, 64, 32, 16))
    if axis == 0:
        mine_spec = pl.BlockSpec((tm, c), lambda j, i, idx: (idx[j] * (r // tm) + i, 0))
    else:
        mine_spec = pl.BlockSpec((tm, c), lambda j, i, idx: (i, idx[j]))

    def body(idx_ref, mine_ref, theirs_ref, o_ref):
        o_ref[0] = (mine_ref[...].astype(F32) + theirs_ref[0].astype(F32)).astype(o_ref.dtype)

    spec = pl.BlockSpec((1, tm, c), lambda j, i, idx: (j, i, 0))
    return pl.pallas_call(
        body, name="scatter_pair_add",
        grid_spec=pltpu.PrefetchScalarGridSpec(
            num_scalar_prefetch=1, grid=(4, r // tm), in_specs=[mine_spec, spec], out_specs=spec),
        out_shape=jax.ShapeDtypeStruct(theirs.shape, theirs.dtype),
        compiler_params=_params(("parallel", "parallel")),
    )(chip_idx, grad, theirs)


def _slab_block(rows, cols, tm, axis):
    if axis == 0:
        return pl.BlockSpec((tm, cols), lambda i, idx: (idx[0] * (rows // tm) + i, 0))
    return pl.BlockSpec((tm, cols), lambda i, idx: (i, idx[0]))


def _place_shard(shard, land, axis, my_idx):
    r, c = shard.shape
    tm = _pick(r, (512, 256, 128, 64, 32, 16))

    def body(idx_ref, s_ref, land_ref, o_ref):
        o_ref[...] = s_ref[...]

    return pl.pallas_call(
        body, name="gather_place",
        grid_spec=pltpu.PrefetchScalarGridSpec(
            num_scalar_prefetch=1, grid=(r // tm,),
            in_specs=[pl.BlockSpec((tm, c), lambda i, idx: (i, 0)), ANY],
            out_specs=_slab_block(r, c, tm, axis)),
        out_shape=jax.ShapeDtypeStruct(land.shape, land.dtype),
        input_output_aliases={2: 0},
        compiler_params=_params(("parallel",)),
    )(my_idx, shard, land)


def _push_start(name, srcs, lands, axes, views, bits, deps):
    ns = len(srcs)

    def body(*refs):
        src_refs, land_refs = refs[:ns], refs[ns:2 * ns]
        send_sems, recv_sems = refs[2 * ns + len(deps):2 * ns + len(deps) + 2]
        token = refs[-1]
        me = _my_position()
        for k in bits:
            for a in range(ns):
                s, d, _ = views(me, k, a, src_refs, land_refs, axes)
                pltpu.make_async_remote_copy(
                    src_ref=s, dst_ref=d, send_sem=send_sems.at[7 * a + k - 1], recv_sem=recv_sems.at[7 * a + k - 1],
                    device_id=_flip(me, k), device_id_type=MESH).start()
        token[...] = jnp.zeros_like(token)

    thru = [pltpu.HBM(a.shape, a.dtype) for a in list(srcs) + list(lands)]
    outs = pl.pallas_call(
        body, name=name,
        in_specs=[HBM_SPEC] * (2 * ns) + [ANY] * len(deps),
        out_specs=[SEM_SPEC, SEM_SPEC] + [HBM_SPEC] * (2 * ns) + [VMEM_SPEC],
        out_shape=[pltpu.SemaphoreType.DMA((7 * ns,)), pltpu.SemaphoreType.DMA((7 * ns,))] + thru
        + [jax.ShapeDtypeStruct((8, 128), F32)],
        input_output_aliases={i: 2 + i for i in range(2 * ns)},
        compiler_params=pltpu.CompilerParams(has_side_effects=DATAFLOW),
    )(*[_in_hbm(a) for a in srcs], *[_in_hbm(a) for a in lands], *deps)
    return (outs[0], outs[1]), outs[2:2 + ns], outs[2 + ns:2 + 2 * ns], outs[-1]


def _gather_finish(lands, axes, sizes):
    ns = len(lands)
    chips = (2, 4, 6)

    def body(*refs):
        land_refs = refs[ns:2 * ns]
        send_sems, recv_sems = refs[2 * ns:]
        me = _my_position()
        sibling = _flip(me, 1)
        copies = []
        for j, kc in enumerate(chips):
            for a in range(ns):
                def slab_of(pos):
                    return _slab(land_refs[a], axes[a], _linear(pos), sizes[a])
                send = pltpu.make_async_remote_copy(
                    src_ref=slab_of(_flip(me, kc)), dst_ref=slab_of(_flip(me, kc)), send_sem=send_sems.at[3 * a + j],
                    recv_sem=recv_sems.at[3 * a + j], device_id=sibling, device_id_type=MESH)
                recv = pltpu.make_async_remote_copy(
                    src_ref=slab_of(_flip(me, kc)), dst_ref=slab_of(_flip(sibling, kc)), send_sem=send_sems.at[3 * a + j],
                    recv_sem=recv_sems.at[3 * a + j], device_id=sibling, device_id_type=MESH)
                send.start()
                copies.append((send, recv))
        for send, recv in copies:
            recv.wait_recv()
        for send, recv in copies:
            send.wait_send()

    return pl.pallas_call(
        body, name="gather_finish", in_specs=[ANY] * ns, out_specs=[ANY] * ns,
        out_shape=[jax.ShapeDtypeStruct(l.shape, l.dtype) for l in lands],
        input_output_aliases={a: a for a in range(ns)},
        scratch_shapes=[pltpu.SemaphoreType.DMA((3 * ns,)), pltpu.SemaphoreType.DMA((3 * ns,))],
        compiler_params=pltpu.CompilerParams(has_side_effects=True),
    )(*lands)


def _push_wait(name, sems, srcs, lands, axes, views, bits, after):
    ns = len(srcs)

    def body(*refs):
        src_refs, land_refs = refs[:ns], refs[ns:2 * ns]
        send_sems, recv_sems = refs[2 * ns:2 * ns + 2]
        me = _my_position()
        for k in bits:
            for a in range(ns):
                s, d, got = views(me, k, a, src_refs, land_refs, axes)
                cp = pltpu.make_async_remote_copy(
                    src_ref=s, dst_ref=got, send_sem=send_sems.at[7 * a + k - 1], recv_sem=recv_sems.at[7 * a + k - 1],
                    device_id=_flip(me, k), device_id_type=MESH)
                cp.wait_send()
                cp.wait_recv()

    thru = [pltpu.HBM(a.shape, a.dtype) for a in list(srcs) + list(lands)]
    outs = pl.pallas_call(
        body, name=name,
        in_specs=[HBM_SPEC] * (2 * ns) + [SEM_SPEC, SEM_SPEC] + [ANY] * len(after),
        out_specs=[HBM_SPEC] * (2 * ns),
        out_shape=thru,
        input_output_aliases={i: i for i in range(2 * ns)},
        compiler_params=pltpu.CompilerParams(has_side_effects=DATAFLOW),
    )(*srcs, *lands, sems[0], sems[1], *after)
    return outs[:ns], outs[ns:]


def _small_allgather(v, name):
    r, c = v.shape

    def body(v_ref, all_ref, sum_ref, send_sems, recv_sems):
        me = _my_position()
        all_ref[_linear(me)] = v_ref[...]
        copies = []
        for k in range(1, N_DEV):
            peer = _flip(me, k)
            copies.append(pltpu.make_async_remote_copy(
                src_ref=v_ref, dst_ref=all_ref.at[_linear(me)], send_sem=send_sems.at[k - 1], recv_sem=recv_sems.at[k - 1],
                device_id=peer, device_id_type=MESH))
        for cp in copies:
            cp.start()
        for k in range(1, N_DEV):
            peer = _flip(me, k)
            pltpu.make_async_remote_copy(
                src_ref=v_ref, dst_ref=all_ref.at[_linear(peer)], send_sem=send_sems.at[k - 1], recv_sem=recv_sems.at[k - 1],
                device_id=peer, device_id_type=MESH).wait_recv()
        for cp in copies:
            cp.wait_send()
        acc = all_ref[0]
        for d in range(1, N_DEV):
            acc = acc + all_ref[d]
        sum_ref[...] = acc

    return pl.pallas_call(
        body,
        name=name,
        in_specs=[VMEM_SPEC],
        out_specs=[VMEM_SPEC, VMEM_SPEC],
        out_shape=[jax.ShapeDtypeStruct((N_DEV, r, c), F32), jax.ShapeDtypeStruct((r, c), F32)],
        scratch_shapes=[pltpu.SemaphoreType.DMA((N_DEV - 1,)), pltpu.SemaphoreType.DMA((N_DEV - 1,))],
        compiler_params=pltpu.CompilerParams(has_side_effects=True, vmem_limit_bytes=VMEM_LIMIT),
    )(v)


def _ada_fwd_call(cin, ada_w, ada_b_cols):
    nl, d, ncol = ada_w.shape
    nrow = cin.shape[0]

    def body(c_ref, w_ref, b_ref, o_ref):
        cs = _silu(c_ref[...]).astype(BF16)
        for l in range(nl):
            o_ref[l] = _dot(cs, w_ref[l].astype(BF16)) + b_ref[l]

    return pl.pallas_call(
        body, name="ada_fwd", in_specs=[VMEM_SPEC] * 3, out_specs=VMEM_SPEC,
        out_shape=jax.ShapeDtypeStruct((nl, nrow, ncol), F32), compiler_params=_params(),
    )(cin, ada_w, ada_b_cols)


def _ada_bwd_call(cin, ada_w, dmod):
    nl, d, ncol = ada_w.shape
    nrow = cin.shape[0]

    def body(c_ref, w_ref, dm_ref, gw_ref, dcs_ref):
        cs = _silu(c_ref[...]).astype(BF16)
        acc = jnp.zeros((nrow, d), F32)
        for l in range(nl):
            dm = dm_ref[l].astype(BF16)
            gw_ref[l] = _dot_tn(cs, dm)
            acc = acc + _dot_nt(dm, w_ref[l].astype(BF16))
        dcs_ref[...] = acc

    return pl.pallas_call(
        body, name="ada_bwd", in_specs=[VMEM_SPEC] * 3, out_specs=[VMEM_SPEC, VMEM_SPEC],
        out_shape=[jax.ShapeDtypeStruct((nl, d, ncol), F32), jax.ShapeDtypeStruct((nrow, d), F32)],
        compiler_params=_params(),
    )(cin, ada_w, dmod)


def _adamw_math(w, g, m, v):
    m = ADAM_B1 * m + (1.0 - ADAM_B1) * g
    v = ADAM_B2 * v + (1.0 - ADAM_B2) * jnp.square(g)
    m_hat = m / (1.0 - ADAM_B1 ** ADAM_STEP)
    v_hat = v / (1.0 - ADAM_B2 ** ADAM_STEP)
    delta = -ADAM_LR * (m_hat / (jnp.sqrt(v_hat) + ADAM_EPS) + ADAM_WD * w)
    return delta, m, v


def _adamw_sharded(w, m, v, mine, slabs, axis, my_idx, layer, prev, name):
    nl, r, c = w.shape
    tm = _pick(r, (128, 64, 32, 16))
    nprev = 0 if prev is None else len(prev)
    nslab = slabs.shape[0]
    if axis is None:
        mine_spec = pl.BlockSpec((1, tm, c), lambda i, idx: (0, i, 0))
    else:
        mine_spec = _slab_block(r, c, tm, axis)

    def body(idx_ref, w_ref, m_ref, v_ref, mine_ref, s_ref, *rest):
        g_ref, d_ref, nm_ref, nv_ref = rest[nprev:]
        g = (mine_ref[0] if axis is None else mine_ref[...]).astype(F32)
        for k in range(nslab):
            g = g + s_ref[k].astype(F32)
        delta, nm, nv = _adamw_math(w_ref[0], g, m_ref[0], v_ref[0])
        g_ref[0], d_ref[0], nm_ref[0], nv_ref[0] = g, delta, nm, nv

    spec = pl.BlockSpec((1, tm, c), lambda i, idx: (layer, i, 0))
    out = jax.ShapeDtypeStruct(w.shape, F32)
    return pl.pallas_call(
        body, name=name,
        grid_spec=pltpu.PrefetchScalarGridSpec(
            num_scalar_prefetch=1, grid=(r // tm,),
            in_specs=[spec, spec, spec, mine_spec,
                      pl.BlockSpec((nslab, tm, c), lambda i, idx: (0, i, 0))] + [ANY] * nprev,
            out_specs=[spec] * 4),
        out_shape=[out] * 4,
        input_output_aliases={6 + j: j for j in range(nprev)},
        compiler_params=_params(("parallel",)),
    )(my_idx, w, m, v, mine, slabs, *(() if prev is None else prev))


def _adamw_dense(w, g, m, v, name):
    r, c = w.shape
    tm = _pick(r, (256, 128, 64, 32, 16, 8))

    def body(w_ref, g_ref, m_ref, v_ref, d_ref, nm_ref, nv_ref):
        d_ref[...], nm_ref[...], nv_ref[...] = _adamw_math(w_ref[...], g_ref[...], m_ref[...], v_ref[...])

    spec = pl.BlockSpec((tm, c), lambda i: (i, 0))
    out = jax.ShapeDtypeStruct(w.shape, F32)
    return pl.pallas_call(
        body, name=name, grid=(r // tm,), in_specs=[spec] * 4, out_specs=[spec] * 3, out_shape=[out] * 3,
        compiler_params=_params(("parallel",)),
    )(w, g, m, v)


def _pack(parts, width=128):
    flat = jnp.concatenate([p.reshape(-1).astype(F32) for p in parts])
    n = flat.shape[0]
    total = -(-n // (8 * width)) * (8 * width)
    return jnp.pad(flat, (0, total - n)).reshape(total // width, width)


def _unpack(buf, shapes):
    flat = buf.reshape(-1)
    out, off = [], 0
    for s in shapes:
        n = int(np.prod(s))
        out.append(flat[off:off + n].reshape(s))
        off += n
    return out


def kernel(x, c, ctx, c_ctx, ada_w, ada_b, norm_g, w_in, na_rpb, ret_decay_logit, w_proj_na, w_proj_ret, w_out, final_g, loss_target, m_c_ctx, m_ada_w, m_ada_b, m_norm_g, m_w_in, m_na_rpb, m_ret_decay_logit, m_w_proj_na, m_w_proj_ret, m_w_out, m_final_g, v_c_ctx, v_ada_w, v_ada_b, v_norm_g, v_w_in, v_na_rpb, v_ret_decay_logit, v_w_proj_na, v_w_proj_ret, v_w_out, v_final_g):
    depth = w_in.shape[0]
    n_lat, d = x.shape[1], x.shape[2]
    n_ctx = ctx.shape[1]
    t = n_lat + n_ctx
    w_na = w_proj_na.shape[1]
    w_retv = w_proj_ret.shape[1] * N_DEV
    in_cols = w_in.shape[2] * N_DEV
    w_qk = (in_cols - 4 * w_na - 2 * w_retv - 2 * d) // 2
    sizes = (w_na, w_na, w_na, w_na, w_qk, w_qk, w_retv, w_retv, d, d)
    off = tuple(int(o) for o in np.cumsum((0,) + sizes))
    NA_Q, NA_K, NA_V, NA_Z, RET_Q, RET_K, RET_V, RET_Z, G_NA, G_RET = range(10)
    rows = n_lat // GRID_W
    me = _my_position()
    my_idx = _linear(me)
    tm_row = _pick(n_ctx, (256, 128))

    idx_arr = jnp.reshape(my_idx, (1,)).astype(jnp.int32)
    chip_idx = jnp.stack([_linear(_flip(me, kc)) for kc in CHIP_BITS]).astype(jnp.int32)

    w_axes = (1, 1, 0, 0)
    w_names = ("w_in", "w_proj_na", "w_proj_ret", "w_out")
    shard = [[w[l].astype(BF16) for w in (w_in, w_proj_na, w_proj_ret, w_out)] for l in range(depth)]
    groups = [[(0, 0)], [(0, 1), (0, 2), (0, 3)]] + [[(l, a) for a in range(4)] for l in range(1, depth)]
    gathers = {}

    def start_gather(gi, deps):
        keys = groups[gi]
        srcs = [shard[l][a] for l, a in keys]
        axes = tuple(w_axes[a] for _, a in keys)
        lands = [_place_shard(s, lax.empty(tuple(n * (N_DEV if i == ax else 1) for i, n in enumerate(s.shape)), BF16),
                              ax, idx_arr) for s, ax in zip(srcs, axes)]
        sizes = tuple(s.shape[ax] for s, ax in zip(srcs, axes))
        sems, srcs, lands, tok = _push_start(f"gather_start_{gi}", srcs, lands, axes, _gather_views, GATHER_BITS, deps)
        flight = dict(name=f"gather_wait_{gi}", sems=sems, srcs=srcs, lands=lands, axes=axes, sizes=sizes, ready=None)
        for pos, key in enumerate(keys):
            gathers[key] = (flight, pos)
        return tok

    token = start_gather(0, ())

    ncol = ada_w.shape[2]
    c_all, _ = _small_allgather(jnp.pad(c, ((0, 7), (0, 0))) + token[:, :1], "allgather_c")
    cin = jnp.concatenate([c_all[:, 0, :], c_ctx[None, :], jnp.zeros((7, d), F32)], axis=0)
    ada_b_cols = lax.dynamic_slice_in_dim(ada_b, my_idx * ncol, ncol, axis=1)[:, None, :]
    mod_cols = _ada_fwd_call(cin, ada_w, ada_b_cols)
    mod_gathered, _ = _small_allgather(mod_cols.reshape(depth * 16, ncol), "allgather_mod")
    mod_all = mod_gathered.reshape(N_DEV, depth, 16, ncol).transpose(1, 2, 0, 3).reshape(depth, 16, N_DEV * ncol)
    mod_lat = lax.dynamic_index_in_dim(mod_all, my_idx, axis=1, keepdims=False)
    mod_ctx = mod_all[:, 8, :]
    token = mod_gathered
    for gi in range(1, len(groups)):
        token = start_gather(gi, (token,))

    def landed(l, a, act):
        flight, pos = gathers[(l, a)]
        if flight["ready"] is None:
            arrived = _push_wait(flight["name"], flight["sems"], flight["srcs"], flight["lands"],
                                 flight["axes"], _gather_views, GATHER_BITS, (act, token))[1]
            flight["ready"] = _gather_finish(arrived, flight["axes"], flight["sizes"])
        return flight["ready"][pos]

    pending, scatters = {}, []

    def send_dw(l, a, dw):
        pending[(l, a)] = dw
        if a == 0:
            keys = [(0, 0)] if l == 0 else [(l, b) for b in range(4)]
        elif l == 0 and a == 1:
            keys = [(0, 1), (0, 2), (0, 3)]
        else:
            return None
        srcs = [pending[k] for k in keys]
        axes = tuple(w_axes[b] for _, b in keys)
        sizes = tuple(s.shape[ax] // N_DEV for s, ax in zip(srcs, axes))
        slab_shapes = [tuple(n // (N_DEV if i == ax else 1) for i, n in enumerate(s.shape)) for s, ax in zip(srcs, axes)]
        by_chip = keys == [(0, 0)]
        if by_chip:
            theirs = _pair_exchange(srcs, axes, sizes)
            srcs = [_pair_add(g, p, ax, chip_idx) for g, p, ax in zip(srcs, theirs, axes)]
            lands = [lax.empty((3,) + shp, BF16) for shp in slab_shapes]
            views, bits = _chip_views, CHIP_BITS[1:]
        else:
            lands = [lax.empty((N_DEV - 1,) + shp, BF16) for shp in slab_shapes]
            views, bits = _scatter_views, PEER_BITS
        sems, srcs, lands, tok = _push_start(f"scatter_start_{len(scatters)}", srcs, lands, axes, views, bits, ())
        scatters.append(dict(name=f"scatter_wait_{len(scatters)}", sems=sems, srcs=srcs, lands=lands, axes=axes, keys=keys,
                             views=views, bits=bits, by_chip=by_chip))
        return tok

    cos2, sin2 = _rope_tables(t, n_lat)
    k_scale = RET_KEY_DIM ** -0.5
    assert off[RET_Q] % w_qk == 0 and off[RET_K] % w_qk == 0
    assert off[NA_Z] % w_na == 0 and off[G_NA] % d == 0 and off[G_RET] % d == 0 and off[RET_Z] % w_retv == 0
    assert off[RET_V] % (_ret_head_group(w_retv // RET_VAL_DIM) * RET_VAL_DIM) == 0
    na_cols = tuple(off[i] // NA_HEAD_DIM for i in (NA_Q, NA_K, NA_V))
    norm_mod_fwd, norm_mod_bwd = _make_rowwise(_f_norm_mod, "norm_mod", (BF16,), (d,), n_lat, tm_row, (0,))
    gate_na_fwd, gate_na_bwd = _make_rowwise(_f_gate_na, "gate_na", (BF16,), (w_na,), n_lat, tm_row, (0, 1),
                                             col_blocks={1: (w_na, off[NA_Z] // w_na)})
    merge_fwd, merge_bwd = _make_rowwise(_f_merge, "merge", (BF16,), (d,), n_lat, tm_row, (0, 1, 2, 3),
                                         col_blocks={0: (d, off[G_NA] // d), 1: (d, off[G_RET] // d)},
                                         drow_dtypes={2: BF16, 3: BF16})
    residual_fwd, _ = _make_rowwise(_f_residual, "residual", (F32,), (d,), n_lat, tm_row, (0, 1))
    _, residual_bwd = _make_rowwise(lambda out, gate: (gate * out,), "residual", (F32,), (d,), n_lat, tm_row, (0,),
                                    drow_dtypes={0: BF16})
    loss_fwd, loss_bwd = _make_rowwise(_f_loss, "loss_head", (F32,), (128,), n_lat, tm_row, (0,))

    def pair(a, b):
        return jnp.stack([a, b])[:, None, :]

    def mod_vectors(mod_lat_l, mod_ctx_l, norm_g_l):
        shift, scale, gate = jnp.split(mod_lat_l, 3)
        c_shift, c_scale, c_gate = jnp.split(mod_ctx_l, 3)
        return pair(norm_g_l, norm_g_l), pair(scale, c_scale), pair(shift, c_shift), pair(gate, c_gate)

    def log_decay(logit):
        return jax.nn.log_sigmoid(logit.astype(F32))

    xa = jnp.concatenate([x[0], ctx[0]], axis=0)
    saved = []
    for l in range(depth):
        vecs, vecs_vjp = jax.vjp(mod_vectors, mod_lat[l], mod_ctx[l], norm_g[l])
        (h,) = norm_mod_fwd((xa,), vecs[:3])
        wl_in = landed(l, 0, h)
        u = _matmul(h, wl_in, out_dtype=BF16, name="in_proj_fwd")
        qr, kr = _rope_fwd_call(u, off[RET_Q] // w_qk, off[RET_K] // w_qk, w_qk, cos2, sin2, k_scale, tm_row)
        bt, bt_vjp = jax.vjp(lambda r: _na_bias_table(r, rows), na_rpb[l])
        lam, lam_vjp = jax.vjp(log_decay, ret_decay_logit[l])
        o_na = _na_fwd_call(u, u, u, na_cols, w_na, bt, n_lat)
        o_f, st_f = _ret_fwd_call(qr, kr, u, off[RET_V], lam[0], n_lat, False)
        o_b, st_b = _ret_fwd_call(qr, kr, u, off[RET_V], lam[1], n_lat, True)
        (a_na,) = gate_na_fwd((o_na, u), ())
        a_ret = _gate_ret_fwd_call(o_f, o_b, u, off[RET_Z] // w_retv, tm_row)
        wl_pna, wl_pret, wl_out = landed(l, 1, a_na), landed(l, 2, a_na), landed(l, 3, a_na)
        y_na = _matmul(a_na, wl_pna, out_dtype=BF16, name="proj_na_fwd")
        y_ret = _matmul(a_ret, wl_pret, out_dtype=BF16, name="proj_ret_fwd")
        (merged,) = merge_fwd((u, u, y_na, y_ret), ())
        out = _matmul(merged, wl_out, out_dtype=F32, name="out_proj_fwd")
        (xa_next,) = residual_fwd((xa, out), vecs[3:])
        saved.append(dict(xa=xa, vecs=vecs, vecs_vjp=vecs_vjp, h=h, w=(wl_in, wl_pna, wl_pret, wl_out), u=u, qr=qr, kr=kr,
                          bt=bt, bt_vjp=bt_vjp, lam=lam, lam_vjp=lam_vjp, o_na=o_na, o_f=o_f,
                          o_b=o_b, st_f=st_f, st_b=st_b, a_na=a_na, a_ret=a_ret, y_na=y_na, y_ret=y_ret,
                          merged=merged, out=out))
        xa = xa_next

    fg_pair, fg_vjp = jax.vjp(lambda g: pair(g, g), final_g)
    (loss_rows,) = loss_fwd((xa, loss_target[0]), (fg_pair,))
    loss = lax.psum(jnp.sum(loss_rows), ("x", "y", "c"))
    (dx_last,), (d_fg_pair,) = loss_bwd((xa, loss_target[0]), (fg_pair,), (jnp.ones_like(loss_rows),))
    (d_final_g,) = fg_vjp(d_fg_pair)
    dxa = jnp.pad(dx_last, ((0, n_ctx), (0, 0)))

    d_mod_lat, d_mod_ctx, d_norm_g, d_rpb, d_decay = ([None] * depth for _ in range(5))
    for l in reversed(range(depth)):
        s = saved[l]
        u, qr, kr = s["u"], s["qr"], s["kr"]
        wl_in, wl_pna, wl_pret, wl_out = s["w"]
        (d_out,), (d_gate,) = residual_bwd((s["out"],), s["vecs"][3:], (dxa,))
        dxa_res = dxa
        send_dw(l, 3, _matmul(s["merged"], d_out, trans_a=True, out_dtype=BF16, name="out_proj_dw"))
        d_merged = _matmul(d_out, wl_out, trans_b=True, out_dtype=BF16, name="out_proj_da")
        (dg_na, dg_ret, dy_na, dy_ret), _ = merge_bwd((u, u, s["y_na"], s["y_ret"]), (), (d_merged,))
        send_dw(l, 2, _matmul(s["a_ret"], dy_ret, trans_a=True, out_dtype=BF16, name="proj_ret_dw"))
        da_ret = _matmul(dy_ret, wl_pret, trans_b=True, out_dtype=BF16, name="proj_ret_da")
        tok = send_dw(l, 1, _matmul(s["a_na"], dy_na, trans_a=True, out_dtype=BF16, name="proj_na_dw"))
        da_na = _matmul(dy_na, wl_pna, trans_b=True, out_dtype=BF16, name="proj_na_da", after=tok)
        do_ret, dz_ret = _gate_ret_bwd_call(s["o_f"], s["o_b"], u, off[RET_Z] // w_retv, da_ret, tm_row)
        (do_na, dz_na), _ = gate_na_bwd((s["o_na"], u), (), (da_na,))
        dq_f, dk_f, dv_f, dl_f = _ret_bwd_call(qr, kr, u, off[RET_V], s["lam"][0], s["st_f"], do_ret, n_lat, False)
        dq_b, dk_b, dv_b, dl_b = _ret_bwd_call(qr, kr, u, off[RET_V], s["lam"][1], s["st_b"], do_ret, n_lat, True)
        dq, dk, dv, dbt = _na_bwd_call(u, u, u, na_cols, w_na, s["bt"], do_na, n_lat)
        du = _assemble_du_call([dq, dk, dv, dz_na, (dq_f, dq_b), (dk_f, dk_b), (dv_f, dv_b), dz_ret, dg_na, dg_ret],
                               off, cos2, sin2, k_scale, _pick(n_ctx, (128,)))
        (d_rpb[l],) = s["bt_vjp"](dbt)
        (d_decay[l],) = s["lam_vjp"](jnp.stack([dl_f[:, 0, 0], dl_b[:, 0, 0]]))
        tok = send_dw(l, 0, _matmul(s["h"], du, trans_a=True, out_dtype=BF16, name="in_proj_dw"))
        dh = _matmul(du, wl_in, trans_b=True, out_dtype=BF16, name="in_proj_da", after=tok)
        (dxa,), d_vecs = norm_mod_bwd((s["xa"],), s["vecs"][:3], (dh,), acc=(dxa_res,))
        d_mod_lat[l], d_mod_ctx[l], d_norm_g[l] = s["vecs_vjp"](tuple(d_vecs) + (d_gate,))
    gx = dxa[:n_lat]
    d_mod_lat, d_mod_ctx, d_norm_g, d_rpb, d_decay = (jnp.stack(a) for a in (d_mod_lat, d_mod_ctx, d_norm_g, d_rpb, d_decay))

    small_shapes = [d_mod_lat.shape, d_mod_ctx.shape, d_norm_g.shape, d_final_g.shape, d_rpb.shape, d_decay.shape]
    packed = _pack([d_mod_lat, d_mod_ctx, d_norm_g, d_final_g, d_rpb, d_decay])
    g_all, g_sum = _small_allgather(packed, "allgather_small_grads")
    dml_sum, dmc_sum, grad_norm_g, grad_final_g, grad_na_rpb, grad_decay = _unpack(g_sum, small_shapes)
    grad_ada_b = dml_sum + dmc_sum
    dml_all = g_all.reshape(N_DEV, -1)[:, :depth * 3 * d].reshape(N_DEV, depth, 3 * d)

    def my_cols(a):
        return lax.dynamic_slice_in_dim(a, my_idx * ncol, ncol, axis=a.ndim - 1)

    dmod = jnp.concatenate(
        [my_cols(dml_all).transpose(1, 0, 2), my_cols(dmc_sum)[:, None, :], jnp.zeros((depth, 7, ncol), F32)], axis=1)
    grad_ada_w, dcs_part = _ada_bwd_call(cin, ada_w, dmod)
    _, dcs = _small_allgather(dcs_part, "allgather_dcsilu")
    sg = jax.nn.sigmoid(c_ctx)
    grad_c_ctx = dcs[8] * (sg * (1.0 + c_ctx * (1.0 - sg)))

    def flat2(a):
        return a.reshape(a.shape[0] * a.shape[1], a.shape[2])

    small_w = [c_ctx, ada_b, norm_g, na_rpb, ret_decay_logit, final_g]
    small_g = [grad_c_ctx, grad_ada_b, grad_norm_g, grad_na_rpb, grad_decay, grad_final_g]
    small_m = [m_c_ctx, m_ada_b, m_norm_g, m_na_rpb, m_ret_decay_logit, m_final_g]
    small_v = [v_c_ctx, v_ada_b, v_norm_g, v_na_rpb, v_ret_decay_logit, v_final_g]
    shp = [a.shape for a in small_w]
    ds_, nms_, nvs_ = _adamw_dense(_pack(small_w), _pack(small_g), _pack(small_m), _pack(small_v), "adamw_small")
    ds_, nms_, nvs_ = _unpack(ds_, shp), _unpack(nms_, shp), _unpack(nvs_, shp)

    d_ada, nm_ada, nv_ada = [a.reshape(ada_w.shape) for a in _adamw_dense(
        flat2(ada_w), flat2(grad_ada_w), flat2(m_ada_w), flat2(v_ada_w), "adamw_ada_w")]

    w_all = (w_in, w_proj_na, w_proj_ret, w_out)
    m_all = (m_w_in, m_w_proj_na, m_w_proj_ret, m_w_out)
    v_all = (v_w_in, v_w_proj_na, v_w_proj_ret, v_w_out)
    upd = [None] * 4
    after = d_ada
    for flight in scatters:
        mine, slabs = _push_wait(flight["name"], flight["sems"], flight["srcs"], flight["lands"], flight["axes"],
                                 flight["views"], flight["bits"], (after,))
        for (l, a), own, s in zip(flight["keys"], mine, slabs):
            upd[a] = _adamw_sharded(w_all[a], m_all[a], v_all[a], own, s, None if flight["by_chip"] else w_axes[a],
                                    idx_arr, l, upd[a], "adamw_" + w_names[a])
            after = upd[a][1]
    (g_w_in, d_w_in, nm_w_in, nv_w_in), (g_pna, d_pna, nm_pna, nv_pna) = upd[0], upd[1]
    (g_pret, d_pret, nm_pret, nv_pret), (g_out, d_out, nm_out, nv_out) = upd[2], upd[3]

    def order(cc, aw, ab, ng, wi, rp, dl, pn, pr, wo, fg):
        return [cc, aw, ab, ng, wi, rp, dl, pn, pr, wo, fg]

    grads_out = order(grad_c_ctx, grad_ada_w, grad_ada_b, grad_norm_g, g_w_in, grad_na_rpb, grad_decay, g_pna, g_pret, g_out, grad_final_g)
    delta_out = order(ds_[0], d_ada, ds_[1], ds_[2], d_w_in, ds_[3], ds_[4], d_pna, d_pret, d_out, ds_[5])
    m_out = order(nms_[0], nm_ada, nms_[1], nms_[2], nm_w_in, nms_[3], nms_[4], nm_pna, nm_pret, nm_out, nms_[5])
    v_out = order(nvs_[0], nv_ada, nvs_[1], nvs_[2], nv_w_in, nvs_[3], nvs_[4], nv_pna, nv_pret, nv_out, nvs_[5])
    return (loss, gx[None], *grads_out, *delta_out, *m_out, *v_out)
```

```python
import numpy as np
import jax
import jax.numpy as jnp
from jax import lax
from jax.experimental import pallas as pl
from jax.experimental.pallas import tpu as pltpu

F32 = jnp.float32
BF16 = jnp.bfloat16

N_DEV = 8
GRID_W = 64
NA_HEAD_DIM = 128
NA_WIN_ROWS = 8
NA_WIN_COLS = 16
RET_KEY_DIM = 128
RET_VAL_DIM = 256
RET_CHUNK = 128
ROPE_BASE = 10000.0
NORM_EPS = 1e-6
MASK_VALUE = -1e30

ADAM_LR = 0.001
ADAM_B1 = 0.9
ADAM_B2 = 0.999
ADAM_EPS = 1e-08
ADAM_WD = 0.01
ADAM_STEP = 10

VMEM_LIMIT = 48 * 1024 * 1024
MESH = pl.DeviceIdType.MESH
ANY = pl.BlockSpec(memory_space=pl.ANY)
VMEM_SPEC = pl.BlockSpec(memory_space=pltpu.VMEM)


def _params(sem=None):
    return pltpu.CompilerParams(dimension_semantics=sem, vmem_limit_bytes=VMEM_LIMIT)


def _pick(n, prefs):
    for p in prefs:
        if n % p == 0:
            return p
    return n


def _dot(a, b):
    return lax.dot_general(a, b, (((1,), (0,)), ((), ())), preferred_element_type=F32)


def _dot_nt(a, b):
    return lax.dot_general(a, b, (((1,), (1,)), ((), ())), preferred_element_type=F32)


def _dot_tn(a, b):
    return lax.dot_general(a, b, (((0,), (0,)), ((), ())), preferred_element_type=F32)


def _silu(x):
    return x * jax.nn.sigmoid(x)


def _matmul(a, b, *, trans_a=False, trans_b=False, out_dtype=F32, name="matmul", after=None):
    if trans_a:
        kdim, m = a.shape
    else:
        m, kdim = a.shape
    if trans_b:
        n, kb = b.shape
    else:
        kb, n = b.shape
    assert kdim == kb, (a.shape, b.shape, trans_a, trans_b)
    tm = _pick(m, (1152, 1024, 768, 512, 256, 128))
    tn = _pick(n, (1024, 512, 256, 128) if trans_b else (512, 256, 128))
    tk = _pick(kdim, (2304, 2048, 1024, 512, 256, 128))
    nk = kdim // tk
    dn = (((0 if trans_a else 1,), (1 if trans_b else 0,)), ((), ()))

    def body(a_ref, b_ref, *rest):
        o_ref, acc_ref = rest[-2:]
        part = lax.dot_general(a_ref[...], b_ref[...], dn, preferred_element_type=F32)
        if nk == 1:
            o_ref[...] = part.astype(o_ref.dtype)
        else:
            k = pl.program_id(2)

            @pl.when(k == 0)
            def _():
                acc_ref[...] = part

            @pl.when(k > 0)
            def _():
                acc_ref[...] += part

            @pl.when(k == nk - 1)
            def _():
                o_ref[...] = acc_ref[...].astype(o_ref.dtype)

    a_spec = pl.BlockSpec((tk, tm), lambda i, j, k: (k, i)) if trans_a else pl.BlockSpec((tm, tk), lambda i, j, k: (i, k))
    b_spec = pl.BlockSpec((tn, tk), lambda i, j, k: (j, k)) if trans_b else pl.BlockSpec((tk, tn), lambda i, j, k: (k, j))
    return pl.pallas_call(
        body,
        name=name,
        grid=(m // tm, n // tn, nk),
        in_specs=[a_spec, b_spec] + ([] if after is None else [ANY]),
        out_specs=pl.BlockSpec((tm, tn), lambda i, j, k: (i, j)),
        out_shape=jax.ShapeDtypeStruct((m, n), out_dtype),
        scratch_shapes=[pltpu.VMEM((tm, tn) if nk > 1 else (8, 128), F32)],
        compiler_params=_params(("parallel", "parallel", "arbitrary")),
    )(*((a, b) if after is None else (a, b, after)))


def _make_rowwise(f, name, out_dtypes, out_cols, n_lat, tm, diff_rows, col_blocks=None, drow_dtypes=None):
    drow_dtypes = drow_dtypes or {}

    def tile_fn(*args):
        return tuple(o.astype(dt) for o, dt in zip(f(*args), out_dtypes))

    def row_spec(k, arr):
        width, index = (col_blocks or {}).get(k, (arr.shape[1], 0))
        return pl.BlockSpec((tm, width), lambda i: (i, index))

    def row_width(k, arr):
        return (col_blocks or {}).get(k, (arr.shape[1], 0))[0]

    def fwd_call(rows, vecs):
        t = min(r.shape[0] for r in rows)
        nr, nv = len(rows), len(vecs)
        nl = n_lat // tm

        def body(*refs):
            grp = (pl.program_id(0) >= nl).astype(jnp.int32)
            args = [r[...] for r in refs[:nr]] + [v[grp] for v in refs[nr:nr + nv]]
            for o_ref, o in zip(refs[nr + nv:], tile_fn(*args)):
                o_ref[...] = o

        return pl.pallas_call(
            body,
            name=name + "_fwd",
            grid=(t // tm,),
            in_specs=[row_spec(k, r) for k, r in enumerate(rows)]
            + [pl.BlockSpec(v.shape, lambda i: (0, 0, 0)) for v in vecs],
            out_specs=[pl.BlockSpec((tm, c), lambda i: (i, 0)) for c in out_cols],
            out_shape=[jax.ShapeDtypeStruct((t, c), dt) for c, dt in zip(out_cols, out_dtypes)],
            compiler_params=_params(("parallel",)),
        )(*rows, *vecs)

    def bwd_call(rows, vecs, gs, acc=None):
        t = min(r.shape[0] for r in rows)
        nr, nv, ng = len(rows), len(vecs), len(gs)
        nl = n_lat // tm
        nd = len(diff_rows)
        acc = [None] * nd if acc is None else list(acc)
        acc_in = [a for a in acc if a is not None]

        def body(*refs):
            i = pl.program_id(0)
            grp = (i >= nl).astype(jnp.int32)
            args = [r[...] for r in refs[:nr]] + [v[grp] for v in refs[nr:nr + nv]]
            g_refs = refs[nr + nv:nr + nv + ng]
            acc_refs = list(refs[nr + nv + ng:nr + nv + ng + len(acc_in)])
            drow_refs = refs[nr + nv + ng + len(acc_in):nr + nv + ng + len(acc_in) + nd]
            dvec_refs = refs[nr + nv + ng + len(acc_in) + nd:]
            _, vjp = jax.vjp(tile_fn, *args)
            grads = vjp(tuple(g[...] for g in g_refs))
            for d_ref, k, a in zip(drow_refs, diff_rows, acc):
                gk = grads[k] if a is None else grads[k] + acc_refs.pop(0)[...]
                d_ref[...] = gk.astype(d_ref.dtype)

            @pl.when(i == 0)
            def _():
                for d_ref in dvec_refs:
                    d_ref[...] = jnp.zeros_like(d_ref)

            for j, d_ref in enumerate(dvec_refs):
                d_ref[grp] += grads[nr + j]

        outs = pl.pallas_call(
            body,
            name=name + "_bwd",
            grid=(t // tm,),
            in_specs=[row_spec(k, r) for k, r in enumerate(rows)]
            + [pl.BlockSpec(v.shape, lambda i: (0, 0, 0)) for v in vecs]
            + [pl.BlockSpec((tm, g.shape[1]), lambda i: (i, 0)) for g in gs]
            + [pl.BlockSpec((tm, a.shape[1]), lambda i: (i, 0)) for a in acc_in],
            out_specs=[pl.BlockSpec((tm, row_width(k, rows[k])), lambda i: (i, 0)) for k in diff_rows]
            + [pl.BlockSpec(v.shape, lambda i: (0, 0, 0)) for v in vecs],
            out_shape=[jax.ShapeDtypeStruct((t, row_width(k, rows[k])), drow_dtypes.get(k, rows[k].dtype))
                       for k in diff_rows]
            + [jax.ShapeDtypeStruct(v.shape, F32) for v in vecs],
            compiler_params=_params(("arbitrary",)),
        )(*rows, *vecs, *gs, *acc_in)
        return outs[:nd], outs[nd:]

    return fwd_call, bwd_call


def _f_norm_mod(x, g, scale, shift):
    r = lax.rsqrt(jnp.mean(x * x, axis=-1, keepdims=True) + NORM_EPS)
    return ((x * r * g) * (1.0 + scale) + shift,)


def _f_gate_na(o, z):
    return (o.astype(F32) * _silu(z.astype(F32)),)


def _f_merge(g_na, g_ret, y_na, y_ret):
    return (jax.nn.sigmoid(g_na.astype(F32)) * y_na.astype(F32) + jax.nn.sigmoid(g_ret.astype(F32)) * y_ret.astype(F32),)


def _f_residual(x, out, gate):
    return (x + gate * out,)


def _f_loss(x, target, g):
    r = lax.rsqrt(jnp.mean(x * x, axis=-1, keepdims=True) + NORM_EPS)
    y = x * r * g
    e = 0.5 * jnp.mean(jnp.square(y - target), axis=-1, keepdims=True)
    return (jnp.broadcast_to(e * (1.0 / 128.0), (x.shape[0], 128)),)


def _gate_ret_fwd_call(of, ob, z, zblk, tm):
    t, w = of.shape
    nh = w // RET_VAL_DIM

    def body(of_ref, ob_ref, z_ref, a_ref):
        for hh in range(nh):
            sl = slice(hh * RET_VAL_DIM, (hh + 1) * RET_VAL_DIM)
            o = of_ref[:, sl].astype(F32) + ob_ref[:, sl].astype(F32)
            r = lax.rsqrt(jnp.mean(o * o, axis=-1, keepdims=True) + NORM_EPS)
            a_ref[:, sl] = ((o * r) * _silu(z_ref[:, sl].astype(F32))).astype(a_ref.dtype)

    spec = pl.BlockSpec((tm, w), lambda i: (i, 0))
    zspec = pl.BlockSpec((tm, w), lambda i: (i, zblk))
    return pl.pallas_call(
        body, name="gate_ret_fwd", grid=(t // tm,), in_specs=[spec, spec, zspec], out_specs=spec,
        out_shape=jax.ShapeDtypeStruct((t, w), BF16), compiler_params=_params(("parallel",)),
    )(of, ob, z)


def _gate_ret_bwd_call(of, ob, z, zblk, da, tm):
    t, w = of.shape
    nh = w // RET_VAL_DIM

    def body(of_ref, ob_ref, z_ref, da_ref, do_ref, dz_ref):
        for hh in range(nh):
            sl = slice(hh * RET_VAL_DIM, (hh + 1) * RET_VAL_DIM)
            o = of_ref[:, sl].astype(F32) + ob_ref[:, sl].astype(F32)
            r = lax.rsqrt(jnp.mean(o * o, axis=-1, keepdims=True) + NORM_EPS)
            n = o * r
            zf = z_ref[:, sl].astype(F32)
            sg = jax.nn.sigmoid(zf)
            g = da_ref[:, sl].astype(F32)
            dn = g * (zf * sg)
            dz_ref[:, sl] = (g * n * (sg * (1.0 + zf * (1.0 - sg)))).astype(dz_ref.dtype)
            do_ref[:, sl] = (r * (dn - n * jnp.mean(dn * n, axis=-1, keepdims=True))).astype(do_ref.dtype)

    spec = pl.BlockSpec((tm, w), lambda i: (i, 0))
    zspec = pl.BlockSpec((tm, w), lambda i: (i, zblk))
    return pl.pallas_call(
        body, name="gate_ret_bwd", grid=(t // tm,), in_specs=[spec, spec, zspec, spec], out_specs=[spec, spec],
        out_shape=[jax.ShapeDtypeStruct((t, w), BF16), jax.ShapeDtypeStruct((t, w), z.dtype)],
        compiler_params=_params(("parallel",)),
    )(of, ob, z, da)


NA_PAIR = 2 * GRID_W
NA_KEY_ROWS = NA_WIN_ROWS + 2
NA_CLASSES = 5


def _na_geometry(t, n_lat):
    rows = n_lat // GRID_W
    assert rows % 2 == 0 and rows >= NA_KEY_ROWS + 2, rows
    return rows, rows // 2, NA_KEY_ROWS * GRID_W, t - n_lat, t // NA_PAIR


def _na_base(p, rows):
    return jnp.clip(2 * p - NA_WIN_ROWS // 2, 0, rows - NA_KEY_ROWS)


def _na_class(p, rows):
    return p - _na_base(p, rows) // 2


def _na_group(pairs, n_ctx):
    assert n_ctx % NA_PAIR == 0, n_ctx
    return 2 if pairs % 2 == 0 and (n_ctx // NA_PAIR) % 2 == 0 else 1


def _na_bias_spec():
    return pl.BlockSpec((1, NA_CLASSES, 2, NA_KEY_ROWS // 2, GRID_W, NA_PAIR), lambda h, g: (h, 0, 0, 0, 0, 0))


def _na_bias_tile(bt_ref, cls):
    return jnp.concatenate(
        [jnp.concatenate([bt_ref[0, cls, i, q] for q in range(NA_KEY_ROWS // 2)], axis=1) for i in range(2)], axis=0)


def _na_add_bias_grad(dbt_ref, cls, ds):
    for i in range(2):
        for q in range(NA_KEY_ROWS // 2):
            dbt_ref[0, cls, i, q] += ds[i * GRID_W:(i + 1) * GRID_W, q * NA_PAIR:(q + 1) * NA_PAIR]


def _na_fwd_call(q, k, v, col0, w, bt, n_lat):
    t = q.shape[0]
    nh = w // NA_HEAD_DIM
    rows, pairs, n_loc, n_ctx, nq = _na_geometry(t, n_lat)
    grp = _na_group(pairs, n_ctx)
    scale = NA_HEAD_DIM ** -0.5

    def body(q_ref, k_ref, v_ref, bt_ref, o_ref):
        g = pl.program_id(1)
        kc = k_ref[pl.ds(n_lat, n_ctx), :]
        vc = v_ref[pl.ds(n_lat, n_ctx), :]

        @pl.when(g < pairs // grp)
        def _():
            for i in range(grp):
                p = g * grp + i
                sl = slice(i * NA_PAIR, (i + 1) * NA_PAIR)
                qb = q_ref[sl, :]
                s_ctx = _dot_nt(qb, kc) * scale
                start = pl.multiple_of(_na_base(p, rows) * GRID_W, GRID_W)
                kw = k_ref[pl.ds(start, n_loc), :]
                vw = v_ref[pl.ds(start, n_loc), :]
                s_loc = _dot_nt(qb, kw) * scale + _na_bias_tile(bt_ref, _na_class(p, rows))
                m = jnp.maximum(jnp.max(s_loc, axis=-1, keepdims=True), jnp.max(s_ctx, axis=-1, keepdims=True))
                p_loc = jnp.exp(s_loc - m)
                p_ctx = jnp.exp(s_ctx - m)
                l = jnp.sum(p_loc, axis=-1, keepdims=True) + jnp.sum(p_ctx, axis=-1, keepdims=True)
                o = _dot(p_loc.astype(BF16), vw) + _dot(p_ctx.astype(BF16), vc)
                o_ref[sl, :] = (o / l).astype(o_ref.dtype)

        @pl.when(g >= pairs // grp)
        def _():
            s_ctx = _dot_nt(q_ref[...], kc) * scale
            m = jnp.max(s_ctx, axis=-1, keepdims=True)
            p = jnp.exp(s_ctx - m)
            l = jnp.sum(p, axis=-1, keepdims=True)
            o_ref[...] = (_dot(p.astype(BF16), vc) / l).astype(o_ref.dtype)

    qspec = pl.BlockSpec((grp * NA_PAIR, NA_HEAD_DIM), lambda h, g: (g, h))
    in_q = pl.BlockSpec((grp * NA_PAIR, NA_HEAD_DIM), lambda h, g: (g, col0[0] + h))
    in_k = pl.BlockSpec((t, NA_HEAD_DIM), lambda h, g: (0, col0[1] + h))
    in_v = pl.BlockSpec((t, NA_HEAD_DIM), lambda h, g: (0, col0[2] + h))
    return pl.pallas_call(
        body,
        name="na_attn_fwd",
        grid=(nh, nq // grp),
        in_specs=[in_q, in_k, in_v, _na_bias_spec()],
        out_specs=qspec,
        out_shape=jax.ShapeDtypeStruct((t, w), BF16),
        compiler_params=_params(("parallel", "arbitrary")),
    )(q, k, v, bt)


def _na_bwd_call(q, k, v, col0, w, bt, do, n_lat):
    t = q.shape[0]
    nh = w // NA_HEAD_DIM
    rows, pairs, n_loc, n_ctx, nq = _na_geometry(t, n_lat)
    scale = NA_HEAD_DIM ** -0.5
    grp = _na_group(pairs, n_ctx)

    def body(q_ref, k_ref, v_ref, do_ref, bt_ref, dq_ref, dk_ref, dv_ref, dbt_ref):
        g = pl.program_id(1)

        @pl.when(g == 0)
        def _():
            dk_ref[...] = jnp.zeros_like(dk_ref)
            dv_ref[...] = jnp.zeros_like(dv_ref)
            dbt_ref[...] = jnp.zeros_like(dbt_ref)

        kc = k_ref[pl.ds(n_lat, n_ctx), :]
        vc = v_ref[pl.ds(n_lat, n_ctx), :]

        @pl.when(g < pairs // grp)
        def _():
            for i in range(grp):
                p = g * grp + i
                sl = slice(i * NA_PAIR, (i + 1) * NA_PAIR)
                qb = q_ref[sl, :]
                dob = do_ref[sl, :]
                s_ctx = _dot_nt(qb, kc) * scale
                dp_ctx = _dot_nt(dob, vc)
                start = pl.multiple_of(_na_base(p, rows) * GRID_W, GRID_W)
                kw = k_ref[pl.ds(start, n_loc), :]
                vw = v_ref[pl.ds(start, n_loc), :]
                cls = _na_class(p, rows)
                s_loc = _dot_nt(qb, kw) * scale + _na_bias_tile(bt_ref, cls)
                m = jnp.maximum(jnp.max(s_loc, axis=-1, keepdims=True), jnp.max(s_ctx, axis=-1, keepdims=True))
                p_loc = jnp.exp(s_loc - m)
                p_ctx = jnp.exp(s_ctx - m)
                inv = 1.0 / (jnp.sum(p_loc, axis=-1, keepdims=True) + jnp.sum(p_ctx, axis=-1, keepdims=True))
                p_loc = p_loc * inv
                p_ctx = p_ctx * inv
                dp_loc = _dot_nt(dob, vw)
                delta = (jnp.sum(p_loc * dp_loc, axis=-1, keepdims=True)
                         + jnp.sum(p_ctx * dp_ctx, axis=-1, keepdims=True))
                ds_loc = p_loc * (dp_loc - delta)
                ds_ctx = p_ctx * (dp_ctx - delta)
                _na_add_bias_grad(dbt_ref, cls, ds_loc)
                dsl = (ds_loc * scale).astype(BF16)
                dsc = (ds_ctx * scale).astype(BF16)
                dq_ref[sl, :] = (_dot(dsl, kw) + _dot(dsc, kc)).astype(dq_ref.dtype)
                dk_ref[pl.ds(start, n_loc), :] += _dot_tn(dsl, qb)
                dv_ref[pl.ds(start, n_loc), :] += _dot_tn(p_loc.astype(BF16), dob)
                dk_ref[pl.ds(n_lat, n_ctx), :] += _dot_tn(dsc, qb)
                dv_ref[pl.ds(n_lat, n_ctx), :] += _dot_tn(p_ctx.astype(BF16), dob)

        @pl.when(g >= pairs // grp)
        def _():
            qb = q_ref[...]
            dob = do_ref[...]
            s_ctx = _dot_nt(qb, kc) * scale
            dp_ctx = _dot_nt(dob, vc)
            m = jnp.max(s_ctx, axis=-1, keepdims=True)
            p = jnp.exp(s_ctx - m)
            p = p * (1.0 / jnp.sum(p, axis=-1, keepdims=True))
            delta = jnp.sum(p * dp_ctx, axis=-1, keepdims=True)
            dsc = (p * (dp_ctx - delta) * scale).astype(BF16)
            dq_ref[...] = _dot(dsc, kc).astype(dq_ref.dtype)
            dk_ref[pl.ds(n_lat, n_ctx), :] += _dot_tn(dsc, qb)
            dv_ref[pl.ds(n_lat, n_ctx), :] += _dot_tn(p.astype(BF16), dob)

    qspec = pl.BlockSpec((grp * NA_PAIR, NA_HEAD_DIM), lambda h, g: (g, h))
    kspec = pl.BlockSpec((t, NA_HEAD_DIM), lambda h, g: (0, h))
    return pl.pallas_call(
        body,
        name="na_attn_bwd",
        grid=(nh, nq // grp),
        in_specs=[pl.BlockSpec((grp * NA_PAIR, NA_HEAD_DIM), lambda h, g: (g, col0[0] + h)),
                  pl.BlockSpec((t, NA_HEAD_DIM), lambda h, g: (0, col0[1] + h)),
                  pl.BlockSpec((t, NA_HEAD_DIM), lambda h, g: (0, col0[2] + h)), qspec, _na_bias_spec()],
        out_specs=[qspec, kspec, kspec, _na_bias_spec()],
        out_shape=[
            jax.ShapeDtypeStruct((t, w), BF16),
            jax.ShapeDtypeStruct((t, w), F32),
            jax.ShapeDtypeStruct((t, w), F32),
            jax.ShapeDtypeStruct(bt.shape, F32),
        ],
        compiler_params=_params(("parallel", "arbitrary")),
    )(q, k, v, do, bt)


def _na_bias_table(rpb, rows):
    pairs = rows // 2
    nb = 2 * NA_WIN_COLS - 1
    nq = NA_KEY_ROWS // 2
    e1 = np.zeros((NA_CLASSES, 2, nq, 2, 2 * NA_WIN_ROWS - 1), np.float32)
    valid = np.zeros((NA_CLASSES, 2, nq, 2), bool)
    for cls, p in enumerate((0, 1, 2, pairs - 2, pairs - 1)):
        base = int(np.clip(2 * p - NA_WIN_ROWS // 2, 0, rows - NA_KEY_ROWS))
        assert p - base // 2 == cls, (rows, cls, p, base)
        for i in range(2):
            r = 2 * p + i
            r0 = int(np.clip(r - NA_WIN_ROWS // 2, 0, rows - NA_WIN_ROWS))
            for kk in range(NA_KEY_ROWS):
                if r0 <= base + kk < r0 + NA_WIN_ROWS:
                    valid[cls, i, kk // 2, kk % 2] = True
                    e1[cls, i, kk // 2, kk % 2, base + kk - r + NA_WIN_ROWS - 1] = 1.0
    cidx = np.arange(GRID_W)
    dc = np.clip(cidx[None, :] - cidx[:, None] + (NA_WIN_COLS - 1), 0, nb - 1)
    c0 = np.clip(cidx - NA_WIN_COLS // 2, 0, GRID_W - NA_WIN_COLS)
    col_in = (cidx[None, :] >= c0[:, None]) & (cidx[None, :] < c0[:, None] + NA_WIN_COLS)
    e2 = np.zeros((GRID_W, 2, GRID_W, 2, nb), np.float32)
    for par in range(2):
        e2[np.arange(GRID_W)[:, None], par, np.arange(GRID_W)[None, :], par, dc] = 1.0
    mask = valid[:, :, :, None, :, None] & col_in[None, None, None, :, None, :]
    t1 = jnp.einsum("hab,xiqpa->hxiqpb", rpb, jnp.asarray(e1), precision=lax.Precision.HIGHEST)
    t1 = t1.reshape(t1.shape[:4] + (2 * nb,))
    b = jnp.einsum("hxiqm,cwm->hxiqcw", t1, jnp.asarray(e2.reshape(GRID_W, 2 * GRID_W, 2 * nb)),
                   precision=lax.Precision.HIGHEST)
    return jnp.where(jnp.asarray(mask.reshape(NA_CLASSES, 2, nq, GRID_W, 2 * GRID_W))[None], b, MASK_VALUE)


def _ret_decays(lam_s, reverse):
    c = RET_CHUNK
    ii = lax.broadcasted_iota(jnp.int32, (c, c), 0)
    jj = lax.broadcasted_iota(jnp.int32, (c, c), 1)
    d = (jj - ii) if reverse else (ii - jj)
    dpos = jnp.maximum(d.astype(F32), 0.0)
    mask = jnp.where(d >= 0, jnp.exp(dpos * lam_s), 0.0)
    pi = lax.broadcasted_iota(jnp.int32, (c, 1), 0).astype(F32)
    qpos = (c - pi) if reverse else (pi + 1.0)
    kpos = pi if reverse else (c - 1.0 - pi)
    qd = jnp.exp(qpos * lam_s)
    kd = jnp.exp(kpos * lam_s)
    g = jnp.exp(jnp.full((1, RET_VAL_DIM), c * lam_s, F32))
    return mask, dpos, qd, kd, qpos, kpos, g


def _ret_decay_scratch(hg):
    return [pltpu.VMEM((hg, RET_CHUNK, RET_CHUNK), F32), pltpu.VMEM((hg, RET_CHUNK, RET_KEY_DIM), F32),
            pltpu.VMEM((hg, RET_CHUNK, RET_KEY_DIM), F32), pltpu.VMEM((hg, 8, RET_VAL_DIM), F32)]


def _ret_fill_decays(lam_ref, first_head, hg, reverse, dmask, dq, dkey, dg):
    for j in range(hg):
        mask, _, qd, kd, _, _, g = _ret_decays(lam_ref[first_head + j], reverse)
        dmask[j] = mask
        dq[j] = jnp.broadcast_to(qd, (RET_CHUNK, RET_KEY_DIM))
        dkey[j] = jnp.broadcast_to(kd, (RET_CHUNK, RET_KEY_DIM))
        dg[j] = jnp.broadcast_to(g, (8, RET_VAL_DIM))


def _ret_head_group(nh):
    return _pick(nh, (8, 4, 2))


def _ret_chunk_of(t, nt, nl, reverse):
    return (nt - 1 - t) if reverse else (t + nl) % nt


def _ret_fwd_call(qr, kr, v, vcol, lam, n_lat, reverse):
    t = qr.shape[0]
    nh = qr.shape[1] // RET_KEY_DIM
    c = RET_CHUNK
    nt, nl = t // c, n_lat // c

    hg = _ret_head_group(nh)
    dk, dv = RET_KEY_DIM, RET_VAL_DIM

    def body(lam_ref, q_ref, k_ref, v_ref, o_ref, s_ref, state, dmask, dq, dkey, dg):
        hb, step = pl.program_id(0), pl.program_id(1)

        @pl.when(step == 0)
        def _():
            state[...] = jnp.zeros_like(state)
            _ret_fill_decays(lam_ref, hb * hg, hg, reverse, dmask, dq, dkey, dg)

        for j in range(hg):
            mask, qd, kd, g = dmask[j], dq[j], dkey[j], dg[j, 0:1, :]
            q, k, vv = q_ref[:, j * dk:(j + 1) * dk], k_ref[:, j * dk:(j + 1) * dk], v_ref[:, j * dv:(j + 1) * dv]
            p = _dot_nt(q, k) * mask
            s = state[j]
            qs = (q.astype(F32) * qd).astype(BF16)
            o_ref[:, j * dv:(j + 1) * dv] = (_dot(p.astype(BF16), vv) + _dot(qs, s.astype(BF16))).astype(o_ref.dtype)
            s_ref[j, 0] = s
            ks = (k.astype(F32) * kd).astype(BF16)
            state[j] = s * g + _dot_tn(ks, vv)

    def cmap(hb, step, lam_ref):
        return (_ret_chunk_of(step, nt, nl, reverse), hb)

    def vmap(hb, step, lam_ref):
        return (_ret_chunk_of(step, nt, nl, reverse), vcol // (hg * dv) + hb)

    return pl.pallas_call(
        body,
        name="retention_rev_fwd" if reverse else "retention_fwd",
        grid_spec=pltpu.PrefetchScalarGridSpec(
            num_scalar_prefetch=1,
            grid=(nh // hg, nt),
            in_specs=[
                pl.BlockSpec((c, hg * dk), cmap),
                pl.BlockSpec((c, hg * dk), cmap),
                pl.BlockSpec((c, hg * dv), vmap),
            ],
            out_specs=[
                pl.BlockSpec((c, hg * dv), cmap),
                pl.BlockSpec((hg, 1, dk, dv), lambda hb, step, lam_ref: (hb, step, 0, 0)),
            ],
            scratch_shapes=[pltpu.VMEM((hg, dk, dv), F32)] + _ret_decay_scratch(hg),
        ),
        out_shape=[
            jax.ShapeDtypeStruct((t, nh * RET_VAL_DIM), BF16),
            jax.ShapeDtypeStruct((nh, nt, RET_KEY_DIM, RET_VAL_DIM), F32),
        ],
        compiler_params=_params(("parallel", "arbitrary")),
    )(lam, qr, kr, v)


def _ret_bwd_call(qr, kr, v, vcol, lam, states, do, n_lat, reverse):
    assert RET_CHUNK == RET_KEY_DIM and RET_VAL_DIM % RET_KEY_DIM == 0
    t = qr.shape[0]
    nh = qr.shape[1] // RET_KEY_DIM
    c = RET_CHUNK
    nt, nl = t // c, n_lat // c

    hg = _ret_head_group(nh)
    dk, dv = RET_KEY_DIM, RET_VAL_DIM

    def body(lam_ref, q_ref, k_ref, v_ref, s_ref, do_ref, dq_ref, dk_ref, dv_ref, dl_ref, dstate, dmask, dq, dkey, dg):
        hb, rstep = pl.program_id(0), pl.program_id(1)

        @pl.when(rstep == 0)
        def _():
            dstate[...] = jnp.zeros_like(dstate)
            dl_ref[...] = jnp.zeros_like(dl_ref)
            _ret_fill_decays(lam_ref, hb * hg, hg, reverse, dmask, dq, dkey, dg)

        _, dpos, _, _, qpos, kpos, _ = _ret_decays(0.0, reverse)
        for j in range(hg):
            mask, qd, kd, g = dmask[j], dq[j], dkey[j], dg[j, 0:1, :]
            ksl, vsl = slice(j * dk, (j + 1) * dk), slice(j * dv, (j + 1) * dv)
            q, k, vv = q_ref[:, ksl], k_ref[:, ksl], v_ref[:, vsl]
            qf, kf = q.astype(F32), k.astype(F32)
            s = s_ref[j, 0]
            ds = dstate[j]
            dob = do_ref[:, vsl].astype(BF16)
            sb, dsb = s.astype(BF16), ds.astype(BF16)
            a = _dot_nt(q, k)
            p = a * mask
            dp = _dot_nt(dob, vv)
            da = dp * mask
            dab = da.astype(BF16)
            dqc = _dot_nt(dob, sb)
            dkc = _dot_nt(vv, dsb)
            qs = (qf * qd).astype(BF16)
            ks = (kf * kd).astype(BF16)
            dq_ref[:, ksl] = (_dot(dab, k) + dqc * qd).astype(dq_ref.dtype)
            dk_ref[:, ksl] = (_dot_tn(dab, q) + dkc * kd).astype(dk_ref.dtype)
            dv_ref[:, vsl] = (_dot_tn(p.astype(BF16), dob) + _dot(ks, dsb)).astype(dv_ref.dtype)
            dsg = ds * s * (g * c)
            terms = da * a * dpos + dqc * qf * (qd * qpos) + dkc * kf * (kd * kpos)
            for half in range(dv // dk):
                terms = terms + dsg[:, half * dk:(half + 1) * dk]
            total = jnp.sum(jnp.sum(terms, axis=0, keepdims=True), axis=1, keepdims=True)
            dl_ref[j] += jnp.broadcast_to(total, (8, 128))
            dstate[j] = ds * g + _dot_tn(qs, dob)

    def cmap(hb, rstep, lam_ref):
        return (_ret_chunk_of(nt - 1 - rstep, nt, nl, reverse), hb)

    def vmap(hb, rstep, lam_ref):
        return (_ret_chunk_of(nt - 1 - rstep, nt, nl, reverse), vcol // (hg * dv) + hb)

    return pl.pallas_call(
        body,
        name="retention_rev_bwd" if reverse else "retention_bwd",
        grid_spec=pltpu.PrefetchScalarGridSpec(
            num_scalar_prefetch=1,
            grid=(nh // hg, nt),
            in_specs=[
                pl.BlockSpec((c, hg * dk), cmap),
                pl.BlockSpec((c, hg * dk), cmap),
                pl.BlockSpec((c, hg * dv), vmap),
                pl.BlockSpec((hg, 1, dk, dv), lambda hb, rstep, lam_ref: (hb, nt - 1 - rstep, 0, 0)),
                pl.BlockSpec((c, hg * dv), cmap),
            ],
            out_specs=[
                pl.BlockSpec((c, hg * dk), cmap),
                pl.BlockSpec((c, hg * dk), cmap),
                pl.BlockSpec((c, hg * dv), cmap),
                pl.BlockSpec((hg, 8, 128), lambda hb, rstep, lam_ref: (hb, 0, 0)),
            ],
            scratch_shapes=[pltpu.VMEM((hg, dk, dv), F32)] + _ret_decay_scratch(hg),
        ),
        out_shape=[
            jax.ShapeDtypeStruct(qr.shape, qr.dtype),
            jax.ShapeDtypeStruct(kr.shape, kr.dtype),
            jax.ShapeDtypeStruct((t, nh * dv), v.dtype),
            jax.ShapeDtypeStruct((nh, 8, 128), F32),
        ],
        compiler_params=_params(("parallel", "arbitrary")),
    )(lam, qr, kr, v, states, do)


def _rope_tables(t, n_lat):
    nf = RET_KEY_DIM // 4
    tok = np.arange(n_lat)
    inv_freq = (ROPE_BASE ** (-np.arange(nf, dtype=np.float32) / nf)).astype(np.float32)
    row = (tok // GRID_W).astype(np.float32)
    col = (tok % GRID_W).astype(np.float32)
    ang = np.concatenate([row[:, None] * inv_freq, col[:, None] * inv_freq], axis=-1).astype(np.float32)
    cos = np.ones((t, 2 * nf), np.float32)
    sin = np.zeros((t, 2 * nf), np.float32)
    cos[:n_lat] = np.cos(ang)
    sin[:n_lat] = np.sin(ang)
    return jnp.asarray(np.concatenate([cos, cos], axis=1)), jnp.asarray(np.concatenate([-sin, sin], axis=1))


def _rotate(x, cos2, sin2):
    return x * cos2 + pltpu.roll(x, RET_KEY_DIM // 2, 1) * sin2


def _rope_fwd_call(u, qblk, kblk, w_qk, cos2, sin2, k_scale, tm):
    t = u.shape[0]
    nh = w_qk // RET_KEY_DIM

    def body(q_ref, k_ref, c_ref, s_ref, qr_ref, kr_ref):
        c, s = c_ref[...], s_ref[...]
        for hh in range(nh):
            sl = slice(hh * RET_KEY_DIM, (hh + 1) * RET_KEY_DIM)
            qr_ref[:, sl] = _rotate(q_ref[:, sl].astype(F32), c, s).astype(qr_ref.dtype)
            kr_ref[:, sl] = (_rotate(k_ref[:, sl].astype(F32), c, s) * k_scale).astype(kr_ref.dtype)

    tab = pl.BlockSpec((tm, RET_KEY_DIM), lambda i: (i, 0))
    out = pl.BlockSpec((tm, w_qk), lambda i: (i, 0))
    return pl.pallas_call(
        body, name="rope_fwd", grid=(t // tm,),
        in_specs=[pl.BlockSpec((tm, w_qk), lambda i: (i, qblk)), pl.BlockSpec((tm, w_qk), lambda i: (i, kblk)), tab, tab],
        out_specs=[out, out], out_shape=[jax.ShapeDtypeStruct((t, w_qk), BF16)] * 2,
        compiler_params=_params(("parallel",)),
    )(u, u, cos2, sin2)


def _assemble_du_call(pieces, off, cos2, sin2, k_scale, tm):
    flat = [a for p in pieces for a in (p if isinstance(p, tuple) else (p,))]
    t = flat[0].shape[0]
    n = len(flat)

    def body(*refs):
        c, s = refs[n][...], -refs[n + 1][...]
        o_ref = refs[n + 2]
        it = iter(refs[:n])
        for blk, p in enumerate(pieces):
            lo = off[blk]
            if not isinstance(p, tuple):
                o_ref[:, lo:off[blk + 1]] = next(it)[...].astype(o_ref.dtype)
                continue
            fwd_ref, rev_ref = next(it), next(it)
            if blk == 6:
                o_ref[:, lo:off[blk + 1]] = (fwd_ref[...].astype(F32) + rev_ref[...].astype(F32)).astype(o_ref.dtype)
                continue
            mult = k_scale if blk == 5 else 1.0
            for hh in range((off[blk + 1] - lo) // RET_KEY_DIM):
                sl = slice(hh * RET_KEY_DIM, (hh + 1) * RET_KEY_DIM)
                dy = (fwd_ref[:, sl].astype(F32) + rev_ref[:, sl].astype(F32)) * mult
                o_ref[:, lo + hh * RET_KEY_DIM:lo + (hh + 1) * RET_KEY_DIM] = _rotate(dy, c, s).astype(o_ref.dtype)

    tab = pl.BlockSpec((tm, RET_KEY_DIM), lambda i: (i, 0))
    return pl.pallas_call(
        body, name="assemble_du", grid=(t // tm,),
        in_specs=[pl.BlockSpec((tm, a.shape[1]), lambda i: (i, 0)) for a in flat] + [tab, tab],
        out_specs=pl.BlockSpec((tm, off[-1]), lambda i: (i, 0)),
        out_shape=jax.ShapeDtypeStruct((t, off[-1]), BF16),
        compiler_params=_params(("parallel",)),
    )(*flat, cos2, sin2)


def _my_position():
    return lax.axis_index("x"), lax.axis_index("y"), lax.axis_index("c")


def _flip(pos, k):
    x, y, c = pos
    return (1 - x if k & 4 else x, 1 - y if k & 2 else y, 1 - c if k & 1 else c)


def _linear(pos):
    return 4 * pos[0] + 2 * pos[1] + pos[2]


def _slab(ref, axis, idx, size):
    start = pl.multiple_of(idx * size, size)
    return ref.at[pl.ds(start, size), :] if axis == 0 else ref.at[:, pl.ds(start, size)]


HBM_SPEC = pl.BlockSpec(memory_space=pltpu.HBM)
SEM_SPEC = pl.BlockSpec(memory_space=pltpu.SEMAPHORE)
DATAFLOW = pltpu.SideEffectType.DATAFLOW_SIDE_EFFECTING
PEER_BITS = (1, 2, 4, 6, 3, 5, 7)
GATHER_BITS = (1, 2, 4, 6)


def _in_hbm(a):
    return pltpu.with_memory_space_constraint(a, pltpu.HBM)


def _gather_views(me, k, a, src_refs, land_refs, axes):
    size = src_refs[a].shape[axes[a]]
    peer = _flip(me, k)
    return src_refs[a], _slab(land_refs[a], axes[a], _linear(me), size), _slab(land_refs[a], axes[a], _linear(peer), size)


def _scatter_views(me, k, a, src_refs, land_refs, axes):
    size = land_refs[a].shape[1 + axes[a]]
    peer = _flip(me, k)
    return _slab(src_refs[a], axes[a], _linear(peer), size), land_refs[a].at[k - 1], land_refs[a].at[k - 1]


CHIP_BITS = (0, 2, 4, 6)


def _chip_views(me, k, a, src_refs, land_refs, axes):
    j = CHIP_BITS.index(k)
    return src_refs[a].at[j], land_refs[a].at[j - 1], land_refs[a].at[j - 1]


def _pair_exchange(grads, axes, sizes):
    ns = len(grads)

    def slab_shape(a):
        s = grads[a].shape
        return (sizes[a], s[1]) if axes[a] == 0 else (s[0], sizes[a])

    def body(*refs):
        g_refs, p_refs = refs[:ns], refs[ns:2 * ns]
        send_sems, recv_sems = refs[2 * ns:]
        me = _my_position()
        sibling = _flip(me, 1)
        copies = []
        for j, kc in enumerate(CHIP_BITS):
            for a in range(ns):
                cp = pltpu.make_async_remote_copy(
                    src_ref=_slab(g_refs[a], axes[a], _linear(_flip(me, kc | 1)), sizes[a]), dst_ref=p_refs[a].at[j],
                    send_sem=send_sems.at[4 * a + j], recv_sem=recv_sems.at[4 * a + j],
                    device_id=sibling, device_id_type=MESH)
                cp.start()
                copies.append(cp)
        for cp in copies:
            cp.wait_recv()
        for cp in copies:
            cp.wait_send()

    return pl.pallas_call(
        body, name="scatter_pair_exchange", in_specs=[ANY] * ns, out_specs=[ANY] * ns,
        out_shape=[jax.ShapeDtypeStruct((4,) + slab_shape(a), grads[a].dtype) for a in range(ns)],
        scratch_shapes=[pltpu.SemaphoreType.DMA((4 * ns,)), pltpu.SemaphoreType.DMA((4 * ns,))],
        compiler_params=pltpu.CompilerParams(has_side_effects=True),
    )(*grads)


def _pair_add(grad, theirs, axis, chip_idx):
    _, r, c = theirs.shape
    tm = _pick(r, (256, 128, 64, 32, 16))
    if axis == 0:
        mine_spec = pl.BlockSpec((tm, c), lambda j, i, idx: (idx[j] * (r // tm) + i, 0))
    else:
        mine_spec = pl.BlockSpec((tm, c), lambda j, i, idx: (i, idx[j]))

    def body(idx_ref, mine_ref, theirs_ref, o_ref):
        o_ref[0] = (mine_ref[...].astype(F32) + theirs_ref[0].astype(F32)).astype(o_ref.dtype)

    spec = pl.BlockSpec((1, tm, c), lambda j, i, idx: (j, i, 0))
    return pl.pallas_call(
        body, name="scatter_pair_add",
        grid_spec=pltpu.PrefetchScalarGridSpec(
            num_scalar_prefetch=1, grid=(4, r // tm), in_specs=[mine_spec, spec], out_specs=spec),
        out_shape=jax.ShapeDtypeStruct(theirs.shape, theirs.dtype),
        compiler_params=_params(("parallel", "parallel")),
    )(chip_idx, grad, theirs)


def _slab_block(rows, cols, tm, axis):
    if axis == 0:
        return pl.BlockSpec((tm, cols), lambda i, idx: (idx[0] * (rows // tm) + i, 0))
    return pl.BlockSpec((tm, cols), lambda i, idx: (i, idx[0]))


def _place_shard(shard, land, axis, my_idx):
    r, c = shard.shape
    tm = _pick(r, (512, 256, 128, 64, 32, 16))

    def body(idx_ref, s_ref, land_ref, o_ref):
        o_ref[...] = s_ref[...]

    return pl.pallas_call(
        body, name="gather_place",
        grid_spec=pltpu.PrefetchScalarGridSpec(
            num_scalar_prefetch=1, grid=(r // tm,),
            in_specs=[pl.BlockSpec((tm, c), lambda i, idx: (i, 0)), ANY],
            out_specs=_slab_block(r, c, tm, axis)),
        out_shape=jax.ShapeDtypeStruct(land.shape, land.dtype),
        input_output_aliases={2: 0},
        compiler_params=_params(("parallel",)),
    )(my_idx, shard, land)


def _push_start(name, srcs, lands, axes, views, bits, deps):
    ns = len(srcs)

    def body(*refs):
        src_refs, land_refs = refs[:ns], refs[ns:2 * ns]
        send_sems, recv_sems = refs[2 * ns + len(deps):2 * ns + len(deps) + 2]
        token = refs[-1]
        me = _my_position()
        for k in bits:
            for a in range(ns):
                s, d, _ = views(me, k, a, src_refs, land_refs, axes)
                pltpu.make_async_remote_copy(
                    src_ref=s, dst_ref=d, send_sem=send_sems.at[7 * a + k - 1], recv_sem=recv_sems.at[7 * a + k - 1],
                    device_id=_flip(me, k), device_id_type=MESH).start()
        token[...] = jnp.zeros_like(token)

    thru = [pltpu.HBM(a.shape, a.dtype) for a in list(srcs) + list(lands)]
    outs = pl.pallas_call(
        body, name=name,
        in_specs=[HBM_SPEC] * (2 * ns) + [ANY] * len(deps),
        out_specs=[SEM_SPEC, SEM_SPEC] + [HBM_SPEC] * (2 * ns) + [VMEM_SPEC],
        out_shape=[pltpu.SemaphoreType.DMA((7 * ns,)), pltpu.SemaphoreType.DMA((7 * ns,))] + thru
        + [jax.ShapeDtypeStruct((8, 128), F32)],
        input_output_aliases={i: 2 + i for i in range(2 * ns)},
        compiler_params=pltpu.CompilerParams(has_side_effects=DATAFLOW),
    )(*[_in_hbm(a) for a in srcs], *[_in_hbm(a) for a in lands], *deps)
    return (outs[0], outs[1]), outs[2:2 + ns], outs[2 + ns:2 + 2 * ns], outs[-1]


def _gather_finish(lands, axes, sizes):
    ns = len(lands)
    chips = (2, 4, 6)

    def body(*refs):
        land_refs = refs[ns:2 * ns]
        send_sems, recv_sems = refs[2 * ns:]
        me = _my_position()
        sibling = _flip(me, 1)
        copies = []
        for j, kc in enumerate(chips):
            for a in range(ns):
                def slab_of(pos):
                    return _slab(land_refs[a], axes[a], _linear(pos), sizes[a])
                send = pltpu.make_async_remote_copy(
                    src_ref=slab_of(_flip(me, kc)), dst_ref=slab_of(_flip(me, kc)), send_sem=send_sems.at[3 * a + j],
                    recv_sem=recv_sems.at[3 * a + j], device_id=sibling, device_id_type=MESH)
                recv = pltpu.make_async_remote_copy(
                    src_ref=slab_of(_flip(me, kc)), dst_ref=slab_of(_flip(sibling, kc)), send_sem=send_sems.at[3 * a + j],
                    recv_sem=recv_sems.at[3 * a + j], device_id=sibling, device_id_type=MESH)
                send.start()
                copies.append((send, recv))
        for send, recv in copies:
            recv.wait_recv()
        for send, recv in copies:
            send.wait_send()

    return pl.pallas_call(
        body, name="gather_finish", in_specs=[ANY] * ns, out_specs=[ANY] * ns,
        out_shape=[jax.ShapeDtypeStruct(l.shape, l.dtype) for l in lands],
        input_output_aliases={a: a for a in range(ns)},
        scratch_shapes=[pltpu.SemaphoreType.DMA((3 * ns,)), pltpu.SemaphoreType.DMA((3 * ns,))],
        compiler_params=pltpu.CompilerParams(has_side_effects=True),
    )(*lands)


def _push_wait(name, sems, srcs, lands, axes, views, bits, after):
    ns = len(srcs)

    def body(*refs):
        src_refs, land_refs = refs[:ns], refs[ns:2 * ns]
        send_sems, recv_sems = refs[2 * ns:2 * ns + 2]
        me = _my_position()
        for k in bits:
            for a in range(ns):
                s, d, got = views(me, k, a, src_refs, land_refs, axes)
                cp = pltpu.make_async_remote_copy(
                    src_ref=s, dst_ref=got, send_sem=send_sems.at[7 * a + k - 1], recv_sem=recv_sems.at[7 * a + k - 1],
                    device_id=_flip(me, k), device_id_type=MESH)
                cp.wait_send()
                cp.wait_recv()

    thru = [pltpu.HBM(a.shape, a.dtype) for a in list(srcs) + list(lands)]
    outs = pl.pallas_call(
        body, name=name,
        in_specs=[HBM_SPEC] * (2 * ns) + [SEM_SPEC, SEM_SPEC] + [ANY] * len(after),
        out_specs=[HBM_SPEC] * (2 * ns),
        out_shape=thru,
        input_output_aliases={i: i for i in range(2 * ns)},
        compiler_params=pltpu.CompilerParams(has_side_effects=DATAFLOW),
    )(*srcs, *lands, sems[0], sems[1], *after)
    return outs[:ns], outs[ns:]


def _small_allgather(v, name):
    r, c = v.shape

    def body(v_ref, all_ref, sum_ref, send_sems, recv_sems):
        me = _my_position()
        all_ref[_linear(me)] = v_ref[...]
        copies = []
        for k in range(1, N_DEV):
            peer = _flip(me, k)
            copies.append(pltpu.make_async_remote_copy(
                src_ref=v_ref, dst_ref=all_ref.at[_linear(me)], send_sem=send_sems.at[k - 1], recv_sem=recv_sems.at[k - 1],
                device_id=peer, device_id_type=MESH))
        for cp in copies:
            cp.start()
        for k in range(1, N_DEV):
            peer = _flip(me, k)
            pltpu.make_async_remote_copy(
                src_ref=v_ref, dst_ref=all_ref.at[_linear(peer)], send_sem=send_sems.at[k - 1], recv_sem=recv_sems.at[k - 1],
                device_id=peer, device_id_type=MESH).wait_recv()
        for cp in copies:
            cp.wait_send()
        acc = all_ref[0]
        for d in range(1, N_DEV):
            acc = acc + all_ref[d]
        sum_ref[...] = acc

    return pl.pallas_call(
        body,
        name=name,
        in_specs=[VMEM_SPEC],
        out_specs=[VMEM_SPEC, VMEM_SPEC],
        out_shape=[jax.ShapeDtypeStruct((N_DEV, r, c), F32), jax.ShapeDtypeStruct((r, c), F32)],
        scratch_shapes=[pltpu.SemaphoreType.DMA((N_DEV - 1,)), pltpu.SemaphoreType.DMA((N_DEV - 1,))],
        compiler_params=pltpu.CompilerParams(has_side_effects=True, vmem_limit_bytes=VMEM_LIMIT),
    )(v)


def _ada_fwd_call(cin, ada_w, ada_b_cols):
    nl, d, ncol = ada_w.shape
    nrow = cin.shape[0]

    def body(c_ref, w_ref, b_ref, o_ref):
        cs = _silu(c_ref[...]).astype(BF16)
        for l in range(nl):
            o_ref[l] = _dot(cs, w_ref[l].astype(BF16)) + b_ref[l]

    return pl.pallas_call(
        body, name="ada_fwd", in_specs=[VMEM_SPEC] * 3, out_specs=VMEM_SPEC,
        out_shape=jax.ShapeDtypeStruct((nl, nrow, ncol), F32), compiler_params=_params(),
    )(cin, ada_w, ada_b_cols)


def _ada_bwd_call(cin, ada_w, dmod):
    nl, d, ncol = ada_w.shape
    nrow = cin.shape[0]

    def body(c_ref, w_ref, dm_ref, gw_ref, dcs_ref):
        cs = _silu(c_ref[...]).astype(BF16)
        acc = jnp.zeros((nrow, d), F32)
        for l in range(nl):
            dm = dm_ref[l].astype(BF16)
            gw_ref[l] = _dot_tn(cs, dm)
            acc = acc + _dot_nt(dm, w_ref[l].astype(BF16))
        dcs_ref[...] = acc

    return pl.pallas_call(
        body, name="ada_bwd", in_specs=[VMEM_SPEC] * 3, out_specs=[VMEM_SPEC, VMEM_SPEC],
        out_shape=[jax.ShapeDtypeStruct((nl, d, ncol), F32), jax.ShapeDtypeStruct((nrow, d), F32)],
        compiler_params=_params(),
    )(cin, ada_w, dmod)


def _adamw_math(w, g, m, v):
    m = ADAM_B1 * m + (1.0 - ADAM_B1) * g
    v = ADAM_B2 * v + (1.0 - ADAM_B2) * jnp.square(g)
    m_hat = m / (1.0 - ADAM_B1 ** ADAM_STEP)
    v_hat = v / (1.0 - ADAM_B2 ** ADAM_STEP)
    delta = -ADAM_LR * (m_hat / (jnp.sqrt(v_hat) + ADAM_EPS) + ADAM_WD * w)
    return delta, m, v


def _adamw_sharded(w, m, v, mine, slabs, axis, my_idx, layer, prev, name):
    nl, r, c = w.shape
    tm = _pick(r, (128, 64, 32, 16))
    nprev = 0 if prev is None else len(prev)
    nslab = slabs.shape[0]
    if axis is None:
        mine_spec = pl.BlockSpec((1, tm, c), lambda i, idx: (0, i, 0))
    else:
        mine_spec = _slab_block(r, c, tm, axis)

    def body(idx_ref, w_ref, m_ref, v_ref, mine_ref, s_ref, *rest):
        g_ref, d_ref, nm_ref, nv_ref = rest[nprev:]
        g = (mine_ref[0] if axis is None else mine_ref[...]).astype(F32)
        for k in range(nslab):
            g = g + s_ref[k].astype(F32)
        delta, nm, nv = _adamw_math(w_ref[0], g, m_ref[0], v_ref[0])
        g_ref[0], d_ref[0], nm_ref[0], nv_ref[0] = g, delta, nm, nv

    spec = pl.BlockSpec((1, tm, c), lambda i, idx: (layer, i, 0))
    out = jax.ShapeDtypeStruct(w.shape, F32)
    return pl.pallas_call(
        body, name=name,
        grid_spec=pltpu.PrefetchScalarGridSpec(
            num_scalar_prefetch=1, grid=(r // tm,),
            in_specs=[spec, spec, spec, mine_spec,
                      pl.BlockSpec((nslab, tm, c), lambda i, idx: (0, i, 0))] + [ANY] * nprev,
            out_specs=[spec] * 4),
        out_shape=[out] * 4,
        input_output_aliases={6 + j: j for j in range(nprev)},
        compiler_params=_params(("parallel",)),
    )(my_idx, w, m, v, mine, slabs, *(() if prev is None else prev))


def _adamw_dense(w, g, m, v, name):
    r, c = w.shape
    tm = _pick(r, (256, 128, 64, 32, 16, 8))

    def body(w_ref, g_ref, m_ref, v_ref, d_ref, nm_ref, nv_ref):
        d_ref[...], nm_ref[...], nv_ref[...] = _adamw_math(w_ref[...], g_ref[...], m_ref[...], v_ref[...])

    spec = pl.BlockSpec((tm, c), lambda i: (i, 0))
    out = jax.ShapeDtypeStruct(w.shape, F32)
    return pl.pallas_call(
        body, name=name, grid=(r // tm,), in_specs=[spec] * 4, out_specs=[spec] * 3, out_shape=[out] * 3,
        compiler_params=_params(("parallel",)),
    )(w, g, m, v)


def _pack(parts, width=128):
    flat = jnp.concatenate([p.reshape(-1).astype(F32) for p in parts])
    n = flat.shape[0]
    total = -(-n // (8 * width)) * (8 * width)
    return jnp.pad(flat, (0, total - n)).reshape(total // width, width)


def _unpack(buf, shapes):
    flat = buf.reshape(-1)
    out, off = [], 0
    for s in shapes:
        n = int(np.prod(s))
        out.append(flat[off:off + n].reshape(s))
        off += n
    return out


def kernel(x, c, ctx, c_ctx, ada_w, ada_b, norm_g, w_in, na_rpb, ret_decay_logit, w_proj_na, w_proj_ret, w_out, final_g, loss_target, m_c_ctx, m_ada_w, m_ada_b, m_norm_g, m_w_in, m_na_rpb, m_ret_decay_logit, m_w_proj_na, m_w_proj_ret, m_w_out, m_final_g, v_c_ctx, v_ada_w, v_ada_b, v_norm_g, v_w_in, v_na_rpb, v_ret_decay_logit, v_w_proj_na, v_w_proj_ret, v_w_out, v_final_g):
    depth = w_in.shape[0]
    n_lat, d = x.shape[1], x.shape[2]
    n_ctx = ctx.shape[1]
    t = n_lat + n_ctx
    w_na = w_proj_na.shape[1]
    w_retv = w_proj_ret.shape[1] * N_DEV
    in_cols = w_in.shape[2] * N_DEV
    w_qk = (in_cols - 4 * w_na - 2 * w_retv - 2 * d) // 2
    sizes = (w_na, w_na, w_na, w_na, w_qk, w_qk, w_retv, w_retv, d, d)
    off = tuple(int(o) for o in np.cumsum((0,) + sizes))
    NA_Q, NA_K, NA_V, NA_Z, RET_Q, RET_K, RET_V, RET_Z, G_NA, G_RET = range(10)
    rows = n_lat // GRID_W
    me = _my_position()
    my_idx = _linear(me)
    tm_row = _pick(n_ctx, (256, 128))

    idx_arr = jnp.reshape(my_idx, (1,)).astype(jnp.int32)
    chip_idx = jnp.stack([_linear(_flip(me, kc)) for kc in CHIP_BITS]).astype(jnp.int32)

    w_axes = (1, 1, 0, 0)
    w_names = ("w_in", "w_proj_na", "w_proj_ret", "w_out")
    shard = [[w[l].astype(BF16) for w in (w_in, w_proj_na, w_proj_ret, w_out)] for l in range(depth)]
    groups = [[(0, 0)], [(0, 1), (0, 2), (0, 3)]] + [[(l, a) for a in range(4)] for l in range(1, depth)]
    gathers = {}

    def start_gather(gi, deps):
        keys = groups[gi]
        srcs = [shard[l][a] for l, a in keys]
        axes = tuple(w_axes[a] for _, a in keys)
        lands = [_place_shard(s, lax.empty(tuple(n * (N_DEV if i == ax else 1) for i, n in enumerate(s.shape)), BF16),
                              ax, idx_arr) for s, ax in zip(srcs, axes)]
        sizes = tuple(s.shape[ax] for s, ax in zip(srcs, axes))
        sems, srcs, lands, tok = _push_start(f"gather_start_{gi}", srcs, lands, axes, _gather_views, GATHER_BITS, deps)
        flight = dict(name=f"gather_wait_{gi}", sems=sems, srcs=srcs, lands=lands, axes=axes, sizes=sizes, ready=None)
        for pos, key in enumerate(keys):
            gathers[key] = (flight, pos)
        return tok

    token = start_gather(0, ())

    ncol = ada_w.shape[2]
    c_all, _ = _small_allgather(jnp.pad(c, ((0, 7), (0, 0))) + token[:, :1], "allgather_c")
    cin = jnp.concatenate([c_all[:, 0, :], c_ctx[None, :], jnp.zeros((7, d), F32)], axis=0)
    ada_b_cols = lax.dynamic_slice_in_dim(ada_b, my_idx * ncol, ncol, axis=1)[:, None, :]
    mod_cols = _ada_fwd_call(cin, ada_w, ada_b_cols)
    mod_gathered, _ = _small_allgather(mod_cols.reshape(depth * 16, ncol), "allgather_mod")
    mod_all = mod_gathered.reshape(N_DEV, depth, 16, ncol).transpose(1, 2, 0, 3).reshape(depth, 16, N_DEV * ncol)
    mod_lat = lax.dynamic_index_in_dim(mod_all, my_idx, axis=1, keepdims=False)
    mod_ctx = mod_all[:, 8, :]
    token = mod_gathered
    for gi in range(1, len(groups)):
        token = start_gather(gi, (token,))

    def landed(l, a, act):
        flight, pos = gathers[(l, a)]
        if flight["ready"] is None:
            arrived = _push_wait(flight["name"], flight["sems"], flight["srcs"], flight["lands"],
                                 flight["axes"], _gather_views, GATHER_BITS, (act, token))[1]
            flight["ready"] = _gather_finish(arrived, flight["axes"], flight["sizes"])
        return flight["ready"][pos]

    pending, scatters = {}, []

    def send_dw(l, a, dw):
        pending[(l, a)] = dw
        if a == 0:
            keys = [(0, 0)] if l == 0 else [(l, b) for b in range(4)]
        elif l == 0 and a == 1:
            keys = [(0, 1), (0, 2), (0, 3)]
        else:
            return None
        srcs = [pending[k] for k in keys]
        axes = tuple(w_axes[b] for _, b in keys)
        sizes = tuple(s.shape[ax] // N_DEV for s, ax in zip(srcs, axes))
        slab_shapes = [tuple(n // (N_DEV if i == ax else 1) for i, n in enumerate(s.shape)) for s, ax in zip(srcs, axes)]
        by_chip = keys == [(0, 0)]
        if by_chip:
            theirs = _pair_exchange(srcs, axes, sizes)
            srcs = [_pair_add(g, p, ax, chip_idx) for g, p, ax in zip(srcs, theirs, axes)]
            lands = [lax.empty((3,) + shp, BF16) for shp in slab_shapes]
            views, bits = _chip_views, CHIP_BITS[1:]
        else:
            lands = [lax.empty((N_DEV - 1,) + shp, BF16) for shp in slab_shapes]
            views, bits = _scatter_views, PEER_BITS
        sems, srcs, lands, tok = _push_start(f"scatter_start_{len(scatters)}", srcs, lands, axes, views, bits, ())
        scatters.append(dict(name=f"scatter_wait_{len(scatters)}", sems=sems, srcs=srcs, lands=lands, axes=axes, keys=keys,
                             views=views, bits=bits, by_chip=by_chip))
        return tok

    cos2, sin2 = _rope_tables(t, n_lat)
    k_scale = RET_KEY_DIM ** -0.5
    assert off[RET_Q] % w_qk == 0 and off[RET_K] % w_qk == 0
    assert off[NA_Z] % w_na == 0 and off[G_NA] % d == 0 and off[G_RET] % d == 0 and off[RET_Z] % w_retv == 0
    assert off[RET_V] % (_ret_head_group(w_retv // RET_VAL_DIM) * RET_VAL_DIM) == 0
    na_cols = tuple(off[i] // NA_HEAD_DIM for i in (NA_Q, NA_K, NA_V))
    norm_mod_fwd, norm_mod_bwd = _make_rowwise(_f_norm_mod, "norm_mod", (BF16,), (d,), n_lat, tm_row, (0,))
    gate_na_fwd, gate_na_bwd = _make_rowwise(_f_gate_na, "gate_na", (BF16,), (w_na,), n_lat, tm_row, (0, 1),
                                             col_blocks={1: (w_na, off[NA_Z] // w_na)})
    merge_fwd, merge_bwd = _make_rowwise(_f_merge, "merge", (BF16,), (d,), n_lat, tm_row, (0, 1, 2, 3),
                                         col_blocks={0: (d, off[G_NA] // d), 1: (d, off[G_RET] // d)},
                                         drow_dtypes={2: BF16, 3: BF16})
    residual_fwd, _ = _make_rowwise(_f_residual, "residual", (F32,), (d,), n_lat, tm_row, (0, 1))
    _, residual_bwd = _make_rowwise(lambda out, gate: (gate * out,), "residual", (F32,), (d,), n_lat, tm_row, (0,),
                                    drow_dtypes={0: BF16})
    loss_fwd, loss_bwd = _make_rowwise(_f_loss, "loss_head", (F32,), (128,), n_lat, tm_row, (0,))

    def pair(a, b):
        return jnp.stack([a, b])[:, None, :]

    def mod_vectors(mod_lat_l, mod_ctx_l, norm_g_l):
        shift, scale, gate = jnp.split(mod_lat_l, 3)
        c_shift, c_scale, c_gate = jnp.split(mod_ctx_l, 3)
        return pair(norm_g_l, norm_g_l), pair(scale, c_scale), pair(shift, c_shift), pair(gate, c_gate)

    def log_decay(logit):
        return jax.nn.log_sigmoid(logit.astype(F32))

    xa = jnp.concatenate([x[0], ctx[0]], axis=0)
    saved = []
    for l in range(depth):
        vecs, vecs_vjp = jax.vjp(mod_vectors, mod_lat[l], mod_ctx[l], norm_g[l])
        (h,) = norm_mod_fwd((xa,), vecs[:3])
        wl_in = landed(l, 0, h)
        u = _matmul(h, wl_in, out_dtype=BF16, name="in_proj_fwd")
        qr, kr = _rope_fwd_call(u, off[RET_Q] // w_qk, off[RET_K] // w_qk, w_qk, cos2, sin2, k_scale, tm_row)
        bt, bt_vjp = jax.vjp(lambda r: _na_bias_table(r, rows), na_rpb[l])
        lam, lam_vjp = jax.vjp(log_decay, ret_decay_logit[l])
        o_na = _na_fwd_call(u, u, u, na_cols, w_na, bt, n_lat)
        o_f, st_f = _ret_fwd_call(qr, kr, u, off[RET_V], lam[0], n_lat, False)
        o_b, st_b = _ret_fwd_call(qr, kr, u, off[RET_V], lam[1], n_lat, True)
        (a_na,) = gate_na_fwd((o_na, u), ())
        a_ret = _gate_ret_fwd_call(o_f, o_b, u, off[RET_Z] // w_retv, tm_row)
        wl_pna, wl_pret, wl_out = landed(l, 1, a_na), landed(l, 2, a_na), landed(l, 3, a_na)
        y_na = _matmul(a_na, wl_pna, out_dtype=BF16, name="proj_na_fwd")
        y_ret = _matmul(a_ret, wl_pret, out_dtype=BF16, name="proj_ret_fwd")
        (merged,) = merge_fwd((u, u, y_na, y_ret), ())
        out = _matmul(merged, wl_out, out_dtype=F32, name="out_proj_fwd")
        (xa_next,) = residual_fwd((xa, out), vecs[3:])
        saved.append(dict(xa=xa, vecs=vecs, vecs_vjp=vecs_vjp, h=h, w=(wl_in, wl_pna, wl_pret, wl_out), u=u, qr=qr, kr=kr,
                          bt=bt, bt_vjp=bt_vjp, lam=lam, lam_vjp=lam_vjp, o_na=o_na, o_f=o_f,
                          o_b=o_b, st_f=st_f, st_b=st_b, a_na=a_na, a_ret=a_ret, y_na=y_na, y_ret=y_ret,
                          merged=merged, out=out))
        xa = xa_next

    fg_pair, fg_vjp = jax.vjp(lambda g: pair(g, g), final_g)
    (loss_rows,) = loss_fwd((xa, loss_target[0]), (fg_pair,))
    loss = lax.psum(jnp.sum(loss_rows), ("x", "y", "c"))
    (dx_last,), (d_fg_pair,) = loss_bwd((xa, loss_target[0]), (fg_pair,), (jnp.ones_like(loss_rows),))
    (d_final_g,) = fg_vjp(d_fg_pair)
    dxa = jnp.pad(dx_last, ((0, n_ctx), (0, 0)))

    d_mod_lat, d_mod_ctx, d_norm_g, d_rpb, d_decay = ([None] * depth for _ in range(5))
    for l in reversed(range(depth)):
        s = saved[l]
        u, qr, kr = s["u"], s["qr"], s["kr"]
        wl_in, wl_pna, wl_pret, wl_out = s["w"]
        (d_out,), (d_gate,) = residual_bwd((s["out"],), s["vecs"][3:], (dxa,))
        dxa_res = dxa
        send_dw(l, 3, _matmul(s["merged"], d_out, trans_a=True, out_dtype=BF16, name="out_proj_dw"))
        d_merged = _matmul(d_out, wl_out, trans_b=True, out_dtype=BF16, name="out_proj_da")
        (dg_na, dg_ret, dy_na, dy_ret), _ = merge_bwd((u, u, s["y_na"], s["y_ret"]), (), (d_merged,))
        send_dw(l, 2, _matmul(s["a_ret"], dy_ret, trans_a=True, out_dtype=BF16, name="proj_ret_dw"))
        da_ret = _matmul(dy_ret, wl_pret, trans_b=True, out_dtype=BF16, name="proj_ret_da")
        tok = send_dw(l, 1, _matmul(s["a_na"], dy_na, trans_a=True, out_dtype=BF16, name="proj_na_dw"))
        da_na = _matmul(dy_na, wl_pna, trans_b=True, out_dtype=BF16, name="proj_na_da", after=tok)
        do_ret, dz_ret = _gate_ret_bwd_call(s["o_f"], s["o_b"], u, off[RET_Z] // w_retv, da_ret, tm_row)
        (do_na, dz_na), _ = gate_na_bwd((s["o_na"], u), (), (da_na,))
        dq_f, dk_f, dv_f, dl_f = _ret_bwd_call(qr, kr, u, off[RET_V], s["lam"][0], s["st_f"], do_ret, n_lat, False)
        dq_b, dk_b, dv_b, dl_b = _ret_bwd_call(qr, kr, u, off[RET_V], s["lam"][1], s["st_b"], do_ret, n_lat, True)
        dq, dk, dv, dbt = _na_bwd_call(u, u, u, na_cols, w_na, s["bt"], do_na, n_lat)
        du = _assemble_du_call([dq, dk, dv, dz_na, (dq_f, dq_b), (dk_f, dk_b), (dv_f, dv_b), dz_ret, dg_na, dg_ret],
                               off, cos2, sin2, k_scale, _pick(n_ctx, (128,)))
        (d_rpb[l],) = s["bt_vjp"](dbt)
        (d_decay[l],) = s["lam_vjp"](jnp.stack([dl_f[:, 0, 0], dl_b[:, 0, 0]]))
        tok = send_dw(l, 0, _matmul(s["h"], du, trans_a=True, out_dtype=BF16, name="in_proj_dw"))
        dh = _matmul(du, wl_in, trans_b=True, out_dtype=BF16, name="in_proj_da", after=tok)
        (dxa,), d_vecs = norm_mod_bwd((s["xa"],), s["vecs"][:3], (dh,), acc=(dxa_res,))
        d_mod_lat[l], d_mod_ctx[l], d_norm_g[l] = s["vecs_vjp"](tuple(d_vecs) + (d_gate,))
    gx = dxa[:n_lat]
    d_mod_lat, d_mod_ctx, d_norm_g, d_rpb, d_decay = (jnp.stack(a) for a in (d_mod_lat, d_mod_ctx, d_norm_g, d_rpb, d_decay))

    small_shapes = [d_mod_lat.shape, d_mod_ctx.shape, d_norm_g.shape, d_final_g.shape, d_rpb.shape, d_decay.shape]
    packed = _pack([d_mod_lat, d_mod_ctx, d_norm_g, d_final_g, d_rpb, d_decay])
    g_all, g_sum = _small_allgather(packed, "allgather_small_grads")
    dml_sum, dmc_sum, grad_norm_g, grad_final_g, grad_na_rpb, grad_decay = _unpack(g_sum, small_shapes)
    grad_ada_b = dml_sum + dmc_sum
    dml_all = g_all.reshape(N_DEV, -1)[:, :depth * 3 * d].reshape(N_DEV, depth, 3 * d)

    def my_cols(a):
        return lax.dynamic_slice_in_dim(a, my_idx * ncol, ncol, axis=a.ndim - 1)

    dmod = jnp.concatenate(
        [my_cols(dml_all).transpose(1, 0, 2), my_cols(dmc_sum)[:, None, :], jnp.zeros((depth, 7, ncol), F32)], axis=1)
    grad_ada_w, dcs_part = _ada_bwd_call(cin, ada_w, dmod)
    _, dcs = _small_allgather(dcs_part, "allgather_dcsilu")
    sg = jax.nn.sigmoid(c_ctx)
    grad_c_ctx = dcs[8] * (sg * (1.0 + c_ctx * (1.0 - sg)))

    def flat2(a):
        return a.reshape(a.shape[0] * a.shape[1], a.shape[2])

    small_w = [c_ctx, ada_b, norm_g, na_rpb, ret_decay_logit, final_g]
    small_g = [grad_c_ctx, grad_ada_b, grad_norm_g, grad_na_rpb, grad_decay, grad_final_g]
    small_m = [m_c_ctx, m_ada_b, m_norm_g, m_na_rpb, m_ret_decay_logit, m_final_g]
    small_v = [v_c_ctx, v_ada_b, v_norm_g, v_na_rpb, v_ret_decay_logit, v_final_g]
    shp = [a.shape for a in small_w]
    ds_, nms_, nvs_ = _adamw_dense(_pack(small_w), _pack(small_g), _pack(small_m), _pack(small_v), "adamw_small")
    ds_, nms_, nvs_ = _unpack(ds_, shp), _unpack(nms_, shp), _unpack(nvs_, shp)

    d_ada, nm_ada, nv_ada = [a.reshape(ada_w.shape) for a in _adamw_dense(
        flat2(ada_w), flat2(grad_ada_w), flat2(m_ada_w), flat2(v_ada_w), "adamw_ada_w")]

    w_all = (w_in, w_proj_na, w_proj_ret, w_out)
    m_all = (m_w_in, m_w_proj_na, m_w_proj_ret, m_w_out)
    v_all = (v_w_in, v_w_proj_na, v_w_proj_ret, v_w_out)
    upd = [None] * 4
    after = d_ada
    for flight in scatters:
        mine, slabs = _push_wait(flight["name"], flight["sems"], flight["srcs"], flight["lands"], flight["axes"],
                                 flight["views"], flight["bits"], (after,))
        for (l, a), own, s in zip(flight["keys"], mine, slabs):
            upd[a] = _adamw_sharded(w_all[a], m_all[a], v_all[a], own, s, None if flight["by_chip"] else w_axes[a],
                                    idx_arr, l, upd[a], "adamw_" + w_names[a])
            after = upd[a][1]
    (g_w_in, d_w_in, nm_w_in, nv_w_in), (g_pna, d_pna, nm_pna, nv_pna) = upd[0], upd[1]
    (g_pret, d_pret, nm_pret, nv_pret), (g_out, d_out, nm_out, nv_out) = upd[2], upd[3]

    def order(cc, aw, ab, ng, wi, rp, dl, pn, pr, wo, fg):
        return [cc, aw, ab, ng, wi, rp, dl, pn, pr, wo, fg]

    grads_out = order(grad_c_ctx, grad_ada_w, grad_ada_b, grad_norm_g, g_w_in, grad_na_rpb, grad_decay, g_pna, g_pret, g_out, grad_final_g)
    delta_out = order(ds_[0], d_ada, ds_[1], ds_[2], d_w_in, ds_[3], ds_[4], d_pna, d_pret, d_out, ds_[5])
    m_out = order(nms_[0], nm_ada, nms_[1], nms_[2], nm_w_in, nms_[3], nms_[4], nm_pna, nm_pret, nm_out, nms_[5])
    v_out = order(nvs_[0], nv_ada, nvs_[1], nvs_[2], nv_w_in, nvs_[3], nvs_[4], nv_pna, nv_pret, nv_out, nvs_[5])
    return (loss, gx[None], *grads_out, *delta_out, *m_out, *v_out)
```

```python
import numpy as np
import jax
import jax.numpy as jnp
from jax import lax
from jax.experimental import pallas as pl
from jax.experimental.pallas import tpu as pltpu

F32 = jnp.float32
BF16 = jnp.bfloat16

N_DEV = 8
GRID_W = 64
NA_HEAD_DIM = 128
NA_WIN_ROWS = 8
NA_WIN_COLS = 16
RET_KEY_DIM = 128
RET_VAL_DIM = 256
RET_CHUNK = 128
ROPE_BASE = 10000.0
NORM_EPS = 1e-6
MASK_VALUE = -1e30

ADAM_LR = 0.001
ADAM_B1 = 0.9
ADAM_B2 = 0.999
ADAM_EPS = 1e-08
ADAM_WD = 0.01
ADAM_STEP = 10

VMEM_LIMIT = 48 * 1024 * 1024
MESH = pl.DeviceIdType.MESH
ANY = pl.BlockSpec(memory_space=pl.ANY)
VMEM_SPEC = pl.BlockSpec(memory_space=pltpu.VMEM)


def _params(sem=None):
    return pltpu.CompilerParams(dimension_semantics=sem, vmem_limit_bytes=VMEM_LIMIT)


def _pick(n, prefs):
    for p in prefs:
        if n % p == 0:
            return p
    return n


def _dot(a, b):
    return lax.dot_general(a, b, (((1,), (0,)), ((), ())), preferred_element_type=F32)


def _dot_nt(a, b):
    return lax.dot_general(a, b, (((1,), (1,)), ((), ())), preferred_element_type=F32)


def _dot_tn(a, b):
    return lax.dot_general(a, b, (((0,), (0,)), ((), ())), preferred_element_type=F32)


def _silu(x):
    return x * jax.nn.sigmoid(x)


def _matmul(a, b, *, trans_a=False, trans_b=False, out_dtype=F32, name="matmul", after=None):
    if trans_a:
        kdim, m = a.shape
    else:
        m, kdim = a.shape
    if trans_b:
        n, kb = b.shape
    else:
        kb, n = b.shape
    assert kdim == kb, (a.shape, b.shape, trans_a, trans_b)
    tm = _pick(m, (1152, 1024, 768, 512, 256, 128))
    tn = _pick(n, (1024, 512, 256, 128))
    tk = _pick(kdim, (2304, 2048, 1024, 512, 256, 128))
    nk = kdim // tk
    dn = (((0 if trans_a else 1,), (1 if trans_b else 0,)), ((), ()))

    def body(a_ref, b_ref, *rest):
        o_ref, acc_ref = rest[-2:]
        part = lax.dot_general(a_ref[...], b_ref[...], dn, preferred_element_type=F32)
        if nk == 1:
            o_ref[...] = part.astype(o_ref.dtype)
        else:
            k = pl.program_id(2)

            @pl.when(k == 0)
            def _():
                acc_ref[...] = part

            @pl.when(k > 0)
            def _():
                acc_ref[...] += part

            @pl.when(k == nk - 1)
            def _():
                o_ref[...] = acc_ref[...].astype(o_ref.dtype)

    a_spec = pl.BlockSpec((tk, tm), lambda i, j, k: (k, i)) if trans_a else pl.BlockSpec((tm, tk), lambda i, j, k: (i, k))
    b_spec = pl.BlockSpec((tn, tk), lambda i, j, k: (j, k)) if trans_b else pl.BlockSpec((tk, tn), lambda i, j, k: (k, j))
    return pl.pallas_call(
        body,
        name=name,
        grid=(m // tm, n // tn, nk),
        in_specs=[a_spec, b_spec] + ([] if after is None else [ANY]),
        out_specs=pl.BlockSpec((tm, tn), lambda i, j, k: (i, j)),
        out_shape=jax.ShapeDtypeStruct((m, n), out_dtype),
        scratch_shapes=[pltpu.VMEM((tm, tn) if nk > 1 else (8, 128), F32)],
        compiler_params=_params(("parallel", "parallel", "arbitrary")),
    )(*((a, b) if after is None else (a, b, after)))


def _make_rowwise(f, name, out_dtypes, out_cols, n_lat, tm, diff_rows, col_blocks=None, drow_dtypes=None):
    drow_dtypes = drow_dtypes or {}

    def tile_fn(*args):
        return tuple(o.astype(dt) for o, dt in zip(f(*args), out_dtypes))

    def row_spec(k, arr):
        width, index = (col_blocks or {}).get(k, (arr.shape[1], 0))
        return pl.BlockSpec((tm, width), lambda i: (i, index))

    def row_width(k, arr):
        return (col_blocks or {}).get(k, (arr.shape[1], 0))[0]

    def fwd_call(rows, vecs):
        t = min(r.shape[0] for r in rows)
        nr, nv = len(rows), len(vecs)
        nl = n_lat // tm

        def body(*refs):
            grp = (pl.program_id(0) >= nl).astype(jnp.int32)
            args = [r[...] for r in refs[:nr]] + [v[grp] for v in refs[nr:nr + nv]]
            for o_ref, o in zip(refs[nr + nv:], tile_fn(*args)):
                o_ref[...] = o

        return pl.pallas_call(
            body,
            name=name + "_fwd",
            grid=(t // tm,),
            in_specs=[row_spec(k, r) for k, r in enumerate(rows)]
            + [pl.BlockSpec(v.shape, lambda i: (0, 0, 0)) for v in vecs],
            out_specs=[pl.BlockSpec((tm, c), lambda i: (i, 0)) for c in out_cols],
            out_shape=[jax.ShapeDtypeStruct((t, c), dt) for c, dt in zip(out_cols, out_dtypes)],
            compiler_params=_params(("parallel",)),
        )(*rows, *vecs)

    def bwd_call(rows, vecs, gs, acc=None):
        t = min(r.shape[0] for r in rows)
        nr, nv, ng = len(rows), len(vecs), len(gs)
        nl = n_lat // tm
        nd = len(diff_rows)
        acc = [None] * nd if acc is None else list(acc)
        acc_in = [a for a in acc if a is not None]

        def body(*refs):
            i = pl.program_id(0)
            grp = (i >= nl).astype(jnp.int32)
            args = [r[...] for r in refs[:nr]] + [v[grp] for v in refs[nr:nr + nv]]
            g_refs = refs[nr + nv:nr + nv + ng]
            acc_refs = list(refs[nr + nv + ng:nr + nv + ng + len(acc_in)])
            drow_refs = refs[nr + nv + ng + len(acc_in):nr + nv + ng + len(acc_in) + nd]
            dvec_refs = refs[nr + nv + ng + len(acc_in) + nd:]
            _, vjp = jax.vjp(tile_fn, *args)
            grads = vjp(tuple(g[...] for g in g_refs))
            for d_ref, k, a in zip(drow_refs, diff_rows, acc):
                gk = grads[k] if a is None else grads[k] + acc_refs.pop(0)[...]
                d_ref[...] = gk.astype(d_ref.dtype)

            @pl.when(i == 0)
            def _():
                for d_ref in dvec_refs:
                    d_ref[...] = jnp.zeros_like(d_ref)

            for j, d_ref in enumerate(dvec_refs):
                d_ref[grp] += grads[nr + j]

        outs = pl.pallas_call(
            body,
            name=name + "_bwd",
            grid=(t // tm,),
            in_specs=[row_spec(k, r) for k, r in enumerate(rows)]
            + [pl.BlockSpec(v.shape, lambda i: (0, 0, 0)) for v in vecs]
            + [pl.BlockSpec((tm, g.shape[1]), lambda i: (i, 0)) for g in gs]
            + [pl.BlockSpec((tm, a.shape[1]), lambda i: (i, 0)) for a in acc_in],
            out_specs=[pl.BlockSpec((tm, row_width(k, rows[k])), lambda i: (i, 0)) for k in diff_rows]
            + [pl.BlockSpec(v.shape, lambda i: (0, 0, 0)) for v in vecs],
            out_shape=[jax.ShapeDtypeStruct((t, row_width(k, rows[k])), drow_dtypes.get(k, rows[k].dtype))
                       for k in diff_rows]
            + [jax.ShapeDtypeStruct(v.shape, F32) for v in vecs],
            compiler_params=_params(("arbitrary",)),
        )(*rows, *vecs, *gs, *acc_in)
        return outs[:nd], outs[nd:]

    return fwd_call, bwd_call


def _f_norm_mod(x, g, scale, shift):
    r = lax.rsqrt(jnp.mean(x * x, axis=-1, keepdims=True) + NORM_EPS)
    return ((x * r * g) * (1.0 + scale) + shift,)


def _f_gate_na(o, z):
    return (o.astype(F32) * _silu(z.astype(F32)),)


def _f_merge(g_na, g_ret, y_na, y_ret):
    return (jax.nn.sigmoid(g_na.astype(F32)) * y_na.astype(F32) + jax.nn.sigmoid(g_ret.astype(F32)) * y_ret.astype(F32),)


def _f_residual(x, out, gate):
    return (x + gate * out,)


def _f_loss(x, target, g):
    r = lax.rsqrt(jnp.mean(x * x, axis=-1, keepdims=True) + NORM_EPS)
    y = x * r * g
    e = 0.5 * jnp.mean(jnp.square(y - target), axis=-1, keepdims=True)
    return (jnp.broadcast_to(e * (1.0 / 128.0), (x.shape[0], 128)),)


def _gate_ret_fwd_call(of, ob, z, zblk, tm):
    t, w = of.shape
    nh = w // RET_VAL_DIM

    def body(of_ref, ob_ref, z_ref, a_ref):
        for hh in range(nh):
            sl = slice(hh * RET_VAL_DIM, (hh + 1) * RET_VAL_DIM)
            o = of_ref[:, sl].astype(F32) + ob_ref[:, sl].astype(F32)
            r = lax.rsqrt(jnp.mean(o * o, axis=-1, keepdims=True) + NORM_EPS)
            a_ref[:, sl] = ((o * r) * _silu(z_ref[:, sl].astype(F32))).astype(a_ref.dtype)

    spec = pl.BlockSpec((tm, w), lambda i: (i, 0))
    zspec = pl.BlockSpec((tm, w), lambda i: (i, zblk))
    return pl.pallas_call(
        body, name="gate_ret_fwd", grid=(t // tm,), in_specs=[spec, spec, zspec], out_specs=spec,
        out_shape=jax.ShapeDtypeStruct((t, w), BF16), compiler_params=_params(("parallel",)),
    )(of, ob, z)


def _gate_ret_bwd_call(of, ob, z, zblk, da, tm):
    t, w = of.shape
    nh = w // RET_VAL_DIM

    def body(of_ref, ob_ref, z_ref, da_ref, do_ref, dz_ref):
        for hh in range(nh):
            sl = slice(hh * RET_VAL_DIM, (hh + 1) * RET_VAL_DIM)
            o = of_ref[:, sl].astype(F32) + ob_ref[:, sl].astype(F32)
            r = lax.rsqrt(jnp.mean(o * o, axis=-1, keepdims=True) + NORM_EPS)
            n = o * r
            zf = z_ref[:, sl].astype(F32)
            sg = jax.nn.sigmoid(zf)
            g = da_ref[:, sl].astype(F32)
            dn = g * (zf * sg)
            dz_ref[:, sl] = (g * n * (sg * (1.0 + zf * (1.0 - sg)))).astype(dz_ref.dtype)
            do_ref[:, sl] = (r * (dn - n * jnp.mean(dn * n, axis=-1, keepdims=True))).astype(do_ref.dtype)

    spec = pl.BlockSpec((tm, w), lambda i: (i, 0))
    zspec = pl.BlockSpec((tm, w), lambda i: (i, zblk))
    return pl.pallas_call(
        body, name="gate_ret_bwd", grid=(t // tm,), in_specs=[spec, spec, zspec, spec], out_specs=[spec, spec],
        out_shape=[jax.ShapeDtypeStruct((t, w), BF16), jax.ShapeDtypeStruct((t, w), z.dtype)],
        compiler_params=_params(("parallel",)),
    )(of, ob, z, da)


NA_PAIR = 2 * GRID_W
NA_KEY_ROWS = NA_WIN_ROWS + 2
NA_CLASSES = 5


def _na_geometry(t, n_lat):
    rows = n_lat // GRID_W
    assert rows % 2 == 0 and rows >= NA_KEY_ROWS + 2, rows
    return rows, rows // 2, NA_KEY_ROWS * GRID_W, t - n_lat, t // NA_PAIR


def _na_base(p, rows):
    return jnp.clip(2 * p - NA_WIN_ROWS // 2, 0, rows - NA_KEY_ROWS)


def _na_class(p, rows):
    return p - _na_base(p, rows) // 2


def _na_group(pairs, n_ctx):
    assert n_ctx % NA_PAIR == 0, n_ctx
    return 2 if pairs % 2 == 0 and (n_ctx // NA_PAIR) % 2 == 0 else 1


def _na_bias_spec():
    return pl.BlockSpec((1, NA_CLASSES, 2, NA_KEY_ROWS // 2, GRID_W, NA_PAIR), lambda h, g: (h, 0, 0, 0, 0, 0))


def _na_bias_tile(bt_ref, cls):
    return jnp.concatenate(
        [jnp.concatenate([bt_ref[0, cls, i, q] for q in range(NA_KEY_ROWS // 2)], axis=1) for i in range(2)], axis=0)


def _na_add_bias_grad(dbt_ref, cls, ds):
    for i in range(2):
        for q in range(NA_KEY_ROWS // 2):
            dbt_ref[0, cls, i, q] += ds[i * GRID_W:(i + 1) * GRID_W, q * NA_PAIR:(q + 1) * NA_PAIR]


def _na_fwd_call(q, k, v, col0, w, bt, n_lat):
    t = q.shape[0]
    nh = w // NA_HEAD_DIM
    rows, pairs, n_loc, n_ctx, nq = _na_geometry(t, n_lat)
    grp = _na_group(pairs, n_ctx)
    scale = NA_HEAD_DIM ** -0.5

    def body(q_ref, k_ref, v_ref, bt_ref, o_ref):
        g = pl.program_id(1)
        kc = k_ref[pl.ds(n_lat, n_ctx), :]
        vc = v_ref[pl.ds(n_lat, n_ctx), :]

        @pl.when(g < pairs // grp)
        def _():
            for i in range(grp):
                p = g * grp + i
                sl = slice(i * NA_PAIR, (i + 1) * NA_PAIR)
                qb = q_ref[sl, :]
                s_ctx = _dot_nt(qb, kc) * scale
                start = pl.multiple_of(_na_base(p, rows) * GRID_W, GRID_W)
                kw = k_ref[pl.ds(start, n_loc), :]
                vw = v_ref[pl.ds(start, n_loc), :]
                s_loc = _dot_nt(qb, kw) * scale + _na_bias_tile(bt_ref, _na_class(p, rows))
                m = jnp.maximum(jnp.max(s_loc, axis=-1, keepdims=True), jnp.max(s_ctx, axis=-1, keepdims=True))
                p_loc = jnp.exp(s_loc - m)
                p_ctx = jnp.exp(s_ctx - m)
                l = jnp.sum(p_loc, axis=-1, keepdims=True) + jnp.sum(p_ctx, axis=-1, keepdims=True)
                o = _dot(p_loc.astype(BF16), vw) + _dot(p_ctx.astype(BF16), vc)
                o_ref[sl, :] = (o / l).astype(o_ref.dtype)

        @pl.when(g >= pairs // grp)
        def _():
            s_ctx = _dot_nt(q_ref[...], kc) * scale
            m = jnp.max(s_ctx, axis=-1, keepdims=True)
            p = jnp.exp(s_ctx - m)
            l = jnp.sum(p, axis=-1, keepdims=True)
            o_ref[...] = (_dot(p.astype(BF16), vc) / l).astype(o_ref.dtype)

    qspec = pl.BlockSpec((grp * NA_PAIR, NA_HEAD_DIM), lambda h, g: (g, h))
    in_q = pl.BlockSpec((grp * NA_PAIR, NA_HEAD_DIM), lambda h, g: (g, col0[0] + h))
    in_k = pl.BlockSpec((t, NA_HEAD_DIM), lambda h, g: (0, col0[1] + h))
    in_v = pl.BlockSpec((t, NA_HEAD_DIM), lambda h, g: (0, col0[2] + h))
    return pl.pallas_call(
        body,
        name="na_attn_fwd",
        grid=(nh, nq // grp),
        in_specs=[in_q, in_k, in_v, _na_bias_spec()],
        out_specs=qspec,
        out_shape=jax.ShapeDtypeStruct((t, w), BF16),
        compiler_params=_params(("parallel", "arbitrary")),
    )(q, k, v, bt)


def _na_bwd_call(q, k, v, col0, w, bt, do, n_lat):
    t = q.shape[0]
    nh = w // NA_HEAD_DIM
    rows, pairs, n_loc, n_ctx, nq = _na_geometry(t, n_lat)
    scale = NA_HEAD_DIM ** -0.5
    grp = _na_group(pairs, n_ctx)

    def body(q_ref, k_ref, v_ref, do_ref, bt_ref, dq_ref, dk_ref, dv_ref, dbt_ref):
        g = pl.program_id(1)

        @pl.when(g == 0)
        def _():
            dk_ref[...] = jnp.zeros_like(dk_ref)
            dv_ref[...] = jnp.zeros_like(dv_ref)
            dbt_ref[...] = jnp.zeros_like(dbt_ref)

        kc = k_ref[pl.ds(n_lat, n_ctx), :]
        vc = v_ref[pl.ds(n_lat, n_ctx), :]

        @pl.when(g < pairs // grp)
        def _():
            for i in range(grp):
                p = g * grp + i
                sl = slice(i * NA_PAIR, (i + 1) * NA_PAIR)
                qb = q_ref[sl, :]
                dob = do_ref[sl, :]
                s_ctx = _dot_nt(qb, kc) * scale
                dp_ctx = _dot_nt(dob, vc)
                start = pl.multiple_of(_na_base(p, rows) * GRID_W, GRID_W)
                kw = k_ref[pl.ds(start, n_loc), :]
                vw = v_ref[pl.ds(start, n_loc), :]
                cls = _na_class(p, rows)
                s_loc = _dot_nt(qb, kw) * scale + _na_bias_tile(bt_ref, cls)
                m = jnp.maximum(jnp.max(s_loc, axis=-1, keepdims=True), jnp.max(s_ctx, axis=-1, keepdims=True))
                p_loc = jnp.exp(s_loc - m)
                p_ctx = jnp.exp(s_ctx - m)
                inv = 1.0 / (jnp.sum(p_loc, axis=-1, keepdims=True) + jnp.sum(p_ctx, axis=-1, keepdims=True))
                p_loc = p_loc * inv
                p_ctx = p_ctx * inv
                dp_loc = _dot_nt(dob, vw)
                delta = (jnp.sum(p_loc * dp_loc, axis=-1, keepdims=True)
                         + jnp.sum(p_ctx * dp_ctx, axis=-1, keepdims=True))
                ds_loc = p_loc * (dp_loc - delta)
                ds_ctx = p_ctx * (dp_ctx - delta)
                _na_add_bias_grad(dbt_ref, cls, ds_loc)
                dsl = (ds_loc * scale).astype(BF16)
                dsc = (ds_ctx * scale).astype(BF16)
                dq_ref[sl, :] = (_dot(dsl, kw) + _dot(dsc, kc)).astype(dq_ref.dtype)
                dk_ref[pl.ds(start, n_loc), :] += _dot_tn(dsl, qb)
                dv_ref[pl.ds(start, n_loc), :] += _dot_tn(p_loc.astype(BF16), dob)
                dk_ref[pl.ds(n_lat, n_ctx), :] += _dot_tn(dsc, qb)
                dv_ref[pl.ds(n_lat, n_ctx), :] += _dot_tn(p_ctx.astype(BF16), dob)

        @pl.when(g >= pairs // grp)
        def _():
            qb = q_ref[...]
            dob = do_ref[...]
            s_ctx = _dot_nt(qb, kc) * scale
            dp_ctx = _dot_nt(dob, vc)
            m = jnp.max(s_ctx, axis=-1, keepdims=True)
            p = jnp.exp(s_ctx - m)
            p = p * (1.0 / jnp.sum(p, axis=-1, keepdims=True))
            delta = jnp.sum(p * dp_ctx, axis=-1, keepdims=True)
            dsc = (p * (dp_ctx - delta) * scale).astype(BF16)
            dq_ref[...] = _dot(dsc, kc).astype(dq_ref.dtype)
            dk_ref[pl.ds(n_lat, n_ctx), :] += _dot_tn(dsc, qb)
            dv_ref[pl.ds(n_lat, n_ctx), :] += _dot_tn(p.astype(BF16), dob)

    qspec = pl.BlockSpec((grp * NA_PAIR, NA_HEAD_DIM), lambda h, g: (g, h))
    kspec = pl.BlockSpec((t, NA_HEAD_DIM), lambda h, g: (0, h))
    return pl.pallas_call(
        body,
        name="na_attn_bwd",
        grid=(nh, nq // grp),
        in_specs=[pl.BlockSpec((grp * NA_PAIR, NA_HEAD_DIM), lambda h, g: (g, col0[0] + h)),
                  pl.BlockSpec((t, NA_HEAD_DIM), lambda h, g: (0, col0[1] + h)),
                  pl.BlockSpec((t, NA_HEAD_DIM), lambda h, g: (0, col0[2] + h)), qspec, _na_bias_spec()],
        out_specs=[qspec, kspec, kspec, _na_bias_spec()],
        out_shape=[
            jax.ShapeDtypeStruct((t, w), BF16),
            jax.ShapeDtypeStruct((t, w), F32),
            jax.ShapeDtypeStruct((t, w), F32),
            jax.ShapeDtypeStruct(bt.shape, F32),
        ],
        compiler_params=_params(("parallel", "arbitrary")),
    )(q, k, v, do, bt)


def _na_bias_table(rpb, rows):
    pairs = rows // 2
    nb = 2 * NA_WIN_COLS - 1
    nq = NA_KEY_ROWS // 2
    e1 = np.zeros((NA_CLASSES, 2, nq, 2, 2 * NA_WIN_ROWS - 1), np.float32)
    valid = np.zeros((NA_CLASSES, 2, nq, 2), bool)
    for cls, p in enumerate((0, 1, 2, pairs - 2, pairs - 1)):
        base = int(np.clip(2 * p - NA_WIN_ROWS // 2, 0, rows - NA_KEY_ROWS))
        assert p - base // 2 == cls, (rows, cls, p, base)
        for i in range(2):
            r = 2 * p + i
            r0 = int(np.clip(r - NA_WIN_ROWS // 2, 0, rows - NA_WIN_ROWS))
            for kk in range(NA_KEY_ROWS):
                if r0 <= base + kk < r0 + NA_WIN_ROWS:
                    valid[cls, i, kk // 2, kk % 2] = True
                    e1[cls, i, kk // 2, kk % 2, base + kk - r + NA_WIN_ROWS - 1] = 1.0
    cidx = np.arange(GRID_W)
    dc = np.clip(cidx[None, :] - cidx[:, None] + (NA_WIN_COLS - 1), 0, nb - 1)
    c0 = np.clip(cidx - NA_WIN_COLS // 2, 0, GRID_W - NA_WIN_COLS)
    col_in = (cidx[None, :] >= c0[:, None]) & (cidx[None, :] < c0[:, None] + NA_WIN_COLS)
    e2 = np.zeros((GRID_W, 2, GRID_W, 2, nb), np.float32)
    for par in range(2):
        e2[np.arange(GRID_W)[:, None], par, np.arange(GRID_W)[None, :], par, dc] = 1.0
    mask = valid[:, :, :, None, :, None] & col_in[None, None, None, :, None, :]
    t1 = jnp.einsum("hab,xiqpa->hxiqpb", rpb, jnp.asarray(e1), precision=lax.Precision.HIGHEST)
    t1 = t1.reshape(t1.shape[:4] + (2 * nb,))
    b = jnp.einsum("hxiqm,cwm->hxiqcw", t1, jnp.asarray(e2.reshape(GRID_W, 2 * GRID_W, 2 * nb)),
                   precision=lax.Precision.HIGHEST)
    return jnp.where(jnp.asarray(mask.reshape(NA_CLASSES, 2, nq, GRID_W, 2 * GRID_W))[None], b, MASK_VALUE)


def _ret_decays(lam_s, reverse):
    c = RET_CHUNK
    ii = lax.broadcasted_iota(jnp.int32, (c, c), 0)
    jj = lax.broadcasted_iota(jnp.int32, (c, c), 1)
    d = (jj - ii) if reverse else (ii - jj)
    dpos = jnp.maximum(d.astype(F32), 0.0)
    mask = jnp.where(d >= 0, jnp.exp(dpos * lam_s), 0.0)
    pi = lax.broadcasted_iota(jnp.int32, (c, 1), 0).astype(F32)
    qpos = (c - pi) if reverse else (pi + 1.0)
    kpos = pi if reverse else (c - 1.0 - pi)
    qd = jnp.exp(qpos * lam_s)
    kd = jnp.exp(kpos * lam_s)
    g = jnp.exp(jnp.full((1, RET_VAL_DIM), c * lam_s, F32))
    return mask, dpos, qd, kd, qpos, kpos, g


def _ret_head_group(nh):
    return _pick(nh, (8, 4, 2))


def _ret_chunk_of(t, nt, nl, reverse):
    return (nt - 1 - t) if reverse else (t + nl) % nt


def _ret_fwd_call(qr, kr, v, vcol, lam, n_lat, reverse):
    t = qr.shape[0]
    nh = qr.shape[1] // RET_KEY_DIM
    c = RET_CHUNK
    nt, nl = t // c, n_lat // c

    hg = _ret_head_group(nh)
    dk, dv = RET_KEY_DIM, RET_VAL_DIM

    def body(lam_ref, q_ref, k_ref, v_ref, o_ref, s_ref, state):
        hb, step = pl.program_id(0), pl.program_id(1)

        @pl.when(step == 0)
        def _():
            state[...] = jnp.zeros_like(state)

        for j in range(hg):
            mask, _, qd, kd, _, _, g = _ret_decays(lam_ref[hb * hg + j], reverse)
            q, k, vv = q_ref[:, j * dk:(j + 1) * dk], k_ref[:, j * dk:(j + 1) * dk], v_ref[:, j * dv:(j + 1) * dv]
            p = _dot_nt(q, k) * mask
            s = state[j]
            qs = (q.astype(F32) * qd).astype(BF16)
            o_ref[:, j * dv:(j + 1) * dv] = (_dot(p.astype(BF16), vv) + _dot(qs, s.astype(BF16))).astype(o_ref.dtype)
            s_ref[j, 0] = s
            ks = (k.astype(F32) * kd).astype(BF16)
            state[j] = s * g + _dot_tn(ks, vv)

    def cmap(hb, step, lam_ref):
        return (_ret_chunk_of(step, nt, nl, reverse), hb)

    def vmap(hb, step, lam_ref):
        return (_ret_chunk_of(step, nt, nl, reverse), vcol // (hg * dv) + hb)

    return pl.pallas_call(
        body,
        name="retention_rev_fwd" if reverse else "retention_fwd",
        grid_spec=pltpu.PrefetchScalarGridSpec(
            num_scalar_prefetch=1,
            grid=(nh // hg, nt),
            in_specs=[
                pl.BlockSpec((c, hg * dk), cmap),
                pl.BlockSpec((c, hg * dk), cmap),
                pl.BlockSpec((c, hg * dv), vmap),
            ],
            out_specs=[
                pl.BlockSpec((c, hg * dv), cmap),
                pl.BlockSpec((hg, 1, dk, dv), lambda hb, step, lam_ref: (hb, step, 0, 0)),
            ],
            scratch_shapes=[pltpu.VMEM((hg, dk, dv), F32)],
        ),
        out_shape=[
            jax.ShapeDtypeStruct((t, nh * RET_VAL_DIM), BF16),
            jax.ShapeDtypeStruct((nh, nt, RET_KEY_DIM, RET_VAL_DIM), F32),
        ],
        compiler_params=_params(("parallel", "arbitrary")),
    )(lam, qr, kr, v)


def _ret_bwd_call(qr, kr, v, vcol, lam, states, do, n_lat, reverse):
    assert RET_CHUNK == RET_KEY_DIM and RET_VAL_DIM % RET_KEY_DIM == 0
    t = qr.shape[0]
    nh = qr.shape[1] // RET_KEY_DIM
    c = RET_CHUNK
    nt, nl = t // c, n_lat // c

    hg = _ret_head_group(nh)
    dk, dv = RET_KEY_DIM, RET_VAL_DIM

    def body(lam_ref, q_ref, k_ref, v_ref, s_ref, do_ref, dq_ref, dk_ref, dv_ref, dl_ref, dstate):
        hb, rstep = pl.program_id(0), pl.program_id(1)

        @pl.when(rstep == 0)
        def _():
            dstate[...] = jnp.zeros_like(dstate)
            dl_ref[...] = jnp.zeros_like(dl_ref)

        for j in range(hg):
            mask, dpos, qd, kd, qpos, kpos, g = _ret_decays(lam_ref[hb * hg + j], reverse)
            ksl, vsl = slice(j * dk, (j + 1) * dk), slice(j * dv, (j + 1) * dv)
            q, k, vv = q_ref[:, ksl], k_ref[:, ksl], v_ref[:, vsl]
            qf, kf = q.astype(F32), k.astype(F32)
            s = s_ref[j, 0]
            ds = dstate[j]
            dob = do_ref[:, vsl].astype(BF16)
            sb, dsb = s.astype(BF16), ds.astype(BF16)
            a = _dot_nt(q, k)
            p = a * mask
            dp = _dot_nt(dob, vv)
            da = dp * mask
            dab = da.astype(BF16)
            dqc = _dot_nt(dob, sb)
            dkc = _dot_nt(vv, dsb)
            qs = (qf * qd).astype(BF16)
            ks = (kf * kd).astype(BF16)
            dq_ref[:, ksl] = (_dot(dab, k) + dqc * qd).astype(dq_ref.dtype)
            dk_ref[:, ksl] = (_dot_tn(dab, q) + dkc * kd).astype(dk_ref.dtype)
            dv_ref[:, vsl] = (_dot_tn(p.astype(BF16), dob) + _dot(ks, dsb)).astype(dv_ref.dtype)
            dsg = ds * s * (g * c)
            terms = da * a * dpos + dqc * qf * (qd * qpos) + dkc * kf * (kd * kpos)
            for half in range(dv // dk):
                terms = terms + dsg[:, half * dk:(half + 1) * dk]
            total = jnp.sum(jnp.sum(terms, axis=0, keepdims=True), axis=1, keepdims=True)
            dl_ref[j] += jnp.broadcast_to(total, (8, 128))
            dstate[j] = ds * g + _dot_tn(qs, dob)

    def cmap(hb, rstep, lam_ref):
        return (_ret_chunk_of(nt - 1 - rstep, nt, nl, reverse), hb)

    def vmap(hb, rstep, lam_ref):
        return (_ret_chunk_of(nt - 1 - rstep, nt, nl, reverse), vcol // (hg * dv) + hb)

    return pl.pallas_call(
        body,
        name="retention_rev_bwd" if reverse else "retention_bwd",
        grid_spec=pltpu.PrefetchScalarGridSpec(
            num_scalar_prefetch=1,
            grid=(nh // hg, nt),
            in_specs=[
                pl.BlockSpec((c, hg * dk), cmap),
                pl.BlockSpec((c, hg * dk), cmap),
                pl.BlockSpec((c, hg * dv), vmap),
                pl.BlockSpec((hg, 1, dk, dv), lambda hb, rstep, lam_ref: (hb, nt - 1 - rstep, 0, 0)),
                pl.BlockSpec((c, hg * dv), cmap),
            ],
            out_specs=[
                pl.BlockSpec((c, hg * dk), cmap),
                pl.BlockSpec((c, hg * dk), cmap),
                pl.BlockSpec((c, hg * dv), cmap),
                pl.BlockSpec((hg, 8, 128), lambda hb, rstep, lam_ref: (hb, 0, 0)),
            ],
            scratch_shapes=[pltpu.VMEM((hg, dk, dv), F32)],
        ),
        out_shape=[
            jax.ShapeDtypeStruct(qr.shape, qr.dtype),
            jax.ShapeDtypeStruct(kr.shape, kr.dtype),
            jax.ShapeDtypeStruct((t, nh * dv), v.dtype),
            jax.ShapeDtypeStruct((nh, 8, 128), F32),
        ],
        compiler_params=_params(("parallel", "arbitrary")),
    )(lam, qr, kr, v, states, do)


def _rope_tables(t, n_lat):
    nf = RET_KEY_DIM // 4
    tok = np.arange(n_lat)
    inv_freq = (ROPE_BASE ** (-np.arange(nf, dtype=np.float32) / nf)).astype(np.float32)
    row = (tok // GRID_W).astype(np.float32)
    col = (tok % GRID_W).astype(np.float32)
    ang = np.concatenate([row[:, None] * inv_freq, col[:, None] * inv_freq], axis=-1).astype(np.float32)
    cos = np.ones((t, 2 * nf), np.float32)
    sin = np.zeros((t, 2 * nf), np.float32)
    cos[:n_lat] = np.cos(ang)
    sin[:n_lat] = np.sin(ang)
    return jnp.asarray(np.concatenate([cos, cos], axis=1)), jnp.asarray(np.concatenate([-sin, sin], axis=1))


def _rotate(x, cos2, sin2):
    return x * cos2 + pltpu.roll(x, RET_KEY_DIM // 2, 1) * sin2


def _rope_fwd_call(u, qblk, kblk, w_qk, cos2, sin2, k_scale, tm):
    t = u.shape[0]
    nh = w_qk // RET_KEY_DIM

    def body(q_ref, k_ref, c_ref, s_ref, qr_ref, kr_ref):
        c, s = c_ref[...], s_ref[...]
        for hh in range(nh):
            sl = slice(hh * RET_KEY_DIM, (hh + 1) * RET_KEY_DIM)
            qr_ref[:, sl] = _rotate(q_ref[:, sl].astype(F32), c, s).astype(qr_ref.dtype)
            kr_ref[:, sl] = (_rotate(k_ref[:, sl].astype(F32), c, s) * k_scale).astype(kr_ref.dtype)

    tab = pl.BlockSpec((tm, RET_KEY_DIM), lambda i: (i, 0))
    out = pl.BlockSpec((tm, w_qk), lambda i: (i, 0))
    return pl.pallas_call(
        body, name="rope_fwd", grid=(t // tm,),
        in_specs=[pl.BlockSpec((tm, w_qk), lambda i: (i, qblk)), pl.BlockSpec((tm, w_qk), lambda i: (i, kblk)), tab, tab],
        out_specs=[out, out], out_shape=[jax.ShapeDtypeStruct((t, w_qk), BF16)] * 2,
        compiler_params=_params(("parallel",)),
    )(u, u, cos2, sin2)


def _assemble_du_call(pieces, off, cos2, sin2, k_scale, tm):
    flat = [a for p in pieces for a in (p if isinstance(p, tuple) else (p,))]
    t = flat[0].shape[0]
    n = len(flat)

    def body(*refs):
        c, s = refs[n][...], -refs[n + 1][...]
        o_ref = refs[n + 2]
        it = iter(refs[:n])
        for blk, p in enumerate(pieces):
            lo = off[blk]
            if not isinstance(p, tuple):
                o_ref[:, lo:off[blk + 1]] = next(it)[...].astype(o_ref.dtype)
                continue
            fwd_ref, rev_ref = next(it), next(it)
            if blk == 6:
                o_ref[:, lo:off[blk + 1]] = (fwd_ref[...].astype(F32) + rev_ref[...].astype(F32)).astype(o_ref.dtype)
                continue
            mult = k_scale if blk == 5 else 1.0
            for hh in range((off[blk + 1] - lo) // RET_KEY_DIM):
                sl = slice(hh * RET_KEY_DIM, (hh + 1) * RET_KEY_DIM)
                dy = (fwd_ref[:, sl].astype(F32) + rev_ref[:, sl].astype(F32)) * mult
                o_ref[:, lo + hh * RET_KEY_DIM:lo + (hh + 1) * RET_KEY_DIM] = _rotate(dy, c, s).astype(o_ref.dtype)

    tab = pl.BlockSpec((tm, RET_KEY_DIM), lambda i: (i, 0))
    return pl.pallas_call(
        body, name="assemble_du", grid=(t // tm,),
        in_specs=[pl.BlockSpec((tm, a.shape[1]), lambda i: (i, 0)) for a in flat] + [tab, tab],
        out_specs=pl.BlockSpec((tm, off[-1]), lambda i: (i, 0)),
        out_shape=jax.ShapeDtypeStruct((t, off[-1]), BF16),
        compiler_params=_params(("parallel",)),
    )(*flat, cos2, sin2)


def _my_position():
    return lax.axis_index("x"), lax.axis_index("y"), lax.axis_index("c")


def _flip(pos, k):
    x, y, c = pos
    return (1 - x if k & 4 else x, 1 - y if k & 2 else y, 1 - c if k & 1 else c)


def _linear(pos):
    return 4 * pos[0] + 2 * pos[1] + pos[2]


def _slab(ref, axis, idx, size):
    start = pl.multiple_of(idx * size, size)
    return ref.at[pl.ds(start, size), :] if axis == 0 else ref.at[:, pl.ds(start, size)]


HBM_SPEC = pl.BlockSpec(memory_space=pltpu.HBM)
SEM_SPEC = pl.BlockSpec(memory_space=pltpu.SEMAPHORE)
DATAFLOW = pltpu.SideEffectType.DATAFLOW_SIDE_EFFECTING
PEER_BITS = (1, 2, 4, 6, 3, 5, 7)
GATHER_BITS = (1, 2, 4, 6)


def _in_hbm(a):
    return pltpu.with_memory_space_constraint(a, pltpu.HBM)


def _gather_views(me, k, a, src_refs, land_refs, axes):
    size = src_refs[a].shape[axes[a]]
    peer = _flip(me, k)
    return src_refs[a], _slab(land_refs[a], axes[a], _linear(me), size), _slab(land_refs[a], axes[a], _linear(peer), size)


def _scatter_views(me, k, a, src_refs, land_refs, axes):
    size = land_refs[a].shape[1 + axes[a]]
    peer = _flip(me, k)
    return _slab(src_refs[a], axes[a], _linear(peer), size), land_refs[a].at[k - 1], land_refs[a].at[k - 1]


CHIP_BITS = (0, 2, 4, 6)


def _chip_views(me, k, a, src_refs, land_refs, axes):
    j = CHIP_BITS.index(k)
    return src_refs[a].at[j], land_refs[a].at[j - 1], land_refs[a].at[j - 1]


def _pair_exchange(grads, axes, sizes):
    ns = len(grads)

    def slab_shape(a):
        s = grads[a].shape
        return (sizes[a], s[1]) if axes[a] == 0 else (s[0], sizes[a])

    def body(*refs):
        g_refs, p_refs = refs[:ns], refs[ns:2 * ns]
        send_sems, recv_sems = refs[2 * ns:]
        me = _my_position()
        sibling = _flip(me, 1)
        copies = []
        for j, kc in enumerate(CHIP_BITS):
            for a in range(ns):
                cp = pltpu.make_async_remote_copy(
                    src_ref=_slab(g_refs[a], axes[a], _linear(_flip(me, kc | 1)), sizes[a]), dst_ref=p_refs[a].at[j],
                    send_sem=send_sems.at[4 * a + j], recv_sem=recv_sems.at[4 * a + j],
                    device_id=sibling, device_id_type=MESH)
                cp.start()
                copies.append(cp)
        for cp in copies:
            cp.wait_recv()
        for cp in copies:
            cp.wait_send()

    return pl.pallas_call(
        body, name="scatter_pair_exchange", in_specs=[ANY] * ns, out_specs=[ANY] * ns,
        out_shape=[jax.ShapeDtypeStruct((4,) + slab_shape(a), grads[a].dtype) for a in range(ns)],
        scratch_shapes=[pltpu.SemaphoreType.DMA((4 * ns,)), pltpu.SemaphoreType.DMA((4 * ns,))],
        compiler_params=pltpu.CompilerParams(has_side_effects=True),
    )(*grads)


def _pair_add(grad, theirs, axis, chip_idx):
    _, r, c = theirs.shape
    tm = _pick(r, (256, 128, 64, 32, 16))
    if axis == 0:
        mine_spec = pl.BlockSpec((tm, c), lambda j, i, idx: (idx[j] * (r // tm) + i, 0))
    else:
        mine_spec = pl.BlockSpec((tm, c), lambda j, i, idx: (i, idx[j]))

    def body(idx_ref, mine_ref, theirs_ref, o_ref):
        o_ref[0] = (mine_ref[...].astype(F32) + theirs_ref[0].astype(F32)).astype(o_ref.dtype)

    spec = pl.BlockSpec((1, tm, c), lambda j, i, idx: (j, i, 0))
    return pl.pallas_call(
        body, name="scatter_pair_add",
        grid_spec=pltpu.PrefetchScalarGridSpec(
            num_scalar_prefetch=1, grid=(4, r // tm), in_specs=[mine_spec, spec], out_specs=spec),
        out_shape=jax.ShapeDtypeStruct(theirs.shape, theirs.dtype),
        compiler_params=_params(("parallel", "parallel")),
    )(chip_idx, grad, theirs)


def _slab_block(rows, cols, tm, axis):
    if axis == 0:
        return pl.BlockSpec((tm, cols), lambda i, idx: (idx[0] * (rows // tm) + i, 0))
    return pl.BlockSpec((tm, cols), lambda i, idx: (i, idx[0]))


def _place_shard(shard, land, axis, my_idx):
    r, c = shard.shape
    tm = _pick(r, (512, 256, 128, 64, 32, 16))

    def body(idx_ref, s_ref, land_ref, o_ref):
        o_ref[...] = s_ref[...]

    return pl.pallas_call(
        body, name="gather_place",
        grid_spec=pltpu.PrefetchScalarGridSpec(
            num_scalar_prefetch=1, grid=(r // tm,),
            in_specs=[pl.BlockSpec((tm, c), lambda i, idx: (i, 0)), ANY],
            out_specs=_slab_block(r, c, tm, axis)),
        out_shape=jax.ShapeDtypeStruct(land.shape, land.dtype),
        input_output_aliases={2: 0},
        compiler_params=_params(("parallel",)),
    )(my_idx, shard, land)


def _push_start(name, srcs, lands, axes, views, bits, deps):
    ns = len(srcs)

    def body(*refs):
        src_refs, land_refs = refs[:ns], refs[ns:2 * ns]
        send_sems, recv_sems = refs[2 * ns + len(deps):2 * ns + len(deps) + 2]
        token = refs[-1]
        me = _my_position()
        for k in bits:
            for a in range(ns):
                s, d, _ = views(me, k, a, src_refs, land_refs, axes)
                pltpu.make_async_remote_copy(
                    src_ref=s, dst_ref=d, send_sem=send_sems.at[7 * a + k - 1], recv_sem=recv_sems.at[7 * a + k - 1],
                    device_id=_flip(me, k), device_id_type=MESH).start()
        token[...] = jnp.zeros_like(token)

    thru = [pltpu.HBM(a.shape, a.dtype) for a in list(srcs) + list(lands)]
    outs = pl.pallas_call(
        body, name=name,
        in_specs=[HBM_SPEC] * (2 * ns) + [ANY] * len(deps),
        out_specs=[SEM_SPEC, SEM_SPEC] + [HBM_SPEC] * (2 * ns) + [VMEM_SPEC],
        out_shape=[pltpu.SemaphoreType.DMA((7 * ns,)), pltpu.SemaphoreType.DMA((7 * ns,))] + thru
        + [jax.ShapeDtypeStruct((8, 128), F32)],
        input_output_aliases={i: 2 + i for i in range(2 * ns)},
        compiler_params=pltpu.CompilerParams(has_side_effects=DATAFLOW),
    )(*[_in_hbm(a) for a in srcs], *[_in_hbm(a) for a in lands], *deps)
    return (outs[0], outs[1]), outs[2:2 + ns], outs[2 + ns:2 + 2 * ns], outs[-1]


def _gather_finish(lands, axes, sizes):
    ns = len(lands)
    chips = (2, 4, 6)

    def body(*refs):
        land_refs = refs[ns:2 * ns]
        send_sems, recv_sems = refs[2 * ns:]
        me = _my_position()
        sibling = _flip(me, 1)
        copies = []
        for j, kc in enumerate(chips):
            for a in range(ns):
                def slab_of(pos):
                    return _slab(land_refs[a], axes[a], _linear(pos), sizes[a])
                send = pltpu.make_async_remote_copy(
                    src_ref=slab_of(_flip(me, kc)), dst_ref=slab_of(_flip(me, kc)), send_sem=send_sems.at[3 * a + j],
                    recv_sem=recv_sems.at[3 * a + j], device_id=sibling, device_id_type=MESH)
                recv = pltpu.make_async_remote_copy(
                    src_ref=slab_of(_flip(me, kc)), dst_ref=slab_of(_flip(sibling, kc)), send_sem=send_sems.at[3 * a + j],
                    recv_sem=recv_sems.at[3 * a + j], device_id=sibling, device_id_type=MESH)
                send.start()
                copies.append((send, recv))
        for send, recv in copies:
            recv.wait_recv()
        for send, recv in copies:
            send.wait_send()

    return pl.pallas_call(
        body, name="gather_finish", in_specs=[ANY] * ns, out_specs=[ANY] * ns,
        out_shape=[jax.ShapeDtypeStruct(l.shape, l.dtype) for l in lands],
        input_output_aliases={a: a for a in range(ns)},
        scratch_shapes=[pltpu.SemaphoreType.DMA((3 * ns,)), pltpu.SemaphoreType.DMA((3 * ns,))],
        compiler_params=pltpu.CompilerParams(has_side_effects=True),
    )(*lands)


def _push_wait(name, sems, srcs, lands, axes, views, bits, after):
    ns = len(srcs)

    def body(*refs):
        src_refs, land_refs = refs[:ns], refs[ns:2 * ns]
        send_sems, recv_sems = refs[2 * ns:2 * ns + 2]
        me = _my_position()
        for k in bits:
            for a in range(ns):
                s, d, got = views(me, k, a, src_refs, land_refs, axes)
                cp = pltpu.make_async_remote_copy(
                    src_ref=s, dst_ref=got, send_sem=send_sems.at[7 * a + k - 1], recv_sem=recv_sems.at[7 * a + k - 1],
                    device_id=_flip(me, k), device_id_type=MESH)
                cp.wait_send()
                cp.wait_recv()

    thru = [pltpu.HBM(a.shape, a.dtype) for a in list(srcs) + list(lands)]
    outs = pl.pallas_call(
        body, name=name,
        in_specs=[HBM_SPEC] * (2 * ns) + [SEM_SPEC, SEM_SPEC] + [ANY] * len(after),
        out_specs=[HBM_SPEC] * (2 * ns),
        out_shape=thru,
        input_output_aliases={i: i for i in range(2 * ns)},
        compiler_params=pltpu.CompilerParams(has_side_effects=DATAFLOW),
    )(*srcs, *lands, sems[0], sems[1], *after)
    return outs[:ns], outs[ns:]


def _small_allgather(v, name):
    r, c = v.shape

    def body(v_ref, all_ref, sum_ref, send_sems, recv_sems):
        me = _my_position()
        all_ref[_linear(me)] = v_ref[...]
        copies = []
        for k in range(1, N_DEV):
            peer = _flip(me, k)
            copies.append(pltpu.make_async_remote_copy(
                src_ref=v_ref, dst_ref=all_ref.at[_linear(me)], send_sem=send_sems.at[k - 1], recv_sem=recv_sems.at[k - 1],
                device_id=peer, device_id_type=MESH))
        for cp in copies:
            cp.start()
        for k in range(1, N_DEV):
            peer = _flip(me, k)
            pltpu.make_async_remote_copy(
                src_ref=v_ref, dst_ref=all_ref.at[_linear(peer)], send_sem=send_sems.at[k - 1], recv_sem=recv_sems.at[k - 1],
                device_id=peer, device_id_type=MESH).wait_recv()
        for cp in copies:
            cp.wait_send()
        acc = all_ref[0]
        for d in range(1, N_DEV):
            acc = acc + all_ref[d]
        sum_ref[...] = acc

    return pl.pallas_call(
        body,
        name=name,
        in_specs=[VMEM_SPEC],
        out_specs=[VMEM_SPEC, VMEM_SPEC],
        out_shape=[jax.ShapeDtypeStruct((N_DEV, r, c), F32), jax.ShapeDtypeStruct((r, c), F32)],
        scratch_shapes=[pltpu.SemaphoreType.DMA((N_DEV - 1,)), pltpu.SemaphoreType.DMA((N_DEV - 1,))],
        compiler_params=pltpu.CompilerParams(has_side_effects=True, vmem_limit_bytes=VMEM_LIMIT),
    )(v)


def _ada_fwd_call(cin, ada_w, ada_b_cols):
    nl, d, ncol = ada_w.shape
    nrow = cin.shape[0]

    def body(c_ref, w_ref, b_ref, o_ref):
        cs = _silu(c_ref[...]).astype(BF16)
        for l in range(nl):
            o_ref[l] = _dot(cs, w_ref[l].astype(BF16)) + b_ref[l]

    return pl.pallas_call(
        body, name="ada_fwd", in_specs=[VMEM_SPEC] * 3, out_specs=VMEM_SPEC,
        out_shape=jax.ShapeDtypeStruct((nl, nrow, ncol), F32), compiler_params=_params(),
    )(cin, ada_w, ada_b_cols)


def _ada_bwd_call(cin, ada_w, dmod):
    nl, d, ncol = ada_w.shape
    nrow = cin.shape[0]

    def body(c_ref, w_ref, dm_ref, gw_ref, dcs_ref):
        cs = _silu(c_ref[...]).astype(BF16)
        acc = jnp.zeros((nrow, d), F32)
        for l in range(nl):
            dm = dm_ref[l].astype(BF16)
            gw_ref[l] = _dot_tn(cs, dm)
            acc = acc + _dot_nt(dm, w_ref[l].astype(BF16))
        dcs_ref[...] = acc

    return pl.pallas_call(
        body, name="ada_bwd", in_specs=[VMEM_SPEC] * 3, out_specs=[VMEM_SPEC, VMEM_SPEC],
        out_shape=[jax.ShapeDtypeStruct((nl, d, ncol), F32), jax.ShapeDtypeStruct((nrow, d), F32)],
        compiler_params=_params(),
    )(cin, ada_w, dmod)


def _adamw_math(w, g, m, v):
    m = ADAM_B1 * m + (1.0 - ADAM_B1) * g
    v = ADAM_B2 * v + (1.0 - ADAM_B2) * jnp.square(g)
    m_hat = m / (1.0 - ADAM_B1 ** ADAM_STEP)
    v_hat = v / (1.0 - ADAM_B2 ** ADAM_STEP)
    delta = -ADAM_LR * (m_hat / (jnp.sqrt(v_hat) + ADAM_EPS) + ADAM_WD * w)
    return delta, m, v


def _adamw_sharded(w, m, v, mine, slabs, axis, my_idx, layer, prev, name):
    nl, r, c = w.shape
    tm = _pick(r, (128, 64, 32, 16))
    nprev = 0 if prev is None else len(prev)
    nslab = slabs.shape[0]
    if axis is None:
        mine_spec = pl.BlockSpec((1, tm, c), lambda i, idx: (0, i, 0))
    else:
        mine_spec = _slab_block(r, c, tm, axis)

    def body(idx_ref, w_ref, m_ref, v_ref, mine_ref, s_ref, *rest):
        g_ref, d_ref, nm_ref, nv_ref = rest[nprev:]
        g = (mine_ref[0] if axis is None else mine_ref[...]).astype(F32)
        for k in range(nslab):
            g = g + s_ref[k].astype(F32)
        delta, nm, nv = _adamw_math(w_ref[0], g, m_ref[0], v_ref[0])
        g_ref[0], d_ref[0], nm_ref[0], nv_ref[0] = g, delta, nm, nv

    spec = pl.BlockSpec((1, tm, c), lambda i, idx: (layer, i, 0))
    out = jax.ShapeDtypeStruct(w.shape, F32)
    return pl.pallas_call(
        body, name=name,
        grid_spec=pltpu.PrefetchScalarGridSpec(
            num_scalar_prefetch=1, grid=(r // tm,),
            in_specs=[spec, spec, spec, mine_spec,
                      pl.BlockSpec((nslab, tm, c), lambda i, idx: (0, i, 0))] + [ANY] * nprev,
            out_specs=[spec] * 4),
        out_shape=[out] * 4,
        input_output_aliases={6 + j: j for j in range(nprev)},
        compiler_params=_params(("parallel",)),
    )(my_idx, w, m, v, mine, slabs, *(() if prev is None else prev))


def _adamw_dense(w, g, m, v, name):
    r, c = w.shape
    tm = _pick(r, (256, 128, 64, 32, 16, 8))

    def body(w_ref, g_ref, m_ref, v_ref, d_ref, nm_ref, nv_ref):
        d_ref[...], nm_ref[...], nv_ref[...] = _adamw_math(w_ref[...], g_ref[...], m_ref[...], v_ref[...])

    spec = pl.BlockSpec((tm, c), lambda i: (i, 0))
    out = jax.ShapeDtypeStruct(w.shape, F32)
    return pl.pallas_call(
        body, name=name, grid=(r // tm,), in_specs=[spec] * 4, out_specs=[spec] * 3, out_shape=[out] * 3,
        compiler_params=_params(("parallel",)),
    )(w, g, m, v)


def _pack(parts, width=128):
    flat = jnp.concatenate([p.reshape(-1).astype(F32) for p in parts])
    n = flat.shape[0]
    total = -(-n // (8 * width)) * (8 * width)
    return jnp.pad(flat, (0, total - n)).reshape(total // width, width)


def _unpack(buf, shapes):
    flat = buf.reshape(-1)
    out, off = [], 0
    for s in shapes:
        n = int(np.prod(s))
        out.append(flat[off:off + n].reshape(s))
        off += n
    return out


def kernel(x, c, ctx, c_ctx, ada_w, ada_b, norm_g, w_in, na_rpb, ret_decay_logit, w_proj_na, w_proj_ret, w_out, final_g, loss_target, m_c_ctx, m_ada_w, m_ada_b, m_norm_g, m_w_in, m_na_rpb, m_ret_decay_logit, m_w_proj_na, m_w_proj_ret, m_w_out, m_final_g, v_c_ctx, v_ada_w, v_ada_b, v_norm_g, v_w_in, v_na_rpb, v_ret_decay_logit, v_w_proj_na, v_w_proj_ret, v_w_out, v_final_g):
    depth = w_in.shape[0]
    n_lat, d = x.shape[1], x.shape[2]
    n_ctx = ctx.shape[1]
    t = n_lat + n_ctx
    w_na = w_proj_na.shape[1]
    w_retv = w_proj_ret.shape[1] * N_DEV
    in_cols = w_in.shape[2] * N_DEV
    w_qk = (in_cols - 4 * w_na - 2 * w_retv - 2 * d) // 2
    sizes = (w_na, w_na, w_na, w_na, w_qk, w_qk, w_retv, w_retv, d, d)
    off = tuple(int(o) for o in np.cumsum((0,) + sizes))
    NA_Q, NA_K, NA_V, NA_Z, RET_Q, RET_K, RET_V, RET_Z, G_NA, G_RET = range(10)
    rows = n_lat // GRID_W
    me = _my_position()
    my_idx = _linear(me)
    tm_row = _pick(n_ctx, (256, 128))

    idx_arr = jnp.reshape(my_idx, (1,)).astype(jnp.int32)
    chip_idx = jnp.stack([_linear(_flip(me, kc)) for kc in CHIP_BITS]).astype(jnp.int32)

    w_axes = (1, 1, 0, 0)
    w_names = ("w_in", "w_proj_na", "w_proj_ret", "w_out")
    shard = [[w[l].astype(BF16) for w in (w_in, w_proj_na, w_proj_ret, w_out)] for l in range(depth)]
    groups = [[(0, 0)], [(0, 1), (0, 2), (0, 3)]] + [[(l, a) for a in range(4)] for l in range(1, depth)]
    gathers = {}

    def start_gather(gi, deps):
        keys = groups[gi]
        srcs = [shard[l][a] for l, a in keys]
        axes = tuple(w_axes[a] for _, a in keys)
        lands = [_place_shard(s, lax.empty(tuple(n * (N_DEV if i == ax else 1) for i, n in enumerate(s.shape)), BF16),
                              ax, idx_arr) for s, ax in zip(srcs, axes)]
        sizes = tuple(s.shape[ax] for s, ax in zip(srcs, axes))
        sems, srcs, lands, tok = _push_start(f"gather_start_{gi}", srcs, lands, axes, _gather_views, GATHER_BITS, deps)
        flight = dict(name=f"gather_wait_{gi}", sems=sems, srcs=srcs, lands=lands, axes=axes, sizes=sizes, ready=None)
        for pos, key in enumerate(keys):
            gathers[key] = (flight, pos)
        return tok

    token = start_gather(0, ())

    ncol = ada_w.shape[2]
    c_all, _ = _small_allgather(jnp.pad(c, ((0, 7), (0, 0))) + token[:, :1], "allgather_c")
    cin = jnp.concatenate([c_all[:, 0, :], c_ctx[None, :], jnp.zeros((7, d), F32)], axis=0)
    ada_b_cols = lax.dynamic_slice_in_dim(ada_b, my_idx * ncol, ncol, axis=1)[:, None, :]
    mod_cols = _ada_fwd_call(cin, ada_w, ada_b_cols)
    mod_gathered, _ = _small_allgather(mod_cols.reshape(depth * 16, ncol), "allgather_mod")
    mod_all = mod_gathered.reshape(N_DEV, depth, 16, ncol).transpose(1, 2, 0, 3).reshape(depth, 16, N_DEV * ncol)
    mod_lat = lax.dynamic_index_in_dim(mod_all, my_idx, axis=1, keepdims=False)
    mod_ctx = mod_all[:, 8, :]
    token = mod_gathered
    for gi in range(1, len(groups)):
        token = start_gather(gi, (token,))

    def landed(l, a, act):
        flight, pos = gathers[(l, a)]
        if flight["ready"] is None:
            arrived = _push_wait(flight["name"], flight["sems"], flight["srcs"], flight["lands"],
                                 flight["axes"], _gather_views, GATHER_BITS, (act, token))[1]
            flight["ready"] = _gather_finish(arrived, flight["axes"], flight["sizes"])
        return flight["ready"][pos]

    pending, scatters = {}, []

    def send_dw(l, a, dw):
        pending[(l, a)] = dw
        if a == 0:
            keys = [(0, 0)] if l == 0 else [(l, b) for b in range(4)]
        elif l == 0 and a == 1:
            keys = [(0, 1), (0, 2), (0, 3)]
        else:
            return None
        srcs = [pending[k] for k in keys]
        axes = tuple(w_axes[b] for _, b in keys)
        sizes = tuple(s.shape[ax] // N_DEV for s, ax in zip(srcs, axes))
        slab_shapes = [tuple(n // (N_DEV if i == ax else 1) for i, n in enumerate(s.shape)) for s, ax in zip(srcs, axes)]
        by_chip = keys == [(0, 0)]
        if by_chip:
            theirs = _pair_exchange(srcs, axes, sizes)
            srcs = [_pair_add(g, p, ax, chip_idx) for g, p, ax in zip(srcs, theirs, axes)]
            lands = [lax.empty((3,) + shp, BF16) for shp in slab_shapes]
            views, bits = _chip_views, CHIP_BITS[1:]
        else:
            lands = [lax.empty((N_DEV - 1,) + shp, BF16) for shp in slab_shapes]
            views, bits = _scatter_views, PEER_BITS
        sems, srcs, lands, tok = _push_start(f"scatter_start_{len(scatters)}", srcs, lands, axes, views, bits, ())
        scatters.append(dict(name=f"scatter_wait_{len(scatters)}", sems=sems, srcs=srcs, lands=lands, axes=axes, keys=keys,
                             views=views, bits=bits, by_chip=by_chip))
        return tok

    cos2, sin2 = _rope_tables(t, n_lat)
    k_scale = RET_KEY_DIM ** -0.5
    assert off[RET_Q] % w_qk == 0 and off[RET_K] % w_qk == 0
    assert off[NA_Z] % w_na == 0 and off[G_NA] % d == 0 and off[G_RET] % d == 0 and off[RET_Z] % w_retv == 0
    assert off[RET_V] % (_ret_head_group(w_retv // RET_VAL_DIM) * RET_VAL_DIM) == 0
    na_cols = tuple(off[i] // NA_HEAD_DIM for i in (NA_Q, NA_K, NA_V))
    norm_mod_fwd, norm_mod_bwd = _make_rowwise(_f_norm_mod, "norm_mod", (BF16,), (d,), n_lat, tm_row, (0,))
    gate_na_fwd, gate_na_bwd = _make_rowwise(_f_gate_na, "gate_na", (BF16,), (w_na,), n_lat, tm_row, (0, 1),
                                             col_blocks={1: (w_na, off[NA_Z] // w_na)})
    merge_fwd, merge_bwd = _make_rowwise(_f_merge, "merge", (BF16,), (d,), n_lat, tm_row, (0, 1, 2, 3),
                                         col_blocks={0: (d, off[G_NA] // d), 1: (d, off[G_RET] // d)},
                                         drow_dtypes={2: BF16, 3: BF16})
    residual_fwd, _ = _make_rowwise(_f_residual, "residual", (F32,), (d,), n_lat, tm_row, (0, 1))
    _, residual_bwd = _make_rowwise(lambda out, gate: (gate * out,), "residual", (F32,), (d,), n_lat, tm_row, (0,),
                                    drow_dtypes={0: BF16})
    loss_fwd, loss_bwd = _make_rowwise(_f_loss, "loss_head", (F32,), (128,), n_lat, tm_row, (0,))

    def pair(a, b):
        return jnp.stack([a, b])[:, None, :]

    def mod_vectors(mod_lat_l, mod_ctx_l, norm_g_l):
        shift, scale, gate = jnp.split(mod_lat_l, 3)
        c_shift, c_scale, c_gate = jnp.split(mod_ctx_l, 3)
        return pair(norm_g_l, norm_g_l), pair(scale, c_scale), pair(shift, c_shift), pair(gate, c_gate)

    def log_decay(logit):
        return jax.nn.log_sigmoid(logit.astype(F32))

    xa = jnp.concatenate([x[0], ctx[0]], axis=0)
    saved = []
    for l in range(depth):
        vecs, vecs_vjp = jax.vjp(mod_vectors, mod_lat[l], mod_ctx[l], norm_g[l])
        (h,) = norm_mod_fwd((xa,), vecs[:3])
        wl_in = landed(l, 0, h)
        u = _matmul(h, wl_in, out_dtype=BF16, name="in_proj_fwd")
        qr, kr = _rope_fwd_call(u, off[RET_Q] // w_qk, off[RET_K] // w_qk, w_qk, cos2, sin2, k_scale, tm_row)
        bt, bt_vjp = jax.vjp(lambda r: _na_bias_table(r, rows), na_rpb[l])
        lam, lam_vjp = jax.vjp(log_decay, ret_decay_logit[l])
        o_na = _na_fwd_call(u, u, u, na_cols, w_na, bt, n_lat)
        o_f, st_f = _ret_fwd_call(qr, kr, u, off[RET_V], lam[0], n_lat, False)
        o_b, st_b = _ret_fwd_call(qr, kr, u, off[RET_V], lam[1], n_lat, True)
        (a_na,) = gate_na_fwd((o_na, u), ())
        a_ret = _gate_ret_fwd_call(o_f, o_b, u, off[RET_Z] // w_retv, tm_row)
        wl_pna, wl_pret, wl_out = landed(l, 1, a_na), landed(l, 2, a_na), landed(l, 3, a_na)
        y_na = _matmul(a_na, wl_pna, out_dtype=BF16, name="proj_na_fwd")
        y_ret = _matmul(a_ret, wl_pret, out_dtype=BF16, name="proj_ret_fwd")
        (merged,) = merge_fwd((u, u, y_na, y_ret), ())
        out = _matmul(merged, wl_out, out_dtype=F32, name="out_proj_fwd")
        (xa_next,) = residual_fwd((xa, out), vecs[3:])
        saved.append(dict(xa=xa, vecs=vecs, vecs_vjp=vecs_vjp, h=h, w=(wl_in, wl_pna, wl_pret, wl_out), u=u, qr=qr, kr=kr,
                          bt=bt, bt_vjp=bt_vjp, lam=lam, lam_vjp=lam_vjp, o_na=o_na, o_f=o_f,
                          o_b=o_b, st_f=st_f, st_b=st_b, a_na=a_na, a_ret=a_ret, y_na=y_na, y_ret=y_ret,
                          merged=merged, out=out))
        xa = xa_next

    fg_pair, fg_vjp = jax.vjp(lambda g: pair(g, g), final_g)
    (loss_rows,) = loss_fwd((xa, loss_target[0]), (fg_pair,))
    loss = lax.psum(jnp.sum(loss_rows), ("x", "y", "c"))
    (dx_last,), (d_fg_pair,) = loss_bwd((xa, loss_target[0]), (fg_pair,), (jnp.ones_like(loss_rows),))
    (d_final_g,) = fg_vjp(d_fg_pair)
    dxa = jnp.pad(dx_last, ((0, n_ctx), (0, 0)))

    d_mod_lat, d_mod_ctx, d_norm_g, d_rpb, d_decay = ([None] * depth for _ in range(5))
    for l in reversed(range(depth)):
        s = saved[l]
        u, qr, kr = s["u"], s["qr"], s["kr"]
        wl_in, wl_pna, wl_pret, wl_out = s["w"]
        (d_out,), (d_gate,) = residual_bwd((s["out"],), s["vecs"][3:], (dxa,))
        dxa_res = dxa
        send_dw(l, 3, _matmul(s["merged"], d_out, trans_a=True, out_dtype=BF16, name="out_proj_dw"))
        d_merged = _matmul(d_out, wl_out, trans_b=True, out_dtype=BF16, name="out_proj_da")
        (dg_na, dg_ret, dy_na, dy_ret), _ = merge_bwd((u, u, s["y_na"], s["y_ret"]), (), (d_merged,))
        send_dw(l, 2, _matmul(s["a_ret"], dy_ret, trans_a=True, out_dtype=BF16, name="proj_ret_dw"))
        da_ret = _matmul(dy_ret, wl_pret, trans_b=True, out_dtype=BF16, name="proj_ret_da")
        tok = send_dw(l, 1, _matmul(s["a_na"], dy_na, trans_a=True, out_dtype=BF16, name="proj_na_dw"))
        da_na = _matmul(dy_na, wl_pna, trans_b=True, out_dtype=BF16, name="proj_na_da", after=tok)
        do_ret, dz_ret = _gate_ret_bwd_call(s["o_f"], s["o_b"], u, off[RET_Z] // w_retv, da_ret, tm_row)
        (do_na, dz_na), _ = gate_na_bwd((s["o_na"], u), (), (da_na,))
        dq_f, dk_f, dv_f, dl_f = _ret_bwd_call(qr, kr, u, off[RET_V], s["lam"][0], s["st_f"], do_ret, n_lat, False)
        dq_b, dk_b, dv_b, dl_b = _ret_bwd_call(qr, kr, u, off[RET_V], s["lam"][1], s["st_b"], do_ret, n_lat, True)
        dq, dk, dv, dbt = _na_bwd_call(u, u, u, na_cols, w_na, s["bt"], do_na, n_lat)
        du = _assemble_du_call([dq, dk, dv, dz_na, (dq_f, dq_b), (dk_f, dk_b), (dv_f, dv_b), dz_ret, dg_na, dg_ret],
                               off, cos2, sin2, k_scale, _pick(n_ctx, (128,)))
        (d_rpb[l],) = s["bt_vjp"](dbt)
        (d_decay[l],) = s["lam_vjp"](jnp.stack([dl_f[:, 0, 0], dl_b[:, 0, 0]]))
        tok = send_dw(l, 0, _matmul(s["h"], du, trans_a=True, out_dtype=BF16, name="in_proj_dw"))
        dh = _matmul(du, wl_in, trans_b=True, out_dtype=BF16, name="in_proj_da", after=tok)
        (dxa,), d_vecs = norm_mod_bwd((s["xa"],), s["vecs"][:3], (dh,), acc=(dxa_res,))
        d_mod_lat[l], d_mod_ctx[l], d_norm_g[l] = s["vecs_vjp"](tuple(d_vecs) + (d_gate,))
    gx = dxa[:n_lat]
    d_mod_lat, d_mod_ctx, d_norm_g, d_rpb, d_decay = (jnp.stack(a) for a in (d_mod_lat, d_mod_ctx, d_norm_g, d_rpb, d_decay))

    small_shapes = [d_mod_lat.shape, d_mod_ctx.shape, d_norm_g.shape, d_final_g.shape, d_rpb.shape, d_decay.shape]
    packed = _pack([d_mod_lat, d_mod_ctx, d_norm_g, d_final_g, d_rpb, d_decay])
    g_all, g_sum = _small_allgather(packed, "allgather_small_grads")
    dml_sum, dmc_sum, grad_norm_g, grad_final_g, grad_na_rpb, grad_decay = _unpack(g_sum, small_shapes)
    grad_ada_b = dml_sum + dmc_sum
    dml_all = g_all.reshape(N_DEV, -1)[:, :depth * 3 * d].reshape(N_DEV, depth, 3 * d)

    def my_cols(a):
        return lax.dynamic_slice_in_dim(a, my_idx * ncol, ncol, axis=a.ndim - 1)

    dmod = jnp.concatenate(
        [my_cols(dml_all).transpose(1, 0, 2), my_cols(dmc_sum)[:, None, :], jnp.zeros((depth, 7, ncol), F32)], axis=1)
    grad_ada_w, dcs_part = _ada_bwd_call(cin, ada_w, dmod)
    _, dcs = _small_allgather(dcs_part, "allgather_dcsilu")
    sg = jax.nn.sigmoid(c_ctx)
    grad_c_ctx = dcs[8] * (sg * (1.0 + c_ctx * (1.0 - sg)))

    def flat2(a):
        return a.reshape(a.shape[0] * a.shape[1], a.shape[2])

    small_w = [c_ctx, ada_b, norm_g, na_rpb, ret_decay_logit, final_g]
    small_g = [grad_c_ctx, grad_ada_b, grad_norm_g, grad_na_rpb, grad_decay, grad_final_g]
    small_m = [m_c_ctx, m_ada_b, m_norm_g, m_na_rpb, m_ret_decay_logit, m_final_g]
    small_v = [v_c_ctx, v_ada_b, v_norm_g, v_na_rpb, v_ret_decay_logit, v_final_g]
    shp = [a.shape for a in small_w]
    ds_, nms_, nvs_ = _adamw_dense(_pack(small_w), _pack(small_g), _pack(small_m), _pack(small_v), "adamw_small")
    ds_, nms_, nvs_ = _unpack(ds_, shp), _unpack(nms_, shp), _unpack(nvs_, shp)

    d_ada, nm_ada, nv_ada = [a.reshape(ada_w.shape) for a in _adamw_dense(
        flat2(ada_w), flat2(grad_ada_w), flat2(m_ada_w), flat2(v_ada_w), "adamw_ada_w")]

    w_all = (w_in, w_proj_na, w_proj_ret, w_out)
    m_all = (m_w_in, m_w_proj_na, m_w_proj_ret, m_w_out)
    v_all = (v_w_in, v_w_proj_na, v_w_proj_ret, v_w_out)
    upd = [None] * 4
    after = d_ada
    for flight in scatters:
        mine, slabs = _push_wait(flight["name"], flight["sems"], flight["srcs"], flight["lands"], flight["axes"],
                                 flight["views"], flight["bits"], (after,))
        for (l, a), own, s in zip(flight["keys"], mine, slabs):
            upd[a] = _adamw_sharded(w_all[a], m_all[a], v_all[a], own, s, None if flight["by_chip"] else w_axes[a],
                                    idx_arr, l, upd[a], "adamw_" + w_names[a])
            after = upd[a][1]
    (g_w_in, d_w_in, nm_w_in, nv_w_in), (g_pna, d_pna, nm_pna, nv_pna) = upd[0], upd[1]
    (g_pret, d_pret, nm_pret, nv_pret), (g_out, d_out, nm_out, nv_out) = upd[2], upd[3]

    def order(cc, aw, ab, ng, wi, rp, dl, pn, pr, wo, fg):
        return [cc, aw, ab, ng, wi, rp, dl, pn, pr, wo, fg]

    grads_out = order(grad_c_ctx, grad_ada_w, grad_ada_b, grad_norm_g, g_w_in, grad_na_rpb, grad_decay, g_pna, g_pret, g_out, grad_final_g)
    delta_out = order(ds_[0], d_ada, ds_[1], ds_[2], d_w_in, ds_[3], ds_[4], d_pna, d_pret, d_out, ds_[5])
    m_out = order(nms_[0], nm_ada, nms_[1], nms_[2], nm_w_in, nms_[3], nms_[4], nm_pna, nm_pret, nm_out, nms_[5])
    v_out = order(nvs_[0], nv_ada, nvs_[1], nvs_[2], nv_w_in, nvs_[3], nvs_[4], nv_pna, nv_pret, nv_out, nvs_[5])
    return (loss, gx[None], *grads_out, *delta_out, *m_out, *v_out)
```

```python
import numpy as np
import jax
import jax.numpy as jnp
from jax import lax
from jax.experimental import pallas as pl
from jax.experimental.pallas import tpu as pltpu

F32 = jnp.float32
BF16 = jnp.bfloat16

N_DEV = 8
GRID_W = 64
NA_HEAD_DIM = 128
NA_WIN_ROWS = 8
NA_WIN_COLS = 16
RET_KEY_DIM = 128
RET_VAL_DIM = 256
RET_CHUNK = 128
ROPE_BASE = 10000.0
NORM_EPS = 1e-6
MASK_VALUE = -1e30

ADAM_LR = 0.001
ADAM_B1 = 0.9
ADAM_B2 = 0.999
ADAM_EPS = 1e-08
ADAM_WD = 0.01
ADAM_STEP = 10

VMEM_LIMIT = 48 * 1024 * 1024
MESH = pl.DeviceIdType.MESH
ANY = pl.BlockSpec(memory_space=pl.ANY)
VMEM_SPEC = pl.BlockSpec(memory_space=pltpu.VMEM)


def _params(sem=None):
    return pltpu.CompilerParams(dimension_semantics=sem, vmem_limit_bytes=VMEM_LIMIT)


def _pick(n, prefs):
    for p in prefs:
        if n % p == 0:
            return p
    return n


def _dot(a, b):
    return lax.dot_general(a, b, (((1,), (0,)), ((), ())), preferred_element_type=F32)


def _dot_nt(a, b):
    return lax.dot_general(a, b, (((1,), (1,)), ((), ())), preferred_element_type=F32)


def _dot_tn(a, b):
    return lax.dot_general(a, b, (((0,), (0,)), ((), ())), preferred_element_type=F32)


def _silu(x):
    return x * jax.nn.sigmoid(x)


def _matmul(a, b, *, trans_a=False, trans_b=False, out_dtype=F32, name="matmul", after=None):
    if trans_a:
        kdim, m = a.shape
    else:
        m, kdim = a.shape
    if trans_b:
        n, kb = b.shape
    else:
        kb, n = b.shape
    assert kdim == kb, (a.shape, b.shape, trans_a, trans_b)
    tm = _pick(m, (1152, 1024, 768, 512, 256, 128))
    tn = _pick(n, (1024, 512, 256, 128))
    tk = _pick(kdim, (3584, 2304, 2048, 1024, 512, 256, 128))
    nk = kdim // tk
    dn = (((0 if trans_a else 1,), (1 if trans_b else 0,)), ((), ()))

    def body(a_ref, b_ref, *rest):
        o_ref, acc_ref = rest[-2:]
        part = lax.dot_general(a_ref[...], b_ref[...], dn, preferred_element_type=F32)
        if nk == 1:
            o_ref[...] = part.astype(o_ref.dtype)
        else:
            k = pl.program_id(2)

            @pl.when(k == 0)
            def _():
                acc_ref[...] = part

            @pl.when(k > 0)
            def _():
                acc_ref[...] += part

            @pl.when(k == nk - 1)
            def _():
                o_ref[...] = acc_ref[...].astype(o_ref.dtype)

    a_spec = pl.BlockSpec((tk, tm), lambda i, j, k: (k, i)) if trans_a else pl.BlockSpec((tm, tk), lambda i, j, k: (i, k))
    b_spec = pl.BlockSpec((tn, tk), lambda i, j, k: (j, k)) if trans_b else pl.BlockSpec((tk, tn), lambda i, j, k: (k, j))
    return pl.pallas_call(
        body,
        name=name,
        grid=(m // tm, n // tn, nk),
        in_specs=[a_spec, b_spec] + ([] if after is None else [ANY]),
        out_specs=pl.BlockSpec((tm, tn), lambda i, j, k: (i, j)),
        out_shape=jax.ShapeDtypeStruct((m, n), out_dtype),
        scratch_shapes=[pltpu.VMEM((tm, tn) if nk > 1 else (8, 128), F32)],
        compiler_params=_params(("parallel", "parallel", "arbitrary")),
    )(*((a, b) if after is None else (a, b, after)))


def _make_rowwise(f, name, out_dtypes, out_cols, n_lat, tm, diff_rows, col_blocks=None, drow_dtypes=None):
    drow_dtypes = drow_dtypes or {}

    def tile_fn(*args):
        return tuple(o.astype(dt) for o, dt in zip(f(*args), out_dtypes))

    def row_spec(k, arr):
        width, index = (col_blocks or {}).get(k, (arr.shape[1], 0))
        return pl.BlockSpec((tm, width), lambda i: (i, index))

    def row_width(k, arr):
        return (col_blocks or {}).get(k, (arr.shape[1], 0))[0]

    def fwd_call(rows, vecs):
        t = min(r.shape[0] for r in rows)
        nr, nv = len(rows), len(vecs)
        nl = n_lat // tm

        def body(*refs):
            grp = (pl.program_id(0) >= nl).astype(jnp.int32)
            args = [r[...] for r in refs[:nr]] + [v[grp] for v in refs[nr:nr + nv]]
            for o_ref, o in zip(refs[nr + nv:], tile_fn(*args)):
                o_ref[...] = o

        return pl.pallas_call(
            body,
            name=name + "_fwd",
            grid=(t // tm,),
            in_specs=[row_spec(k, r) for k, r in enumerate(rows)]
            + [pl.BlockSpec(v.shape, lambda i: (0, 0, 0)) for v in vecs],
            out_specs=[pl.BlockSpec((tm, c), lambda i: (i, 0)) for c in out_cols],
            out_shape=[jax.ShapeDtypeStruct((t, c), dt) for c, dt in zip(out_cols, out_dtypes)],
            compiler_params=_params(("parallel",)),
        )(*rows, *vecs)

    def bwd_call(rows, vecs, gs, acc=None):
        t = min(r.shape[0] for r in rows)
        nr, nv, ng = len(rows), len(vecs), len(gs)
        nl = n_lat // tm
        nd = len(diff_rows)
        acc = [None] * nd if acc is None else list(acc)
        acc_in = [a for a in acc if a is not None]

        def body(*refs):
            i = pl.program_id(0)
            grp = (i >= nl).astype(jnp.int32)
            args = [r[...] for r in refs[:nr]] + [v[grp] for v in refs[nr:nr + nv]]
            g_refs = refs[nr + nv:nr + nv + ng]
            acc_refs = list(refs[nr + nv + ng:nr + nv + ng + len(acc_in)])
            drow_refs = refs[nr + nv + ng + len(acc_in):nr + nv + ng + len(acc_in) + nd]
            dvec_refs = refs[nr + nv + ng + len(acc_in) + nd:]
            _, vjp = jax.vjp(tile_fn, *args)
            grads = vjp(tuple(g[...] for g in g_refs))
            for d_ref, k, a in zip(drow_refs, diff_rows, acc):
                gk = grads[k] if a is None else grads[k] + acc_refs.pop(0)[...]
                d_ref[...] = gk.astype(d_ref.dtype)

            @pl.when(i == 0)
            def _():
                for d_ref in dvec_refs:
                    d_ref[...] = jnp.zeros_like(d_ref)

            for j, d_ref in enumerate(dvec_refs):
                d_ref[grp] += grads[nr + j]

        outs = pl.pallas_call(
            body,
            name=name + "_bwd",
            grid=(t // tm,),
            in_specs=[row_spec(k, r) for k, r in enumerate(rows)]
            + [pl.BlockSpec(v.shape, lambda i: (0, 0, 0)) for v in vecs]
            + [pl.BlockSpec((tm, g.shape[1]), lambda i: (i, 0)) for g in gs]
            + [pl.BlockSpec((tm, a.shape[1]), lambda i: (i, 0)) for a in acc_in],
            out_specs=[pl.BlockSpec((tm, row_width(k, rows[k])), lambda i: (i, 0)) for k in diff_rows]
            + [pl.BlockSpec(v.shape, lambda i: (0, 0, 0)) for v in vecs],
            out_shape=[jax.ShapeDtypeStruct((t, row_width(k, rows[k])), drow_dtypes.get(k, rows[k].dtype))
                       for k in diff_rows]
            + [jax.ShapeDtypeStruct(v.shape, F32) for v in vecs],
            compiler_params=_params(("arbitrary",)),
        )(*rows, *vecs, *gs, *acc_in)
        return outs[:nd], outs[nd:]

    return fwd_call, bwd_call


def _f_norm_mod(x, g, scale, shift):
    r = lax.rsqrt(jnp.mean(x * x, axis=-1, keepdims=True) + NORM_EPS)
    return ((x * r * g) * (1.0 + scale) + shift,)


def _f_gate_na(o, z):
    return (o.astype(F32) * _silu(z.astype(F32)),)


def _f_merge(g_na, g_ret, y_na, y_ret):
    return (jax.nn.sigmoid(g_na.astype(F32)) * y_na.astype(F32) + jax.nn.sigmoid(g_ret.astype(F32)) * y_ret.astype(F32),)


def _f_residual(x, out, gate):
    return (x + gate * out,)


def _f_loss(x, target, g):
    r = lax.rsqrt(jnp.mean(x * x, axis=-1, keepdims=True) + NORM_EPS)
    y = x * r * g
    e = 0.5 * jnp.mean(jnp.square(y - target), axis=-1, keepdims=True)
    return (jnp.broadcast_to(e * (1.0 / 128.0), (x.shape[0], 128)),)


def _gate_ret_fwd_call(of, ob, z, zblk, tm):
    t, w = of.shape
    nh = w // RET_VAL_DIM

    def body(of_ref, ob_ref, z_ref, a_ref):
        for hh in range(nh):
            sl = slice(hh * RET_VAL_DIM, (hh + 1) * RET_VAL_DIM)
            o = of_ref[:, sl].astype(F32) + ob_ref[:, sl].astype(F32)
            r = lax.rsqrt(jnp.mean(o * o, axis=-1, keepdims=True) + NORM_EPS)
            a_ref[:, sl] = ((o * r) * _silu(z_ref[:, sl].astype(F32))).astype(a_ref.dtype)

    spec = pl.BlockSpec((tm, w), lambda i: (i, 0))
    zspec = pl.BlockSpec((tm, w), lambda i: (i, zblk))
    return pl.pallas_call(
        body, name="gate_ret_fwd", grid=(t // tm,), in_specs=[spec, spec, zspec], out_specs=spec,
        out_shape=jax.ShapeDtypeStruct((t, w), BF16), compiler_params=_params(("parallel",)),
    )(of, ob, z)


def _gate_ret_bwd_call(of, ob, z, zblk, da, tm):
    t, w = of.shape
    nh = w // RET_VAL_DIM

    def body(of_ref, ob_ref, z_ref, da_ref, do_ref, dz_ref):
        for hh in range(nh):
            sl = slice(hh * RET_VAL_DIM, (hh + 1) * RET_VAL_DIM)
            o = of_ref[:, sl].astype(F32) + ob_ref[:, sl].astype(F32)
            r = lax.rsqrt(jnp.mean(o * o, axis=-1, keepdims=True) + NORM_EPS)
            n = o * r
            zf = z_ref[:, sl].astype(F32)
            sg = jax.nn.sigmoid(zf)
            g = da_ref[:, sl].astype(F32)
            dn = g * (zf * sg)
            dz_ref[:, sl] = (g * n * (sg * (1.0 + zf * (1.0 - sg)))).astype(dz_ref.dtype)
            do_ref[:, sl] = (r * (dn - n * jnp.mean(dn * n, axis=-1, keepdims=True))).astype(do_ref.dtype)

    spec = pl.BlockSpec((tm, w), lambda i: (i, 0))
    zspec = pl.BlockSpec((tm, w), lambda i: (i, zblk))
    return pl.pallas_call(
        body, name="gate_ret_bwd", grid=(t // tm,), in_specs=[spec, spec, zspec, spec], out_specs=[spec, spec],
        out_shape=[jax.ShapeDtypeStruct((t, w), BF16), jax.ShapeDtypeStruct((t, w), z.dtype)],
        compiler_params=_params(("parallel",)),
    )(of, ob, z, da)


NA_PAIR = 2 * GRID_W
NA_KEY_ROWS = NA_WIN_ROWS + 2
NA_CLASSES = 5


def _na_geometry(t, n_lat):
    rows = n_lat // GRID_W
    assert rows % 2 == 0 and rows >= NA_KEY_ROWS + 2, rows
    return rows, rows // 2, NA_KEY_ROWS * GRID_W, t - n_lat, t // NA_PAIR


def _na_base(p, rows):
    return jnp.clip(2 * p - NA_WIN_ROWS // 2, 0, rows - NA_KEY_ROWS)


def _na_class(p, rows):
    return p - _na_base(p, rows) // 2


def _na_group(pairs, n_ctx):
    assert n_ctx % NA_PAIR == 0, n_ctx
    return 2 if pairs % 2 == 0 and (n_ctx // NA_PAIR) % 2 == 0 else 1


def _na_bias_spec():
    return pl.BlockSpec((1, NA_CLASSES, 2, NA_KEY_ROWS // 2, GRID_W, NA_PAIR), lambda h, g: (h, 0, 0, 0, 0, 0))


def _na_bias_tile(bt_ref, cls):
    return jnp.concatenate(
        [jnp.concatenate([bt_ref[0, cls, i, q] for q in range(NA_KEY_ROWS // 2)], axis=1) for i in range(2)], axis=0)


def _na_add_bias_grad(dbt_ref, cls, ds):
    for i in range(2):
        for q in range(NA_KEY_ROWS // 2):
            dbt_ref[0, cls, i, q] += ds[i * GRID_W:(i + 1) * GRID_W, q * NA_PAIR:(q + 1) * NA_PAIR]


def _na_fwd_call(q, k, v, col0, w, bt, n_lat):
    t = q.shape[0]
    nh = w // NA_HEAD_DIM
    rows, pairs, n_loc, n_ctx, nq = _na_geometry(t, n_lat)
    grp = _na_group(pairs, n_ctx)
    scale = NA_HEAD_DIM ** -0.5

    def body(q_ref, k_ref, v_ref, bt_ref, o_ref):
        g = pl.program_id(1)
        kc = k_ref[pl.ds(n_lat, n_ctx), :]
        vc = v_ref[pl.ds(n_lat, n_ctx), :]

        @pl.when(g < pairs // grp)
        def _():
            for i in range(grp):
                p = g * grp + i
                sl = slice(i * NA_PAIR, (i + 1) * NA_PAIR)
                qb = q_ref[sl, :]
                s_ctx = _dot_nt(qb, kc) * scale
                start = pl.multiple_of(_na_base(p, rows) * GRID_W, GRID_W)
                kw = k_ref[pl.ds(start, n_loc), :]
                vw = v_ref[pl.ds(start, n_loc), :]
                s_loc = _dot_nt(qb, kw) * scale + _na_bias_tile(bt_ref, _na_class(p, rows))
                m = jnp.maximum(jnp.max(s_loc, axis=-1, keepdims=True), jnp.max(s_ctx, axis=-1, keepdims=True))
                p_loc = jnp.exp(s_loc - m)
                p_ctx = jnp.exp(s_ctx - m)
                l = jnp.sum(p_loc, axis=-1, keepdims=True) + jnp.sum(p_ctx, axis=-1, keepdims=True)
                o = _dot(p_loc.astype(BF16), vw) + _dot(p_ctx.astype(BF16), vc)
                o_ref[sl, :] = (o / l).astype(o_ref.dtype)

        @pl.when(g >= pairs // grp)
        def _():
            s_ctx = _dot_nt(q_ref[...], kc) * scale
            m = jnp.max(s_ctx, axis=-1, keepdims=True)
            p = jnp.exp(s_ctx - m)
            l = jnp.sum(p, axis=-1, keepdims=True)
            o_ref[...] = (_dot(p.astype(BF16), vc) / l).astype(o_ref.dtype)

    qspec = pl.BlockSpec((grp * NA_PAIR, NA_HEAD_DIM), lambda h, g: (g, h))
    in_q = pl.BlockSpec((grp * NA_PAIR, NA_HEAD_DIM), lambda h, g: (g, col0[0] + h))
    in_k = pl.BlockSpec((t, NA_HEAD_DIM), lambda h, g: (0, col0[1] + h))
    in_v = pl.BlockSpec((t, NA_HEAD_DIM), lambda h, g: (0, col0[2] + h))
    return pl.pallas_call(
        body,
        name="na_attn_fwd",
        grid=(nh, nq // grp),
        in_specs=[in_q, in_k, in_v, _na_bias_spec()],
        out_specs=qspec,
        out_shape=jax.ShapeDtypeStruct((t, w), BF16),
        compiler_params=_params(("parallel", "arbitrary")),
    )(q, k, v, bt)


def _na_bwd_call(q, k, v, col0, w, bt, do, n_lat):
    t = q.shape[0]
    nh = w // NA_HEAD_DIM
    rows, pairs, n_loc, n_ctx, nq = _na_geometry(t, n_lat)
    scale = NA_HEAD_DIM ** -0.5
    grp = _na_group(pairs, n_ctx)

    def body(q_ref, k_ref, v_ref, do_ref, bt_ref, dq_ref, dk_ref, dv_ref, dbt_ref):
        g = pl.program_id(1)

        @pl.when(g == 0)
        def _():
            dk_ref[...] = jnp.zeros_like(dk_ref)
            dv_ref[...] = jnp.zeros_like(dv_ref)
            dbt_ref[...] = jnp.zeros_like(dbt_ref)

        kc = k_ref[pl.ds(n_lat, n_ctx), :]
        vc = v_ref[pl.ds(n_lat, n_ctx), :]

        @pl.when(g < pairs // grp)
        def _():
            for i in range(grp):
                p = g * grp + i
                sl = slice(i * NA_PAIR, (i + 1) * NA_PAIR)
                qb = q_ref[sl, :]
                dob = do_ref[sl, :]
                s_ctx = _dot_nt(qb, kc) * scale
                dp_ctx = _dot_nt(dob, vc)
                start = pl.multiple_of(_na_base(p, rows) * GRID_W, GRID_W)
                kw = k_ref[pl.ds(start, n_loc), :]
                vw = v_ref[pl.ds(start, n_loc), :]
                cls = _na_class(p, rows)
                s_loc = _dot_nt(qb, kw) * scale + _na_bias_tile(bt_ref, cls)
                m = jnp.maximum(jnp.max(s_loc, axis=-1, keepdims=True), jnp.max(s_ctx, axis=-1, keepdims=True))
                p_loc = jnp.exp(s_loc - m)
                p_ctx = jnp.exp(s_ctx - m)
                inv = 1.0 / (jnp.sum(p_loc, axis=-1, keepdims=True) + jnp.sum(p_ctx, axis=-1, keepdims=True))
                p_loc = p_loc * inv
                p_ctx = p_ctx * inv
                dp_loc = _dot_nt(dob, vw)
                delta = (jnp.sum(p_loc * dp_loc, axis=-1, keepdims=True)
                         + jnp.sum(p_ctx * dp_ctx, axis=-1, keepdims=True))
                ds_loc = p_loc * (dp_loc - delta)
                ds_ctx = p_ctx * (dp_ctx - delta)
                _na_add_bias_grad(dbt_ref, cls, ds_loc)
                dsl = (ds_loc * scale).astype(BF16)
                dsc = (ds_ctx * scale).astype(BF16)
                dq_ref[sl, :] = (_dot(dsl, kw) + _dot(dsc, kc)).astype(dq_ref.dtype)
                dk_ref[pl.ds(start, n_loc), :] += _dot_tn(dsl, qb)
                dv_ref[pl.ds(start, n_loc), :] += _dot_tn(p_loc.astype(BF16), dob)
                dk_ref[pl.ds(n_lat, n_ctx), :] += _dot_tn(dsc, qb)
                dv_ref[pl.ds(n_lat, n_ctx), :] += _dot_tn(p_ctx.astype(BF16), dob)

        @pl.when(g >= pairs // grp)
        def _():
            qb = q_ref[...]
            dob = do_ref[...]
            s_ctx = _dot_nt(qb, kc) * scale
            dp_ctx = _dot_nt(dob, vc)
            m = jnp.max(s_ctx, axis=-1, keepdims=True)
            p = jnp.exp(s_ctx - m)
            p = p * (1.0 / jnp.sum(p, axis=-1, keepdims=True))
            delta = jnp.sum(p * dp_ctx, axis=-1, keepdims=True)
            dsc = (p * (dp_ctx - delta) * scale).astype(BF16)
            dq_ref[...] = _dot(dsc, kc).astype(dq_ref.dtype)
            dk_ref[pl.ds(n_lat, n_ctx), :] += _dot_tn(dsc, qb)
            dv_ref[pl.ds(n_lat, n_ctx), :] += _dot_tn(p.astype(BF16), dob)

    qspec = pl.BlockSpec((grp * NA_PAIR, NA_HEAD_DIM), lambda h, g: (g, h))
    kspec = pl.BlockSpec((t, NA_HEAD_DIM), lambda h, g: (0, h))
    return pl.pallas_call(
        body,
        name="na_attn_bwd",
        grid=(nh, nq // grp),
        in_specs=[pl.BlockSpec((grp * NA_PAIR, NA_HEAD_DIM), lambda h, g: (g, col0[0] + h)),
                  pl.BlockSpec((t, NA_HEAD_DIM), lambda h, g: (0, col0[1] + h)),
                  pl.BlockSpec((t, NA_HEAD_DIM), lambda h, g: (0, col0[2] + h)), qspec, _na_bias_spec()],
        out_specs=[qspec, kspec, kspec, _na_bias_spec()],
        out_shape=[
            jax.ShapeDtypeStruct((t, w), BF16),
            jax.ShapeDtypeStruct((t, w), F32),
            jax.ShapeDtypeStruct((t, w), F32),
            jax.ShapeDtypeStruct(bt.shape, F32),
        ],
        compiler_params=_params(("parallel", "arbitrary")),
    )(q, k, v, do, bt)


def _na_bias_table(rpb, rows):
    pairs = rows // 2
    nb = 2 * NA_WIN_COLS - 1
    nq = NA_KEY_ROWS // 2
    e1 = np.zeros((NA_CLASSES, 2, nq, 2, 2 * NA_WIN_ROWS - 1), np.float32)
    valid = np.zeros((NA_CLASSES, 2, nq, 2), bool)
    for cls, p in enumerate((0, 1, 2, pairs - 2, pairs - 1)):
        base = int(np.clip(2 * p - NA_WIN_ROWS // 2, 0, rows - NA_KEY_ROWS))
        assert p - base // 2 == cls, (rows, cls, p, base)
        for i in range(2):
            r = 2 * p + i
            r0 = int(np.clip(r - NA_WIN_ROWS // 2, 0, rows - NA_WIN_ROWS))
            for kk in range(NA_KEY_ROWS):
                if r0 <= base + kk < r0 + NA_WIN_ROWS:
                    valid[cls, i, kk // 2, kk % 2] = True
                    e1[cls, i, kk // 2, kk % 2, base + kk - r + NA_WIN_ROWS - 1] = 1.0
    cidx = np.arange(GRID_W)
    dc = np.clip(cidx[None, :] - cidx[:, None] + (NA_WIN_COLS - 1), 0, nb - 1)
    c0 = np.clip(cidx - NA_WIN_COLS // 2, 0, GRID_W - NA_WIN_COLS)
    col_in = (cidx[None, :] >= c0[:, None]) & (cidx[None, :] < c0[:, None] + NA_WIN_COLS)
    e2 = np.zeros((GRID_W, 2, GRID_W, 2, nb), np.float32)
    for par in range(2):
        e2[np.arange(GRID_W)[:, None], par, np.arange(GRID_W)[None, :], par, dc] = 1.0
    mask = valid[:, :, :, None, :, None] & col_in[None, None, None, :, None, :]
    t1 = jnp.einsum("hab,xiqpa->hxiqpb", rpb, jnp.asarray(e1), precision=lax.Precision.HIGHEST)
    t1 = t1.reshape(t1.shape[:4] + (2 * nb,))
    b = jnp.einsum("hxiqm,cwm->hxiqcw", t1, jnp.asarray(e2.reshape(GRID_W, 2 * GRID_W, 2 * nb)),
                   precision=lax.Precision.HIGHEST)
    return jnp.where(jnp.asarray(mask.reshape(NA_CLASSES, 2, nq, GRID_W, 2 * GRID_W))[None], b, MASK_VALUE)


def _ret_decays(lam_s, reverse):
    c = RET_CHUNK
    ii = lax.broadcasted_iota(jnp.int32, (c, c), 0)
    jj = lax.broadcasted_iota(jnp.int32, (c, c), 1)
    d = (jj - ii) if reverse else (ii - jj)
    dpos = jnp.maximum(d.astype(F32), 0.0)
    mask = jnp.where(d >= 0, jnp.exp(dpos * lam_s), 0.0)
    pi = lax.broadcasted_iota(jnp.int32, (c, 1), 0).astype(F32)
    qpos = (c - pi) if reverse else (pi + 1.0)
    kpos = pi if reverse else (c - 1.0 - pi)
    qd = jnp.exp(qpos * lam_s)
    kd = jnp.exp(kpos * lam_s)
    g = jnp.exp(jnp.full((1, RET_VAL_DIM), c * lam_s, F32))
    return mask, dpos, qd, kd, qpos, kpos, g


def _ret_head_group(nh):
    return _pick(nh, (8, 4, 2))


def _ret_chunk_of(t, nt, nl, reverse):
    return (nt - 1 - t) if reverse else (t + nl) % nt


def _ret_fwd_call(qr, kr, v, vcol, lam, n_lat, reverse):
    t = qr.shape[0]
    nh = qr.shape[1] // RET_KEY_DIM
    c = RET_CHUNK
    nt, nl = t // c, n_lat // c

    hg = _ret_head_group(nh)
    dk, dv = RET_KEY_DIM, RET_VAL_DIM

    def body(lam_ref, q_ref, k_ref, v_ref, o_ref, s_ref, state):
        hb, step = pl.program_id(0), pl.program_id(1)

        @pl.when(step == 0)
        def _():
            state[...] = jnp.zeros_like(state)

        for j in range(hg):
            mask, _, qd, kd, _, _, g = _ret_decays(lam_ref[hb * hg + j], reverse)
            q, k, vv = q_ref[:, j * dk:(j + 1) * dk], k_ref[:, j * dk:(j + 1) * dk], v_ref[:, j * dv:(j + 1) * dv]
            p = _dot_nt(q, k) * mask
            s = state[j]
            qs = (q.astype(F32) * qd).astype(BF16)
            o_ref[:, j * dv:(j + 1) * dv] = (_dot(p.astype(BF16), vv) + _dot(qs, s.astype(BF16))).astype(o_ref.dtype)
            s_ref[j, 0] = s
            ks = (k.astype(F32) * kd).astype(BF16)
            state[j] = s * g + _dot_tn(ks, vv)

    def cmap(hb, step, lam_ref):
        return (_ret_chunk_of(step, nt, nl, reverse), hb)

    def vmap(hb, step, lam_ref):
        return (_ret_chunk_of(step, nt, nl, reverse), vcol // (hg * dv) + hb)

    return pl.pallas_call(
        body,
        name="retention_rev_fwd" if reverse else "retention_fwd",
        grid_spec=pltpu.PrefetchScalarGridSpec(
            num_scalar_prefetch=1,
            grid=(nh // hg, nt),
            in_specs=[
                pl.BlockSpec((c, hg * dk), cmap),
                pl.BlockSpec((c, hg * dk), cmap),
                pl.BlockSpec((c, hg * dv), vmap),
            ],
            out_specs=[
                pl.BlockSpec((c, hg * dv), cmap),
                pl.BlockSpec((hg, 1, dk, dv), lambda hb, step, lam_ref: (hb, step, 0, 0)),
            ],
            scratch_shapes=[pltpu.VMEM((hg, dk, dv), F32)],
        ),
        out_shape=[
            jax.ShapeDtypeStruct((t, nh * RET_VAL_DIM), BF16),
            jax.ShapeDtypeStruct((nh, nt, RET_KEY_DIM, RET_VAL_DIM), F32),
        ],
        compiler_params=_params(("parallel", "arbitrary")),
    )(lam, qr, kr, v)


def _ret_bwd_call(qr, kr, v, vcol, lam, states, do, n_lat, reverse):
    assert RET_CHUNK == RET_KEY_DIM and RET_VAL_DIM % RET_KEY_DIM == 0
    t = qr.shape[0]
    nh = qr.shape[1] // RET_KEY_DIM
    c = RET_CHUNK
    nt, nl = t // c, n_lat // c

    hg = _ret_head_group(nh)
    dk, dv = RET_KEY_DIM, RET_VAL_DIM

    def body(lam_ref, q_ref, k_ref, v_ref, s_ref, do_ref, dq_ref, dk_ref, dv_ref, dl_ref, dstate):
        hb, rstep = pl.program_id(0), pl.program_id(1)

        @pl.when(rstep == 0)
        def _():
            dstate[...] = jnp.zeros_like(dstate)
            dl_ref[...] = jnp.zeros_like(dl_ref)

        for j in range(hg):
            mask, dpos, qd, kd, qpos, kpos, g = _ret_decays(lam_ref[hb * hg + j], reverse)
            ksl, vsl = slice(j * dk, (j + 1) * dk), slice(j * dv, (j + 1) * dv)
            q, k, vv = q_ref[:, ksl], k_ref[:, ksl], v_ref[:, vsl]
            qf, kf = q.astype(F32), k.astype(F32)
            s = s_ref[j, 0]
            ds = dstate[j]
            dob = do_ref[:, vsl].astype(BF16)
            sb, dsb = s.astype(BF16), ds.astype(BF16)
            a = _dot_nt(q, k)
            p = a * mask
            dp = _dot_nt(dob, vv)
            da = dp * mask
            dab = da.astype(BF16)
            dqc = _dot_nt(dob, sb)
            dkc = _dot_nt(vv, dsb)
            qs = (qf * qd).astype(BF16)
            ks = (kf * kd).astype(BF16)
            dq_ref[:, ksl] = (_dot(dab, k) + dqc * qd).astype(dq_ref.dtype)
            dk_ref[:, ksl] = (_dot_tn(dab, q) + dkc * kd).astype(dk_ref.dtype)
            dv_ref[:, vsl] = (_dot_tn(p.astype(BF16), dob) + _dot(ks, dsb)).astype(dv_ref.dtype)
            dsg = ds * s * (g * c)
            terms = da * a * dpos + dqc * qf * (qd * qpos) + dkc * kf * (kd * kpos)
            for half in range(dv // dk):
                terms = terms + dsg[:, half * dk:(half + 1) * dk]
            total = jnp.sum(jnp.sum(terms, axis=0, keepdims=True), axis=1, keepdims=True)
            dl_ref[j] += jnp.broadcast_to(total, (8, 128))
            dstate[j] = ds * g + _dot_tn(qs, dob)

    def cmap(hb, rstep, lam_ref):
        return (_ret_chunk_of(nt - 1 - rstep, nt, nl, reverse), hb)

    def vmap(hb, rstep, lam_ref):
        return (_ret_chunk_of(nt - 1 - rstep, nt, nl, reverse), vcol // (hg * dv) + hb)

    return pl.pallas_call(
        body,
        name="retention_rev_bwd" if reverse else "retention_bwd",
        grid_spec=pltpu.PrefetchScalarGridSpec(
            num_scalar_prefetch=1,
            grid=(nh // hg, nt),
            in_specs=[
                pl.BlockSpec((c, hg * dk), cmap),
                pl.BlockSpec((c, hg * dk), cmap),
                pl.BlockSpec((c, hg * dv), vmap),
                pl.BlockSpec((hg, 1, dk, dv), lambda hb, rstep, lam_ref: (hb, nt - 1 - rstep, 0, 0)),
                pl.BlockSpec((c, hg * dv), cmap),
            ],
            out_specs=[
                pl.BlockSpec((c, hg * dk), cmap),
                pl.BlockSpec((c, hg * dk), cmap),
                pl.BlockSpec((c, hg * dv), cmap),
                pl.BlockSpec((hg, 8, 128), lambda hb, rstep, lam_ref: (hb, 0, 0)),
            ],
            scratch_shapes=[pltpu.VMEM((hg, dk, dv), F32)],
        ),
        out_shape=[
            jax.ShapeDtypeStruct(qr.shape, qr.dtype),
            jax.ShapeDtypeStruct(kr.shape, kr.dtype),
            jax.ShapeDtypeStruct((t, nh * dv), v.dtype),
            jax.ShapeDtypeStruct((nh, 8, 128), F32),
        ],
        compiler_params=_params(("parallel", "arbitrary")),
    )(lam, qr, kr, v, states, do)


def _rope_tables(t, n_lat):
    nf = RET_KEY_DIM // 4
    tok = np.arange(n_lat)
    inv_freq = (ROPE_BASE ** (-np.arange(nf, dtype=np.float32) / nf)).astype(np.float32)
    row = (tok // GRID_W).astype(np.float32)
    col = (tok % GRID_W).astype(np.float32)
    ang = np.concatenate([row[:, None] * inv_freq, col[:, None] * inv_freq], axis=-1).astype(np.float32)
    cos = np.ones((t, 2 * nf), np.float32)
    sin = np.zeros((t, 2 * nf), np.float32)
    cos[:n_lat] = np.cos(ang)
    sin[:n_lat] = np.sin(ang)
    return jnp.asarray(np.concatenate([cos, cos], axis=1)), jnp.asarray(np.concatenate([-sin, sin], axis=1))


def _rotate(x, cos2, sin2):
    return x * cos2 + pltpu.roll(x, RET_KEY_DIM // 2, 1) * sin2


def _rope_fwd_call(u, qblk, kblk, w_qk, cos2, sin2, k_scale, tm):
    t = u.shape[0]
    nh = w_qk // RET_KEY_DIM

    def body(q_ref, k_ref, c_ref, s_ref, qr_ref, kr_ref):
        c, s = c_ref[...], s_ref[...]
        for hh in range(nh):
            sl = slice(hh * RET_KEY_DIM, (hh + 1) * RET_KEY_DIM)
            qr_ref[:, sl] = _rotate(q_ref[:, sl].astype(F32), c, s).astype(qr_ref.dtype)
            kr_ref[:, sl] = (_rotate(k_ref[:, sl].astype(F32), c, s) * k_scale).astype(kr_ref.dtype)

    tab = pl.BlockSpec((tm, RET_KEY_DIM), lambda i: (i, 0))
    out = pl.BlockSpec((tm, w_qk), lambda i: (i, 0))
    return pl.pallas_call(
        body, name="rope_fwd", grid=(t // tm,),
        in_specs=[pl.BlockSpec((tm, w_qk), lambda i: (i, qblk)), pl.BlockSpec((tm, w_qk), lambda i: (i, kblk)), tab, tab],
        out_specs=[out, out], out_shape=[jax.ShapeDtypeStruct((t, w_qk), BF16)] * 2,
        compiler_params=_params(("parallel",)),
    )(u, u, cos2, sin2)


def _assemble_du_call(pieces, off, cos2, sin2, k_scale, tm):
    flat = [a for p in pieces for a in (p if isinstance(p, tuple) else (p,))]
    t = flat[0].shape[0]
    n = len(flat)

    def body(*refs):
        c, s = refs[n][...], -refs[n + 1][...]
        o_ref = refs[n + 2]
        it = iter(refs[:n])
        for blk, p in enumerate(pieces):
            lo = off[blk]
            if not isinstance(p, tuple):
                o_ref[:, lo:off[blk + 1]] = next(it)[...].astype(o_ref.dtype)
                continue
            fwd_ref, rev_ref = next(it), next(it)
            if blk == 6:
                o_ref[:, lo:off[blk + 1]] = (fwd_ref[...].astype(F32) + rev_ref[...].astype(F32)).astype(o_ref.dtype)
                continue
            mult = k_scale if blk == 5 else 1.0
            for hh in range((off[blk + 1] - lo) // RET_KEY_DIM):
                sl = slice(hh * RET_KEY_DIM, (hh + 1) * RET_KEY_DIM)
                dy = (fwd_ref[:, sl].astype(F32) + rev_ref[:, sl].astype(F32)) * mult
                o_ref[:, lo + hh * RET_KEY_DIM:lo + (hh + 1) * RET_KEY_DIM] = _rotate(dy, c, s).astype(o_ref.dtype)

    tab = pl.BlockSpec((tm, RET_KEY_DIM), lambda i: (i, 0))
    return pl.pallas_call(
        body, name="assemble_du", grid=(t // tm,),
        in_specs=[pl.BlockSpec((tm, a.shape[1]), lambda i: (i, 0)) for a in flat] + [tab, tab],
        out_specs=pl.BlockSpec((tm, off[-1]), lambda i: (i, 0)),
        out_shape=jax.ShapeDtypeStruct((t, off[-1]), BF16),
        compiler_params=_params(("parallel",)),
    )(*flat, cos2, sin2)


def _my_position():
    return lax.axis_index("x"), lax.axis_index("y"), lax.axis_index("c")


def _flip(pos, k):
    x, y, c = pos
    return (1 - x if k & 4 else x, 1 - y if k & 2 else y, 1 - c if k & 1 else c)


def _linear(pos):
    return 4 * pos[0] + 2 * pos[1] + pos[2]


def _slab(ref, axis, idx, size):
    start = pl.multiple_of(idx * size, size)
    return ref.at[pl.ds(start, size), :] if axis == 0 else ref.at[:, pl.ds(start, size)]


HBM_SPEC = pl.BlockSpec(memory_space=pltpu.HBM)
SEM_SPEC = pl.BlockSpec(memory_space=pltpu.SEMAPHORE)
DATAFLOW = pltpu.SideEffectType.DATAFLOW_SIDE_EFFECTING
PEER_BITS = (1, 2, 4, 6, 3, 5, 7)
GATHER_BITS = (1, 2, 4, 6)


def _in_hbm(a):
    return pltpu.with_memory_space_constraint(a, pltpu.HBM)


def _gather_views(me, k, a, src_refs, land_refs, axes):
    size = src_refs[a].shape[axes[a]]
    peer = _flip(me, k)
    return src_refs[a], _slab(land_refs[a], axes[a], _linear(me), size), _slab(land_refs[a], axes[a], _linear(peer), size)


def _scatter_views(me, k, a, src_refs, land_refs, axes):
    size = land_refs[a].shape[1 + axes[a]]
    peer = _flip(me, k)
    return _slab(src_refs[a], axes[a], _linear(peer), size), land_refs[a].at[k - 1], land_refs[a].at[k - 1]


CHIP_BITS = (0, 2, 4, 6)


def _chip_views(me, k, a, src_refs, land_refs, axes):
    j = CHIP_BITS.index(k)
    return src_refs[a].at[j], land_refs[a].at[j - 1], land_refs[a].at[j - 1]


def _pair_exchange(grads, axes, sizes):
    ns = len(grads)

    def slab_shape(a):
        s = grads[a].shape
        return (sizes[a], s[1]) if axes[a] == 0 else (s[0], sizes[a])

    def body(*refs):
        g_refs, p_refs = refs[:ns], refs[ns:2 * ns]
        send_sems, recv_sems = refs[2 * ns:]
        me = _my_position()
        sibling = _flip(me, 1)
        copies = []
        for j, kc in enumerate(CHIP_BITS):
            for a in range(ns):
                cp = pltpu.make_async_remote_copy(
                    src_ref=_slab(g_refs[a], axes[a], _linear(_flip(me, kc | 1)), sizes[a]), dst_ref=p_refs[a].at[j],
                    send_sem=send_sems.at[4 * a + j], recv_sem=recv_sems.at[4 * a + j],
                    device_id=sibling, device_id_type=MESH)
                cp.start()
                copies.append(cp)
        for cp in copies:
            cp.wait_recv()
        for cp in copies:
            cp.wait_send()

    return pl.pallas_call(
        body, name="scatter_pair_exchange", in_specs=[ANY] * ns, out_specs=[ANY] * ns,
        out_shape=[jax.ShapeDtypeStruct((4,) + slab_shape(a), grads[a].dtype) for a in range(ns)],
        scratch_shapes=[pltpu.SemaphoreType.DMA((4 * ns,)), pltpu.SemaphoreType.DMA((4 * ns,))],
        compiler_params=pltpu.CompilerParams(has_side_effects=True),
    )(*grads)


def _pair_add(grad, theirs, axis, chip_idx):
    _, r, c = theirs.shape
    tm = _pick(r, (256, 128, 64, 32, 16))
    if axis == 0:
        mine_spec = pl.BlockSpec((tm, c), lambda j, i, idx: (idx[j] * (r // tm) + i, 0))
    else:
        mine_spec = pl.BlockSpec((tm, c), lambda j, i, idx: (i, idx[j]))

    def body(idx_ref, mine_ref, theirs_ref, o_ref):
        o_ref[0] = (mine_ref[...].astype(F32) + theirs_ref[0].astype(F32)).astype(o_ref.dtype)

    spec = pl.BlockSpec((1, tm, c), lambda j, i, idx: (j, i, 0))
    return pl.pallas_call(
        body, name="scatter_pair_add",
        grid_spec=pltpu.PrefetchScalarGridSpec(
            num_scalar_prefetch=1, grid=(4, r // tm), in_specs=[mine_spec, spec], out_specs=spec),
        out_shape=jax.ShapeDtypeStruct(theirs.shape, theirs.dtype),
        compiler_params=_params(("parallel", "parallel")),
    )(chip_idx, grad, theirs)


def _slab_block(rows, cols, tm, axis):
    if axis == 0:
        return pl.BlockSpec((tm, cols), lambda i, idx: (idx[0] * (rows // tm) + i, 0))
    return pl.BlockSpec((tm, cols), lambda i, idx: (i, idx[0]))


def _place_shard(shard, land, axis, my_idx):
    r, c = shard.shape
    tm = _pick(r, (512, 256, 128, 64, 32, 16))

    def body(idx_ref, s_ref, land_ref, o_ref):
        o_ref[...] = s_ref[...]

    return pl.pallas_call(
        body, name="gather_place",
        grid_spec=pltpu.PrefetchScalarGridSpec(
            num_scalar_prefetch=1, grid=(r // tm,),
            in_specs=[pl.BlockSpec((tm, c), lambda i, idx: (i, 0)), ANY],
            out_specs=_slab_block(r, c, tm, axis)),
        out_shape=jax.ShapeDtypeStruct(land.shape, land.dtype),
        input_output_aliases={2: 0},
        compiler_params=_params(("parallel",)),
    )(my_idx, shard, land)


def _push_start(name, srcs, lands, axes, views, bits, deps):
    ns = len(srcs)

    def body(*refs):
        src_refs, land_refs = refs[:ns], refs[ns:2 * ns]
        send_sems, recv_sems = refs[2 * ns + len(deps):2 * ns + len(deps) + 2]
        token = refs[-1]
        me = _my_position()
        for k in bits:
            for a in range(ns):
                s, d, _ = views(me, k, a, src_refs, land_refs, axes)
                pltpu.make_async_remote_copy(
                    src_ref=s, dst_ref=d, send_sem=send_sems.at[7 * a + k - 1], recv_sem=recv_sems.at[7 * a + k - 1],
                    device_id=_flip(me, k), device_id_type=MESH).start()
        token[...] = jnp.zeros_like(token)

    thru = [pltpu.HBM(a.shape, a.dtype) for a in list(srcs) + list(lands)]
    outs = pl.pallas_call(
        body, name=name,
        in_specs=[HBM_SPEC] * (2 * ns) + [ANY] * len(deps),
        out_specs=[SEM_SPEC, SEM_SPEC] + [HBM_SPEC] * (2 * ns) + [VMEM_SPEC],
        out_shape=[pltpu.SemaphoreType.DMA((7 * ns,)), pltpu.SemaphoreType.DMA((7 * ns,))] + thru
        + [jax.ShapeDtypeStruct((8, 128), F32)],
        input_output_aliases={i: 2 + i for i in range(2 * ns)},
        compiler_params=pltpu.CompilerParams(has_side_effects=DATAFLOW),
    )(*[_in_hbm(a) for a in srcs], *[_in_hbm(a) for a in lands], *deps)
    return (outs[0], outs[1]), outs[2:2 + ns], outs[2 + ns:2 + 2 * ns], outs[-1]


def _gather_finish(lands, axes, sizes):
    ns = len(lands)
    chips = (2, 4, 6)

    def body(*refs):
        land_refs = refs[ns:2 * ns]
        send_sems, recv_sems = refs[2 * ns:]
        me = _my_position()
        sibling = _flip(me, 1)
        copies = []
        for j, kc in enumerate(chips):
            for a in range(ns):
                def slab_of(pos):
                    return _slab(land_refs[a], axes[a], _linear(pos), sizes[a])
                send = pltpu.make_async_remote_copy(
                    src_ref=slab_of(_flip(me, kc)), dst_ref=slab_of(_flip(me, kc)), send_sem=send_sems.at[3 * a + j],
                    recv_sem=recv_sems.at[3 * a + j], device_id=sibling, device_id_type=MESH)
                recv = pltpu.make_async_remote_copy(
                    src_ref=slab_of(_flip(me, kc)), dst_ref=slab_of(_flip(sibling, kc)), send_sem=send_sems.at[3 * a + j],
                    recv_sem=recv_sems.at[3 * a + j], device_id=sibling, device_id_type=MESH)
                send.start()
                copies.append((send, recv))
        for send, recv in copies:
            recv.wait_recv()
        for send, recv in copies:
            send.wait_send()

    return pl.pallas_call(
        body, name="gather_finish", in_specs=[ANY] * ns, out_specs=[ANY] * ns,
        out_shape=[jax.ShapeDtypeStruct(l.shape, l.dtype) for l in lands],
        input_output_aliases={a: a for a in range(ns)},
        scratch_shapes=[pltpu.SemaphoreType.DMA((3 * ns,)), pltpu.SemaphoreType.DMA((3 * ns,))],
        compiler_params=pltpu.CompilerParams(has_side_effects=True),
    )(*lands)


def _push_wait(name, sems, srcs, lands, axes, views, bits, after):
    ns = len(srcs)

    def body(*refs):
        src_refs, land_refs = refs[:ns], refs[ns:2 * ns]
        send_sems, recv_sems = refs[2 * ns:2 * ns + 2]
        me = _my_position()
        for k in bits:
            for a in range(ns):
                s, d, got = views(me, k, a, src_refs, land_refs, axes)
                cp = pltpu.make_async_remote_copy(
                    src_ref=s, dst_ref=got, send_sem=send_sems.at[7 * a + k - 1], recv_sem=recv_sems.at[7 * a + k - 1],
                    device_id=_flip(me, k), device_id_type=MESH)
                cp.wait_send()
                cp.wait_recv()

    thru = [pltpu.HBM(a.shape, a.dtype) for a in list(srcs) + list(lands)]
    outs = pl.pallas_call(
        body, name=name,
        in_specs=[HBM_SPEC] * (2 * ns) + [SEM_SPEC, SEM_SPEC] + [ANY] * len(after),
        out_specs=[HBM_SPEC] * (2 * ns),
        out_shape=thru,
        input_output_aliases={i: i for i in range(2 * ns)},
        compiler_params=pltpu.CompilerParams(has_side_effects=DATAFLOW),
    )(*srcs, *lands, sems[0], sems[1], *after)
    return outs[:ns], outs[ns:]


def _small_allgather(v, name):
    r, c = v.shape

    def body(v_ref, all_ref, sum_ref, send_sems, recv_sems):
        me = _my_position()
        all_ref[_linear(me)] = v_ref[...]
        copies = []
        for k in range(1, N_DEV):
            peer = _flip(me, k)
            copies.append(pltpu.make_async_remote_copy(
                src_ref=v_ref, dst_ref=all_ref.at[_linear(me)], send_sem=send_sems.at[k - 1], recv_sem=recv_sems.at[k - 1],
                device_id=peer, device_id_type=MESH))
        for cp in copies:
            cp.start()
        for k in range(1, N_DEV):
            peer = _flip(me, k)
            pltpu.make_async_remote_copy(
                src_ref=v_ref, dst_ref=all_ref.at[_linear(peer)], send_sem=send_sems.at[k - 1], recv_sem=recv_sems.at[k - 1],
                device_id=peer, device_id_type=MESH).wait_recv()
        for cp in copies:
            cp.wait_send()
        acc = all_ref[0]
        for d in range(1, N_DEV):
            acc = acc + all_ref[d]
        sum_ref[...] = acc

    return pl.pallas_call(
        body,
        name=name,
        in_specs=[VMEM_SPEC],
        out_specs=[VMEM_SPEC, VMEM_SPEC],
        out_shape=[jax.ShapeDtypeStruct((N_DEV, r, c), F32), jax.ShapeDtypeStruct((r, c), F32)],
        scratch_shapes=[pltpu.SemaphoreType.DMA((N_DEV - 1,)), pltpu.SemaphoreType.DMA((N_DEV - 1,))],
        compiler_params=pltpu.CompilerParams(has_side_effects=True, vmem_limit_bytes=VMEM_LIMIT),
    )(v)


def _ada_fwd_call(cin, ada_w, ada_b_cols):
    nl, d, ncol = ada_w.shape
    nrow = cin.shape[0]

    def body(c_ref, w_ref, b_ref, o_ref):
        cs = _silu(c_ref[...]).astype(BF16)
        for l in range(nl):
            o_ref[l] = _dot(cs, w_ref[l].astype(BF16)) + b_ref[l]

    return pl.pallas_call(
        body, name="ada_fwd", in_specs=[VMEM_SPEC] * 3, out_specs=VMEM_SPEC,
        out_shape=jax.ShapeDtypeStruct((nl, nrow, ncol), F32), compiler_params=_params(),
    )(cin, ada_w, ada_b_cols)


def _ada_bwd_call(cin, ada_w, dmod):
    nl, d, ncol = ada_w.shape
    nrow = cin.shape[0]

    def body(c_ref, w_ref, dm_ref, gw_ref, dcs_ref):
        cs = _silu(c_ref[...]).astype(BF16)
        acc = jnp.zeros((nrow, d), F32)
        for l in range(nl):
            dm = dm_ref[l].astype(BF16)
            gw_ref[l] = _dot_tn(cs, dm)
            acc = acc + _dot_nt(dm, w_ref[l].astype(BF16))
        dcs_ref[...] = acc

    return pl.pallas_call(
        body, name="ada_bwd", in_specs=[VMEM_SPEC] * 3, out_specs=[VMEM_SPEC, VMEM_SPEC],
        out_shape=[jax.ShapeDtypeStruct((nl, d, ncol), F32), jax.ShapeDtypeStruct((nrow, d), F32)],
        compiler_params=_params(),
    )(cin, ada_w, dmod)


def _adamw_math(w, g, m, v):
    m = ADAM_B1 * m + (1.0 - ADAM_B1) * g
    v = ADAM_B2 * v + (1.0 - ADAM_B2) * jnp.square(g)
    m_hat = m / (1.0 - ADAM_B1 ** ADAM_STEP)
    v_hat = v / (1.0 - ADAM_B2 ** ADAM_STEP)
    delta = -ADAM_LR * (m_hat / (jnp.sqrt(v_hat) + ADAM_EPS) + ADAM_WD * w)
    return delta, m, v


def _adamw_sharded(w, m, v, mine, slabs, axis, my_idx, layer, prev, name):
    nl, r, c = w.shape
    tm = _pick(r, (128, 64, 32, 16))
    nprev = 0 if prev is None else len(prev)
    nslab = slabs.shape[0]
    if axis is None:
        mine_spec = pl.BlockSpec((1, tm, c), lambda i, idx: (0, i, 0))
    else:
        mine_spec = _slab_block(r, c, tm, axis)

    def body(idx_ref, w_ref, m_ref, v_ref, mine_ref, s_ref, *rest):
        g_ref, d_ref, nm_ref, nv_ref = rest[nprev:]
        g = (mine_ref[0] if axis is None else mine_ref[...]).astype(F32)
        for k in range(nslab):
            g = g + s_ref[k].astype(F32)
        delta, nm, nv = _adamw_math(w_ref[0], g, m_ref[0], v_ref[0])
        g_ref[0], d_ref[0], nm_ref[0], nv_ref[0] = g, delta, nm, nv

    spec = pl.BlockSpec((1, tm, c), lambda i, idx: (layer, i, 0))
    out = jax.ShapeDtypeStruct(w.shape, F32)
    return pl.pallas_call(
        body, name=name,
        grid_spec=pltpu.PrefetchScalarGridSpec(
            num_scalar_prefetch=1, grid=(r // tm,),
            in_specs=[spec, spec, spec, mine_spec,
                      pl.BlockSpec((nslab, tm, c), lambda i, idx: (0, i, 0))] + [ANY] * nprev,
            out_specs=[spec] * 4),
        out_shape=[out] * 4,
        input_output_aliases={6 + j: j for j in range(nprev)},
        compiler_params=_params(("parallel",)),
    )(my_idx, w, m, v, mine, slabs, *(() if prev is None else prev))


def _adamw_dense(w, g, m, v, name):
    r, c = w.shape
    tm = _pick(r, (256, 128, 64, 32, 16, 8))

    def body(w_ref, g_ref, m_ref, v_ref, d_ref, nm_ref, nv_ref):
        d_ref[...], nm_ref[...], nv_ref[...] = _adamw_math(w_ref[...], g_ref[...], m_ref[...], v_ref[...])

    spec = pl.BlockSpec((tm, c), lambda i: (i, 0))
    out = jax.ShapeDtypeStruct(w.shape, F32)
    return pl.pallas_call(
        body, name=name, grid=(r // tm,), in_specs=[spec] * 4, out_specs=[spec] * 3, out_shape=[out] * 3,
        compiler_params=_params(("parallel",)),
    )(w, g, m, v)


def _pack(parts, width=128):
    flat = jnp.concatenate([p.reshape(-1).astype(F32) for p in parts])
    n = flat.shape[0]
    total = -(-n // (8 * width)) * (8 * width)
    return jnp.pad(flat, (0, total - n)).reshape(total // width, width)


def _unpack(buf, shapes):
    flat = buf.reshape(-1)
    out, off = [], 0
    for s in shapes:
        n = int(np.prod(s))
        out.append(flat[off:off + n].reshape(s))
        off += n
    return out


def kernel(x, c, ctx, c_ctx, ada_w, ada_b, norm_g, w_in, na_rpb, ret_decay_logit, w_proj_na, w_proj_ret, w_out, final_g, loss_target, m_c_ctx, m_ada_w, m_ada_b, m_norm_g, m_w_in, m_na_rpb, m_ret_decay_logit, m_w_proj_na, m_w_proj_ret, m_w_out, m_final_g, v_c_ctx, v_ada_w, v_ada_b, v_norm_g, v_w_in, v_na_rpb, v_ret_decay_logit, v_w_proj_na, v_w_proj_ret, v_w_out, v_final_g):
    depth = w_in.shape[0]
    n_lat, d = x.shape[1], x.shape[2]
    n_ctx = ctx.shape[1]
    t = n_lat + n_ctx
    w_na = w_proj_na.shape[1]
    w_retv = w_proj_ret.shape[1] * N_DEV
    in_cols = w_in.shape[2] * N_DEV
    w_qk = (in_cols - 4 * w_na - 2 * w_retv - 2 * d) // 2
    sizes = (w_na, w_na, w_na, w_na, w_qk, w_qk, w_retv, w_retv, d, d)
    off = tuple(int(o) for o in np.cumsum((0,) + sizes))
    NA_Q, NA_K, NA_V, NA_Z, RET_Q, RET_K, RET_V, RET_Z, G_NA, G_RET = range(10)
    rows = n_lat // GRID_W
    me = _my_position()
    my_idx = _linear(me)
    tm_row = _pick(n_ctx, (256, 128))

    idx_arr = jnp.reshape(my_idx, (1,)).astype(jnp.int32)
    chip_idx = jnp.stack([_linear(_flip(me, kc)) for kc in CHIP_BITS]).astype(jnp.int32)

    w_axes = (1, 1, 0, 0)
    w_names = ("w_in", "w_proj_na", "w_proj_ret", "w_out")
    shard = [[w[l].astype(BF16) for w in (w_in, w_proj_na, w_proj_ret, w_out)] for l in range(depth)]
    groups = [[(0, 0)], [(0, 1), (0, 2), (0, 3)]] + [[(l, a) for a in range(4)] for l in range(1, depth)]
    gathers = {}

    def start_gather(gi, deps):
        keys = groups[gi]
        srcs = [shard[l][a] for l, a in keys]
        axes = tuple(w_axes[a] for _, a in keys)
        lands = [_place_shard(s, lax.empty(tuple(n * (N_DEV if i == ax else 1) for i, n in enumerate(s.shape)), BF16),
                              ax, idx_arr) for s, ax in zip(srcs, axes)]
        sizes = tuple(s.shape[ax] for s, ax in zip(srcs, axes))
        sems, srcs, lands, tok = _push_start(f"gather_start_{gi}", srcs, lands, axes, _gather_views, GATHER_BITS, deps)
        flight = dict(name=f"gather_wait_{gi}", sems=sems, srcs=srcs, lands=lands, axes=axes, sizes=sizes, ready=None)
        for pos, key in enumerate(keys):
            gathers[key] = (flight, pos)
        return tok

    token = start_gather(0, ())

    ncol = ada_w.shape[2]
    c_all, _ = _small_allgather(jnp.pad(c, ((0, 7), (0, 0))) + token[:, :1], "allgather_c")
    cin = jnp.concatenate([c_all[:, 0, :], c_ctx[None, :], jnp.zeros((7, d), F32)], axis=0)
    ada_b_cols = lax.dynamic_slice_in_dim(ada_b, my_idx * ncol, ncol, axis=1)[:, None, :]
    mod_cols = _ada_fwd_call(cin, ada_w, ada_b_cols)
    mod_gathered, _ = _small_allgather(mod_cols.reshape(depth * 16, ncol), "allgather_mod")
    mod_all = mod_gathered.reshape(N_DEV, depth, 16, ncol).transpose(1, 2, 0, 3).reshape(depth, 16, N_DEV * ncol)
    mod_lat = lax.dynamic_index_in_dim(mod_all, my_idx, axis=1, keepdims=False)
    mod_ctx = mod_all[:, 8, :]
    token = mod_gathered
    for gi in range(1, len(groups)):
        token = start_gather(gi, (token,))

    def landed(l, a, act):
        flight, pos = gathers[(l, a)]
        if flight["ready"] is None:
            arrived = _push_wait(flight["name"], flight["sems"], flight["srcs"], flight["lands"],
                                 flight["axes"], _gather_views, GATHER_BITS, (act, token))[1]
            flight["ready"] = _gather_finish(arrived, flight["axes"], flight["sizes"])
        return flight["ready"][pos]

    pending, scatters = {}, []

    def send_dw(l, a, dw):
        pending[(l, a)] = dw
        if a == 0:
            keys = [(0, 0)] if l == 0 else [(l, b) for b in range(4)]
        elif l == 0 and a == 1:
            keys = [(0, 1), (0, 2), (0, 3)]
        else:
            return None
        srcs = [pending[k] for k in keys]
        axes = tuple(w_axes[b] for _, b in keys)
        sizes = tuple(s.shape[ax] // N_DEV for s, ax in zip(srcs, axes))
        slab_shapes = [tuple(n // (N_DEV if i == ax else 1) for i, n in enumerate(s.shape)) for s, ax in zip(srcs, axes)]
        by_chip = keys == [(0, 0)]
        if by_chip:
            theirs = _pair_exchange(srcs, axes, sizes)
            srcs = [_pair_add(g, p, ax, chip_idx) for g, p, ax in zip(srcs, theirs, axes)]
            lands = [lax.empty((3,) + shp, BF16) for shp in slab_shapes]
            views, bits = _chip_views, CHIP_BITS[1:]
        else:
            lands = [lax.empty((N_DEV - 1,) + shp, BF16) for shp in slab_shapes]
            views, bits = _scatter_views, PEER_BITS
        sems, srcs, lands, tok = _push_start(f"scatter_start_{len(scatters)}", srcs, lands, axes, views, bits, ())
        scatters.append(dict(name=f"scatter_wait_{len(scatters)}", sems=sems, srcs=srcs, lands=lands, axes=axes, keys=keys,
                             views=views, bits=bits, by_chip=by_chip))
        return tok

    cos2, sin2 = _rope_tables(t, n_lat)
    k_scale = RET_KEY_DIM ** -0.5
    assert off[RET_Q] % w_qk == 0 and off[RET_K] % w_qk == 0
    assert off[NA_Z] % w_na == 0 and off[G_NA] % d == 0 and off[G_RET] % d == 0 and off[RET_Z] % w_retv == 0
    assert off[RET_V] % (_ret_head_group(w_retv // RET_VAL_DIM) * RET_VAL_DIM) == 0
    na_cols = tuple(off[i] // NA_HEAD_DIM for i in (NA_Q, NA_K, NA_V))
    norm_mod_fwd, norm_mod_bwd = _make_rowwise(_f_norm_mod, "norm_mod", (BF16,), (d,), n_lat, tm_row, (0,))
    gate_na_fwd, gate_na_bwd = _make_rowwise(_f_gate_na, "gate_na", (BF16,), (w_na,), n_lat, tm_row, (0, 1),
                                             col_blocks={1: (w_na, off[NA_Z] // w_na)})
    merge_fwd, merge_bwd = _make_rowwise(_f_merge, "merge", (BF16,), (d,), n_lat, tm_row, (0, 1, 2, 3),
                                         col_blocks={0: (d, off[G_NA] // d), 1: (d, off[G_RET] // d)},
                                         drow_dtypes={2: BF16, 3: BF16})
    residual_fwd, _ = _make_rowwise(_f_residual, "residual", (F32,), (d,), n_lat, tm_row, (0, 1))
    _, residual_bwd = _make_rowwise(lambda out, gate: (gate * out,), "residual", (F32,), (d,), n_lat, tm_row, (0,),
                                    drow_dtypes={0: BF16})
    loss_fwd, loss_bwd = _make_rowwise(_f_loss, "loss_head", (F32,), (128,), n_lat, tm_row, (0,))

    def pair(a, b):
        return jnp.stack([a, b])[:, None, :]

    def mod_vectors(mod_lat_l, mod_ctx_l, norm_g_l):
        shift, scale, gate = jnp.split(mod_lat_l, 3)
        c_shift, c_scale, c_gate = jnp.split(mod_ctx_l, 3)
        return pair(norm_g_l, norm_g_l), pair(scale, c_scale), pair(shift, c_shift), pair(gate, c_gate)

    def log_decay(logit):
        return jax.nn.log_sigmoid(logit.astype(F32))

    xa = jnp.concatenate([x[0], ctx[0]], axis=0)
    saved = []
    for l in range(depth):
        vecs, vecs_vjp = jax.vjp(mod_vectors, mod_lat[l], mod_ctx[l], norm_g[l])
        (h,) = norm_mod_fwd((xa,), vecs[:3])
        wl_in = landed(l, 0, h)
        u = _matmul(h, wl_in, out_dtype=BF16, name="in_proj_fwd")
        qr, kr = _rope_fwd_call(u, off[RET_Q] // w_qk, off[RET_K] // w_qk, w_qk, cos2, sin2, k_scale, tm_row)
        bt, bt_vjp = jax.vjp(lambda r: _na_bias_table(r, rows), na_rpb[l])
        lam, lam_vjp = jax.vjp(log_decay, ret_decay_logit[l])
        o_na = _na_fwd_call(u, u, u, na_cols, w_na, bt, n_lat)
        o_f, st_f = _ret_fwd_call(qr, kr, u, off[RET_V], lam[0], n_lat, False)
        o_b, st_b = _ret_fwd_call(qr, kr, u, off[RET_V], lam[1], n_lat, True)
        (a_na,) = gate_na_fwd((o_na, u), ())
        a_ret = _gate_ret_fwd_call(o_f, o_b, u, off[RET_Z] // w_retv, tm_row)
        wl_pna, wl_pret, wl_out = landed(l, 1, a_na), landed(l, 2, a_na), landed(l, 3, a_na)
        y_na = _matmul(a_na, wl_pna, out_dtype=BF16, name="proj_na_fwd")
        y_ret = _matmul(a_ret, wl_pret, out_dtype=BF16, name="proj_ret_fwd")
        (merged,) = merge_fwd((u, u, y_na, y_ret), ())
        out = _matmul(merged, wl_out, out_dtype=F32, name="out_proj_fwd")
        (xa_next,) = residual_fwd((xa, out), vecs[3:])
        saved.append(dict(xa=xa, vecs=vecs, vecs_vjp=vecs_vjp, h=h, w=(wl_in, wl_pna, wl_pret, wl_out), u=u, qr=qr, kr=kr,
                          bt=bt, bt_vjp=bt_vjp, lam=lam, lam_vjp=lam_vjp, o_na=o_na, o_f=o_f,
                          o_b=o_b, st_f=st_f, st_b=st_b, a_na=a_na, a_ret=a_ret, y_na=y_na, y_ret=y_ret,
                          merged=merged, out=out))
        xa = xa_next

    fg_pair, fg_vjp = jax.vjp(lambda g: pair(g, g), final_g)
    (loss_rows,) = loss_fwd((xa, loss_target[0]), (fg_pair,))
    loss = lax.psum(jnp.sum(loss_rows), ("x", "y", "c"))
    (dx_last,), (d_fg_pair,) = loss_bwd((xa, loss_target[0]), (fg_pair,), (jnp.ones_like(loss_rows),))
    (d_final_g,) = fg_vjp(d_fg_pair)
    dxa = jnp.pad(dx_last, ((0, n_ctx), (0, 0)))

    d_mod_lat, d_mod_ctx, d_norm_g, d_rpb, d_decay = ([None] * depth for _ in range(5))
    for l in reversed(range(depth)):
        s = saved[l]
        u, qr, kr = s["u"], s["qr"], s["kr"]
        wl_in, wl_pna, wl_pret, wl_out = s["w"]
        (d_out,), (d_gate,) = residual_bwd((s["out"],), s["vecs"][3:], (dxa,))
        dxa_res = dxa
        send_dw(l, 3, _matmul(s["merged"], d_out, trans_a=True, out_dtype=BF16, name="out_proj_dw"))
        d_merged = _matmul(d_out, wl_out, trans_b=True, out_dtype=BF16, name="out_proj_da")
        (dg_na, dg_ret, dy_na, dy_ret), _ = merge_bwd((u, u, s["y_na"], s["y_ret"]), (), (d_merged,))
        send_dw(l, 2, _matmul(s["a_ret"], dy_ret, trans_a=True, out_dtype=BF16, name="proj_ret_dw"))
        da_ret = _matmul(dy_ret, wl_pret, trans_b=True, out_dtype=BF16, name="proj_ret_da")
        tok = send_dw(l, 1, _matmul(s["a_na"], dy_na, trans_a=True, out_dtype=BF16, name="proj_na_dw"))
        da_na = _matmul(dy_na, wl_pna, trans_b=True, out_dtype=BF16, name="proj_na_da", after=tok)
        do_ret, dz_ret = _gate_ret_bwd_call(s["o_f"], s["o_b"], u, off[RET_Z] // w_retv, da_ret, tm_row)
        (do_na, dz_na), _ = gate_na_bwd((s["o_na"], u), (), (da_na,))
        dq_f, dk_f, dv_f, dl_f = _ret_bwd_call(qr, kr, u, off[RET_V], s["lam"][0], s["st_f"], do_ret, n_lat, False)
        dq_b, dk_b, dv_b, dl_b = _ret_bwd_call(qr, kr, u, off[RET_V], s["lam"][1], s["st_b"], do_ret, n_lat, True)
        dq, dk, dv, dbt = _na_bwd_call(u, u, u, na_cols, w_na, s["bt"], do_na, n_lat)
        du = _assemble_du_call([dq, dk, dv, dz_na, (dq_f, dq_b), (dk_f, dk_b), (dv_f, dv_b), dz_ret, dg_na, dg_ret],
                               off, cos2, sin2, k_scale, _pick(n_ctx, (128,)))
        (d_rpb[l],) = s["bt_vjp"](dbt)
        (d_decay[l],) = s["lam_vjp"](jnp.stack([dl_f[:, 0, 0], dl_b[:, 0, 0]]))
        tok = send_dw(l, 0, _matmul(s["h"], du, trans_a=True, out_dtype=BF16, name="in_proj_dw"))
        dh = _matmul(du, wl_in, trans_b=True, out_dtype=BF16, name="in_proj_da", after=tok)
        (dxa,), d_vecs = norm_mod_bwd((s["xa"],), s["vecs"][:3], (dh,), acc=(dxa_res,))
        d_mod_lat[l], d_mod_ctx[l], d_norm_g[l] = s["vecs_vjp"](tuple(d_vecs) + (d_gate,))
    gx = dxa[:n_lat]
    d_mod_lat, d_mod_ctx, d_norm_g, d_rpb, d_decay = (jnp.stack(a) for a in (d_mod_lat, d_mod_ctx, d_norm_g, d_rpb, d_decay))

    small_shapes = [d_mod_lat.shape, d_mod_ctx.shape, d_norm_g.shape, d_final_g.shape, d_rpb.shape, d_decay.shape]
    packed = _pack([d_mod_lat, d_mod_ctx, d_norm_g, d_final_g, d_rpb, d_decay])
    g_all, g_sum = _small_allgather(packed, "allgather_small_grads")
    dml_sum, dmc_sum, grad_norm_g, grad_final_g, grad_na_rpb, grad_decay = _unpack(g_sum, small_shapes)
    grad_ada_b = dml_sum + dmc_sum
    dml_all = g_all.reshape(N_DEV, -1)[:, :depth * 3 * d].reshape(N_DEV, depth, 3 * d)

    def my_cols(a):
        return lax.dynamic_slice_in_dim(a, my_idx * ncol, ncol, axis=a.ndim - 1)

    dmod = jnp.concatenate(
        [my_cols(dml_all).transpose(1, 0, 2), my_cols(dmc_sum)[:, None, :], jnp.zeros((depth, 7, ncol), F32)], axis=1)
    grad_ada_w, dcs_part = _ada_bwd_call(cin, ada_w, dmod)
    _, dcs = _small_allgather(dcs_part, "allgather_dcsilu")
    sg = jax.nn.sigmoid(c_ctx)
    grad_c_ctx = dcs[8] * (sg * (1.0 + c_ctx * (1.0 - sg)))

    def flat2(a):
        return a.reshape(a.shape[0] * a.shape[1], a.shape[2])

    small_w = [c_ctx, ada_b, norm_g, na_rpb, ret_decay_logit, final_g]
    small_g = [grad_c_ctx, grad_ada_b, grad_norm_g, grad_na_rpb, grad_decay, grad_final_g]
    small_m = [m_c_ctx, m_ada_b, m_norm_g, m_na_rpb, m_ret_decay_logit, m_final_g]
    small_v = [v_c_ctx, v_ada_b, v_norm_g, v_na_rpb, v_ret_decay_logit, v_final_g]
    shp = [a.shape for a in small_w]
    ds_, nms_, nvs_ = _adamw_dense(_pack(small_w), _pack(small_g), _pack(small_m), _pack(small_v), "adamw_small")
    ds_, nms_, nvs_ = _unpack(ds_, shp), _unpack(nms_, shp), _unpack(nvs_, shp)

    d_ada, nm_ada, nv_ada = [a.reshape(ada_w.shape) for a in _adamw_dense(
        flat2(ada_w), flat2(grad_ada_w), flat2(m_ada_w), flat2(v_ada_w), "adamw_ada_w")]

    w_all = (w_in, w_proj_na, w_proj_ret, w_out)
    m_all = (m_w_in, m_w_proj_na, m_w_proj_ret, m_w_out)
    v_all = (v_w_in, v_w_proj_na, v_w_proj_ret, v_w_out)
    upd = [None] * 4
    after = d_ada
    for flight in scatters:
        mine, slabs = _push_wait(flight["name"], flight["sems"], flight["srcs"], flight["lands"], flight["axes"],
                                 flight["views"], flight["bits"], (after,))
        for (l, a), own, s in zip(flight["keys"], mine, slabs):
            upd[a] = _adamw_sharded(w_all[a], m_all[a], v_all[a], own, s, None if flight["by_chip"] else w_axes[a],
                                    idx_arr, l, upd[a], "adamw_" + w_names[a])
            after = upd[a][1]
    (g_w_in, d_w_in, nm_w_in, nv_w_in), (g_pna, d_pna, nm_pna, nv_pna) = upd[0], upd[1]
    (g_pret, d_pret, nm_pret, nv_pret), (g_out, d_out, nm_out, nv_out) = upd[2], upd[3]

    def order(cc, aw, ab, ng, wi, rp, dl, pn, pr, wo, fg):
        return [cc, aw, ab, ng, wi, rp, dl, pn, pr, wo, fg]

    grads_out = order(grad_c_ctx, grad_ada_w, grad_ada_b, grad_norm_g, g_w_in, grad_na_rpb, grad_decay, g_pna, g_pret, g_out, grad_final_g)
    delta_out = order(ds_[0], d_ada, ds_[1], ds_[2], d_w_in, ds_[3], ds_[4], d_pna, d_pret, d_out, ds_[5])
    m_out = order(nms_[0], nm_ada, nms_[1], nms_[2], nm_w_in, nms_[3], nms_[4], nm_pna, nm_pret, nm_out, nms_[5])
    v_out = order(nvs_[0], nv_ada, nvs_[1], nvs_[2], nv_w_in, nvs_[3], nvs_[4], nv_pna, nv_pret, nv_out, nvs_[5])
    return (loss, gx[None], *grads_out, *delta_out, *m_out, *v_out)
```

```python
import numpy as np
import jax
import jax.numpy as jnp
from jax import lax
from jax.experimental import pallas as pl
from jax.experimental.pallas import tpu as pltpu

F32 = jnp.float32
BF16 = jnp.bfloat16

N_DEV = 8
GRID_W = 64
NA_HEAD_DIM = 128
NA_WIN_ROWS = 8
NA_WIN_COLS = 16
RET_KEY_DIM = 128
RET_VAL_DIM = 256
RET_CHUNK = 128
ROPE_BASE = 10000.0
NORM_EPS = 1e-6
MASK_VALUE = -1e30

ADAM_LR = 0.001
ADAM_B1 = 0.9
ADAM_B2 = 0.999
ADAM_EPS = 1e-08
ADAM_WD = 0.01
ADAM_STEP = 10

VMEM_LIMIT = 48 * 1024 * 1024
MESH = pl.DeviceIdType.MESH
ANY = pl.BlockSpec(memory_space=pl.ANY)
VMEM_SPEC = pl.BlockSpec(memory_space=pltpu.VMEM)


def _params(sem=None):
    return pltpu.CompilerParams(dimension_semantics=sem, vmem_limit_bytes=VMEM_LIMIT)


def _pick(n, prefs):
    for p in prefs:
        if n % p == 0:
            return p
    return n


def _dot(a, b):
    return lax.dot_general(a, b, (((1,), (0,)), ((), ())), preferred_element_type=F32)


def _dot_nt(a, b):
    return lax.dot_general(a, b, (((1,), (1,)), ((), ())), preferred_element_type=F32)


def _dot_tn(a, b):
    return lax.dot_general(a, b, (((0,), (0,)), ((), ())), preferred_element_type=F32)


def _silu(x):
    return x * jax.nn.sigmoid(x)


def _matmul(a, b, *, trans_a=False, trans_b=False, out_dtype=F32, name="matmul", after=None):
    if trans_a:
        kdim, m = a.shape
    else:
        m, kdim = a.shape
    if trans_b:
        n, kb = b.shape
    else:
        kb, n = b.shape
    assert kdim == kb, (a.shape, b.shape, trans_a, trans_b)
    tm = _pick(m, (1152, 1024, 768, 512, 256, 128))
    tn = _pick(n, (1024, 512, 256, 128))
    tk = _pick(kdim, (2304, 2048, 1024, 512, 256, 128))
    nk = kdim // tk
    dn = (((0 if trans_a else 1,), (1 if trans_b else 0,)), ((), ()))

    def body(a_ref, b_ref, *rest):
        o_ref, acc_ref = rest[-2:]
        part = lax.dot_general(a_ref[...], b_ref[...], dn, preferred_element_type=F32)
        if nk == 1:
            o_ref[...] = part.astype(o_ref.dtype)
        else:
            k = pl.program_id(2)

            @pl.when(k == 0)
            def _():
                acc_ref[...] = part

            @pl.when(k > 0)
            def _():
                acc_ref[...] += part

            @pl.when(k == nk - 1)
            def _():
                o_ref[...] = acc_ref[...].astype(o_ref.dtype)

    a_spec = pl.BlockSpec((tk, tm), lambda i, j, k: (k, i)) if trans_a else pl.BlockSpec((tm, tk), lambda i, j, k: (i, k))
    b_spec = pl.BlockSpec((tn, tk), lambda i, j, k: (j, k)) if trans_b else pl.BlockSpec((tk, tn), lambda i, j, k: (k, j))
    return pl.pallas_call(
        body,
        name=name,
        grid=(m // tm, n // tn, nk),
        in_specs=[a_spec, b_spec] + ([] if after is None else [ANY]),
        out_specs=pl.BlockSpec((tm, tn), lambda i, j, k: (i, j)),
        out_shape=jax.ShapeDtypeStruct((m, n), out_dtype),
        scratch_shapes=[pltpu.VMEM((tm, tn) if nk > 1 else (8, 128), F32)],
        compiler_params=_params(("parallel", "parallel", "arbitrary")),
    )(*((a, b) if after is None else (a, b, after)))


def _make_rowwise(f, name, out_dtypes, out_cols, n_lat, tm, diff_rows, col_blocks=None, drow_dtypes=None):
    drow_dtypes = drow_dtypes or {}

    def tile_fn(*args):
        return tuple(o.astype(dt) for o, dt in zip(f(*args), out_dtypes))

    def row_spec(k, arr):
        width, index = (col_blocks or {}).get(k, (arr.shape[1], 0))
        return pl.BlockSpec((tm, width), lambda i: (i, index))

    def row_width(k, arr):
        return (col_blocks or {}).get(k, (arr.shape[1], 0))[0]

    def fwd_call(rows, vecs):
        t = min(r.shape[0] for r in rows)
        nr, nv = len(rows), len(vecs)
        nl = n_lat // tm

        def body(*refs):
            grp = (pl.program_id(0) >= nl).astype(jnp.int32)
            args = [r[...] for r in refs[:nr]] + [v[grp] for v in refs[nr:nr + nv]]
            for o_ref, o in zip(refs[nr + nv:], tile_fn(*args)):
                o_ref[...] = o

        return pl.pallas_call(
            body,
            name=name + "_fwd",
            grid=(t // tm,),
            in_specs=[row_spec(k, r) for k, r in enumerate(rows)]
            + [pl.BlockSpec(v.shape, lambda i: (0, 0, 0)) for v in vecs],
            out_specs=[pl.BlockSpec((tm, c), lambda i: (i, 0)) for c in out_cols],
            out_shape=[jax.ShapeDtypeStruct((t, c), dt) for c, dt in zip(out_cols, out_dtypes)],
            compiler_params=_params(("parallel",)),
        )(*rows, *vecs)

    def bwd_call(rows, vecs, gs, acc=None):
        t = min(r.shape[0] for r in rows)
        nr, nv, ng = len(rows), len(vecs), len(gs)
        nl = n_lat // tm
        nd = len(diff_rows)
        acc = [None] * nd if acc is None else list(acc)
        acc_in = [a for a in acc if a is not None]

        def body(*refs):
            i = pl.program_id(0)
            grp = (i >= nl).astype(jnp.int32)
            args = [r[...] for r in refs[:nr]] + [v[grp] for v in refs[nr:nr + nv]]
            g_refs = refs[nr + nv:nr + nv + ng]
            acc_refs = list(refs[nr + nv + ng:nr + nv + ng + len(acc_in)])
            drow_refs = refs[nr + nv + ng + len(acc_in):nr + nv + ng + len(acc_in) + nd]
            dvec_refs = refs[nr + nv + ng + len(acc_in) + nd:]
            _, vjp = jax.vjp(tile_fn, *args)
            grads = vjp(tuple(g[...] for g in g_refs))
            for d_ref, k, a in zip(drow_refs, diff_rows, acc):
                gk = grads[k] if a is None else grads[k] + acc_refs.pop(0)[...]
                d_ref[...] = gk.astype(d_ref.dtype)

            @pl.when(i == 0)
            def _():
                for d_ref in dvec_refs:
                    d_ref[...] = jnp.zeros_like(d_ref)

            for j, d_ref in enumerate(dvec_refs):
                d_ref[grp] += grads[nr + j]

        outs = pl.pallas_call(
            body,
            name=name + "_bwd",
            grid=(t // tm,),
            in_specs=[row_spec(k, r) for k, r in enumerate(rows)]
            + [pl.BlockSpec(v.shape, lambda i: (0, 0, 0)) for v in vecs]
            + [pl.BlockSpec((tm, g.shape[1]), lambda i: (i, 0)) for g in gs]
            + [pl.BlockSpec((tm, a.shape[1]), lambda i: (i, 0)) for a in acc_in],
            out_specs=[pl.BlockSpec((tm, row_width(k, rows[k])), lambda i: (i, 0)) for k in diff_rows]
            + [pl.BlockSpec(v.shape, lambda i: (0, 0, 0)) for v in vecs],
            out_shape=[jax.ShapeDtypeStruct((t, row_width(k, rows[k])), drow_dtypes.get(k, rows[k].dtype))
                       for k in diff_rows]
            + [jax.ShapeDtypeStruct(v.shape, F32) for v in vecs],
            compiler_params=_params(("arbitrary",)),
        )(*rows, *vecs, *gs, *acc_in)
        return outs[:nd], outs[nd:]

    return fwd_call, bwd_call


def _f_norm_mod(x, g, scale, shift):
    r = lax.rsqrt(jnp.mean(x * x, axis=-1, keepdims=True) + NORM_EPS)
    return ((x * r * g) * (1.0 + scale) + shift,)


def _f_gate_na(o, z):
    return (o.astype(F32) * _silu(z.astype(F32)),)


def _f_merge(g_na, g_ret, y_na, y_ret):
    return (jax.nn.sigmoid(g_na.astype(F32)) * y_na.astype(F32) + jax.nn.sigmoid(g_ret.astype(F32)) * y_ret.astype(F32),)


def _f_residual(x, out, gate):
    return (x + gate * out,)


def _f_residual_norm(x, out, gate, g, scale, shift):
    xn = x + gate * out
    return (xn,) + _f_norm_mod(xn, g, scale, shift)


def _f_loss(x, target, g):
    r = lax.rsqrt(jnp.mean(x * x, axis=-1, keepdims=True) + NORM_EPS)
    y = x * r * g
    e = 0.5 * jnp.mean(jnp.square(y - target), axis=-1, keepdims=True)
    return (jnp.broadcast_to(e * (1.0 / 128.0), (x.shape[0], 128)),)


def _gate_ret_fwd_call(of, ob, z, zblk, tm):
    t, w = of.shape
    nh = w // RET_VAL_DIM

    def body(of_ref, ob_ref, z_ref, a_ref):
        for hh in range(nh):
            sl = slice(hh * RET_VAL_DIM, (hh + 1) * RET_VAL_DIM)
            o = of_ref[:, sl].astype(F32) + ob_ref[:, sl].astype(F32)
            r = lax.rsqrt(jnp.mean(o * o, axis=-1, keepdims=True) + NORM_EPS)
            a_ref[:, sl] = ((o * r) * _silu(z_ref[:, sl].astype(F32))).astype(a_ref.dtype)

    spec = pl.BlockSpec((tm, w), lambda i: (i, 0))
    zspec = pl.BlockSpec((tm, w), lambda i: (i, zblk))
    return pl.pallas_call(
        body, name="gate_ret_fwd", grid=(t // tm,), in_specs=[spec, spec, zspec], out_specs=spec,
        out_shape=jax.ShapeDtypeStruct((t, w), BF16), compiler_params=_params(("parallel",)),
    )(of, ob, z)


def _gate_ret_bwd_call(of, ob, z, zblk, da, tm):
    t, w = of.shape
    nh = w // RET_VAL_DIM

    def body(of_ref, ob_ref, z_ref, da_ref, do_ref, dz_ref):
        for hh in range(nh):
            sl = slice(hh * RET_VAL_DIM, (hh + 1) * RET_VAL_DIM)
            o = of_ref[:, sl].astype(F32) + ob_ref[:, sl].astype(F32)
            r = lax.rsqrt(jnp.mean(o * o, axis=-1, keepdims=True) + NORM_EPS)
            n = o * r
            zf = z_ref[:, sl].astype(F32)
            sg = jax.nn.sigmoid(zf)
            g = da_ref[:, sl].astype(F32)
            dn = g * (zf * sg)
            dz_ref[:, sl] = (g * n * (sg * (1.0 + zf * (1.0 - sg)))).astype(dz_ref.dtype)
            do_ref[:, sl] = (r * (dn - n * jnp.mean(dn * n, axis=-1, keepdims=True))).astype(do_ref.dtype)

    spec = pl.BlockSpec((tm, w), lambda i: (i, 0))
    zspec = pl.BlockSpec((tm, w), lambda i: (i, zblk))
    return pl.pallas_call(
        body, name="gate_ret_bwd", grid=(t // tm,), in_specs=[spec, spec, zspec, spec], out_specs=[spec, spec],
        out_shape=[jax.ShapeDtypeStruct((t, w), BF16), jax.ShapeDtypeStruct((t, w), z.dtype)],
        compiler_params=_params(("parallel",)),
    )(of, ob, z, da)


NA_PAIR = 2 * GRID_W
NA_KEY_ROWS = NA_WIN_ROWS + 2
NA_CLASSES = 5


def _na_geometry(t, n_lat):
    rows = n_lat // GRID_W
    assert rows % 2 == 0 and rows >= NA_KEY_ROWS + 2, rows
    return rows, rows // 2, NA_KEY_ROWS * GRID_W, t - n_lat, t // NA_PAIR


def _na_base(p, rows):
    return jnp.clip(2 * p - NA_WIN_ROWS // 2, 0, rows - NA_KEY_ROWS)


def _na_class(p, rows):
    return p - _na_base(p, rows) // 2


def _na_group(pairs, n_ctx):
    assert n_ctx % NA_PAIR == 0, n_ctx
    return 2 if pairs % 2 == 0 and (n_ctx // NA_PAIR) % 2 == 0 else 1


def _na_bias_spec():
    return pl.BlockSpec((1, NA_CLASSES, 2, NA_KEY_ROWS // 2, GRID_W, NA_PAIR), lambda h, g: (h, 0, 0, 0, 0, 0))


def _na_bias_tile(bt_ref, cls):
    return jnp.concatenate(
        [jnp.concatenate([bt_ref[0, cls, i, q] for q in range(NA_KEY_ROWS // 2)], axis=1) for i in range(2)], axis=0)


def _na_add_bias_grad(dbt_ref, cls, ds):
    for i in range(2):
        for q in range(NA_KEY_ROWS // 2):
            dbt_ref[0, cls, i, q] += ds[i * GRID_W:(i + 1) * GRID_W, q * NA_PAIR:(q + 1) * NA_PAIR]


def _na_fwd_call(q, k, v, col0, w, bt, n_lat):
    t = q.shape[0]
    nh = w // NA_HEAD_DIM
    rows, pairs, n_loc, n_ctx, nq = _na_geometry(t, n_lat)
    grp = _na_group(pairs, n_ctx)
    scale = NA_HEAD_DIM ** -0.5

    def body(q_ref, k_ref, v_ref, bt_ref, o_ref):
        g = pl.program_id(1)
        kc = k_ref[pl.ds(n_lat, n_ctx), :]
        vc = v_ref[pl.ds(n_lat, n_ctx), :]

        @pl.when(g < pairs // grp)
        def _():
            for i in range(grp):
                p = g * grp + i
                sl = slice(i * NA_PAIR, (i + 1) * NA_PAIR)
                qb = q_ref[sl, :]
                s_ctx = _dot_nt(qb, kc) * scale
                start = pl.multiple_of(_na_base(p, rows) * GRID_W, GRID_W)
                kw = k_ref[pl.ds(start, n_loc), :]
                vw = v_ref[pl.ds(start, n_loc), :]
                s_loc = _dot_nt(qb, kw) * scale + _na_bias_tile(bt_ref, _na_class(p, rows))
                m = jnp.maximum(jnp.max(s_loc, axis=-1, keepdims=True), jnp.max(s_ctx, axis=-1, keepdims=True))
                p_loc = jnp.exp(s_loc - m)
                p_ctx = jnp.exp(s_ctx - m)
                l = jnp.sum(p_loc, axis=-1, keepdims=True) + jnp.sum(p_ctx, axis=-1, keepdims=True)
                o = _dot(p_loc.astype(BF16), vw) + _dot(p_ctx.astype(BF16), vc)
                o_ref[sl, :] = (o / l).astype(o_ref.dtype)

        @pl.when(g >= pairs // grp)
        def _():
            s_ctx = _dot_nt(q_ref[...], kc) * scale
            m = jnp.max(s_ctx, axis=-1, keepdims=True)
            p = jnp.exp(s_ctx - m)
            l = jnp.sum(p, axis=-1, keepdims=True)
            o_ref[...] = (_dot(p.astype(BF16), vc) / l).astype(o_ref.dtype)

    qspec = pl.BlockSpec((grp * NA_PAIR, NA_HEAD_DIM), lambda h, g: (g, h))
    in_q = pl.BlockSpec((grp * NA_PAIR, NA_HEAD_DIM), lambda h, g: (g, col0[0] + h))
    in_k = pl.BlockSpec((t, NA_HEAD_DIM), lambda h, g: (0, col0[1] + h))
    in_v = pl.BlockSpec((t, NA_HEAD_DIM), lambda h, g: (0, col0[2] + h))
    return pl.pallas_call(
        body,
        name="na_attn_fwd",
        grid=(nh, nq // grp),
        in_specs=[in_q, in_k, in_v, _na_bias_spec()],
        out_specs=qspec,
        out_shape=jax.ShapeDtypeStruct((t, w), BF16),
        compiler_params=_params(("parallel", "arbitrary")),
    )(q, k, v, bt)


def _na_bwd_call(q, k, v, col0, w, bt, do, n_lat):
    t = q.shape[0]
    nh = w // NA_HEAD_DIM
    rows, pairs, n_loc, n_ctx, nq = _na_geometry(t, n_lat)
    scale = NA_HEAD_DIM ** -0.5
    grp = _na_group(pairs, n_ctx)

    def body(q_ref, k_ref, v_ref, do_ref, bt_ref, dq_ref, dk_ref, dv_ref, dbt_ref):
        g = pl.program_id(1)

        @pl.when(g == 0)
        def _():
            dk_ref[...] = jnp.zeros_like(dk_ref)
            dv_ref[...] = jnp.zeros_like(dv_ref)
            dbt_ref[...] = jnp.zeros_like(dbt_ref)

        kc = k_ref[pl.ds(n_lat, n_ctx), :]
        vc = v_ref[pl.ds(n_lat, n_ctx), :]

        @pl.when(g < pairs // grp)
        def _():
            for i in range(grp):
                p = g * grp + i
                sl = slice(i * NA_PAIR, (i + 1) * NA_PAIR)
                qb = q_ref[sl, :]
                dob = do_ref[sl, :]
                s_ctx = _dot_nt(qb, kc) * scale
                dp_ctx = _dot_nt(dob, vc)
                start = pl.multiple_of(_na_base(p, rows) * GRID_W, GRID_W)
                kw = k_ref[pl.ds(start, n_loc), :]
                vw = v_ref[pl.ds(start, n_loc), :]
                cls = _na_class(p, rows)
                s_loc = _dot_nt(qb, kw) * scale + _na_bias_tile(bt_ref, cls)
                m = jnp.maximum(jnp.max(s_loc, axis=-1, keepdims=True), jnp.max(s_ctx, axis=-1, keepdims=True))
                p_loc = jnp.exp(s_loc - m)
                p_ctx = jnp.exp(s_ctx - m)
                inv = 1.0 / (jnp.sum(p_loc, axis=-1, keepdims=True) + jnp.sum(p_ctx, axis=-1, keepdims=True))
                p_loc = p_loc * inv
                p_ctx = p_ctx * inv
                dp_loc = _dot_nt(dob, vw)
                delta = (jnp.sum(p_loc * dp_loc, axis=-1, keepdims=True)
                         + jnp.sum(p_ctx * dp_ctx, axis=-1, keepdims=True))
                ds_loc = p_loc * (dp_loc - delta)
                ds_ctx = p_ctx * (dp_ctx - delta)
                _na_add_bias_grad(dbt_ref, cls, ds_loc)
                dsl = (ds_loc * scale).astype(BF16)
                dsc = (ds_ctx * scale).astype(BF16)
                dq_ref[sl, :] = (_dot(dsl, kw) + _dot(dsc, kc)).astype(dq_ref.dtype)
                dk_ref[pl.ds(start, n_loc), :] += _dot_tn(dsl, qb)
                dv_ref[pl.ds(start, n_loc), :] += _dot_tn(p_loc.astype(BF16), dob)
                dk_ref[pl.ds(n_lat, n_ctx), :] += _dot_tn(dsc, qb)
                dv_ref[pl.ds(n_lat, n_ctx), :] += _dot_tn(p_ctx.astype(BF16), dob)

        @pl.when(g >= pairs // grp)
        def _():
            qb = q_ref[...]
            dob = do_ref[...]
            s_ctx = _dot_nt(qb, kc) * scale
            dp_ctx = _dot_nt(dob, vc)
            m = jnp.max(s_ctx, axis=-1, keepdims=True)
            p = jnp.exp(s_ctx - m)
            p = p * (1.0 / jnp.sum(p, axis=-1, keepdims=True))
            delta = jnp.sum(p * dp_ctx, axis=-1, keepdims=True)
            dsc = (p * (dp_ctx - delta) * scale).astype(BF16)
            dq_ref[...] = _dot(dsc, kc).astype(dq_ref.dtype)
            dk_ref[pl.ds(n_lat, n_ctx), :] += _dot_tn(dsc, qb)
            dv_ref[pl.ds(n_lat, n_ctx), :] += _dot_tn(p.astype(BF16), dob)

    qspec = pl.BlockSpec((grp * NA_PAIR, NA_HEAD_DIM), lambda h, g: (g, h))
    kspec = pl.BlockSpec((t, NA_HEAD_DIM), lambda h, g: (0, h))
    return pl.pallas_call(
        body,
        name="na_attn_bwd",
        grid=(nh, nq // grp),
        in_specs=[pl.BlockSpec((grp * NA_PAIR, NA_HEAD_DIM), lambda h, g: (g, col0[0] + h)),
                  pl.BlockSpec((t, NA_HEAD_DIM), lambda h, g: (0, col0[1] + h)),
                  pl.BlockSpec((t, NA_HEAD_DIM), lambda h, g: (0, col0[2] + h)), qspec, _na_bias_spec()],
        out_specs=[qspec, kspec, kspec, _na_bias_spec()],
        out_shape=[
            jax.ShapeDtypeStruct((t, w), BF16),
            jax.ShapeDtypeStruct((t, w), F32),
            jax.ShapeDtypeStruct((t, w), F32),
            jax.ShapeDtypeStruct(bt.shape, F32),
        ],
        compiler_params=_params(("parallel", "arbitrary")),
    )(q, k, v, do, bt)


def _na_bias_table(rpb, rows):
    pairs = rows // 2
    nb = 2 * NA_WIN_COLS - 1
    nq = NA_KEY_ROWS // 2
    e1 = np.zeros((NA_CLASSES, 2, nq, 2, 2 * NA_WIN_ROWS - 1), np.float32)
    valid = np.zeros((NA_CLASSES, 2, nq, 2), bool)
    for cls, p in enumerate((0, 1, 2, pairs - 2, pairs - 1)):
        base = int(np.clip(2 * p - NA_WIN_ROWS // 2, 0, rows - NA_KEY_ROWS))
        assert p - base // 2 == cls, (rows, cls, p, base)
        for i in range(2):
            r = 2 * p + i
            r0 = int(np.clip(r - NA_WIN_ROWS // 2, 0, rows - NA_WIN_ROWS))
            for kk in range(NA_KEY_ROWS):
                if r0 <= base + kk < r0 + NA_WIN_ROWS:
                    valid[cls, i, kk // 2, kk % 2] = True
                    e1[cls, i, kk // 2, kk % 2, base + kk - r + NA_WIN_ROWS - 1] = 1.0
    cidx = np.arange(GRID_W)
    dc = np.clip(cidx[None, :] - cidx[:, None] + (NA_WIN_COLS - 1), 0, nb - 1)
    c0 = np.clip(cidx - NA_WIN_COLS // 2, 0, GRID_W - NA_WIN_COLS)
    col_in = (cidx[None, :] >= c0[:, None]) & (cidx[None, :] < c0[:, None] + NA_WIN_COLS)
    e2 = np.zeros((GRID_W, 2, GRID_W, 2, nb), np.float32)
    for par in range(2):
        e2[np.arange(GRID_W)[:, None], par, np.arange(GRID_W)[None, :], par, dc] = 1.0
    mask = valid[:, :, :, None, :, None] & col_in[None, None, None, :, None, :]
    t1 = jnp.einsum("hab,xiqpa->hxiqpb", rpb, jnp.asarray(e1), precision=lax.Precision.HIGHEST)
    t1 = t1.reshape(t1.shape[:4] + (2 * nb,))
    b = jnp.einsum("hxiqm,cwm->hxiqcw", t1, jnp.asarray(e2.reshape(GRID_W, 2 * GRID_W, 2 * nb)),
                   precision=lax.Precision.HIGHEST)
    return jnp.where(jnp.asarray(mask.reshape(NA_CLASSES, 2, nq, GRID_W, 2 * GRID_W))[None], b, MASK_VALUE)


def _ret_decays(lam_s, reverse):
    c = RET_CHUNK
    ii = lax.broadcasted_iota(jnp.int32, (c, c), 0)
    jj = lax.broadcasted_iota(jnp.int32, (c, c), 1)
    d = (jj - ii) if reverse else (ii - jj)
    dpos = jnp.maximum(d.astype(F32), 0.0)
    mask = jnp.where(d >= 0, jnp.exp(dpos * lam_s), 0.0)
    pi = lax.broadcasted_iota(jnp.int32, (c, 1), 0).astype(F32)
    qpos = (c - pi) if reverse else (pi + 1.0)
    kpos = pi if reverse else (c - 1.0 - pi)
    qd = jnp.exp(qpos * lam_s)
    kd = jnp.exp(kpos * lam_s)
    g = jnp.exp(jnp.full((1, RET_VAL_DIM), c * lam_s, F32))
    return mask, dpos, qd, kd, qpos, kpos, g


def _ret_head_group(nh):
    return _pick(nh, (8, 4, 2))


def _ret_chunk_of(t, nt, nl, reverse):
    return (nt - 1 - t) if reverse else (t + nl) % nt


def _ret_fwd_call(qr, kr, v, vcol, lam, n_lat, reverse):
    t = qr.shape[0]
    nh = qr.shape[1] // RET_KEY_DIM
    c = RET_CHUNK
    nt, nl = t // c, n_lat // c

    hg = _ret_head_group(nh)
    dk, dv = RET_KEY_DIM, RET_VAL_DIM

    def body(lam_ref, q_ref, k_ref, v_ref, o_ref, s_ref, state):
        hb, step = pl.program_id(0), pl.program_id(1)

        @pl.when(step == 0)
        def _():
            state[...] = jnp.zeros_like(state)

        for j in range(hg):
            mask, _, qd, kd, _, _, g = _ret_decays(lam_ref[hb * hg + j], reverse)
            q, k, vv = q_ref[:, j * dk:(j + 1) * dk], k_ref[:, j * dk:(j + 1) * dk], v_ref[:, j * dv:(j + 1) * dv]
            p = _dot_nt(q, k) * mask
            s = state[j]
            qs = (q.astype(F32) * qd).astype(BF16)
            o_ref[:, j * dv:(j + 1) * dv] = (_dot(p.astype(BF16), vv) + _dot(qs, s.astype(BF16))).astype(o_ref.dtype)
            s_ref[j, 0] = s
            ks = (k.astype(F32) * kd).astype(BF16)
            state[j] = s * g + _dot_tn(ks, vv)

    def cmap(hb, step, lam_ref):
        return (_ret_chunk_of(step, nt, nl, reverse), hb)

    def vmap(hb, step, lam_ref):
        return (_ret_chunk_of(step, nt, nl, reverse), vcol // (hg * dv) + hb)

    return pl.pallas_call(
        body,
        name="retention_rev_fwd" if reverse else "retention_fwd",
        grid_spec=pltpu.PrefetchScalarGridSpec(
            num_scalar_prefetch=1,
            grid=(nh // hg, nt),
            in_specs=[
                pl.BlockSpec((c, hg * dk), cmap),
                pl.BlockSpec((c, hg * dk), cmap),
                pl.BlockSpec((c, hg * dv), vmap),
            ],
            out_specs=[
                pl.BlockSpec((c, hg * dv), cmap),
                pl.BlockSpec((hg, 1, dk, dv), lambda hb, step, lam_ref: (hb, step, 0, 0)),
            ],
            scratch_shapes=[pltpu.VMEM((hg, dk, dv), F32)],
        ),
        out_shape=[
            jax.ShapeDtypeStruct((t, nh * RET_VAL_DIM), BF16),
            jax.ShapeDtypeStruct((nh, nt, RET_KEY_DIM, RET_VAL_DIM), F32),
        ],
        compiler_params=_params(("parallel", "arbitrary")),
    )(lam, qr, kr, v)


def _ret_bwd_call(qr, kr, v, vcol, lam, states, do, n_lat, reverse):
    assert RET_CHUNK == RET_KEY_DIM and RET_VAL_DIM % RET_KEY_DIM == 0
    t = qr.shape[0]
    nh = qr.shape[1] // RET_KEY_DIM
    c = RET_CHUNK
    nt, nl = t // c, n_lat // c

    hg = _ret_head_group(nh)
    dk, dv = RET_KEY_DIM, RET_VAL_DIM

    def body(lam_ref, q_ref, k_ref, v_ref, s_ref, do_ref, dq_ref, dk_ref, dv_ref, dl_ref, dstate):
        hb, rstep = pl.program_id(0), pl.program_id(1)

        @pl.when(rstep == 0)
        def _():
            dstate[...] = jnp.zeros_like(dstate)
            dl_ref[...] = jnp.zeros_like(dl_ref)

        for j in range(hg):
            mask, dpos, qd, kd, qpos, kpos, g = _ret_decays(lam_ref[hb * hg + j], reverse)
            ksl, vsl = slice(j * dk, (j + 1) * dk), slice(j * dv, (j + 1) * dv)
            q, k, vv = q_ref[:, ksl], k_ref[:, ksl], v_ref[:, vsl]
            qf, kf = q.astype(F32), k.astype(F32)
            s = s_ref[j, 0]
            ds = dstate[j]
            dob = do_ref[:, vsl].astype(BF16)
            sb, dsb = s.astype(BF16), ds.astype(BF16)
            a = _dot_nt(q, k)
            p = a * mask
            dp = _dot_nt(dob, vv)
            da = dp * mask
            dab = da.astype(BF16)
            dqc = _dot_nt(dob, sb)
            dkc = _dot_nt(vv, dsb)
            qs = (qf * qd).astype(BF16)
            ks = (kf * kd).astype(BF16)
            dq_ref[:, ksl] = (_dot(dab, k) + dqc * qd).astype(dq_ref.dtype)
            dk_ref[:, ksl] = (_dot_tn(dab, q) + dkc * kd).astype(dk_ref.dtype)
            dv_ref[:, vsl] = (_dot_tn(p.astype(BF16), dob) + _dot(ks, dsb)).astype(dv_ref.dtype)
            dsg = ds * s * (g * c)
            terms = da * a * dpos + dqc * qf * (qd * qpos) + dkc * kf * (kd * kpos)
            for half in range(dv // dk):
                terms = terms + dsg[:, half * dk:(half + 1) * dk]
            total = jnp.sum(jnp.sum(terms, axis=0, keepdims=True), axis=1, keepdims=True)
            dl_ref[j] += jnp.broadcast_to(total, (8, 128))
            dstate[j] = ds * g + _dot_tn(qs, dob)

    def cmap(hb, rstep, lam_ref):
        return (_ret_chunk_of(nt - 1 - rstep, nt, nl, reverse), hb)

    def vmap(hb, rstep, lam_ref):
        return (_ret_chunk_of(nt - 1 - rstep, nt, nl, reverse), vcol // (hg * dv) + hb)

    return pl.pallas_call(
        body,
        name="retention_rev_bwd" if reverse else "retention_bwd",
        grid_spec=pltpu.PrefetchScalarGridSpec(
            num_scalar_prefetch=1,
            grid=(nh // hg, nt),
            in_specs=[
                pl.BlockSpec((c, hg * dk), cmap),
                pl.BlockSpec((c, hg * dk), cmap),
                pl.BlockSpec((c, hg * dv), vmap),
                pl.BlockSpec((hg, 1, dk, dv), lambda hb, rstep, lam_ref: (hb, nt - 1 - rstep, 0, 0)),
                pl.BlockSpec((c, hg * dv), cmap),
            ],
            out_specs=[
                pl.BlockSpec((c, hg * dk), cmap),
                pl.BlockSpec((c, hg * dk), cmap),
                pl.BlockSpec((c, hg * dv), cmap),
                pl.BlockSpec((hg, 8, 128), lambda hb, rstep, lam_ref: (hb, 0, 0)),
            ],
            scratch_shapes=[pltpu.VMEM((hg, dk, dv), F32)],
        ),
        out_shape=[
            jax.ShapeDtypeStruct(qr.shape, qr.dtype),
            jax.ShapeDtypeStruct(kr.shape, kr.dtype),
            jax.ShapeDtypeStruct((t, nh * dv), v.dtype),
            jax.ShapeDtypeStruct((nh, 8, 128), F32),
        ],
        compiler_params=_params(("parallel", "arbitrary")),
    )(lam, qr, kr, v, states, do)


def _rope_tables(t, n_lat):
    nf = RET_KEY_DIM // 4
    tok = np.arange(n_lat)
    inv_freq = (ROPE_BASE ** (-np.arange(nf, dtype=np.float32) / nf)).astype(np.float32)
    row = (tok // GRID_W).astype(np.float32)
    col = (tok % GRID_W).astype(np.float32)
    ang = np.concatenate([row[:, None] * inv_freq, col[:, None] * inv_freq], axis=-1).astype(np.float32)
    cos = np.ones((t, 2 * nf), np.float32)
    sin = np.zeros((t, 2 * nf), np.float32)
    cos[:n_lat] = np.cos(ang)
    sin[:n_lat] = np.sin(ang)
    return jnp.asarray(np.concatenate([cos, cos], axis=1)), jnp.asarray(np.concatenate([-sin, sin], axis=1))


def _rotate(x, cos2, sin2):
    return x * cos2 + pltpu.roll(x, RET_KEY_DIM // 2, 1) * sin2


def _rope_fwd_call(u, qblk, kblk, w_qk, cos2, sin2, k_scale, tm):
    t = u.shape[0]
    nh = w_qk // RET_KEY_DIM

    def body(q_ref, k_ref, c_ref, s_ref, qr_ref, kr_ref):
        c, s = c_ref[...], s_ref[...]
        for hh in range(nh):
            sl = slice(hh * RET_KEY_DIM, (hh + 1) * RET_KEY_DIM)
            qr_ref[:, sl] = _rotate(q_ref[:, sl].astype(F32), c, s).astype(qr_ref.dtype)
            kr_ref[:, sl] = (_rotate(k_ref[:, sl].astype(F32), c, s) * k_scale).astype(kr_ref.dtype)

    tab = pl.BlockSpec((tm, RET_KEY_DIM), lambda i: (i, 0))
    out = pl.BlockSpec((tm, w_qk), lambda i: (i, 0))
    return pl.pallas_call(
        body, name="rope_fwd", grid=(t // tm,),
        in_specs=[pl.BlockSpec((tm, w_qk), lambda i: (i, qblk)), pl.BlockSpec((tm, w_qk), lambda i: (i, kblk)), tab, tab],
        out_specs=[out, out], out_shape=[jax.ShapeDtypeStruct((t, w_qk), BF16)] * 2,
        compiler_params=_params(("parallel",)),
    )(u, u, cos2, sin2)


def _assemble_du_call(pieces, off, cos2, sin2, k_scale, tm):
    flat = [a for p in pieces for a in (p if isinstance(p, tuple) else (p,))]
    t = flat[0].shape[0]
    n = len(flat)

    def body(*refs):
        c, s = refs[n][...], -refs[n + 1][...]
        o_ref = refs[n + 2]
        it = iter(refs[:n])
        for blk, p in enumerate(pieces):
            lo = off[blk]
            if not isinstance(p, tuple):
                o_ref[:, lo:off[blk + 1]] = next(it)[...].astype(o_ref.dtype)
                continue
            fwd_ref, rev_ref = next(it), next(it)
            if blk == 6:
                o_ref[:, lo:off[blk + 1]] = (fwd_ref[...].astype(F32) + rev_ref[...].astype(F32)).astype(o_ref.dtype)
                continue
            mult = k_scale if blk == 5 else 1.0
            for hh in range((off[blk + 1] - lo) // RET_KEY_DIM):
                sl = slice(hh * RET_KEY_DIM, (hh + 1) * RET_KEY_DIM)
                dy = (fwd_ref[:, sl].astype(F32) + rev_ref[:, sl].astype(F32)) * mult
                o_ref[:, lo + hh * RET_KEY_DIM:lo + (hh + 1) * RET_KEY_DIM] = _rotate(dy, c, s).astype(o_ref.dtype)

    tab = pl.BlockSpec((tm, RET_KEY_DIM), lambda i: (i, 0))
    return pl.pallas_call(
        body, name="assemble_du", grid=(t // tm,),
        in_specs=[pl.BlockSpec((tm, a.shape[1]), lambda i: (i, 0)) for a in flat] + [tab, tab],
        out_specs=pl.BlockSpec((tm, off[-1]), lambda i: (i, 0)),
        out_shape=jax.ShapeDtypeStruct((t, off[-1]), BF16),
        compiler_params=_params(("parallel",)),
    )(*flat, cos2, sin2)


def _my_position():
    return lax.axis_index("x"), lax.axis_index("y"), lax.axis_index("c")


def _flip(pos, k):
    x, y, c = pos
    return (1 - x if k & 4 else x, 1 - y if k & 2 else y, 1 - c if k & 1 else c)


def _linear(pos):
    return 4 * pos[0] + 2 * pos[1] + pos[2]


def _slab(ref, axis, idx, size):
    start = pl.multiple_of(idx * size, size)
    return ref.at[pl.ds(start, size), :] if axis == 0 else ref.at[:, pl.ds(start, size)]


HBM_SPEC = pl.BlockSpec(memory_space=pltpu.HBM)
SEM_SPEC = pl.BlockSpec(memory_space=pltpu.SEMAPHORE)
DATAFLOW = pltpu.SideEffectType.DATAFLOW_SIDE_EFFECTING
PEER_BITS = (1, 2, 4, 6, 3, 5, 7)
GATHER_BITS = (1, 2, 4, 6)


def _in_hbm(a):
    return pltpu.with_memory_space_constraint(a, pltpu.HBM)


def _gather_views(me, k, a, src_refs, land_refs, axes):
    size = src_refs[a].shape[axes[a]]
    peer = _flip(me, k)
    return src_refs[a], _slab(land_refs[a], axes[a], _linear(me), size), _slab(land_refs[a], axes[a], _linear(peer), size)


def _scatter_views(me, k, a, src_refs, land_refs, axes):
    size = land_refs[a].shape[1 + axes[a]]
    peer = _flip(me, k)
    return _slab(src_refs[a], axes[a], _linear(peer), size), land_refs[a].at[k - 1], land_refs[a].at[k - 1]


CHIP_BITS = (0, 2, 4, 6)


def _chip_views(me, k, a, src_refs, land_refs, axes):
    j = CHIP_BITS.index(k)
    return src_refs[a].at[j], land_refs[a].at[j - 1], land_refs[a].at[j - 1]


def _pair_exchange(grads, axes, sizes):
    ns = len(grads)

    def slab_shape(a):
        s = grads[a].shape
        return (sizes[a], s[1]) if axes[a] == 0 else (s[0], sizes[a])

    def body(*refs):
        g_refs, p_refs = refs[:ns], refs[ns:2 * ns]
        send_sems, recv_sems = refs[2 * ns:]
        me = _my_position()
        sibling = _flip(me, 1)
        copies = []
        for j, kc in enumerate(CHIP_BITS):
            for a in range(ns):
                cp = pltpu.make_async_remote_copy(
                    src_ref=_slab(g_refs[a], axes[a], _linear(_flip(me, kc | 1)), sizes[a]), dst_ref=p_refs[a].at[j],
                    send_sem=send_sems.at[4 * a + j], recv_sem=recv_sems.at[4 * a + j],
                    device_id=sibling, device_id_type=MESH)
                cp.start()
                copies.append(cp)
        for cp in copies:
            cp.wait_recv()
        for cp in copies:
            cp.wait_send()

    return pl.pallas_call(
        body, name="scatter_pair_exchange", in_specs=[ANY] * ns, out_specs=[ANY] * ns,
        out_shape=[jax.ShapeDtypeStruct((4,) + slab_shape(a), grads[a].dtype) for a in range(ns)],
        scratch_shapes=[pltpu.SemaphoreType.DMA((4 * ns,)), pltpu.SemaphoreType.DMA((4 * ns,))],
        compiler_params=pltpu.CompilerParams(has_side_effects=True),
    )(*grads)


def _pair_add(grad, theirs, axis, chip_idx):
    _, r, c = theirs.shape
    tm = _pick(r, (256, 128, 64, 32, 16))
    if axis == 0:
        mine_spec = pl.BlockSpec((tm, c), lambda j, i, idx: (idx[j] * (r // tm) + i, 0))
    else:
        mine_spec = pl.BlockSpec((tm, c), lambda j, i, idx: (i, idx[j]))

    def body(idx_ref, mine_ref, theirs_ref, o_ref):
        o_ref[0] = (mine_ref[...].astype(F32) + theirs_ref[0].astype(F32)).astype(o_ref.dtype)

    spec = pl.BlockSpec((1, tm, c), lambda j, i, idx: (j, i, 0))
    return pl.pallas_call(
        body, name="scatter_pair_add",
        grid_spec=pltpu.PrefetchScalarGridSpec(
            num_scalar_prefetch=1, grid=(4, r // tm), in_specs=[mine_spec, spec], out_specs=spec),
        out_shape=jax.ShapeDtypeStruct(theirs.shape, theirs.dtype),
        compiler_params=_params(("parallel", "parallel")),
    )(chip_idx, grad, theirs)


def _slab_block(rows, cols, tm, axis):
    if axis == 0:
        return pl.BlockSpec((tm, cols), lambda i, idx: (idx[0] * (rows // tm) + i, 0))
    return pl.BlockSpec((tm, cols), lambda i, idx: (i, idx[0]))


def _place_shard(shard, land, axis, my_idx):
    r, c = shard.shape
    tm = _pick(r, (512, 256, 128, 64, 32, 16))

    def body(idx_ref, s_ref, land_ref, o_ref):
        o_ref[...] = s_ref[...]

    return pl.pallas_call(
        body, name="gather_place",
        grid_spec=pltpu.PrefetchScalarGridSpec(
            num_scalar_prefetch=1, grid=(r // tm,),
            in_specs=[pl.BlockSpec((tm, c), lambda i, idx: (i, 0)), ANY],
            out_specs=_slab_block(r, c, tm, axis)),
        out_shape=jax.ShapeDtypeStruct(land.shape, land.dtype),
        input_output_aliases={2: 0},
        compiler_params=_params(("parallel",)),
    )(my_idx, shard, land)


def _push_start(name, srcs, lands, axes, views, bits, deps):
    ns = len(srcs)

    def body(*refs):
        src_refs, land_refs = refs[:ns], refs[ns:2 * ns]
        send_sems, recv_sems = refs[2 * ns + len(deps):2 * ns + len(deps) + 2]
        token = refs[-1]
        me = _my_position()
        for k in bits:
            for a in range(ns):
                s, d, _ = views(me, k, a, src_refs, land_refs, axes)
                pltpu.make_async_remote_copy(
                    src_ref=s, dst_ref=d, send_sem=send_sems.at[7 * a + k - 1], recv_sem=recv_sems.at[7 * a + k - 1],
                    device_id=_flip(me, k), device_id_type=MESH).start()
        token[...] = jnp.zeros_like(token)

    thru = [pltpu.HBM(a.shape, a.dtype) for a in list(srcs) + list(lands)]
    outs = pl.pallas_call(
        body, name=name,
        in_specs=[HBM_SPEC] * (2 * ns) + [ANY] * len(deps),
        out_specs=[SEM_SPEC, SEM_SPEC] + [HBM_SPEC] * (2 * ns) + [VMEM_SPEC],
        out_shape=[pltpu.SemaphoreType.DMA((7 * ns,)), pltpu.SemaphoreType.DMA((7 * ns,))] + thru
        + [jax.ShapeDtypeStruct((8, 128), F32)],
        input_output_aliases={i: 2 + i for i in range(2 * ns)},
        compiler_params=pltpu.CompilerParams(has_side_effects=DATAFLOW),
    )(*[_in_hbm(a) for a in srcs], *[_in_hbm(a) for a in lands], *deps)
    return (outs[0], outs[1]), outs[2:2 + ns], outs[2 + ns:2 + 2 * ns], outs[-1]


def _gather_finish(lands, axes, sizes):
    ns = len(lands)
    chips = (2, 4, 6)

    def body(*refs):
        land_refs = refs[ns:2 * ns]
        send_sems, recv_sems = refs[2 * ns:]
        me = _my_position()
        sibling = _flip(me, 1)
        copies = []
        for j, kc in enumerate(chips):
            for a in range(ns):
                def slab_of(pos):
                    return _slab(land_refs[a], axes[a], _linear(pos), sizes[a])
                send = pltpu.make_async_remote_copy(
                    src_ref=slab_of(_flip(me, kc)), dst_ref=slab_of(_flip(me, kc)), send_sem=send_sems.at[3 * a + j],
                    recv_sem=recv_sems.at[3 * a + j], device_id=sibling, device_id_type=MESH)
                recv = pltpu.make_async_remote_copy(
                    src_ref=slab_of(_flip(me, kc)), dst_ref=slab_of(_flip(sibling, kc)), send_sem=send_sems.at[3 * a + j],
                    recv_sem=recv_sems.at[3 * a + j], device_id=sibling, device_id_type=MESH)
                send.start()
                copies.append((send, recv))
        for send, recv in copies:
            recv.wait_recv()
        for send, recv in copies:
            send.wait_send()

    return pl.pallas_call(
        body, name="gather_finish", in_specs=[ANY] * ns, out_specs=[ANY] * ns,
        out_shape=[jax.ShapeDtypeStruct(l.shape, l.dtype) for l in lands],
        input_output_aliases={a: a for a in range(ns)},
        scratch_shapes=[pltpu.SemaphoreType.DMA((3 * ns,)), pltpu.SemaphoreType.DMA((3 * ns,))],
        compiler_params=pltpu.CompilerParams(has_side_effects=True),
    )(*lands)


def _push_wait(name, sems, srcs, lands, axes, views, bits, after):
    ns = len(srcs)

    def body(*refs):
        src_refs, land_refs = refs[:ns], refs[ns:2 * ns]
        send_sems, recv_sems = refs[2 * ns:2 * ns + 2]
        me = _my_position()
        for k in bits:
            for a in range(ns):
                s, d, got = views(me, k, a, src_refs, land_refs, axes)
                cp = pltpu.make_async_remote_copy(
                    src_ref=s, dst_ref=got, send_sem=send_sems.at[7 * a + k - 1], recv_sem=recv_sems.at[7 * a + k - 1],
                    device_id=_flip(me, k), device_id_type=MESH)
                cp.wait_send()
                cp.wait_recv()

    thru = [pltpu.HBM(a.shape, a.dtype) for a in list(srcs) + list(lands)]
    outs = pl.pallas_call(
        body, name=name,
        in_specs=[HBM_SPEC] * (2 * ns) + [SEM_SPEC, SEM_SPEC] + [ANY] * len(after),
        out_specs=[HBM_SPEC] * (2 * ns),
        out_shape=thru,
        input_output_aliases={i: i for i in range(2 * ns)},
        compiler_params=pltpu.CompilerParams(has_side_effects=DATAFLOW),
    )(*srcs, *lands, sems[0], sems[1], *after)
    return outs[:ns], outs[ns:]


def _small_allgather(v, name):
    r, c = v.shape

    def body(v_ref, all_ref, sum_ref, send_sems, recv_sems):
        me = _my_position()
        all_ref[_linear(me)] = v_ref[...]
        copies = []
        for k in range(1, N_DEV):
            peer = _flip(me, k)
            copies.append(pltpu.make_async_remote_copy(
                src_ref=v_ref, dst_ref=all_ref.at[_linear(me)], send_sem=send_sems.at[k - 1], recv_sem=recv_sems.at[k - 1],
                device_id=peer, device_id_type=MESH))
        for cp in copies:
            cp.start()
        for k in range(1, N_DEV):
            peer = _flip(me, k)
            pltpu.make_async_remote_copy(
                src_ref=v_ref, dst_ref=all_ref.at[_linear(peer)], send_sem=send_sems.at[k - 1], recv_sem=recv_sems.at[k - 1],
                device_id=peer, device_id_type=MESH).wait_recv()
        for cp in copies:
            cp.wait_send()
        acc = all_ref[0]
        for d in range(1, N_DEV):
            acc = acc + all_ref[d]
        sum_ref[...] = acc

    return pl.pallas_call(
        body,
        name=name,
        in_specs=[VMEM_SPEC],
        out_specs=[VMEM_SPEC, VMEM_SPEC],
        out_shape=[jax.ShapeDtypeStruct((N_DEV, r, c), F32), jax.ShapeDtypeStruct((r, c), F32)],
        scratch_shapes=[pltpu.SemaphoreType.DMA((N_DEV - 1,)), pltpu.SemaphoreType.DMA((N_DEV - 1,))],
        compiler_params=pltpu.CompilerParams(has_side_effects=True, vmem_limit_bytes=VMEM_LIMIT),
    )(v)


def _ada_fwd_call(cin, ada_w, ada_b_cols):
    nl, d, ncol = ada_w.shape
    nrow = cin.shape[0]

    def body(c_ref, w_ref, b_ref, o_ref):
        cs = _silu(c_ref[...]).astype(BF16)
        for l in range(nl):
            o_ref[l] = _dot(cs, w_ref[l].astype(BF16)) + b_ref[l]

    return pl.pallas_call(
        body, name="ada_fwd", in_specs=[VMEM_SPEC] * 3, out_specs=VMEM_SPEC,
        out_shape=jax.ShapeDtypeStruct((nl, nrow, ncol), F32), compiler_params=_params(),
    )(cin, ada_w, ada_b_cols)


def _ada_bwd_call(cin, ada_w, dmod):
    nl, d, ncol = ada_w.shape
    nrow = cin.shape[0]

    def body(c_ref, w_ref, dm_ref, gw_ref, dcs_ref):
        cs = _silu(c_ref[...]).astype(BF16)
        acc = jnp.zeros((nrow, d), F32)
        for l in range(nl):
            dm = dm_ref[l].astype(BF16)
            gw_ref[l] = _dot_tn(cs, dm)
            acc = acc + _dot_nt(dm, w_ref[l].astype(BF16))
        dcs_ref[...] = acc

    return pl.pallas_call(
        body, name="ada_bwd", in_specs=[VMEM_SPEC] * 3, out_specs=[VMEM_SPEC, VMEM_SPEC],
        out_shape=[jax.ShapeDtypeStruct((nl, d, ncol), F32), jax.ShapeDtypeStruct((nrow, d), F32)],
        compiler_params=_params(),
    )(cin, ada_w, dmod)


def _adamw_math(w, g, m, v):
    m = ADAM_B1 * m + (1.0 - ADAM_B1) * g
    v = ADAM_B2 * v + (1.0 - ADAM_B2) * jnp.square(g)
    m_hat = m / (1.0 - ADAM_B1 ** ADAM_STEP)
    v_hat = v / (1.0 - ADAM_B2 ** ADAM_STEP)
    delta = -ADAM_LR * (m_hat / (jnp.sqrt(v_hat) + ADAM_EPS) + ADAM_WD * w)
    return delta, m, v


def _adamw_sharded(w, m, v, mine, slabs, axis, my_idx, layer, prev, name):
    nl, r, c = w.shape
    tm = _pick(r, (128, 64, 32, 16))
    nprev = 0 if prev is None else len(prev)
    nslab = slabs.shape[0]
    if axis is None:
        mine_spec = pl.BlockSpec((1, tm, c), lambda i, idx: (0, i, 0))
    else:
        mine_spec = _slab_block(r, c, tm, axis)

    def body(idx_ref, w_ref, m_ref, v_ref, mine_ref, s_ref, *rest):
        g_ref, d_ref, nm_ref, nv_ref = rest[nprev:]
        g = (mine_ref[0] if axis is None else mine_ref[...]).astype(F32)
        for k in range(nslab):
            g = g + s_ref[k].astype(F32)
        delta, nm, nv = _adamw_math(w_ref[0], g, m_ref[0], v_ref[0])
        g_ref[0], d_ref[0], nm_ref[0], nv_ref[0] = g, delta, nm, nv

    spec = pl.BlockSpec((1, tm, c), lambda i, idx: (layer, i, 0))
    out = jax.ShapeDtypeStruct(w.shape, F32)
    return pl.pallas_call(
        body, name=name,
        grid_spec=pltpu.PrefetchScalarGridSpec(
            num_scalar_prefetch=1, grid=(r // tm,),
            in_specs=[spec, spec, spec, mine_spec,
                      pl.BlockSpec((nslab, tm, c), lambda i, idx: (0, i, 0))] + [ANY] * nprev,
            out_specs=[spec] * 4),
        out_shape=[out] * 4,
        input_output_aliases={6 + j: j for j in range(nprev)},
        compiler_params=_params(("parallel",)),
    )(my_idx, w, m, v, mine, slabs, *(() if prev is None else prev))


def _adamw_dense(w, g, m, v, name):
    r, c = w.shape
    tm = _pick(r, (256, 128, 64, 32, 16, 8))

    def body(w_ref, g_ref, m_ref, v_ref, d_ref, nm_ref, nv_ref):
        d_ref[...], nm_ref[...], nv_ref[...] = _adamw_math(w_ref[...], g_ref[...], m_ref[...], v_ref[...])

    spec = pl.BlockSpec((tm, c), lambda i: (i, 0))
    out = jax.ShapeDtypeStruct(w.shape, F32)
    return pl.pallas_call(
        body, name=name, grid=(r // tm,), in_specs=[spec] * 4, out_specs=[spec] * 3, out_shape=[out] * 3,
        compiler_params=_params(("parallel",)),
    )(w, g, m, v)


def _pack(parts, width=128):
    flat = jnp.concatenate([p.reshape(-1).astype(F32) for p in parts])
    n = flat.shape[0]
    total = -(-n // (8 * width)) * (8 * width)
    return jnp.pad(flat, (0, total - n)).reshape(total // width, width)


def _unpack(buf, shapes):
    flat = buf.reshape(-1)
    out, off = [], 0
    for s in shapes:
        n = int(np.prod(s))
        out.append(flat[off:off + n].reshape(s))
        off += n
    return out


def kernel(x, c, ctx, c_ctx, ada_w, ada_b, norm_g, w_in, na_rpb, ret_decay_logit, w_proj_na, w_proj_ret, w_out, final_g, loss_target, m_c_ctx, m_ada_w, m_ada_b, m_norm_g, m_w_in, m_na_rpb, m_ret_decay_logit, m_w_proj_na, m_w_proj_ret, m_w_out, m_final_g, v_c_ctx, v_ada_w, v_ada_b, v_norm_g, v_w_in, v_na_rpb, v_ret_decay_logit, v_w_proj_na, v_w_proj_ret, v_w_out, v_final_g):
    depth = w_in.shape[0]
    n_lat, d = x.shape[1], x.shape[2]
    n_ctx = ctx.shape[1]
    t = n_lat + n_ctx
    w_na = w_proj_na.shape[1]
    w_retv = w_proj_ret.shape[1] * N_DEV
    in_cols = w_in.shape[2] * N_DEV
    w_qk = (in_cols - 4 * w_na - 2 * w_retv - 2 * d) // 2
    sizes = (w_na, w_na, w_na, w_na, w_qk, w_qk, w_retv, w_retv, d, d)
    off = tuple(int(o) for o in np.cumsum((0,) + sizes))
    NA_Q, NA_K, NA_V, NA_Z, RET_Q, RET_K, RET_V, RET_Z, G_NA, G_RET = range(10)
    rows = n_lat // GRID_W
    me = _my_position()
    my_idx = _linear(me)
    tm_row = _pick(n_ctx, (256, 128))

    idx_arr = jnp.reshape(my_idx, (1,)).astype(jnp.int32)
    chip_idx = jnp.stack([_linear(_flip(me, kc)) for kc in CHIP_BITS]).astype(jnp.int32)

    w_axes = (1, 1, 0, 0)
    w_names = ("w_in", "w_proj_na", "w_proj_ret", "w_out")
    shard = [[w[l].astype(BF16) for w in (w_in, w_proj_na, w_proj_ret, w_out)] for l in range(depth)]
    groups = [[(0, 0)], [(0, 1), (0, 2), (0, 3)]] + [[(l, a) for a in range(4)] for l in range(1, depth)]
    gathers = {}

    def start_gather(gi, deps):
        keys = groups[gi]
        srcs = [shard[l][a] for l, a in keys]
        axes = tuple(w_axes[a] for _, a in keys)
        lands = [_place_shard(s, lax.empty(tuple(n * (N_DEV if i == ax else 1) for i, n in enumerate(s.shape)), BF16),
                              ax, idx_arr) for s, ax in zip(srcs, axes)]
        sizes = tuple(s.shape[ax] for s, ax in zip(srcs, axes))
        sems, srcs, lands, tok = _push_start(f"gather_start_{gi}", srcs, lands, axes, _gather_views, GATHER_BITS, deps)
        flight = dict(name=f"gather_wait_{gi}", sems=sems, srcs=srcs, lands=lands, axes=axes, sizes=sizes, ready=None)
        for pos, key in enumerate(keys):
            gathers[key] = (flight, pos)
        return tok

    token = start_gather(0, ())

    ncol = ada_w.shape[2]
    c_all, _ = _small_allgather(jnp.pad(c, ((0, 7), (0, 0))) + token[:, :1], "allgather_c")
    cin = jnp.concatenate([c_all[:, 0, :], c_ctx[None, :], jnp.zeros((7, d), F32)], axis=0)
    ada_b_cols = lax.dynamic_slice_in_dim(ada_b, my_idx * ncol, ncol, axis=1)[:, None, :]
    mod_cols = _ada_fwd_call(cin, ada_w, ada_b_cols)
    mod_gathered, _ = _small_allgather(mod_cols.reshape(depth * 16, ncol), "allgather_mod")
    mod_all = mod_gathered.reshape(N_DEV, depth, 16, ncol).transpose(1, 2, 0, 3).reshape(depth, 16, N_DEV * ncol)
    mod_lat = lax.dynamic_index_in_dim(mod_all, my_idx, axis=1, keepdims=False)
    mod_ctx = mod_all[:, 8, :]
    token = mod_gathered
    for gi in range(1, len(groups)):
        token = start_gather(gi, (token,))

    def landed(l, a, act):
        flight, pos = gathers[(l, a)]
        if flight["ready"] is None:
            arrived = _push_wait(flight["name"], flight["sems"], flight["srcs"], flight["lands"],
                                 flight["axes"], _gather_views, GATHER_BITS, (act, token))[1]
            flight["ready"] = _gather_finish(arrived, flight["axes"], flight["sizes"])
        return flight["ready"][pos]

    pending, scatters = {}, []

    def send_dw(l, a, dw):
        pending[(l, a)] = dw
        if a == 0:
            keys = [(0, 0)] if l == 0 else [(l, b) for b in range(4)]
        elif l == 0 and a == 1:
            keys = [(0, 1), (0, 2), (0, 3)]
        else:
            return None
        srcs = [pending[k] for k in keys]
        axes = tuple(w_axes[b] for _, b in keys)
        sizes = tuple(s.shape[ax] // N_DEV for s, ax in zip(srcs, axes))
        slab_shapes = [tuple(n // (N_DEV if i == ax else 1) for i, n in enumerate(s.shape)) for s, ax in zip(srcs, axes)]
        by_chip = keys == [(0, 0)]
        if by_chip:
            theirs = _pair_exchange(srcs, axes, sizes)
            srcs = [_pair_add(g, p, ax, chip_idx) for g, p, ax in zip(srcs, theirs, axes)]
            lands = [lax.empty((3,) + shp, BF16) for shp in slab_shapes]
            views, bits = _chip_views, CHIP_BITS[1:]
        else:
            lands = [lax.empty((N_DEV - 1,) + shp, BF16) for shp in slab_shapes]
            views, bits = _scatter_views, PEER_BITS
        sems, srcs, lands, tok = _push_start(f"scatter_start_{len(scatters)}", srcs, lands, axes, views, bits, ())
        scatters.append(dict(name=f"scatter_wait_{len(scatters)}", sems=sems, srcs=srcs, lands=lands, axes=axes, keys=keys,
                             views=views, bits=bits, by_chip=by_chip))
        return tok

    cos2, sin2 = _rope_tables(t, n_lat)
    k_scale = RET_KEY_DIM ** -0.5
    assert off[RET_Q] % w_qk == 0 and off[RET_K] % w_qk == 0
    assert off[NA_Z] % w_na == 0 and off[G_NA] % d == 0 and off[G_RET] % d == 0 and off[RET_Z] % w_retv == 0
    assert off[RET_V] % (_ret_head_group(w_retv // RET_VAL_DIM) * RET_VAL_DIM) == 0
    na_cols = tuple(off[i] // NA_HEAD_DIM for i in (NA_Q, NA_K, NA_V))
    norm_mod_fwd, norm_mod_bwd = _make_rowwise(_f_norm_mod, "norm_mod", (BF16,), (d,), n_lat, tm_row, (0,))
    gate_na_fwd, gate_na_bwd = _make_rowwise(_f_gate_na, "gate_na", (BF16,), (w_na,), n_lat, tm_row, (0, 1),
                                             col_blocks={1: (w_na, off[NA_Z] // w_na)})
    merge_fwd, merge_bwd = _make_rowwise(_f_merge, "merge", (BF16,), (d,), n_lat, tm_row, (0, 1, 2, 3),
                                         col_blocks={0: (d, off[G_NA] // d), 1: (d, off[G_RET] // d)},
                                         drow_dtypes={2: BF16, 3: BF16})
    residual_fwd, _ = _make_rowwise(_f_residual, "residual", (F32,), (d,), n_lat, tm_row, (0, 1))
    residual_norm_fwd, _ = _make_rowwise(_f_residual_norm, "residual_norm", (F32, BF16), (d, d), n_lat, tm_row, (0, 1))
    _, residual_bwd = _make_rowwise(lambda out, gate: (gate * out,), "residual", (F32,), (d,), n_lat, tm_row, (0,),
                                    drow_dtypes={0: BF16})
    loss_fwd, loss_bwd = _make_rowwise(_f_loss, "loss_head", (F32,), (128,), n_lat, tm_row, (0,))

    def pair(a, b):
        return jnp.stack([a, b])[:, None, :]

    def mod_vectors(mod_lat_l, mod_ctx_l, norm_g_l):
        shift, scale, gate = jnp.split(mod_lat_l, 3)
        c_shift, c_scale, c_gate = jnp.split(mod_ctx_l, 3)
        return pair(norm_g_l, norm_g_l), pair(scale, c_scale), pair(shift, c_shift), pair(gate, c_gate)

    def log_decay(logit):
        return jax.nn.log_sigmoid(logit.astype(F32))

    xa = jnp.concatenate([x[0], ctx[0]], axis=0)
    saved = []
    all_vecs = [jax.vjp(mod_vectors, mod_lat[l], mod_ctx[l], norm_g[l]) for l in range(depth)]
    (h,) = norm_mod_fwd((xa,), all_vecs[0][0][:3])
    for l in range(depth):
        vecs, vecs_vjp = all_vecs[l]
        wl_in = landed(l, 0, h)
        u = _matmul(h, wl_in, out_dtype=BF16, name="in_proj_fwd")
        qr, kr = _rope_fwd_call(u, off[RET_Q] // w_qk, off[RET_K] // w_qk, w_qk, cos2, sin2, k_scale, tm_row)
        bt, bt_vjp = jax.vjp(lambda r: _na_bias_table(r, rows), na_rpb[l])
        lam, lam_vjp = jax.vjp(log_decay, ret_decay_logit[l])
        o_na = _na_fwd_call(u, u, u, na_cols, w_na, bt, n_lat)
        o_f, st_f = _ret_fwd_call(qr, kr, u, off[RET_V], lam[0], n_lat, False)
        o_b, st_b = _ret_fwd_call(qr, kr, u, off[RET_V], lam[1], n_lat, True)
        (a_na,) = gate_na_fwd((o_na, u), ())
        a_ret = _gate_ret_fwd_call(o_f, o_b, u, off[RET_Z] // w_retv, tm_row)
        wl_pna, wl_pret, wl_out = landed(l, 1, a_na), landed(l, 2, a_na), landed(l, 3, a_na)
        y_na = _matmul(a_na, wl_pna, out_dtype=BF16, name="proj_na_fwd")
        y_ret = _matmul(a_ret, wl_pret, out_dtype=BF16, name="proj_ret_fwd")
        (merged,) = merge_fwd((u, u, y_na, y_ret), ())
        out = _matmul(merged, wl_out, out_dtype=F32, name="out_proj_fwd")
        if l + 1 < depth:
            xa_next, h_next = residual_norm_fwd((xa, out), (vecs[3],) + tuple(all_vecs[l + 1][0][:3]))
        else:
            (xa_next,), h_next = residual_fwd((xa, out), vecs[3:]), None
        saved.append(dict(xa=xa, vecs=vecs, vecs_vjp=vecs_vjp, h=h, w=(wl_in, wl_pna, wl_pret, wl_out), u=u, qr=qr, kr=kr,
                          bt=bt, bt_vjp=bt_vjp, lam=lam, lam_vjp=lam_vjp, o_na=o_na, o_f=o_f,
                          o_b=o_b, st_f=st_f, st_b=st_b, a_na=a_na, a_ret=a_ret, y_na=y_na, y_ret=y_ret,
                          merged=merged, out=out))
        xa, h = xa_next, h_next

    fg_pair, fg_vjp = jax.vjp(lambda g: pair(g, g), final_g)
    (loss_rows,) = loss_fwd((xa, loss_target[0]), (fg_pair,))
    loss = lax.psum(jnp.sum(loss_rows), ("x", "y", "c"))
    (dx_last,), (d_fg_pair,) = loss_bwd((xa, loss_target[0]), (fg_pair,), (jnp.ones_like(loss_rows),))
    (d_final_g,) = fg_vjp(d_fg_pair)
    dxa = jnp.pad(dx_last, ((0, n_ctx), (0, 0)))

    d_mod_lat, d_mod_ctx, d_norm_g, d_rpb, d_decay = ([None] * depth for _ in range(5))
    for l in reversed(range(depth)):
        s = saved[l]
        u, qr, kr = s["u"], s["qr"], s["kr"]
        wl_in, wl_pna, wl_pret, wl_out = s["w"]
        (d_out,), (d_gate,) = residual_bwd((s["out"],), s["vecs"][3:], (dxa,))
        dxa_res = dxa
        send_dw(l, 3, _matmul(s["merged"], d_out, trans_a=True, out_dtype=BF16, name="out_proj_dw"))
        d_merged = _matmul(d_out, wl_out, trans_b=True, out_dtype=BF16, name="out_proj_da")
        (dg_na, dg_ret, dy_na, dy_ret), _ = merge_bwd((u, u, s["y_na"], s["y_ret"]), (), (d_merged,))
        send_dw(l, 2, _matmul(s["a_ret"], dy_ret, trans_a=True, out_dtype=BF16, name="proj_ret_dw"))
        da_ret = _matmul(dy_ret, wl_pret, trans_b=True, out_dtype=BF16, name="proj_ret_da")
        tok = send_dw(l, 1, _matmul(s["a_na"], dy_na, trans_a=True, out_dtype=BF16, name="proj_na_dw"))
        da_na = _matmul(dy_na, wl_pna, trans_b=True, out_dtype=BF16, name="proj_na_da", after=tok)
        do_ret, dz_ret = _gate_ret_bwd_call(s["o_f"], s["o_b"], u, off[RET_Z] // w_retv, da_ret, tm_row)
        (do_na, dz_na), _ = gate_na_bwd((s["o_na"], u), (), (da_na,))
        dq_f, dk_f, dv_f, dl_f = _ret_bwd_call(qr, kr, u, off[RET_V], s["lam"][0], s["st_f"], do_ret, n_lat, False)
        dq_b, dk_b, dv_b, dl_b = _ret_bwd_call(qr, kr, u, off[RET_V], s["lam"][1], s["st_b"], do_ret, n_lat, True)
        dq, dk, dv, dbt = _na_bwd_call(u, u, u, na_cols, w_na, s["bt"], do_na, n_lat)
        du = _assemble_du_call([dq, dk, dv, dz_na, (dq_f, dq_b), (dk_f, dk_b), (dv_f, dv_b), dz_ret, dg_na, dg_ret],
                               off, cos2, sin2, k_scale, _pick(n_ctx, (128,)))
        (d_rpb[l],) = s["bt_vjp"](dbt)
        (d_decay[l],) = s["lam_vjp"](jnp.stack([dl_f[:, 0, 0], dl_b[:, 0, 0]]))
        tok = send_dw(l, 0, _matmul(s["h"], du, trans_a=True, out_dtype=BF16, name="in_proj_dw"))
        dh = _matmul(du, wl_in, trans_b=True, out_dtype=BF16, name="in_proj_da", after=tok)
        (dxa,), d_vecs = norm_mod_bwd((s["xa"],), s["vecs"][:3], (dh,), acc=(dxa_res,))
        d_mod_lat[l], d_mod_ctx[l], d_norm_g[l] = s["vecs_vjp"](tuple(d_vecs) + (d_gate,))
    gx = dxa[:n_lat]
    d_mod_lat, d_mod_ctx, d_norm_g, d_rpb, d_decay = (jnp.stack(a) for a in (d_mod_lat, d_mod_ctx, d_norm_g, d_rpb, d_decay))

    small_shapes = [d_mod_lat.shape, d_mod_ctx.shape, d_norm_g.shape, d_final_g.shape, d_rpb.shape, d_decay.shape]
    packed = _pack([d_mod_lat, d_mod_ctx, d_norm_g, d_final_g, d_rpb, d_decay])
    g_all, g_sum = _small_allgather(packed, "allgather_small_grads")
    dml_sum, dmc_sum, grad_norm_g, grad_final_g, grad_na_rpb, grad_decay = _unpack(g_sum, small_shapes)
    grad_ada_b = dml_sum + dmc_sum
    dml_all = g_all.reshape(N_DEV, -1)[:, :depth * 3 * d].reshape(N_DEV, depth, 3 * d)

    def my_cols(a):
        return lax.dynamic_slice_in_dim(a, my_idx * ncol, ncol, axis=a.ndim - 1)

    dmod = jnp.concatenate(
        [my_cols(dml_all).transpose(1, 0, 2), my_cols(dmc_sum)[:, None, :], jnp.zeros((depth, 7, ncol), F32)], axis=1)
    grad_ada_w, dcs_part = _ada_bwd_call(cin, ada_w, dmod)
    _, dcs = _small_allgather(dcs_part, "allgather_dcsilu")
    sg = jax.nn.sigmoid(c_ctx)
    grad_c_ctx = dcs[8] * (sg * (1.0 + c_ctx * (1.0 - sg)))

    def flat2(a):
        return a.reshape(a.shape[0] * a.shape[1], a.shape[2])

    small_w = [c_ctx, ada_b, norm_g, na_rpb, ret_decay_logit, final_g]
    small_g = [grad_c_ctx, grad_ada_b, grad_norm_g, grad_na_rpb, grad_decay, grad_final_g]
    small_m = [m_c_ctx, m_ada_b, m_norm_g, m_na_rpb, m_ret_decay_logit, m_final_g]
    small_v = [v_c_ctx, v_ada_b, v_norm_g, v_na_rpb, v_ret_decay_logit, v_final_g]
    shp = [a.shape for a in small_w]
    ds_, nms_, nvs_ = _adamw_dense(_pack(small_w), _pack(small_g), _pack(small_m), _pack(small_v), "adamw_small")
    ds_, nms_, nvs_ = _unpack(ds_, shp), _unpack(nms_, shp), _unpack(nvs_, shp)

    d_ada, nm_ada, nv_ada = [a.reshape(ada_w.shape) for a in _adamw_dense(
        flat2(ada_w), flat2(grad_ada_w), flat2(m_ada_w), flat2(v_ada_w), "adamw_ada_w")]

    w_all = (w_in, w_proj_na, w_proj_ret, w_out)
    m_all = (m_w_in, m_w_proj_na, m_w_proj_ret, m_w_out)
    v_all = (v_w_in, v_w_proj_na, v_w_proj_ret, v_w_out)
    upd = [None] * 4
    after = d_ada
    for flight in scatters:
        mine, slabs = _push_wait(flight["name"], flight["sems"], flight["srcs"], flight["lands"], flight["axes"],
                                 flight["views"], flight["bits"], (after,))
        for (l, a), own, s in zip(flight["keys"], mine, slabs):
            upd[a] = _adamw_sharded(w_all[a], m_all[a], v_all[a], own, s, None if flight["by_chip"] else w_axes[a],
                                    idx_arr, l, upd[a], "adamw_" + w_names[a])
            after = upd[a][1]
    (g_w_in, d_w_in, nm_w_in, nv_w_in), (g_pna, d_pna, nm_pna, nv_pna) = upd[0], upd[1]
    (g_pret, d_pret, nm_pret, nv_pret), (g_out, d_out, nm_out, nv_out) = upd[2], upd[3]

    def order(cc, aw, ab, ng, wi, rp, dl, pn, pr, wo, fg):
        return [cc, aw, ab, ng, wi, rp, dl, pn, pr, wo, fg]

    grads_out = order(grad_c_ctx, grad_ada_w, grad_ada_b, grad_norm_g, g_w_in, grad_na_rpb, grad_decay, g_pna, g_pret, g_out, grad_final_g)
    delta_out = order(ds_[0], d_ada, ds_[1], ds_[2], d_w_in, ds_[3], ds_[4], d_pna, d_pret, d_out, ds_[5])
    m_out = order(nms_[0], nm_ada, nms_[1], nms_[2], nm_w_in, nms_[3], nms_[4], nm_pna, nm_pret, nm_out, nms_[5])
    v_out = order(nvs_[0], nv_ada, nvs_[1], nvs_[2], nv_w_in, nvs_[3], nvs_[4], nv_pna, nv_pret, nv_out, nvs_[5])
    return (loss, gx[None], *grads_out, *delta_out, *m_out, *v_out)
```
